```python
import jax, jax.numpy as jnp
from jax import lax
import numpy as np

D_MODEL = 1024
BATCH = 8
SEQ = 4096
DEPTH = 4

CHUNK = 64
PLE_DIM = 256
N_EVEN = (DEPTH + 1) // 2
N_ODD = DEPTH // 2
D_A = D_MODEL
CONV_A = 31
D_B = D_MODEL
HEAD_DIM = 64
H_B = D_B // HEAD_DIM
N_GROUPS = 4
N_STATE = 128
CONV_B = 4
XBC_DIM = D_B + 2 * N_GROUPS * N_STATE
E_IN = 2 * D_A + D_B + XBC_DIM + H_B
D_C = D_MODEL
CONV_C = 3
D_FF = 2816
CONV_F = 3
LN_EPS = 1e-5

kernel_name = "hybrid_conformer_ssd_shortconv_trunk"


def _layer_norm(x, g, b):
    xf = x.astype(jnp.float32)
    mu = jnp.mean(xf, axis=-1, keepdims=True)
    var = jnp.mean(jnp.square(xf - mu), axis=-1, keepdims=True)
    return ((xf - mu) * lax.rsqrt(var + LN_EPS) * g + b).astype(x.dtype)


def _rms_norm(x, g):
    xf = x.astype(jnp.float32)
    return (xf * lax.rsqrt(jnp.mean(jnp.square(xf), axis=-1, keepdims=True) + LN_EPS) * g).astype(x.dtype)


def _dwconv_causal(x, w, b=None):
    k, c = w.shape
    y = lax.conv_general_dilated(x, w[:, None, :].astype(x.dtype), window_strides=(1,),
                                 padding=[(k - 1, 0)], dimension_numbers=("NWC", "WIO", "NWC"),
                                 feature_group_count=c)
    if b is not None:
        y = y + b
    return y


def _conformer_conv(u, conv_w, conv_b, ln_g, ln_b):
    a = u[..., :D_A] * jax.nn.sigmoid(u[..., D_A:])
    a = _dwconv_causal(a, conv_w, conv_b)
    return jax.nn.silu(_layer_norm(a, ln_g, ln_b))


def _ssd(xh, dt, a, bm, cm):
    bsz, l, h, p = xh.shape
    g, n = bm.shape[2], bm.shape[3]
    hg = h // g
    c = l // CHUNK
    x = xh.reshape(bsz, c, CHUNK, g, hg, p)
    dtc = dt.reshape(bsz, c, CHUNK, g, hg)
    bc = bm.reshape(bsz, c, CHUNK, g, n)
    cc = cm.reshape(bsz, c, CHUNK, g, n)
    cum = jnp.cumsum(dtc * a.reshape(g, hg), axis=2)
    mask = jnp.tril(jnp.ones((CHUNK, CHUNK), dtype=bool))[:, :, None, None]
    seg = cum[:, :, :, None] - cum[:, :, None, :]
    decay = jnp.where(mask, jnp.exp(jnp.where(mask, seg, 0.0)), 0.0)
    cb = jnp.einsum("bclgn,bcsgn->bclsg", cc, bc)
    w = cb[..., None] * decay * dtc[:, :, None]
    y_diag = jnp.einsum("bclsgh,bcsghp->bclghp", w, x)
    decay_states = jnp.exp(cum[:, :, -1:] - cum) * dtc
    states = jnp.einsum("bclgn,bclgh,bclghp->bcghpn", bc, decay_states, x)
    chunk_decay = jnp.exp(cum[:, :, -1])

    def step(hstate, inp):
        dec, st = inp
        return hstate * dec[..., None, None] + st, hstate

    h0 = jnp.zeros((bsz, g, hg, p, n), dtype=states.dtype)
    _, prev = lax.scan(step, h0, (jnp.moveaxis(chunk_decay, 1, 0), jnp.moveaxis(states, 1, 0)))
    prev = jnp.moveaxis(prev, 0, 1)
    y_off = jnp.einsum("bclgn,bcghpn,bclgh->bclghp", cc, prev, jnp.exp(cum))
    return (y_diag + y_off).reshape(bsz, l, h, p).astype(xh.dtype)


def _mamba2(z, xbc, dt_raw, conv_w, conv_b, dt_bias, a_log, d_skip, norm_g):
    bsz, l, _ = z.shape
    xbc = jax.nn.silu(_dwconv_causal(xbc, conv_w, conv_b))
    gn = N_GROUPS * N_STATE
    xs = xbc[..., :D_B].reshape(bsz, l, H_B, HEAD_DIM)
    bm = xbc[..., D_B:D_B + gn].reshape(bsz, l, N_GROUPS, N_STATE)
    cm = xbc[..., D_B + gn:].reshape(bsz, l, N_GROUPS, N_STATE)
    dt = jax.nn.softplus(dt_raw.astype(jnp.float32) + dt_bias)
    a = -jnp.exp(a_log.astype(jnp.float32))
    y = _ssd(xs, dt, a, bm, cm) + d_skip[:, None] * xs
    y = y.reshape(bsz, l, D_B) * jax.nn.silu(z)
    return _rms_norm(y, norm_g)


def _short_conv(x, w_in, conv_w, w_out):
    u = x @ w_in
    bg, cg, v = u[..., :D_C], u[..., D_C:2 * D_C], u[..., 2 * D_C:]
    return (bg * _dwconv_causal(cg * v, conv_w)) @ w_out


def _conv_ffn(x, w_up, conv_w, conv_b, w_down):
    h = _dwconv_causal(x @ w_up, conv_w, conv_b)
    return (jax.nn.silu(h[..., :D_FF]) * h[..., D_FF:]) @ w_down


def _fwd_setup_inputs(seed: int = 0) -> dict:
    key = jax.random.key(seed)
    ks = jax.random.split(key, 32)
    beta = (8.0 * DEPTH) ** -0.25
    nrm = jax.random.normal

    def dense(k, shape, fan_in, scale=1.0):
        return nrm(k, shape, jnp.float32) * (fan_in ** -0.5) * scale

    dt0 = jnp.exp(jax.random.uniform(ks[8], (N_EVEN, H_B), jnp.float32) * (np.log(0.1) - np.log(0.001)) + np.log(0.001))
    return {
        "x": nrm(ks[0], (BATCH, SEQ, D_MODEL), jnp.float32),
        "p": nrm(ks[1], (DEPTH, BATCH, SEQ, PLE_DIM), jnp.float32),
        "e_w_in": dense(ks[2], (N_EVEN, D_MODEL, E_IN), D_MODEL),
        "e_conv_a_w": dense(ks[3], (N_EVEN, CONV_A, D_A), CONV_A),
        "e_conv_a_b": 0.02 * nrm(ks[4], (N_EVEN, D_A), jnp.float32),
        "e_ln_a_g": 1.0 + 0.02 * nrm(ks[5], (N_EVEN, D_A), jnp.float32),
        "e_ln_a_b": 0.02 * nrm(ks[6], (N_EVEN, D_A), jnp.float32),
        "e_conv_b_w": dense(ks[7], (N_EVEN, CONV_B, XBC_DIM), CONV_B),
        "e_conv_b_b": 0.02 * nrm(ks[9], (N_EVEN, XBC_DIM), jnp.float32),
        "e_dt_bias": dt0 + jnp.log(-jnp.expm1(-dt0)),
        "e_a_log": jnp.log(jax.random.uniform(ks[10], (N_EVEN, H_B), jnp.float32, 1.0, 16.0)),
        "e_d_skip": 1.0 + 0.1 * nrm(ks[11], (N_EVEN, H_B), jnp.float32),
        "e_norm_b_g": 1.0 + 0.02 * nrm(ks[12], (N_EVEN, D_B), jnp.float32),
        "e_w_out": dense(ks[13], (N_EVEN, D_A + D_B, D_MODEL), D_A + D_B, beta),
        "o_w_in": dense(ks[14], (N_ODD, D_MODEL, 3 * D_C), D_MODEL),
        "o_conv_w": dense(ks[15], (N_ODD, CONV_C, D_C), CONV_C),
        "o_w_out": dense(ks[16], (N_ODD, D_C, D_MODEL), D_C, beta),
        "f_w_up": dense(ks[17], (DEPTH, D_MODEL, 2 * D_FF), D_MODEL),
        "f_conv_w": dense(ks[18], (DEPTH, CONV_F, 2 * D_FF), CONV_F),
        "f_conv_b": 0.02 * nrm(ks[19], (DEPTH, 2 * D_FF), jnp.float32),
        "f_w_down": dense(ks[20], (DEPTH, D_FF, D_MODEL), D_FF, beta),
        "ple_w_proj": dense(ks[21], (DEPTH, PLE_DIM, D_MODEL), PLE_DIM, beta),
        "ple_w_gate": dense(ks[22], (DEPTH, D_MODEL, D_MODEL), D_MODEL),
        "ln_g": 1.0 + 0.02 * nrm(ks[23], (DEPTH, 2, D_MODEL), jnp.float32),
        "ln_b": 0.02 * nrm(ks[24], (DEPTH, 2, D_MODEL), jnp.float32),
    }


def _fwd_reference(x, p, e_w_in, e_conv_a_w, e_conv_a_b, e_ln_a_g, e_ln_a_b, e_conv_b_w, e_conv_b_b,
              e_dt_bias, e_a_log, e_d_skip, e_norm_b_g, e_w_out, o_w_in, o_conv_w, o_w_out,
              f_w_up, f_conv_w, f_conv_b, f_w_down, ple_w_proj, ple_w_gate, ln_g, ln_b):
    alpha = (2.0 * DEPTH) ** 0.25
    o_a = 2 * D_A
    o_x = o_a + D_B
    o_dt = o_x + XBC_DIM
    for i in range(DEPTH):
        j = i // 2
        if i % 2 == 0:
            u = x @ e_w_in[j]
            ya = _conformer_conv(u[..., :o_a], e_conv_a_w[j], e_conv_a_b[j], e_ln_a_g[j], e_ln_a_b[j])
            yb = _mamba2(u[..., o_a:o_x], u[..., o_x:o_dt], u[..., o_dt:], e_conv_b_w[j], e_conv_b_b[j],
                         e_dt_bias[j], e_a_log[j], e_d_skip[j], e_norm_b_g[j])
            mix = jnp.concatenate([ya, yb], axis=-1) @ e_w_out[j]
        else:
            mix = _short_conv(x, o_w_in[j], o_conv_w[j], o_w_out[j])
        x = _layer_norm(alpha * x + mix, ln_g[i, 0], ln_b[i, 0])
        ffn = _conv_ffn(x, f_w_up[i], f_conv_w[i], f_conv_b[i], f_w_down[i])
        ple = (p[i] @ ple_w_proj[i]) * jax.nn.sigmoid(x @ ple_w_gate[i])
        x = _layer_norm(alpha * x + ffn + ple, ln_g[i, 1], ln_b[i, 1])
    return x


import jax as _jax
import jax.numpy as _jnp

TWIN_FORMAT = 'train_step'
FWD_PARAMS = ['x', 'p', 'e_w_in', 'e_conv_a_w', 'e_conv_a_b', 'e_ln_a_g', 'e_ln_a_b', 'e_conv_b_w', 'e_conv_b_b', 'e_dt_bias', 'e_a_log', 'e_d_skip', 'e_norm_b_g', 'e_w_out', 'o_w_in', 'o_conv_w', 'o_w_out', 'f_w_up', 'f_conv_w', 'f_conv_b', 'f_w_down', 'ple_w_proj', 'ple_w_gate', 'ln_g', 'ln_b']
TWIN_WEIGHTS = ['e_w_in', 'e_conv_a_w', 'e_conv_a_b', 'e_ln_a_g', 'e_ln_a_b', 'e_conv_b_w', 'e_conv_b_b', 'e_dt_bias', 'e_a_log', 'e_d_skip', 'e_norm_b_g', 'e_w_out', 'o_w_in', 'o_conv_w', 'o_w_out', 'f_w_up', 'f_conv_w', 'f_conv_b', 'f_w_down', 'ple_w_proj', 'ple_w_gate', 'ln_g', 'ln_b']
TWIN_DIFF_INPUT = 'x'
TWIN_INPUTS = ['x', 'p', 'e_w_in', 'e_conv_a_w', 'e_conv_a_b', 'e_ln_a_g', 'e_ln_a_b', 'e_conv_b_w', 'e_conv_b_b', 'e_dt_bias', 'e_a_log', 'e_d_skip', 'e_norm_b_g', 'e_w_out', 'o_w_in', 'o_conv_w', 'o_w_out', 'f_w_up', 'f_conv_w', 'f_conv_b', 'f_w_down', 'ple_w_proj', 'ple_w_gate', 'ln_g', 'ln_b', 'loss_target', 'm_e_w_in', 'm_e_conv_a_w', 'm_e_conv_a_b', 'm_e_ln_a_g', 'm_e_ln_a_b', 'm_e_conv_b_w', 'm_e_conv_b_b', 'm_e_dt_bias', 'm_e_a_log', 'm_e_d_skip', 'm_e_norm_b_g', 'm_e_w_out', 'm_o_w_in', 'm_o_conv_w', 'm_o_w_out', 'm_f_w_up', 'm_f_conv_w', 'm_f_conv_b', 'm_f_w_down', 'm_ple_w_proj', 'm_ple_w_gate', 'm_ln_g', 'm_ln_b', 'v_e_w_in', 'v_e_conv_a_w', 'v_e_conv_a_b', 'v_e_ln_a_g', 'v_e_ln_a_b', 'v_e_conv_b_w', 'v_e_conv_b_b', 'v_e_dt_bias', 'v_e_a_log', 'v_e_d_skip', 'v_e_norm_b_g', 'v_e_w_out', 'v_o_w_in', 'v_o_conv_w', 'v_o_w_out', 'v_f_w_up', 'v_f_conv_w', 'v_f_conv_b', 'v_f_w_down', 'v_ple_w_proj', 'v_ple_w_gate', 'v_ln_g', 'v_ln_b']
TWIN_OUTPUTS = ['loss', 'grad_x', 'grad_e_w_in', 'grad_e_conv_a_w', 'grad_e_conv_a_b', 'grad_e_ln_a_g', 'grad_e_ln_a_b', 'grad_e_conv_b_w', 'grad_e_conv_b_b', 'grad_e_dt_bias', 'grad_e_a_log', 'grad_e_d_skip', 'grad_e_norm_b_g', 'grad_e_w_out', 'grad_o_w_in', 'grad_o_conv_w', 'grad_o_w_out', 'grad_f_w_up', 'grad_f_conv_w', 'grad_f_conv_b', 'grad_f_w_down', 'grad_ple_w_proj', 'grad_ple_w_gate', 'grad_ln_g', 'grad_ln_b', 'delta_e_w_in', 'delta_e_conv_a_w', 'delta_e_conv_a_b', 'delta_e_ln_a_g', 'delta_e_ln_a_b', 'delta_e_conv_b_w', 'delta_e_conv_b_b', 'delta_e_dt_bias', 'delta_e_a_log', 'delta_e_d_skip', 'delta_e_norm_b_g', 'delta_e_w_out', 'delta_o_w_in', 'delta_o_conv_w', 'delta_o_w_out', 'delta_f_w_up', 'delta_f_conv_w', 'delta_f_conv_b', 'delta_f_w_down', 'delta_ple_w_proj', 'delta_ple_w_gate', 'delta_ln_g', 'delta_ln_b', 'new_m_e_w_in', 'new_m_e_conv_a_w', 'new_m_e_conv_a_b', 'new_m_e_ln_a_g', 'new_m_e_ln_a_b', 'new_m_e_conv_b_w', 'new_m_e_conv_b_b', 'new_m_e_dt_bias', 'new_m_e_a_log', 'new_m_e_d_skip', 'new_m_e_norm_b_g', 'new_m_e_w_out', 'new_m_o_w_in', 'new_m_o_conv_w', 'new_m_o_w_out', 'new_m_f_w_up', 'new_m_f_conv_w', 'new_m_f_conv_b', 'new_m_f_w_down', 'new_m_ple_w_proj', 'new_m_ple_w_gate', 'new_m_ln_g', 'new_m_ln_b', 'new_v_e_w_in', 'new_v_e_conv_a_w', 'new_v_e_conv_a_b', 'new_v_e_ln_a_g', 'new_v_e_ln_a_b', 'new_v_e_conv_b_w', 'new_v_e_conv_b_b', 'new_v_e_dt_bias', 'new_v_e_a_log', 'new_v_e_d_skip', 'new_v_e_norm_b_g', 'new_v_e_w_out', 'new_v_o_w_in', 'new_v_o_conv_w', 'new_v_o_w_out', 'new_v_f_w_up', 'new_v_f_conv_w', 'new_v_f_conv_b', 'new_v_f_w_down', 'new_v_ple_w_proj', 'new_v_ple_w_gate', 'new_v_ln_g', 'new_v_ln_b']
TWIN_LEAF_KINDS = {'loss': 'loss', 'grad_x': 'grad_x', 'grad_e_w_in': 'grad_w', 'grad_e_conv_a_w': 'grad_w', 'grad_e_conv_a_b': 'grad_w', 'grad_e_ln_a_g': 'grad_w', 'grad_e_ln_a_b': 'grad_w', 'grad_e_conv_b_w': 'grad_w', 'grad_e_conv_b_b': 'grad_w', 'grad_e_dt_bias': 'grad_w', 'grad_e_a_log': 'grad_w', 'grad_e_d_skip': 'grad_w', 'grad_e_norm_b_g': 'grad_w', 'grad_e_w_out': 'grad_w', 'grad_o_w_in': 'grad_w', 'grad_o_conv_w': 'grad_w', 'grad_o_w_out': 'grad_w', 'grad_f_w_up': 'grad_w', 'grad_f_conv_w': 'grad_w', 'grad_f_conv_b': 'grad_w', 'grad_f_w_down': 'grad_w', 'grad_ple_w_proj': 'grad_w', 'grad_ple_w_gate': 'grad_w', 'grad_ln_g': 'grad_w', 'grad_ln_b': 'grad_w', 'delta_e_w_in': 'delta_w', 'delta_e_conv_a_w': 'delta_w', 'delta_e_conv_a_b': 'delta_w', 'delta_e_ln_a_g': 'delta_w', 'delta_e_ln_a_b': 'delta_w', 'delta_e_conv_b_w': 'delta_w', 'delta_e_conv_b_b': 'delta_w', 'delta_e_dt_bias': 'delta_w', 'delta_e_a_log': 'delta_w', 'delta_e_d_skip': 'delta_w', 'delta_e_norm_b_g': 'delta_w', 'delta_e_w_out': 'delta_w', 'delta_o_w_in': 'delta_w', 'delta_o_conv_w': 'delta_w', 'delta_o_w_out': 'delta_w', 'delta_f_w_up': 'delta_w', 'delta_f_conv_w': 'delta_w', 'delta_f_conv_b': 'delta_w', 'delta_f_w_down': 'delta_w', 'delta_ple_w_proj': 'delta_w', 'delta_ple_w_gate': 'delta_w', 'delta_ln_g': 'delta_w', 'delta_ln_b': 'delta_w', 'new_m_e_w_in': 'new_m', 'new_m_e_conv_a_w': 'new_m', 'new_m_e_conv_a_b': 'new_m', 'new_m_e_ln_a_g': 'new_m', 'new_m_e_ln_a_b': 'new_m', 'new_m_e_conv_b_w': 'new_m', 'new_m_e_conv_b_b': 'new_m', 'new_m_e_dt_bias': 'new_m', 'new_m_e_a_log': 'new_m', 'new_m_e_d_skip': 'new_m', 'new_m_e_norm_b_g': 'new_m', 'new_m_e_w_out': 'new_m', 'new_m_o_w_in': 'new_m', 'new_m_o_conv_w': 'new_m', 'new_m_o_w_out': 'new_m', 'new_m_f_w_up': 'new_m', 'new_m_f_conv_w': 'new_m', 'new_m_f_conv_b': 'new_m', 'new_m_f_w_down': 'new_m', 'new_m_ple_w_proj': 'new_m', 'new_m_ple_w_gate': 'new_m', 'new_m_ln_g': 'new_m', 'new_m_ln_b': 'new_m', 'new_v_e_w_in': 'new_v', 'new_v_e_conv_a_w': 'new_v', 'new_v_e_conv_a_b': 'new_v', 'new_v_e_ln_a_g': 'new_v', 'new_v_e_ln_a_b': 'new_v', 'new_v_e_conv_b_w': 'new_v', 'new_v_e_conv_b_b': 'new_v', 'new_v_e_dt_bias': 'new_v', 'new_v_e_a_log': 'new_v', 'new_v_e_d_skip': 'new_v', 'new_v_e_norm_b_g': 'new_v', 'new_v_e_w_out': 'new_v', 'new_v_o_w_in': 'new_v', 'new_v_o_conv_w': 'new_v', 'new_v_o_w_out': 'new_v', 'new_v_f_w_up': 'new_v', 'new_v_f_conv_w': 'new_v', 'new_v_f_conv_b': 'new_v', 'new_v_f_w_down': 'new_v', 'new_v_ple_w_proj': 'new_v', 'new_v_ple_w_gate': 'new_v', 'new_v_ln_g': 'new_v', 'new_v_ln_b': 'new_v'}


def _forward(args):
    return _fwd_reference(*[args[k] for k in FWD_PARAMS])


def _output_shape():
    out = _jax.eval_shape(lambda: _forward(_fwd_setup_inputs(0)))
    return out.shape, out.dtype

N_MICROBATCH = 1
ADAM_LR = 0.001
ADAM_B1 = 0.9
ADAM_B2 = 0.999
ADAM_EPS = 1e-08
ADAM_WD = 0.01
ADAM_STEP = 10
PER_EXAMPLE_BATCH_AXIS = {'x': 0, 'p': 1, 'loss_target': 0}
SHARED_INPUTS = []
_WEIGHT_DTYPES = {'e_w_in': _jnp.float32, 'e_conv_a_w': _jnp.float32, 'e_conv_a_b': _jnp.float32, 'e_ln_a_g': _jnp.float32, 'e_ln_a_b': _jnp.float32, 'e_conv_b_w': _jnp.float32, 'e_conv_b_b': _jnp.float32, 'e_dt_bias': _jnp.float32, 'e_a_log': _jnp.float32, 'e_d_skip': _jnp.float32, 'e_norm_b_g': _jnp.float32, 'e_w_out': _jnp.float32, 'o_w_in': _jnp.float32, 'o_conv_w': _jnp.float32, 'o_w_out': _jnp.float32, 'f_w_up': _jnp.float32, 'f_conv_w': _jnp.float32, 'f_conv_b': _jnp.float32, 'f_w_down': _jnp.float32, 'ple_w_proj': _jnp.float32, 'ple_w_gate': _jnp.float32, 'ln_g': _jnp.float32, 'ln_b': _jnp.float32}
MOMENT_SCALE = {'e_w_in': 2.476621e-02, 'e_conv_a_w': 2.165330e-02, 'e_conv_a_b': 6.252988e-02, 'e_ln_a_g': 3.341452e-02, 'e_ln_a_b': 3.852392e-02, 'e_conv_b_w': 2.544408e-02, 'e_conv_b_b': 4.222575e-02, 'e_dt_bias': 1.245367e-01, 'e_a_log': 9.513363e-02, 'e_d_skip': 2.518149e-01, 'e_norm_b_g': 3.621487e-02, 'e_w_out': 1.020033e-01, 'o_w_in': 4.539138e-02, 'o_conv_w': 4.643284e-02, 'o_w_out': 1.079317e-01, 'f_w_up': 1.755800e-02, 'f_conv_w': 1.739983e-02, 'f_conv_b': 1.910858e-02, 'f_w_down': 6.815207e-02, 'ple_w_proj': 6.019391e-02, 'ple_w_gate': 9.875164e-03, 'ln_g': 1.134827e+01, 'ln_b': 8.843450e-01}


def _to_microbatches(a, axis):
    t = _jnp.moveaxis(a, axis, 0)
    t = t.reshape((N_MICROBATCH, t.shape[0] // N_MICROBATCH) + t.shape[1:])
    return _jnp.moveaxis(t, 1, axis + 1)


def setup_inputs(seed: int = 0) -> dict:
    inp = _fwd_setup_inputs(seed)
    key = _jax.random.fold_in(_jax.random.key(seed), 7919)
    shape, _ = _output_shape()
    out = dict(inp)
    out["loss_target"] = _jax.random.normal(_jax.random.fold_in(key, 0), shape, _jnp.float32)
    for i, name in enumerate(TWIN_WEIGHTS):
        w = inp[name].astype(_jnp.float32)
        if MOMENT_SCALE is None:
            s = _jnp.sqrt(_jnp.mean(_jnp.square(w)) + 1e-30)
        else:
            s = MOMENT_SCALE[name]
        km, kv = _jax.random.split(_jax.random.fold_in(key, i + 1))
        out[name] = w
        out["m_" + name] = s * _jax.random.normal(km, w.shape, _jnp.float32)
        out["v_" + name] = (s * s) * _jax.random.uniform(kv, w.shape, _jnp.float32, 0.5, 1.5)
    if N_MICROBATCH > 1:
        for name, axis in PER_EXAMPLE_BATCH_AXIS.items():
            out[name] = _to_microbatches(out[name], axis)
    return {'x': out['x'], 'p': out['p'], 'e_w_in': out['e_w_in'], 'e_conv_a_w': out['e_conv_a_w'], 'e_conv_a_b': out['e_conv_a_b'], 'e_ln_a_g': out['e_ln_a_g'], 'e_ln_a_b': out['e_ln_a_b'], 'e_conv_b_w': out['e_conv_b_w'], 'e_conv_b_b': out['e_conv_b_b'], 'e_dt_bias': out['e_dt_bias'], 'e_a_log': out['e_a_log'], 'e_d_skip': out['e_d_skip'], 'e_norm_b_g': out['e_norm_b_g'], 'e_w_out': out['e_w_out'], 'o_w_in': out['o_w_in'], 'o_conv_w': out['o_conv_w'], 'o_w_out': out['o_w_out'], 'f_w_up': out['f_w_up'], 'f_conv_w': out['f_conv_w'], 'f_conv_b': out['f_conv_b'], 'f_w_down': out['f_w_down'], 'ple_w_proj': out['ple_w_proj'], 'ple_w_gate': out['ple_w_gate'], 'ln_g': out['ln_g'], 'ln_b': out['ln_b'], 'loss_target': out['loss_target'], 'm_e_w_in': out['m_e_w_in'], 'm_e_conv_a_w': out['m_e_conv_a_w'], 'm_e_conv_a_b': out['m_e_conv_a_b'], 'm_e_ln_a_g': out['m_e_ln_a_g'], 'm_e_ln_a_b': out['m_e_ln_a_b'], 'm_e_conv_b_w': out['m_e_conv_b_w'], 'm_e_conv_b_b': out['m_e_conv_b_b'], 'm_e_dt_bias': out['m_e_dt_bias'], 'm_e_a_log': out['m_e_a_log'], 'm_e_d_skip': out['m_e_d_skip'], 'm_e_norm_b_g': out['m_e_norm_b_g'], 'm_e_w_out': out['m_e_w_out'], 'm_o_w_in': out['m_o_w_in'], 'm_o_conv_w': out['m_o_conv_w'], 'm_o_w_out': out['m_o_w_out'], 'm_f_w_up': out['m_f_w_up'], 'm_f_conv_w': out['m_f_conv_w'], 'm_f_conv_b': out['m_f_conv_b'], 'm_f_w_down': out['m_f_w_down'], 'm_ple_w_proj': out['m_ple_w_proj'], 'm_ple_w_gate': out['m_ple_w_gate'], 'm_ln_g': out['m_ln_g'], 'm_ln_b': out['m_ln_b'], 'v_e_w_in': out['v_e_w_in'], 'v_e_conv_a_w': out['v_e_conv_a_w'], 'v_e_conv_a_b': out['v_e_conv_a_b'], 'v_e_ln_a_g': out['v_e_ln_a_g'], 'v_e_ln_a_b': out['v_e_ln_a_b'], 'v_e_conv_b_w': out['v_e_conv_b_w'], 'v_e_conv_b_b': out['v_e_conv_b_b'], 'v_e_dt_bias': out['v_e_dt_bias'], 'v_e_a_log': out['v_e_a_log'], 'v_e_d_skip': out['v_e_d_skip'], 'v_e_norm_b_g': out['v_e_norm_b_g'], 'v_e_w_out': out['v_e_w_out'], 'v_o_w_in': out['v_o_w_in'], 'v_o_conv_w': out['v_o_conv_w'], 'v_o_w_out': out['v_o_w_out'], 'v_f_w_up': out['v_f_w_up'], 'v_f_conv_w': out['v_f_conv_w'], 'v_f_conv_b': out['v_f_conv_b'], 'v_f_w_down': out['v_f_w_down'], 'v_ple_w_proj': out['v_ple_w_proj'], 'v_ple_w_gate': out['v_ple_w_gate'], 'v_ln_g': out['v_ln_g'], 'v_ln_b': out['v_ln_b']}


def _loss(weights, diff, rest, loss_target):
    with _jax.named_scope("forward"):
        args = {**rest, TWIN_DIFF_INPUT: diff, **{k: w.astype(_WEIGHT_DTYPES[k]) for k, w in weights.items()}}
        y = _forward(args)
    with _jax.named_scope("loss_head"):
        err = _jnp.square(y.astype(_jnp.float32) - loss_target)
        return 0.5 * _jnp.sum(_jnp.mean(err, axis=-1)) if err.ndim else 0.5 * err


def _adamw(w, g, m, v):
    m = ADAM_B1 * m + (1.0 - ADAM_B1) * g
    v = ADAM_B2 * v + (1.0 - ADAM_B2) * _jnp.square(g)
    m_hat = m / (1.0 - ADAM_B1 ** ADAM_STEP)
    v_hat = v / (1.0 - ADAM_B2 ** ADAM_STEP)
    delta = -ADAM_LR * (m_hat / (_jnp.sqrt(v_hat) + ADAM_EPS) + ADAM_WD * w)
    return delta, m, v


def reference(x, p, e_w_in, e_conv_a_w, e_conv_a_b, e_ln_a_g, e_ln_a_b, e_conv_b_w, e_conv_b_b, e_dt_bias, e_a_log, e_d_skip, e_norm_b_g, e_w_out, o_w_in, o_conv_w, o_w_out, f_w_up, f_conv_w, f_conv_b, f_w_down, ple_w_proj, ple_w_gate, ln_g, ln_b, loss_target, m_e_w_in, m_e_conv_a_w, m_e_conv_a_b, m_e_ln_a_g, m_e_ln_a_b, m_e_conv_b_w, m_e_conv_b_b, m_e_dt_bias, m_e_a_log, m_e_d_skip, m_e_norm_b_g, m_e_w_out, m_o_w_in, m_o_conv_w, m_o_w_out, m_f_w_up, m_f_conv_w, m_f_conv_b, m_f_w_down, m_ple_w_proj, m_ple_w_gate, m_ln_g, m_ln_b, v_e_w_in, v_e_conv_a_w, v_e_conv_a_b, v_e_ln_a_g, v_e_ln_a_b, v_e_conv_b_w, v_e_conv_b_b, v_e_dt_bias, v_e_a_log, v_e_d_skip, v_e_norm_b_g, v_e_w_out, v_o_w_in, v_o_conv_w, v_o_w_out, v_f_w_up, v_f_conv_w, v_f_conv_b, v_f_w_down, v_ple_w_proj, v_ple_w_gate, v_ln_g, v_ln_b):
    given = dict(x=x, p=p, e_w_in=e_w_in, e_conv_a_w=e_conv_a_w, e_conv_a_b=e_conv_a_b, e_ln_a_g=e_ln_a_g, e_ln_a_b=e_ln_a_b, e_conv_b_w=e_conv_b_w, e_conv_b_b=e_conv_b_b, e_dt_bias=e_dt_bias, e_a_log=e_a_log, e_d_skip=e_d_skip, e_norm_b_g=e_norm_b_g, e_w_out=e_w_out, o_w_in=o_w_in, o_conv_w=o_conv_w, o_w_out=o_w_out, f_w_up=f_w_up, f_conv_w=f_conv_w, f_conv_b=f_conv_b, f_w_down=f_w_down, ple_w_proj=ple_w_proj, ple_w_gate=ple_w_gate, ln_g=ln_g, ln_b=ln_b, loss_target=loss_target, m_e_w_in=m_e_w_in, m_e_conv_a_w=m_e_conv_a_w, m_e_conv_a_b=m_e_conv_a_b, m_e_ln_a_g=m_e_ln_a_g, m_e_ln_a_b=m_e_ln_a_b, m_e_conv_b_w=m_e_conv_b_w, m_e_conv_b_b=m_e_conv_b_b, m_e_dt_bias=m_e_dt_bias, m_e_a_log=m_e_a_log, m_e_d_skip=m_e_d_skip, m_e_norm_b_g=m_e_norm_b_g, m_e_w_out=m_e_w_out, m_o_w_in=m_o_w_in, m_o_conv_w=m_o_conv_w, m_o_w_out=m_o_w_out, m_f_w_up=m_f_w_up, m_f_conv_w=m_f_conv_w, m_f_conv_b=m_f_conv_b, m_f_w_down=m_f_w_down, m_ple_w_proj=m_ple_w_proj, m_ple_w_gate=m_ple_w_gate, m_ln_g=m_ln_g, m_ln_b=m_ln_b, v_e_w_in=v_e_w_in, v_e_conv_a_w=v_e_conv_a_w, v_e_conv_a_b=v_e_conv_a_b, v_e_ln_a_g=v_e_ln_a_g, v_e_ln_a_b=v_e_ln_a_b, v_e_conv_b_w=v_e_conv_b_w, v_e_conv_b_b=v_e_conv_b_b, v_e_dt_bias=v_e_dt_bias, v_e_a_log=v_e_a_log, v_e_d_skip=v_e_d_skip, v_e_norm_b_g=v_e_norm_b_g, v_e_w_out=v_e_w_out, v_o_w_in=v_o_w_in, v_o_conv_w=v_o_conv_w, v_o_w_out=v_o_w_out, v_f_w_up=v_f_w_up, v_f_conv_w=v_f_conv_w, v_f_conv_b=v_f_conv_b, v_f_w_down=v_f_w_down, v_ple_w_proj=v_ple_w_proj, v_ple_w_gate=v_ple_w_gate, v_ln_g=v_ln_g, v_ln_b=v_ln_b)
    weights = {n: given[n] for n in TWIN_WEIGHTS}
    shared = {n: given[n] for n in SHARED_INPUTS}
    per_example = {n: given[n] for n in ['x', 'p']}
    grad_fn = _jax.value_and_grad(_loss, argnums=(0, 1))

    def one_microbatch(ex, loss_target):
        ex = dict(ex)
        diff = ex.pop(TWIN_DIFF_INPUT)
        return grad_fn(weights, diff, {**shared, **ex}, loss_target)

    if N_MICROBATCH == 1:
        loss, (grad_w, grad_x) = one_microbatch(per_example, given["loss_target"])
    else:
        def body(carry, xs):
            loss_sum, grad_sum = carry
            l_k, (gw_k, gx_k) = one_microbatch(xs[0], xs[1])
            with _jax.named_scope("update"):
                return (loss_sum + l_k, _jax.tree.map(_jnp.add, grad_sum, gw_k)), gx_k

        init = (_jnp.zeros((), _jnp.float32), _jax.tree.map(_jnp.zeros_like, weights))
        (loss, grad_w), grad_x = _jax.lax.scan(body, init, (per_example, given["loss_target"]))
    with _jax.named_scope("update"):
        delta_w, new_m, new_v = {}, {}, {}
        for n in TWIN_WEIGHTS:
            delta_w[n], new_m[n], new_v[n] = _adamw(weights[n], grad_w[n], given["m_" + n], given["v_" + n])
    return (loss, grad_x, *[grad_w[n] for n in TWIN_WEIGHTS], *[delta_w[n] for n in TWIN_WEIGHTS],
            *[new_m[n] for n in TWIN_WEIGHTS], *[new_v[n] for n in TWIN_WEIGHTS])
```

```python
import functools
import math

import jax
import jax.numpy as jnp
from jax import lax
from jax.experimental import pallas as pl
from jax.experimental.pallas import tpu as pltpu

F32 = jnp.float32
BF16 = jnp.bfloat16
MESH = pl.DeviceIdType.MESH

DEPTH = 4
ALPHA = (2.0 * DEPTH) ** 0.25
LN_EPS = 1e-5
D = 1024
HEAD_P = 64
N_STATE = 128
N_HEADS = 16
N_GROUPS = 4
CONV_A, CONV_B, CONV_C, CONV_F = 31, 4, 3, 3
D_FF = 2816
PLE = 256
E_IN = 5136

ADAM_LR, ADAM_B1, ADAM_B2, ADAM_EPS, ADAM_WD, ADAM_STEP = 0.001, 0.9, 0.999, 1e-08, 0.01, 10

LANES = 128
SUBLANES = 8
VMEM_LIMIT = 56 * 1024 * 1024
SSD_Q = 128
CONV_R = 128
CONV_PAD = 32
ROW_T = 256
PACK_W = 1024
PACK_ROWS = 16384
HI = lax.Precision.HIGHEST


def _cparams(sem=None):
    return pltpu.CompilerParams(dimension_semantics=sem, vmem_limit_bytes=VMEM_LIMIT)


def _pick(n, cands):
    for c in cands:
        if n % c == 0:
            return c
    return n


def _sig(v):
    return jax.nn.sigmoid(v)


_DIMS = {"nn": (((1,), (0,)), ((), ())), "nt": (((1,), (1,)), ((), ())), "tn": (((0,), (0,)), ((), ()))}


def _mm(a, b, mode, name, out_dtype=F32, add=None, add_scale=1.0):
    if mode == "nn":
        (M, K), (K2, N) = a.shape, b.shape
    elif mode == "nt":
        (M, K), (N, K2) = a.shape, b.shape
    else:
        (K, M), (K2, N) = a.shape, b.shape
    assert K == K2, (a.shape, b.shape, mode)
    tm = _pick(M, (512, 256, 128))
    tn = _pick(N, (1024, 1408, 512, 256, 128))
    tk = _pick(K, (512, 256, 128))
    nk = K // tk
    has_add = add is not None

    def body(*refs):
        if has_add:
            a_ref, b_ref, add_ref, o_ref, acc_ref = refs
        else:
            a_ref, b_ref, o_ref, acc_ref = refs
        k = pl.program_id(2)

        @pl.when(k == 0)
        def _():
            acc_ref[...] = jnp.zeros_like(acc_ref)

        acc_ref[...] += lax.dot_general(a_ref[...].astype(BF16), b_ref[...].astype(BF16), _DIMS[mode],
                                        preferred_element_type=F32)

        @pl.when(k == nk - 1)
        def _():
            r = acc_ref[...]
            if has_add:
                r = r + add_scale * add_ref[...].astype(F32)
            o_ref[...] = r.astype(out_dtype)

    if mode == "tn":
        a_spec = pl.BlockSpec((tk, tm), lambda i, j, k: (k, i))
    else:
        a_spec = pl.BlockSpec((tm, tk), lambda i, j, k: (i, k))
    if mode == "nt":
        b_spec = pl.BlockSpec((tn, tk), lambda i, j, k: (j, k))
    else:
        b_spec = pl.BlockSpec((tk, tn), lambda i, j, k: (k, j))
    o_spec = pl.BlockSpec((tm, tn), lambda i, j, k: (i, j))
    in_specs = [a_spec, b_spec] + ([o_spec] if has_add else [])
    args = (a, b) + ((add,) if has_add else ())
    return pl.pallas_call(
        body, name=name, grid=(M // tm, N // tn, nk), in_specs=in_specs, out_specs=o_spec,
        out_shape=jax.ShapeDtypeStruct((M, N), out_dtype), scratch_shapes=[pltpu.VMEM((tm, tn), F32)],
        compiler_params=_cparams(("parallel", "parallel", "arbitrary")))(*args)


def _rows(T, width=D):
    return pl.BlockSpec((ROW_T, width), lambda i: (i, 0))


def _vec(width=D):
    return pl.BlockSpec((1, width), lambda i: (0, 0))


def _ln_stats(h):
    mu = jnp.mean(h, axis=-1, keepdims=True)
    hc = h - mu
    var = jnp.mean(hc * hc, axis=-1, keepdims=True)
    rstd = lax.rsqrt(var + LN_EPS)
    return hc * rstd, rstd


def _res_ln_fwd(x, adds, ple, g, b, name):
    T = x.shape[0]
    n_add = len(adds)
    has_ple = ple is not None

    def body(*refs):
        x_ref = refs[0]
        add_refs = refs[1:1 + n_add]
        pos = 1 + n_add
        if has_ple:
            pp_ref, gl_ref = refs[pos], refs[pos + 1]
            pos += 2
        g_ref, b_ref, h_ref, y_ref, yb_ref = refs[pos:pos + 5]
        h = ALPHA * x_ref[...]
        for r in add_refs:
            h = h + r[...]
        if has_ple:
            h = h + pp_ref[...] * _sig(gl_ref[...])
        xhat, _ = _ln_stats(h)
        y = xhat * g_ref[...] + b_ref[...]
        h_ref[...] = h
        y_ref[...] = y
        yb_ref[...] = y.astype(BF16)

    n_in = 1 + n_add + (2 if has_ple else 0)
    args = (x,) + tuple(adds) + (tuple(ple) if has_ple else ()) + (g, b)
    return pl.pallas_call(
        body, name=name, grid=(T // ROW_T,), in_specs=[_rows(T)] * n_in + [_vec(), _vec()],
        out_specs=(_rows(T), _rows(T), _rows(T)),
        out_shape=(jax.ShapeDtypeStruct((T, D), F32), jax.ShapeDtypeStruct((T, D), F32), jax.ShapeDtypeStruct((T, D), BF16)),
        compiler_params=_cparams(("parallel",)))(*args)


def _res_ln_bwd(dy, h, g, ple, name):
    T = dy.shape[0]
    has_ple = ple is not None

    def body(*refs):
        if has_ple:
            dy_ref, h_ref, g_ref, pp_ref, gl_ref, dh_ref, dhb_ref, dg_ref, db_ref, dpp_ref, dgl_ref = refs
        else:
            dy_ref, h_ref, g_ref, dh_ref, dhb_ref, dg_ref, db_ref = refs
        i = pl.program_id(0)

        @pl.when(i == 0)
        def _():
            dg_ref[...] = jnp.zeros_like(dg_ref)
            db_ref[...] = jnp.zeros_like(db_ref)

        dyv = dy_ref[...]
        xhat, rstd = _ln_stats(h_ref[...])
        dg_ref[...] += jnp.sum(dyv * xhat, axis=0, keepdims=True)
        db_ref[...] += jnp.sum(dyv, axis=0, keepdims=True)
        dxh = dyv * g_ref[...]
        dh = rstd * (dxh - jnp.mean(dxh, axis=-1, keepdims=True) - xhat * jnp.mean(dxh * xhat, axis=-1, keepdims=True))
        dh_ref[...] = dh
        dhb_ref[...] = dh.astype(BF16)
        if has_ple:
            s = _sig(gl_ref[...])
            dpp_ref[...] = (dh * s).astype(BF16)
            dgl_ref[...] = (dh * pp_ref[...] * s * (1.0 - s)).astype(BF16)

    args = (dy, h, g) + (tuple(ple) if has_ple else ())
    in_specs = [_rows(T), _rows(T), _vec()] + ([_rows(T), _rows(T)] if has_ple else [])
    out_specs = [_rows(T), _rows(T), _vec(), _vec()] + ([_rows(T), _rows(T)] if has_ple else [])
    out_shape = [jax.ShapeDtypeStruct((T, D), F32), jax.ShapeDtypeStruct((T, D), BF16),
                 jax.ShapeDtypeStruct((1, D), F32), jax.ShapeDtypeStruct((1, D), F32)]
    if has_ple:
        out_shape += [jax.ShapeDtypeStruct((T, D), BF16), jax.ShapeDtypeStruct((T, D), BF16)]
    return pl.pallas_call(
        body, name=name, grid=(T // ROW_T,), in_specs=in_specs, out_specs=tuple(out_specs), out_shape=tuple(out_shape),
        compiler_params=_cparams(("arbitrary",)))(*args)


def _ln_silu_fwd(ac, g, b, name):
    T = ac.shape[0]

    def body(a_ref, g_ref, b_ref, o_ref):
        xhat, _ = _ln_stats(a_ref[...])
        ln = xhat * g_ref[...] + b_ref[...]
        o_ref[...] = (ln * _sig(ln)).astype(BF16)

    return pl.pallas_call(
        body, name=name, grid=(T // ROW_T,), in_specs=[_rows(T), _vec(), _vec()], out_specs=_rows(T),
        out_shape=jax.ShapeDtypeStruct((T, D), BF16), compiler_params=_cparams(("parallel",)))(ac, g, b)


def _ln_silu_bwd(ac, dya, g, b, name):
    T = ac.shape[0]

    def body(a_ref, d_ref, g_ref, b_ref, da_ref, dg_ref, db_ref):
        i = pl.program_id(0)

        @pl.when(i == 0)
        def _():
            dg_ref[...] = jnp.zeros_like(dg_ref)
            db_ref[...] = jnp.zeros_like(db_ref)

        xhat, rstd = _ln_stats(a_ref[...])
        ln = xhat * g_ref[...] + b_ref[...]
        s = _sig(ln)
        dln = d_ref[...] * s * (1.0 + ln * (1.0 - s))
        dg_ref[...] += jnp.sum(dln * xhat, axis=0, keepdims=True)
        db_ref[...] += jnp.sum(dln, axis=0, keepdims=True)
        dxh = dln * g_ref[...]
        da_ref[...] = rstd * (dxh - jnp.mean(dxh, axis=-1, keepdims=True)
                              - xhat * jnp.mean(dxh * xhat, axis=-1, keepdims=True))

    return pl.pallas_call(
        body, name=name, grid=(T // ROW_T,), in_specs=[_rows(T), _rows(T), _vec(), _vec()],
        out_specs=(_rows(T), _vec(), _vec()),
        out_shape=(jax.ShapeDtypeStruct((T, D), F32), jax.ShapeDtypeStruct((1, D), F32), jax.ShapeDtypeStruct((1, D), F32)),
        compiler_params=_cparams(("arbitrary",)))(ac, dya, g, b)


def _gate_rms_fwd(y, z, g, name):
    T = y.shape[0]

    def body(y_ref, z_ref, g_ref, o_ref):
        zv = z_ref[...]
        yg = y_ref[...] * (zv * _sig(zv))
        r = lax.rsqrt(jnp.mean(yg * yg, axis=-1, keepdims=True) + LN_EPS)
        o_ref[...] = (yg * r * g_ref[...]).astype(BF16)

    return pl.pallas_call(
        body, name=name, grid=(T // ROW_T,), in_specs=[_rows(T), _rows(T), _vec()], out_specs=_rows(T),
        out_shape=jax.ShapeDtypeStruct((T, D), BF16), compiler_params=_cparams(("parallel",)))(y, z, g)


def _gate_rms_bwd(y, z, dout, g, name):
    T = y.shape[0]

    def body(y_ref, z_ref, d_ref, g_ref, dy_ref, dz_ref, dg_ref):
        i = pl.program_id(0)

        @pl.when(i == 0)
        def _():
            dg_ref[...] = jnp.zeros_like(dg_ref)

        yv, zv, dv = y_ref[...], z_ref[...], d_ref[...]
        s = _sig(zv)
        sz = zv * s
        yg = yv * sz
        r = lax.rsqrt(jnp.mean(yg * yg, axis=-1, keepdims=True) + LN_EPS)
        dg_ref[...] += jnp.sum(dv * yg * r, axis=0, keepdims=True)
        dn = dv * g_ref[...]
        dyg = r * dn - yg * (r * r * r) * jnp.mean(dn * yg, axis=-1, keepdims=True)
        dy_ref[...] = dyg * sz
        dz_ref[...] = dyg * yv * s * (1.0 + zv * (1.0 - s))

    return pl.pallas_call(
        body, name=name, grid=(T // ROW_T,), in_specs=[_rows(T), _rows(T), _rows(T), _vec()],
        out_specs=(_rows(T), _rows(T), _vec()),
        out_shape=(jax.ShapeDtypeStruct((T, D), F32), jax.ShapeDtypeStruct((T, D), F32), jax.ShapeDtypeStruct((1, D), F32)),
        compiler_params=_cparams(("arbitrary",)))(y, z, dout, g)


def _loss_head(y, target, name):
    T = y.shape[0]

    def body(y_ref, t_ref, s_ref, d_ref):
        i = pl.program_id(0)

        @pl.when(i == 0)
        def _():
            s_ref[...] = jnp.zeros_like(s_ref)

        err = y_ref[...] - t_ref[...]
        s_ref[...] += jnp.sum(jnp.sum(err * err, axis=1, keepdims=True), axis=0, keepdims=True)
        d_ref[...] = err * (1.0 / D)

    return pl.pallas_call(
        body, name=name, grid=(T // ROW_T,), in_specs=[_rows(T), _rows(T)],
        out_specs=(pl.BlockSpec((SUBLANES, LANES), lambda i: (0, 0)), _rows(T)),
        out_shape=(jax.ShapeDtypeStruct((SUBLANES, LANES), F32), jax.ShapeDtypeStruct((T, D), F32)),
        compiler_params=_cparams(("arbitrary",)))(y, target)


def _taps_fwd(pad_ref, w_ref, K, base):
    off = CONV_PAD - (K - 1)
    acc = w_ref[0:1, :] * pad_ref[pl.ds(base + off, CONV_R), :]
    for k in range(1, K):
        acc = acc + w_ref[k:k + 1, :] * pad_ref[pl.ds(base + off + k, CONV_R), :]
    return acc


def _taps_bwd(padd_ref, w_ref, K, base):
    acc = w_ref[0:1, :] * padd_ref[pl.ds(base + (K - 1), CONV_R), :]
    for k in range(1, K):
        acc = acc + w_ref[k:k + 1, :] * padd_ref[pl.ds(base + (K - 1) - k, CONV_R), :]
    return acc


def _fold8(v):
    return v.reshape(CONV_R // SUBLANES, SUBLANES, v.shape[-1]).sum(0)


def _wgrad_acc(dw_ref, pad_ref, d, K, base):
    off = CONV_PAD - (K - 1)
    for k in range(K):
        dw_ref[k * SUBLANES:(k + 1) * SUBLANES, :] += _fold8(d * pad_ref[pl.ds(base + off + k, CONV_R), :])


def _loop_rows(T, fn):
    def step(r, carry):
        fn(pl.multiple_of(r * CONV_R, CONV_R))
        return carry
    lax.fori_loop(0, T // CONV_R, step, 0)


def _col(T, off_blocks=0, rows=None):
    return pl.BlockSpec((T if rows is None else rows, LANES), lambda j: (0, j + off_blocks))


def _conv_call(body, name, T, n_tiles, in_specs, out_specs, out_shape, n_pad, n_padd=0):
    scratch = [pltpu.VMEM((T + CONV_PAD, LANES), F32)] * (n_pad + n_padd)
    return pl.pallas_call(body, name=name, grid=(n_tiles,), in_specs=in_specs, out_specs=out_specs, out_shape=out_shape,
                          scratch_shapes=scratch, compiler_params=_cparams(("parallel",)))


def _zero_head(ref):
    ref[0:CONV_PAD, :] = jnp.zeros((CONV_PAD, LANES), F32)


def _zero_tail(ref, T):
    ref[T:T + CONV_PAD, :] = jnp.zeros((CONV_PAD, LANES), F32)


def _sds(shape, dtype=F32):
    return jax.ShapeDtypeStruct(shape, dtype)


def _conv_a_fwd(ua, w, b, name):
    T = ua.shape[0]
    K, nt = CONV_A, D // LANES

    def body(al_ref, ag_ref, w_ref, b_ref, o_ref, pad_ref):
        _zero_head(pad_ref)

        def pre(base):
            pad_ref[pl.ds(base + CONV_PAD, CONV_R), :] = al_ref[pl.ds(base, CONV_R), :] * _sig(ag_ref[pl.ds(base, CONV_R), :])
        _loop_rows(T, pre)

        def main(base):
            o_ref[pl.ds(base, CONV_R), :] = _taps_fwd(pad_ref, w_ref, K, base) + b_ref[...]
        _loop_rows(T, main)

    return _conv_call(body, name, T, nt, [_col(T), _col(T, nt), _col(T, rows=K), _col(T, rows=1)], _col(T),
                      _sds((T, D)), 1)(ua, ua, w, b)


def _conv_a_bwd(ua, w, dac, name):
    T = ua.shape[0]
    K, nt = CONV_A, D // LANES

    def body(al_ref, ag_ref, w_ref, d_ref, dal_ref, dag_ref, dw_ref, db_ref, pad_ref, padd_ref):
        _zero_head(pad_ref)
        _zero_tail(padd_ref, T)
        dw_ref[...] = jnp.zeros_like(dw_ref)
        db_ref[...] = jnp.zeros_like(db_ref)

        def pre(base):
            rows = pl.ds(base, CONV_R)
            pad_ref[pl.ds(base + CONV_PAD, CONV_R), :] = al_ref[rows, :] * _sig(ag_ref[rows, :])
            padd_ref[rows, :] = d_ref[rows, :]
        _loop_rows(T, pre)

        def main(base):
            rows = pl.ds(base, CONV_R)
            d = d_ref[rows, :]
            _wgrad_acc(dw_ref, pad_ref, d, K, base)
            db_ref[...] += _fold8(d)
            da = _taps_bwd(padd_ref, w_ref, K, base)
            al, s = al_ref[rows, :], _sig(ag_ref[rows, :])
            dal_ref[rows, :] = da * s
            dag_ref[rows, :] = da * al * s * (1.0 - s)
        _loop_rows(T, main)

    return _conv_call(body, name, T, nt, [_col(T), _col(T, nt), _col(T, rows=K), _col(T)],
                      (_col(T), _col(T), _col(T, rows=K * SUBLANES), _col(T, rows=SUBLANES)),
                      (_sds((T, D)), _sds((T, D)), _sds((K * SUBLANES, D)), _sds((SUBLANES, D))), 1, 1)(ua, ua, w, dac)


def _conv_b_fwd(xu, w, b, name):
    T, C = xu.shape
    K, nt = CONV_B, C // LANES

    def body(x_ref, w_ref, b_ref, o_ref, pad_ref):
        _zero_head(pad_ref)
        pad_ref[CONV_PAD:CONV_PAD + T, :] = x_ref[...]

        def main(base):
            hc = _taps_fwd(pad_ref, w_ref, K, base) + b_ref[...]
            o_ref[pl.ds(base, CONV_R), :] = hc * _sig(hc)
        _loop_rows(T, main)

    return _conv_call(body, name, T, nt, [_col(T), _col(T, rows=K), _col(T, rows=1)], _col(T), _sds((T, C)), 1)(xu, w, b)


def _conv_b_bwd(xu, w, b, dxc, name):
    T, C = xu.shape
    K, nt = CONV_B, C // LANES

    def body(x_ref, w_ref, b_ref, d_ref, dx_ref, dw_ref, db_ref, pad_ref, padd_ref):
        _zero_head(pad_ref)
        _zero_tail(padd_ref, T)
        dw_ref[...] = jnp.zeros_like(dw_ref)
        db_ref[...] = jnp.zeros_like(db_ref)
        pad_ref[CONV_PAD:CONV_PAD + T, :] = x_ref[...]

        def pre(base):
            rows = pl.ds(base, CONV_R)
            hc = _taps_fwd(pad_ref, w_ref, K, base) + b_ref[...]
            s = _sig(hc)
            padd_ref[rows, :] = d_ref[rows, :] * s * (1.0 + hc * (1.0 - s))
        _loop_rows(T, pre)

        def main(base):
            d = padd_ref[pl.ds(base, CONV_R), :]
            _wgrad_acc(dw_ref, pad_ref, d, K, base)
            db_ref[...] += _fold8(d)
            dx_ref[pl.ds(base, CONV_R), :] = _taps_bwd(padd_ref, w_ref, K, base)
        _loop_rows(T, main)

    return _conv_call(body, name, T, nt, [_col(T), _col(T, rows=K), _col(T, rows=1), _col(T)],
                      (_col(T), _col(T, rows=K * SUBLANES), _col(T, rows=SUBLANES)),
                      (_sds((T, C)), _sds((K * SUBLANES, C)), _sds((SUBLANES, C))), 1, 1)(xu, w, b, dxc)


def _conv_c_fwd(uo, w, name):
    T = uo.shape[0]
    K, nt = CONV_C, D // LANES

    def body(bg_ref, cg_ref, v_ref, w_ref, o_ref, pad_ref):
        _zero_head(pad_ref)
        pad_ref[CONV_PAD:CONV_PAD + T, :] = cg_ref[...] * v_ref[...]

        def main(base):
            rows = pl.ds(base, CONV_R)
            o_ref[rows, :] = (bg_ref[rows, :] * _taps_fwd(pad_ref, w_ref, K, base)).astype(BF16)
        _loop_rows(T, main)

    return _conv_call(body, name, T, nt, [_col(T), _col(T, nt), _col(T, 2 * nt), _col(T, rows=K)], _col(T),
                      _sds((T, D), BF16), 1)(uo, uo, uo, w)


def _conv_c_bwd(uo, w, dsc, name):
    T = uo.shape[0]
    K, nt = CONV_C, D // LANES

    def body(bg_ref, cg_ref, v_ref, w_ref, d_ref, dbg_ref, dcg_ref, dv_ref, dw_ref, pad_ref, padd_ref):
        _zero_head(pad_ref)
        _zero_tail(padd_ref, T)
        dw_ref[...] = jnp.zeros_like(dw_ref)
        pad_ref[CONV_PAD:CONV_PAD + T, :] = cg_ref[...] * v_ref[...]

        def pre(base):
            rows = pl.ds(base, CONV_R)
            d = d_ref[rows, :]
            dbg_ref[rows, :] = (d * _taps_fwd(pad_ref, w_ref, K, base)).astype(BF16)
            padd_ref[rows, :] = d * bg_ref[rows, :]
        _loop_rows(T, pre)

        def main(base):
            rows = pl.ds(base, CONV_R)
            _wgrad_acc(dw_ref, pad_ref, padd_ref[rows, :], K, base)
            dq = _taps_bwd(padd_ref, w_ref, K, base)
            dcg_ref[rows, :] = (dq * v_ref[rows, :]).astype(BF16)
            dv_ref[rows, :] = (dq * cg_ref[rows, :]).astype(BF16)
        _loop_rows(T, main)

    return _conv_call(body, name, T, nt, [_col(T), _col(T, nt), _col(T, 2 * nt), _col(T, rows=K), _col(T)],
                      (_col(T), _col(T), _col(T), _col(T, rows=K * SUBLANES)),
                      (_sds((T, D), BF16), _sds((T, D), BF16), _sds((T, D), BF16), _sds((K * SUBLANES, D))), 1, 1)(uo, uo, uo, w, dsc)


def _conv_f_fwd(up, w, b, name):
    T = up.shape[0]
    K, nt = CONV_F, D_FF // LANES

    def body(u1_ref, u2_ref, w1_ref, w2_ref, b1_ref, b2_ref, o_ref, pad1_ref, pad2_ref):
        _zero_head(pad1_ref)
        _zero_head(pad2_ref)
        pad1_ref[CONV_PAD:CONV_PAD + T, :] = u1_ref[...]
        pad2_ref[CONV_PAD:CONV_PAD + T, :] = u2_ref[...]

        def main(base):
            h1 = _taps_fwd(pad1_ref, w1_ref, K, base) + b1_ref[...]
            h2 = _taps_fwd(pad2_ref, w2_ref, K, base) + b2_ref[...]
            o_ref[pl.ds(base, CONV_R), :] = (h1 * _sig(h1) * h2).astype(BF16)
        _loop_rows(T, main)

    return _conv_call(body, name, T, nt,
                      [_col(T), _col(T, nt), _col(T, rows=K), _col(T, nt, rows=K), _col(T, rows=1), _col(T, nt, rows=1)],
                      _col(T), _sds((T, D_FF), BF16), 2)(up, up, w, w, b, b)


def _conv_f_bwd(up, w, b, dact, name):
    T = up.shape[0]
    K, nt = CONV_F, D_FF // LANES

    def body(u1_ref, u2_ref, w1_ref, w2_ref, b1_ref, b2_ref, d_ref, du1_ref, du2_ref, dw1_ref, dw2_ref, db1_ref, db2_ref,
             pad1_ref, pad2_ref, padd1_ref, padd2_ref):
        _zero_head(pad1_ref)
        _zero_head(pad2_ref)
        _zero_tail(padd1_ref, T)
        _zero_tail(padd2_ref, T)
        for r in (dw1_ref, dw2_ref, db1_ref, db2_ref):
            r[...] = jnp.zeros_like(r)
        pad1_ref[CONV_PAD:CONV_PAD + T, :] = u1_ref[...]
        pad2_ref[CONV_PAD:CONV_PAD + T, :] = u2_ref[...]

        def pre(base):
            rows = pl.ds(base, CONV_R)
            h1 = _taps_fwd(pad1_ref, w1_ref, K, base) + b1_ref[...]
            h2 = _taps_fwd(pad2_ref, w2_ref, K, base) + b2_ref[...]
            s = _sig(h1)
            d = d_ref[rows, :]
            padd1_ref[rows, :] = d * h2 * s * (1.0 + h1 * (1.0 - s))
            padd2_ref[rows, :] = d * h1 * s
        _loop_rows(T, pre)

        def main(base):
            rows = pl.ds(base, CONV_R)
            d1, d2 = padd1_ref[rows, :], padd2_ref[rows, :]
            _wgrad_acc(dw1_ref, pad1_ref, d1, K, base)
            _wgrad_acc(dw2_ref, pad2_ref, d2, K, base)
            db1_ref[...] += _fold8(d1)
            db2_ref[...] += _fold8(d2)
            du1_ref[rows, :] = _taps_bwd(padd1_ref, w1_ref, K, base).astype(BF16)
            du2_ref[rows, :] = _taps_bwd(padd2_ref, w2_ref, K, base).astype(BF16)
        _loop_rows(T, main)

    wrow, brow = _col(T, rows=K * SUBLANES), _col(T, rows=SUBLANES)
    return _conv_call(body, name, T, nt,
                      [_col(T), _col(T, nt), _col(T, rows=K), _col(T, nt, rows=K), _col(T, rows=1), _col(T, nt, rows=1), _col(T)],
                      (_col(T), _col(T), wrow, wrow, brow, brow),
                      (_sds((T, D_FF), BF16), _sds((T, D_FF), BF16), _sds((K * SUBLANES, D_FF)), _sds((K * SUBLANES, D_FF)),
                       _sds((SUBLANES, D_FF)), _sds((SUBLANES, D_FF))), 2, 2)(up, up, w, w, b, b, dact)


def _dot(a, b, dims="nn"):
    return lax.dot_general(a.astype(BF16), b.astype(BF16), _DIMS[dims], preferred_element_type=F32)


def _ssd_small(xcr_ref, xrr_ref, bc_ref, br_ref, ac_ref, ar_ref):
    Q = SSD_Q
    li = lax.broadcasted_iota(jnp.int32, (Q, Q), 0)
    si = lax.broadcasted_iota(jnp.int32, (Q, Q), 1)
    tril = li >= si
    dtc = jax.nn.softplus(xcr_ref[...] + bc_ref[...])
    dtr = jax.nn.softplus(xrr_ref[...] + br_ref[...])
    cumc = jnp.dot(tril.astype(F32), dtc * ac_ref[...], precision=HI, preferred_element_type=F32)
    cumr = jnp.dot(dtr * ar_ref[...], (li <= si).astype(F32), precision=HI, preferred_element_type=F32)
    return tril, dtc, dtr, cumc, cumr


def _ssd_specs(nc, rev):
    Q = SSD_Q
    cc = (lambda c: nc - 1 - c) if rev else (lambda c: c)
    x_spec = pl.BlockSpec((Q, 2 * LANES), lambda g, c: (cc(c), g))
    b_spec = pl.BlockSpec((Q, LANES), lambda g, c: (cc(c), 8 + g))
    c_spec = pl.BlockSpec((Q, LANES), lambda g, c: (cc(c), 12 + g))
    colm = pl.BlockSpec((None, Q, LANES), lambda g, c: (g, cc(c), 0))
    rowm = pl.BlockSpec((None, SUBLANES, Q), lambda g, c: (g, 0, cc(c)))
    colv = pl.BlockSpec((None, 1, LANES), lambda g, c: (g, 0, 0))
    rowv = pl.BlockSpec((None, SUBLANES, 1), lambda g, c: (g, 0, 0))
    st_spec = pl.BlockSpec((None, None, 2 * LANES, N_STATE), lambda g, c: (cc(c), g, 0, 0))
    return x_spec, b_spec, c_spec, colm, rowm, colv, rowv, st_spec


def _ssd_fwd(xc, raw_col, raw_row, bias_col, bias_row, a_col, a_row, dskip, name):
    T = xc.shape[0]
    Q = SSD_Q
    nc = T // Q
    x_spec, b_spec, c_spec, colm, rowm, colv, rowv, st_spec = _ssd_specs(nc, False)

    def body(dk_ref, x_ref, b_ref, c_ref, xcr_ref, xrr_ref, bc_ref, br_ref, ac_ref, ar_ref, y_ref, st_ref, h_ref):
        g = pl.program_id(0)

        @pl.when(pl.program_id(1) == 0)
        def _():
            h_ref[...] = jnp.zeros_like(h_ref)

        tril, dtc, dtr, cumc, cumr = _ssd_small(xcr_ref, xrr_ref, bc_ref, br_ref, ac_ref, ar_ref)
        Bm, Cm = b_ref[...], c_ref[...]
        S = _dot(Cm, Bm, "nt")
        lo = lax.broadcasted_iota(jnp.int32, (Q, LANES), 1) < HEAD_P
        rlo = lax.broadcasted_iota(jnp.int32, (LANES, N_STATE), 0) < HEAD_P
        st_ref[...] = h_ref[...]
        clast = cumc[Q - 1:Q, :]
        for pr in range(2):
            cols = slice(pr * LANES, (pr + 1) * LANES)
            xp = x_ref[:, cols]
            yd = jnp.zeros((Q, LANES), F32)
            for q in range(2):
                hh = 2 * pr + q
                seg = cumc[:, hh:hh + 1] - cumr[hh:hh + 1, :]
                lm = jnp.where(tril, jnp.exp(jnp.where(tril, seg, 0.0)), 0.0)
                w = S * lm * dtr[hh:hh + 1, :]
                xm = jnp.where(lo if q == 0 else jnp.logical_not(lo), xp, 0.0)
                yd = yd + _dot(w, xm)
            h0, h1 = 2 * pr, 2 * pr + 1
            c0, c1 = cumc[:, h0:h0 + 1], cumc[:, h1:h1 + 1]
            e_pair = jnp.where(lo, jnp.exp(c0), jnp.exp(c1))
            hp = h_ref[cols, :]
            ch = _dot(Cm, hp, "nt")
            dsk = jnp.where(lo, dk_ref[4 * g + h0], dk_ref[4 * g + h1])
            y_ref[:, cols] = yd + e_pair * ch + dsk * xp
            cl0, cl1 = clast[:, h0:h0 + 1], clast[:, h1:h1 + 1]
            sdec = jnp.where(lo, jnp.exp(cl0 - c0) * dtc[:, h0:h0 + 1], jnp.exp(cl1 - c1) * dtc[:, h1:h1 + 1])
            decrow = jnp.where(rlo, jnp.exp(cl0), jnp.exp(cl1))
            h_ref[cols, :] = hp * decrow + _dot(xp * sdec, Bm, "tn")

    smem = pl.BlockSpec(memory_space=pltpu.SMEM)
    return pl.pallas_call(
        body, name=name, grid=(N_GROUPS, nc),
        in_specs=[smem, x_spec, b_spec, c_spec, colm, rowm, colv, rowv, colv, rowv],
        out_specs=(x_spec, st_spec),
        out_shape=(_sds((T, D)), _sds((nc, N_GROUPS, 2 * LANES, N_STATE))),
        scratch_shapes=[pltpu.VMEM((2 * LANES, N_STATE), F32)],
        compiler_params=_cparams(("parallel", "arbitrary")))(dskip, xc, xc, xc, raw_col, raw_row, bias_col, bias_row, a_col, a_row)


def _ssd_bwd(xc, raw_col, raw_row, bias_col, bias_row, a_col, a_row, dskip, states, dy, name):
    T = xc.shape[0]
    Q = SSD_Q
    nc = T // Q
    x_spec, b_spec, c_spec, colm, rowm, colv, rowv, st_spec = _ssd_specs(nc, True)
    bo_spec = pl.BlockSpec((Q, LANES), lambda g, c: (nc - 1 - c, g))
    dd_spec = pl.BlockSpec((None, None, SUBLANES, 2 * LANES), lambda g, c: (nc - 1 - c, g, 0, 0))

    def body(dk_ref, x_ref, b_ref, c_ref, xcr_ref, xrr_ref, bc_ref, br_ref, ac_ref, ar_ref, st_ref, dy_ref,
             dx_ref, db_ref, dc_ref, sq_ref, cms_ref, ddac_ref, ddar_ref, dd_ref, dh_ref):
        g = pl.program_id(0)

        @pl.when(pl.program_id(1) == 0)
        def _():
            dh_ref[...] = jnp.zeros_like(dh_ref)

        tril, dtc, dtr, cumc, cumr = _ssd_small(xcr_ref, xrr_ref, bc_ref, br_ref, ac_ref, ar_ref)
        Bm, Cm = b_ref[...], c_ref[...]
        S = _dot(Cm, Bm, "nt")
        lane = lax.broadcasted_iota(jnp.int32, (Q, LANES), 1)
        sub = lax.broadcasted_iota(jnp.int32, (SUBLANES, Q), 0)
        rowi = lax.broadcasted_iota(jnp.int32, (Q, LANES), 0)
        lo = lane < HEAD_P
        rlo = lax.broadcasted_iota(jnp.int32, (LANES, N_STATE), 0) < HEAD_P
        clast = cumc[Q - 1:Q, :]
        ds_g = jnp.zeros((Q, Q), F32)
        dcm = jnp.zeros((Q, N_STATE), F32)
        dbm = jnp.zeros((Q, N_STATE), F32)
        dcum_col = jnp.zeros((Q, LANES), F32)
        dcum_row = jnp.zeros((SUBLANES, Q), F32)
        sq_col = jnp.zeros((Q, LANES), F32)
        cms_row = jnp.zeros((SUBLANES, Q), F32)
        for pr in range(2):
            cols = slice(pr * LANES, (pr + 1) * LANES)
            xp, dyp = x_ref[:, cols], dy_ref[:, cols]
            hin, dhp = st_ref[cols, :], dh_ref[cols, :]
            h0, h1 = 2 * pr, 2 * pr + 1
            c0, c1 = cumc[:, h0:h0 + 1], cumc[:, h1:h1 + 1]
            cl0, cl1 = clast[:, h0:h0 + 1], clast[:, h1:h1 + 1]
            e_pair = jnp.where(lo, jnp.exp(c0), jnp.exp(c1))
            edec = jnp.where(lo, jnp.exp(cl0 - c0), jnp.exp(cl1 - c1))
            dt_pair = jnp.where(lo, dtc[:, h0:h0 + 1], dtc[:, h1:h1 + 1])
            sdec = edec * dt_pair
            ch = _dot(Cm, hin, "nt")
            xb = _dot(Bm, dhp, "nt")
            dye = dyp * e_pair
            t1 = dye * ch
            t2 = xp * xb * edec
            hh_prod = dhp * hin
            dsk = jnp.where(lo, dk_ref[4 * g + h0], dk_ref[4 * g + h1])
            dxp = sdec * xb + dsk * dyp
            for q in range(2):
                hh = 2 * pr + q
                mine = lo if q == 0 else jnp.logical_not(lo)
                seg = cumc[:, hh:hh + 1] - cumr[hh:hh + 1, :]
                lm = jnp.where(tril, jnp.exp(jnp.where(tril, seg, 0.0)), 0.0)
                dtrow = dtr[hh:hh + 1, :]
                w = S * lm * dtrow
                dym = jnp.where(mine, dyp, 0.0)
                gl = _dot(dym, xp, "nt") * lm
                ds_g = ds_g + gl * dtrow
                ms = gl * S
                m = ms * dtrow
                dxp = dxp + _dot(w, dym, "tn")
                cms_row = jnp.where(sub == hh, jnp.sum(ms, axis=0, keepdims=True), cms_row)
                dcum_row = jnp.where(sub == hh, -jnp.sum(m, axis=0, keepdims=True), dcum_row)
                t1h = jnp.sum(jnp.where(mine, t1, 0.0), axis=1, keepdims=True)
                sqh = jnp.sum(jnp.where(mine, t2, 0.0), axis=1, keepdims=True)
                sth = sqh * dtc[:, hh:hh + 1]
                rmine = rlo if q == 0 else jnp.logical_not(rlo)
                hsum = jnp.sum(jnp.sum(jnp.where(rmine, hh_prod, 0.0), axis=1, keepdims=True), axis=0, keepdims=True)
                last = jnp.sum(sth, axis=0, keepdims=True) + jnp.exp(clast[:, hh:hh + 1]) * hsum
                dcol = jnp.sum(m, axis=1, keepdims=True) + t1h - sth
                dcum_col = jnp.where(lane == hh, dcol + jnp.where(rowi == Q - 1, last, 0.0), dcum_col)
                sq_col = jnp.where(lane == hh, sqh, sq_col)
            dcm = dcm + _dot(dye, hin)
            dbm = dbm + _dot(xp * sdec, dhp)
            decrow = jnp.where(rlo, jnp.exp(cl0), jnp.exp(cl1))
            dh_ref[cols, :] = dhp * decrow + _dot(dye, Cm, "tn")
            dx_ref[:, cols] = dxp
            dd_ref[:, cols] = jnp.broadcast_to(jnp.sum(dyp * xp, axis=0, keepdims=True), (SUBLANES, LANES))
        dc_ref[...] = dcm + _dot(ds_g, Bm)
        db_ref[...] = dbm + _dot(ds_g, Cm, "tn")
        li = lax.broadcasted_iota(jnp.int32, (Q, Q), 0)
        si = lax.broadcasted_iota(jnp.int32, (Q, Q), 1)
        ddac_ref[...] = jnp.dot((li <= si).astype(F32), dcum_col, precision=HI, preferred_element_type=F32)
        ddar_ref[...] = jnp.dot(dcum_row, tril.astype(F32), precision=HI, preferred_element_type=F32)
        sq_ref[...] = sq_col
        cms_ref[...] = cms_row

    smem = pl.BlockSpec(memory_space=pltpu.SMEM)
    return pl.pallas_call(
        body, name=name, grid=(N_GROUPS, nc),
        in_specs=[smem, x_spec, b_spec, c_spec, colm, rowm, colv, rowv, colv, rowv, st_spec, x_spec],
        out_specs=(x_spec, bo_spec, bo_spec, colm, rowm, colm, rowm, dd_spec),
        out_shape=(_sds((T, D)), _sds((T, D // 2)), _sds((T, D // 2)), _sds((N_GROUPS, T, LANES)), _sds((N_GROUPS, SUBLANES, T)),
                   _sds((N_GROUPS, T, LANES)), _sds((N_GROUPS, SUBLANES, T)), _sds((nc, N_GROUPS, SUBLANES, 2 * LANES))),
        scratch_shapes=[pltpu.VMEM((2 * LANES, N_STATE), F32)],
        compiler_params=_cparams(("parallel", "arbitrary")))(dskip, xc, xc, xc, raw_col, raw_row, bias_col, bias_row, a_col, a_row,
                                                            states, dy)


def _adamw(w, g, m, v, name):
    shape = w.shape
    cols = shape[-1]
    w2, g2, m2, v2 = (t.reshape(-1, cols) for t in (w, g, m, v))
    rows = w2.shape[0]
    tr = 256 if (rows % 256 == 0 and rows > 256) else rows
    c1 = 1.0 - ADAM_B1 ** ADAM_STEP
    c2 = 1.0 - ADAM_B2 ** ADAM_STEP

    def body(w_ref, g_ref, m_ref, v_ref, d_ref, mo_ref, vo_ref):
        gv = g_ref[...]
        mn = ADAM_B1 * m_ref[...] + (1.0 - ADAM_B1) * gv
        vn = ADAM_B2 * v_ref[...] + (1.0 - ADAM_B2) * (gv * gv)
        d_ref[...] = -ADAM_LR * ((mn / c1) / (jnp.sqrt(vn / c2) + ADAM_EPS) + ADAM_WD * w_ref[...])
        mo_ref[...] = mn
        vo_ref[...] = vn

    spec = pl.BlockSpec((tr, cols), lambda i: (i, 0))
    out = pl.pallas_call(body, name=name, grid=(rows // tr,), in_specs=[spec] * 4, out_specs=(spec,) * 3,
                         out_shape=(_sds((rows, cols)),) * 3, compiler_params=_cparams(("parallel",)))(w2, g2, m2, v2)
    return tuple(o.reshape(shape) for o in out)


def _place():
    x, y, c = lax.axis_index("x"), lax.axis_index("y"), lax.axis_index("c")
    chips = [(1 - x, y), (x, 1 - y), (1 - x, 1 - y)]
    return x, y, c, chips


_ANY = pl.BlockSpec(memory_space=pl.ANY)


def _gather_weights(wb, ws):
    R = wb.shape[0]
    HR = R // 2

    def body(wb_ref, ws_ref, WB_ref, WS_ref, send_sems, recv_sems, lsems):
        x, y, c, chips = _place()
        me = 2 * x + y
        sib = (x, y, 1 - c)
        half = pl.ds(pl.multiple_of(c * HR, 32), HR)
        ohalf = pl.ds(pl.multiple_of((1 - c) * HR, 32), HR)
        l1 = pltpu.make_async_copy(wb_ref, WB_ref.at[me], lsems.at[0])
        l2 = pltpu.make_async_copy(ws_ref, WS_ref.at[me], lsems.at[1])
        l1.start()
        l2.start()

        def rcopy(k, src, dst, to):
            return pltpu.make_async_remote_copy(src_ref=src, dst_ref=dst, send_sem=send_sems.at[k], recv_sem=recv_sems.at[k],
                                                device_id=to, device_id_type=MESH)

        first = [rcopy(j, wb_ref.at[half], WB_ref.at[me, half], (*chip, c)) for j, chip in enumerate(chips)]
        small = [rcopy(6 + j, ws_ref, WS_ref.at[me], (*chip, c)) for j, chip in enumerate(chips)]
        for cp in first + small:
            cp.start()
        passed = []
        for j, (px, py) in enumerate(chips):
            slot = WB_ref.at[2 * px + py, half]
            rcopy(j, slot, slot, sib).wait_recv()
            fwd = rcopy(3 + j, slot, slot, sib)
            fwd.start()
            passed.append(fwd)
        for j, (px, py) in enumerate(chips):
            oslot = WB_ref.at[2 * px + py, ohalf]
            rcopy(3 + j, oslot, oslot, sib).wait_recv()
            sslot = WS_ref.at[2 * px + py]
            rcopy(6 + j, sslot, sslot, sib).wait_recv()
        for cp in first + small + passed:
            cp.wait_send()
        l1.wait()
        l2.wait()

    return pl.pallas_call(
        body, name="gather_weights", in_specs=[_ANY, _ANY], out_specs=(_ANY, _ANY),
        out_shape=(jax.ShapeDtypeStruct((4,) + wb.shape, wb.dtype), jax.ShapeDtypeStruct((4,) + ws.shape, ws.dtype)),
        scratch_shapes=[pltpu.SemaphoreType.DMA((9,)), pltpu.SemaphoreType.DMA((9,)), pltpu.SemaphoreType.DMA((2,))],
        compiler_params=pltpu.CompilerParams(has_side_effects=True))(wb, ws)


def _swap_sibling(v, name):
    def body(v_ref, o_ref, send_sem, recv_sem):
        x, y, c, _ = _place()
        cp = pltpu.make_async_remote_copy(src_ref=v_ref, dst_ref=o_ref, send_sem=send_sem, recv_sem=recv_sem,
                                          device_id=(x, y, 1 - c), device_id_type=MESH)
        cp.start()
        cp.wait()

    return pl.pallas_call(body, name=name, in_specs=[_ANY], out_specs=_ANY, out_shape=jax.ShapeDtypeStruct(v.shape, v.dtype),
                          scratch_shapes=[pltpu.SemaphoreType.DMA(()), pltpu.SemaphoreType.DMA(())],
                          compiler_params=pltpu.CompilerParams(has_side_effects=True))(v)


def _scatter_to_chips(pb):
    def body(p_ref, o_ref, send_sems, recv_sems, lsem):
        x, y, c, chips = _place()
        me = 2 * x + y
        loc = pltpu.make_async_copy(p_ref.at[me], o_ref.at[me], lsem)
        loc.start()
        sends = []
        for j, (px, py) in enumerate(chips):
            cp = pltpu.make_async_remote_copy(src_ref=p_ref.at[2 * px + py], dst_ref=o_ref.at[me], send_sem=send_sems.at[j],
                                              recv_sem=recv_sems.at[j], device_id=(px, py, c), device_id_type=MESH)
            cp.start()
            sends.append(cp)
        for j, (px, py) in enumerate(chips):
            slot = o_ref.at[2 * px + py]
            pltpu.make_async_remote_copy(src_ref=slot, dst_ref=slot, send_sem=send_sems.at[j], recv_sem=recv_sems.at[j],
                                         device_id=(px, py, c), device_id_type=MESH).wait_recv()
        for cp in sends:
            cp.wait_send()
        loc.wait()

    return pl.pallas_call(body, name="scatter_grads", in_specs=[_ANY], out_specs=_ANY,
                          out_shape=jax.ShapeDtypeStruct(pb.shape, pb.dtype),
                          scratch_shapes=[pltpu.SemaphoreType.DMA((3,)), pltpu.SemaphoreType.DMA((3,)), pltpu.SemaphoreType.DMA(())],
                          compiler_params=pltpu.CompilerParams(has_side_effects=True))(pb)


def _join_halves(rh):
    HR = rh.shape[0]

    def body(r_ref, o_ref, send_sem, recv_sem, lsem):
        x, y, c, _ = _place()
        half = pl.ds(pl.multiple_of(c * HR, 32), HR)
        ohalf = pl.ds(pl.multiple_of((1 - c) * HR, 32), HR)
        loc = pltpu.make_async_copy(r_ref, o_ref.at[half], lsem)
        loc.start()
        cp = pltpu.make_async_remote_copy(src_ref=r_ref, dst_ref=o_ref.at[half], send_sem=send_sem, recv_sem=recv_sem,
                                          device_id=(x, y, 1 - c), device_id_type=MESH)
        cp.start()
        pltpu.make_async_remote_copy(src_ref=r_ref, dst_ref=o_ref.at[ohalf], send_sem=send_sem, recv_sem=recv_sem,
                                     device_id=(x, y, 1 - c), device_id_type=MESH).wait_recv()
        cp.wait_send()
        loc.wait()

    return pl.pallas_call(body, name="join_halves", in_specs=[_ANY], out_specs=_ANY,
                          out_shape=jax.ShapeDtypeStruct((2 * HR,) + rh.shape[1:], rh.dtype),
                          scratch_shapes=[pltpu.SemaphoreType.DMA(()), pltpu.SemaphoreType.DMA(()), pltpu.SemaphoreType.DMA(())],
                          compiler_params=pltpu.CompilerParams(has_side_effects=True))(rh)


def _allgather_small(v):
    m_per, n = v.shape

    def body(x_ref, out_ref, send_sems, recv_sems, local_sem):
        x, y, c, chips = _place()
        me, sibling = (x, y, c), (x, y, 1 - c)

        def rows(px, py, pc):
            return out_ref.at[pl.ds(pl.multiple_of((4 * px + 2 * py + pc) * m_per, SUBLANES), m_per), :]

        def copy(k, block, to, src=None):
            return pltpu.make_async_remote_copy(src_ref=rows(*block) if src is None else src, dst_ref=rows(*block),
                                                send_sem=send_sems.at[k], recv_sem=recv_sems.at[k], device_id=to, device_id_type=MESH)

        mine = pltpu.make_async_copy(x_ref, rows(*me), local_sem)
        mine.start()
        first = [copy(0, me, sibling, src=x_ref)]
        first += [copy(1 + j, me, (*chip, c), src=x_ref) for j, chip in enumerate(chips)]
        for cp in first:
            cp.start()
        passed = [copy(4 + j, (*chip, c), sibling) for j, chip in enumerate(chips)]
        for j, chip in enumerate(chips):
            copy(1 + j, (*chip, c), me).wait_recv()
            passed[j].start()
        copy(0, sibling, me).wait_recv()
        for j, chip in enumerate(chips):
            copy(4 + j, (*chip, 1 - c), me).wait_recv()
        for cp in first + passed:
            cp.wait_send()
        mine.wait()

    vm = pl.BlockSpec(memory_space=pltpu.VMEM)
    return pl.pallas_call(body, name="allgather_small", in_specs=[vm], out_specs=vm, out_shape=_sds((8 * m_per, n)),
                          scratch_shapes=[pltpu.SemaphoreType.DMA((7,)), pltpu.SemaphoreType.DMA((7,)), pltpu.SemaphoreType.DMA(())],
                          compiler_params=pltpu.CompilerParams(has_side_effects=True, vmem_limit_bytes=VMEM_LIMIT))(v)


def _add_cast(gk, ra):
    n, hr, w = gk.shape
    spec = pl.BlockSpec((None, 512, w), lambda k, i: (k, i, 0))

    def body(a_ref, b_ref, o_ref):
        o_ref[...] = (a_ref[...] + b_ref[...].astype(F32)).astype(BF16)

    return pl.pallas_call(body, name="grad_add_sibling", grid=(n, hr // 512), in_specs=[spec, spec], out_specs=spec,
                          out_shape=_sds(gk.shape, BF16), compiler_params=_cparams(("parallel", "parallel")))(gk, ra)


def _sum4(rc):
    n, hr, w = rc.shape
    specs = [pl.BlockSpec((None, 512, w), functools.partial(lambda i, k: (k, i, 0), k=k)) for k in range(4)]

    def body(r0, r1, r2, r3, o_ref):
        o_ref[...] = ((r0[...].astype(F32) + r1[...].astype(F32)) + r2[...].astype(F32)) + r3[...].astype(F32)

    return pl.pallas_call(body, name="grad_sum_chips", grid=(hr // 512,), in_specs=specs,
                          out_specs=pl.BlockSpec((512, w), lambda i: (i, 0)), out_shape=_sds((hr, w)),
                          compiler_params=_cparams(("parallel",)))(rc, rc, rc, rc)


def _sum8(v, m_per):
    def body(v_ref, o_ref):
        acc = v_ref[0:m_per, :]
        for k in range(1, 8):
            acc = acc + v_ref[k * m_per:(k + 1) * m_per, :]
        o_ref[...] = acc

    return pl.pallas_call(body, name="small_sum_devices", out_shape=_sds((m_per, v.shape[1])),
                          compiler_params=pltpu.CompilerParams(vmem_limit_bytes=VMEM_LIMIT))(v)


BIG = (("e_w_in", 2), ("e_w_out", 1), ("o_w_in", 2), ("o_w_out", 1), ("f_w_up", 2), ("f_w_down", 1), ("ple_w_proj", 2),
       ("ple_w_gate", 1))
SMALL_SHARDED = (("e_conv_a_w", 2), ("e_conv_b_w", 2), ("o_conv_w", 2), ("f_conv_w", 2), ("ln_g", 2), ("ln_b", 2))
SMALL_REPL = ("e_conv_a_b", "e_ln_a_g", "e_ln_a_b", "e_conv_b_b", "e_dt_bias", "e_a_log", "e_d_skip", "e_norm_b_g", "f_conv_b")

WEIGHT_ORDER = ('e_w_in', 'e_conv_a_w', 'e_conv_a_b', 'e_ln_a_g', 'e_ln_a_b', 'e_conv_b_w', 'e_conv_b_b', 'e_dt_bias', 'e_a_log',
                'e_d_skip', 'e_norm_b_g', 'e_w_out', 'o_w_in', 'o_conv_w', 'o_w_out', 'f_w_up', 'f_conv_w', 'f_conv_b', 'f_w_down',
                'ple_w_proj', 'ple_w_gate', 'ln_g', 'ln_b')


def _pack_rows(parts, width, total_rows, dtype):
    flat = jnp.concatenate([p.reshape(-1).astype(dtype) for p in parts])
    flat = jnp.pad(flat, (0, total_rows * width - flat.shape[0]))
    return flat.reshape(total_rows, width)


def _unpack_rows(buf, shapes):
    flat = buf.reshape(-1)
    out, pos = [], 0
    for s in shapes:
        n = math.prod(s)
        out.append(flat[pos:pos + n].reshape(s))
        pos += n
    return out


def _small_rows(shapes):
    n = sum(math.prod(s) for s in shapes)
    return -(-n // (LANES * SUBLANES)) * SUBLANES


def _seg(w, bounds):
    return [w[:, a:b] for a, b in bounds]


def _padcols(w, width):
    return jnp.pad(w, ((0, 0), (0, width - w.shape[1])))


def _fold_rows(dw, K):
    return dw.reshape(K, SUBLANES, dw.shape[-1]).sum(1)


def _local_step(x, p, target, W):
    T = x.shape[0]
    xb = x.astype(BF16)
    saved = []
    xc_f = x
    for i in range(DEPTH):
        j = i // 2
        L = {}
        L["x"], L["xb"] = xc_f, xb
        if i % 2 == 0:
            w_in = W["e_w_in"][j]
            segs = _seg(w_in, ((0, 2 * D), (2 * D, 3 * D), (3 * D, 5 * D))) + [_padcols(w_in[:, 5 * D:], LANES)]
            L["w_in"] = segs
            ua = _mm(xb, segs[0], "nn", f"l{i}_in_a")
            z = _mm(xb, segs[1], "nn", f"l{i}_in_z")
            xu = _mm(xb, segs[2], "nn", f"l{i}_in_xbc")
            udt = _mm(xb, segs[3], "nn", f"l{i}_in_dt")
            ac = _conv_a_fwd(ua, W["e_conv_a_w"][j], W["e_conv_a_b"][j][None], f"l{i}_conv_a")
            ya = _ln_silu_fwd(ac, W["e_ln_a_g"][j][None], W["e_ln_a_b"][j][None], f"l{i}_ln_a")
            xc = _conv_b_fwd(xu, W["e_conv_b_w"][j], W["e_conv_b_b"][j][None], f"l{i}_conv_b")
            sm = _ssd_small_inputs(udt[:, :N_HEADS], W["e_dt_bias"][j], W["e_a_log"][j])
            y, states = _ssd_fwd(xc, *sm, W["e_d_skip"][j], f"l{i}_ssd")
            yb = _gate_rms_fwd(y, z, W["e_norm_b_g"][j][None], f"l{i}_gate_rms")
            w_out = W["e_w_out"][j]
            mix = _mm(ya, w_out[:D], "nn", f"l{i}_out_a")
            mix = _mm(yb, w_out[D:], "nn", f"l{i}_out_b", add=mix)
            L.update(ua=ua, z=z, xu=xu, udt=udt, ac=ac, ya=ya, xc=xc, sm=sm, y=y, states=states, yb=yb)
        else:
            w_in = W["o_w_in"][j]
            uo = _mm(xb, w_in, "nn", f"l{i}_in")
            sc = _conv_c_fwd(uo, W["o_conv_w"][j], f"l{i}_conv_c")
            mix = _mm(sc, W["o_w_out"][j], "nn", f"l{i}_out")
            L.update(uo=uo, sc=sc)
        h1, x1, x1b = _res_ln_fwd(xc_f, [mix], None, W["ln_g"][i, 0][None], W["ln_b"][i, 0][None], f"l{i}_ln1")
        up = _mm(x1b, W["f_w_up"][i], "nn", f"l{i}_ffn_up")
        act = _conv_f_fwd(up, W["f_conv_w"][i], W["f_conv_b"][i][None], f"l{i}_conv_f")
        ffn = _mm(act, W["f_w_down"][i], "nn", f"l{i}_ffn_down")
        pb = p[i].astype(BF16)
        pp = _mm(pb, W["ple_w_proj"][i], "nn", f"l{i}_ple_proj")
        gl = _mm(x1b, W["ple_w_gate"][i], "nn", f"l{i}_ple_gate")
        h2, x2, x2b = _res_ln_fwd(x1, [ffn], (pp, gl), W["ln_g"][i, 1][None], W["ln_b"][i, 1][None], f"l{i}_ln2")
        L.update(h1=h1, x1=x1, x1b=x1b, up=up, act=act, pb=pb, pp=pp, gl=gl, h2=h2)
        saved.append(L)
        xc_f, xb = x2, x2b

    sq, dx = _loss_head(xc_f, target, "loss_head")

    G = {n: [None] * (DEPTH if n.startswith(("f_", "ple_", "ln_")) else DEPTH // 2) for n in WEIGHT_ORDER}
    for i in reversed(range(DEPTH)):
        j = i // 2
        L = saved[i]
        dh2, dh2b, dg2, db2, dpp, dgl = _res_ln_bwd(dx, L["h2"], W["ln_g"][i, 1][None], (L["pp"], L["gl"]), f"l{i}_ln2_bwd")
        G["f_w_down"][i] = _mm(L["act"], dh2b, "tn", f"l{i}_dw_down")
        dact = _mm(dh2b, W["f_w_down"][i], "nt", f"l{i}_dact")
        du1, du2, dw1, dw2, dbf1, dbf2 = _conv_f_bwd(L["up"], W["f_conv_w"][i], W["f_conv_b"][i][None], dact, f"l{i}_conv_f_bwd")
        G["f_conv_w"][i] = jnp.concatenate([_fold_rows(dw1, CONV_F), _fold_rows(dw2, CONV_F)], axis=1)
        G["f_conv_b"][i] = jnp.concatenate([dbf1.sum(0), dbf2.sum(0)])
        w_up = W["f_w_up"][i]
        G["f_w_up"][i] = jnp.concatenate([_mm(L["x1b"], du1, "tn", f"l{i}_dw_up1"), _mm(L["x1b"], du2, "tn", f"l{i}_dw_up2")], axis=1)
        G["ple_w_proj"][i] = _mm(L["pb"], dpp, "tn", f"l{i}_dw_proj")
        G["ple_w_gate"][i] = _mm(L["x1b"], dgl, "tn", f"l{i}_dw_gate")
        dx1 = _mm(du1, w_up[:, :D_FF], "nt", f"l{i}_dx1_a", add=dh2, add_scale=ALPHA)
        dx1 = _mm(du2, w_up[:, D_FF:], "nt", f"l{i}_dx1_b", add=dx1)
        dx1 = _mm(dgl, W["ple_w_gate"][i], "nt", f"l{i}_dx1_c", add=dx1)
        dh1, dh1b, dg1, db1 = _res_ln_bwd(dx1, L["h1"], W["ln_g"][i, 0][None], None, f"l{i}_ln1_bwd")
        G["ln_g"][i] = jnp.concatenate([dg1, dg2], axis=0)
        G["ln_b"][i] = jnp.concatenate([db1, db2], axis=0)
        if i % 2 == 0:
            w_out = W["e_w_out"][j]
            G["e_w_out"][j] = jnp.concatenate([_mm(L["ya"], dh1b, "tn", f"l{i}_dw_out_a"), _mm(L["yb"], dh1b, "tn", f"l{i}_dw_out_b")], axis=0)
            dya = _mm(dh1b, w_out[:D], "nt", f"l{i}_dya")
            dyb = _mm(dh1b, w_out[D:], "nt", f"l{i}_dyb")
            dac, dga, dba = _ln_silu_bwd(L["ac"], dya, W["e_ln_a_g"][j][None], W["e_ln_a_b"][j][None], f"l{i}_ln_a_bwd")
            G["e_ln_a_g"][j], G["e_ln_a_b"][j] = dga[0], dba[0]
            dal, dag, dwa, dbca = _conv_a_bwd(L["ua"], W["e_conv_a_w"][j], dac, f"l{i}_conv_a_bwd")
            G["e_conv_a_w"][j] = _fold_rows(dwa, CONV_A)
            G["e_conv_a_b"][j] = dbca.sum(0)
            dy, dz, dgn = _gate_rms_bwd(L["y"], L["z"], dyb, W["e_norm_b_g"][j][None], f"l{i}_gate_rms_bwd")
            G["e_norm_b_g"][j] = dgn[0]
            dxs, dbs, dcs, sq_col, cms_row, dda_col, dda_row, ddp = _ssd_bwd(L["xc"], *L["sm"], W["e_d_skip"][j], L["states"], dy,
                                                                             f"l{i}_ssd_bwd")
            draw, G["e_dt_bias"][j], G["e_a_log"][j] = _ssd_small_grads(L["udt"][:, :N_HEADS], W["e_dt_bias"][j], W["e_a_log"][j],
                                                                       sq_col, cms_row, dda_col, dda_row)
            G["e_d_skip"][j] = ddp[:, :, 0, :].sum(0).reshape(N_HEADS, HEAD_P).sum(1)
            dxc = jnp.concatenate([dxs, dbs, dcs], axis=1)
            dxu, dwb, dbcb = _conv_b_bwd(L["xu"], W["e_conv_b_w"][j], W["e_conv_b_b"][j][None], dxc, f"l{i}_conv_b_bwd")
            G["e_conv_b_w"][j] = _fold_rows(dwb, CONV_B)
            G["e_conv_b_b"][j] = dbcb.sum(0)
            dudt = _padcols(draw, LANES)
            segs = L["w_in"]
            xb_l = L["xb"]
            G["e_w_in"][j] = jnp.concatenate(
                [_mm(xb_l, dal, "tn", f"l{i}_dw_in_al"), _mm(xb_l, dag, "tn", f"l{i}_dw_in_ag"), _mm(xb_l, dz, "tn", f"l{i}_dw_in_z"),
                 _mm(xb_l, dxu, "tn", f"l{i}_dw_in_xbc"), _mm(xb_l, dudt, "tn", f"l{i}_dw_in_dt")[:, :N_HEADS]], axis=1)
            dx = _mm(dal, segs[0][:, :D], "nt", f"l{i}_dx_al", add=dh1, add_scale=ALPHA)
            dx = _mm(dag, segs[0][:, D:], "nt", f"l{i}_dx_ag", add=dx)
            dx = _mm(dz, segs[1], "nt", f"l{i}_dx_z", add=dx)
            dx = _mm(dxu, segs[2], "nt", f"l{i}_dx_xbc", add=dx)
            dx = _mm(dudt, segs[3], "nt", f"l{i}_dx_dt", add=dx)
        else:
            G["o_w_out"][j] = _mm(L["sc"], dh1b, "tn", f"l{i}_dw_out")
            dsc = _mm(dh1b, W["o_w_out"][j], "nt", f"l{i}_dsc")
            dbg, dcg, dv, dwc = _conv_c_bwd(L["uo"], W["o_conv_w"][j], dsc, f"l{i}_conv_c_bwd")
            G["o_conv_w"][j] = _fold_rows(dwc, CONV_C)
            w_in = W["o_w_in"][j]
            xb_l = L["xb"]
            G["o_w_in"][j] = jnp.concatenate([_mm(xb_l, dbg, "tn", f"l{i}_dw_in_bg"), _mm(xb_l, dcg, "tn", f"l{i}_dw_in_cg"),
                                              _mm(xb_l, dv, "tn", f"l{i}_dw_in_v")], axis=1)
            dx = _mm(dbg, w_in[:, :D], "nt", f"l{i}_dx_bg", add=dh1, add_scale=ALPHA)
            dx = _mm(dcg, w_in[:, D:2 * D], "nt", f"l{i}_dx_cg", add=dx)
            dx = _mm(dv, w_in[:, 2 * D:], "nt", f"l{i}_dx_v", add=dx)
    grads = {n: jnp.stack(v) for n, v in G.items()}
    return sq, dx, grads


def _ssd_small_inputs(raw, dt_bias, a_log):
    T = raw.shape[0]
    a = -jnp.exp(a_log)
    rg = raw.reshape(T, N_GROUPS, 4)
    raw_col = jnp.pad(jnp.transpose(rg, (1, 0, 2)), ((0, 0), (0, 0), (0, LANES - 4)))
    raw_row = jnp.pad(jnp.transpose(rg, (1, 2, 0)), ((0, 0), (0, SUBLANES - 4), (0, 0)))

    def colv(v):
        return jnp.pad(v.reshape(N_GROUPS, 1, 4), ((0, 0), (0, 0), (0, LANES - 4)))

    def rowv(v):
        return jnp.pad(v.reshape(N_GROUPS, 4, 1), ((0, 0), (0, SUBLANES - 4), (0, 0)))

    return raw_col, raw_row, colv(dt_bias), rowv(dt_bias), colv(a), rowv(a)


def _ssd_small_grads(raw, dt_bias, a_log, sq_col, cms_row, dda_col, dda_row):
    T = raw.shape[0]

    def join(col, row):
        c = jnp.transpose(col[:, :, :4], (1, 0, 2)).reshape(T, N_HEADS)
        r = jnp.transpose(row[:, :4, :], (2, 0, 1)).reshape(T, N_HEADS)
        return c + r

    a = -jnp.exp(a_log)
    pre = raw + dt_bias
    dt = jax.nn.softplus(pre)
    dda = join(dda_col, dda_row)
    ddt = join(sq_col, cms_row) + a * dda
    draw = ddt * jax.nn.sigmoid(pre)
    da = jnp.sum(dt * dda, axis=0)
    return draw, jnp.sum(draw, axis=0), da * a


def kernel(x, p, e_w_in, e_conv_a_w, e_conv_a_b, e_ln_a_g, e_ln_a_b, e_conv_b_w, e_conv_b_b, e_dt_bias, e_a_log, e_d_skip, e_norm_b_g, e_w_out, o_w_in, o_conv_w, o_w_out, f_w_up, f_conv_w, f_conv_b, f_w_down, ple_w_proj, ple_w_gate, ln_g, ln_b, loss_target, m_e_w_in, m_e_conv_a_w, m_e_conv_a_b, m_e_ln_a_g, m_e_ln_a_b, m_e_conv_b_w, m_e_conv_b_b, m_e_dt_bias, m_e_a_log, m_e_d_skip, m_e_norm_b_g, m_e_w_out, m_o_w_in, m_o_conv_w, m_o_w_out, m_f_w_up, m_f_conv_w, m_f_conv_b, m_f_w_down, m_ple_w_proj, m_ple_w_gate, m_ln_g, m_ln_b, v_e_w_in, v_e_conv_a_w, v_e_conv_a_b, v_e_ln_a_g, v_e_ln_a_b, v_e_conv_b_w, v_e_conv_b_b, v_e_dt_bias, v_e_a_log, v_e_d_skip, v_e_norm_b_g, v_e_w_out, v_o_w_in, v_o_conv_w, v_o_w_out, v_f_w_up, v_f_conv_w, v_f_conv_b, v_f_w_down, v_ple_w_proj, v_ple_w_gate, v_ln_g, v_ln_b):
    args = dict(locals())
    w_shard = {n: args[n] for n in WEIGHT_ORDER}
    m_shard = {n: args["m_" + n] for n in WEIGHT_ORDER}
    v_shard = {n: args["v_" + n] for n in WEIGHT_ORDER}
    xi, yi, ci = lax.axis_index("x"), lax.axis_index("y"), lax.axis_index("c")
    chip = 2 * xi + yi

    big_shapes = [w_shard[n].shape for n, _ in BIG]
    small_shapes = [w_shard[n].shape for n, _ in SMALL_SHARDED]
    sr = _small_rows(small_shapes)
    wb = _pack_rows([w_shard[n] for n, _ in BIG], PACK_W, PACK_ROWS, BF16)
    ws = _pack_rows([w_shard[n] for n, _ in SMALL_SHARDED], LANES, sr, F32)
    WB, WS = _gather_weights(wb, ws)
    W = {n: w_shard[n] for n in SMALL_REPL}
    parts_b = [_unpack_rows(WB[k], big_shapes) for k in range(4)]
    parts_s = [_unpack_rows(WS[k], small_shapes) for k in range(4)]
    for idx, (n, ax) in enumerate(BIG):
        W[n] = jnp.concatenate([parts_b[k][idx] for k in range(4)], axis=ax)
    for idx, (n, ax) in enumerate(SMALL_SHARDED):
        W[n] = jnp.concatenate([parts_s[k][idx] for k in range(4)], axis=ax)

    sq, dx, G = _local_step(x[0], p[:, 0], loss_target[0], W)
    loss = lax.psum(0.5 * sq[0, 0] / D, ("x", "y", "c"))
    grad_x = dx[None]

    def shard_of(g, ax, k):
        n = g.shape[ax] // 4
        return lax.slice_in_dim(g, k * n, (k + 1) * n, axis=ax)

    gpk = jnp.stack([_pack_rows([shard_of(G[n], ax, k) for n, ax in BIG], PACK_W, PACK_ROWS, F32) for k in range(4)])
    hr = PACK_ROWS // 2
    keep = lax.dynamic_slice_in_dim(gpk, ci * hr, hr, axis=1)
    give = lax.dynamic_slice_in_dim(gpk, (1 - ci) * hr, hr, axis=1).astype(BF16)
    got = _swap_sibling(give, "swap_grads")
    rc = _scatter_to_chips(_add_cast(keep, got))
    rfull = _join_halves(_sum4(rc))
    gbig = dict(zip([n for n, _ in BIG], _unpack_rows(rfull, big_shapes)))

    small_all = ([shard_of(G[n], ax, k) for k in range(4) for n, ax in SMALL_SHARDED] + [G[n] for n in SMALL_REPL])
    small_all_shapes = [t.shape for t in small_all]
    mr = _small_rows(small_all_shapes)
    sg = _sum8(_allgather_small(_pack_rows(small_all, LANES, mr, F32)), mr)
    sparts = _unpack_rows(sg, small_all_shapes)
    ns = len(SMALL_SHARDED)
    gsmall = {}
    for idx, (n, ax) in enumerate(SMALL_SHARDED):
        stacked = jnp.stack([sparts[k * ns + idx] for k in range(4)])
        gsmall[n] = lax.dynamic_index_in_dim(stacked, chip, axis=0, keepdims=False)
    for idx, n in enumerate(SMALL_REPL):
        gsmall[n] = sparts[4 * ns + idx]

    grads, deltas, new_m, new_v = [], [], [], []
    for n in WEIGHT_ORDER:
        g = gbig[n] if n in gbig else gsmall[n]
        d, mn, vn = _adamw(w_shard[n], g, m_shard[n], v_shard[n], f"adamw_{n}")
        grads.append(g)
        deltas.append(d)
        new_m.append(mn)
        new_v.append(vn)
    return (loss, grad_x, *grads, *deltas, *new_m, *new_v)
```

```python
import functools
import math

import jax
import jax.numpy as jnp
from jax import lax
from jax.experimental import pallas as pl
from jax.experimental.pallas import tpu as pltpu

F32 = jnp.float32
BF16 = jnp.bfloat16
MESH = pl.DeviceIdType.MESH

DEPTH = 4
ALPHA = (2.0 * DEPTH) ** 0.25
LN_EPS = 1e-5
D = 1024
HEAD_P = 64
N_STATE = 128
N_HEADS = 16
N_GROUPS = 4
CONV_A, CONV_B, CONV_C, CONV_F = 31, 4, 3, 3
D_FF = 2816
PLE = 256
E_IN = 5136

ADAM_LR, ADAM_B1, ADAM_B2, ADAM_EPS, ADAM_WD, ADAM_STEP = 0.001, 0.9, 0.999, 1e-08, 0.01, 10

LANES = 128
SUBLANES = 8
VMEM_LIMIT = 56 * 1024 * 1024
SSD_Q = 128
CONV_R = 128
CONV_PAD = 32
ROW_T = 256
HI = lax.Precision.HIGHEST


def _cparams(sem=None):
    return pltpu.CompilerParams(dimension_semantics=sem, vmem_limit_bytes=VMEM_LIMIT)


def _sig(v):
    return jax.nn.sigmoid(v)


_DIMS = {"nn": (((1,), (0,)), ((), ())), "nt": (((1,), (1,)), ((), ())), "tn": (((0,), (0,)), ((), ()))}


class V:
    def __init__(self, arr, lead=(), r0=0, c0=0, rows=None, cols=None):
        self.arr, self.lead, self.r0, self.c0 = arr, tuple(lead), r0, c0
        R, C = arr.shape[-2:]
        self.rows = R - r0 if rows is None else rows
        self.cols = C - c0 if cols is None else cols

    def spec(self, br, bc, fn):
        assert self.r0 % br == 0 and self.c0 % bc == 0, (self.r0, self.c0, br, bc)
        ro, co, lead = self.r0 // br, self.c0 // bc, self.lead

        def index(i, j, k):
            r, c = fn(i, j, k)
            return lead + (r + ro, c + co)

        return pl.BlockSpec((None,) * len(lead) + (br, bc), index)


def _v(t):
    return t if isinstance(t, V) else V(t)


def _tile(n, offs, cands):
    for c in cands:
        if n % c == 0 and all(o % c == 0 for o in offs):
            return c
    raise ValueError((n, offs))


_TILES = (1024, 1408, 512, 256, 128)


def _mm(a, b, mode, name, out_dtype=F32, add=None, add_scale=1.0, dst=None):
    a, b = _v(a), _v(b)
    add = _v(add) if add is not None else None
    if mode == "nn":
        M, K, K2, N = a.rows, a.cols, b.rows, b.cols
        am, ak, bk, bn = a.r0, a.c0, b.r0, b.c0
    elif mode == "nt":
        M, K, N, K2 = a.rows, a.cols, b.rows, b.cols
        am, ak, bn, bk = a.r0, a.c0, b.r0, b.c0
    else:
        K, M, K2, N = a.rows, a.cols, b.rows, b.cols
        ak, am, bk, bn = a.r0, a.c0, b.r0, b.c0
    assert K == K2, (name, mode, M, K, K2, N)
    if dst is None:
        buf, full_shape, o_lead, o_r0, o_c0 = None, (M, N), (), 0, 0
    else:
        buf, full_shape, o_lead, o_r0, o_c0 = dst
    tm = _tile(M, [am, o_r0] + ([add.r0] if add else []), _TILES)
    tn = _tile(N, [bn, o_c0] + ([add.c0] if add else []), _TILES)
    tk = _tile(K, [ak, bk], _TILES)
    nk = K // tk
    has_add, has_buf = add is not None, buf is not None

    def body(*refs):
        a_ref, b_ref = refs[0], refs[1]
        add_ref = refs[2] if has_add else None
        o_ref = refs[2 + has_add + has_buf]

        def finish(r):
            if has_add:
                r = r + add_scale * add_ref[...].astype(F32)
            o_ref[...] = r.astype(o_ref.dtype)

        part = lax.dot_general(a_ref[...].astype(BF16), b_ref[...].astype(BF16), _DIMS[mode], preferred_element_type=F32)
        if nk == 1:
            finish(part)
        else:
            acc_ref = refs[-1]
            k = pl.program_id(2)

            @pl.when(k == 0)
            def _():
                acc_ref[...] = part

            @pl.when(jnp.logical_and(k > 0, k < nk - 1))
            def _():
                acc_ref[...] += part

            @pl.when(k == nk - 1)
            def _():
                finish(acc_ref[...] + part)

    if mode == "tn":
        a_spec = a.spec(tk, tm, lambda i, j, k: (k, i))
    else:
        a_spec = a.spec(tm, tk, lambda i, j, k: (i, k))
    if mode == "nt":
        b_spec = b.spec(tn, tk, lambda i, j, k: (j, k))
    else:
        b_spec = b.spec(tk, tn, lambda i, j, k: (k, j))
    in_specs, args = [a_spec, b_spec], [a.arr, b.arr]
    if has_add:
        in_specs.append(add.spec(tm, tn, lambda i, j, k: (i, j)))
        args.append(add.arr)
    aliases = {}
    if has_buf:
        aliases = {len(args): 0}
        in_specs.append(pl.BlockSpec(memory_space=pl.ANY))
        args.append(buf)
        out_dtype = buf.dtype
    o_view = V(jax.ShapeDtypeStruct(full_shape, out_dtype), o_lead, o_r0, o_c0, M, N)
    return pl.pallas_call(
        body, name=name, grid=(M // tm, N // tn, nk), in_specs=in_specs, out_specs=o_view.spec(tm, tn, lambda i, j, k: (i, j)),
        out_shape=jax.ShapeDtypeStruct(full_shape, out_dtype), input_output_aliases=aliases,
        scratch_shapes=[pltpu.VMEM((tm, tn), F32)] if nk > 1 else [],
        compiler_params=_cparams(("parallel", "parallel", "arbitrary")))(*args)


def _rows(T, width=D):
    return pl.BlockSpec((ROW_T, width), lambda i: (i, 0))


def _vec(width=D):
    return pl.BlockSpec((1, width), lambda i: (0, 0))


def _ln_stats(h):
    mu = jnp.mean(h, axis=-1, keepdims=True)
    hc = h - mu
    var = jnp.mean(hc * hc, axis=-1, keepdims=True)
    rstd = lax.rsqrt(var + LN_EPS)
    return hc * rstd, rstd


def _res_ln_fwd(x, adds, ple, g, b, name):
    T = x.shape[0]
    n_add = len(adds)
    has_ple = ple is not None

    def body(*refs):
        x_ref = refs[0]
        add_refs = refs[1:1 + n_add]
        pos = 1 + n_add
        if has_ple:
            pp_ref, gl_ref = refs[pos], refs[pos + 1]
            pos += 2
        g_ref, b_ref, h_ref, y_ref, yb_ref = refs[pos:pos + 5]
        h = ALPHA * x_ref[...]
        for r in add_refs:
            h = h + r[...]
        if has_ple:
            h = h + pp_ref[...] * _sig(gl_ref[...])
        xhat, _ = _ln_stats(h)
        y = xhat * g_ref[...] + b_ref[...]
        h_ref[...] = h
        y_ref[...] = y
        yb_ref[...] = y.astype(BF16)

    n_in = 1 + n_add + (2 if has_ple else 0)
    args = (x,) + tuple(adds) + (tuple(ple) if has_ple else ()) + (g, b)
    return pl.pallas_call(
        body, name=name, grid=(T // ROW_T,), in_specs=[_rows(T)] * n_in + [_vec(), _vec()],
        out_specs=(_rows(T), _rows(T), _rows(T)),
        out_shape=(jax.ShapeDtypeStruct((T, D), F32), jax.ShapeDtypeStruct((T, D), F32), jax.ShapeDtypeStruct((T, D), BF16)),
        compiler_params=_cparams(("parallel",)))(*args)


def _res_ln_bwd(dy, h, g, ple, name):
    T = dy.shape[0]
    has_ple = ple is not None

    def body(*refs):
        if has_ple:
            dy_ref, h_ref, g_ref, pp_ref, gl_ref, dh_ref, dhb_ref, dg_ref, db_ref, dpp_ref, dgl_ref = refs
        else:
            dy_ref, h_ref, g_ref, dh_ref, dhb_ref, dg_ref, db_ref = refs
        i = pl.program_id(0)

        @pl.when(i == 0)
        def _():
            dg_ref[...] = jnp.zeros_like(dg_ref)
            db_ref[...] = jnp.zeros_like(db_ref)

        dyv = dy_ref[...]
        xhat, rstd = _ln_stats(h_ref[...])
        dg_ref[...] += jnp.sum(dyv * xhat, axis=0, keepdims=True)
        db_ref[...] += jnp.sum(dyv, axis=0, keepdims=True)
        dxh = dyv * g_ref[...]
        dh = rstd * (dxh - jnp.mean(dxh, axis=-1, keepdims=True) - xhat * jnp.mean(dxh * xhat, axis=-1, keepdims=True))
        dh_ref[...] = dh
        dhb_ref[...] = dh.astype(BF16)
        if has_ple:
            s = _sig(gl_ref[...])
            dpp_ref[...] = (dh * s).astype(BF16)
            dgl_ref[...] = (dh * pp_ref[...] * s * (1.0 - s)).astype(BF16)

    args = (dy, h, g) + (tuple(ple) if has_ple else ())
    in_specs = [_rows(T), _rows(T), _vec()] + ([_rows(T), _rows(T)] if has_ple else [])
    out_specs = [_rows(T), _rows(T), _vec(), _vec()] + ([_rows(T), _rows(T)] if has_ple else [])
    out_shape = [jax.ShapeDtypeStruct((T, D), F32), jax.ShapeDtypeStruct((T, D), BF16),
                 jax.ShapeDtypeStruct((1, D), F32), jax.ShapeDtypeStruct((1, D), F32)]
    if has_ple:
        out_shape += [jax.ShapeDtypeStruct((T, D), BF16), jax.ShapeDtypeStruct((T, D), BF16)]
    return pl.pallas_call(
        body, name=name, grid=(T // ROW_T,), in_specs=in_specs, out_specs=tuple(out_specs), out_shape=tuple(out_shape),
        compiler_params=_cparams(("arbitrary",)))(*args)


def _ln_silu_fwd(ac, g, b, name):
    T = ac.shape[0]

    def body(a_ref, g_ref, b_ref, o_ref):
        xhat, _ = _ln_stats(a_ref[...])
        ln = xhat * g_ref[...] + b_ref[...]
        o_ref[...] = (ln * _sig(ln)).astype(BF16)

    return pl.pallas_call(
        body, name=name, grid=(T // ROW_T,), in_specs=[_rows(T), _vec(), _vec()], out_specs=_rows(T),
        out_shape=jax.ShapeDtypeStruct((T, D), BF16), compiler_params=_cparams(("parallel",)))(ac, g, b)


def _ln_silu_bwd(ac, dya, g, b, name):
    T = ac.shape[0]

    def body(a_ref, d_ref, g_ref, b_ref, da_ref, dg_ref, db_ref):
        i = pl.program_id(0)

        @pl.when(i == 0)
        def _():
            dg_ref[...] = jnp.zeros_like(dg_ref)
            db_ref[...] = jnp.zeros_like(db_ref)

        xhat, rstd = _ln_stats(a_ref[...])
        ln = xhat * g_ref[...] + b_ref[...]
        s = _sig(ln)
        dln = d_ref[...] * s * (1.0 + ln * (1.0 - s))
        dg_ref[...] += jnp.sum(dln * xhat, axis=0, keepdims=True)
        db_ref[...] += jnp.sum(dln, axis=0, keepdims=True)
        dxh = dln * g_ref[...]
        da_ref[...] = rstd * (dxh - jnp.mean(dxh, axis=-1, keepdims=True)
                              - xhat * jnp.mean(dxh * xhat, axis=-1, keepdims=True))

    return pl.pallas_call(
        body, name=name, grid=(T // ROW_T,), in_specs=[_rows(T), _rows(T), _vec(), _vec()],
        out_specs=(_rows(T), _vec(), _vec()),
        out_shape=(jax.ShapeDtypeStruct((T, D), F32), jax.ShapeDtypeStruct((1, D), F32), jax.ShapeDtypeStruct((1, D), F32)),
        compiler_params=_cparams(("arbitrary",)))(ac, dya, g, b)


def _gate_rms_fwd(y, z, g, name):
    T = y.shape[0]

    def body(y_ref, z_ref, g_ref, o_ref):
        zv = z_ref[...]
        yg = y_ref[...] * (zv * _sig(zv))
        r = lax.rsqrt(jnp.mean(yg * yg, axis=-1, keepdims=True) + LN_EPS)
        o_ref[...] = (yg * r * g_ref[...]).astype(BF16)

    return pl.pallas_call(
        body, name=name, grid=(T // ROW_T,), in_specs=[_rows(T), _rows(T), _vec()], out_specs=_rows(T),
        out_shape=jax.ShapeDtypeStruct((T, D), BF16), compiler_params=_cparams(("parallel",)))(y, z, g)


def _gate_rms_bwd(y, z, dout, g, name):
    T = y.shape[0]

    def body(y_ref, z_ref, d_ref, g_ref, dy_ref, dz_ref, dg_ref):
        i = pl.program_id(0)

        @pl.when(i == 0)
        def _():
            dg_ref[...] = jnp.zeros_like(dg_ref)

        yv, zv, dv = y_ref[...], z_ref[...], d_ref[...]
        s = _sig(zv)
        sz = zv * s
        yg = yv * sz
        r = lax.rsqrt(jnp.mean(yg * yg, axis=-1, keepdims=True) + LN_EPS)
        dg_ref[...] += jnp.sum(dv * yg * r, axis=0, keepdims=True)
        dn = dv * g_ref[...]
        dyg = r * dn - yg * (r * r * r) * jnp.mean(dn * yg, axis=-1, keepdims=True)
        dy_ref[...] = dyg * sz
        dz_ref[...] = dyg * yv * s * (1.0 + zv * (1.0 - s))

    return pl.pallas_call(
        body, name=name, grid=(T // ROW_T,), in_specs=[_rows(T), _rows(T), _rows(T), _vec()],
        out_specs=(_rows(T), _rows(T), _vec()),
        out_shape=(jax.ShapeDtypeStruct((T, D), F32), jax.ShapeDtypeStruct((T, D), F32), jax.ShapeDtypeStruct((1, D), F32)),
        compiler_params=_cparams(("arbitrary",)))(y, z, dout, g)


def _loss_head(y, target, name):
    T = y.shape[0]

    def body(y_ref, t_ref, s_ref, d_ref):
        i = pl.program_id(0)

        @pl.when(i == 0)
        def _():
            s_ref[...] = jnp.zeros_like(s_ref)

        err = y_ref[...] - t_ref[...]
        s_ref[...] += jnp.sum(jnp.sum(err * err, axis=1, keepdims=True), axis=0, keepdims=True)
        d_ref[...] = err * (1.0 / D)

    return pl.pallas_call(
        body, name=name, grid=(T // ROW_T,), in_specs=[_rows(T), _rows(T)],
        out_specs=(pl.BlockSpec((SUBLANES, LANES), lambda i: (0, 0)), _rows(T)),
        out_shape=(jax.ShapeDtypeStruct((SUBLANES, LANES), F32), jax.ShapeDtypeStruct((T, D), F32)),
        compiler_params=_cparams(("arbitrary",)))(y, target)


def _taps_fwd(pad_ref, w_ref, K, base):
    off = CONV_PAD - (K - 1)
    acc = w_ref[0:1, :] * pad_ref[pl.ds(base + off, CONV_R), :]
    for k in range(1, K):
        acc = acc + w_ref[k:k + 1, :] * pad_ref[pl.ds(base + off + k, CONV_R), :]
    return acc


def _taps_bwd(padd_ref, w_ref, K, base):
    acc = w_ref[0:1, :] * padd_ref[pl.ds(base + (K - 1), CONV_R), :]
    for k in range(1, K):
        acc = acc + w_ref[k:k + 1, :] * padd_ref[pl.ds(base + (K - 1) - k, CONV_R), :]
    return acc


def _fold8(v):
    return v.reshape(CONV_R // SUBLANES, SUBLANES, v.shape[-1]).sum(0)


def _wgrad_acc(dw_ref, pad_ref, d, K, base):
    off = CONV_PAD - (K - 1)
    for k in range(K):
        dw_ref[k * SUBLANES:(k + 1) * SUBLANES, :] += _fold8(d * pad_ref[pl.ds(base + off + k, CONV_R), :])


def _loop_rows(T, fn):
    def step(r, carry):
        fn(pl.multiple_of(r * CONV_R, CONV_R))
        return carry
    lax.fori_loop(0, T // CONV_R, step, 0)


def _col(T, off_blocks=0, rows=None):
    return pl.BlockSpec((T if rows is None else rows, LANES), lambda j: (0, j + off_blocks))


def _conv_call(body, name, T, n_tiles, in_specs, out_specs, out_shape, n_pad, n_padd=0):
    scratch = [pltpu.VMEM((T + CONV_PAD, LANES), F32)] * (n_pad + n_padd)
    return pl.pallas_call(body, name=name, grid=(n_tiles,), in_specs=in_specs, out_specs=out_specs, out_shape=out_shape,
                          scratch_shapes=scratch, compiler_params=_cparams(("parallel",)))


def _zero_head(ref):
    ref[0:CONV_PAD, :] = jnp.zeros((CONV_PAD, LANES), F32)


def _zero_tail(ref, T):
    ref[T:T + CONV_PAD, :] = jnp.zeros((CONV_PAD, LANES), F32)


def _sds(shape, dtype=F32):
    return jax.ShapeDtypeStruct(shape, dtype)


def _conv_a_fwd(ua, w, b, name):
    T = ua.shape[0]
    K, nt = CONV_A, D // LANES

    def body(al_ref, ag_ref, w_ref, b_ref, o_ref, pad_ref):
        _zero_head(pad_ref)

        def pre(base):
            pad_ref[pl.ds(base + CONV_PAD, CONV_R), :] = al_ref[pl.ds(base, CONV_R), :] * _sig(ag_ref[pl.ds(base, CONV_R), :])
        _loop_rows(T, pre)

        def main(base):
            o_ref[pl.ds(base, CONV_R), :] = _taps_fwd(pad_ref, w_ref, K, base) + b_ref[...]
        _loop_rows(T, main)

    return _conv_call(body, name, T, nt, [_col(T), _col(T, nt), _col(T, rows=K), _col(T, rows=1)], _col(T),
                      _sds((T, D)), 1)(ua, ua, w, b)


def _conv_a_bwd(ua, w, dac, name):
    T = ua.shape[0]
    K, nt = CONV_A, D // LANES

    def body(al_ref, ag_ref, w_ref, d_ref, dal_ref, dag_ref, dw_ref, db_ref, pad_ref, padd_ref):
        _zero_head(pad_ref)
        _zero_tail(padd_ref, T)
        dw_ref[...] = jnp.zeros_like(dw_ref)
        db_ref[...] = jnp.zeros_like(db_ref)

        def pre(base):
            rows = pl.ds(base, CONV_R)
            pad_ref[pl.ds(base + CONV_PAD, CONV_R), :] = al_ref[rows, :] * _sig(ag_ref[rows, :])
            padd_ref[rows, :] = d_ref[rows, :]
        _loop_rows(T, pre)

        def main(base):
            rows = pl.ds(base, CONV_R)
            d = d_ref[rows, :]
            _wgrad_acc(dw_ref, pad_ref, d, K, base)
            db_ref[...] += _fold8(d)
            da = _taps_bwd(padd_ref, w_ref, K, base)
            al, s = al_ref[rows, :], _sig(ag_ref[rows, :])
            dal_ref[rows, :] = da * s
            dag_ref[rows, :] = da * al * s * (1.0 - s)
        _loop_rows(T, main)

    return _conv_call(body, name, T, nt, [_col(T), _col(T, nt), _col(T, rows=K), _col(T)],
                      (_col(T), _col(T), _col(T, rows=K * SUBLANES), _col(T, rows=SUBLANES)),
                      (_sds((T, D)), _sds((T, D)), _sds((K * SUBLANES, D)), _sds((SUBLANES, D))), 1, 1)(ua, ua, w, dac)


def _conv_b_fwd(xu, w, b, name):
    T, C = xu.shape
    K, nt = CONV_B, C // LANES

    def body(x_ref, w_ref, b_ref, o_ref, pad_ref):
        _zero_head(pad_ref)
        pad_ref[CONV_PAD:CONV_PAD + T, :] = x_ref[...]

        def main(base):
            hc = _taps_fwd(pad_ref, w_ref, K, base) + b_ref[...]
            o_ref[pl.ds(base, CONV_R), :] = hc * _sig(hc)
        _loop_rows(T, main)

    return _conv_call(body, name, T, nt, [_col(T), _col(T, rows=K), _col(T, rows=1)], _col(T), _sds((T, C)), 1)(xu, w, b)


def _conv_b_bwd(xu, w, b, dxs, dbs, dcs, name):
    T, C = xu.shape
    K, nt = CONV_B, C // LANES
    nx, nb = dxs.shape[1] // LANES, dbs.shape[1] // LANES

    def body(x_ref, w_ref, b_ref, d1_ref, d2_ref, d3_ref, dx_ref, dw_ref, db_ref, pad_ref, padd_ref):
        j = pl.program_id(0)
        _zero_head(pad_ref)
        _zero_tail(padd_ref, T)
        dw_ref[...] = jnp.zeros_like(dw_ref)
        db_ref[...] = jnp.zeros_like(db_ref)
        pad_ref[CONV_PAD:CONV_PAD + T, :] = x_ref[...]

        def pre(base):
            rows = pl.ds(base, CONV_R)
            hc = _taps_fwd(pad_ref, w_ref, K, base) + b_ref[...]
            s = _sig(hc)
            d = jnp.where(j < nx, d1_ref[rows, :], jnp.where(j < nx + nb, d2_ref[rows, :], d3_ref[rows, :]))
            padd_ref[rows, :] = d * s * (1.0 + hc * (1.0 - s))
        _loop_rows(T, pre)

        def main(base):
            d = padd_ref[pl.ds(base, CONV_R), :]
            _wgrad_acc(dw_ref, pad_ref, d, K, base)
            db_ref[...] += _fold8(d)
            dx_ref[pl.ds(base, CONV_R), :] = _taps_bwd(padd_ref, w_ref, K, base)
        _loop_rows(T, main)

    def piece(lo, n):
        return pl.BlockSpec((T, LANES), lambda j: (0, jnp.clip(j - lo, 0, n - 1)))

    return _conv_call(body, name, T, nt,
                      [_col(T), _col(T, rows=K), _col(T, rows=1), piece(0, nx), piece(nx, nb), piece(nx + nb, nt - nx - nb)],
                      (_col(T), _col(T, rows=K * SUBLANES), _col(T, rows=SUBLANES)),
                      (_sds((T, C)), _sds((K * SUBLANES, C)), _sds((SUBLANES, C))), 1, 1)(xu, w, b, dxs, dbs, dcs)


def _conv_c_fwd(uo, w, name):
    T = uo.shape[0]
    K, nt = CONV_C, D // LANES

    def body(bg_ref, cg_ref, v_ref, w_ref, o_ref, pad_ref):
        _zero_head(pad_ref)
        pad_ref[CONV_PAD:CONV_PAD + T, :] = cg_ref[...] * v_ref[...]

        def main(base):
            rows = pl.ds(base, CONV_R)
            o_ref[rows, :] = (bg_ref[rows, :] * _taps_fwd(pad_ref, w_ref, K, base)).astype(BF16)
        _loop_rows(T, main)

    return _conv_call(body, name, T, nt, [_col(T), _col(T, nt), _col(T, 2 * nt), _col(T, rows=K)], _col(T),
                      _sds((T, D), BF16), 1)(uo, uo, uo, w)


def _conv_c_bwd(uo, w, dsc, name):
    T = uo.shape[0]
    K, nt = CONV_C, D // LANES

    def body(bg_ref, cg_ref, v_ref, w_ref, d_ref, dbg_ref, dcg_ref, dv_ref, dw_ref, pad_ref, padd_ref):
        _zero_head(pad_ref)
        _zero_tail(padd_ref, T)
        dw_ref[...] = jnp.zeros_like(dw_ref)
        pad_ref[CONV_PAD:CONV_PAD + T, :] = cg_ref[...] * v_ref[...]

        def pre(base):
            rows = pl.ds(base, CONV_R)
            d = d_ref[rows, :]
            dbg_ref[rows, :] = (d * _taps_fwd(pad_ref, w_ref, K, base)).astype(BF16)
            padd_ref[rows, :] = d * bg_ref[rows, :]
        _loop_rows(T, pre)

        def main(base):
            rows = pl.ds(base, CONV_R)
            _wgrad_acc(dw_ref, pad_ref, padd_ref[rows, :], K, base)
            dq = _taps_bwd(padd_ref, w_ref, K, base)
            dcg_ref[rows, :] = (dq * v_ref[rows, :]).astype(BF16)
            dv_ref[rows, :] = (dq * cg_ref[rows, :]).astype(BF16)
        _loop_rows(T, main)

    return _conv_call(body, name, T, nt, [_col(T), _col(T, nt), _col(T, 2 * nt), _col(T, rows=K), _col(T)],
                      (_col(T), _col(T), _col(T), _col(T, rows=K * SUBLANES)),
                      (_sds((T, D), BF16), _sds((T, D), BF16), _sds((T, D), BF16), _sds((K * SUBLANES, D))), 1, 1)(uo, uo, uo, w, dsc)


def _conv_f_fwd(up, w, b, name):
    T = up.shape[0]
    K, nt = CONV_F, D_FF // LANES

    def body(u1_ref, u2_ref, w1_ref, w2_ref, b1_ref, b2_ref, o_ref, pad1_ref, pad2_ref):
        _zero_head(pad1_ref)
        _zero_head(pad2_ref)
        pad1_ref[CONV_PAD:CONV_PAD + T, :] = u1_ref[...]
        pad2_ref[CONV_PAD:CONV_PAD + T, :] = u2_ref[...]

        def main(base):
            h1 = _taps_fwd(pad1_ref, w1_ref, K, base) + b1_ref[...]
            h2 = _taps_fwd(pad2_ref, w2_ref, K, base) + b2_ref[...]
            o_ref[pl.ds(base, CONV_R), :] = (h1 * _sig(h1) * h2).astype(BF16)
        _loop_rows(T, main)

    return _conv_call(body, name, T, nt,
                      [_col(T), _col(T, nt), _col(T, rows=K), _col(T, nt, rows=K), _col(T, rows=1), _col(T, nt, rows=1)],
                      _col(T), _sds((T, D_FF), BF16), 2)(up, up, w, w, b, b)


def _conv_f_bwd(up, w, b, dact, name):
    T = up.shape[0]
    K, nt = CONV_F, D_FF // LANES

    def body(u1_ref, u2_ref, w1_ref, w2_ref, b1_ref, b2_ref, d_ref, du1_ref, du2_ref, dw1_ref, dw2_ref, db1_ref, db2_ref,
             pad1_ref, pad2_ref, padd1_ref, padd2_ref):
        _zero_head(pad1_ref)
        _zero_head(pad2_ref)
        _zero_tail(padd1_ref, T)
        _zero_tail(padd2_ref, T)
        for r in (dw1_ref, dw2_ref, db1_ref, db2_ref):
            r[...] = jnp.zeros_like(r)
        pad1_ref[CONV_PAD:CONV_PAD + T, :] = u1_ref[...]
        pad2_ref[CONV_PAD:CONV_PAD + T, :] = u2_ref[...]

        def pre(base):
            rows = pl.ds(base, CONV_R)
            h1 = _taps_fwd(pad1_ref, w1_ref, K, base) + b1_ref[...]
            h2 = _taps_fwd(pad2_ref, w2_ref, K, base) + b2_ref[...]
            s = _sig(h1)
            d = d_ref[rows, :]
            padd1_ref[rows, :] = d * h2 * s * (1.0 + h1 * (1.0 - s))
            padd2_ref[rows, :] = d * h1 * s
        _loop_rows(T, pre)

        def main(base):
            rows = pl.ds(base, CONV_R)
            d1, d2 = padd1_ref[rows, :], padd2_ref[rows, :]
            _wgrad_acc(dw1_ref, pad1_ref, d1, K, base)
            _wgrad_acc(dw2_ref, pad2_ref, d2, K, base)
            db1_ref[...] += _fold8(d1)
            db2_ref[...] += _fold8(d2)
            du1_ref[rows, :] = _taps_bwd(padd1_ref, w1_ref, K, base).astype(BF16)
            du2_ref[rows, :] = _taps_bwd(padd2_ref, w2_ref, K, base).astype(BF16)
        _loop_rows(T, main)

    wrow, brow = _col(T, rows=K * SUBLANES), _col(T, rows=SUBLANES)
    return _conv_call(body, name, T, nt,
                      [_col(T), _col(T, nt), _col(T, rows=K), _col(T, nt, rows=K), _col(T, rows=1), _col(T, nt, rows=1), _col(T)],
                      (_col(T), _col(T), wrow, wrow, brow, brow),
                      (_sds((T, D_FF), BF16), _sds((T, D_FF), BF16), _sds((K * SUBLANES, D_FF)), _sds((K * SUBLANES, D_FF)),
                       _sds((SUBLANES, D_FF)), _sds((SUBLANES, D_FF))), 2, 2)(up, up, w, w, b, b, dact)


def _dot(a, b, dims="nn"):
    return lax.dot_general(a.astype(BF16), b.astype(BF16), _DIMS[dims], preferred_element_type=F32)


def _ssd_small(xcr_ref, xrr_ref, bc_ref, br_ref, ac_ref, ar_ref):
    Q = SSD_Q
    li = lax.broadcasted_iota(jnp.int32, (Q, Q), 0)
    si = lax.broadcasted_iota(jnp.int32, (Q, Q), 1)
    tril = li >= si
    dtc = jax.nn.softplus(xcr_ref[...] + bc_ref[...])
    dtr = jax.nn.softplus(xrr_ref[...] + br_ref[...])
    cumc = jnp.dot(tril.astype(F32), dtc * ac_ref[...], precision=HI, preferred_element_type=F32)
    cumr = jnp.dot(dtr * ar_ref[...], (li <= si).astype(F32), precision=HI, preferred_element_type=F32)
    return tril, dtc, dtr, cumc, cumr


def _ssd_specs(nc, rev):
    Q = SSD_Q
    cc = (lambda c: nc - 1 - c) if rev else (lambda c: c)
    x_spec = pl.BlockSpec((Q, 2 * LANES), lambda g, c: (cc(c), g))
    b_spec = pl.BlockSpec((Q, LANES), lambda g, c: (cc(c), 8 + g))
    c_spec = pl.BlockSpec((Q, LANES), lambda g, c: (cc(c), 12 + g))
    colm = pl.BlockSpec((None, Q, LANES), lambda g, c: (g, cc(c), 0))
    rowm = pl.BlockSpec((None, SUBLANES, Q), lambda g, c: (g, 0, cc(c)))
    colv = pl.BlockSpec((None, 1, LANES), lambda g, c: (g, 0, 0))
    rowv = pl.BlockSpec((None, SUBLANES, 1), lambda g, c: (g, 0, 0))
    st_spec = pl.BlockSpec((None, None, 2 * LANES, N_STATE), lambda g, c: (cc(c), g, 0, 0))
    return x_spec, b_spec, c_spec, colm, rowm, colv, rowv, st_spec


def _ssd_fwd(xc, raw_col, raw_row, bias_col, bias_row, a_col, a_row, dskip, name):
    T = xc.shape[0]
    Q = SSD_Q
    nc = T // Q
    x_spec, b_spec, c_spec, colm, rowm, colv, rowv, st_spec = _ssd_specs(nc, False)

    def body(dk_ref, x_ref, b_ref, c_ref, xcr_ref, xrr_ref, bc_ref, br_ref, ac_ref, ar_ref, y_ref, st_ref, h_ref):
        g = pl.program_id(0)

        @pl.when(pl.program_id(1) == 0)
        def _():
            h_ref[...] = jnp.zeros_like(h_ref)

        tril, dtc, dtr, cumc, cumr = _ssd_small(xcr_ref, xrr_ref, bc_ref, br_ref, ac_ref, ar_ref)
        Bm, Cm = b_ref[...], c_ref[...]
        S = _dot(Cm, Bm, "nt")
        lo = lax.broadcasted_iota(jnp.int32, (Q, LANES), 1) < HEAD_P
        rlo = lax.broadcasted_iota(jnp.int32, (LANES, N_STATE), 0) < HEAD_P
        st_ref[...] = h_ref[...]
        clast = cumc[Q - 1:Q, :]
        for pr in range(2):
            cols = slice(pr * LANES, (pr + 1) * LANES)
            xp = x_ref[:, cols]
            yd = jnp.zeros((Q, LANES), F32)
            for q in range(2):
                hh = 2 * pr + q
                seg = cumc[:, hh:hh + 1] - cumr[hh:hh + 1, :]
                lm = jnp.where(tril, jnp.exp(jnp.where(tril, seg, 0.0)), 0.0)
                w = S * lm * dtr[hh:hh + 1, :]
                xm = jnp.where(lo if q == 0 else jnp.logical_not(lo), xp, 0.0)
                yd = yd + _dot(w, xm)
            h0, h1 = 2 * pr, 2 * pr + 1
            c0, c1 = cumc[:, h0:h0 + 1], cumc[:, h1:h1 + 1]
            e_pair = jnp.where(lo, jnp.exp(c0), jnp.exp(c1))
            hp = h_ref[cols, :]
            ch = _dot(Cm, hp, "nt")
            dsk = jnp.where(lo, dk_ref[4 * g + h0], dk_ref[4 * g + h1])
            y_ref[:, cols] = yd + e_pair * ch + dsk * xp
            cl0, cl1 = clast[:, h0:h0 + 1], clast[:, h1:h1 + 1]
            sdec = jnp.where(lo, jnp.exp(cl0 - c0) * dtc[:, h0:h0 + 1], jnp.exp(cl1 - c1) * dtc[:, h1:h1 + 1])
            decrow = jnp.where(rlo, jnp.exp(cl0), jnp.exp(cl1))
            h_ref[cols, :] = hp * decrow + _dot(xp * sdec, Bm, "tn")

    smem = pl.BlockSpec(memory_space=pltpu.SMEM)
    return pl.pallas_call(
        body, name=name, grid=(N_GROUPS, nc),
        in_specs=[smem, x_spec, b_spec, c_spec, colm, rowm, colv, rowv, colv, rowv],
        out_specs=(x_spec, st_spec),
        out_shape=(_sds((T, D)), _sds((nc, N_GROUPS, 2 * LANES, N_STATE))),
        scratch_shapes=[pltpu.VMEM((2 * LANES, N_STATE), F32)],
        compiler_params=_cparams(("parallel", "arbitrary")))(dskip, xc, xc, xc, raw_col, raw_row, bias_col, bias_row, a_col, a_row)


def _ssd_bwd(xc, raw_col, raw_row, bias_col, bias_row, a_col, a_row, dskip, states, dy, name):
    T = xc.shape[0]
    Q = SSD_Q
    nc = T // Q
    x_spec, b_spec, c_spec, colm, rowm, colv, rowv, st_spec = _ssd_specs(nc, True)
    bo_spec = pl.BlockSpec((Q, LANES), lambda g, c: (nc - 1 - c, g))
    dd_spec = pl.BlockSpec((None, None, SUBLANES, 2 * LANES), lambda g, c: (nc - 1 - c, g, 0, 0))

    def body(dk_ref, x_ref, b_ref, c_ref, xcr_ref, xrr_ref, bc_ref, br_ref, ac_ref, ar_ref, st_ref, dy_ref,
             dx_ref, db_ref, dc_ref, sq_ref, cms_ref, ddac_ref, ddar_ref, dd_ref, dh_ref):
        g = pl.program_id(0)

        @pl.when(pl.program_id(1) == 0)
        def _():
            dh_ref[...] = jnp.zeros_like(dh_ref)

        tril, dtc, dtr, cumc, cumr = _ssd_small(xcr_ref, xrr_ref, bc_ref, br_ref, ac_ref, ar_ref)
        Bm, Cm = b_ref[...], c_ref[...]
        S = _dot(Cm, Bm, "nt")
        lane = lax.broadcasted_iota(jnp.int32, (Q, LANES), 1)
        sub = lax.broadcasted_iota(jnp.int32, (SUBLANES, Q), 0)
        rowi = lax.broadcasted_iota(jnp.int32, (Q, LANES), 0)
        lo = lane < HEAD_P
        rlo = lax.broadcasted_iota(jnp.int32, (LANES, N_STATE), 0) < HEAD_P
        clast = cumc[Q - 1:Q, :]
        ds_g = jnp.zeros((Q, Q), F32)
        dcm = jnp.zeros((Q, N_STATE), F32)
        dbm = jnp.zeros((Q, N_STATE), F32)
        dcum_col = jnp.zeros((Q, LANES), F32)
        dcum_row = jnp.zeros((SUBLANES, Q), F32)
        sq_col = jnp.zeros((Q, LANES), F32)
        cms_row = jnp.zeros((SUBLANES, Q), F32)
        for pr in range(2):
            cols = slice(pr * LANES, (pr + 1) * LANES)
            xp, dyp = x_ref[:, cols], dy_ref[:, cols]
            hin, dhp = st_ref[cols, :], dh_ref[cols, :]
            h0, h1 = 2 * pr, 2 * pr + 1
            c0, c1 = cumc[:, h0:h0 + 1], cumc[:, h1:h1 + 1]
            cl0, cl1 = clast[:, h0:h0 + 1], clast[:, h1:h1 + 1]
            e_pair = jnp.where(lo, jnp.exp(c0), jnp.exp(c1))
            edec = jnp.where(lo, jnp.exp(cl0 - c0), jnp.exp(cl1 - c1))
            dt_pair = jnp.where(lo, dtc[:, h0:h0 + 1], dtc[:, h1:h1 + 1])
            sdec = edec * dt_pair
            ch = _dot(Cm, hin, "nt")
            xb = _dot(Bm, dhp, "nt")
            dye = dyp * e_pair
            t1 = dye * ch
            t2 = xp * xb * edec
            hh_prod = dhp * hin
            dsk = jnp.where(lo, dk_ref[4 * g + h0], dk_ref[4 * g + h1])
            dxp = sdec * xb + dsk * dyp
            for q in range(2):
                hh = 2 * pr + q
                mine = lo if q == 0 else jnp.logical_not(lo)
                seg = cumc[:, hh:hh + 1] - cumr[hh:hh + 1, :]
                lm = jnp.where(tril, jnp.exp(jnp.where(tril, seg, 0.0)), 0.0)
                dtrow = dtr[hh:hh + 1, :]
                w = S * lm * dtrow
                dym = jnp.where(mine, dyp, 0.0)
                gl = _dot(dym, xp, "nt") * lm
                ds_g = ds_g + gl * dtrow
                ms = gl * S
                m = ms * dtrow
                dxp = dxp + _dot(w, dym, "tn")
                cms_row = jnp.where(sub == hh, jnp.sum(ms, axis=0, keepdims=True), cms_row)
                dcum_row = jnp.where(sub == hh, -jnp.sum(m, axis=0, keepdims=True), dcum_row)
                t1h = jnp.sum(jnp.where(mine, t1, 0.0), axis=1, keepdims=True)
                sqh = jnp.sum(jnp.where(mine, t2, 0.0), axis=1, keepdims=True)
                sth = sqh * dtc[:, hh:hh + 1]
                rmine = rlo if q == 0 else jnp.logical_not(rlo)
                hsum = jnp.sum(jnp.sum(jnp.where(rmine, hh_prod, 0.0), axis=1, keepdims=True), axis=0, keepdims=True)
                last = jnp.sum(sth, axis=0, keepdims=True) + jnp.exp(clast[:, hh:hh + 1]) * hsum
                dcol = jnp.sum(m, axis=1, keepdims=True) + t1h - sth
                dcum_col = jnp.where(lane == hh, dcol + jnp.where(rowi == Q - 1, last, 0.0), dcum_col)
                sq_col = jnp.where(lane == hh, sqh, sq_col)
            dcm = dcm + _dot(dye, hin)
            dbm = dbm + _dot(xp * sdec, dhp)
            decrow = jnp.where(rlo, jnp.exp(cl0), jnp.exp(cl1))
            dh_ref[cols, :] = dhp * decrow + _dot(dye, Cm, "tn")
            dx_ref[:, cols] = dxp
            dd_ref[:, cols] = jnp.broadcast_to(jnp.sum(dyp * xp, axis=0, keepdims=True), (SUBLANES, LANES))
        dc_ref[...] = dcm + _dot(ds_g, Bm)
        db_ref[...] = dbm + _dot(ds_g, Cm, "tn")
        li = lax.broadcasted_iota(jnp.int32, (Q, Q), 0)
        si = lax.broadcasted_iota(jnp.int32, (Q, Q), 1)
        ddac_ref[...] = jnp.dot((li <= si).astype(F32), dcum_col, precision=HI, preferred_element_type=F32)
        ddar_ref[...] = jnp.dot(dcum_row, tril.astype(F32), precision=HI, preferred_element_type=F32)
        sq_ref[...] = sq_col
        cms_ref[...] = cms_row

    smem = pl.BlockSpec(memory_space=pltpu.SMEM)
    return pl.pallas_call(
        body, name=name, grid=(N_GROUPS, nc),
        in_specs=[smem, x_spec, b_spec, c_spec, colm, rowm, colv, rowv, colv, rowv, st_spec, x_spec],
        out_specs=(x_spec, bo_spec, bo_spec, colm, rowm, colm, rowm, dd_spec),
        out_shape=(_sds((T, D)), _sds((T, D // 2)), _sds((T, D // 2)), _sds((N_GROUPS, T, LANES)), _sds((N_GROUPS, SUBLANES, T)),
                   _sds((N_GROUPS, T, LANES)), _sds((N_GROUPS, SUBLANES, T)), _sds((nc, N_GROUPS, SUBLANES, 2 * LANES))),
        scratch_shapes=[pltpu.VMEM((2 * LANES, N_STATE), F32)],
        compiler_params=_cparams(("parallel", "arbitrary")))(dskip, xc, xc, xc, raw_col, raw_row, bias_col, bias_row, a_col, a_row,
                                                            states, dy)


def _adamw(w, g, m, v, name):
    shape = w.shape
    cols = shape[-1]
    w2, g2, m2, v2 = (t.reshape(-1, cols) for t in (w, g, m, v))
    rows = w2.shape[0]
    tr = 256 if (rows % 256 == 0 and rows > 256) else rows
    c1 = 1.0 - ADAM_B1 ** ADAM_STEP
    c2 = 1.0 - ADAM_B2 ** ADAM_STEP

    def body(w_ref, g_ref, m_ref, v_ref, d_ref, mo_ref, vo_ref):
        gv = g_ref[...]
        mn = ADAM_B1 * m_ref[...] + (1.0 - ADAM_B1) * gv
        vn = ADAM_B2 * v_ref[...] + (1.0 - ADAM_B2) * (gv * gv)
        d_ref[...] = -ADAM_LR * ((mn / c1) / (jnp.sqrt(vn / c2) + ADAM_EPS) + ADAM_WD * w_ref[...])
        mo_ref[...] = mn
        vo_ref[...] = vn

    spec = pl.BlockSpec((tr, cols), lambda i: (i, 0))
    out = pl.pallas_call(body, name=name, grid=(rows // tr,), in_specs=[spec] * 4, out_specs=(spec,) * 3,
                         out_shape=(_sds((rows, cols)),) * 3, compiler_params=_cparams(("parallel",)))(w2, g2, m2, v2)
    return tuple(o.reshape(shape) for o in out)


def _place():
    x, y, c = lax.axis_index("x"), lax.axis_index("y"), lax.axis_index("c")
    chips = [(1 - x, y), (x, 1 - y), (1 - x, 1 - y)]
    return x, y, c, chips


_ANY = pl.BlockSpec(memory_space=pl.ANY)


TENSORS = (("e_w_in", "row", 2, 4096, 1284, 1024), ("e_w_out", "row", 2, 2048, 1024, 512), ("o_w_in", "col", 2, 1024, 3072, 768),
           ("o_w_out", "row", 2, 1024, 1024, 256), ("f_w_up", "col", 4, 1024, 5632, 1408), ("f_w_down", "row", 4, 2816, 1024, 704),
           ("ple_w_proj", "col", 4, 256, 1024, 256), ("ple_w_gate", "row", 4, 1024, 1024, 256))
NT = len(TENSORS)


def _win(ref, kind, n, k, l0, nl):
    if kind == "row":
        return ref.at[pl.ds(l0, nl), pl.ds(pl.multiple_of(k * n, 16), n), :]
    return ref.at[pl.ds(l0, nl), :, pl.ds(pl.multiple_of(k * n, LANES), n)]


def _shard_dims(kind, A, B, n):
    return (n, B) if kind == "row" else (A, n)


def _cast_into(w, spec, me):
    name, kind, L, A, B, n = spec
    As, Bs = _shard_dims(kind, A, B, n)

    def body(me_ref, w_ref, o_ref):
        o_ref[...] = w_ref[...].astype(BF16)

    omap = (lambda l, m: (l, m[0], 0)) if kind == "row" else (lambda l, m: (l, 0, m[0]))
    grid_spec = pltpu.PrefetchScalarGridSpec(
        num_scalar_prefetch=1, grid=(L,), in_specs=[pl.BlockSpec((None, As, Bs), lambda l, m: (l, 0, 0))],
        out_specs=pl.BlockSpec((None, As, Bs), omap))
    return pl.pallas_call(body, name=f"cast_{name}", grid_spec=grid_spec, out_shape=_sds((L, A, B), BF16),
                          compiler_params=_cparams(("parallel",)))(me, w.reshape(L, As, Bs))


def _rcopy(send_sems, recv_sems, k, src, dst, to):
    return pltpu.make_async_remote_copy(src_ref=src, dst_ref=dst, send_sem=send_sems.at[k], recv_sem=recv_sems.at[k],
                                        device_id=to, device_id_type=MESH)


def _gather_all(fulls, ws):
    def body(*refs):
        ws_ref = refs[NT]
        outs = refs[NT + 1:2 * NT + 1]
        WS_ref, send_sems, recv_sems, lsem = refs[2 * NT + 1:]
        x, y, c, chips = _place()
        me = 2 * x + y
        sib = (x, y, 1 - c)
        rc = functools.partial(_rcopy, send_sems, recv_sems)
        loc = pltpu.make_async_copy(ws_ref, WS_ref.at[me], lsem)
        loc.start()
        first = []
        for t, (_, kind, L, A, B, n) in enumerate(TENSORS):
            mine = _win(outs[t], kind, n, me, c * (L // 2), L // 2)
            first += [rc(6 * t + j, mine, mine, (*chip, c)) for j, chip in enumerate(chips)]
        small = [rc(6 * NT + j, ws_ref, WS_ref.at[me], (*chip, c)) for j, chip in enumerate(chips)]
        for cp in first + small:
            cp.start()
        passed = []
        for t, (_, kind, L, A, B, n) in enumerate(TENSORS):
            for j, (px, py) in enumerate(chips):
                slot = _win(outs[t], kind, n, 2 * px + py, c * (L // 2), L // 2)
                rc(6 * t + j, slot, slot, sib).wait_recv()
                fwd = rc(6 * t + 3 + j, slot, slot, sib)
                fwd.start()
                passed.append(fwd)
        for t, (_, kind, L, A, B, n) in enumerate(TENSORS):
            for j, (px, py) in enumerate(chips):
                oslot = _win(outs[t], kind, n, 2 * px + py, (1 - c) * (L // 2), L // 2)
                rc(6 * t + 3 + j, oslot, oslot, sib).wait_recv()
        for j, (px, py) in enumerate(chips):
            sslot = WS_ref.at[2 * px + py]
            rc(6 * NT + j, sslot, sslot, sib).wait_recv()
        for cp in first + small + passed:
            cp.wait_send()
        loc.wait()

    ns = 6 * NT + 3
    out = pl.pallas_call(
        body, name="gather_weights", in_specs=[_ANY] * (NT + 1), out_specs=(_ANY,) * (NT + 1),
        out_shape=tuple(_sds(f.shape, f.dtype) for f in fulls) + (_sds((4,) + ws.shape, ws.dtype),),
        input_output_aliases={t: t for t in range(NT)},
        scratch_shapes=[pltpu.SemaphoreType.DMA((ns,)), pltpu.SemaphoreType.DMA((ns,)), pltpu.SemaphoreType.DMA(())],
        compiler_params=pltpu.CompilerParams(has_side_effects=True))(*fulls, ws)
    return out[:NT], out[NT]


def _swap_all(gs):
    def body(*refs):
        g_refs, o_refs = refs[:NT], refs[NT:2 * NT]
        send_sems, recv_sems = refs[2 * NT:]
        x, y, c, _ = _place()
        cps = []
        for t, (_, kind, L, A, B, n) in enumerate(TENSORS):
            cp = _rcopy(send_sems, recv_sems, t, g_refs[t].at[pl.ds((1 - c) * (L // 2), L // 2)], o_refs[t], (x, y, 1 - c))
            cp.start()
            cps.append(cp)
        for cp in cps:
            cp.wait()

    return pl.pallas_call(
        body, name="swap_grads", in_specs=[_ANY] * NT, out_specs=(_ANY,) * NT,
        out_shape=tuple(_sds((s[2] // 2, s[3], s[4])) for s in TENSORS),
        scratch_shapes=[pltpu.SemaphoreType.DMA((NT,)), pltpu.SemaphoreType.DMA((NT,))],
        compiler_params=pltpu.CompilerParams(has_side_effects=True))(*gs)


def _scatter_all(ps):
    def body(*refs):
        p_refs, o_refs = refs[:NT], refs[NT:2 * NT]
        send_sems, recv_sems = refs[2 * NT:]
        x, y, c, chips = _place()
        cps = []
        for t, (_, kind, L, A, B, n) in enumerate(TENSORS):
            for j, (px, py) in enumerate(chips):
                cp = _rcopy(send_sems, recv_sems, 3 * t + j, _win(p_refs[t], kind, n, 2 * px + py, 0, L // 2), o_refs[t].at[j],
                            (px, py, c))
                cp.start()
                cps.append(cp)
        for t in range(NT):
            for j, (px, py) in enumerate(chips):
                slot = o_refs[t].at[j]
                _rcopy(send_sems, recv_sems, 3 * t + j, slot, slot, (px, py, c)).wait_recv()
        for cp in cps:
            cp.wait_send()

    return pl.pallas_call(
        body, name="scatter_grads", in_specs=[_ANY] * NT, out_specs=(_ANY,) * NT,
        out_shape=tuple(_sds((3, s[2] // 2) + _shard_dims(s[1], s[3], s[4], s[5]), BF16) for s in TENSORS),
        scratch_shapes=[pltpu.SemaphoreType.DMA((3 * NT,)), pltpu.SemaphoreType.DMA((3 * NT,))],
        compiler_params=pltpu.CompilerParams(has_side_effects=True))(*ps)


def _join_all(rs):
    def body(*refs):
        outs = refs[NT:2 * NT]
        send_sems, recv_sems = refs[2 * NT:]
        x, y, c, _ = _place()
        cps = []
        for t, (_, kind, L, A, B, n) in enumerate(TENSORS):
            mine = outs[t].at[pl.ds(c * (L // 2), L // 2)]
            cp = _rcopy(send_sems, recv_sems, t, mine, mine, (x, y, 1 - c))
            cp.start()
            cps.append(cp)
        for t, (_, kind, L, A, B, n) in enumerate(TENSORS):
            other = outs[t].at[pl.ds((1 - c) * (L // 2), L // 2)]
            _rcopy(send_sems, recv_sems, t, other, other, (x, y, 1 - c)).wait_recv()
        for cp in cps:
            cp.wait_send()

    return pl.pallas_call(
        body, name="join_halves", in_specs=[_ANY] * NT, out_specs=(_ANY,) * NT,
        out_shape=tuple(_sds(r.shape, r.dtype) for r in rs), input_output_aliases={t: t for t in range(NT)},
        scratch_shapes=[pltpu.SemaphoreType.DMA((NT,)), pltpu.SemaphoreType.DMA((NT,))],
        compiler_params=pltpu.CompilerParams(has_side_effects=True))(*rs)


def _add_half(g, ra, spec, cvec):
    name, kind, L, A, B, n = spec
    tr = _tile(A, [], (256, 128))

    def body(c_ref, g_ref, r_ref, o_ref):
        o_ref[...] = (g_ref[...] + r_ref[...]).astype(BF16)

    blk = (None, tr, B)
    grid_spec = pltpu.PrefetchScalarGridSpec(
        num_scalar_prefetch=1, grid=(L // 2, A // tr),
        in_specs=[pl.BlockSpec(blk, lambda l, i, cr: (l + cr[0] * (L // 2), i, 0)), pl.BlockSpec(blk, lambda l, i, cr: (l, i, 0))],
        out_specs=pl.BlockSpec(blk, lambda l, i, cr: (l, i, 0)))
    return pl.pallas_call(body, name=f"addhalf_{name}", grid_spec=grid_spec, out_shape=_sds((L // 2, A, B), BF16),
                          compiler_params=_cparams(("parallel", "parallel")))(cvec, g, ra)


def _sum_own(p, rc, spec, mevec):
    name, kind, L, A, B, n = spec
    As, Bs = _shard_dims(kind, A, B, n)
    tr = As if As * Bs * 4 <= 4 * 1024 * 1024 else _tile(As, [], (256, 128))
    nb = As // tr

    def body(m_ref, p_ref, r0, r1, r2, o_ref):
        o_ref[...] = ((p_ref[...].astype(F32) + r0[...].astype(F32)) + r1[...].astype(F32)) + r2[...].astype(F32)

    if kind == "row":
        pmap = lambda l, i, m: (l, m[0] * nb + i, 0)
    else:
        pmap = lambda l, i, m: (l, i, m[0])
    rspec = [pl.BlockSpec((None, None, tr, Bs), functools.partial(lambda l, i, m, j: (j, l, i, 0), j=j)) for j in range(3)]
    grid_spec = pltpu.PrefetchScalarGridSpec(
        num_scalar_prefetch=1, grid=(L // 2, nb), in_specs=[pl.BlockSpec((None, tr, Bs), pmap)] + rspec,
        out_specs=pl.BlockSpec((None, tr, Bs), lambda l, i, m: (l + m[1] * (L // 2), i, 0)))
    return pl.pallas_call(body, name=f"sumown_{name}", grid_spec=grid_spec, out_shape=_sds((L, As, Bs)),
                          compiler_params=_cparams(("parallel", "parallel")))(mevec, p, rc, rc, rc)


def _allgather_small(v):
    m_per, n = v.shape

    def body(x_ref, out_ref, send_sems, recv_sems, local_sem):
        x, y, c, chips = _place()
        me, sibling = (x, y, c), (x, y, 1 - c)

        def rows(px, py, pc):
            return out_ref.at[pl.ds(pl.multiple_of((4 * px + 2 * py + pc) * m_per, SUBLANES), m_per), :]

        def copy(k, block, to, src=None):
            return pltpu.make_async_remote_copy(src_ref=rows(*block) if src is None else src, dst_ref=rows(*block),
                                                send_sem=send_sems.at[k], recv_sem=recv_sems.at[k], device_id=to, device_id_type=MESH)

        mine = pltpu.make_async_copy(x_ref, rows(*me), local_sem)
        mine.start()
        first = [copy(0, me, sibling, src=x_ref)]
        first += [copy(1 + j, me, (*chip, c), src=x_ref) for j, chip in enumerate(chips)]
        for cp in first:
            cp.start()
        passed = [copy(4 + j, (*chip, c), sibling) for j, chip in enumerate(chips)]
        for j, chip in enumerate(chips):
            copy(1 + j, (*chip, c), me).wait_recv()
            passed[j].start()
        copy(0, sibling, me).wait_recv()
        for j, chip in enumerate(chips):
            copy(4 + j, (*chip, 1 - c), me).wait_recv()
        for cp in first + passed:
            cp.wait_send()
        mine.wait()

    vm = pl.BlockSpec(memory_space=pltpu.VMEM)
    return pl.pallas_call(body, name="allgather_small", in_specs=[vm], out_specs=vm, out_shape=_sds((8 * m_per, n)),
                          scratch_shapes=[pltpu.SemaphoreType.DMA((7,)), pltpu.SemaphoreType.DMA((7,)), pltpu.SemaphoreType.DMA(())],
                          compiler_params=pltpu.CompilerParams(has_side_effects=True, vmem_limit_bytes=VMEM_LIMIT))(v)


def _sum8(v, m_per):
    def body(v_ref, o_ref):
        acc = v_ref[0:m_per, :]
        for k in range(1, 8):
            acc = acc + v_ref[k * m_per:(k + 1) * m_per, :]
        o_ref[...] = acc

    return pl.pallas_call(body, name="small_sum_devices", out_shape=_sds((m_per, v.shape[1])),
                          compiler_params=pltpu.CompilerParams(vmem_limit_bytes=VMEM_LIMIT))(v)


SMALL_SHARDED = (("e_conv_a_w", 2), ("e_conv_b_w", 2), ("o_conv_w", 2), ("f_conv_w", 2), ("ln_g", 2), ("ln_b", 2))
SMALL_REPL = ("e_conv_a_b", "e_ln_a_g", "e_ln_a_b", "e_conv_b_b", "e_dt_bias", "e_a_log", "e_d_skip", "e_norm_b_g", "f_conv_b")

WEIGHT_ORDER = ('e_w_in', 'e_conv_a_w', 'e_conv_a_b', 'e_ln_a_g', 'e_ln_a_b', 'e_conv_b_w', 'e_conv_b_b', 'e_dt_bias', 'e_a_log',
                'e_d_skip', 'e_norm_b_g', 'e_w_out', 'o_w_in', 'o_conv_w', 'o_w_out', 'f_w_up', 'f_conv_w', 'f_conv_b', 'f_w_down',
                'ple_w_proj', 'ple_w_gate', 'ln_g', 'ln_b')


def _pack_rows(parts, width, total_rows, dtype):
    flat = jnp.concatenate([p.reshape(-1).astype(dtype) for p in parts])
    flat = jnp.pad(flat, (0, total_rows * width - flat.shape[0]))
    return flat.reshape(total_rows, width)


def _unpack_rows(buf, shapes):
    flat = buf.reshape(-1)
    out, pos = [], 0
    for s in shapes:
        n = math.prod(s)
        out.append(flat[pos:pos + n].reshape(s))
        pos += n
    return out


def _small_rows(shapes):
    n = sum(math.prod(s) for s in shapes)
    return -(-n // (LANES * SUBLANES)) * SUBLANES


E_PAD = 5248
SEG_A, SEG_Z, SEG_X, SEG_DT = (0, 2 * D), (2 * D, D), (3 * D, 2 * D), (5 * D, LANES)
G_SHAPES = {"e_w_in": (2, D, E_PAD), "e_w_out": (2, 2 * D, D), "o_w_in": (2, D, 3 * D), "o_w_out": (2, D, D),
            "f_w_up": (4, D, 2 * D_FF), "f_w_down": (4, D_FF, D), "ple_w_proj": (4, PLE, D), "ple_w_gate": (4, D, D)}


def _padcols(w, width):
    return jnp.pad(w, ((0, 0), (0, width - w.shape[1])))


def _fold_rows(dw, K):
    return dw.reshape(K, SUBLANES, dw.shape[-1]).sum(1)


def _local_step(x, p, target, W):
    T = x.shape[0]
    xb = x
    saved = []
    xc_f = x
    for i in range(DEPTH):
        j = i // 2
        L = {}
        L["x"], L["xb"] = xc_f, xb
        if i % 2 == 0:
            def w_in(seg, c0=0, cols=None, j=j):
                return V(W["e_w_in"], (j,), c0=seg[0] + c0, cols=seg[1] if cols is None else cols)

            ua = _mm(xb, w_in(SEG_A), "nn", f"l{i}_in_a")
            z = _mm(xb, w_in(SEG_Z), "nn", f"l{i}_in_z")
            xu = _mm(xb, w_in(SEG_X), "nn", f"l{i}_in_xbc")
            udt = _mm(xb, w_in(SEG_DT), "nn", f"l{i}_in_dt")
            ac = _conv_a_fwd(ua, W["e_conv_a_w"][j], W["e_conv_a_b"][j][None], f"l{i}_conv_a")
            ya = _ln_silu_fwd(ac, W["e_ln_a_g"][j][None], W["e_ln_a_b"][j][None], f"l{i}_ln_a")
            xc = _conv_b_fwd(xu, W["e_conv_b_w"][j], W["e_conv_b_b"][j][None], f"l{i}_conv_b")
            sm = _ssd_small_inputs(udt[:, :N_HEADS], W["e_dt_bias"][j], W["e_a_log"][j])
            y, states = _ssd_fwd(xc, *sm, W["e_d_skip"][j], f"l{i}_ssd")
            yb = _gate_rms_fwd(y, z, W["e_norm_b_g"][j][None], f"l{i}_gate_rms")
            mix = _mm(ya, V(W["e_w_out"], (j,), rows=D), "nn", f"l{i}_out_a")
            mix = _mm(yb, V(W["e_w_out"], (j,), r0=D), "nn", f"l{i}_out_b", add=mix)
            L.update(ua=ua, z=z, xu=xu, udt=udt, ac=ac, ya=ya, xc=xc, sm=sm, y=y, states=states, yb=yb, w_in=w_in)
        else:
            uo = _mm(xb, V(W["o_w_in"], (j,)), "nn", f"l{i}_in")
            sc = _conv_c_fwd(uo, W["o_conv_w"][j], f"l{i}_conv_c")
            mix = _mm(sc, V(W["o_w_out"], (j,)), "nn", f"l{i}_out")
            L.update(uo=uo, sc=sc)
        h1, x1, x1b = _res_ln_fwd(xc_f, [mix], None, W["ln_g"][i, 0][None], W["ln_b"][i, 0][None], f"l{i}_ln1")
        up = _mm(x1b, V(W["f_w_up"], (i,)), "nn", f"l{i}_ffn_up")
        act = _conv_f_fwd(up, W["f_conv_w"][i], W["f_conv_b"][i][None], f"l{i}_conv_f")
        ffn = _mm(act, V(W["f_w_down"], (i,)), "nn", f"l{i}_ffn_down")
        pv = V(p, (i, 0))
        pp = _mm(pv, V(W["ple_w_proj"], (i,)), "nn", f"l{i}_ple_proj")
        gl = _mm(x1b, V(W["ple_w_gate"], (i,)), "nn", f"l{i}_ple_gate")
        h2, x2, x2b = _res_ln_fwd(x1, [ffn], (pp, gl), W["ln_g"][i, 1][None], W["ln_b"][i, 1][None], f"l{i}_ln2")
        L.update(h1=h1, x1=x1, x1b=x1b, up=up, act=act, pv=pv, pp=pp, gl=gl, h2=h2)
        saved.append(L)
        xc_f, xb = x2, x2b

    sq, dx = _loss_head(xc_f, target, "loss_head")

    GB = {}

    def into(n, layer, r0=0, c0=0):
        return (GB.get(n), G_SHAPES[n], (layer,), r0, c0)

    G = {n: [None] * (DEPTH if n.startswith(("f_", "ln_")) else DEPTH // 2) for n in WEIGHT_ORDER if n not in G_SHAPES}
    for i in reversed(range(DEPTH)):
        j = i // 2
        L = saved[i]
        dh2, dh2b, dg2, db2, dpp, dgl = _res_ln_bwd(dx, L["h2"], W["ln_g"][i, 1][None], (L["pp"], L["gl"]), f"l{i}_ln2_bwd")
        GB["f_w_down"] = _mm(L["act"], dh2b, "tn", f"l{i}_dw_down", dst=into("f_w_down", i))
        dact = _mm(dh2b, V(W["f_w_down"], (i,)), "nt", f"l{i}_dact")
        du1, du2, dw1, dw2, dbf1, dbf2 = _conv_f_bwd(L["up"], W["f_conv_w"][i], W["f_conv_b"][i][None], dact, f"l{i}_conv_f_bwd")
        G["f_conv_w"][i] = jnp.concatenate([_fold_rows(dw1, CONV_F), _fold_rows(dw2, CONV_F)], axis=1)
        G["f_conv_b"][i] = jnp.concatenate([dbf1.sum(0), dbf2.sum(0)])
        GB["f_w_up"] = _mm(L["x1b"], du1, "tn", f"l{i}_dw_up1", dst=into("f_w_up", i))
        GB["f_w_up"] = _mm(L["x1b"], du2, "tn", f"l{i}_dw_up2", dst=into("f_w_up", i, c0=D_FF))
        GB["ple_w_proj"] = _mm(L["pv"], dpp, "tn", f"l{i}_dw_proj", dst=into("ple_w_proj", i))
        GB["ple_w_gate"] = _mm(L["x1b"], dgl, "tn", f"l{i}_dw_gate", dst=into("ple_w_gate", i))
        dx1 = _mm(du1, V(W["f_w_up"], (i,), cols=D_FF), "nt", f"l{i}_dx1_a", add=dh2, add_scale=ALPHA)
        dx1 = _mm(du2, V(W["f_w_up"], (i,), c0=D_FF), "nt", f"l{i}_dx1_b", add=dx1)
        dx1 = _mm(dgl, V(W["ple_w_gate"], (i,)), "nt", f"l{i}_dx1_c", add=dx1)
        dh1, dh1b, dg1, db1 = _res_ln_bwd(dx1, L["h1"], W["ln_g"][i, 0][None], None, f"l{i}_ln1_bwd")
        G["ln_g"][i] = jnp.concatenate([dg1, dg2], axis=0)
        G["ln_b"][i] = jnp.concatenate([db1, db2], axis=0)
        if i % 2 == 0:
            GB["e_w_out"] = _mm(L["ya"], dh1b, "tn", f"l{i}_dw_out_a", dst=into("e_w_out", j))
            GB["e_w_out"] = _mm(L["yb"], dh1b, "tn", f"l{i}_dw_out_b", dst=into("e_w_out", j, r0=D))
            dya = _mm(dh1b, V(W["e_w_out"], (j,), rows=D), "nt", f"l{i}_dya")
            dyb = _mm(dh1b, V(W["e_w_out"], (j,), r0=D), "nt", f"l{i}_dyb")
            dac, dga, dba = _ln_silu_bwd(L["ac"], dya, W["e_ln_a_g"][j][None], W["e_ln_a_b"][j][None], f"l{i}_ln_a_bwd")
            G["e_ln_a_g"][j], G["e_ln_a_b"][j] = dga[0], dba[0]
            dal, dag, dwa, dbca = _conv_a_bwd(L["ua"], W["e_conv_a_w"][j], dac, f"l{i}_conv_a_bwd")
            G["e_conv_a_w"][j] = _fold_rows(dwa, CONV_A)
            G["e_conv_a_b"][j] = dbca.sum(0)
            dy, dz, dgn = _gate_rms_bwd(L["y"], L["z"], dyb, W["e_norm_b_g"][j][None], f"l{i}_gate_rms_bwd")
            G["e_norm_b_g"][j] = dgn[0]
            dxs, dbs, dcs, sq_col, cms_row, dda_col, dda_row, ddp = _ssd_bwd(L["xc"], *L["sm"], W["e_d_skip"][j], L["states"], dy,
                                                                             f"l{i}_ssd_bwd")
            draw, G["e_dt_bias"][j], G["e_a_log"][j] = _ssd_small_grads(L["udt"][:, :N_HEADS], W["e_dt_bias"][j], W["e_a_log"][j],
                                                                       sq_col, cms_row, dda_col, dda_row)
            G["e_d_skip"][j] = ddp[:, :, 0, :].sum(0).reshape(N_HEADS, HEAD_P).sum(1)
            dxu, dwb, dbcb = _conv_b_bwd(L["xu"], W["e_conv_b_w"][j], W["e_conv_b_b"][j][None], dxs, dbs, dcs, f"l{i}_conv_b_bwd")
            G["e_conv_b_w"][j] = _fold_rows(dwb, CONV_B)
            G["e_conv_b_b"][j] = dbcb.sum(0)
            dudt = _padcols(draw, LANES)
            w_in = L["w_in"]
            xb_l = L["xb"]
            for nm, dseg, c0 in (("al", dal, 0), ("ag", dag, D), ("z", dz, SEG_Z[0]), ("xbc", dxu, SEG_X[0]), ("dt", dudt, SEG_DT[0])):
                GB["e_w_in"] = _mm(xb_l, dseg, "tn", f"l{i}_dw_in_{nm}", dst=into("e_w_in", j, c0=c0))
            dx = _mm(dal, w_in(SEG_A, cols=D), "nt", f"l{i}_dx_al", add=dh1, add_scale=ALPHA)
            dx = _mm(dag, w_in(SEG_A, c0=D, cols=D), "nt", f"l{i}_dx_ag", add=dx)
            dx = _mm(dz, w_in(SEG_Z), "nt", f"l{i}_dx_z", add=dx)
            dx = _mm(dxu, w_in(SEG_X), "nt", f"l{i}_dx_xbc", add=dx)
            dx = _mm(dudt, w_in(SEG_DT), "nt", f"l{i}_dx_dt", add=dx)
        else:
            GB["o_w_out"] = _mm(L["sc"], dh1b, "tn", f"l{i}_dw_out", dst=into("o_w_out", j))
            dsc = _mm(dh1b, V(W["o_w_out"], (j,)), "nt", f"l{i}_dsc")
            dbg, dcg, dv, dwc = _conv_c_bwd(L["uo"], W["o_conv_w"][j], dsc, f"l{i}_conv_c_bwd")
            G["o_conv_w"][j] = _fold_rows(dwc, CONV_C)
            xb_l = L["xb"]
            dx = dh1
            for nm, dseg, c0, scale in (("bg", dbg, 0, ALPHA), ("cg", dcg, D, 1.0), ("v", dv, 2 * D, 1.0)):
                GB["o_w_in"] = _mm(xb_l, dseg, "tn", f"l{i}_dw_in_{nm}", dst=into("o_w_in", j, c0=c0))
                dx = _mm(dseg, V(W["o_w_in"], (j,), c0=c0, cols=D), "nt", f"l{i}_dx_{nm}", add=dx, add_scale=scale)
    grads = {n: jnp.stack(v) for n, v in G.items()}
    return sq, dx, GB, grads


def _ssd_small_inputs(raw, dt_bias, a_log):
    T = raw.shape[0]
    a = -jnp.exp(a_log)
    rg = raw.reshape(T, N_GROUPS, 4)
    raw_col = jnp.pad(jnp.transpose(rg, (1, 0, 2)), ((0, 0), (0, 0), (0, LANES - 4)))
    raw_row = jnp.pad(jnp.transpose(rg, (1, 2, 0)), ((0, 0), (0, SUBLANES - 4), (0, 0)))

    def colv(v):
        return jnp.pad(v.reshape(N_GROUPS, 1, 4), ((0, 0), (0, 0), (0, LANES - 4)))

    def rowv(v):
        return jnp.pad(v.reshape(N_GROUPS, 4, 1), ((0, 0), (0, SUBLANES - 4), (0, 0)))

    return raw_col, raw_row, colv(dt_bias), rowv(dt_bias), colv(a), rowv(a)


def _ssd_small_grads(raw, dt_bias, a_log, sq_col, cms_row, dda_col, dda_row):
    T = raw.shape[0]

    def join(col, row):
        c = jnp.transpose(col[:, :, :4], (1, 0, 2)).reshape(T, N_HEADS)
        r = jnp.transpose(row[:, :4, :], (2, 0, 1)).reshape(T, N_HEADS)
        return c + r

    a = -jnp.exp(a_log)
    pre = raw + dt_bias
    dt = jax.nn.softplus(pre)
    dda = join(dda_col, dda_row)
    ddt = join(sq_col, cms_row) + a * dda
    draw = ddt * jax.nn.sigmoid(pre)
    da = jnp.sum(dt * dda, axis=0)
    return draw, jnp.sum(draw, axis=0), da * a


def kernel(x, p, e_w_in, e_conv_a_w, e_conv_a_b, e_ln_a_g, e_ln_a_b, e_conv_b_w, e_conv_b_b, e_dt_bias, e_a_log, e_d_skip, e_norm_b_g, e_w_out, o_w_in, o_conv_w, o_w_out, f_w_up, f_conv_w, f_conv_b, f_w_down, ple_w_proj, ple_w_gate, ln_g, ln_b, loss_target, m_e_w_in, m_e_conv_a_w, m_e_conv_a_b, m_e_ln_a_g, m_e_ln_a_b, m_e_conv_b_w, m_e_conv_b_b, m_e_dt_bias, m_e_a_log, m_e_d_skip, m_e_norm_b_g, m_e_w_out, m_o_w_in, m_o_conv_w, m_o_w_out, m_f_w_up, m_f_conv_w, m_f_conv_b, m_f_w_down, m_ple_w_proj, m_ple_w_gate, m_ln_g, m_ln_b, v_e_w_in, v_e_conv_a_w, v_e_conv_a_b, v_e_ln_a_g, v_e_ln_a_b, v_e_conv_b_w, v_e_conv_b_b, v_e_dt_bias, v_e_a_log, v_e_d_skip, v_e_norm_b_g, v_e_w_out, v_o_w_in, v_o_conv_w, v_o_w_out, v_f_w_up, v_f_conv_w, v_f_conv_b, v_f_w_down, v_ple_w_proj, v_ple_w_gate, v_ln_g, v_ln_b):
    args = dict(locals())
    w_shard = {n: args[n] for n in WEIGHT_ORDER}
    m_shard = {n: args["m_" + n] for n in WEIGHT_ORDER}
    v_shard = {n: args["v_" + n] for n in WEIGHT_ORDER}
    xi, yi, ci = lax.axis_index("x"), lax.axis_index("y"), lax.axis_index("c")
    chip = 2 * xi + yi

    mevec = jnp.stack([chip, ci]).astype(jnp.int32)
    small_shapes = [w_shard[n].shape for n, _ in SMALL_SHARDED]
    sr = _small_rows(small_shapes)
    ws = _pack_rows([w_shard[n] for n, _ in SMALL_SHARDED], LANES, sr, F32)
    fulls = [_cast_into(w_shard[s[0]], s, mevec[:1]) for s in TENSORS]
    fulls, WS = _gather_all(fulls, ws)
    W = {n: w_shard[n] for n in SMALL_REPL}
    W.update({s[0]: f for s, f in zip(TENSORS, fulls)})
    ew = jnp.transpose(W["e_w_in"].reshape(2, 4, D, E_IN // 4), (0, 2, 1, 3)).reshape(2, D, E_IN)
    W["e_w_in"] = jnp.pad(ew, ((0, 0), (0, 0), (0, E_PAD - E_IN)))
    parts_s = [_unpack_rows(WS[k], small_shapes) for k in range(4)]
    for idx, (n, ax) in enumerate(SMALL_SHARDED):
        W[n] = jnp.concatenate([parts_s[k][idx] for k in range(4)], axis=ax)

    sq, dx, GB, G = _local_step(x[0], p, loss_target[0], W)
    loss = lax.psum(0.5 * sq[0, 0] / D, ("x", "y", "c"))
    grad_x = dx[None]

    def shard_of(g, ax, k):
        n = g.shape[ax] // 4
        return lax.slice_in_dim(g, k * n, (k + 1) * n, axis=ax)

    GB["e_w_in"] = jnp.transpose(GB["e_w_in"][:, :, :E_IN].reshape(2, D, 4, E_IN // 4), (0, 2, 1, 3)).reshape(2, 4 * D, E_IN // 4)
    gs = [GB[s[0]] for s in TENSORS]
    ras = _swap_all(gs)
    ps = [_add_half(g, ra, s, mevec[1:]) for g, ra, s in zip(gs, ras, TENSORS)]
    rcs = _scatter_all(ps)
    rs = _join_all([_sum_own(pt, rc, s, mevec) for pt, rc, s in zip(ps, rcs, TENSORS)])
    gbig = {s[0]: r.reshape(w_shard[s[0]].shape) for s, r in zip(TENSORS, rs)}

    small_all = ([shard_of(G[n], ax, k) for k in range(4) for n, ax in SMALL_SHARDED] + [G[n] for n in SMALL_REPL])
    small_all_shapes = [t.shape for t in small_all]
    mr = _small_rows(small_all_shapes)
    sg = _sum8(_allgather_small(_pack_rows(small_all, LANES, mr, F32)), mr)
    sparts = _unpack_rows(sg, small_all_shapes)
    ns = len(SMALL_SHARDED)
    gsmall = {}
    for idx, (n, ax) in enumerate(SMALL_SHARDED):
        stacked = jnp.stack([sparts[k * ns + idx] for k in range(4)])
        gsmall[n] = lax.dynamic_index_in_dim(stacked, chip, axis=0, keepdims=False)
    for idx, n in enumerate(SMALL_REPL):
        gsmall[n] = sparts[4 * ns + idx]

    grads, deltas, new_m, new_v = [], [], [], []
    for n in WEIGHT_ORDER:
        g = gbig[n] if n in gbig else gsmall[n]
        d, mn, vn = _adamw(w_shard[n], g, m_shard[n], v_shard[n], f"adamw_{n}")
        grads.append(g)
        deltas.append(d)
        new_m.append(mn)
        new_v.append(vn)
    return (loss, grad_x, *grads, *deltas, *new_m, *new_v)
```

```python
import functools
import math

import jax
import jax.numpy as jnp
from jax import lax
from jax.experimental import pallas as pl
from jax.experimental.pallas import tpu as pltpu

F32 = jnp.float32
BF16 = jnp.bfloat16
MESH = pl.DeviceIdType.MESH

DEPTH = 4
ALPHA = (2.0 * DEPTH) ** 0.25
LN_EPS = 1e-5
D = 1024
HEAD_P = 64
N_STATE = 128
N_HEADS = 16
N_GROUPS = 4
CONV_A, CONV_B, CONV_C, CONV_F = 31, 4, 3, 3
D_FF = 2816
PLE = 256
E_IN = 5136

ADAM_LR, ADAM_B1, ADAM_B2, ADAM_EPS, ADAM_WD, ADAM_STEP = 0.001, 0.9, 0.999, 1e-08, 0.01, 10

LANES = 128
SUBLANES = 8
VMEM_LIMIT = 56 * 1024 * 1024
SSD_Q = 128
CONV_R = 128
CONV_PAD = 32
ROW_T = 256
HI = lax.Precision.HIGHEST


def _cparams(sem=None):
    return pltpu.CompilerParams(dimension_semantics=sem, vmem_limit_bytes=VMEM_LIMIT)


def _sig(v):
    return jax.nn.sigmoid(v)


_DIMS = {"nn": (((1,), (0,)), ((), ())), "nt": (((1,), (1,)), ((), ())), "tn": (((0,), (0,)), ((), ()))}


class Layers:
    def __init__(self, n_layers):
        self.where = [None] * n_layers

    def put(self, arr, l0):
        for k in range(arr.shape[0]):
            self.where[l0 + k] = (arr, k)


class V:
    def __init__(self, arr, lead=(), r0=0, c0=0, rows=None, cols=None):
        if isinstance(arr, Layers):
            arr, k = arr.where[lead[0]]
            lead = (k,) + tuple(lead[1:])
        self.arr, self.lead, self.r0, self.c0 = arr, tuple(lead), r0, c0
        R, C = arr.shape[-2:]
        self.rows = R - r0 if rows is None else rows
        self.cols = C - c0 if cols is None else cols

    def spec(self, br, bc, fn):
        assert self.r0 % br == 0 and self.c0 % bc == 0, (self.r0, self.c0, br, bc)
        ro, co, lead = self.r0 // br, self.c0 // bc, self.lead

        def index(i, j, k):
            r, c = fn(i, j, k)
            return lead + (r + ro, c + co)

        return pl.BlockSpec((None,) * len(lead) + (br, bc), index)


def _v(t):
    return t if isinstance(t, V) else V(t)


def _tile(n, offs, cands):
    for c in cands:
        if n % c == 0 and all(o % c == 0 for o in offs):
            return c
    raise ValueError((n, offs))


_TILES = (1024, 1408, 512, 256, 128)


def _mm(a, b, mode, name, out_dtype=F32, add=None, add_scale=1.0, dst=None, after=None):
    a, b = _v(a), _v(b)
    add = _v(add) if add is not None else None
    if mode == "nn":
        M, K, K2, N = a.rows, a.cols, b.rows, b.cols
        am, ak, bk, bn = a.r0, a.c0, b.r0, b.c0
    elif mode == "nt":
        M, K, N, K2 = a.rows, a.cols, b.rows, b.cols
        am, ak, bn, bk = a.r0, a.c0, b.r0, b.c0
    else:
        K, M, K2, N = a.rows, a.cols, b.rows, b.cols
        ak, am, bk, bn = a.r0, a.c0, b.r0, b.c0
    assert K == K2, (name, mode, M, K, K2, N)
    if dst is None:
        buf, full_shape, o_lead, o_r0, o_c0 = None, (M, N), (), 0, 0
    else:
        buf, full_shape, o_lead, o_r0, o_c0 = dst
    tm = _tile(M, [am, o_r0] + ([add.r0] if add else []), _TILES)
    tn = _tile(N, [bn, o_c0] + ([add.c0] if add else []), _TILES)
    tk = _tile(K, [ak, bk], _TILES)
    nk = K // tk
    has_add, has_buf, has_after = add is not None, buf is not None, after is not None

    def body(*refs):
        a_ref, b_ref = refs[0], refs[1]
        add_ref = refs[2] if has_add else None
        o_ref = refs[2 + has_add + has_buf + has_after]

        def finish(r):
            if has_add:
                r = r + add_scale * add_ref[...].astype(F32)
            o_ref[...] = r.astype(o_ref.dtype)

        part = lax.dot_general(a_ref[...].astype(BF16), b_ref[...].astype(BF16), _DIMS[mode], preferred_element_type=F32)
        if nk == 1:
            finish(part)
        else:
            acc_ref = refs[-1]
            k = pl.program_id(2)

            @pl.when(k == 0)
            def _():
                acc_ref[...] = part

            @pl.when(jnp.logical_and(k > 0, k < nk - 1))
            def _():
                acc_ref[...] += part

            @pl.when(k == nk - 1)
            def _():
                finish(acc_ref[...] + part)

    if mode == "tn":
        a_spec = a.spec(tk, tm, lambda i, j, k: (k, i))
    else:
        a_spec = a.spec(tm, tk, lambda i, j, k: (i, k))
    if mode == "nt":
        b_spec = b.spec(tn, tk, lambda i, j, k: (j, k))
    else:
        b_spec = b.spec(tk, tn, lambda i, j, k: (k, j))
    in_specs, args = [a_spec, b_spec], [a.arr, b.arr]
    if has_add:
        in_specs.append(add.spec(tm, tn, lambda i, j, k: (i, j)))
        args.append(add.arr)
    aliases = {}
    if has_buf:
        aliases = {len(args): 0}
        in_specs.append(pl.BlockSpec(memory_space=pl.ANY))
        args.append(buf)
        out_dtype = buf.dtype
    if has_after:
        in_specs.append(pl.BlockSpec(memory_space=pl.ANY))
        args.append(after)
    o_view = V(jax.ShapeDtypeStruct(full_shape, out_dtype), o_lead, o_r0, o_c0, M, N)
    return pl.pallas_call(
        body, name=name, grid=(M // tm, N // tn, nk), in_specs=in_specs, out_specs=o_view.spec(tm, tn, lambda i, j, k: (i, j)),
        out_shape=jax.ShapeDtypeStruct(full_shape, out_dtype), input_output_aliases=aliases,
        scratch_shapes=[pltpu.VMEM((tm, tn), F32)] if nk > 1 else [],
        compiler_params=_cparams(("parallel", "parallel", "arbitrary")))(*args)


def _rows(T, width=D):
    return pl.BlockSpec((ROW_T, width), lambda i: (i, 0))


def _vec(width=D):
    return pl.BlockSpec((1, width), lambda i: (0, 0))


def _ln_stats(h):
    mu = jnp.mean(h, axis=-1, keepdims=True)
    hc = h - mu
    var = jnp.mean(hc * hc, axis=-1, keepdims=True)
    rstd = lax.rsqrt(var + LN_EPS)
    return hc * rstd, rstd


def _res_ln_fwd(x, adds, ple, g, b, name):
    T = x.shape[0]
    n_add = len(adds)
    has_ple = ple is not None

    def body(*refs):
        x_ref = refs[0]
        add_refs = refs[1:1 + n_add]
        pos = 1 + n_add
        if has_ple:
            pp_ref, gl_ref = refs[pos], refs[pos + 1]
            pos += 2
        g_ref, b_ref, h_ref, y_ref, yb_ref = refs[pos:pos + 5]
        h = ALPHA * x_ref[...]
        for r in add_refs:
            h = h + r[...]
        if has_ple:
            h = h + pp_ref[...] * _sig(gl_ref[...])
        xhat, _ = _ln_stats(h)
        y = xhat * g_ref[...] + b_ref[...]
        h_ref[...] = h
        y_ref[...] = y
        yb_ref[...] = y.astype(BF16)

    n_in = 1 + n_add + (2 if has_ple else 0)
    args = (x,) + tuple(adds) + (tuple(ple) if has_ple else ()) + (g, b)
    return pl.pallas_call(
        body, name=name, grid=(T // ROW_T,), in_specs=[_rows(T)] * n_in + [_vec(), _vec()],
        out_specs=(_rows(T), _rows(T), _rows(T)),
        out_shape=(jax.ShapeDtypeStruct((T, D), F32), jax.ShapeDtypeStruct((T, D), F32), jax.ShapeDtypeStruct((T, D), BF16)),
        compiler_params=_cparams(("parallel",)))(*args)


def _res_ln_bwd(dy, h, g, ple, name):
    T = dy.shape[0]
    has_ple = ple is not None

    def body(*refs):
        if has_ple:
            dy_ref, h_ref, g_ref, pp_ref, gl_ref, dh_ref, dhb_ref, dg_ref, db_ref, dpp_ref, dgl_ref = refs
        else:
            dy_ref, h_ref, g_ref, dh_ref, dhb_ref, dg_ref, db_ref = refs
        i = pl.program_id(0)

        @pl.when(i == 0)
        def _():
            dg_ref[...] = jnp.zeros_like(dg_ref)
            db_ref[...] = jnp.zeros_like(db_ref)

        dyv = dy_ref[...]
        xhat, rstd = _ln_stats(h_ref[...])
        dg_ref[...] += jnp.sum(dyv * xhat, axis=0, keepdims=True)
        db_ref[...] += jnp.sum(dyv, axis=0, keepdims=True)
        dxh = dyv * g_ref[...]
        dh = rstd * (dxh - jnp.mean(dxh, axis=-1, keepdims=True) - xhat * jnp.mean(dxh * xhat, axis=-1, keepdims=True))
        dh_ref[...] = dh
        dhb_ref[...] = dh.astype(BF16)
        if has_ple:
            s = _sig(gl_ref[...])
            dpp_ref[...] = (dh * s).astype(BF16)
            dgl_ref[...] = (dh * pp_ref[...] * s * (1.0 - s)).astype(BF16)

    args = (dy, h, g) + (tuple(ple) if has_ple else ())
    in_specs = [_rows(T), _rows(T), _vec()] + ([_rows(T), _rows(T)] if has_ple else [])
    out_specs = [_rows(T), _rows(T), _vec(), _vec()] + ([_rows(T), _rows(T)] if has_ple else [])
    out_shape = [jax.ShapeDtypeStruct((T, D), F32), jax.ShapeDtypeStruct((T, D), BF16),
                 jax.ShapeDtypeStruct((1, D), F32), jax.ShapeDtypeStruct((1, D), F32)]
    if has_ple:
        out_shape += [jax.ShapeDtypeStruct((T, D), BF16), jax.ShapeDtypeStruct((T, D), BF16)]
    return pl.pallas_call(
        body, name=name, grid=(T // ROW_T,), in_specs=in_specs, out_specs=tuple(out_specs), out_shape=tuple(out_shape),
        compiler_params=_cparams(("arbitrary",)))(*args)


def _ln_silu_fwd(ac, g, b, name):
    T = ac.shape[0]

    def body(a_ref, g_ref, b_ref, o_ref):
        xhat, _ = _ln_stats(a_ref[...])
        ln = xhat * g_ref[...] + b_ref[...]
        o_ref[...] = (ln * _sig(ln)).astype(BF16)

    return pl.pallas_call(
        body, name=name, grid=(T // ROW_T,), in_specs=[_rows(T), _vec(), _vec()], out_specs=_rows(T),
        out_shape=jax.ShapeDtypeStruct((T, D), BF16), compiler_params=_cparams(("parallel",)))(ac, g, b)


def _ln_silu_bwd(ac, dya, g, b, name):
    T = ac.shape[0]

    def body(a_ref, d_ref, g_ref, b_ref, da_ref, dg_ref, db_ref):
        i = pl.program_id(0)

        @pl.when(i == 0)
        def _():
            dg_ref[...] = jnp.zeros_like(dg_ref)
            db_ref[...] = jnp.zeros_like(db_ref)

        xhat, rstd = _ln_stats(a_ref[...])
        ln = xhat * g_ref[...] + b_ref[...]
        s = _sig(ln)
        dln = d_ref[...] * s * (1.0 + ln * (1.0 - s))
        dg_ref[...] += jnp.sum(dln * xhat, axis=0, keepdims=True)
        db_ref[...] += jnp.sum(dln, axis=0, keepdims=True)
        dxh = dln * g_ref[...]
        da_ref[...] = rstd * (dxh - jnp.mean(dxh, axis=-1, keepdims=True)
                              - xhat * jnp.mean(dxh * xhat, axis=-1, keepdims=True))

    return pl.pallas_call(
        body, name=name, grid=(T // ROW_T,), in_specs=[_rows(T), _rows(T), _vec(), _vec()],
        out_specs=(_rows(T), _vec(), _vec()),
        out_shape=(jax.ShapeDtypeStruct((T, D), F32), jax.ShapeDtypeStruct((1, D), F32), jax.ShapeDtypeStruct((1, D), F32)),
        compiler_params=_cparams(("arbitrary",)))(ac, dya, g, b)


def _gate_rms_fwd(y, z, g, name):
    T = y.shape[0]

    def body(y_ref, z_ref, g_ref, o_ref):
        zv = z_ref[...]
        yg = y_ref[...] * (zv * _sig(zv))
        r = lax.rsqrt(jnp.mean(yg * yg, axis=-1, keepdims=True) + LN_EPS)
        o_ref[...] = (yg * r * g_ref[...]).astype(BF16)

    return pl.pallas_call(
        body, name=name, grid=(T // ROW_T,), in_specs=[_rows(T), _rows(T), _vec()], out_specs=_rows(T),
        out_shape=jax.ShapeDtypeStruct((T, D), BF16), compiler_params=_cparams(("parallel",)))(y, z, g)


def _gate_rms_bwd(y, z, dout, g, name):
    T = y.shape[0]

    def body(y_ref, z_ref, d_ref, g_ref, dy_ref, dz_ref, dg_ref):
        i = pl.program_id(0)

        @pl.when(i == 0)
        def _():
            dg_ref[...] = jnp.zeros_like(dg_ref)

        yv, zv, dv = y_ref[...], z_ref[...], d_ref[...]
        s = _sig(zv)
        sz = zv * s
        yg = yv * sz
        r = lax.rsqrt(jnp.mean(yg * yg, axis=-1, keepdims=True) + LN_EPS)
        dg_ref[...] += jnp.sum(dv * yg * r, axis=0, keepdims=True)
        dn = dv * g_ref[...]
        dyg = r * dn - yg * (r * r * r) * jnp.mean(dn * yg, axis=-1, keepdims=True)
        dy_ref[...] = dyg * sz
        dz_ref[...] = dyg * yv * s * (1.0 + zv * (1.0 - s))

    return pl.pallas_call(
        body, name=name, grid=(T // ROW_T,), in_specs=[_rows(T), _rows(T), _rows(T), _vec()],
        out_specs=(_rows(T), _rows(T), _vec()),
        out_shape=(jax.ShapeDtypeStruct((T, D), F32), jax.ShapeDtypeStruct((T, D), F32), jax.ShapeDtypeStruct((1, D), F32)),
        compiler_params=_cparams(("arbitrary",)))(y, z, dout, g)


def _loss_head(y, target, name):
    T = y.shape[0]

    def body(y_ref, t_ref, s_ref, d_ref):
        i = pl.program_id(0)

        @pl.when(i == 0)
        def _():
            s_ref[...] = jnp.zeros_like(s_ref)

        err = y_ref[...] - t_ref[...]
        s_ref[...] += jnp.sum(jnp.sum(err * err, axis=1, keepdims=True), axis=0, keepdims=True)
        d_ref[...] = err * (1.0 / D)

    return pl.pallas_call(
        body, name=name, grid=(T // ROW_T,), in_specs=[_rows(T), _rows(T)],
        out_specs=(pl.BlockSpec((SUBLANES, LANES), lambda i: (0, 0)), _rows(T)),
        out_shape=(jax.ShapeDtypeStruct((SUBLANES, LANES), F32), jax.ShapeDtypeStruct((T, D), F32)),
        compiler_params=_cparams(("arbitrary",)))(y, target)


def _taps_fwd(pad_ref, w_ref, K, base):
    off = CONV_PAD - (K - 1)
    acc = w_ref[0:1, :] * pad_ref[pl.ds(base + off, CONV_R), :]
    for k in range(1, K):
        acc = acc + w_ref[k:k + 1, :] * pad_ref[pl.ds(base + off + k, CONV_R), :]
    return acc


def _taps_bwd(padd_ref, w_ref, K, base):
    acc = w_ref[0:1, :] * padd_ref[pl.ds(base + (K - 1), CONV_R), :]
    for k in range(1, K):
        acc = acc + w_ref[k:k + 1, :] * padd_ref[pl.ds(base + (K - 1) - k, CONV_R), :]
    return acc


def _fold8(v):
    return v.reshape(CONV_R // SUBLANES, SUBLANES, v.shape[-1]).sum(0)


def _wgrad_acc(dw_ref, pad_ref, d, K, base):
    off = CONV_PAD - (K - 1)
    for k in range(K):
        dw_ref[k * SUBLANES:(k + 1) * SUBLANES, :] += _fold8(d * pad_ref[pl.ds(base + off + k, CONV_R), :])


def _loop_rows(T, fn):
    def step(r, carry):
        fn(pl.multiple_of(r * CONV_R, CONV_R))
        return carry
    lax.fori_loop(0, T // CONV_R, step, 0)


def _col(T, off_blocks=0, rows=None):
    return pl.BlockSpec((T if rows is None else rows, LANES), lambda j: (0, j + off_blocks))


def _conv_call(body, name, T, n_tiles, in_specs, out_specs, out_shape, n_pad, n_padd=0):
    scratch = [pltpu.VMEM((T + CONV_PAD, LANES), F32)] * (n_pad + n_padd)
    return pl.pallas_call(body, name=name, grid=(n_tiles,), in_specs=in_specs, out_specs=out_specs, out_shape=out_shape,
                          scratch_shapes=scratch, compiler_params=_cparams(("parallel",)))


def _zero_head(ref):
    ref[0:CONV_PAD, :] = jnp.zeros((CONV_PAD, LANES), F32)


def _zero_tail(ref, T):
    ref[T:T + CONV_PAD, :] = jnp.zeros((CONV_PAD, LANES), F32)


def _sds(shape, dtype=F32):
    return jax.ShapeDtypeStruct(shape, dtype)


def _conv_a_fwd(ua, w, b, name):
    T = ua.shape[0]
    K, nt = CONV_A, D // LANES

    def body(al_ref, ag_ref, w_ref, b_ref, o_ref, pad_ref):
        _zero_head(pad_ref)

        def pre(base):
            pad_ref[pl.ds(base + CONV_PAD, CONV_R), :] = al_ref[pl.ds(base, CONV_R), :] * _sig(ag_ref[pl.ds(base, CONV_R), :])
        _loop_rows(T, pre)

        def main(base):
            o_ref[pl.ds(base, CONV_R), :] = _taps_fwd(pad_ref, w_ref, K, base) + b_ref[...]
        _loop_rows(T, main)

    return _conv_call(body, name, T, nt, [_col(T), _col(T, nt), _col(T, rows=K), _col(T, rows=1)], _col(T),
                      _sds((T, D)), 1)(ua, ua, w, b)


def _conv_a_bwd(ua, w, dac, name):
    T = ua.shape[0]
    K, nt = CONV_A, D // LANES

    def body(al_ref, ag_ref, w_ref, d_ref, dal_ref, dag_ref, dw_ref, db_ref, pad_ref, padd_ref):
        _zero_head(pad_ref)
        _zero_tail(padd_ref, T)
        dw_ref[...] = jnp.zeros_like(dw_ref)
        db_ref[...] = jnp.zeros_like(db_ref)

        def pre(base):
            rows = pl.ds(base, CONV_R)
            pad_ref[pl.ds(base + CONV_PAD, CONV_R), :] = al_ref[rows, :] * _sig(ag_ref[rows, :])
            padd_ref[rows, :] = d_ref[rows, :]
        _loop_rows(T, pre)

        def main(base):
            rows = pl.ds(base, CONV_R)
            d = d_ref[rows, :]
            _wgrad_acc(dw_ref, pad_ref, d, K, base)
            db_ref[...] += _fold8(d)
            da = _taps_bwd(padd_ref, w_ref, K, base)
            al, s = al_ref[rows, :], _sig(ag_ref[rows, :])
            dal_ref[rows, :] = da * s
            dag_ref[rows, :] = da * al * s * (1.0 - s)
        _loop_rows(T, main)

    return _conv_call(body, name, T, nt, [_col(T), _col(T, nt), _col(T, rows=K), _col(T)],
                      (_col(T), _col(T), _col(T, rows=K * SUBLANES), _col(T, rows=SUBLANES)),
                      (_sds((T, D)), _sds((T, D)), _sds((K * SUBLANES, D)), _sds((SUBLANES, D))), 1, 1)(ua, ua, w, dac)


def _conv_b_fwd(xu, w, b, name):
    T, C = xu.shape
    K, nt = CONV_B, C // LANES

    def body(x_ref, w_ref, b_ref, o_ref, pad_ref):
        _zero_head(pad_ref)
        pad_ref[CONV_PAD:CONV_PAD + T, :] = x_ref[...]

        def main(base):
            hc = _taps_fwd(pad_ref, w_ref, K, base) + b_ref[...]
            o_ref[pl.ds(base, CONV_R), :] = hc * _sig(hc)
        _loop_rows(T, main)

    return _conv_call(body, name, T, nt, [_col(T), _col(T, rows=K), _col(T, rows=1)], _col(T), _sds((T, C)), 1)(xu, w, b)


def _conv_b_bwd(xu, w, b, dxs, dbs, dcs, name):
    T, C = xu.shape
    K, nt = CONV_B, C // LANES
    nx, nb = dxs.shape[1] // LANES, dbs.shape[1] // LANES

    def body(x_ref, w_ref, b_ref, d1_ref, d2_ref, d3_ref, dx_ref, dw_ref, db_ref, pad_ref, padd_ref):
        j = pl.program_id(0)
        _zero_head(pad_ref)
        _zero_tail(padd_ref, T)
        dw_ref[...] = jnp.zeros_like(dw_ref)
        db_ref[...] = jnp.zeros_like(db_ref)
        pad_ref[CONV_PAD:CONV_PAD + T, :] = x_ref[...]

        def pre(base):
            rows = pl.ds(base, CONV_R)
            hc = _taps_fwd(pad_ref, w_ref, K, base) + b_ref[...]
            s = _sig(hc)
            d = jnp.where(j < nx, d1_ref[rows, :], jnp.where(j < nx + nb, d2_ref[rows, :], d3_ref[rows, :]))
            padd_ref[rows, :] = d * s * (1.0 + hc * (1.0 - s))
        _loop_rows(T, pre)

        def main(base):
            d = padd_ref[pl.ds(base, CONV_R), :]
            _wgrad_acc(dw_ref, pad_ref, d, K, base)
            db_ref[...] += _fold8(d)
            dx_ref[pl.ds(base, CONV_R), :] = _taps_bwd(padd_ref, w_ref, K, base)
        _loop_rows(T, main)

    def piece(lo, n):
        return pl.BlockSpec((T, LANES), lambda j: (0, jnp.clip(j - lo, 0, n - 1)))

    return _conv_call(body, name, T, nt,
                      [_col(T), _col(T, rows=K), _col(T, rows=1), piece(0, nx), piece(nx, nb), piece(nx + nb, nt - nx - nb)],
                      (_col(T), _col(T, rows=K * SUBLANES), _col(T, rows=SUBLANES)),
                      (_sds((T, C)), _sds((K * SUBLANES, C)), _sds((SUBLANES, C))), 1, 1)(xu, w, b, dxs, dbs, dcs)


def _conv_c_fwd(uo, w, name):
    T = uo.shape[0]
    K, nt = CONV_C, D // LANES

    def body(bg_ref, cg_ref, v_ref, w_ref, o_ref, pad_ref):
        _zero_head(pad_ref)
        pad_ref[CONV_PAD:CONV_PAD + T, :] = cg_ref[...] * v_ref[...]

        def main(base):
            rows = pl.ds(base, CONV_R)
            o_ref[rows, :] = (bg_ref[rows, :] * _taps_fwd(pad_ref, w_ref, K, base)).astype(BF16)
        _loop_rows(T, main)

    return _conv_call(body, name, T, nt, [_col(T), _col(T, nt), _col(T, 2 * nt), _col(T, rows=K)], _col(T),
                      _sds((T, D), BF16), 1)(uo, uo, uo, w)


def _conv_c_bwd(uo, w, dsc, name):
    T = uo.shape[0]
    K, nt = CONV_C, D // LANES

    def body(bg_ref, cg_ref, v_ref, w_ref, d_ref, dbg_ref, dcg_ref, dv_ref, dw_ref, pad_ref, padd_ref):
        _zero_head(pad_ref)
        _zero_tail(padd_ref, T)
        dw_ref[...] = jnp.zeros_like(dw_ref)
        pad_ref[CONV_PAD:CONV_PAD + T, :] = cg_ref[...] * v_ref[...]

        def pre(base):
            rows = pl.ds(base, CONV_R)
            d = d_ref[rows, :]
            dbg_ref[rows, :] = (d * _taps_fwd(pad_ref, w_ref, K, base)).astype(BF16)
            padd_ref[rows, :] = d * bg_ref[rows, :]
        _loop_rows(T, pre)

        def main(base):
            rows = pl.ds(base, CONV_R)
            _wgrad_acc(dw_ref, pad_ref, padd_ref[rows, :], K, base)
            dq = _taps_bwd(padd_ref, w_ref, K, base)
            dcg_ref[rows, :] = (dq * v_ref[rows, :]).astype(BF16)
            dv_ref[rows, :] = (dq * cg_ref[rows, :]).astype(BF16)
        _loop_rows(T, main)

    return _conv_call(body, name, T, nt, [_col(T), _col(T, nt), _col(T, 2 * nt), _col(T, rows=K), _col(T)],
                      (_col(T), _col(T), _col(T), _col(T, rows=K * SUBLANES)),
                      (_sds((T, D), BF16), _sds((T, D), BF16), _sds((T, D), BF16), _sds((K * SUBLANES, D))), 1, 1)(uo, uo, uo, w, dsc)


def _conv_f_fwd(up, w, b, name):
    T = up.shape[0]
    K, nt = CONV_F, D_FF // LANES

    def body(u1_ref, u2_ref, w1_ref, w2_ref, b1_ref, b2_ref, o_ref, pad1_ref, pad2_ref):
        _zero_head(pad1_ref)
        _zero_head(pad2_ref)
        pad1_ref[CONV_PAD:CONV_PAD + T, :] = u1_ref[...]
        pad2_ref[CONV_PAD:CONV_PAD + T, :] = u2_ref[...]

        def main(base):
            h1 = _taps_fwd(pad1_ref, w1_ref, K, base) + b1_ref[...]
            h2 = _taps_fwd(pad2_ref, w2_ref, K, base) + b2_ref[...]
            o_ref[pl.ds(base, CONV_R), :] = (h1 * _sig(h1) * h2).astype(BF16)
        _loop_rows(T, main)

    return _conv_call(body, name, T, nt,
                      [_col(T), _col(T, nt), _col(T, rows=K), _col(T, nt, rows=K), _col(T, rows=1), _col(T, nt, rows=1)],
                      _col(T), _sds((T, D_FF), BF16), 2)(up, up, w, w, b, b)


def _conv_f_bwd(up, w, b, dact, name):
    T = up.shape[0]
    K, nt = CONV_F, D_FF // LANES

    def body(u1_ref, u2_ref, w1_ref, w2_ref, b1_ref, b2_ref, d_ref, du1_ref, du2_ref, dw1_ref, dw2_ref, db1_ref, db2_ref,
             pad1_ref, pad2_ref, padd1_ref, padd2_ref):
        _zero_head(pad1_ref)
        _zero_head(pad2_ref)
        _zero_tail(padd1_ref, T)
        _zero_tail(padd2_ref, T)
        for r in (dw1_ref, dw2_ref, db1_ref, db2_ref):
            r[...] = jnp.zeros_like(r)
        pad1_ref[CONV_PAD:CONV_PAD + T, :] = u1_ref[...]
        pad2_ref[CONV_PAD:CONV_PAD + T, :] = u2_ref[...]

        def pre(base):
            rows = pl.ds(base, CONV_R)
            h1 = _taps_fwd(pad1_ref, w1_ref, K, base) + b1_ref[...]
            h2 = _taps_fwd(pad2_ref, w2_ref, K, base) + b2_ref[...]
            s = _sig(h1)
            d = d_ref[rows, :]
            padd1_ref[rows, :] = d * h2 * s * (1.0 + h1 * (1.0 - s))
            padd2_ref[rows, :] = d * h1 * s
        _loop_rows(T, pre)

        def main(base):
            rows = pl.ds(base, CONV_R)
            d1, d2 = padd1_ref[rows, :], padd2_ref[rows, :]
            _wgrad_acc(dw1_ref, pad1_ref, d1, K, base)
            _wgrad_acc(dw2_ref, pad2_ref, d2, K, base)
            db1_ref[...] += _fold8(d1)
            db2_ref[...] += _fold8(d2)
            du1_ref[rows, :] = _taps_bwd(padd1_ref, w1_ref, K, base).astype(BF16)
            du2_ref[rows, :] = _taps_bwd(padd2_ref, w2_ref, K, base).astype(BF16)
        _loop_rows(T, main)

    wrow, brow = _col(T, rows=K * SUBLANES), _col(T, rows=SUBLANES)
    return _conv_call(body, name, T, nt,
                      [_col(T), _col(T, nt), _col(T, rows=K), _col(T, nt, rows=K), _col(T, rows=1), _col(T, nt, rows=1), _col(T)],
                      (_col(T), _col(T), wrow, wrow, brow, brow),
                      (_sds((T, D_FF), BF16), _sds((T, D_FF), BF16), _sds((K * SUBLANES, D_FF)), _sds((K * SUBLANES, D_FF)),
                       _sds((SUBLANES, D_FF)), _sds((SUBLANES, D_FF))), 2, 2)(up, up, w, w, b, b, dact)


def _dot(a, b, dims="nn"):
    return lax.dot_general(a.astype(BF16), b.astype(BF16), _DIMS[dims], preferred_element_type=F32)


def _ssd_small(xcr_ref, xrr_ref, bc_ref, br_ref, ac_ref, ar_ref):
    Q = SSD_Q
    li = lax.broadcasted_iota(jnp.int32, (Q, Q), 0)
    si = lax.broadcasted_iota(jnp.int32, (Q, Q), 1)
    tril = li >= si
    dtc = jax.nn.softplus(xcr_ref[...] + bc_ref[...])
    dtr = jax.nn.softplus(xrr_ref[...] + br_ref[...])
    cumc = jnp.dot(tril.astype(F32), dtc * ac_ref[...], precision=HI, preferred_element_type=F32)
    cumr = jnp.dot(dtr * ar_ref[...], (li <= si).astype(F32), precision=HI, preferred_element_type=F32)
    return tril, dtc, dtr, cumc, cumr


def _ssd_specs(nc, rev):
    Q = SSD_Q
    cc = (lambda c: nc - 1 - c) if rev else (lambda c: c)
    x_spec = pl.BlockSpec((Q, 2 * LANES), lambda g, c: (cc(c), g))
    b_spec = pl.BlockSpec((Q, LANES), lambda g, c: (cc(c), 8 + g))
    c_spec = pl.BlockSpec((Q, LANES), lambda g, c: (cc(c), 12 + g))
    colm = pl.BlockSpec((None, Q, LANES), lambda g, c: (g, cc(c), 0))
    rowm = pl.BlockSpec((None, SUBLANES, Q), lambda g, c: (g, 0, cc(c)))
    colv = pl.BlockSpec((None, 1, LANES), lambda g, c: (g, 0, 0))
    rowv = pl.BlockSpec((None, SUBLANES, 1), lambda g, c: (g, 0, 0))
    st_spec = pl.BlockSpec((None, None, 2 * LANES, N_STATE), lambda g, c: (cc(c), g, 0, 0))
    return x_spec, b_spec, c_spec, colm, rowm, colv, rowv, st_spec


def _ssd_fwd(xc, raw_col, raw_row, bias_col, bias_row, a_col, a_row, dskip, name):
    T = xc.shape[0]
    Q = SSD_Q
    nc = T // Q
    x_spec, b_spec, c_spec, colm, rowm, colv, rowv, st_spec = _ssd_specs(nc, False)

    def body(dk_ref, x_ref, b_ref, c_ref, xcr_ref, xrr_ref, bc_ref, br_ref, ac_ref, ar_ref, y_ref, st_ref, h_ref):
        g = pl.program_id(0)

        @pl.when(pl.program_id(1) == 0)
        def _():
            h_ref[...] = jnp.zeros_like(h_ref)

        tril, dtc, dtr, cumc, cumr = _ssd_small(xcr_ref, xrr_ref, bc_ref, br_ref, ac_ref, ar_ref)
        Bm, Cm = b_ref[...], c_ref[...]
        S = _dot(Cm, Bm, "nt")
        lo = lax.broadcasted_iota(jnp.int32, (Q, LANES), 1) < HEAD_P
        rlo = lax.broadcasted_iota(jnp.int32, (LANES, N_STATE), 0) < HEAD_P
        st_ref[...] = h_ref[...]
        clast = cumc[Q - 1:Q, :]
        for pr in range(2):
            cols = slice(pr * LANES, (pr + 1) * LANES)
            xp = x_ref[:, cols]
            yd = jnp.zeros((Q, LANES), F32)
            for q in range(2):
                hh = 2 * pr + q
                seg = cumc[:, hh:hh + 1] - cumr[hh:hh + 1, :]
                lm = jnp.where(tril, jnp.exp(jnp.where(tril, seg, 0.0)), 0.0)
                w = S * lm * dtr[hh:hh + 1, :]
                xm = jnp.where(lo if q == 0 else jnp.logical_not(lo), xp, 0.0)
                yd = yd + _dot(w, xm)
            h0, h1 = 2 * pr, 2 * pr + 1
            c0, c1 = cumc[:, h0:h0 + 1], cumc[:, h1:h1 + 1]
            e_pair = jnp.where(lo, jnp.exp(c0), jnp.exp(c1))
            hp = h_ref[cols, :]
            ch = _dot(Cm, hp, "nt")
            dsk = jnp.where(lo, dk_ref[4 * g + h0], dk_ref[4 * g + h1])
            y_ref[:, cols] = yd + e_pair * ch + dsk * xp
            cl0, cl1 = clast[:, h0:h0 + 1], clast[:, h1:h1 + 1]
            sdec = jnp.where(lo, jnp.exp(cl0 - c0) * dtc[:, h0:h0 + 1], jnp.exp(cl1 - c1) * dtc[:, h1:h1 + 1])
            decrow = jnp.where(rlo, jnp.exp(cl0), jnp.exp(cl1))
            h_ref[cols, :] = hp * decrow + _dot(xp * sdec, Bm, "tn")

    smem = pl.BlockSpec(memory_space=pltpu.SMEM)
    return pl.pallas_call(
        body, name=name, grid=(N_GROUPS, nc),
        in_specs=[smem, x_spec, b_spec, c_spec, colm, rowm, colv, rowv, colv, rowv],
        out_specs=(x_spec, st_spec),
        out_shape=(_sds((T, D)), _sds((nc, N_GROUPS, 2 * LANES, N_STATE))),
        scratch_shapes=[pltpu.VMEM((2 * LANES, N_STATE), F32)],
        compiler_params=_cparams(("parallel", "arbitrary")))(dskip, xc, xc, xc, raw_col, raw_row, bias_col, bias_row, a_col, a_row)


def _ssd_bwd(xc, raw_col, raw_row, bias_col, bias_row, a_col, a_row, dskip, states, dy, name):
    T = xc.shape[0]
    Q = SSD_Q
    nc = T // Q
    x_spec, b_spec, c_spec, colm, rowm, colv, rowv, st_spec = _ssd_specs(nc, True)
    bo_spec = pl.BlockSpec((Q, LANES), lambda g, c: (nc - 1 - c, g))
    dd_spec = pl.BlockSpec((None, None, SUBLANES, 2 * LANES), lambda g, c: (nc - 1 - c, g, 0, 0))

    def body(dk_ref, x_ref, b_ref, c_ref, xcr_ref, xrr_ref, bc_ref, br_ref, ac_ref, ar_ref, st_ref, dy_ref,
             dx_ref, db_ref, dc_ref, sq_ref, cms_ref, ddac_ref, ddar_ref, dd_ref, dh_ref):
        g = pl.program_id(0)

        @pl.when(pl.program_id(1) == 0)
        def _():
            dh_ref[...] = jnp.zeros_like(dh_ref)

        tril, dtc, dtr, cumc, cumr = _ssd_small(xcr_ref, xrr_ref, bc_ref, br_ref, ac_ref, ar_ref)
        Bm, Cm = b_ref[...], c_ref[...]
        S = _dot(Cm, Bm, "nt")
        lane = lax.broadcasted_iota(jnp.int32, (Q, LANES), 1)
        sub = lax.broadcasted_iota(jnp.int32, (SUBLANES, Q), 0)
        rowi = lax.broadcasted_iota(jnp.int32, (Q, LANES), 0)
        lo = lane < HEAD_P
        rlo = lax.broadcasted_iota(jnp.int32, (LANES, N_STATE), 0) < HEAD_P
        clast = cumc[Q - 1:Q, :]
        ds_g = jnp.zeros((Q, Q), F32)
        dcm = jnp.zeros((Q, N_STATE), F32)
        dbm = jnp.zeros((Q, N_STATE), F32)
        dcum_col = jnp.zeros((Q, LANES), F32)
        dcum_row = jnp.zeros((SUBLANES, Q), F32)
        sq_col = jnp.zeros((Q, LANES), F32)
        cms_row = jnp.zeros((SUBLANES, Q), F32)
        for pr in range(2):
            cols = slice(pr * LANES, (pr + 1) * LANES)
            xp, dyp = x_ref[:, cols], dy_ref[:, cols]
            hin, dhp = st_ref[cols, :], dh_ref[cols, :]
            h0, h1 = 2 * pr, 2 * pr + 1
            c0, c1 = cumc[:, h0:h0 + 1], cumc[:, h1:h1 + 1]
            cl0, cl1 = clast[:, h0:h0 + 1], clast[:, h1:h1 + 1]
            e_pair = jnp.where(lo, jnp.exp(c0), jnp.exp(c1))
            edec = jnp.where(lo, jnp.exp(cl0 - c0), jnp.exp(cl1 - c1))
            dt_pair = jnp.where(lo, dtc[:, h0:h0 + 1], dtc[:, h1:h1 + 1])
            sdec = edec * dt_pair
            ch = _dot(Cm, hin, "nt")
            xb = _dot(Bm, dhp, "nt")
            dye = dyp * e_pair
            t1 = dye * ch
            t2 = xp * xb * edec
            hh_prod = dhp * hin
            dsk = jnp.where(lo, dk_ref[4 * g + h0], dk_ref[4 * g + h1])
            dxp = sdec * xb + dsk * dyp
            for q in range(2):
                hh = 2 * pr + q
                mine = lo if q == 0 else jnp.logical_not(lo)
                seg = cumc[:, hh:hh + 1] - cumr[hh:hh + 1, :]
                lm = jnp.where(tril, jnp.exp(jnp.where(tril, seg, 0.0)), 0.0)
                dtrow = dtr[hh:hh + 1, :]
                w = S * lm * dtrow
                dym = jnp.where(mine, dyp, 0.0)
                gl = _dot(dym, xp, "nt") * lm
                ds_g = ds_g + gl * dtrow
                ms = gl * S
                m = ms * dtrow
                dxp = dxp + _dot(w, dym, "tn")
                cms_row = jnp.where(sub == hh, jnp.sum(ms, axis=0, keepdims=True), cms_row)
                dcum_row = jnp.where(sub == hh, -jnp.sum(m, axis=0, keepdims=True), dcum_row)
                t1h = jnp.sum(jnp.where(mine, t1, 0.0), axis=1, keepdims=True)
                sqh = jnp.sum(jnp.where(mine, t2, 0.0), axis=1, keepdims=True)
                sth = sqh * dtc[:, hh:hh + 1]
                rmine = rlo if q == 0 else jnp.logical_not(rlo)
                hsum = jnp.sum(jnp.sum(jnp.where(rmine, hh_prod, 0.0), axis=1, keepdims=True), axis=0, keepdims=True)
                last = jnp.sum(sth, axis=0, keepdims=True) + jnp.exp(clast[:, hh:hh + 1]) * hsum
                dcol = jnp.sum(m, axis=1, keepdims=True) + t1h - sth
                dcum_col = jnp.where(lane == hh, dcol + jnp.where(rowi == Q - 1, last, 0.0), dcum_col)
                sq_col = jnp.where(lane == hh, sqh, sq_col)
            dcm = dcm + _dot(dye, hin)
            dbm = dbm + _dot(xp * sdec, dhp)
            decrow = jnp.where(rlo, jnp.exp(cl0), jnp.exp(cl1))
            dh_ref[cols, :] = dhp * decrow + _dot(dye, Cm, "tn")
            dx_ref[:, cols] = dxp
            dd_ref[:, cols] = jnp.broadcast_to(jnp.sum(dyp * xp, axis=0, keepdims=True), (SUBLANES, LANES))
        dc_ref[...] = dcm + _dot(ds_g, Bm)
        db_ref[...] = dbm + _dot(ds_g, Cm, "tn")
        li = lax.broadcasted_iota(jnp.int32, (Q, Q), 0)
        si = lax.broadcasted_iota(jnp.int32, (Q, Q), 1)
        ddac_ref[...] = jnp.dot((li <= si).astype(F32), dcum_col, precision=HI, preferred_element_type=F32)
        ddar_ref[...] = jnp.dot(dcum_row, tril.astype(F32), precision=HI, preferred_element_type=F32)
        sq_ref[...] = sq_col
        cms_ref[...] = cms_row

    smem = pl.BlockSpec(memory_space=pltpu.SMEM)
    return pl.pallas_call(
        body, name=name, grid=(N_GROUPS, nc),
        in_specs=[smem, x_spec, b_spec, c_spec, colm, rowm, colv, rowv, colv, rowv, st_spec, x_spec],
        out_specs=(x_spec, bo_spec, bo_spec, colm, rowm, colm, rowm, dd_spec),
        out_shape=(_sds((T, D)), _sds((T, D // 2)), _sds((T, D // 2)), _sds((N_GROUPS, T, LANES)), _sds((N_GROUPS, SUBLANES, T)),
                   _sds((N_GROUPS, T, LANES)), _sds((N_GROUPS, SUBLANES, T)), _sds((nc, N_GROUPS, SUBLANES, 2 * LANES))),
        scratch_shapes=[pltpu.VMEM((2 * LANES, N_STATE), F32)],
        compiler_params=_cparams(("parallel", "arbitrary")))(dskip, xc, xc, xc, raw_col, raw_row, bias_col, bias_row, a_col, a_row,
                                                            states, dy)


def _adamw(w, g, m, v, name):
    shape = w.shape
    cols = shape[-1]
    w2, g2, m2, v2 = (t.reshape(-1, cols) for t in (w, g, m, v))
    rows = w2.shape[0]
    tr = 256 if (rows % 256 == 0 and rows > 256) else rows
    c1 = 1.0 - ADAM_B1 ** ADAM_STEP
    c2 = 1.0 - ADAM_B2 ** ADAM_STEP

    def body(w_ref, g_ref, m_ref, v_ref, d_ref, mo_ref, vo_ref):
        gv = g_ref[...]
        mn = ADAM_B1 * m_ref[...] + (1.0 - ADAM_B1) * gv
        vn = ADAM_B2 * v_ref[...] + (1.0 - ADAM_B2) * (gv * gv)
        d_ref[...] = -ADAM_LR * ((mn / c1) / (jnp.sqrt(vn / c2) + ADAM_EPS) + ADAM_WD * w_ref[...])
        mo_ref[...] = mn
        vo_ref[...] = vn

    spec = pl.BlockSpec((tr, cols), lambda i: (i, 0))
    out = pl.pallas_call(body, name=name, grid=(rows // tr,), in_specs=[spec] * 4, out_specs=(spec,) * 3,
                         out_shape=(_sds((rows, cols)),) * 3, compiler_params=_cparams(("parallel",)))(w2, g2, m2, v2)
    return tuple(o.reshape(shape) for o in out)


def _place():
    x, y, c = lax.axis_index("x"), lax.axis_index("y"), lax.axis_index("c")
    chips = [(1 - x, y), (x, 1 - y), (1 - x, 1 - y)]
    return x, y, c, chips


_ANY = pl.BlockSpec(memory_space=pl.ANY)


TENSORS = (("e_w_in", "row", 2, 4096, 1284, 1024), ("e_w_out", "row", 2, 2048, 1024, 512), ("o_w_in", "col", 2, 1024, 3072, 768),
           ("o_w_out", "row", 2, 1024, 1024, 256), ("f_w_up", "col", 4, 1024, 5632, 1408), ("f_w_down", "row", 4, 2816, 1024, 704),
           ("ple_w_proj", "col", 4, 256, 1024, 256), ("ple_w_gate", "row", 4, 1024, 1024, 256))
W_GROUPS = ((0,), (1, 2, 3))
G_GROUPS = ((1, 2, 3), (0,))


def _tensor_layer(name, layer):
    if name.startswith("e_"):
        return layer // 2 if layer % 2 == 0 else None
    if name.startswith("o_"):
        return layer // 2 if layer % 2 == 1 else None
    return layer


def _group_items(layers):
    items = []
    for name, kind, L, A, B, n in TENSORS:
        tls = sorted(t for t in (_tensor_layer(name, l) for l in layers) if t is not None)
        if tls:
            assert tls == list(range(tls[0], tls[0] + len(tls)))
            items.append((name, kind, len(tls), A, B, n, tls[0]))
    return items


def _hwin(ref, it, k, h):
    name, kind, Lg, A, B, n, l0 = it
    if kind == "row":
        return ref.at[:, pl.ds(pl.multiple_of(k * n + h * (n // 2), 16), n // 2), :]
    return ref.at[:, pl.ds(pl.multiple_of(h * (A // 2), 16), A // 2), pl.ds(pl.multiple_of(k * n, LANES), n)]


def _shard_dims(kind, A, B, n):
    return (n, B) if kind == "row" else (A, n)


def _cast_into(w, it, me):
    name, kind, Lg, A, B, n, l0 = it
    As, Bs = _shard_dims(kind, A, B, n)

    def body(me_ref, w_ref, o_ref):
        o_ref[...] = w_ref[...].astype(BF16)

    omap = (lambda l, m: (l, m[0], 0)) if kind == "row" else (lambda l, m: (l, 0, m[0]))
    grid_spec = pltpu.PrefetchScalarGridSpec(
        num_scalar_prefetch=1, grid=(Lg,), in_specs=[pl.BlockSpec((None, As, Bs), lambda l, m: (l + l0, 0, 0))],
        out_specs=pl.BlockSpec((None, As, Bs), omap))
    return pl.pallas_call(body, name=f"cast_{name}_{l0}", grid_spec=grid_spec, out_shape=_sds((Lg, A, B), BF16),
                          compiler_params=_cparams(("parallel",)))(me, w.reshape(-1, As, Bs))


_HBM = pl.BlockSpec(memory_space=pltpu.HBM)
_SEM = pl.BlockSpec(memory_space=pltpu.SEMAPHORE)
_EFFECT = pltpu.SideEffectType.DATAFLOW_SIDE_EFFECTING


def _hbm(a):
    return pltpu.with_memory_space_constraint(a, pltpu.HBM)


def _split_start(thru, n_copies, issue, name):
    N = len(thru)

    def body(*refs):
        outs = refs[N:2 * N]
        send_sems, recv_sems, token = refs[2 * N:]
        for cp in issue(outs, send_sems, recv_sems):
            cp.start()
        token[...] = jnp.zeros_like(token)

    out = pl.pallas_call(
        body, name=name, in_specs=[_HBM] * N, out_specs=(_HBM,) * N + (_SEM, _SEM, pl.BlockSpec(memory_space=pltpu.VMEM)),
        out_shape=tuple(pltpu.HBM(a.shape, a.dtype) for a in thru)
        + (pltpu.SemaphoreType.DMA((n_copies,)), pltpu.SemaphoreType.DMA((n_copies,)), _sds((SUBLANES, LANES))),
        input_output_aliases={t: t for t in range(N)},
        compiler_params=pltpu.CompilerParams(has_side_effects=_EFFECT))(*[_hbm(a) for a in thru])
    return list(out[:N]), out[N], out[N + 1], out[N + 2]


def _split_wait(thru, send_sems, recv_sems, after, waits, name):
    N = len(thru)

    def body(*refs):
        ins = refs[:N]
        for cp, side in waits(ins, refs[N], refs[N + 1]):
            if side == "send":
                cp.wait_send()
            else:
                cp.wait_recv()

    out = pl.pallas_call(
        body, name=name, in_specs=[_HBM] * N + [_SEM, _SEM, _ANY], out_specs=(_HBM,) * N,
        out_shape=tuple(pltpu.HBM(a.shape, a.dtype) for a in thru), input_output_aliases={t: t for t in range(N)},
        compiler_params=pltpu.CompilerParams(has_side_effects=_EFFECT))(*thru, send_sems, recv_sems, after)
    return list(out)


def _rcopy(send_sems, recv_sems, k, src, dst, to):
    return pltpu.make_async_remote_copy(src_ref=src, dst_ref=dst, send_sem=send_sems.at[k], recv_sem=recv_sems.at[k],
                                        device_id=to, device_id_type=MESH)


def _gather_copies(items, refs, send_sems, recv_sems, what):
    x, y, c, chips = _place()
    me = 2 * x + y
    out = []
    for t, it in enumerate(items):
        mine = _hwin(refs[t], it, me, c)
        for j, (px, py) in enumerate(chips):
            if what == "start":
                out.append(_rcopy(send_sems, recv_sems, 3 * t + j, mine, mine, (px, py, c)))
            else:
                slot = _hwin(refs[t], it, 2 * px + py, c)
                out.append((_rcopy(send_sems, recv_sems, 3 * t + j, mine, mine, (px, py, c)), "send"))
                out.append((_rcopy(send_sems, recv_sems, 3 * t + j, slot, slot, (px, py, c)), "recv"))
    return out


def _gather_start(fulls, items, name):
    return _split_start(fulls, 3 * len(items), functools.partial(_gather_copies, items, what="start"), name)


def _gather_wait(fulls, send_sems, recv_sems, after, items, name):
    return _split_wait(fulls, send_sems, recv_sems, after, functools.partial(_gather_copies, items, what="wait"), name)


def _gather_fwd(fulls, items, name, ws=None):
    N = len(fulls)
    has_ws = ws is not None

    def body(*refs):
        outs = refs[N + has_ws:2 * N + has_ws]
        rest = refs[2 * N + has_ws:]
        x, y, c, chips = _place()
        me = 2 * x + y
        sib = (x, y, 1 - c)
        if has_ws:
            ws_ref = refs[N]
            WS_ref, send_sems, recv_sems, lsem = rest
            loc = pltpu.make_async_copy(ws_ref, WS_ref.at[me], lsem)
            loc.start()
        else:
            send_sems, recv_sems = rest
        rc = functools.partial(_rcopy, send_sems, recv_sems)
        cps = []
        for t, it in enumerate(items):
            for j, (px, py) in enumerate(chips):
                slot = _hwin(outs[t], it, 2 * px + py, c)
                cps.append(rc(3 * t + j, slot, slot, sib))
        if has_ws:
            cps += [rc(3 * N + j, ws_ref, WS_ref.at[me], (*chip, c)) for j, chip in enumerate(chips)]
        for cp in cps:
            cp.start()
        for t, it in enumerate(items):
            for j, (px, py) in enumerate(chips):
                oslot = _hwin(outs[t], it, 2 * px + py, 1 - c)
                rc(3 * t + j, oslot, oslot, sib).wait_recv()
        if has_ws:
            for j, (px, py) in enumerate(chips):
                sslot = WS_ref.at[2 * px + py]
                rc(3 * N + j, sslot, sslot, sib).wait_recv()
        for cp in cps:
            cp.wait_send()
        if has_ws:
            loc.wait()

    ns = 3 * N + (3 if has_ws else 0)
    out_shape = tuple(_sds(f.shape, f.dtype) for f in fulls)
    scratch = [pltpu.SemaphoreType.DMA((ns,)), pltpu.SemaphoreType.DMA((ns,))]
    args = list(fulls)
    if has_ws:
        out_shape += (_sds((4,) + ws.shape, ws.dtype),)
        scratch.append(pltpu.SemaphoreType.DMA(()))
        args.append(ws)
    out = pl.pallas_call(
        body, name=name, in_specs=[_ANY] * len(args), out_specs=(_ANY,) * len(out_shape), out_shape=out_shape,
        input_output_aliases={t: t for t in range(N)}, scratch_shapes=scratch,
        compiler_params=pltpu.CompilerParams(has_side_effects=True))(*args)
    return (list(out[:N]), out[N]) if has_ws else (list(out), None)


def _half_shape(it):
    name, kind, Lg, A, B, n, l0 = it
    return (Lg, 4, n // 2, B) if kind == "row" else (Lg, A // 2, B)


def _piece_shape(it):
    name, kind, Lg, A, B, n, l0 = it
    return (Lg, n // 2, B) if kind == "row" else (Lg, A // 2, n)


def _swap_grads(gs, items, name):
    N = len(gs)

    def body(*refs):
        g_refs, o_refs = refs[:N], refs[N:2 * N]
        send_sems, recv_sems = refs[2 * N:]
        x, y, c, _ = _place()
        sib = (x, y, 1 - c)
        cps = []
        for t, it in enumerate(items):
            name_, kind, Lg, A, B, n, l0 = it
            if kind == "row":
                for k in range(4):
                    cps.append(_rcopy(send_sems, recv_sems, 4 * t + k, _hwin(g_refs[t], it, k, 1 - c), o_refs[t].at[:, k], sib))
            else:
                src = g_refs[t].at[:, pl.ds(pl.multiple_of((1 - c) * (A // 2), 16), A // 2), :]
                cps.append(_rcopy(send_sems, recv_sems, 4 * t, src, o_refs[t], sib))
        for cp in cps:
            cp.start()
        for cp in cps:
            cp.wait()

    return pl.pallas_call(
        body, name=name, in_specs=[_ANY] * N, out_specs=(_ANY,) * N, out_shape=tuple(_sds(_half_shape(it)) for it in items),
        scratch_shapes=[pltpu.SemaphoreType.DMA((4 * N,)), pltpu.SemaphoreType.DMA((4 * N,))],
        compiler_params=pltpu.CompilerParams(has_side_effects=True))(*gs)


def _add_half(g, ra, it, cvec):
    name, kind, Lg, A, B, n, l0 = it
    if kind == "row":
        blk = (None, n // 2, B)
        grid = (Lg, 4)
        g_spec = pl.BlockSpec(blk, lambda l, k, cr: (l, 2 * k + cr[0], 0))
        h_spec = pl.BlockSpec((None, None, n // 2, B), lambda l, k, cr: (l, k, 0, 0))
    else:
        tr = _tile(A // 2, [], (256, 128))
        nb = (A // 2) // tr
        grid = (Lg, nb)
        g_spec = pl.BlockSpec((None, tr, B), lambda l, i, cr: (l, cr[0] * nb + i, 0))
        h_spec = pl.BlockSpec((None, tr, B), lambda l, i, cr: (l, i, 0))

    def body(c_ref, g_ref, r_ref, o_ref):
        o_ref[...] = (g_ref[...] + r_ref[...]).astype(BF16)

    grid_spec = pltpu.PrefetchScalarGridSpec(num_scalar_prefetch=1, grid=grid, in_specs=[g_spec, h_spec], out_specs=h_spec)
    return pl.pallas_call(body, name=f"addhalf_{name}_{l0}", grid_spec=grid_spec, out_shape=_sds(_half_shape(it), BF16),
                          compiler_params=_cparams(("parallel", "parallel")))(cvec, g, ra)


def _scatter_copies(items, refs, send_sems, recv_sems, what):
    N = len(items)
    x, y, c, chips = _place()
    out = []
    for t, it in enumerate(items):
        name, kind, Lg, A, B, n, l0 = it
        for j, (px, py) in enumerate(chips):
            k = 2 * px + py
            src = refs[t].at[:, k] if kind == "row" else refs[t].at[:, :, pl.ds(pl.multiple_of(k * n, LANES), n)]
            cp = _rcopy(send_sems, recv_sems, 3 * t + j, src, refs[N + t].at[j], (px, py, c))
            if what == "start":
                out.append(cp)
            else:
                out += [(cp, "send"), (cp, "recv")]
    return out


def _scatter_start(ps, items, name):
    lands = [lax.empty((3,) + _piece_shape(it), BF16) for it in items]
    return _split_start(list(ps) + lands, 3 * len(items), functools.partial(_scatter_copies, items, what="start"), name)


def _scatter_wait(thru, send_sems, recv_sems, after, items, name):
    return _split_wait(thru, send_sems, recv_sems, after, functools.partial(_scatter_copies, items, what="wait"), name)


def _sum_own(p, rc, it, mevec, buf):
    name, kind, Lg, A, B, n, l0 = it
    As, Bs = _shard_dims(kind, A, B, n)
    L = [s[2] for s in TENSORS if s[0] == name][0]
    hb = (As // 2, Bs)
    has_buf = buf is not None

    def body(*refs):
        p_ref, r0, r1, r2 = refs[1:5]
        o_ref = refs[5 + has_buf]
        o_ref[...] = ((p_ref[...].astype(F32) + r0[...].astype(F32)) + r1[...].astype(F32)) + r2[...].astype(F32)

    if kind == "row":
        p_spec = pl.BlockSpec((None, None) + hb, lambda l, m: (l, m[0], 0, 0))
    else:
        p_spec = pl.BlockSpec((None,) + hb, lambda l, m: (l, 0, m[0]))
    r_specs = [pl.BlockSpec((None, None) + hb, functools.partial(lambda l, m, j: (j, l, 0, 0), j=j)) for j in range(3)]
    in_specs = [p_spec] + r_specs + ([_ANY] if has_buf else [])
    grid_spec = pltpu.PrefetchScalarGridSpec(num_scalar_prefetch=1, grid=(Lg,), in_specs=in_specs,
                                             out_specs=pl.BlockSpec((None,) + hb, lambda l, m: (l + l0, m[1], 0)))
    args = (mevec, p, rc, rc, rc) + ((buf,) if has_buf else ())
    return pl.pallas_call(body, name=f"sumown_{name}_{l0}", grid_spec=grid_spec, out_shape=_sds((L, As, Bs)),
                          input_output_aliases={5: 0} if has_buf else {}, compiler_params=_cparams(("parallel",)))(*args)


def _join_halves(rs, items, name):
    N = len(rs)

    def body(*refs):
        outs = refs[N:2 * N]
        send_sems, recv_sems = refs[2 * N:]
        x, y, c, _ = _place()
        sib = (x, y, 1 - c)

        def half(t, h):
            name_, kind, Lg, A, B, n, l0 = items[t]
            hr = _shard_dims(kind, A, B, n)[0] // 2
            return outs[t].at[pl.ds(l0, Lg), pl.ds(pl.multiple_of(h * hr, SUBLANES), hr), :]

        cps = [_rcopy(send_sems, recv_sems, t, half(t, c), half(t, c), sib) for t in range(N)]
        for cp in cps:
            cp.start()
        for t in range(N):
            _rcopy(send_sems, recv_sems, t, half(t, 1 - c), half(t, 1 - c), sib).wait_recv()
        for cp in cps:
            cp.wait_send()

    return list(pl.pallas_call(
        body, name=name, in_specs=[_ANY] * N, out_specs=(_ANY,) * N, out_shape=tuple(_sds(r.shape, r.dtype) for r in rs),
        input_output_aliases={t: t for t in range(N)},
        scratch_shapes=[pltpu.SemaphoreType.DMA((N,)), pltpu.SemaphoreType.DMA((N,))],
        compiler_params=pltpu.CompilerParams(has_side_effects=True))(*rs))


def _allgather_small(v):
    m_per, n = v.shape

    def body(x_ref, out_ref, send_sems, recv_sems, local_sem):
        x, y, c, chips = _place()
        me, sibling = (x, y, c), (x, y, 1 - c)

        def rows(px, py, pc):
            return out_ref.at[pl.ds(pl.multiple_of((4 * px + 2 * py + pc) * m_per, SUBLANES), m_per), :]

        def copy(k, block, to, src=None):
            return pltpu.make_async_remote_copy(src_ref=rows(*block) if src is None else src, dst_ref=rows(*block),
                                                send_sem=send_sems.at[k], recv_sem=recv_sems.at[k], device_id=to, device_id_type=MESH)

        mine = pltpu.make_async_copy(x_ref, rows(*me), local_sem)
        mine.start()
        first = [copy(0, me, sibling, src=x_ref)]
        first += [copy(1 + j, me, (*chip, c), src=x_ref) for j, chip in enumerate(chips)]
        for cp in first:
            cp.start()
        passed = [copy(4 + j, (*chip, c), sibling) for j, chip in enumerate(chips)]
        for j, chip in enumerate(chips):
            copy(1 + j, (*chip, c), me).wait_recv()
            passed[j].start()
        copy(0, sibling, me).wait_recv()
        for j, chip in enumerate(chips):
            copy(4 + j, (*chip, 1 - c), me).wait_recv()
        for cp in first + passed:
            cp.wait_send()
        mine.wait()

    vm = pl.BlockSpec(memory_space=pltpu.VMEM)
    return pl.pallas_call(body, name="allgather_small", in_specs=[vm], out_specs=vm, out_shape=_sds((8 * m_per, n)),
                          scratch_shapes=[pltpu.SemaphoreType.DMA((7,)), pltpu.SemaphoreType.DMA((7,)), pltpu.SemaphoreType.DMA(())],
                          compiler_params=pltpu.CompilerParams(has_side_effects=True, vmem_limit_bytes=VMEM_LIMIT))(v)


def _sum8(v, m_per):
    def body(v_ref, o_ref):
        acc = v_ref[0:m_per, :]
        for k in range(1, 8):
            acc = acc + v_ref[k * m_per:(k + 1) * m_per, :]
        o_ref[...] = acc

    return pl.pallas_call(body, name="small_sum_devices", out_shape=_sds((m_per, v.shape[1])),
                          compiler_params=pltpu.CompilerParams(vmem_limit_bytes=VMEM_LIMIT))(v)


SMALL_SHARDED = (("e_conv_a_w", 2), ("e_conv_b_w", 2), ("o_conv_w", 2), ("f_conv_w", 2), ("ln_g", 2), ("ln_b", 2))
SMALL_REPL = ("e_conv_a_b", "e_ln_a_g", "e_ln_a_b", "e_conv_b_b", "e_dt_bias", "e_a_log", "e_d_skip", "e_norm_b_g", "f_conv_b")

WEIGHT_ORDER = ('e_w_in', 'e_conv_a_w', 'e_conv_a_b', 'e_ln_a_g', 'e_ln_a_b', 'e_conv_b_w', 'e_conv_b_b', 'e_dt_bias', 'e_a_log',
                'e_d_skip', 'e_norm_b_g', 'e_w_out', 'o_w_in', 'o_conv_w', 'o_w_out', 'f_w_up', 'f_conv_w', 'f_conv_b', 'f_w_down',
                'ple_w_proj', 'ple_w_gate', 'ln_g', 'ln_b')


def _pack_rows(parts, width, total_rows, dtype):
    flat = jnp.concatenate([p.reshape(-1).astype(dtype) for p in parts])
    flat = jnp.pad(flat, (0, total_rows * width - flat.shape[0]))
    return flat.reshape(total_rows, width)


def _unpack_rows(buf, shapes):
    flat = buf.reshape(-1)
    out, pos = [], 0
    for s in shapes:
        n = math.prod(s)
        out.append(flat[pos:pos + n].reshape(s))
        pos += n
    return out


def _small_rows(shapes):
    n = sum(math.prod(s) for s in shapes)
    return -(-n // (LANES * SUBLANES)) * SUBLANES


E_PAD = 5248
SEG_A, SEG_Z, SEG_X, SEG_DT = (0, 2 * D), (2 * D, D), (3 * D, 2 * D), (5 * D, LANES)
G_SHAPES = {"e_w_in": (2, D, E_PAD), "e_w_out": (2, 2 * D, D), "o_w_in": (2, D, 3 * D), "o_w_out": (2, D, D),
            "f_w_up": (4, D, 2 * D_FF), "f_w_down": (4, D_FF, D), "ple_w_proj": (4, PLE, D), "ple_w_gate": (4, D, D)}


def _padcols(w, width):
    return jnp.pad(w, ((0, 0), (0, width - w.shape[1])))


def _fold_rows(dw, K):
    return dw.reshape(K, SUBLANES, dw.shape[-1]).sum(1)


class GradBuffers(dict):
    def __init__(self):
        super().__init__()
        self.where = {}
        for gi, layers in enumerate(G_GROUPS):
            for name, kind, Lg, A, B, n, l0 in _group_items(layers):
                for k in range(Lg):
                    self.where[(name, l0 + k)] = (gi, k, Lg)
        self.current = {}

    def into(self, name, layer, r0=0, c0=0):
        gi, k, Lg = self.where[(name, layer)]
        self.current[name] = (name, gi)
        return (self.get((name, gi)), (Lg,) + G_SHAPES[name][1:], (k,), r0, c0)

    def __setitem__(self, name, value):
        super().__setitem__(self.current[name], value)


def _local_step(x, p, target, W, comm=None):
    T = x.shape[0]
    xb = x
    saved = []
    xc_f = x
    for i in range(DEPTH):
        j = i // 2
        L = {}
        L["x"], L["xb"] = xc_f, xb
        tok = comm.layer_starts(i, xb) if comm is not None else None
        if i % 2 == 0:
            def w_in(seg, c0=0, cols=None, j=j):
                return V(W["e_w_in"], (j,), c0=seg[0] + c0, cols=seg[1] if cols is None else cols)

            ua = _mm(xb, w_in(SEG_A), "nn", f"l{i}_in_a", after=tok)
            z = _mm(xb, w_in(SEG_Z), "nn", f"l{i}_in_z")
            xu = _mm(xb, w_in(SEG_X), "nn", f"l{i}_in_xbc")
            udt = _mm(xb, w_in(SEG_DT), "nn", f"l{i}_in_dt")
            ac = _conv_a_fwd(ua, W["e_conv_a_w"][j], W["e_conv_a_b"][j][None], f"l{i}_conv_a")
            ya = _ln_silu_fwd(ac, W["e_ln_a_g"][j][None], W["e_ln_a_b"][j][None], f"l{i}_ln_a")
            xc = _conv_b_fwd(xu, W["e_conv_b_w"][j], W["e_conv_b_b"][j][None], f"l{i}_conv_b")
            sm = _ssd_small_inputs(udt[:, :N_HEADS], W["e_dt_bias"][j], W["e_a_log"][j])
            y, states = _ssd_fwd(xc, *sm, W["e_d_skip"][j], f"l{i}_ssd")
            yb = _gate_rms_fwd(y, z, W["e_norm_b_g"][j][None], f"l{i}_gate_rms")
            mix = _mm(ya, V(W["e_w_out"], (j,), rows=D), "nn", f"l{i}_out_a")
            mix = _mm(yb, V(W["e_w_out"], (j,), r0=D), "nn", f"l{i}_out_b", add=mix)
            L.update(ua=ua, z=z, xu=xu, udt=udt, ac=ac, ya=ya, xc=xc, sm=sm, y=y, states=states, yb=yb, w_in=w_in)
        else:
            uo = _mm(xb, V(W["o_w_in"], (j,)), "nn", f"l{i}_in", after=tok)
            sc = _conv_c_fwd(uo, W["o_conv_w"][j], f"l{i}_conv_c")
            mix = _mm(sc, V(W["o_w_out"], (j,)), "nn", f"l{i}_out")
            L.update(uo=uo, sc=sc)
        h1, x1, x1b = _res_ln_fwd(xc_f, [mix], None, W["ln_g"][i, 0][None], W["ln_b"][i, 0][None], f"l{i}_ln1")
        up = _mm(x1b, V(W["f_w_up"], (i,)), "nn", f"l{i}_ffn_up")
        act = _conv_f_fwd(up, W["f_conv_w"][i], W["f_conv_b"][i][None], f"l{i}_conv_f")
        ffn = _mm(act, V(W["f_w_down"], (i,)), "nn", f"l{i}_ffn_down")
        pv = V(p, (i, 0))
        pp = _mm(pv, V(W["ple_w_proj"], (i,)), "nn", f"l{i}_ple_proj")
        gl = _mm(x1b, V(W["ple_w_gate"], (i,)), "nn", f"l{i}_ple_gate")
        h2, x2, x2b = _res_ln_fwd(x1, [ffn], (pp, gl), W["ln_g"][i, 1][None], W["ln_b"][i, 1][None], f"l{i}_ln2")
        L.update(h1=h1, x1=x1, x1b=x1b, up=up, act=act, pv=pv, pp=pp, gl=gl, h2=h2)
        saved.append(L)
        xc_f, xb = x2, x2b

    sq, dx = _loss_head(xc_f, target, "loss_head")

    GB = GradBuffers()
    into = GB.into
    tok = None

    G = {n: [None] * (DEPTH if n.startswith(("f_", "ln_")) else DEPTH // 2) for n in WEIGHT_ORDER if n not in G_SHAPES}
    for i in reversed(range(DEPTH)):
        j = i // 2
        L = saved[i]
        dh2, dh2b, dg2, db2, dpp, dgl = _res_ln_bwd(dx, L["h2"], W["ln_g"][i, 1][None], (L["pp"], L["gl"]), f"l{i}_ln2_bwd")
        GB["f_w_down"] = _mm(L["act"], dh2b, "tn", f"l{i}_dw_down", dst=into("f_w_down", i), after=tok)
        dact = _mm(dh2b, V(W["f_w_down"], (i,)), "nt", f"l{i}_dact")
        du1, du2, dw1, dw2, dbf1, dbf2 = _conv_f_bwd(L["up"], W["f_conv_w"][i], W["f_conv_b"][i][None], dact, f"l{i}_conv_f_bwd")
        G["f_conv_w"][i] = jnp.concatenate([_fold_rows(dw1, CONV_F), _fold_rows(dw2, CONV_F)], axis=1)
        G["f_conv_b"][i] = jnp.concatenate([dbf1.sum(0), dbf2.sum(0)])
        GB["f_w_up"] = _mm(L["x1b"], du1, "tn", f"l{i}_dw_up1", dst=into("f_w_up", i))
        GB["f_w_up"] = _mm(L["x1b"], du2, "tn", f"l{i}_dw_up2", dst=into("f_w_up", i, c0=D_FF))
        GB["ple_w_proj"] = _mm(L["pv"], dpp, "tn", f"l{i}_dw_proj", dst=into("ple_w_proj", i))
        GB["ple_w_gate"] = _mm(L["x1b"], dgl, "tn", f"l{i}_dw_gate", dst=into("ple_w_gate", i))
        dx1 = _mm(du1, V(W["f_w_up"], (i,), cols=D_FF), "nt", f"l{i}_dx1_a", add=dh2, add_scale=ALPHA)
        dx1 = _mm(du2, V(W["f_w_up"], (i,), c0=D_FF), "nt", f"l{i}_dx1_b", add=dx1)
        dx1 = _mm(dgl, V(W["ple_w_gate"], (i,)), "nt", f"l{i}_dx1_c", add=dx1)
        dh1, dh1b, dg1, db1 = _res_ln_bwd(dx1, L["h1"], W["ln_g"][i, 0][None], None, f"l{i}_ln1_bwd")
        G["ln_g"][i] = jnp.concatenate([dg1, dg2], axis=0)
        G["ln_b"][i] = jnp.concatenate([db1, db2], axis=0)
        if i % 2 == 0:
            GB["e_w_out"] = _mm(L["ya"], dh1b, "tn", f"l{i}_dw_out_a", dst=into("e_w_out", j))
            GB["e_w_out"] = _mm(L["yb"], dh1b, "tn", f"l{i}_dw_out_b", dst=into("e_w_out", j, r0=D))
            dya = _mm(dh1b, V(W["e_w_out"], (j,), rows=D), "nt", f"l{i}_dya")
            dyb = _mm(dh1b, V(W["e_w_out"], (j,), r0=D), "nt", f"l{i}_dyb")
            dac, dga, dba = _ln_silu_bwd(L["ac"], dya, W["e_ln_a_g"][j][None], W["e_ln_a_b"][j][None], f"l{i}_ln_a_bwd")
            G["e_ln_a_g"][j], G["e_ln_a_b"][j] = dga[0], dba[0]
            dal, dag, dwa, dbca = _conv_a_bwd(L["ua"], W["e_conv_a_w"][j], dac, f"l{i}_conv_a_bwd")
            G["e_conv_a_w"][j] = _fold_rows(dwa, CONV_A)
            G["e_conv_a_b"][j] = dbca.sum(0)
            dy, dz, dgn = _gate_rms_bwd(L["y"], L["z"], dyb, W["e_norm_b_g"][j][None], f"l{i}_gate_rms_bwd")
            G["e_norm_b_g"][j] = dgn[0]
            dxs, dbs, dcs, sq_col, cms_row, dda_col, dda_row, ddp = _ssd_bwd(L["xc"], *L["sm"], W["e_d_skip"][j], L["states"], dy,
                                                                             f"l{i}_ssd_bwd")
            draw, G["e_dt_bias"][j], G["e_a_log"][j] = _ssd_small_grads(L["udt"][:, :N_HEADS], W["e_dt_bias"][j], W["e_a_log"][j],
                                                                       sq_col, cms_row, dda_col, dda_row)
            G["e_d_skip"][j] = ddp[:, :, 0, :].sum(0).reshape(N_HEADS, HEAD_P).sum(1)
            dxu, dwb, dbcb = _conv_b_bwd(L["xu"], W["e_conv_b_w"][j], W["e_conv_b_b"][j][None], dxs, dbs, dcs, f"l{i}_conv_b_bwd")
            G["e_conv_b_w"][j] = _fold_rows(dwb, CONV_B)
            G["e_conv_b_b"][j] = dbcb.sum(0)
            dudt = _padcols(draw, LANES)
            w_in = L["w_in"]
            xb_l = L["xb"]
            for nm, dseg, c0 in (("al", dal, 0), ("ag", dag, D), ("z", dz, SEG_Z[0]), ("xbc", dxu, SEG_X[0]), ("dt", dudt, SEG_DT[0])):
                GB["e_w_in"] = _mm(xb_l, dseg, "tn", f"l{i}_dw_in_{nm}", dst=into("e_w_in", j, c0=c0))
            dx = _mm(dal, w_in(SEG_A, cols=D), "nt", f"l{i}_dx_al", add=dh1, add_scale=ALPHA)
            dx = _mm(dag, w_in(SEG_A, c0=D, cols=D), "nt", f"l{i}_dx_ag", add=dx)
            dx = _mm(dz, w_in(SEG_Z), "nt", f"l{i}_dx_z", add=dx)
            dx = _mm(dxu, w_in(SEG_X), "nt", f"l{i}_dx_xbc", add=dx)
            dx = _mm(dudt, w_in(SEG_DT), "nt", f"l{i}_dx_dt", add=dx)
        else:
            GB["o_w_out"] = _mm(L["sc"], dh1b, "tn", f"l{i}_dw_out", dst=into("o_w_out", j))
            dsc = _mm(dh1b, V(W["o_w_out"], (j,)), "nt", f"l{i}_dsc")
            dbg, dcg, dv, dwc = _conv_c_bwd(L["uo"], W["o_conv_w"][j], dsc, f"l{i}_conv_c_bwd")
            G["o_conv_w"][j] = _fold_rows(dwc, CONV_C)
            xb_l = L["xb"]
            dx = dh1
            for nm, dseg, c0, scale in (("bg", dbg, 0, ALPHA), ("cg", dcg, D, 1.0), ("v", dv, 2 * D, 1.0)):
                GB["o_w_in"] = _mm(xb_l, dseg, "tn", f"l{i}_dw_in_{nm}", dst=into("o_w_in", j, c0=c0))
                dx = _mm(dseg, V(W["o_w_in"], (j,), c0=c0, cols=D), "nt", f"l{i}_dx_{nm}", add=dx, add_scale=scale)
        tok = comm.layer_grads_done(i, GB) if comm is not None else None
    grads = {n: jnp.stack(v) for n, v in G.items()}
    return sq, dx, GB, grads


def _ssd_small_inputs(raw, dt_bias, a_log):
    T = raw.shape[0]
    a = -jnp.exp(a_log)
    rg = raw.reshape(T, N_GROUPS, 4)
    raw_col = jnp.pad(jnp.transpose(rg, (1, 0, 2)), ((0, 0), (0, 0), (0, LANES - 4)))
    raw_row = jnp.pad(jnp.transpose(rg, (1, 2, 0)), ((0, 0), (0, SUBLANES - 4), (0, 0)))

    def colv(v):
        return jnp.pad(v.reshape(N_GROUPS, 1, 4), ((0, 0), (0, 0), (0, LANES - 4)))

    def rowv(v):
        return jnp.pad(v.reshape(N_GROUPS, 4, 1), ((0, 0), (0, SUBLANES - 4), (0, 0)))

    return raw_col, raw_row, colv(dt_bias), rowv(dt_bias), colv(a), rowv(a)


def _ssd_small_grads(raw, dt_bias, a_log, sq_col, cms_row, dda_col, dda_row):
    T = raw.shape[0]

    def join(col, row):
        c = jnp.transpose(col[:, :, :4], (1, 0, 2)).reshape(T, N_HEADS)
        r = jnp.transpose(row[:, :4, :], (2, 0, 1)).reshape(T, N_HEADS)
        return c + r

    a = -jnp.exp(a_log)
    pre = raw + dt_bias
    dt = jax.nn.softplus(pre)
    dda = join(dda_col, dda_row)
    ddt = join(sq_col, cms_row) + a * dda
    draw = ddt * jax.nn.sigmoid(pre)
    da = jnp.sum(dt * dda, axis=0)
    return draw, jnp.sum(draw, axis=0), da * a


def kernel(x, p, e_w_in, e_conv_a_w, e_conv_a_b, e_ln_a_g, e_ln_a_b, e_conv_b_w, e_conv_b_b, e_dt_bias, e_a_log, e_d_skip, e_norm_b_g, e_w_out, o_w_in, o_conv_w, o_w_out, f_w_up, f_conv_w, f_conv_b, f_w_down, ple_w_proj, ple_w_gate, ln_g, ln_b, loss_target, m_e_w_in, m_e_conv_a_w, m_e_conv_a_b, m_e_ln_a_g, m_e_ln_a_b, m_e_conv_b_w, m_e_conv_b_b, m_e_dt_bias, m_e_a_log, m_e_d_skip, m_e_norm_b_g, m_e_w_out, m_o_w_in, m_o_conv_w, m_o_w_out, m_f_w_up, m_f_conv_w, m_f_conv_b, m_f_w_down, m_ple_w_proj, m_ple_w_gate, m_ln_g, m_ln_b, v_e_w_in, v_e_conv_a_w, v_e_conv_a_b, v_e_ln_a_g, v_e_ln_a_b, v_e_conv_b_w, v_e_conv_b_b, v_e_dt_bias, v_e_a_log, v_e_d_skip, v_e_norm_b_g, v_e_w_out, v_o_w_in, v_o_conv_w, v_o_w_out, v_f_w_up, v_f_conv_w, v_f_conv_b, v_f_w_down, v_ple_w_proj, v_ple_w_gate, v_ln_g, v_ln_b):
    args = dict(locals())
    w_shard = {n: args[n] for n in WEIGHT_ORDER}
    m_shard = {n: args["m_" + n] for n in WEIGHT_ORDER}
    v_shard = {n: args["v_" + n] for n in WEIGHT_ORDER}
    xi, yi, ci = lax.axis_index("x"), lax.axis_index("y"), lax.axis_index("c")
    chip = 2 * xi + yi

    mevec = jnp.stack([chip, ci]).astype(jnp.int32)
    small_shapes = [w_shard[n].shape for n, _ in SMALL_SHARDED]
    sr = _small_rows(small_shapes)
    ws = _pack_rows([w_shard[n] for n, _ in SMALL_SHARDED], LANES, sr, F32)
    W = {n: w_shard[n] for n in SMALL_REPL}
    W.update({s[0]: Layers(s[2]) for s in TENSORS})
    w_items = [_group_items(layers) for layers in W_GROUPS]
    g_items = [_group_items(layers) for layers in G_GROUPS]

    def install(items, fulls):
        for it, f in zip(items, fulls):
            if it[0] == "e_w_in":
                f = jnp.transpose(f.reshape(it[2], 4, D, E_IN // 4), (0, 2, 1, 3)).reshape(it[2], D, E_IN)
                f = jnp.pad(f, ((0, 0), (0, 0), (0, E_PAD - E_IN)))
            W[it[0]].put(f, it[6])

    started = []
    for gi, items in enumerate(w_items):
        fulls = [_cast_into(w_shard[it[0]], it, mevec[:1]) for it in items]
        started.append(_gather_start(fulls, items, f"gather_start_{gi}"))
    fulls, ssem, rsem, _ = started[0]
    fulls = _gather_wait(fulls, ssem, rsem, started[-1][3], w_items[0], "gather_wait_0")
    fulls, WS = _gather_fwd(fulls, w_items[0], "gather_fwd_0", ws)
    install(w_items[0], fulls)
    parts_s = [_unpack_rows(WS[k], small_shapes) for k in range(4)]
    for idx, (n, ax) in enumerate(SMALL_SHARDED):
        W[n] = jnp.concatenate([parts_s[k][idx] for k in range(4)], axis=ax)

    class Comm:
        sent = {}

        def layer_starts(self, layer, after):
            for gi in range(1, len(W_GROUPS)):
                if W_GROUPS[gi][0] == layer:
                    fulls, ssem, rsem, _ = started[gi]
                    fulls = _gather_wait(fulls, ssem, rsem, after, w_items[gi], f"gather_wait_{gi}")
                    fulls, _ = _gather_fwd(fulls, w_items[gi], f"gather_fwd_{gi}")
                    install(w_items[gi], fulls)
            return None

        def layer_grads_done(self, layer, GB):
            tok = None
            for gi, layers in enumerate(G_GROUPS):
                if min(layers) == layer:
                    items = g_items[gi]
                    gs = []
                    for it in items:
                        g = GB[(it[0], gi)]
                        if it[0] == "e_w_in":
                            g = jnp.transpose(g[:, :, :E_IN].reshape(it[2], D, 4, E_IN // 4), (0, 2, 1, 3)).reshape(it[2], 4 * D, E_IN // 4)
                        gs.append(g)
                    ras = _swap_grads(gs, items, f"swap_grads_{gi}")
                    ps = [_add_half(g, ra, it, mevec[1:]) for g, ra, it in zip(gs, ras, items)]
                    thru, ssem, rsem, tok = _scatter_start(ps, items, f"scatter_start_{gi}")
                    self.sent[gi] = (thru, ssem, rsem)
            return tok

    comm = Comm()

    sq, dx, GB, G = _local_step(x[0], p, loss_target[0], W, comm)
    loss = lax.psum(0.5 * sq[0, 0] / D, ("x", "y", "c"))
    grad_x = dx[None]

    def shard_of(g, ax, k):
        n = g.shape[ax] // 4
        return lax.slice_in_dim(g, k * n, (k + 1) * n, axis=ax)

    reduced = {}
    after = dx
    for gi, items in enumerate(g_items):
        thru, ssem, rsem = comm.sent[gi]
        thru = _scatter_wait(thru, ssem, rsem, after, items, f"scatter_wait_{gi}")
        ps, rcs = thru[:len(items)], thru[len(items):]
        rs = [_sum_own(pt, rc, it, mevec, reduced.get(it[0])) for pt, rc, it in zip(ps, rcs, items)]
        rs = _join_halves(rs, items, f"join_halves_{gi}")
        reduced.update({it[0]: r for it, r in zip(items, rs)})
        after = rs[0]
    gbig = {s[0]: reduced[s[0]].reshape(w_shard[s[0]].shape) for s in TENSORS}

    small_all = ([shard_of(G[n], ax, k) for k in range(4) for n, ax in SMALL_SHARDED] + [G[n] for n in SMALL_REPL])
    small_all_shapes = [t.shape for t in small_all]
    mr = _small_rows(small_all_shapes)
    sg = _sum8(_allgather_small(_pack_rows(small_all, LANES, mr, F32)), mr)
    sparts = _unpack_rows(sg, small_all_shapes)
    ns = len(SMALL_SHARDED)
    gsmall = {}
    for idx, (n, ax) in enumerate(SMALL_SHARDED):
        stacked = jnp.stack([sparts[k * ns + idx] for k in range(4)])
        gsmall[n] = lax.dynamic_index_in_dim(stacked, chip, axis=0, keepdims=False)
    for idx, n in enumerate(SMALL_REPL):
        gsmall[n] = sparts[4 * ns + idx]

    grads, deltas, new_m, new_v = [], [], [], []
    for n in WEIGHT_ORDER:
        g = gbig[n] if n in gbig else gsmall[n]
        d, mn, vn = _adamw(w_shard[n], g, m_shard[n], v_shard[n], f"adamw_{n}")
        grads.append(g)
        deltas.append(d)
        new_m.append(mn)
        new_v.append(vn)
    return (loss, grad_x, *grads, *deltas, *new_m, *new_v)
```

```python
import functools
import math

import jax
import jax.numpy as jnp
from jax import lax
from jax.experimental import pallas as pl
from jax.experimental.pallas import tpu as pltpu

F32 = jnp.float32
BF16 = jnp.bfloat16
MESH = pl.DeviceIdType.MESH

DEPTH = 4
ALPHA = (2.0 * DEPTH) ** 0.25
LN_EPS = 1e-5
D = 1024
HEAD_P = 64
N_STATE = 128
N_HEADS = 16
N_GROUPS = 4
CONV_A, CONV_B, CONV_C, CONV_F = 31, 4, 3, 3
D_FF = 2816
PLE = 256
E_IN = 5136

ADAM_LR, ADAM_B1, ADAM_B2, ADAM_EPS, ADAM_WD, ADAM_STEP = 0.001, 0.9, 0.999, 1e-08, 0.01, 10

LANES = 128
SUBLANES = 8
VMEM_LIMIT = 56 * 1024 * 1024
SSD_Q = 128
CONV_R = 128
CONV_PAD = 32
ROW_T = 256
HI = lax.Precision.HIGHEST


def _cparams(sem=None):
    return pltpu.CompilerParams(dimension_semantics=sem, vmem_limit_bytes=VMEM_LIMIT)


def _sig(v):
    return jax.nn.sigmoid(v)


_DIMS = {"nn": (((1,), (0,)), ((), ())), "nt": (((1,), (1,)), ((), ())), "tn": (((0,), (0,)), ((), ()))}


class Layers:
    def __init__(self, n_layers):
        self.where = [None] * n_layers

    def put(self, arr, l0):
        for k in range(arr.shape[0]):
            self.where[l0 + k] = (arr, k)


class V:
    def __init__(self, arr, lead=(), r0=0, c0=0, rows=None, cols=None):
        if isinstance(arr, Layers):
            arr, k = arr.where[lead[0]]
            lead = (k,) + tuple(lead[1:])
        self.arr, self.lead, self.r0, self.c0 = arr, tuple(lead), r0, c0
        R, C = arr.shape[-2:]
        self.rows = R - r0 if rows is None else rows
        self.cols = C - c0 if cols is None else cols

    def spec(self, br, bc, fn):
        assert self.r0 % br == 0 and self.c0 % bc == 0, (self.r0, self.c0, br, bc)
        ro, co, lead = self.r0 // br, self.c0 // bc, self.lead

        def index(i, j, k):
            r, c = fn(i, j, k)
            return lead + (r + ro, c + co)

        return pl.BlockSpec((None,) * len(lead) + (br, bc), index)


def _v(t):
    return t if isinstance(t, V) else V(t)


def _tile(n, offs, cands):
    for c in cands:
        if n % c == 0 and all(o % c == 0 for o in offs):
            return c
    raise ValueError((n, offs))


_TILES = (1024, 1408, 512, 256, 128)


def _mm(a, b, mode, name, out_dtype=F32, add=None, add_scale=1.0, dst=None, after=None):
    a, b = _v(a), _v(b)
    add = _v(add) if add is not None else None
    if mode == "nn":
        M, K, K2, N = a.rows, a.cols, b.rows, b.cols
        am, ak, bk, bn = a.r0, a.c0, b.r0, b.c0
    elif mode == "nt":
        M, K, N, K2 = a.rows, a.cols, b.rows, b.cols
        am, ak, bn, bk = a.r0, a.c0, b.r0, b.c0
    else:
        K, M, K2, N = a.rows, a.cols, b.rows, b.cols
        ak, am, bk, bn = a.r0, a.c0, b.r0, b.c0
    assert K == K2, (name, mode, M, K, K2, N)
    if dst is None:
        buf, full_shape, o_lead, o_r0, o_c0 = None, (M, N), (), 0, 0
    else:
        buf, full_shape, o_lead, o_r0, o_c0 = dst
    tm = _tile(M, [am, o_r0] + ([add.r0] if add else []), _TILES)
    tn = _tile(N, [bn, o_c0] + ([add.c0] if add else []), _TILES)
    tk = _tile(K, [ak, bk], _TILES)
    nk = K // tk
    has_add, has_buf, has_after = add is not None, buf is not None, after is not None

    def body(*refs):
        a_ref, b_ref = refs[0], refs[1]
        add_ref = refs[2] if has_add else None
        o_ref = refs[2 + has_add + has_buf + has_after]

        def finish(r):
            if has_add:
                r = r + add_scale * add_ref[...].astype(F32)
            o_ref[...] = r.astype(o_ref.dtype)

        part = lax.dot_general(a_ref[...].astype(BF16), b_ref[...].astype(BF16), _DIMS[mode], preferred_element_type=F32)
        if nk == 1:
            finish(part)
        else:
            acc_ref = refs[-1]
            k = pl.program_id(2)

            @pl.when(k == 0)
            def _():
                acc_ref[...] = part

            @pl.when(jnp.logical_and(k > 0, k < nk - 1))
            def _():
                acc_ref[...] += part

            @pl.when(k == nk - 1)
            def _():
                finish(acc_ref[...] + part)

    if mode == "tn":
        a_spec = a.spec(tk, tm, lambda i, j, k: (k, i))
    else:
        a_spec = a.spec(tm, tk, lambda i, j, k: (i, k))
    if mode == "nt":
        b_spec = b.spec(tn, tk, lambda i, j, k: (j, k))
    else:
        b_spec = b.spec(tk, tn, lambda i, j, k: (k, j))
    in_specs, args = [a_spec, b_spec], [a.arr, b.arr]
    if has_add:
        in_specs.append(add.spec(tm, tn, lambda i, j, k: (i, j)))
        args.append(add.arr)
    aliases = {}
    if has_buf:
        aliases = {len(args): 0}
        in_specs.append(pl.BlockSpec(memory_space=pl.ANY))
        args.append(buf)
        out_dtype = buf.dtype
    if has_after:
        in_specs.append(pl.BlockSpec(memory_space=pl.ANY))
        args.append(after)
    o_view = V(jax.ShapeDtypeStruct(full_shape, out_dtype), o_lead, o_r0, o_c0, M, N)
    return pl.pallas_call(
        body, name=name, grid=(M // tm, N // tn, nk), in_specs=in_specs, out_specs=o_view.spec(tm, tn, lambda i, j, k: (i, j)),
        out_shape=jax.ShapeDtypeStruct(full_shape, out_dtype), input_output_aliases=aliases,
        scratch_shapes=[pltpu.VMEM((tm, tn), F32)] if nk > 1 else [],
        compiler_params=_cparams(("parallel", "parallel", "arbitrary")))(*args)


def _rows(T, width=D):
    return pl.BlockSpec((ROW_T, width), lambda i: (i, 0))


def _vec(width=D):
    return pl.BlockSpec((1, width), lambda i: (0, 0))


def _ln_stats(h):
    mu = jnp.mean(h, axis=-1, keepdims=True)
    hc = h - mu
    var = jnp.mean(hc * hc, axis=-1, keepdims=True)
    rstd = lax.rsqrt(var + LN_EPS)
    return hc * rstd, rstd


def _res_ln_fwd(x, adds, ple, g, b, name):
    T = x.shape[0]
    n_add = len(adds)
    has_ple = ple is not None

    def body(*refs):
        x_ref = refs[0]
        add_refs = refs[1:1 + n_add]
        pos = 1 + n_add
        if has_ple:
            pp_ref, gl_ref = refs[pos], refs[pos + 1]
            pos += 2
        g_ref, b_ref, h_ref, y_ref, yb_ref = refs[pos:pos + 5]
        h = ALPHA * x_ref[...]
        for r in add_refs:
            h = h + r[...]
        if has_ple:
            h = h + pp_ref[...] * _sig(gl_ref[...])
        xhat, _ = _ln_stats(h)
        y = xhat * g_ref[...] + b_ref[...]
        h_ref[...] = h
        y_ref[...] = y
        yb_ref[...] = y.astype(BF16)

    n_in = 1 + n_add + (2 if has_ple else 0)
    args = (x,) + tuple(adds) + (tuple(ple) if has_ple else ()) + (g, b)
    return pl.pallas_call(
        body, name=name, grid=(T // ROW_T,), in_specs=[_rows(T)] * n_in + [_vec(), _vec()],
        out_specs=(_rows(T), _rows(T), _rows(T)),
        out_shape=(jax.ShapeDtypeStruct((T, D), F32), jax.ShapeDtypeStruct((T, D), F32), jax.ShapeDtypeStruct((T, D), BF16)),
        compiler_params=_cparams(("parallel",)))(*args)


def _res_ln_bwd(dy, h, g, ple, name):
    T = dy.shape[0]
    has_ple = ple is not None

    def body(*refs):
        if has_ple:
            dy_ref, h_ref, g_ref, pp_ref, gl_ref, dh_ref, dhb_ref, dg_ref, db_ref, dpp_ref, dgl_ref = refs
        else:
            dy_ref, h_ref, g_ref, dh_ref, dhb_ref, dg_ref, db_ref = refs
        i = pl.program_id(0)

        @pl.when(i == 0)
        def _():
            dg_ref[...] = jnp.zeros_like(dg_ref)
            db_ref[...] = jnp.zeros_like(db_ref)

        dyv = dy_ref[...]
        xhat, rstd = _ln_stats(h_ref[...])
        dg_ref[...] += jnp.sum(dyv * xhat, axis=0, keepdims=True)
        db_ref[...] += jnp.sum(dyv, axis=0, keepdims=True)
        dxh = dyv * g_ref[...]
        dh = rstd * (dxh - jnp.mean(dxh, axis=-1, keepdims=True) - xhat * jnp.mean(dxh * xhat, axis=-1, keepdims=True))
        dh_ref[...] = dh
        dhb_ref[...] = dh.astype(BF16)
        if has_ple:
            s = _sig(gl_ref[...])
            dpp_ref[...] = (dh * s).astype(BF16)
            dgl_ref[...] = (dh * pp_ref[...] * s * (1.0 - s)).astype(BF16)

    args = (dy, h, g) + (tuple(ple) if has_ple else ())
    in_specs = [_rows(T), _rows(T), _vec()] + ([_rows(T), _rows(T)] if has_ple else [])
    out_specs = [_rows(T), _rows(T), _vec(), _vec()] + ([_rows(T), _rows(T)] if has_ple else [])
    out_shape = [jax.ShapeDtypeStruct((T, D), F32), jax.ShapeDtypeStruct((T, D), BF16),
                 jax.ShapeDtypeStruct((1, D), F32), jax.ShapeDtypeStruct((1, D), F32)]
    if has_ple:
        out_shape += [jax.ShapeDtypeStruct((T, D), BF16), jax.ShapeDtypeStruct((T, D), BF16)]
    return pl.pallas_call(
        body, name=name, grid=(T // ROW_T,), in_specs=in_specs, out_specs=tuple(out_specs), out_shape=tuple(out_shape),
        compiler_params=_cparams(("arbitrary",)))(*args)


def _ln_silu_fwd(ac, g, b, name):
    T = ac.shape[0]

    def body(a_ref, g_ref, b_ref, o_ref):
        xhat, _ = _ln_stats(a_ref[...])
        ln = xhat * g_ref[...] + b_ref[...]
        o_ref[...] = (ln * _sig(ln)).astype(BF16)

    return pl.pallas_call(
        body, name=name, grid=(T // ROW_T,), in_specs=[_rows(T), _vec(), _vec()], out_specs=_rows(T),
        out_shape=jax.ShapeDtypeStruct((T, D), BF16), compiler_params=_cparams(("parallel",)))(ac, g, b)


def _ln_silu_bwd(ac, dya, g, b, name):
    T = ac.shape[0]

    def body(a_ref, d_ref, g_ref, b_ref, da_ref, dg_ref, db_ref):
        i = pl.program_id(0)

        @pl.when(i == 0)
        def _():
            dg_ref[...] = jnp.zeros_like(dg_ref)
            db_ref[...] = jnp.zeros_like(db_ref)

        xhat, rstd = _ln_stats(a_ref[...])
        ln = xhat * g_ref[...] + b_ref[...]
        s = _sig(ln)
        dln = d_ref[...] * s * (1.0 + ln * (1.0 - s))
        dg_ref[...] += jnp.sum(dln * xhat, axis=0, keepdims=True)
        db_ref[...] += jnp.sum(dln, axis=0, keepdims=True)
        dxh = dln * g_ref[...]
        da_ref[...] = rstd * (dxh - jnp.mean(dxh, axis=-1, keepdims=True)
                              - xhat * jnp.mean(dxh * xhat, axis=-1, keepdims=True))

    return pl.pallas_call(
        body, name=name, grid=(T // ROW_T,), in_specs=[_rows(T), _rows(T), _vec(), _vec()],
        out_specs=(_rows(T), _vec(), _vec()),
        out_shape=(jax.ShapeDtypeStruct((T, D), F32), jax.ShapeDtypeStruct((1, D), F32), jax.ShapeDtypeStruct((1, D), F32)),
        compiler_params=_cparams(("arbitrary",)))(ac, dya, g, b)


def _gate_rms_fwd(y, z, g, name):
    T = y.shape[0]

    def body(y_ref, z_ref, g_ref, o_ref):
        zv = z_ref[...]
        yg = y_ref[...] * (zv * _sig(zv))
        r = lax.rsqrt(jnp.mean(yg * yg, axis=-1, keepdims=True) + LN_EPS)
        o_ref[...] = (yg * r * g_ref[...]).astype(BF16)

    return pl.pallas_call(
        body, name=name, grid=(T // ROW_T,), in_specs=[_rows(T), _rows(T), _vec()], out_specs=_rows(T),
        out_shape=jax.ShapeDtypeStruct((T, D), BF16), compiler_params=_cparams(("parallel",)))(y, z, g)


def _gate_rms_bwd(y, z, dout, g, name):
    T = y.shape[0]

    def body(y_ref, z_ref, d_ref, g_ref, dy_ref, dz_ref, dg_ref):
        i = pl.program_id(0)

        @pl.when(i == 0)
        def _():
            dg_ref[...] = jnp.zeros_like(dg_ref)

        yv, zv, dv = y_ref[...], z_ref[...], d_ref[...]
        s = _sig(zv)
        sz = zv * s
        yg = yv * sz
        r = lax.rsqrt(jnp.mean(yg * yg, axis=-1, keepdims=True) + LN_EPS)
        dg_ref[...] += jnp.sum(dv * yg * r, axis=0, keepdims=True)
        dn = dv * g_ref[...]
        dyg = r * dn - yg * (r * r * r) * jnp.mean(dn * yg, axis=-1, keepdims=True)
        dy_ref[...] = dyg * sz
        dz_ref[...] = dyg * yv * s * (1.0 + zv * (1.0 - s))

    return pl.pallas_call(
        body, name=name, grid=(T // ROW_T,), in_specs=[_rows(T), _rows(T), _rows(T), _vec()],
        out_specs=(_rows(T), _rows(T), _vec()),
        out_shape=(jax.ShapeDtypeStruct((T, D), F32), jax.ShapeDtypeStruct((T, D), F32), jax.ShapeDtypeStruct((1, D), F32)),
        compiler_params=_cparams(("arbitrary",)))(y, z, dout, g)


def _loss_head(y, target, name):
    T = y.shape[0]

    def body(y_ref, t_ref, s_ref, d_ref):
        i = pl.program_id(0)

        @pl.when(i == 0)
        def _():
            s_ref[...] = jnp.zeros_like(s_ref)

        err = y_ref[...] - t_ref[...]
        s_ref[...] += jnp.sum(jnp.sum(err * err, axis=1, keepdims=True), axis=0, keepdims=True)
        d_ref[...] = err * (1.0 / D)

    return pl.pallas_call(
        body, name=name, grid=(T // ROW_T,), in_specs=[_rows(T), _rows(T)],
        out_specs=(pl.BlockSpec((SUBLANES, LANES), lambda i: (0, 0)), _rows(T)),
        out_shape=(jax.ShapeDtypeStruct((SUBLANES, LANES), F32), jax.ShapeDtypeStruct((T, D), F32)),
        compiler_params=_cparams(("arbitrary",)))(y, target)


def _taps_fwd(pad_ref, w_ref, K, base):
    off = CONV_PAD - (K - 1)
    acc = w_ref[0:1, :] * pad_ref[pl.ds(base + off, CONV_R), :]
    for k in range(1, K):
        acc = acc + w_ref[k:k + 1, :] * pad_ref[pl.ds(base + off + k, CONV_R), :]
    return acc


def _taps_bwd(padd_ref, w_ref, K, base):
    acc = w_ref[0:1, :] * padd_ref[pl.ds(base + (K - 1), CONV_R), :]
    for k in range(1, K):
        acc = acc + w_ref[k:k + 1, :] * padd_ref[pl.ds(base + (K - 1) - k, CONV_R), :]
    return acc


def _fold8(v):
    return v.reshape(CONV_R // SUBLANES, SUBLANES, v.shape[-1]).sum(0)


def _wgrad_acc(dw_ref, pad_ref, d, K, base):
    off = CONV_PAD - (K - 1)
    for k in range(K):
        dw_ref[k * SUBLANES:(k + 1) * SUBLANES, :] += _fold8(d * pad_ref[pl.ds(base + off + k, CONV_R), :])


def _loop_rows(T, fn):
    def step(r, carry):
        fn(pl.multiple_of(r * CONV_R, CONV_R))
        return carry
    lax.fori_loop(0, T // CONV_R, step, 0)


def _col(T, off_blocks=0, rows=None):
    return pl.BlockSpec((T if rows is None else rows, LANES), lambda j: (0, j + off_blocks))


def _conv_call(body, name, T, n_tiles, in_specs, out_specs, out_shape, n_pad, n_padd=0):
    scratch = [pltpu.VMEM((T + CONV_PAD, LANES), F32)] * (n_pad + n_padd)
    return pl.pallas_call(body, name=name, grid=(n_tiles,), in_specs=in_specs, out_specs=out_specs, out_shape=out_shape,
                          scratch_shapes=scratch, compiler_params=_cparams(("parallel",)))


def _zero_head(ref):
    ref[0:CONV_PAD, :] = jnp.zeros((CONV_PAD, LANES), F32)


def _zero_tail(ref, T):
    ref[T:T + CONV_PAD, :] = jnp.zeros((CONV_PAD, LANES), F32)


def _sds(shape, dtype=F32):
    return jax.ShapeDtypeStruct(shape, dtype)


def _conv_a_fwd(ua, w, b, name):
    T = ua.shape[0]
    K, nt = CONV_A, D // LANES

    def body(al_ref, ag_ref, w_ref, b_ref, o_ref, pad_ref):
        _zero_head(pad_ref)

        def pre(base):
            pad_ref[pl.ds(base + CONV_PAD, CONV_R), :] = al_ref[pl.ds(base, CONV_R), :] * _sig(ag_ref[pl.ds(base, CONV_R), :])
        _loop_rows(T, pre)

        def main(base):
            o_ref[pl.ds(base, CONV_R), :] = _taps_fwd(pad_ref, w_ref, K, base) + b_ref[...]
        _loop_rows(T, main)

    return _conv_call(body, name, T, nt, [_col(T), _col(T, nt), _col(T, rows=K), _col(T, rows=1)], _col(T),
                      _sds((T, D)), 1)(ua, ua, w, b)


def _conv_a_bwd(ua, w, dac, name):
    T = ua.shape[0]
    K, nt = CONV_A, D // LANES

    def body(al_ref, ag_ref, w_ref, d_ref, dal_ref, dag_ref, dw_ref, db_ref, pad_ref, padd_ref):
        _zero_head(pad_ref)
        _zero_tail(padd_ref, T)
        dw_ref[...] = jnp.zeros_like(dw_ref)
        db_ref[...] = jnp.zeros_like(db_ref)

        def pre(base):
            rows = pl.ds(base, CONV_R)
            pad_ref[pl.ds(base + CONV_PAD, CONV_R), :] = al_ref[rows, :] * _sig(ag_ref[rows, :])
            padd_ref[rows, :] = d_ref[rows, :]
        _loop_rows(T, pre)

        def main(base):
            rows = pl.ds(base, CONV_R)
            d = d_ref[rows, :]
            _wgrad_acc(dw_ref, pad_ref, d, K, base)
            db_ref[...] += _fold8(d)
            da = _taps_bwd(padd_ref, w_ref, K, base)
            al, s = al_ref[rows, :], _sig(ag_ref[rows, :])
            dal_ref[rows, :] = da * s
            dag_ref[rows, :] = da * al * s * (1.0 - s)
        _loop_rows(T, main)

    return _conv_call(body, name, T, nt, [_col(T), _col(T, nt), _col(T, rows=K), _col(T)],
                      (_col(T), _col(T), _col(T, rows=K * SUBLANES), _col(T, rows=SUBLANES)),
                      (_sds((T, D)), _sds((T, D)), _sds((K * SUBLANES, D)), _sds((SUBLANES, D))), 1, 1)(ua, ua, w, dac)


def _conv_b_fwd(xu, w, b, name):
    T, C = xu.shape
    K, nt = CONV_B, C // LANES

    def body(x_ref, w_ref, b_ref, o_ref, pad_ref):
        _zero_head(pad_ref)
        pad_ref[CONV_PAD:CONV_PAD + T, :] = x_ref[...]

        def main(base):
            hc = _taps_fwd(pad_ref, w_ref, K, base) + b_ref[...]
            o_ref[pl.ds(base, CONV_R), :] = hc * _sig(hc)
        _loop_rows(T, main)

    return _conv_call(body, name, T, nt, [_col(T), _col(T, rows=K), _col(T, rows=1)], _col(T), _sds((T, C)), 1)(xu, w, b)


def _conv_b_bwd(xu, w, b, dxs, dbs, dcs, name):
    T, C = xu.shape
    K, nt = CONV_B, C // LANES
    nx, nb = dxs.shape[1] // LANES, dbs.shape[1] // LANES

    def body(x_ref, w_ref, b_ref, d1_ref, d2_ref, d3_ref, dx_ref, dw_ref, db_ref, pad_ref, padd_ref):
        j = pl.program_id(0)
        _zero_head(pad_ref)
        _zero_tail(padd_ref, T)
        dw_ref[...] = jnp.zeros_like(dw_ref)
        db_ref[...] = jnp.zeros_like(db_ref)
        pad_ref[CONV_PAD:CONV_PAD + T, :] = x_ref[...]

        def pre(base):
            rows = pl.ds(base, CONV_R)
            hc = _taps_fwd(pad_ref, w_ref, K, base) + b_ref[...]
            s = _sig(hc)
            d = jnp.where(j < nx, d1_ref[rows, :], jnp.where(j < nx + nb, d2_ref[rows, :], d3_ref[rows, :]))
            padd_ref[rows, :] = d * s * (1.0 + hc * (1.0 - s))
        _loop_rows(T, pre)

        def main(base):
            d = padd_ref[pl.ds(base, CONV_R), :]
            _wgrad_acc(dw_ref, pad_ref, d, K, base)
            db_ref[...] += _fold8(d)
            dx_ref[pl.ds(base, CONV_R), :] = _taps_bwd(padd_ref, w_ref, K, base)
        _loop_rows(T, main)

    def piece(lo, n):
        return pl.BlockSpec((T, LANES), lambda j: (0, jnp.clip(j - lo, 0, n - 1)))

    return _conv_call(body, name, T, nt,
                      [_col(T), _col(T, rows=K), _col(T, rows=1), piece(0, nx), piece(nx, nb), piece(nx + nb, nt - nx - nb)],
                      (_col(T), _col(T, rows=K * SUBLANES), _col(T, rows=SUBLANES)),
                      (_sds((T, C)), _sds((K * SUBLANES, C)), _sds((SUBLANES, C))), 1, 1)(xu, w, b, dxs, dbs, dcs)


def _conv_c_fwd(uo, w, name):
    T = uo.shape[0]
    K, nt = CONV_C, D // LANES

    def body(bg_ref, cg_ref, v_ref, w_ref, o_ref, pad_ref):
        _zero_head(pad_ref)
        pad_ref[CONV_PAD:CONV_PAD + T, :] = cg_ref[...] * v_ref[...]

        def main(base):
            rows = pl.ds(base, CONV_R)
            o_ref[rows, :] = (bg_ref[rows, :] * _taps_fwd(pad_ref, w_ref, K, base)).astype(BF16)
        _loop_rows(T, main)

    return _conv_call(body, name, T, nt, [_col(T), _col(T, nt), _col(T, 2 * nt), _col(T, rows=K)], _col(T),
                      _sds((T, D), BF16), 1)(uo, uo, uo, w)


def _conv_c_bwd(uo, w, dsc, name):
    T = uo.shape[0]
    K, nt = CONV_C, D // LANES

    def body(bg_ref, cg_ref, v_ref, w_ref, d_ref, dbg_ref, dcg_ref, dv_ref, dw_ref, pad_ref, padd_ref):
        _zero_head(pad_ref)
        _zero_tail(padd_ref, T)
        dw_ref[...] = jnp.zeros_like(dw_ref)
        pad_ref[CONV_PAD:CONV_PAD + T, :] = cg_ref[...] * v_ref[...]

        def pre(base):
            rows = pl.ds(base, CONV_R)
            d = d_ref[rows, :]
            dbg_ref[rows, :] = (d * _taps_fwd(pad_ref, w_ref, K, base)).astype(BF16)
            padd_ref[rows, :] = d * bg_ref[rows, :]
        _loop_rows(T, pre)

        def main(base):
            rows = pl.ds(base, CONV_R)
            _wgrad_acc(dw_ref, pad_ref, padd_ref[rows, :], K, base)
            dq = _taps_bwd(padd_ref, w_ref, K, base)
            dcg_ref[rows, :] = (dq * v_ref[rows, :]).astype(BF16)
            dv_ref[rows, :] = (dq * cg_ref[rows, :]).astype(BF16)
        _loop_rows(T, main)

    return _conv_call(body, name, T, nt, [_col(T), _col(T, nt), _col(T, 2 * nt), _col(T, rows=K), _col(T)],
                      (_col(T), _col(T), _col(T), _col(T, rows=K * SUBLANES)),
                      (_sds((T, D), BF16), _sds((T, D), BF16), _sds((T, D), BF16), _sds((K * SUBLANES, D))), 1, 1)(uo, uo, uo, w, dsc)


def _conv_f_fwd(up, w, b, name):
    T = up.shape[0]
    K, nt = CONV_F, D_FF // LANES

    def body(u1_ref, u2_ref, w1_ref, w2_ref, b1_ref, b2_ref, o_ref, pad1_ref, pad2_ref):
        _zero_head(pad1_ref)
        _zero_head(pad2_ref)
        pad1_ref[CONV_PAD:CONV_PAD + T, :] = u1_ref[...]
        pad2_ref[CONV_PAD:CONV_PAD + T, :] = u2_ref[...]

        def main(base):
            h1 = _taps_fwd(pad1_ref, w1_ref, K, base) + b1_ref[...]
            h2 = _taps_fwd(pad2_ref, w2_ref, K, base) + b2_ref[...]
            o_ref[pl.ds(base, CONV_R), :] = (h1 * _sig(h1) * h2).astype(BF16)
        _loop_rows(T, main)

    return _conv_call(body, name, T, nt,
                      [_col(T), _col(T, nt), _col(T, rows=K), _col(T, nt, rows=K), _col(T, rows=1), _col(T, nt, rows=1)],
                      _col(T), _sds((T, D_FF), BF16), 2)(up, up, w, w, b, b)


def _conv_f_bwd(up, w, b, dact, name):
    T = up.shape[0]
    K, nt = CONV_F, D_FF // LANES

    def body(u1_ref, u2_ref, w1_ref, w2_ref, b1_ref, b2_ref, d_ref, du1_ref, du2_ref, dw1_ref, dw2_ref, db1_ref, db2_ref,
             pad1_ref, pad2_ref, padd1_ref, padd2_ref):
        _zero_head(pad1_ref)
        _zero_head(pad2_ref)
        _zero_tail(padd1_ref, T)
        _zero_tail(padd2_ref, T)
        for r in (dw1_ref, dw2_ref, db1_ref, db2_ref):
            r[...] = jnp.zeros_like(r)
        pad1_ref[CONV_PAD:CONV_PAD + T, :] = u1_ref[...]
        pad2_ref[CONV_PAD:CONV_PAD + T, :] = u2_ref[...]

        def pre(base):
            rows = pl.ds(base, CONV_R)
            h1 = _taps_fwd(pad1_ref, w1_ref, K, base) + b1_ref[...]
            h2 = _taps_fwd(pad2_ref, w2_ref, K, base) + b2_ref[...]
            s = _sig(h1)
            d = d_ref[rows, :]
            padd1_ref[rows, :] = d * h2 * s * (1.0 + h1 * (1.0 - s))
            padd2_ref[rows, :] = d * h1 * s
        _loop_rows(T, pre)

        def main(base):
            rows = pl.ds(base, CONV_R)
            d1, d2 = padd1_ref[rows, :], padd2_ref[rows, :]
            _wgrad_acc(dw1_ref, pad1_ref, d1, K, base)
            _wgrad_acc(dw2_ref, pad2_ref, d2, K, base)
            db1_ref[...] += _fold8(d1)
            db2_ref[...] += _fold8(d2)
            du1_ref[rows, :] = _taps_bwd(padd1_ref, w1_ref, K, base).astype(BF16)
            du2_ref[rows, :] = _taps_bwd(padd2_ref, w2_ref, K, base).astype(BF16)
        _loop_rows(T, main)

    wrow, brow = _col(T, rows=K * SUBLANES), _col(T, rows=SUBLANES)
    return _conv_call(body, name, T, nt,
                      [_col(T), _col(T, nt), _col(T, rows=K), _col(T, nt, rows=K), _col(T, rows=1), _col(T, nt, rows=1), _col(T)],
                      (_col(T), _col(T), wrow, wrow, brow, brow),
                      (_sds((T, D_FF), BF16), _sds((T, D_FF), BF16), _sds((K * SUBLANES, D_FF)), _sds((K * SUBLANES, D_FF)),
                       _sds((SUBLANES, D_FF)), _sds((SUBLANES, D_FF))), 2, 2)(up, up, w, w, b, b, dact)


def _dot(a, b, dims="nn"):
    return lax.dot_general(a.astype(BF16), b.astype(BF16), _DIMS[dims], preferred_element_type=F32)


def _ssd_small(xcr_ref, xrr_ref, bc_ref, br_ref, ac_ref, ar_ref):
    Q = SSD_Q
    li = lax.broadcasted_iota(jnp.int32, (Q, Q), 0)
    si = lax.broadcasted_iota(jnp.int32, (Q, Q), 1)
    tril = li >= si
    dtc = jax.nn.softplus(xcr_ref[...] + bc_ref[...])
    dtr = jax.nn.softplus(xrr_ref[...] + br_ref[...])
    cumc = jnp.dot(tril.astype(F32), dtc * ac_ref[...], precision=HI, preferred_element_type=F32)
    cumr = jnp.dot(dtr * ar_ref[...], (li <= si).astype(F32), precision=HI, preferred_element_type=F32)
    return tril, dtc, dtr, cumc, cumr


def _ssd_specs(nc, rev):
    Q = SSD_Q
    cc = (lambda c: nc - 1 - c) if rev else (lambda c: c)
    x_spec = pl.BlockSpec((Q, 2 * LANES), lambda g, c: (cc(c), g))
    b_spec = pl.BlockSpec((Q, LANES), lambda g, c: (cc(c), 8 + g))
    c_spec = pl.BlockSpec((Q, LANES), lambda g, c: (cc(c), 12 + g))
    colm = pl.BlockSpec((None, Q, LANES), lambda g, c: (g, cc(c), 0))
    rowm = pl.BlockSpec((None, SUBLANES, Q), lambda g, c: (g, 0, cc(c)))
    colv = pl.BlockSpec((None, 1, LANES), lambda g, c: (g, 0, 0))
    rowv = pl.BlockSpec((None, SUBLANES, 1), lambda g, c: (g, 0, 0))
    st_spec = pl.BlockSpec((None, None, 2 * LANES, N_STATE), lambda g, c: (cc(c), g, 0, 0))
    return x_spec, b_spec, c_spec, colm, rowm, colv, rowv, st_spec


def _ssd_fwd(xc, raw_col, raw_row, bias_col, bias_row, a_col, a_row, dskip, name):
    T = xc.shape[0]
    Q = SSD_Q
    nc = T // Q
    x_spec, b_spec, c_spec, colm, rowm, colv, rowv, st_spec = _ssd_specs(nc, False)

    def body(dk_ref, x_ref, b_ref, c_ref, xcr_ref, xrr_ref, bc_ref, br_ref, ac_ref, ar_ref, y_ref, st_ref, h_ref):
        g = pl.program_id(0)

        @pl.when(pl.program_id(1) == 0)
        def _():
            h_ref[...] = jnp.zeros_like(h_ref)

        tril, dtc, dtr, cumc, cumr = _ssd_small(xcr_ref, xrr_ref, bc_ref, br_ref, ac_ref, ar_ref)
        Bm, Cm = b_ref[...], c_ref[...]
        S = _dot(Cm, Bm, "nt")
        lo = lax.broadcasted_iota(jnp.int32, (Q, LANES), 1) < HEAD_P
        rlo = lax.broadcasted_iota(jnp.int32, (LANES, N_STATE), 0) < HEAD_P
        st_ref[...] = h_ref[...]
        clast = cumc[Q - 1:Q, :]
        for pr in range(2):
            cols = slice(pr * LANES, (pr + 1) * LANES)
            xp = x_ref[:, cols]
            yd = jnp.zeros((Q, LANES), F32)
            for q in range(2):
                hh = 2 * pr + q
                seg = cumc[:, hh:hh + 1] - cumr[hh:hh + 1, :]
                lm = jnp.where(tril, jnp.exp(jnp.where(tril, seg, 0.0)), 0.0)
                w = S * lm * dtr[hh:hh + 1, :]
                xm = jnp.where(lo if q == 0 else jnp.logical_not(lo), xp, 0.0)
                yd = yd + _dot(w, xm)
            h0, h1 = 2 * pr, 2 * pr + 1
            c0, c1 = cumc[:, h0:h0 + 1], cumc[:, h1:h1 + 1]
            e_pair = jnp.where(lo, jnp.exp(c0), jnp.exp(c1))
            hp = h_ref[cols, :]
            ch = _dot(Cm, hp, "nt")
            dsk = jnp.where(lo, dk_ref[4 * g + h0], dk_ref[4 * g + h1])
            y_ref[:, cols] = yd + e_pair * ch + dsk * xp
            cl0, cl1 = clast[:, h0:h0 + 1], clast[:, h1:h1 + 1]
            sdec = jnp.where(lo, jnp.exp(cl0 - c0) * dtc[:, h0:h0 + 1], jnp.exp(cl1 - c1) * dtc[:, h1:h1 + 1])
            decrow = jnp.where(rlo, jnp.exp(cl0), jnp.exp(cl1))
            h_ref[cols, :] = hp * decrow + _dot(xp * sdec, Bm, "tn")

    smem = pl.BlockSpec(memory_space=pltpu.SMEM)
    return pl.pallas_call(
        body, name=name, grid=(N_GROUPS, nc),
        in_specs=[smem, x_spec, b_spec, c_spec, colm, rowm, colv, rowv, colv, rowv],
        out_specs=(x_spec, st_spec),
        out_shape=(_sds((T, D)), _sds((nc, N_GROUPS, 2 * LANES, N_STATE))),
        scratch_shapes=[pltpu.VMEM((2 * LANES, N_STATE), F32)],
        compiler_params=_cparams(("parallel", "arbitrary")))(dskip, xc, xc, xc, raw_col, raw_row, bias_col, bias_row, a_col, a_row)


def _ssd_bwd(xc, raw_col, raw_row, bias_col, bias_row, a_col, a_row, dskip, states, dy, name):
    T = xc.shape[0]
    Q = SSD_Q
    nc = T // Q
    x_spec, b_spec, c_spec, colm, rowm, colv, rowv, st_spec = _ssd_specs(nc, True)
    bo_spec = pl.BlockSpec((Q, LANES), lambda g, c: (nc - 1 - c, g))
    dd_spec = pl.BlockSpec((None, None, SUBLANES, 2 * LANES), lambda g, c: (nc - 1 - c, g, 0, 0))

    def body(dk_ref, x_ref, b_ref, c_ref, xcr_ref, xrr_ref, bc_ref, br_ref, ac_ref, ar_ref, st_ref, dy_ref,
             dx_ref, db_ref, dc_ref, sq_ref, cms_ref, ddac_ref, ddar_ref, dd_ref, dh_ref):
        g = pl.program_id(0)

        @pl.when(pl.program_id(1) == 0)
        def _():
            dh_ref[...] = jnp.zeros_like(dh_ref)

        tril, dtc, dtr, cumc, cumr = _ssd_small(xcr_ref, xrr_ref, bc_ref, br_ref, ac_ref, ar_ref)
        Bm, Cm = b_ref[...], c_ref[...]
        S = _dot(Cm, Bm, "nt")
        lane = lax.broadcasted_iota(jnp.int32, (Q, LANES), 1)
        sub = lax.broadcasted_iota(jnp.int32, (SUBLANES, Q), 0)
        rowi = lax.broadcasted_iota(jnp.int32, (Q, LANES), 0)
        lo = lane < HEAD_P
        rlo = lax.broadcasted_iota(jnp.int32, (LANES, N_STATE), 0) < HEAD_P
        clast = cumc[Q - 1:Q, :]
        ds_g = jnp.zeros((Q, Q), F32)
        dcm = jnp.zeros((Q, N_STATE), F32)
        dbm = jnp.zeros((Q, N_STATE), F32)
        dcum_col = jnp.zeros((Q, LANES), F32)
        dcum_row = jnp.zeros((SUBLANES, Q), F32)
        sq_col = jnp.zeros((Q, LANES), F32)
        cms_row = jnp.zeros((SUBLANES, Q), F32)
        for pr in range(2):
            cols = slice(pr * LANES, (pr + 1) * LANES)
            xp, dyp = x_ref[:, cols], dy_ref[:, cols]
            hin, dhp = st_ref[cols, :], dh_ref[cols, :]
            h0, h1 = 2 * pr, 2 * pr + 1
            c0, c1 = cumc[:, h0:h0 + 1], cumc[:, h1:h1 + 1]
            cl0, cl1 = clast[:, h0:h0 + 1], clast[:, h1:h1 + 1]
            e_pair = jnp.where(lo, jnp.exp(c0), jnp.exp(c1))
            edec = jnp.where(lo, jnp.exp(cl0 - c0), jnp.exp(cl1 - c1))
            dt_pair = jnp.where(lo, dtc[:, h0:h0 + 1], dtc[:, h1:h1 + 1])
            sdec = edec * dt_pair
            ch = _dot(Cm, hin, "nt")
            xb = _dot(Bm, dhp, "nt")
            dye = dyp * e_pair
            t1 = dye * ch
            t2 = xp * xb * edec
            hh_prod = dhp * hin
            dsk = jnp.where(lo, dk_ref[4 * g + h0], dk_ref[4 * g + h1])
            dxp = sdec * xb + dsk * dyp
            for q in range(2):
                hh = 2 * pr + q
                mine = lo if q == 0 else jnp.logical_not(lo)
                seg = cumc[:, hh:hh + 1] - cumr[hh:hh + 1, :]
                lm = jnp.where(tril, jnp.exp(jnp.where(tril, seg, 0.0)), 0.0)
                dtrow = dtr[hh:hh + 1, :]
                w = S * lm * dtrow
                dym = jnp.where(mine, dyp, 0.0)
                gl = _dot(dym, xp, "nt") * lm
                ds_g = ds_g + gl * dtrow
                ms = gl * S
                m = ms * dtrow
                dxp = dxp + _dot(w, dym, "tn")
                cms_row = jnp.where(sub == hh, jnp.sum(ms, axis=0, keepdims=True), cms_row)
                dcum_row = jnp.where(sub == hh, -jnp.sum(m, axis=0, keepdims=True), dcum_row)
                t1h = jnp.sum(jnp.where(mine, t1, 0.0), axis=1, keepdims=True)
                sqh = jnp.sum(jnp.where(mine, t2, 0.0), axis=1, keepdims=True)
                sth = sqh * dtc[:, hh:hh + 1]
                rmine = rlo if q == 0 else jnp.logical_not(rlo)
                hsum = jnp.sum(jnp.sum(jnp.where(rmine, hh_prod, 0.0), axis=1, keepdims=True), axis=0, keepdims=True)
                last = jnp.sum(sth, axis=0, keepdims=True) + jnp.exp(clast[:, hh:hh + 1]) * hsum
                dcol = jnp.sum(m, axis=1, keepdims=True) + t1h - sth
                dcum_col = jnp.where(lane == hh, dcol + jnp.where(rowi == Q - 1, last, 0.0), dcum_col)
                sq_col = jnp.where(lane == hh, sqh, sq_col)
            dcm = dcm + _dot(dye, hin)
            dbm = dbm + _dot(xp * sdec, dhp)
            decrow = jnp.where(rlo, jnp.exp(cl0), jnp.exp(cl1))
            dh_ref[cols, :] = dhp * decrow + _dot(dye, Cm, "tn")
            dx_ref[:, cols] = dxp
            dd_ref[:, cols] = jnp.broadcast_to(jnp.sum(dyp * xp, axis=0, keepdims=True), (SUBLANES, LANES))
        dc_ref[...] = dcm + _dot(ds_g, Bm)
        db_ref[...] = dbm + _dot(ds_g, Cm, "tn")
        li = lax.broadcasted_iota(jnp.int32, (Q, Q), 0)
        si = lax.broadcasted_iota(jnp.int32, (Q, Q), 1)
        ddac_ref[...] = jnp.dot((li <= si).astype(F32), dcum_col, precision=HI, preferred_element_type=F32)
        ddar_ref[...] = jnp.dot(dcum_row, tril.astype(F32), precision=HI, preferred_element_type=F32)
        sq_ref[...] = sq_col
        cms_ref[...] = cms_row

    smem = pl.BlockSpec(memory_space=pltpu.SMEM)
    return pl.pallas_call(
        body, name=name, grid=(N_GROUPS, nc),
        in_specs=[smem, x_spec, b_spec, c_spec, colm, rowm, colv, rowv, colv, rowv, st_spec, x_spec],
        out_specs=(x_spec, bo_spec, bo_spec, colm, rowm, colm, rowm, dd_spec),
        out_shape=(_sds((T, D)), _sds((T, D // 2)), _sds((T, D // 2)), _sds((N_GROUPS, T, LANES)), _sds((N_GROUPS, SUBLANES, T)),
                   _sds((N_GROUPS, T, LANES)), _sds((N_GROUPS, SUBLANES, T)), _sds((nc, N_GROUPS, SUBLANES, 2 * LANES))),
        scratch_shapes=[pltpu.VMEM((2 * LANES, N_STATE), F32)],
        compiler_params=_cparams(("parallel", "arbitrary")))(dskip, xc, xc, xc, raw_col, raw_row, bias_col, bias_row, a_col, a_row,
                                                            states, dy)


def _adamw(w, g, m, v, name):
    shape = w.shape
    cols = shape[-1]
    w2, g2, m2, v2 = (t.reshape(-1, cols) for t in (w, g, m, v))
    rows = w2.shape[0]
    tr = 256 if (rows % 256 == 0 and rows > 256) else rows
    c1 = 1.0 - ADAM_B1 ** ADAM_STEP
    c2 = 1.0 - ADAM_B2 ** ADAM_STEP

    def body(w_ref, g_ref, m_ref, v_ref, d_ref, mo_ref, vo_ref):
        gv = g_ref[...]
        mn = ADAM_B1 * m_ref[...] + (1.0 - ADAM_B1) * gv
        vn = ADAM_B2 * v_ref[...] + (1.0 - ADAM_B2) * (gv * gv)
        d_ref[...] = -ADAM_LR * ((mn / c1) / (jnp.sqrt(vn / c2) + ADAM_EPS) + ADAM_WD * w_ref[...])
        mo_ref[...] = mn
        vo_ref[...] = vn

    spec = pl.BlockSpec((tr, cols), lambda i: (i, 0))
    out = pl.pallas_call(body, name=name, grid=(rows // tr,), in_specs=[spec] * 4, out_specs=(spec,) * 3,
                         out_shape=(_sds((rows, cols)),) * 3, compiler_params=_cparams(("parallel",)))(w2, g2, m2, v2)
    return tuple(o.reshape(shape) for o in out)


def _place():
    x, y, c = lax.axis_index("x"), lax.axis_index("y"), lax.axis_index("c")
    chips = [(1 - x, y), (x, 1 - y), (1 - x, 1 - y)]
    return x, y, c, chips


_ANY = pl.BlockSpec(memory_space=pl.ANY)


TENSORS = (("e_w_in", "row", 2, 4096, 1284, 1024), ("e_w_out", "row", 2, 2048, 1024, 512), ("o_w_in", "col", 2, 1024, 3072, 768),
           ("o_w_out", "row", 2, 1024, 1024, 256), ("f_w_up", "col", 4, 1024, 5632, 1408), ("f_w_down", "row", 4, 2816, 1024, 704),
           ("ple_w_proj", "col", 4, 256, 1024, 256), ("ple_w_gate", "row", 4, 1024, 1024, 256))
W_GROUPS = ((0,), (1, 2, 3))
G_GROUPS = ((1, 2, 3), (0,))


def _tensor_layer(name, layer):
    if name.startswith("e_"):
        return layer // 2 if layer % 2 == 0 else None
    if name.startswith("o_"):
        return layer // 2 if layer % 2 == 1 else None
    return layer


def _group_items(layers):
    items = []
    for name, kind, L, A, B, n in TENSORS:
        tls = sorted(t for t in (_tensor_layer(name, l) for l in layers) if t is not None)
        if tls:
            assert tls == list(range(tls[0], tls[0] + len(tls)))
            items.append((name, kind, len(tls), A, B, n, tls[0]))
    return items


def _hwin(ref, it, k, h):
    name, kind, Lg, A, B, n, l0 = it
    if kind == "row":
        return ref.at[:, pl.ds(pl.multiple_of(k * n + h * (n // 2), 16), n // 2), :]
    return ref.at[:, pl.ds(pl.multiple_of(h * (A // 2), 16), A // 2), pl.ds(pl.multiple_of(k * n, LANES), n)]


def _shard_dims(kind, A, B, n):
    return (n, B) if kind == "row" else (A, n)


def _cast_into(w, it, me):
    name, kind, Lg, A, B, n, l0 = it
    As, Bs = _shard_dims(kind, A, B, n)

    def body(me_ref, w_ref, o_ref):
        o_ref[...] = w_ref[...].astype(BF16)

    omap = (lambda l, m: (l, m[0], 0)) if kind == "row" else (lambda l, m: (l, 0, m[0]))
    grid_spec = pltpu.PrefetchScalarGridSpec(
        num_scalar_prefetch=1, grid=(Lg,), in_specs=[pl.BlockSpec((None, As, Bs), lambda l, m: (l + l0, 0, 0))],
        out_specs=pl.BlockSpec((None, As, Bs), omap))
    return pl.pallas_call(body, name=f"cast_{name}_{l0}", grid_spec=grid_spec, out_shape=_sds((Lg, A, B), BF16),
                          compiler_params=_cparams(("parallel",)))(me, w.reshape(-1, As, Bs))


_HBM = pl.BlockSpec(memory_space=pltpu.HBM)
_SEM = pl.BlockSpec(memory_space=pltpu.SEMAPHORE)
_EFFECT = pltpu.SideEffectType.DATAFLOW_SIDE_EFFECTING


def _hbm(a):
    return pltpu.with_memory_space_constraint(a, pltpu.HBM)


def _split_start(thru, n_copies, issue, name, after=None):
    N = len(thru)
    has_after = after is not None

    def body(*refs):
        outs = refs[N + has_after:2 * N + has_after]
        send_sems, recv_sems, token = refs[2 * N + has_after:]
        for cp in issue(outs, send_sems, recv_sems):
            cp.start()
        token[...] = jnp.zeros_like(token)

    out = pl.pallas_call(
        body, name=name, in_specs=[_HBM] * N + ([_ANY] if has_after else []),
        out_specs=(_HBM,) * N + (_SEM, _SEM, pl.BlockSpec(memory_space=pltpu.VMEM)),
        out_shape=tuple(pltpu.HBM(a.shape, a.dtype) for a in thru)
        + (pltpu.SemaphoreType.DMA((n_copies,)), pltpu.SemaphoreType.DMA((n_copies,)), _sds((SUBLANES, LANES))),
        input_output_aliases={t: t for t in range(N)},
        compiler_params=pltpu.CompilerParams(has_side_effects=_EFFECT))(*[_hbm(a) for a in thru], *([after] if has_after else []))
    return list(out[:N]), out[N], out[N + 1], out[N + 2]


def _split_wait(thru, send_sems, recv_sems, after, waits, name):
    N = len(thru)

    def body(*refs):
        ins = refs[:N]
        for cp, side in waits(ins, refs[N], refs[N + 1]):
            if side == "send":
                cp.wait_send()
            else:
                cp.wait_recv()

    out = pl.pallas_call(
        body, name=name, in_specs=[_HBM] * N + [_SEM, _SEM, _ANY], out_specs=(_HBM,) * N,
        out_shape=tuple(pltpu.HBM(a.shape, a.dtype) for a in thru), input_output_aliases={t: t for t in range(N)},
        compiler_params=pltpu.CompilerParams(has_side_effects=_EFFECT))(*thru, send_sems, recv_sems, after)
    return list(out)


def _rcopy(send_sems, recv_sems, k, src, dst, to):
    return pltpu.make_async_remote_copy(src_ref=src, dst_ref=dst, send_sem=send_sems.at[k], recv_sem=recv_sems.at[k],
                                        device_id=to, device_id_type=MESH)


def _gather_copies(items, refs, send_sems, recv_sems, what):
    x, y, c, chips = _place()
    me = 2 * x + y
    out = []
    for t, it in enumerate(items):
        mine = _hwin(refs[t], it, me, c)
        for j, (px, py) in enumerate(chips):
            if what == "start":
                out.append(_rcopy(send_sems, recv_sems, 3 * t + j, mine, mine, (px, py, c)))
            else:
                slot = _hwin(refs[t], it, 2 * px + py, c)
                out.append((_rcopy(send_sems, recv_sems, 3 * t + j, mine, mine, (px, py, c)), "send"))
                out.append((_rcopy(send_sems, recv_sems, 3 * t + j, slot, slot, (px, py, c)), "recv"))
    return out


def _gather_start(fulls, items, name, after=None):
    return _split_start(fulls, 3 * len(items), functools.partial(_gather_copies, items, what="start"), name, after)


def _gather_wait(fulls, send_sems, recv_sems, after, items, name):
    return _split_wait(fulls, send_sems, recv_sems, after, functools.partial(_gather_copies, items, what="wait"), name)


def _gather_fwd(fulls, items, name, ws=None):
    N = len(fulls)
    has_ws = ws is not None

    def body(*refs):
        outs = refs[N + has_ws:2 * N + has_ws]
        rest = refs[2 * N + has_ws:]
        x, y, c, chips = _place()
        me = 2 * x + y
        sib = (x, y, 1 - c)
        if has_ws:
            ws_ref = refs[N]
            WS_ref, send_sems, recv_sems, lsem = rest
            loc = pltpu.make_async_copy(ws_ref, WS_ref.at[me], lsem)
            loc.start()
        else:
            send_sems, recv_sems = rest
        rc = functools.partial(_rcopy, send_sems, recv_sems)
        cps = []
        for t, it in enumerate(items):
            for j, (px, py) in enumerate(chips):
                slot = _hwin(outs[t], it, 2 * px + py, c)
                cps.append(rc(3 * t + j, slot, slot, sib))
        if has_ws:
            cps += [rc(3 * N + j, ws_ref, WS_ref.at[me], (*chip, c)) for j, chip in enumerate(chips)]
        for cp in cps:
            cp.start()
        for t, it in enumerate(items):
            for j, (px, py) in enumerate(chips):
                oslot = _hwin(outs[t], it, 2 * px + py, 1 - c)
                rc(3 * t + j, oslot, oslot, sib).wait_recv()
        if has_ws:
            for j, (px, py) in enumerate(chips):
                sslot = WS_ref.at[2 * px + py]
                rc(3 * N + j, sslot, sslot, sib).wait_recv()
        for cp in cps:
            cp.wait_send()
        if has_ws:
            loc.wait()

    ns = 3 * N + (3 if has_ws else 0)
    out_shape = tuple(_sds(f.shape, f.dtype) for f in fulls)
    scratch = [pltpu.SemaphoreType.DMA((ns,)), pltpu.SemaphoreType.DMA((ns,))]
    args = list(fulls)
    if has_ws:
        out_shape += (_sds((4,) + ws.shape, ws.dtype),)
        scratch.append(pltpu.SemaphoreType.DMA(()))
        args.append(ws)
    out = pl.pallas_call(
        body, name=name, in_specs=[_ANY] * len(args), out_specs=(_ANY,) * len(out_shape), out_shape=out_shape,
        input_output_aliases={t: t for t in range(N)}, scratch_shapes=scratch,
        compiler_params=pltpu.CompilerParams(has_side_effects=True))(*args)
    return (list(out[:N]), out[N]) if has_ws else (list(out), None)


def _half_shape(it):
    name, kind, Lg, A, B, n, l0 = it
    return (Lg, 4, n // 2, B) if kind == "row" else (Lg, A // 2, B)


def _piece_shape(it):
    name, kind, Lg, A, B, n, l0 = it
    return (Lg, n // 2, B) if kind == "row" else (Lg, A // 2, n)


def _swap_grads(gs, items, name):
    N = len(gs)

    def body(*refs):
        g_refs, o_refs = refs[:N], refs[N:2 * N]
        send_sems, recv_sems = refs[2 * N:]
        x, y, c, _ = _place()
        sib = (x, y, 1 - c)
        cps = []
        for t, it in enumerate(items):
            name_, kind, Lg, A, B, n, l0 = it
            if kind == "row":
                for k in range(4):
                    cps.append(_rcopy(send_sems, recv_sems, 4 * t + k, _hwin(g_refs[t], it, k, 1 - c), o_refs[t].at[:, k], sib))
            else:
                src = g_refs[t].at[:, pl.ds(pl.multiple_of((1 - c) * (A // 2), 16), A // 2), :]
                cps.append(_rcopy(send_sems, recv_sems, 4 * t, src, o_refs[t], sib))
        for cp in cps:
            cp.start()
        for cp in cps:
            cp.wait()

    return pl.pallas_call(
        body, name=name, in_specs=[_ANY] * N, out_specs=(_ANY,) * N, out_shape=tuple(_sds(_half_shape(it)) for it in items),
        scratch_shapes=[pltpu.SemaphoreType.DMA((4 * N,)), pltpu.SemaphoreType.DMA((4 * N,))],
        compiler_params=pltpu.CompilerParams(has_side_effects=True))(*gs)


def _add_half(g, ra, it, cvec):
    name, kind, Lg, A, B, n, l0 = it
    if kind == "row":
        blk = (None, n // 2, B)
        grid = (Lg, 4)
        g_spec = pl.BlockSpec(blk, lambda l, k, cr: (l, 2 * k + cr[0], 0))
        h_spec = pl.BlockSpec((None, None, n // 2, B), lambda l, k, cr: (l, k, 0, 0))
    else:
        tr = _tile(A // 2, [], (256, 128))
        nb = (A // 2) // tr
        grid = (Lg, nb)
        g_spec = pl.BlockSpec((None, tr, B), lambda l, i, cr: (l, cr[0] * nb + i, 0))
        h_spec = pl.BlockSpec((None, tr, B), lambda l, i, cr: (l, i, 0))

    def body(c_ref, g_ref, r_ref, o_ref):
        o_ref[...] = (g_ref[...] + r_ref[...]).astype(BF16)

    grid_spec = pltpu.PrefetchScalarGridSpec(num_scalar_prefetch=1, grid=grid, in_specs=[g_spec, h_spec], out_specs=h_spec)
    return pl.pallas_call(body, name=f"addhalf_{name}_{l0}", grid_spec=grid_spec, out_shape=_sds(_half_shape(it), BF16),
                          compiler_params=_cparams(("parallel", "parallel")))(cvec, g, ra)


def _scatter_copies(items, refs, send_sems, recv_sems, what):
    N = len(items)
    x, y, c, chips = _place()
    out = []
    for t, it in enumerate(items):
        name, kind, Lg, A, B, n, l0 = it
        for j, (px, py) in enumerate(chips):
            k = 2 * px + py
            src = refs[t].at[:, k] if kind == "row" else refs[t].at[:, :, pl.ds(pl.multiple_of(k * n, LANES), n)]
            cp = _rcopy(send_sems, recv_sems, 3 * t + j, src, refs[N + t].at[j], (px, py, c))
            if what == "start":
                out.append(cp)
            else:
                out += [(cp, "send"), (cp, "recv")]
    return out


def _scatter_start(ps, items, name):
    lands = [lax.empty((3,) + _piece_shape(it), BF16) for it in items]
    return _split_start(list(ps) + lands, 3 * len(items), functools.partial(_scatter_copies, items, what="start"), name)


def _scatter_wait(thru, send_sems, recv_sems, after, items, name):
    return _split_wait(thru, send_sems, recv_sems, after, functools.partial(_scatter_copies, items, what="wait"), name)


def _sum_own(p, rc, it, mevec, buf):
    name, kind, Lg, A, B, n, l0 = it
    As, Bs = _shard_dims(kind, A, B, n)
    L = [s[2] for s in TENSORS if s[0] == name][0]
    hb = (As // 2, Bs)
    has_buf = buf is not None

    def body(*refs):
        p_ref, r0, r1, r2 = refs[1:5]
        o_ref = refs[5 + has_buf]
        o_ref[...] = ((p_ref[...].astype(F32) + r0[...].astype(F32)) + r1[...].astype(F32)) + r2[...].astype(F32)

    if kind == "row":
        p_spec = pl.BlockSpec((None, None) + hb, lambda l, m: (l, m[0], 0, 0))
    else:
        p_spec = pl.BlockSpec((None,) + hb, lambda l, m: (l, 0, m[0]))
    r_specs = [pl.BlockSpec((None, None) + hb, functools.partial(lambda l, m, j: (j, l, 0, 0), j=j)) for j in range(3)]
    in_specs = [p_spec] + r_specs + ([_ANY] if has_buf else [])
    grid_spec = pltpu.PrefetchScalarGridSpec(num_scalar_prefetch=1, grid=(Lg,), in_specs=in_specs,
                                             out_specs=pl.BlockSpec((None,) + hb, lambda l, m: (l + l0, m[1], 0)))
    args = (mevec, p, rc, rc, rc) + ((buf,) if has_buf else ())
    return pl.pallas_call(body, name=f"sumown_{name}_{l0}", grid_spec=grid_spec, out_shape=_sds((L, As, Bs)),
                          input_output_aliases={5: 0} if has_buf else {}, compiler_params=_cparams(("parallel",)))(*args)


def _join_halves(rs, items, name):
    N = len(rs)

    def body(*refs):
        outs = refs[N:2 * N]
        send_sems, recv_sems = refs[2 * N:]
        x, y, c, _ = _place()
        sib = (x, y, 1 - c)

        def half(t, h):
            name_, kind, Lg, A, B, n, l0 = items[t]
            hr = _shard_dims(kind, A, B, n)[0] // 2
            return outs[t].at[pl.ds(l0, Lg), pl.ds(pl.multiple_of(h * hr, SUBLANES), hr), :]

        cps = [_rcopy(send_sems, recv_sems, t, half(t, c), half(t, c), sib) for t in range(N)]
        for cp in cps:
            cp.start()
        for t in range(N):
            _rcopy(send_sems, recv_sems, t, half(t, 1 - c), half(t, 1 - c), sib).wait_recv()
        for cp in cps:
            cp.wait_send()

    return list(pl.pallas_call(
        body, name=name, in_specs=[_ANY] * N, out_specs=(_ANY,) * N, out_shape=tuple(_sds(r.shape, r.dtype) for r in rs),
        input_output_aliases={t: t for t in range(N)},
        scratch_shapes=[pltpu.SemaphoreType.DMA((N,)), pltpu.SemaphoreType.DMA((N,))],
        compiler_params=pltpu.CompilerParams(has_side_effects=True))(*rs))


def _allgather_small(v):
    m_per, n = v.shape

    def body(x_ref, out_ref, send_sems, recv_sems, local_sem):
        x, y, c, chips = _place()
        me, sibling = (x, y, c), (x, y, 1 - c)

        def rows(px, py, pc):
            return out_ref.at[pl.ds(pl.multiple_of((4 * px + 2 * py + pc) * m_per, SUBLANES), m_per), :]

        def copy(k, block, to, src=None):
            return pltpu.make_async_remote_copy(src_ref=rows(*block) if src is None else src, dst_ref=rows(*block),
                                                send_sem=send_sems.at[k], recv_sem=recv_sems.at[k], device_id=to, device_id_type=MESH)

        mine = pltpu.make_async_copy(x_ref, rows(*me), local_sem)
        mine.start()
        first = [copy(0, me, sibling, src=x_ref)]
        first += [copy(1 + j, me, (*chip, c), src=x_ref) for j, chip in enumerate(chips)]
        for cp in first:
            cp.start()
        passed = [copy(4 + j, (*chip, c), sibling) for j, chip in enumerate(chips)]
        for j, chip in enumerate(chips):
            copy(1 + j, (*chip, c), me).wait_recv()
            passed[j].start()
        copy(0, sibling, me).wait_recv()
        for j, chip in enumerate(chips):
            copy(4 + j, (*chip, 1 - c), me).wait_recv()
        for cp in first + passed:
            cp.wait_send()
        mine.wait()

    vm = pl.BlockSpec(memory_space=pltpu.VMEM)
    return pl.pallas_call(body, name="allgather_small", in_specs=[vm], out_specs=vm, out_shape=_sds((8 * m_per, n)),
                          scratch_shapes=[pltpu.SemaphoreType.DMA((7,)), pltpu.SemaphoreType.DMA((7,)), pltpu.SemaphoreType.DMA(())],
                          compiler_params=pltpu.CompilerParams(has_side_effects=True, vmem_limit_bytes=VMEM_LIMIT))(v)


def _sum8(v, m_per):
    def body(v_ref, o_ref):
        acc = v_ref[0:m_per, :]
        for k in range(1, 8):
            acc = acc + v_ref[k * m_per:(k + 1) * m_per, :]
        o_ref[...] = acc

    return pl.pallas_call(body, name="small_sum_devices", out_shape=_sds((m_per, v.shape[1])),
                          compiler_params=pltpu.CompilerParams(vmem_limit_bytes=VMEM_LIMIT))(v)


SMALL_SHARDED = (("e_conv_a_w", 2), ("e_conv_b_w", 2), ("o_conv_w", 2), ("f_conv_w", 2), ("ln_g", 2), ("ln_b", 2))
SMALL_REPL = ("e_conv_a_b", "e_ln_a_g", "e_ln_a_b", "e_conv_b_b", "e_dt_bias", "e_a_log", "e_d_skip", "e_norm_b_g", "f_conv_b")

WEIGHT_ORDER = ('e_w_in', 'e_conv_a_w', 'e_conv_a_b', 'e_ln_a_g', 'e_ln_a_b', 'e_conv_b_w', 'e_conv_b_b', 'e_dt_bias', 'e_a_log',
                'e_d_skip', 'e_norm_b_g', 'e_w_out', 'o_w_in', 'o_conv_w', 'o_w_out', 'f_w_up', 'f_conv_w', 'f_conv_b', 'f_w_down',
                'ple_w_proj', 'ple_w_gate', 'ln_g', 'ln_b')


def _pack_rows(parts, width, total_rows, dtype):
    flat = jnp.concatenate([p.reshape(-1).astype(dtype) for p in parts])
    flat = jnp.pad(flat, (0, total_rows * width - flat.shape[0]))
    return flat.reshape(total_rows, width)


def _unpack_rows(buf, shapes):
    flat = buf.reshape(-1)
    out, pos = [], 0
    for s in shapes:
        n = math.prod(s)
        out.append(flat[pos:pos + n].reshape(s))
        pos += n
    return out


def _small_rows(shapes):
    n = sum(math.prod(s) for s in shapes)
    return -(-n // (LANES * SUBLANES)) * SUBLANES


E_PAD = 5248
SEG_A, SEG_Z, SEG_X, SEG_DT = (0, 2 * D), (2 * D, D), (3 * D, 2 * D), (5 * D, LANES)
G_SHAPES = {"e_w_in": (2, D, E_PAD), "e_w_out": (2, 2 * D, D), "o_w_in": (2, D, 3 * D), "o_w_out": (2, D, D),
            "f_w_up": (4, D, 2 * D_FF), "f_w_down": (4, D_FF, D), "ple_w_proj": (4, PLE, D), "ple_w_gate": (4, D, D)}


def _padcols(w, width):
    return jnp.pad(w, ((0, 0), (0, width - w.shape[1])))


def _fold_rows(dw, K):
    return dw.reshape(K, SUBLANES, dw.shape[-1]).sum(1)


class GradBuffers(dict):
    def __init__(self):
        super().__init__()
        self.where = {}
        for gi, layers in enumerate(G_GROUPS):
            for name, kind, Lg, A, B, n, l0 in _group_items(layers):
                for k in range(Lg):
                    self.where[(name, l0 + k)] = (gi, k, Lg)
        self.current = {}

    def into(self, name, layer, r0=0, c0=0):
        gi, k, Lg = self.where[(name, layer)]
        self.current[name] = (name, gi)
        return (self.get((name, gi)), (Lg,) + G_SHAPES[name][1:], (k,), r0, c0)

    def __setitem__(self, name, value):
        super().__setitem__(self.current[name], value)


def _local_step(x, p, target, W, comm=None):
    T = x.shape[0]
    xb = x
    saved = []
    xc_f = x
    for i in range(DEPTH):
        j = i // 2
        L = {}
        L["x"], L["xb"] = xc_f, xb
        tok = comm.layer_starts(i, xb) if comm is not None else None
        if i % 2 == 0:
            def w_in(seg, c0=0, cols=None, j=j):
                return V(W["e_w_in"], (j,), c0=seg[0] + c0, cols=seg[1] if cols is None else cols)

            ua = _mm(xb, w_in(SEG_A), "nn", f"l{i}_in_a", after=tok)
            z = _mm(xb, w_in(SEG_Z), "nn", f"l{i}_in_z")
            xu = _mm(xb, w_in(SEG_X), "nn", f"l{i}_in_xbc")
            udt = _mm(xb, w_in(SEG_DT), "nn", f"l{i}_in_dt")
            ac = _conv_a_fwd(ua, W["e_conv_a_w"][j], W["e_conv_a_b"][j][None], f"l{i}_conv_a")
            ya = _ln_silu_fwd(ac, W["e_ln_a_g"][j][None], W["e_ln_a_b"][j][None], f"l{i}_ln_a")
            xc = _conv_b_fwd(xu, W["e_conv_b_w"][j], W["e_conv_b_b"][j][None], f"l{i}_conv_b")
            sm = _ssd_small_inputs(udt[:, :N_HEADS], W["e_dt_bias"][j], W["e_a_log"][j])
            y, states = _ssd_fwd(xc, *sm, W["e_d_skip"][j], f"l{i}_ssd")
            yb = _gate_rms_fwd(y, z, W["e_norm_b_g"][j][None], f"l{i}_gate_rms")
            mix = _mm(ya, V(W["e_w_out"], (j,), rows=D), "nn", f"l{i}_out_a")
            mix = _mm(yb, V(W["e_w_out"], (j,), r0=D), "nn", f"l{i}_out_b", add=mix)
            L.update(ua=ua, z=z, xu=xu, udt=udt, ac=ac, ya=ya, xc=xc, sm=sm, y=y, states=states, yb=yb, w_in=w_in)
        else:
            uo = _mm(xb, V(W["o_w_in"], (j,)), "nn", f"l{i}_in", after=tok)
            sc = _conv_c_fwd(uo, W["o_conv_w"][j], f"l{i}_conv_c")
            mix = _mm(sc, V(W["o_w_out"], (j,)), "nn", f"l{i}_out")
            L.update(uo=uo, sc=sc)
        h1, x1, x1b = _res_ln_fwd(xc_f, [mix], None, W["ln_g"][i, 0][None], W["ln_b"][i, 0][None], f"l{i}_ln1")
        up = _mm(x1b, V(W["f_w_up"], (i,)), "nn", f"l{i}_ffn_up")
        act = _conv_f_fwd(up, W["f_conv_w"][i], W["f_conv_b"][i][None], f"l{i}_conv_f")
        ffn = _mm(act, V(W["f_w_down"], (i,)), "nn", f"l{i}_ffn_down")
        pv = V(p, (i, 0))
        pp = _mm(pv, V(W["ple_w_proj"], (i,)), "nn", f"l{i}_ple_proj")
        gl = _mm(x1b, V(W["ple_w_gate"], (i,)), "nn", f"l{i}_ple_gate")
        h2, x2, x2b = _res_ln_fwd(x1, [ffn], (pp, gl), W["ln_g"][i, 1][None], W["ln_b"][i, 1][None], f"l{i}_ln2")
        L.update(h1=h1, x1=x1, x1b=x1b, up=up, act=act, pv=pv, pp=pp, gl=gl, h2=h2)
        saved.append(L)
        xc_f, xb = x2, x2b

    sq, dx = _loss_head(xc_f, target, "loss_head")

    GB = GradBuffers()
    into = GB.into
    tok = None

    G = {n: [None] * (DEPTH if n.startswith(("f_", "ln_")) else DEPTH // 2) for n in WEIGHT_ORDER if n not in G_SHAPES}
    for i in reversed(range(DEPTH)):
        j = i // 2
        L = saved[i]
        dh2, dh2b, dg2, db2, dpp, dgl = _res_ln_bwd(dx, L["h2"], W["ln_g"][i, 1][None], (L["pp"], L["gl"]), f"l{i}_ln2_bwd")
        GB["f_w_down"] = _mm(L["act"], dh2b, "tn", f"l{i}_dw_down", dst=into("f_w_down", i))
        dact = _mm(dh2b, V(W["f_w_down"], (i,)), "nt", f"l{i}_dact", after=tok)
        du1, du2, dw1, dw2, dbf1, dbf2 = _conv_f_bwd(L["up"], W["f_conv_w"][i], W["f_conv_b"][i][None], dact, f"l{i}_conv_f_bwd")
        G["f_conv_w"][i] = jnp.concatenate([_fold_rows(dw1, CONV_F), _fold_rows(dw2, CONV_F)], axis=1)
        G["f_conv_b"][i] = jnp.concatenate([dbf1.sum(0), dbf2.sum(0)])
        GB["f_w_up"] = _mm(L["x1b"], du1, "tn", f"l{i}_dw_up1", dst=into("f_w_up", i))
        GB["f_w_up"] = _mm(L["x1b"], du2, "tn", f"l{i}_dw_up2", dst=into("f_w_up", i, c0=D_FF))
        GB["ple_w_proj"] = _mm(L["pv"], dpp, "tn", f"l{i}_dw_proj", dst=into("ple_w_proj", i))
        GB["ple_w_gate"] = _mm(L["x1b"], dgl, "tn", f"l{i}_dw_gate", dst=into("ple_w_gate", i))
        dx1 = _mm(du1, V(W["f_w_up"], (i,), cols=D_FF), "nt", f"l{i}_dx1_a", add=dh2, add_scale=ALPHA)
        dx1 = _mm(du2, V(W["f_w_up"], (i,), c0=D_FF), "nt", f"l{i}_dx1_b", add=dx1)
        dx1 = _mm(dgl, V(W["ple_w_gate"], (i,)), "nt", f"l{i}_dx1_c", add=dx1)
        dh1, dh1b, dg1, db1 = _res_ln_bwd(dx1, L["h1"], W["ln_g"][i, 0][None], None, f"l{i}_ln1_bwd")
        G["ln_g"][i] = jnp.concatenate([dg1, dg2], axis=0)
        G["ln_b"][i] = jnp.concatenate([db1, db2], axis=0)
        if i % 2 == 0:
            GB["e_w_out"] = _mm(L["ya"], dh1b, "tn", f"l{i}_dw_out_a", dst=into("e_w_out", j))
            GB["e_w_out"] = _mm(L["yb"], dh1b, "tn", f"l{i}_dw_out_b", dst=into("e_w_out", j, r0=D))
            dya = _mm(dh1b, V(W["e_w_out"], (j,), rows=D), "nt", f"l{i}_dya")
            dyb = _mm(dh1b, V(W["e_w_out"], (j,), r0=D), "nt", f"l{i}_dyb")
            dac, dga, dba = _ln_silu_bwd(L["ac"], dya, W["e_ln_a_g"][j][None], W["e_ln_a_b"][j][None], f"l{i}_ln_a_bwd")
            G["e_ln_a_g"][j], G["e_ln_a_b"][j] = dga[0], dba[0]
            dal, dag, dwa, dbca = _conv_a_bwd(L["ua"], W["e_conv_a_w"][j], dac, f"l{i}_conv_a_bwd")
            G["e_conv_a_w"][j] = _fold_rows(dwa, CONV_A)
            G["e_conv_a_b"][j] = dbca.sum(0)
            dy, dz, dgn = _gate_rms_bwd(L["y"], L["z"], dyb, W["e_norm_b_g"][j][None], f"l{i}_gate_rms_bwd")
            G["e_norm_b_g"][j] = dgn[0]
            dxs, dbs, dcs, sq_col, cms_row, dda_col, dda_row, ddp = _ssd_bwd(L["xc"], *L["sm"], W["e_d_skip"][j], L["states"], dy,
                                                                             f"l{i}_ssd_bwd")
            draw, G["e_dt_bias"][j], G["e_a_log"][j] = _ssd_small_grads(L["udt"][:, :N_HEADS], W["e_dt_bias"][j], W["e_a_log"][j],
                                                                       sq_col, cms_row, dda_col, dda_row)
            G["e_d_skip"][j] = ddp[:, :, 0, :].sum(0).reshape(N_HEADS, HEAD_P).sum(1)
            dxu, dwb, dbcb = _conv_b_bwd(L["xu"], W["e_conv_b_w"][j], W["e_conv_b_b"][j][None], dxs, dbs, dcs, f"l{i}_conv_b_bwd")
            G["e_conv_b_w"][j] = _fold_rows(dwb, CONV_B)
            G["e_conv_b_b"][j] = dbcb.sum(0)
            dudt = _padcols(draw, LANES)
            w_in = L["w_in"]
            xb_l = L["xb"]
            for nm, dseg, c0 in (("al", dal, 0), ("ag", dag, D), ("z", dz, SEG_Z[0]), ("xbc", dxu, SEG_X[0]), ("dt", dudt, SEG_DT[0])):
                GB["e_w_in"] = _mm(xb_l, dseg, "tn", f"l{i}_dw_in_{nm}", dst=into("e_w_in", j, c0=c0))
            dx = _mm(dal, w_in(SEG_A, cols=D), "nt", f"l{i}_dx_al", add=dh1, add_scale=ALPHA)
            dx = _mm(dag, w_in(SEG_A, c0=D, cols=D), "nt", f"l{i}_dx_ag", add=dx)
            dx = _mm(dz, w_in(SEG_Z), "nt", f"l{i}_dx_z", add=dx)
            dx = _mm(dxu, w_in(SEG_X), "nt", f"l{i}_dx_xbc", add=dx)
            dx = _mm(dudt, w_in(SEG_DT), "nt", f"l{i}_dx_dt", add=dx)
        else:
            GB["o_w_out"] = _mm(L["sc"], dh1b, "tn", f"l{i}_dw_out", dst=into("o_w_out", j))
            dsc = _mm(dh1b, V(W["o_w_out"], (j,)), "nt", f"l{i}_dsc")
            dbg, dcg, dv, dwc = _conv_c_bwd(L["uo"], W["o_conv_w"][j], dsc, f"l{i}_conv_c_bwd")
            G["o_conv_w"][j] = _fold_rows(dwc, CONV_C)
            xb_l = L["xb"]
            dx = dh1
            for nm, dseg, c0, scale in (("bg", dbg, 0, ALPHA), ("cg", dcg, D, 1.0), ("v", dv, 2 * D, 1.0)):
                GB["o_w_in"] = _mm(xb_l, dseg, "tn", f"l{i}_dw_in_{nm}", dst=into("o_w_in", j, c0=c0))
                dx = _mm(dseg, V(W["o_w_in"], (j,), c0=c0, cols=D), "nt", f"l{i}_dx_{nm}", add=dx, add_scale=scale)
        tok = comm.layer_grads_done(i, GB) if comm is not None else None
    grads = {n: jnp.stack(v) for n, v in G.items()}
    return sq, dx, GB, grads


def _ssd_small_inputs(raw, dt_bias, a_log):
    T = raw.shape[0]
    a = -jnp.exp(a_log)
    rg = raw.reshape(T, N_GROUPS, 4)
    raw_col = jnp.pad(jnp.transpose(rg, (1, 0, 2)), ((0, 0), (0, 0), (0, LANES - 4)))
    raw_row = jnp.pad(jnp.transpose(rg, (1, 2, 0)), ((0, 0), (0, SUBLANES - 4), (0, 0)))

    def colv(v):
        return jnp.pad(v.reshape(N_GROUPS, 1, 4), ((0, 0), (0, 0), (0, LANES - 4)))

    def rowv(v):
        return jnp.pad(v.reshape(N_GROUPS, 4, 1), ((0, 0), (0, SUBLANES - 4), (0, 0)))

    return raw_col, raw_row, colv(dt_bias), rowv(dt_bias), colv(a), rowv(a)


def _ssd_small_grads(raw, dt_bias, a_log, sq_col, cms_row, dda_col, dda_row):
    T = raw.shape[0]

    def join(col, row):
        c = jnp.transpose(col[:, :, :4], (1, 0, 2)).reshape(T, N_HEADS)
        r = jnp.transpose(row[:, :4, :], (2, 0, 1)).reshape(T, N_HEADS)
        return c + r

    a = -jnp.exp(a_log)
    pre = raw + dt_bias
    dt = jax.nn.softplus(pre)
    dda = join(dda_col, dda_row)
    ddt = join(sq_col, cms_row) + a * dda
    draw = ddt * jax.nn.sigmoid(pre)
    da = jnp.sum(dt * dda, axis=0)
    return draw, jnp.sum(draw, axis=0), da * a


def kernel(x, p, e_w_in, e_conv_a_w, e_conv_a_b, e_ln_a_g, e_ln_a_b, e_conv_b_w, e_conv_b_b, e_dt_bias, e_a_log, e_d_skip, e_norm_b_g, e_w_out, o_w_in, o_conv_w, o_w_out, f_w_up, f_conv_w, f_conv_b, f_w_down, ple_w_proj, ple_w_gate, ln_g, ln_b, loss_target, m_e_w_in, m_e_conv_a_w, m_e_conv_a_b, m_e_ln_a_g, m_e_ln_a_b, m_e_conv_b_w, m_e_conv_b_b, m_e_dt_bias, m_e_a_log, m_e_d_skip, m_e_norm_b_g, m_e_w_out, m_o_w_in, m_o_conv_w, m_o_w_out, m_f_w_up, m_f_conv_w, m_f_conv_b, m_f_w_down, m_ple_w_proj, m_ple_w_gate, m_ln_g, m_ln_b, v_e_w_in, v_e_conv_a_w, v_e_conv_a_b, v_e_ln_a_g, v_e_ln_a_b, v_e_conv_b_w, v_e_conv_b_b, v_e_dt_bias, v_e_a_log, v_e_d_skip, v_e_norm_b_g, v_e_w_out, v_o_w_in, v_o_conv_w, v_o_w_out, v_f_w_up, v_f_conv_w, v_f_conv_b, v_f_w_down, v_ple_w_proj, v_ple_w_gate, v_ln_g, v_ln_b):
    args = dict(locals())
    w_shard = {n: args[n] for n in WEIGHT_ORDER}
    m_shard = {n: args["m_" + n] for n in WEIGHT_ORDER}
    v_shard = {n: args["v_" + n] for n in WEIGHT_ORDER}
    xi, yi, ci = lax.axis_index("x"), lax.axis_index("y"), lax.axis_index("c")
    chip = 2 * xi + yi

    mevec = jnp.stack([chip, ci]).astype(jnp.int32)
    small_shapes = [w_shard[n].shape for n, _ in SMALL_SHARDED]
    sr = _small_rows(small_shapes)
    ws = _pack_rows([w_shard[n] for n, _ in SMALL_SHARDED], LANES, sr, F32)
    W = {n: w_shard[n] for n in SMALL_REPL}
    W.update({s[0]: Layers(s[2]) for s in TENSORS})
    w_items = [_group_items(layers) for layers in W_GROUPS]
    g_items = [_group_items(layers) for layers in G_GROUPS]

    def install(items, fulls):
        for it, f in zip(items, fulls):
            if it[0] == "e_w_in":
                f = jnp.transpose(f.reshape(it[2], 4, D, E_IN // 4), (0, 2, 1, 3)).reshape(it[2], D, E_IN)
                f = jnp.pad(f, ((0, 0), (0, 0), (0, E_PAD - E_IN)))
            W[it[0]].put(f, it[6])

    casts = [[_cast_into(w_shard[it[0]], it, mevec[:1]) for it in items] for items in w_items]
    fulls, ssem, rsem, _ = _gather_start(casts[0], w_items[0], "gather_start_0")
    fulls = _gather_wait(fulls, ssem, rsem, casts[-1][-1], w_items[0], "gather_wait_0")
    fulls, WS = _gather_fwd(fulls, w_items[0], "gather_fwd_0", ws)
    install(w_items[0], fulls)
    started = {}
    after = fulls[0]
    for gi in range(1, len(w_items)):
        started[gi] = _gather_start(casts[gi], w_items[gi], f"gather_start_{gi}", after)
        after = started[gi][3]
    parts_s = [_unpack_rows(WS[k], small_shapes) for k in range(4)]
    for idx, (n, ax) in enumerate(SMALL_SHARDED):
        W[n] = jnp.concatenate([parts_s[k][idx] for k in range(4)], axis=ax)

    class Comm:
        sent = {}
        last_token = None

        def layer_starts(self, layer, after):
            if layer == 0:
                return started[len(w_items) - 1][3] if len(w_items) > 1 else None
            for gi in range(1, len(W_GROUPS)):
                if W_GROUPS[gi][0] == layer:
                    fulls, ssem, rsem, _ = started[gi]
                    fulls = _gather_wait(fulls, ssem, rsem, after, w_items[gi], f"gather_wait_{gi}")
                    fulls, _ = _gather_fwd(fulls, w_items[gi], f"gather_fwd_{gi}")
                    install(w_items[gi], fulls)
            return None

        def layer_grads_done(self, layer, GB):
            tok = None
            for gi, layers in enumerate(G_GROUPS):
                if min(layers) == layer:
                    items = g_items[gi]
                    gs = []
                    for it in items:
                        g = GB[(it[0], gi)]
                        if it[0] == "e_w_in":
                            g = jnp.transpose(g[:, :, :E_IN].reshape(it[2], D, 4, E_IN // 4), (0, 2, 1, 3)).reshape(it[2], 4 * D, E_IN // 4)
                        gs.append(g)
                    ras = _swap_grads(gs, items, f"swap_grads_{gi}")
                    ps = [_add_half(g, ra, it, mevec[1:]) for g, ra, it in zip(gs, ras, items)]
                    thru, ssem, rsem, tok = _scatter_start(ps, items, f"scatter_start_{gi}")
                    self.sent[gi] = (thru, ssem, rsem, tok)
            return tok

    comm = Comm()

    sq, dx, GB, G = _local_step(x[0], p, loss_target[0], W, comm)
    loss = lax.psum(0.5 * sq[0, 0] / D, ("x", "y", "c"))
    grad_x = dx[None]

    def shard_of(g, ax, k):
        n = g.shape[ax] // 4
        return lax.slice_in_dim(g, k * n, (k + 1) * n, axis=ax)

    reduced = {}
    after = comm.sent[len(g_items) - 1][3]
    for gi, items in enumerate(g_items):
        thru, ssem, rsem, _ = comm.sent[gi]
        thru = _scatter_wait(thru, ssem, rsem, after, items, f"scatter_wait_{gi}")
        ps, rcs = thru[:len(items)], thru[len(items):]
        rs = [_sum_own(pt, rc, it, mevec, reduced.get(it[0])) for pt, rc, it in zip(ps, rcs, items)]
        rs = _join_halves(rs, items, f"join_halves_{gi}")
        reduced.update({it[0]: r for it, r in zip(items, rs)})
        after = rs[0]
    gbig = {s[0]: reduced[s[0]].reshape(w_shard[s[0]].shape) for s in TENSORS}

    small_all = ([shard_of(G[n], ax, k) for k in range(4) for n, ax in SMALL_SHARDED] + [G[n] for n in SMALL_REPL])
    small_all_shapes = [t.shape for t in small_all]
    mr = _small_rows(small_all_shapes)
    sg = _sum8(_allgather_small(_pack_rows(small_all, LANES, mr, F32)), mr)
    sparts = _unpack_rows(sg, small_all_shapes)
    ns = len(SMALL_SHARDED)
    gsmall = {}
    for idx, (n, ax) in enumerate(SMALL_SHARDED):
        stacked = jnp.stack([sparts[k * ns + idx] for k in range(4)])
        gsmall[n] = lax.dynamic_index_in_dim(stacked, chip, axis=0, keepdims=False)
    for idx, n in enumerate(SMALL_REPL):
        gsmall[n] = sparts[4 * ns + idx]

    grads, deltas, new_m, new_v = [], [], [], []
    for n in WEIGHT_ORDER:
        g = gbig[n] if n in gbig else gsmall[n]
        d, mn, vn = _adamw(w_shard[n], g, m_shard[n], v_shard[n], f"adamw_{n}")
        grads.append(g)
        deltas.append(d)
        new_m.append(mn)
        new_v.append(vn)
    return (loss, grad_x, *grads, *deltas, *new_m, *new_v)
```

```python
import functools
import math

import jax
import jax.numpy as jnp
from jax import lax
from jax.experimental import pallas as pl
from jax.experimental.pallas import tpu as pltpu

F32 = jnp.float32
BF16 = jnp.bfloat16
MESH = pl.DeviceIdType.MESH

DEPTH = 4
ALPHA = (2.0 * DEPTH) ** 0.25
LN_EPS = 1e-5
D = 1024
HEAD_P = 64
N_STATE = 128
N_HEADS = 16
N_GROUPS = 4
CONV_A, CONV_B, CONV_C, CONV_F = 31, 4, 3, 3
D_FF = 2816
PLE = 256
E_IN = 5136

ADAM_LR, ADAM_B1, ADAM_B2, ADAM_EPS, ADAM_WD, ADAM_STEP = 0.001, 0.9, 0.999, 1e-08, 0.01, 10

LANES = 128
SUBLANES = 8
VMEM_LIMIT = 56 * 1024 * 1024
SSD_Q = 128
CONV_R = 128
CONV_PAD = 32
ROW_T = 256
HI = lax.Precision.HIGHEST


def _cparams(sem=None):
    return pltpu.CompilerParams(dimension_semantics=sem, vmem_limit_bytes=VMEM_LIMIT)


def _sig(v):
    return jax.nn.sigmoid(v)


_DIMS = {"nn": (((1,), (0,)), ((), ())), "nt": (((1,), (1,)), ((), ())), "tn": (((0,), (0,)), ((), ()))}


class Layers:
    def __init__(self, n_layers):
        self.where = [None] * n_layers

    def put(self, arr, l0):
        for k in range(arr.shape[0]):
            self.where[l0 + k] = (arr, k)


class V:
    def __init__(self, arr, lead=(), r0=0, c0=0, rows=None, cols=None):
        if isinstance(arr, Layers):
            arr, k = arr.where[lead[0]]
            lead = (k,) + tuple(lead[1:])
        self.arr, self.lead, self.r0, self.c0 = arr, tuple(lead), r0, c0
        R, C = arr.shape[-2:]
        self.rows = R - r0 if rows is None else rows
        self.cols = C - c0 if cols is None else cols

    def spec(self, br, bc, fn):
        assert self.r0 % br == 0 and self.c0 % bc == 0, (self.r0, self.c0, br, bc)
        ro, co, lead = self.r0 // br, self.c0 // bc, self.lead

        def index(i, j, k):
            r, c = fn(i, j, k)
            return lead + (r + ro, c + co)

        return pl.BlockSpec((None,) * len(lead) + (br, bc), index)


def _v(t):
    return t if isinstance(t, V) else V(t)


def _tile(n, offs, cands):
    for c in cands:
        if n % c == 0 and all(o % c == 0 for o in offs):
            return c
    raise ValueError((n, offs))


_TILES = (1024, 1408, 512, 256, 128)


def _mm(a, b, mode, name, out_dtype=F32, add=None, add_scale=1.0, dst=None, after=None):
    a, b = _v(a), _v(b)
    add = _v(add) if add is not None else None
    if mode == "nn":
        M, K, K2, N = a.rows, a.cols, b.rows, b.cols
        am, ak, bk, bn = a.r0, a.c0, b.r0, b.c0
    elif mode == "nt":
        M, K, N, K2 = a.rows, a.cols, b.rows, b.cols
        am, ak, bn, bk = a.r0, a.c0, b.r0, b.c0
    else:
        K, M, K2, N = a.rows, a.cols, b.rows, b.cols
        ak, am, bk, bn = a.r0, a.c0, b.r0, b.c0
    assert K == K2, (name, mode, M, K, K2, N)
    if dst is None:
        buf, full_shape, o_lead, o_r0, o_c0 = None, (M, N), (), 0, 0
    else:
        buf, full_shape, o_lead, o_r0, o_c0 = dst
    tm = _tile(M, [am, o_r0] + ([add.r0] if add else []), _TILES)
    tn = _tile(N, [bn, o_c0] + ([add.c0] if add else []), _TILES)
    tk = _tile(K, [ak, bk], _TILES)
    nk = K // tk
    has_add, has_buf, has_after = add is not None, buf is not None, after is not None

    def body(*refs):
        a_ref, b_ref = refs[0], refs[1]
        add_ref = refs[2] if has_add else None
        o_ref = refs[2 + has_add + has_buf + has_after]

        def finish(r):
            if has_add:
                r = r + add_scale * add_ref[...].astype(F32)
            o_ref[...] = r.astype(o_ref.dtype)

        part = lax.dot_general(a_ref[...].astype(BF16), b_ref[...].astype(BF16), _DIMS[mode], preferred_element_type=F32)
        if nk == 1:
            finish(part)
        else:
            acc_ref = refs[-1]
            k = pl.program_id(2)

            @pl.when(k == 0)
            def _():
                acc_ref[...] = part

            @pl.when(jnp.logical_and(k > 0, k < nk - 1))
            def _():
                acc_ref[...] += part

            @pl.when(k == nk - 1)
            def _():
                finish(acc_ref[...] + part)

    if mode == "tn":
        a_spec = a.spec(tk, tm, lambda i, j, k: (k, i))
    else:
        a_spec = a.spec(tm, tk, lambda i, j, k: (i, k))
    if mode == "nt":
        b_spec = b.spec(tn, tk, lambda i, j, k: (j, k))
    else:
        b_spec = b.spec(tk, tn, lambda i, j, k: (k, j))
    in_specs, args = [a_spec, b_spec], [a.arr, b.arr]
    if has_add:
        in_specs.append(add.spec(tm, tn, lambda i, j, k: (i, j)))
        args.append(add.arr)
    aliases = {}
    if has_buf:
        aliases = {len(args): 0}
        in_specs.append(pl.BlockSpec(memory_space=pl.ANY))
        args.append(buf)
        out_dtype = buf.dtype
    if has_after:
        in_specs.append(pl.BlockSpec(memory_space=pl.ANY))
        args.append(after)
    o_view = V(jax.ShapeDtypeStruct(full_shape, out_dtype), o_lead, o_r0, o_c0, M, N)
    return pl.pallas_call(
        body, name=name, grid=(M // tm, N // tn, nk), in_specs=in_specs, out_specs=o_view.spec(tm, tn, lambda i, j, k: (i, j)),
        out_shape=jax.ShapeDtypeStruct(full_shape, out_dtype), input_output_aliases=aliases,
        scratch_shapes=[pltpu.VMEM((tm, tn), F32)] if nk > 1 else [],
        compiler_params=_cparams(("parallel", "parallel", "arbitrary")))(*args)


def _rows(T, width=D):
    return pl.BlockSpec((ROW_T, width), lambda i: (i, 0))


def _vec(width=D):
    return pl.BlockSpec((1, width), lambda i: (0, 0))


def _ln_stats(h):
    mu = jnp.mean(h, axis=-1, keepdims=True)
    hc = h - mu
    var = jnp.mean(hc * hc, axis=-1, keepdims=True)
    rstd = lax.rsqrt(var + LN_EPS)
    return hc * rstd, rstd


def _res_ln_fwd(x, adds, ple, g, b, name):
    T = x.shape[0]
    n_add = len(adds)
    has_ple = ple is not None

    def body(*refs):
        x_ref = refs[0]
        add_refs = refs[1:1 + n_add]
        pos = 1 + n_add
        if has_ple:
            pp_ref, gl_ref = refs[pos], refs[pos + 1]
            pos += 2
        g_ref, b_ref, h_ref, y_ref, yb_ref = refs[pos:pos + 5]
        h = ALPHA * x_ref[...]
        for r in add_refs:
            h = h + r[...]
        if has_ple:
            h = h + pp_ref[...] * _sig(gl_ref[...])
        xhat, _ = _ln_stats(h)
        y = xhat * g_ref[...] + b_ref[...]
        h_ref[...] = h
        y_ref[...] = y
        yb_ref[...] = y.astype(BF16)

    n_in = 1 + n_add + (2 if has_ple else 0)
    args = (x,) + tuple(adds) + (tuple(ple) if has_ple else ()) + (g, b)
    return pl.pallas_call(
        body, name=name, grid=(T // ROW_T,), in_specs=[_rows(T)] * n_in + [_vec(), _vec()],
        out_specs=(_rows(T), _rows(T), _rows(T)),
        out_shape=(jax.ShapeDtypeStruct((T, D), F32), jax.ShapeDtypeStruct((T, D), F32), jax.ShapeDtypeStruct((T, D), BF16)),
        compiler_params=_cparams(("parallel",)))(*args)


def _res_ln_bwd(dy, h, g, ple, name):
    T = dy.shape[0]
    has_ple = ple is not None

    def body(*refs):
        if has_ple:
            dy_ref, h_ref, g_ref, pp_ref, gl_ref, dh_ref, dhb_ref, dg_ref, db_ref, dpp_ref, dgl_ref = refs
        else:
            dy_ref, h_ref, g_ref, dh_ref, dhb_ref, dg_ref, db_ref = refs
        i = pl.program_id(0)

        @pl.when(i == 0)
        def _():
            dg_ref[...] = jnp.zeros_like(dg_ref)
            db_ref[...] = jnp.zeros_like(db_ref)

        dyv = dy_ref[...]
        xhat, rstd = _ln_stats(h_ref[...])
        dg_ref[...] += jnp.sum(dyv * xhat, axis=0, keepdims=True)
        db_ref[...] += jnp.sum(dyv, axis=0, keepdims=True)
        dxh = dyv * g_ref[...]
        dh = rstd * (dxh - jnp.mean(dxh, axis=-1, keepdims=True) - xhat * jnp.mean(dxh * xhat, axis=-1, keepdims=True))
        dh_ref[...] = dh
        dhb_ref[...] = dh.astype(BF16)
        if has_ple:
            s = _sig(gl_ref[...])
            dpp_ref[...] = (dh * s).astype(BF16)
            dgl_ref[...] = (dh * pp_ref[...] * s * (1.0 - s)).astype(BF16)

    args = (dy, h, g) + (tuple(ple) if has_ple else ())
    in_specs = [_rows(T), _rows(T), _vec()] + ([_rows(T), _rows(T)] if has_ple else [])
    out_specs = [_rows(T), _rows(T), _vec(), _vec()] + ([_rows(T), _rows(T)] if has_ple else [])
    out_shape = [jax.ShapeDtypeStruct((T, D), F32), jax.ShapeDtypeStruct((T, D), BF16),
                 jax.ShapeDtypeStruct((1, D), F32), jax.ShapeDtypeStruct((1, D), F32)]
    if has_ple:
        out_shape += [jax.ShapeDtypeStruct((T, D), BF16), jax.ShapeDtypeStruct((T, D), BF16)]
    return pl.pallas_call(
        body, name=name, grid=(T // ROW_T,), in_specs=in_specs, out_specs=tuple(out_specs), out_shape=tuple(out_shape),
        compiler_params=_cparams(("arbitrary",)))(*args)


def _ln_silu_fwd(ac, g, b, name):
    T = ac.shape[0]

    def body(a_ref, g_ref, b_ref, o_ref):
        xhat, _ = _ln_stats(a_ref[...])
        ln = xhat * g_ref[...] + b_ref[...]
        o_ref[...] = (ln * _sig(ln)).astype(BF16)

    return pl.pallas_call(
        body, name=name, grid=(T // ROW_T,), in_specs=[_rows(T), _vec(), _vec()], out_specs=_rows(T),
        out_shape=jax.ShapeDtypeStruct((T, D), BF16), compiler_params=_cparams(("parallel",)))(ac, g, b)


def _ln_silu_bwd(ac, dya, g, b, name):
    T = ac.shape[0]

    def body(a_ref, d_ref, g_ref, b_ref, da_ref, dg_ref, db_ref):
        i = pl.program_id(0)

        @pl.when(i == 0)
        def _():
            dg_ref[...] = jnp.zeros_like(dg_ref)
            db_ref[...] = jnp.zeros_like(db_ref)

        xhat, rstd = _ln_stats(a_ref[...])
        ln = xhat * g_ref[...] + b_ref[...]
        s = _sig(ln)
        dln = d_ref[...] * s * (1.0 + ln * (1.0 - s))
        dg_ref[...] += jnp.sum(dln * xhat, axis=0, keepdims=True)
        db_ref[...] += jnp.sum(dln, axis=0, keepdims=True)
        dxh = dln * g_ref[...]
        da_ref[...] = rstd * (dxh - jnp.mean(dxh, axis=-1, keepdims=True)
                              - xhat * jnp.mean(dxh * xhat, axis=-1, keepdims=True))

    return pl.pallas_call(
        body, name=name, grid=(T // ROW_T,), in_specs=[_rows(T), _rows(T), _vec(), _vec()],
        out_specs=(_rows(T), _vec(), _vec()),
        out_shape=(jax.ShapeDtypeStruct((T, D), F32), jax.ShapeDtypeStruct((1, D), F32), jax.ShapeDtypeStruct((1, D), F32)),
        compiler_params=_cparams(("arbitrary",)))(ac, dya, g, b)


def _gate_rms_fwd(y, z, g, name):
    T = y.shape[0]

    def body(y_ref, z_ref, g_ref, o_ref):
        zv = z_ref[...]
        yg = y_ref[...] * (zv * _sig(zv))
        r = lax.rsqrt(jnp.mean(yg * yg, axis=-1, keepdims=True) + LN_EPS)
        o_ref[...] = (yg * r * g_ref[...]).astype(BF16)

    return pl.pallas_call(
        body, name=name, grid=(T // ROW_T,), in_specs=[_rows(T), _rows(T), _vec()], out_specs=_rows(T),
        out_shape=jax.ShapeDtypeStruct((T, D), BF16), compiler_params=_cparams(("parallel",)))(y, z, g)


def _gate_rms_bwd(y, z, dout, g, name):
    T = y.shape[0]

    def body(y_ref, z_ref, d_ref, g_ref, dy_ref, dz_ref, dg_ref):
        i = pl.program_id(0)

        @pl.when(i == 0)
        def _():
            dg_ref[...] = jnp.zeros_like(dg_ref)

        yv, zv, dv = y_ref[...], z_ref[...], d_ref[...]
        s = _sig(zv)
        sz = zv * s
        yg = yv * sz
        r = lax.rsqrt(jnp.mean(yg * yg, axis=-1, keepdims=True) + LN_EPS)
        dg_ref[...] += jnp.sum(dv * yg * r, axis=0, keepdims=True)
        dn = dv * g_ref[...]
        dyg = r * dn - yg * (r * r * r) * jnp.mean(dn * yg, axis=-1, keepdims=True)
        dy_ref[...] = dyg * sz
        dz_ref[...] = dyg * yv * s * (1.0 + zv * (1.0 - s))

    return pl.pallas_call(
        body, name=name, grid=(T // ROW_T,), in_specs=[_rows(T), _rows(T), _rows(T), _vec()],
        out_specs=(_rows(T), _rows(T), _vec()),
        out_shape=(jax.ShapeDtypeStruct((T, D), F32), jax.ShapeDtypeStruct((T, D), F32), jax.ShapeDtypeStruct((1, D), F32)),
        compiler_params=_cparams(("arbitrary",)))(y, z, dout, g)


def _loss_head(y, target, name):
    T = y.shape[0]

    def body(y_ref, t_ref, s_ref, d_ref):
        i = pl.program_id(0)

        @pl.when(i == 0)
        def _():
            s_ref[...] = jnp.zeros_like(s_ref)

        err = y_ref[...] - t_ref[...]
        s_ref[...] += jnp.sum(jnp.sum(err * err, axis=1, keepdims=True), axis=0, keepdims=True)
        d_ref[...] = err * (1.0 / D)

    return pl.pallas_call(
        body, name=name, grid=(T // ROW_T,), in_specs=[_rows(T), _rows(T)],
        out_specs=(pl.BlockSpec((SUBLANES, LANES), lambda i: (0, 0)), _rows(T)),
        out_shape=(jax.ShapeDtypeStruct((SUBLANES, LANES), F32), jax.ShapeDtypeStruct((T, D), F32)),
        compiler_params=_cparams(("arbitrary",)))(y, target)


def _taps_fwd(pad_ref, w_ref, K, base):
    off = CONV_PAD - (K - 1)
    acc = w_ref[0:1, :] * pad_ref[pl.ds(base + off, CONV_R), :]
    for k in range(1, K):
        acc = acc + w_ref[k:k + 1, :] * pad_ref[pl.ds(base + off + k, CONV_R), :]
    return acc


def _taps_bwd(padd_ref, w_ref, K, base):
    acc = w_ref[0:1, :] * padd_ref[pl.ds(base + (K - 1), CONV_R), :]
    for k in range(1, K):
        acc = acc + w_ref[k:k + 1, :] * padd_ref[pl.ds(base + (K - 1) - k, CONV_R), :]
    return acc


def _fold8(v):
    return v.reshape(CONV_R // SUBLANES, SUBLANES, v.shape[-1]).sum(0)


def _wgrad_acc(dw_ref, pad_ref, d, K, base):
    off = CONV_PAD - (K - 1)
    for k in range(K):
        dw_ref[k * SUBLANES:(k + 1) * SUBLANES, :] += _fold8(d * pad_ref[pl.ds(base + off + k, CONV_R), :])


def _loop_rows(T, fn):
    def step(r, carry):
        fn(pl.multiple_of(r * CONV_R, CONV_R))
        return carry
    lax.fori_loop(0, T // CONV_R, step, 0)


def _col(T, off_blocks=0, rows=None):
    return pl.BlockSpec((T if rows is None else rows, LANES), lambda j: (0, j + off_blocks))


def _conv_call(body, name, T, n_tiles, in_specs, out_specs, out_shape, n_pad, n_padd=0):
    scratch = [pltpu.VMEM((T + CONV_PAD, LANES), F32)] * (n_pad + n_padd)
    return pl.pallas_call(body, name=name, grid=(n_tiles,), in_specs=in_specs, out_specs=out_specs, out_shape=out_shape,
                          scratch_shapes=scratch, compiler_params=_cparams(("parallel",)))


def _zero_head(ref):
    ref[0:CONV_PAD, :] = jnp.zeros((CONV_PAD, LANES), F32)


def _zero_tail(ref, T):
    ref[T:T + CONV_PAD, :] = jnp.zeros((CONV_PAD, LANES), F32)


def _sds(shape, dtype=F32):
    return jax.ShapeDtypeStruct(shape, dtype)


def _conv_a_fwd(ua, w, b, name):
    T = ua.shape[0]
    K, nt = CONV_A, D // LANES

    def body(al_ref, ag_ref, w_ref, b_ref, o_ref, pad_ref):
        _zero_head(pad_ref)

        def pre(base):
            pad_ref[pl.ds(base + CONV_PAD, CONV_R), :] = al_ref[pl.ds(base, CONV_R), :] * _sig(ag_ref[pl.ds(base, CONV_R), :])
        _loop_rows(T, pre)

        def main(base):
            o_ref[pl.ds(base, CONV_R), :] = _taps_fwd(pad_ref, w_ref, K, base) + b_ref[...]
        _loop_rows(T, main)

    return _conv_call(body, name, T, nt, [_col(T), _col(T, nt), _col(T, rows=K), _col(T, rows=1)], _col(T),
                      _sds((T, D)), 1)(ua, ua, w, b)


def _conv_a_bwd(ua, w, dac, name):
    T = ua.shape[0]
    K, nt = CONV_A, D // LANES

    def body(al_ref, ag_ref, w_ref, d_ref, dal_ref, dag_ref, dw_ref, db_ref, pad_ref, padd_ref):
        _zero_head(pad_ref)
        _zero_tail(padd_ref, T)
        dw_ref[...] = jnp.zeros_like(dw_ref)
        db_ref[...] = jnp.zeros_like(db_ref)

        def pre(base):
            rows = pl.ds(base, CONV_R)
            pad_ref[pl.ds(base + CONV_PAD, CONV_R), :] = al_ref[rows, :] * _sig(ag_ref[rows, :])
            padd_ref[rows, :] = d_ref[rows, :]
        _loop_rows(T, pre)

        def main(base):
            rows = pl.ds(base, CONV_R)
            d = d_ref[rows, :]
            _wgrad_acc(dw_ref, pad_ref, d, K, base)
            db_ref[...] += _fold8(d)
            da = _taps_bwd(padd_ref, w_ref, K, base)
            al, s = al_ref[rows, :], _sig(ag_ref[rows, :])
            dal_ref[rows, :] = da * s
            dag_ref[rows, :] = da * al * s * (1.0 - s)
        _loop_rows(T, main)

    return _conv_call(body, name, T, nt, [_col(T), _col(T, nt), _col(T, rows=K), _col(T)],
                      (_col(T), _col(T), _col(T, rows=K * SUBLANES), _col(T, rows=SUBLANES)),
                      (_sds((T, D)), _sds((T, D)), _sds((K * SUBLANES, D)), _sds((SUBLANES, D))), 1, 1)(ua, ua, w, dac)


def _conv_b_fwd(xu, w, b, name):
    T, C = xu.shape
    K, nt = CONV_B, C // LANES

    def body(x_ref, w_ref, b_ref, o_ref, pad_ref):
        _zero_head(pad_ref)
        pad_ref[CONV_PAD:CONV_PAD + T, :] = x_ref[...]

        def main(base):
            hc = _taps_fwd(pad_ref, w_ref, K, base) + b_ref[...]
            o_ref[pl.ds(base, CONV_R), :] = hc * _sig(hc)
        _loop_rows(T, main)

    return _conv_call(body, name, T, nt, [_col(T), _col(T, rows=K), _col(T, rows=1)], _col(T), _sds((T, C)), 1)(xu, w, b)


def _conv_b_bwd(xu, w, b, dxs, dbs, dcs, name):
    T, C = xu.shape
    K, nt = CONV_B, C // LANES
    nx, nb = dxs.shape[1] // LANES, dbs.shape[1] // LANES

    def body(x_ref, w_ref, b_ref, d1_ref, d2_ref, d3_ref, dx_ref, dw_ref, db_ref, pad_ref, padd_ref):
        j = pl.program_id(0)
        _zero_head(pad_ref)
        _zero_tail(padd_ref, T)
        dw_ref[...] = jnp.zeros_like(dw_ref)
        db_ref[...] = jnp.zeros_like(db_ref)
        pad_ref[CONV_PAD:CONV_PAD + T, :] = x_ref[...]

        def pre(base):
            rows = pl.ds(base, CONV_R)
            hc = _taps_fwd(pad_ref, w_ref, K, base) + b_ref[...]
            s = _sig(hc)
            d = jnp.where(j < nx, d1_ref[rows, :], jnp.where(j < nx + nb, d2_ref[rows, :], d3_ref[rows, :]))
            padd_ref[rows, :] = d * s * (1.0 + hc * (1.0 - s))
        _loop_rows(T, pre)

        def main(base):
            d = padd_ref[pl.ds(base, CONV_R), :]
            _wgrad_acc(dw_ref, pad_ref, d, K, base)
            db_ref[...] += _fold8(d)
            dx_ref[pl.ds(base, CONV_R), :] = _taps_bwd(padd_ref, w_ref, K, base)
        _loop_rows(T, main)

    def piece(lo, n):
        return pl.BlockSpec((T, LANES), lambda j: (0, jnp.clip(j - lo, 0, n - 1)))

    return _conv_call(body, name, T, nt,
                      [_col(T), _col(T, rows=K), _col(T, rows=1), piece(0, nx), piece(nx, nb), piece(nx + nb, nt - nx - nb)],
                      (_col(T), _col(T, rows=K * SUBLANES), _col(T, rows=SUBLANES)),
                      (_sds((T, C)), _sds((K * SUBLANES, C)), _sds((SUBLANES, C))), 1, 1)(xu, w, b, dxs, dbs, dcs)


def _conv_c_fwd(uo, w, name):
    T = uo.shape[0]
    K, nt = CONV_C, D // LANES

    def body(bg_ref, cg_ref, v_ref, w_ref, o_ref, pad_ref):
        _zero_head(pad_ref)
        pad_ref[CONV_PAD:CONV_PAD + T, :] = cg_ref[...] * v_ref[...]

        def main(base):
            rows = pl.ds(base, CONV_R)
            o_ref[rows, :] = (bg_ref[rows, :] * _taps_fwd(pad_ref, w_ref, K, base)).astype(BF16)
        _loop_rows(T, main)

    return _conv_call(body, name, T, nt, [_col(T), _col(T, nt), _col(T, 2 * nt), _col(T, rows=K)], _col(T),
                      _sds((T, D), BF16), 1)(uo, uo, uo, w)


def _conv_c_bwd(uo, w, dsc, name):
    T = uo.shape[0]
    K, nt = CONV_C, D // LANES

    def body(bg_ref, cg_ref, v_ref, w_ref, d_ref, dbg_ref, dcg_ref, dv_ref, dw_ref, pad_ref, padd_ref):
        _zero_head(pad_ref)
        _zero_tail(padd_ref, T)
        dw_ref[...] = jnp.zeros_like(dw_ref)
        pad_ref[CONV_PAD:CONV_PAD + T, :] = cg_ref[...] * v_ref[...]

        def pre(base):
            rows = pl.ds(base, CONV_R)
            d = d_ref[rows, :]
            dbg_ref[rows, :] = (d * _taps_fwd(pad_ref, w_ref, K, base)).astype(BF16)
            padd_ref[rows, :] = d * bg_ref[rows, :]
        _loop_rows(T, pre)

        def main(base):
            rows = pl.ds(base, CONV_R)
            _wgrad_acc(dw_ref, pad_ref, padd_ref[rows, :], K, base)
            dq = _taps_bwd(padd_ref, w_ref, K, base)
            dcg_ref[rows, :] = (dq * v_ref[rows, :]).astype(BF16)
            dv_ref[rows, :] = (dq * cg_ref[rows, :]).astype(BF16)
        _loop_rows(T, main)

    return _conv_call(body, name, T, nt, [_col(T), _col(T, nt), _col(T, 2 * nt), _col(T, rows=K), _col(T)],
                      (_col(T), _col(T), _col(T), _col(T, rows=K * SUBLANES)),
                      (_sds((T, D), BF16), _sds((T, D), BF16), _sds((T, D), BF16), _sds((K * SUBLANES, D))), 1, 1)(uo, uo, uo, w, dsc)


def _conv_f_fwd(up, w, b, name):
    T = up.shape[0]
    K, nt = CONV_F, D_FF // LANES

    def body(u1_ref, u2_ref, w1_ref, w2_ref, b1_ref, b2_ref, o_ref, pad1_ref, pad2_ref):
        _zero_head(pad1_ref)
        _zero_head(pad2_ref)
        pad1_ref[CONV_PAD:CONV_PAD + T, :] = u1_ref[...]
        pad2_ref[CONV_PAD:CONV_PAD + T, :] = u2_ref[...]

        def main(base):
            h1 = _taps_fwd(pad1_ref, w1_ref, K, base) + b1_ref[...]
            h2 = _taps_fwd(pad2_ref, w2_ref, K, base) + b2_ref[...]
            o_ref[pl.ds(base, CONV_R), :] = (h1 * _sig(h1) * h2).astype(BF16)
        _loop_rows(T, main)

    return _conv_call(body, name, T, nt,
                      [_col(T), _col(T, nt), _col(T, rows=K), _col(T, nt, rows=K), _col(T, rows=1), _col(T, nt, rows=1)],
                      _col(T), _sds((T, D_FF), BF16), 2)(up, up, w, w, b, b)


def _conv_f_bwd(up, w, b, dact, name):
    T = up.shape[0]
    K, nt = CONV_F, D_FF // LANES

    def body(u1_ref, u2_ref, w1_ref, w2_ref, b1_ref, b2_ref, d_ref, du1_ref, du2_ref, dw1_ref, dw2_ref, db1_ref, db2_ref,
             pad1_ref, pad2_ref, padd1_ref, padd2_ref):
        _zero_head(pad1_ref)
        _zero_head(pad2_ref)
        _zero_tail(padd1_ref, T)
        _zero_tail(padd2_ref, T)
        for r in (dw1_ref, dw2_ref, db1_ref, db2_ref):
            r[...] = jnp.zeros_like(r)
        pad1_ref[CONV_PAD:CONV_PAD + T, :] = u1_ref[...]
        pad2_ref[CONV_PAD:CONV_PAD + T, :] = u2_ref[...]

        def pre(base):
            rows = pl.ds(base, CONV_R)
            h1 = _taps_fwd(pad1_ref, w1_ref, K, base) + b1_ref[...]
            h2 = _taps_fwd(pad2_ref, w2_ref, K, base) + b2_ref[...]
            s = _sig(h1)
            d = d_ref[rows, :]
            padd1_ref[rows, :] = d * h2 * s * (1.0 + h1 * (1.0 - s))
            padd2_ref[rows, :] = d * h1 * s
        _loop_rows(T, pre)

        def main(base):
            rows = pl.ds(base, CONV_R)
            d1, d2 = padd1_ref[rows, :], padd2_ref[rows, :]
            _wgrad_acc(dw1_ref, pad1_ref, d1, K, base)
            _wgrad_acc(dw2_ref, pad2_ref, d2, K, base)
            db1_ref[...] += _fold8(d1)
            db2_ref[...] += _fold8(d2)
            du1_ref[rows, :] = _taps_bwd(padd1_ref, w1_ref, K, base).astype(BF16)
            du2_ref[rows, :] = _taps_bwd(padd2_ref, w2_ref, K, base).astype(BF16)
        _loop_rows(T, main)

    wrow, brow = _col(T, rows=K * SUBLANES), _col(T, rows=SUBLANES)
    return _conv_call(body, name, T, nt,
                      [_col(T), _col(T, nt), _col(T, rows=K), _col(T, nt, rows=K), _col(T, rows=1), _col(T, nt, rows=1), _col(T)],
                      (_col(T), _col(T), wrow, wrow, brow, brow),
                      (_sds((T, D_FF), BF16), _sds((T, D_FF), BF16), _sds((K * SUBLANES, D_FF)), _sds((K * SUBLANES, D_FF)),
                       _sds((SUBLANES, D_FF)), _sds((SUBLANES, D_FF))), 2, 2)(up, up, w, w, b, b, dact)


def _dot(a, b, dims="nn"):
    return lax.dot_general(a.astype(BF16), b.astype(BF16), _DIMS[dims], preferred_element_type=F32)


def _ssd_small(xcr_ref, xrr_ref, bc_ref, br_ref, ac_ref, ar_ref):
    Q = SSD_Q
    li = lax.broadcasted_iota(jnp.int32, (Q, Q), 0)
    si = lax.broadcasted_iota(jnp.int32, (Q, Q), 1)
    tril = li >= si
    dtc = jax.nn.softplus(xcr_ref[...] + bc_ref[...])
    dtr = jax.nn.softplus(xrr_ref[...] + br_ref[...])
    cumc = jnp.dot(tril.astype(F32), dtc * ac_ref[...], precision=HI, preferred_element_type=F32)
    cumr = jnp.dot(dtr * ar_ref[...], (li <= si).astype(F32), precision=HI, preferred_element_type=F32)
    return tril, dtc, dtr, cumc, cumr


def _ssd_specs(nc, rev):
    Q = SSD_Q
    cc = (lambda c: nc - 1 - c) if rev else (lambda c: c)
    x_spec = pl.BlockSpec((Q, 2 * LANES), lambda g, c: (cc(c), g))
    b_spec = pl.BlockSpec((Q, LANES), lambda g, c: (cc(c), 8 + g))
    c_spec = pl.BlockSpec((Q, LANES), lambda g, c: (cc(c), 12 + g))
    colm = pl.BlockSpec((None, Q, LANES), lambda g, c: (g, cc(c), 0))
    rowm = pl.BlockSpec((None, SUBLANES, Q), lambda g, c: (g, 0, cc(c)))
    colv = pl.BlockSpec((None, 1, LANES), lambda g, c: (g, 0, 0))
    rowv = pl.BlockSpec((None, SUBLANES, 1), lambda g, c: (g, 0, 0))
    st_spec = pl.BlockSpec((None, None, 2 * LANES, N_STATE), lambda g, c: (cc(c), g, 0, 0))
    return x_spec, b_spec, c_spec, colm, rowm, colv, rowv, st_spec


def _ssd_fwd(xc, raw_col, raw_row, bias_col, bias_row, a_col, a_row, dskip, name):
    T = xc.shape[0]
    Q = SSD_Q
    nc = T // Q
    x_spec, b_spec, c_spec, colm, rowm, colv, rowv, st_spec = _ssd_specs(nc, False)

    def body(dk_ref, x_ref, b_ref, c_ref, xcr_ref, xrr_ref, bc_ref, br_ref, ac_ref, ar_ref, y_ref, st_ref, h_ref):
        g = pl.program_id(0)

        @pl.when(pl.program_id(1) == 0)
        def _():
            h_ref[...] = jnp.zeros_like(h_ref)

        tril, dtc, dtr, cumc, cumr = _ssd_small(xcr_ref, xrr_ref, bc_ref, br_ref, ac_ref, ar_ref)
        Bm, Cm = b_ref[...], c_ref[...]
        S = _dot(Cm, Bm, "nt")
        lo = lax.broadcasted_iota(jnp.int32, (Q, LANES), 1) < HEAD_P
        rlo = lax.broadcasted_iota(jnp.int32, (LANES, N_STATE), 0) < HEAD_P
        st_ref[...] = h_ref[...]
        clast = cumc[Q - 1:Q, :]
        for pr in range(2):
            cols = slice(pr * LANES, (pr + 1) * LANES)
            xp = x_ref[:, cols]
            yd = jnp.zeros((Q, LANES), F32)
            for q in range(2):
                hh = 2 * pr + q
                seg = cumc[:, hh:hh + 1] - cumr[hh:hh + 1, :]
                lm = jnp.where(tril, jnp.exp(jnp.where(tril, seg, 0.0)), 0.0)
                w = S * lm * dtr[hh:hh + 1, :]
                xm = jnp.where(lo if q == 0 else jnp.logical_not(lo), xp, 0.0)
                yd = yd + _dot(w, xm)
            h0, h1 = 2 * pr, 2 * pr + 1
            c0, c1 = cumc[:, h0:h0 + 1], cumc[:, h1:h1 + 1]
            e_pair = jnp.where(lo, jnp.exp(c0), jnp.exp(c1))
            hp = h_ref[cols, :]
            ch = _dot(Cm, hp, "nt")
            dsk = jnp.where(lo, dk_ref[4 * g + h0], dk_ref[4 * g + h1])
            y_ref[:, cols] = yd + e_pair * ch + dsk * xp
            cl0, cl1 = clast[:, h0:h0 + 1], clast[:, h1:h1 + 1]
            sdec = jnp.where(lo, jnp.exp(cl0 - c0) * dtc[:, h0:h0 + 1], jnp.exp(cl1 - c1) * dtc[:, h1:h1 + 1])
            decrow = jnp.where(rlo, jnp.exp(cl0), jnp.exp(cl1))
            h_ref[cols, :] = hp * decrow + _dot(xp * sdec, Bm, "tn")

    smem = pl.BlockSpec(memory_space=pltpu.SMEM)
    return pl.pallas_call(
        body, name=name, grid=(N_GROUPS, nc),
        in_specs=[smem, x_spec, b_spec, c_spec, colm, rowm, colv, rowv, colv, rowv],
        out_specs=(x_spec, st_spec),
        out_shape=(_sds((T, D)), _sds((nc, N_GROUPS, 2 * LANES, N_STATE))),
        scratch_shapes=[pltpu.VMEM((2 * LANES, N_STATE), F32)],
        compiler_params=_cparams(("parallel", "arbitrary")))(dskip, xc, xc, xc, raw_col, raw_row, bias_col, bias_row, a_col, a_row)


def _ssd_bwd(xc, raw_col, raw_row, bias_col, bias_row, a_col, a_row, dskip, states, dy, name):
    T = xc.shape[0]
    Q = SSD_Q
    nc = T // Q
    x_spec, b_spec, c_spec, colm, rowm, colv, rowv, st_spec = _ssd_specs(nc, True)
    bo_spec = pl.BlockSpec((Q, LANES), lambda g, c: (nc - 1 - c, g))
    dd_spec = pl.BlockSpec((None, None, SUBLANES, 2 * LANES), lambda g, c: (nc - 1 - c, g, 0, 0))

    def body(dk_ref, x_ref, b_ref, c_ref, xcr_ref, xrr_ref, bc_ref, br_ref, ac_ref, ar_ref, st_ref, dy_ref,
             dx_ref, db_ref, dc_ref, sq_ref, cms_ref, ddac_ref, ddar_ref, dd_ref, dh_ref):
        g = pl.program_id(0)

        @pl.when(pl.program_id(1) == 0)
        def _():
            dh_ref[...] = jnp.zeros_like(dh_ref)

        tril, dtc, dtr, cumc, cumr = _ssd_small(xcr_ref, xrr_ref, bc_ref, br_ref, ac_ref, ar_ref)
        Bm, Cm = b_ref[...], c_ref[...]
        S = _dot(Cm, Bm, "nt")
        lane = lax.broadcasted_iota(jnp.int32, (Q, LANES), 1)
        sub = lax.broadcasted_iota(jnp.int32, (SUBLANES, Q), 0)
        rowi = lax.broadcasted_iota(jnp.int32, (Q, LANES), 0)
        lo = lane < HEAD_P
        rlo = lax.broadcasted_iota(jnp.int32, (LANES, N_STATE), 0) < HEAD_P
        clast = cumc[Q - 1:Q, :]
        ds_g = jnp.zeros((Q, Q), F32)
        dcm = jnp.zeros((Q, N_STATE), F32)
        dbm = jnp.zeros((Q, N_STATE), F32)
        dcum_col = jnp.zeros((Q, LANES), F32)
        dcum_row = jnp.zeros((SUBLANES, Q), F32)
        sq_col = jnp.zeros((Q, LANES), F32)
        cms_row = jnp.zeros((SUBLANES, Q), F32)
        for pr in range(2):
            cols = slice(pr * LANES, (pr + 1) * LANES)
            xp, dyp = x_ref[:, cols], dy_ref[:, cols]
            hin, dhp = st_ref[cols, :], dh_ref[cols, :]
            h0, h1 = 2 * pr, 2 * pr + 1
            c0, c1 = cumc[:, h0:h0 + 1], cumc[:, h1:h1 + 1]
            cl0, cl1 = clast[:, h0:h0 + 1], clast[:, h1:h1 + 1]
            e_pair = jnp.where(lo, jnp.exp(c0), jnp.exp(c1))
            edec = jnp.where(lo, jnp.exp(cl0 - c0), jnp.exp(cl1 - c1))
            dt_pair = jnp.where(lo, dtc[:, h0:h0 + 1], dtc[:, h1:h1 + 1])
            sdec = edec * dt_pair
            ch = _dot(Cm, hin, "nt")
            xb = _dot(Bm, dhp, "nt")
            dye = dyp * e_pair
            t1 = dye * ch
            t2 = xp * xb * edec
            hh_prod = dhp * hin
            dsk = jnp.where(lo, dk_ref[4 * g + h0], dk_ref[4 * g + h1])
            dxp = sdec * xb + dsk * dyp
            for q in range(2):
                hh = 2 * pr + q
                mine = lo if q == 0 else jnp.logical_not(lo)
                seg = cumc[:, hh:hh + 1] - cumr[hh:hh + 1, :]
                lm = jnp.where(tril, jnp.exp(jnp.where(tril, seg, 0.0)), 0.0)
                dtrow = dtr[hh:hh + 1, :]
                w = S * lm * dtrow
                dym = jnp.where(mine, dyp, 0.0)
                gl = _dot(dym, xp, "nt") * lm
                ds_g = ds_g + gl * dtrow
                ms = gl * S
                m = ms * dtrow
                dxp = dxp + _dot(w, dym, "tn")
                cms_row = jnp.where(sub == hh, jnp.sum(ms, axis=0, keepdims=True), cms_row)
                dcum_row = jnp.where(sub == hh, -jnp.sum(m, axis=0, keepdims=True), dcum_row)
                t1h = jnp.sum(jnp.where(mine, t1, 0.0), axis=1, keepdims=True)
                sqh = jnp.sum(jnp.where(mine, t2, 0.0), axis=1, keepdims=True)
                sth = sqh * dtc[:, hh:hh + 1]
                rmine = rlo if q == 0 else jnp.logical_not(rlo)
                hsum = jnp.sum(jnp.sum(jnp.where(rmine, hh_prod, 0.0), axis=1, keepdims=True), axis=0, keepdims=True)
                last = jnp.sum(sth, axis=0, keepdims=True) + jnp.exp(clast[:, hh:hh + 1]) * hsum
                dcol = jnp.sum(m, axis=1, keepdims=True) + t1h - sth
                dcum_col = jnp.where(lane == hh, dcol + jnp.where(rowi == Q - 1, last, 0.0), dcum_col)
                sq_col = jnp.where(lane == hh, sqh, sq_col)
            dcm = dcm + _dot(dye, hin)
            dbm = dbm + _dot(xp * sdec, dhp)
            decrow = jnp.where(rlo, jnp.exp(cl0), jnp.exp(cl1))
            dh_ref[cols, :] = dhp * decrow + _dot(dye, Cm, "tn")
            dx_ref[:, cols] = dxp
            dd_ref[:, cols] = jnp.broadcast_to(jnp.sum(dyp * xp, axis=0, keepdims=True), (SUBLANES, LANES))
        dc_ref[...] = dcm + _dot(ds_g, Bm)
        db_ref[...] = dbm + _dot(ds_g, Cm, "tn")
        li = lax.broadcasted_iota(jnp.int32, (Q, Q), 0)
        si = lax.broadcasted_iota(jnp.int32, (Q, Q), 1)
        ddac_ref[...] = jnp.dot((li <= si).astype(F32), dcum_col, precision=HI, preferred_element_type=F32)
        ddar_ref[...] = jnp.dot(dcum_row, tril.astype(F32), precision=HI, preferred_element_type=F32)
        sq_ref[...] = sq_col
        cms_ref[...] = cms_row

    smem = pl.BlockSpec(memory_space=pltpu.SMEM)
    return pl.pallas_call(
        body, name=name, grid=(N_GROUPS, nc),
        in_specs=[smem, x_spec, b_spec, c_spec, colm, rowm, colv, rowv, colv, rowv, st_spec, x_spec],
        out_specs=(x_spec, bo_spec, bo_spec, colm, rowm, colm, rowm, dd_spec),
        out_shape=(_sds((T, D)), _sds((T, D // 2)), _sds((T, D // 2)), _sds((N_GROUPS, T, LANES)), _sds((N_GROUPS, SUBLANES, T)),
                   _sds((N_GROUPS, T, LANES)), _sds((N_GROUPS, SUBLANES, T)), _sds((nc, N_GROUPS, SUBLANES, 2 * LANES))),
        scratch_shapes=[pltpu.VMEM((2 * LANES, N_STATE), F32)],
        compiler_params=_cparams(("parallel", "arbitrary")))(dskip, xc, xc, xc, raw_col, raw_row, bias_col, bias_row, a_col, a_row,
                                                            states, dy)


def _adam_math(wv, gv, mv, vv):
    c1 = 1.0 - ADAM_B1 ** ADAM_STEP
    c2 = 1.0 - ADAM_B2 ** ADAM_STEP
    mn = ADAM_B1 * mv + (1.0 - ADAM_B1) * gv
    vn = ADAM_B2 * vv + (1.0 - ADAM_B2) * (gv * gv)
    return -ADAM_LR * ((mn / c1) / (jnp.sqrt(vn / c2) + ADAM_EPS) + ADAM_WD * wv), mn, vn


def _adamw_layers(w, g, m, v, l0, Lg, bufs, name):
    L, As, Bs = w.shape
    tr = _tile(As, [], (256, 352, 128))
    has_bufs = bufs is not None

    def body(*refs):
        w_ref, g_ref, m_ref, v_ref = refs[:4]
        d_ref, mo_ref, vo_ref = refs[4 + 3 * has_bufs:]
        d_ref[...], mo_ref[...], vo_ref[...] = _adam_math(w_ref[...], g_ref[...], m_ref[...], v_ref[...])

    spec = pl.BlockSpec((None, tr, Bs), lambda l, i: (l + l0, i, 0))
    args = (w, g, m, v) + (tuple(bufs) if has_bufs else ())
    return pl.pallas_call(
        body, name=name, grid=(Lg, As // tr), in_specs=[spec] * 4 + [_ANY] * (3 * has_bufs), out_specs=(spec,) * 3,
        out_shape=(_sds((L, As, Bs)),) * 3, input_output_aliases={4: 0, 5: 1, 6: 2} if has_bufs else {},
        compiler_params=_cparams(("parallel", "parallel")))(*args)


def _adamw(w, g, m, v, name):
    shape = w.shape
    cols = shape[-1]
    w2, g2, m2, v2 = (t.reshape(-1, cols) for t in (w, g, m, v))
    rows = w2.shape[0]
    tr = 256 if (rows % 256 == 0 and rows > 256) else rows
    c1 = 1.0 - ADAM_B1 ** ADAM_STEP
    c2 = 1.0 - ADAM_B2 ** ADAM_STEP

    def body(w_ref, g_ref, m_ref, v_ref, d_ref, mo_ref, vo_ref):
        gv = g_ref[...]
        mn = ADAM_B1 * m_ref[...] + (1.0 - ADAM_B1) * gv
        vn = ADAM_B2 * v_ref[...] + (1.0 - ADAM_B2) * (gv * gv)
        d_ref[...] = -ADAM_LR * ((mn / c1) / (jnp.sqrt(vn / c2) + ADAM_EPS) + ADAM_WD * w_ref[...])
        mo_ref[...] = mn
        vo_ref[...] = vn

    spec = pl.BlockSpec((tr, cols), lambda i: (i, 0))
    out = pl.pallas_call(body, name=name, grid=(rows // tr,), in_specs=[spec] * 4, out_specs=(spec,) * 3,
                         out_shape=(_sds((rows, cols)),) * 3, compiler_params=_cparams(("parallel",)))(w2, g2, m2, v2)
    return tuple(o.reshape(shape) for o in out)


def _place():
    x, y, c = lax.axis_index("x"), lax.axis_index("y"), lax.axis_index("c")
    chips = [(1 - x, y), (x, 1 - y), (1 - x, 1 - y)]
    return x, y, c, chips


_ANY = pl.BlockSpec(memory_space=pl.ANY)


TENSORS = (("e_w_in", "row", 2, 4096, 1284, 1024), ("e_w_out", "row", 2, 2048, 1024, 512), ("o_w_in", "col", 2, 1024, 3072, 768),
           ("o_w_out", "row", 2, 1024, 1024, 256), ("f_w_up", "col", 4, 1024, 5632, 1408), ("f_w_down", "row", 4, 2816, 1024, 704),
           ("ple_w_proj", "col", 4, 256, 1024, 256), ("ple_w_gate", "row", 4, 1024, 1024, 256))
MIX, FFN = "mix", "ffn"
W_GROUPS = (((0, MIX),), ((0, FFN),), ((1, MIX), (1, FFN)), ((2, MIX), (2, FFN), (3, MIX), (3, FFN)))
G_GROUPS = (((3, FFN), (3, MIX), (2, FFN), (2, MIX), (1, FFN), (1, MIX)), ((0, FFN),), ((0, MIX),))


def _tensor_layer(name, layer):
    if name.startswith("e_"):
        return layer // 2 if layer % 2 == 0 else None
    if name.startswith("o_"):
        return layer // 2 if layer % 2 == 1 else None
    return layer


def _part(name):
    return MIX if name.startswith(("e_", "o_")) else FFN


def _group_items(members):
    items = []
    for name, kind, L, A, B, n in TENSORS:
        tls = sorted(t for t in (_tensor_layer(name, l) for l, part in members if part == _part(name)) if t is not None)
        if tls:
            assert tls == list(range(tls[0], tls[0] + len(tls)))
            items.append((name, kind, len(tls), A, B, n, tls[0]))
    return items


def _hwin(ref, it, k, h):
    name, kind, Lg, A, B, n, l0 = it
    if kind == "row":
        return ref.at[:, pl.ds(pl.multiple_of(k * n + h * (n // 2), 16), n // 2), :]
    return ref.at[:, pl.ds(pl.multiple_of(h * (A // 2), 16), A // 2), pl.ds(pl.multiple_of(k * n, LANES), n)]


def _shard_dims(kind, A, B, n):
    return (n, B) if kind == "row" else (A, n)


def _cast_into(w, it, me):
    name, kind, Lg, A, B, n, l0 = it
    As, Bs = _shard_dims(kind, A, B, n)

    def body(me_ref, w_ref, o_ref):
        o_ref[...] = w_ref[...].astype(BF16)

    omap = (lambda l, m: (l, m[0], 0)) if kind == "row" else (lambda l, m: (l, 0, m[0]))
    grid_spec = pltpu.PrefetchScalarGridSpec(
        num_scalar_prefetch=1, grid=(Lg,), in_specs=[pl.BlockSpec((None, As, Bs), lambda l, m: (l + l0, 0, 0))],
        out_specs=pl.BlockSpec((None, As, Bs), omap))
    return pl.pallas_call(body, name=f"cast_{name}_{l0}", grid_spec=grid_spec, out_shape=_sds((Lg, A, B), BF16),
                          compiler_params=_cparams(("parallel",)))(me, w.reshape(-1, As, Bs))


_HBM = pl.BlockSpec(memory_space=pltpu.HBM)
_SEM = pl.BlockSpec(memory_space=pltpu.SEMAPHORE)
_EFFECT = pltpu.SideEffectType.DATAFLOW_SIDE_EFFECTING


def _hbm(a):
    return pltpu.with_memory_space_constraint(a, pltpu.HBM)


def _split_start(thru, n_copies, issue, name, after=None):
    N = len(thru)
    has_after = after is not None

    def body(*refs):
        outs = refs[N + has_after:2 * N + has_after]
        send_sems, recv_sems, token = refs[2 * N + has_after:]
        for cp in issue(outs, send_sems, recv_sems):
            cp.start()
        token[...] = jnp.zeros_like(token)

    out = pl.pallas_call(
        body, name=name, in_specs=[_HBM] * N + ([_ANY] if has_after else []),
        out_specs=(_HBM,) * N + (_SEM, _SEM, pl.BlockSpec(memory_space=pltpu.VMEM)),
        out_shape=tuple(pltpu.HBM(a.shape, a.dtype) for a in thru)
        + (pltpu.SemaphoreType.DMA((n_copies,)), pltpu.SemaphoreType.DMA((n_copies,)), _sds((SUBLANES, LANES))),
        input_output_aliases={t: t for t in range(N)},
        compiler_params=pltpu.CompilerParams(has_side_effects=_EFFECT))(*[_hbm(a) for a in thru], *([after] if has_after else []))
    return list(out[:N]), out[N], out[N + 1], out[N + 2]


def _split_wait(thru, send_sems, recv_sems, after, waits, name):
    N = len(thru)

    def body(*refs):
        ins = refs[:N]
        for cp, side in waits(ins, refs[N], refs[N + 1]):
            if side == "send":
                cp.wait_send()
            else:
                cp.wait_recv()

    out = pl.pallas_call(
        body, name=name, in_specs=[_HBM] * N + [_SEM, _SEM, _ANY], out_specs=(_HBM,) * N,
        out_shape=tuple(pltpu.HBM(a.shape, a.dtype) for a in thru), input_output_aliases={t: t for t in range(N)},
        compiler_params=pltpu.CompilerParams(has_side_effects=_EFFECT))(*thru, send_sems, recv_sems, after)
    return list(out)


def _rcopy(send_sems, recv_sems, k, src, dst, to):
    return pltpu.make_async_remote_copy(src_ref=src, dst_ref=dst, send_sem=send_sems.at[k], recv_sem=recv_sems.at[k],
                                        device_id=to, device_id_type=MESH)


def _gather_copies(items, refs, send_sems, recv_sems, what):
    x, y, c, chips = _place()
    me = 2 * x + y
    out = []
    for t, it in enumerate(items):
        mine = _hwin(refs[t], it, me, c)
        for j, (px, py) in enumerate(chips):
            if what == "start":
                out.append(_rcopy(send_sems, recv_sems, 3 * t + j, mine, mine, (px, py, c)))
            else:
                slot = _hwin(refs[t], it, 2 * px + py, c)
                out.append((_rcopy(send_sems, recv_sems, 3 * t + j, mine, mine, (px, py, c)), "send"))
                out.append((_rcopy(send_sems, recv_sems, 3 * t + j, slot, slot, (px, py, c)), "recv"))
    return out


def _gather_start(fulls, items, name, after=None):
    return _split_start(fulls, 3 * len(items), functools.partial(_gather_copies, items, what="start"), name, after)


def _gather_wait(fulls, send_sems, recv_sems, after, items, name):
    return _split_wait(fulls, send_sems, recv_sems, after, functools.partial(_gather_copies, items, what="wait"), name)


def _gather_fwd(fulls, items, name, ws=None):
    N = len(fulls)
    has_ws = ws is not None

    def body(*refs):
        outs = refs[N + has_ws:2 * N + has_ws]
        rest = refs[2 * N + has_ws:]
        x, y, c, chips = _place()
        me = 2 * x + y
        sib = (x, y, 1 - c)
        if has_ws:
            ws_ref = refs[N]
            WS_ref, send_sems, recv_sems, lsem = rest
            loc = pltpu.make_async_copy(ws_ref, WS_ref.at[me], lsem)
            loc.start()
        else:
            send_sems, recv_sems = rest
        rc = functools.partial(_rcopy, send_sems, recv_sems)
        cps = []
        for t, it in enumerate(items):
            for j, (px, py) in enumerate(chips):
                slot = _hwin(outs[t], it, 2 * px + py, c)
                cps.append(rc(3 * t + j, slot, slot, sib))
        if has_ws:
            cps += [rc(3 * N + j, ws_ref, WS_ref.at[me], (*chip, c)) for j, chip in enumerate(chips)]
        for cp in cps:
            cp.start()
        for t, it in enumerate(items):
            for j, (px, py) in enumerate(chips):
                oslot = _hwin(outs[t], it, 2 * px + py, 1 - c)
                rc(3 * t + j, oslot, oslot, sib).wait_recv()
        if has_ws:
            for j, (px, py) in enumerate(chips):
                sslot = WS_ref.at[2 * px + py]
                rc(3 * N + j, sslot, sslot, sib).wait_recv()
        for cp in cps:
            cp.wait_send()
        if has_ws:
            loc.wait()

    ns = 3 * N + (3 if has_ws else 0)
    out_shape = tuple(_sds(f.shape, f.dtype) for f in fulls)
    scratch = [pltpu.SemaphoreType.DMA((ns,)), pltpu.SemaphoreType.DMA((ns,))]
    args = list(fulls)
    if has_ws:
        out_shape += (_sds((4,) + ws.shape, ws.dtype),)
        scratch.append(pltpu.SemaphoreType.DMA(()))
        args.append(ws)
    out = pl.pallas_call(
        body, name=name, in_specs=[_ANY] * len(args), out_specs=(_ANY,) * len(out_shape), out_shape=out_shape,
        input_output_aliases={t: t for t in range(N)}, scratch_shapes=scratch,
        compiler_params=pltpu.CompilerParams(has_side_effects=True))(*args)
    return (list(out[:N]), out[N]) if has_ws else (list(out), None)


def _half_shape(it):
    name, kind, Lg, A, B, n, l0 = it
    return (Lg, 4, n // 2, B) if kind == "row" else (Lg, A // 2, B)


def _piece_shape(it):
    name, kind, Lg, A, B, n, l0 = it
    return (Lg, n // 2, B) if kind == "row" else (Lg, A // 2, n)


def _swap_grads(gs, items, name):
    N = len(gs)

    def body(*refs):
        g_refs, o_refs = refs[:N], refs[N:2 * N]
        send_sems, recv_sems = refs[2 * N:]
        x, y, c, _ = _place()
        sib = (x, y, 1 - c)
        cps = []
        for t, it in enumerate(items):
            name_, kind, Lg, A, B, n, l0 = it
            if kind == "row":
                for k in range(4):
                    cps.append(_rcopy(send_sems, recv_sems, 4 * t + k, _hwin(g_refs[t], it, k, 1 - c), o_refs[t].at[:, k], sib))
            else:
                src = g_refs[t].at[:, pl.ds(pl.multiple_of((1 - c) * (A // 2), 16), A // 2), :]
                cps.append(_rcopy(send_sems, recv_sems, 4 * t, src, o_refs[t], sib))
        for cp in cps:
            cp.start()
        for cp in cps:
            cp.wait()

    return pl.pallas_call(
        body, name=name, in_specs=[_ANY] * N, out_specs=(_ANY,) * N, out_shape=tuple(_sds(_half_shape(it)) for it in items),
        scratch_shapes=[pltpu.SemaphoreType.DMA((4 * N,)), pltpu.SemaphoreType.DMA((4 * N,))],
        compiler_params=pltpu.CompilerParams(has_side_effects=True))(*gs)


def _add_half(g, ra, it, cvec):
    name, kind, Lg, A, B, n, l0 = it
    if kind == "row":
        blk = (None, n // 2, B)
        grid = (Lg, 4)
        g_spec = pl.BlockSpec(blk, lambda l, k, cr: (l, 2 * k + cr[0], 0))
        h_spec = pl.BlockSpec((None, None, n // 2, B), lambda l, k, cr: (l, k, 0, 0))
    else:
        tr = _tile(A // 2, [], (256, 128))
        nb = (A // 2) // tr
        grid = (Lg, nb)
        g_spec = pl.BlockSpec((None, tr, B), lambda l, i, cr: (l, cr[0] * nb + i, 0))
        h_spec = pl.BlockSpec((None, tr, B), lambda l, i, cr: (l, i, 0))

    def body(c_ref, g_ref, r_ref, o_ref):
        o_ref[...] = (g_ref[...] + r_ref[...]).astype(BF16)

    grid_spec = pltpu.PrefetchScalarGridSpec(num_scalar_prefetch=1, grid=grid, in_specs=[g_spec, h_spec], out_specs=h_spec)
    return pl.pallas_call(body, name=f"addhalf_{name}_{l0}", grid_spec=grid_spec, out_shape=_sds(_half_shape(it), BF16),
                          compiler_params=_cparams(("parallel", "parallel")))(cvec, g, ra)


def _scatter_copies(items, refs, send_sems, recv_sems, what):
    N = len(items)
    x, y, c, chips = _place()
    out = []
    for t, it in enumerate(items):
        name, kind, Lg, A, B, n, l0 = it
        for j, (px, py) in enumerate(chips):
            k = 2 * px + py
            src = refs[t].at[:, k] if kind == "row" else refs[t].at[:, :, pl.ds(pl.multiple_of(k * n, LANES), n)]
            cp = _rcopy(send_sems, recv_sems, 3 * t + j, src, refs[N + t].at[j], (px, py, c))
            if what == "start":
                out.append(cp)
            else:
                out += [(cp, "send"), (cp, "recv")]
    return out


def _scatter_start(ps, items, name):
    lands = [lax.empty((3,) + _piece_shape(it), BF16) for it in items]
    return _split_start(list(ps) + lands, 3 * len(items), functools.partial(_scatter_copies, items, what="start"), name)


def _scatter_wait(thru, send_sems, recv_sems, after, items, name):
    return _split_wait(thru, send_sems, recv_sems, after, functools.partial(_scatter_copies, items, what="wait"), name)


def _sum_own(p, rc, it, mevec, buf):
    name, kind, Lg, A, B, n, l0 = it
    As, Bs = _shard_dims(kind, A, B, n)
    L = [s[2] for s in TENSORS if s[0] == name][0]
    hb = (As // 2, Bs)
    has_buf = buf is not None

    def body(*refs):
        p_ref, r0, r1, r2 = refs[1:5]
        o_ref = refs[5 + has_buf]
        o_ref[...] = ((p_ref[...].astype(F32) + r0[...].astype(F32)) + r1[...].astype(F32)) + r2[...].astype(F32)

    if kind == "row":
        p_spec = pl.BlockSpec((None, None) + hb, lambda l, m: (l, m[0], 0, 0))
    else:
        p_spec = pl.BlockSpec((None,) + hb, lambda l, m: (l, 0, m[0]))
    r_specs = [pl.BlockSpec((None, None) + hb, functools.partial(lambda l, m, j: (j, l, 0, 0), j=j)) for j in range(3)]
    in_specs = [p_spec] + r_specs + ([_ANY] if has_buf else [])
    grid_spec = pltpu.PrefetchScalarGridSpec(num_scalar_prefetch=1, grid=(Lg,), in_specs=in_specs,
                                             out_specs=pl.BlockSpec((None,) + hb, lambda l, m: (l + l0, m[1], 0)))
    args = (mevec, p, rc, rc, rc) + ((buf,) if has_buf else ())
    return pl.pallas_call(body, name=f"sumown_{name}_{l0}", grid_spec=grid_spec, out_shape=_sds((L, As, Bs)),
                          input_output_aliases={5: 0} if has_buf else {}, compiler_params=_cparams(("parallel",)))(*args)


def _join_halves(rs, items, name):
    N = len(rs)

    def body(*refs):
        outs = refs[N:2 * N]
        send_sems, recv_sems = refs[2 * N:]
        x, y, c, _ = _place()
        sib = (x, y, 1 - c)

        def half(t, h):
            name_, kind, Lg, A, B, n, l0 = items[t]
            hr = _shard_dims(kind, A, B, n)[0] // 2
            return outs[t].at[pl.ds(l0, Lg), pl.ds(pl.multiple_of(h * hr, SUBLANES), hr), :]

        cps = [_rcopy(send_sems, recv_sems, t, half(t, c), half(t, c), sib) for t in range(N)]
        for cp in cps:
            cp.start()
        for t in range(N):
            _rcopy(send_sems, recv_sems, t, half(t, 1 - c), half(t, 1 - c), sib).wait_recv()
        for cp in cps:
            cp.wait_send()

    return list(pl.pallas_call(
        body, name=name, in_specs=[_ANY] * N, out_specs=(_ANY,) * N, out_shape=tuple(_sds(r.shape, r.dtype) for r in rs),
        input_output_aliases={t: t for t in range(N)},
        scratch_shapes=[pltpu.SemaphoreType.DMA((N,)), pltpu.SemaphoreType.DMA((N,))],
        compiler_params=pltpu.CompilerParams(has_side_effects=True))(*rs))


def _allgather_small(v):
    m_per, n = v.shape

    def body(x_ref, out_ref, send_sems, recv_sems, local_sem):
        x, y, c, chips = _place()
        me, sibling = (x, y, c), (x, y, 1 - c)

        def rows(px, py, pc):
            return out_ref.at[pl.ds(pl.multiple_of((4 * px + 2 * py + pc) * m_per, SUBLANES), m_per), :]

        def copy(k, block, to, src=None):
            return pltpu.make_async_remote_copy(src_ref=rows(*block) if src is None else src, dst_ref=rows(*block),
                                                send_sem=send_sems.at[k], recv_sem=recv_sems.at[k], device_id=to, device_id_type=MESH)

        mine = pltpu.make_async_copy(x_ref, rows(*me), local_sem)
        mine.start()
        first = [copy(0, me, sibling, src=x_ref)]
        first += [copy(1 + j, me, (*chip, c), src=x_ref) for j, chip in enumerate(chips)]
        for cp in first:
            cp.start()
        passed = [copy(4 + j, (*chip, c), sibling) for j, chip in enumerate(chips)]
        for j, chip in enumerate(chips):
            copy(1 + j, (*chip, c), me).wait_recv()
            passed[j].start()
        copy(0, sibling, me).wait_recv()
        for j, chip in enumerate(chips):
            copy(4 + j, (*chip, 1 - c), me).wait_recv()
        for cp in first + passed:
            cp.wait_send()
        mine.wait()

    vm = pl.BlockSpec(memory_space=pltpu.VMEM)
    return pl.pallas_call(body, name="allgather_small", in_specs=[vm], out_specs=vm, out_shape=_sds((8 * m_per, n)),
                          scratch_shapes=[pltpu.SemaphoreType.DMA((7,)), pltpu.SemaphoreType.DMA((7,)), pltpu.SemaphoreType.DMA(())],
                          compiler_params=pltpu.CompilerParams(has_side_effects=True, vmem_limit_bytes=VMEM_LIMIT))(v)


def _sum8(v, m_per):
    def body(v_ref, o_ref):
        acc = v_ref[0:m_per, :]
        for k in range(1, 8):
            acc = acc + v_ref[k * m_per:(k + 1) * m_per, :]
        o_ref[...] = acc

    return pl.pallas_call(body, name="small_sum_devices", out_shape=_sds((m_per, v.shape[1])),
                          compiler_params=pltpu.CompilerParams(vmem_limit_bytes=VMEM_LIMIT))(v)


SMALL_SHARDED = (("e_conv_a_w", 2), ("e_conv_b_w", 2), ("o_conv_w", 2), ("f_conv_w", 2), ("ln_g", 2), ("ln_b", 2))
SMALL_REPL = ("e_conv_a_b", "e_ln_a_g", "e_ln_a_b", "e_conv_b_b", "e_dt_bias", "e_a_log", "e_d_skip", "e_norm_b_g", "f_conv_b")

WEIGHT_ORDER = ('e_w_in', 'e_conv_a_w', 'e_conv_a_b', 'e_ln_a_g', 'e_ln_a_b', 'e_conv_b_w', 'e_conv_b_b', 'e_dt_bias', 'e_a_log',
                'e_d_skip', 'e_norm_b_g', 'e_w_out', 'o_w_in', 'o_conv_w', 'o_w_out', 'f_w_up', 'f_conv_w', 'f_conv_b', 'f_w_down',
                'ple_w_proj', 'ple_w_gate', 'ln_g', 'ln_b')


def _pack_rows(parts, width, total_rows, dtype):
    flat = jnp.concatenate([p.reshape(-1).astype(dtype) for p in parts])
    flat = jnp.pad(flat, (0, total_rows * width - flat.shape[0]))
    return flat.reshape(total_rows, width)


def _unpack_rows(buf, shapes):
    flat = buf.reshape(-1)
    out, pos = [], 0
    for s in shapes:
        n = math.prod(s)
        out.append(flat[pos:pos + n].reshape(s))
        pos += n
    return out


def _small_rows(shapes):
    n = sum(math.prod(s) for s in shapes)
    return -(-n // (LANES * SUBLANES)) * SUBLANES


E_PAD = 5248
SEG_A, SEG_Z, SEG_X, SEG_DT = (0, 2 * D), (2 * D, D), (3 * D, 2 * D), (5 * D, LANES)
G_SHAPES = {"e_w_in": (2, D, E_PAD), "e_w_out": (2, 2 * D, D), "o_w_in": (2, D, 3 * D), "o_w_out": (2, D, D),
            "f_w_up": (4, D, 2 * D_FF), "f_w_down": (4, D_FF, D), "ple_w_proj": (4, PLE, D), "ple_w_gate": (4, D, D)}


def _padcols(w, width):
    return jnp.pad(w, ((0, 0), (0, width - w.shape[1])))


def _fold_rows(dw, K):
    return dw.reshape(K, SUBLANES, dw.shape[-1]).sum(1)


class GradBuffers(dict):
    def __init__(self):
        super().__init__()
        self.where = {}
        for gi, layers in enumerate(G_GROUPS):
            for name, kind, Lg, A, B, n, l0 in _group_items(layers):
                for k in range(Lg):
                    self.where[(name, l0 + k)] = (gi, k, Lg)
        self.current = {}

    def into(self, name, layer, r0=0, c0=0):
        gi, k, Lg = self.where[(name, layer)]
        self.current[name] = (name, gi)
        return (self.get((name, gi)), (Lg,) + G_SHAPES[name][1:], (k,), r0, c0)

    def __setitem__(self, name, value):
        super().__setitem__(self.current[name], value)


def _local_step(x, p, target, W, comm=None):
    T = x.shape[0]
    xb = x
    saved = []
    xc_f = x
    for i in range(DEPTH):
        j = i // 2
        L = {}
        L["x"], L["xb"] = xc_f, xb
        tok = comm.part_starts(i, MIX, xb) if comm is not None else None
        if i % 2 == 0:
            def w_in(seg, c0=0, cols=None, j=j):
                return V(W["e_w_in"], (j,), c0=seg[0] + c0, cols=seg[1] if cols is None else cols)

            ua = _mm(xb, w_in(SEG_A), "nn", f"l{i}_in_a", after=tok)
            z = _mm(xb, w_in(SEG_Z), "nn", f"l{i}_in_z")
            xu = _mm(xb, w_in(SEG_X), "nn", f"l{i}_in_xbc")
            udt = _mm(xb, w_in(SEG_DT), "nn", f"l{i}_in_dt")
            ac = _conv_a_fwd(ua, W["e_conv_a_w"][j], W["e_conv_a_b"][j][None], f"l{i}_conv_a")
            ya = _ln_silu_fwd(ac, W["e_ln_a_g"][j][None], W["e_ln_a_b"][j][None], f"l{i}_ln_a")
            xc = _conv_b_fwd(xu, W["e_conv_b_w"][j], W["e_conv_b_b"][j][None], f"l{i}_conv_b")
            sm = _ssd_small_inputs(udt[:, :N_HEADS], W["e_dt_bias"][j], W["e_a_log"][j])
            y, states = _ssd_fwd(xc, *sm, W["e_d_skip"][j], f"l{i}_ssd")
            yb = _gate_rms_fwd(y, z, W["e_norm_b_g"][j][None], f"l{i}_gate_rms")
            mix = _mm(ya, V(W["e_w_out"], (j,), rows=D), "nn", f"l{i}_out_a")
            mix = _mm(yb, V(W["e_w_out"], (j,), r0=D), "nn", f"l{i}_out_b", add=mix)
            L.update(ua=ua, z=z, xu=xu, udt=udt, ac=ac, ya=ya, xc=xc, sm=sm, y=y, states=states, yb=yb, w_in=w_in)
        else:
            uo = _mm(xb, V(W["o_w_in"], (j,)), "nn", f"l{i}_in", after=tok)
            sc = _conv_c_fwd(uo, W["o_conv_w"][j], f"l{i}_conv_c")
            mix = _mm(sc, V(W["o_w_out"], (j,)), "nn", f"l{i}_out")
            L.update(uo=uo, sc=sc)
        h1, x1, x1b = _res_ln_fwd(xc_f, [mix], None, W["ln_g"][i, 0][None], W["ln_b"][i, 0][None], f"l{i}_ln1")
        tok = comm.part_starts(i, FFN, x1b) if comm is not None else None
        up = _mm(x1b, V(W["f_w_up"], (i,)), "nn", f"l{i}_ffn_up", after=tok)
        act = _conv_f_fwd(up, W["f_conv_w"][i], W["f_conv_b"][i][None], f"l{i}_conv_f")
        ffn = _mm(act, V(W["f_w_down"], (i,)), "nn", f"l{i}_ffn_down")
        pv = V(p, (i, 0))
        pp = _mm(pv, V(W["ple_w_proj"], (i,)), "nn", f"l{i}_ple_proj")
        gl = _mm(x1b, V(W["ple_w_gate"], (i,)), "nn", f"l{i}_ple_gate")
        h2, x2, x2b = _res_ln_fwd(x1, [ffn], (pp, gl), W["ln_g"][i, 1][None], W["ln_b"][i, 1][None], f"l{i}_ln2")
        L.update(h1=h1, x1=x1, x1b=x1b, up=up, act=act, pv=pv, pp=pp, gl=gl, h2=h2)
        saved.append(L)
        xc_f, xb = x2, x2b

    sq, dx = _loss_head(xc_f, target, "loss_head")

    GB = GradBuffers()
    into = GB.into
    tok = None

    G = {n: [None] * (DEPTH if n.startswith(("f_", "ln_")) else DEPTH // 2) for n in WEIGHT_ORDER if n not in G_SHAPES}
    for i in reversed(range(DEPTH)):
        j = i // 2
        L = saved[i]
        dh2, dh2b, dg2, db2, dpp, dgl = _res_ln_bwd(dx, L["h2"], W["ln_g"][i, 1][None], (L["pp"], L["gl"]), f"l{i}_ln2_bwd")
        GB["f_w_down"] = _mm(L["act"], dh2b, "tn", f"l{i}_dw_down", dst=into("f_w_down", i))
        dact = _mm(dh2b, V(W["f_w_down"], (i,)), "nt", f"l{i}_dact", after=tok)
        du1, du2, dw1, dw2, dbf1, dbf2 = _conv_f_bwd(L["up"], W["f_conv_w"][i], W["f_conv_b"][i][None], dact, f"l{i}_conv_f_bwd")
        G["f_conv_w"][i] = jnp.concatenate([_fold_rows(dw1, CONV_F), _fold_rows(dw2, CONV_F)], axis=1)
        G["f_conv_b"][i] = jnp.concatenate([dbf1.sum(0), dbf2.sum(0)])
        GB["f_w_up"] = _mm(L["x1b"], du1, "tn", f"l{i}_dw_up1", dst=into("f_w_up", i))
        GB["f_w_up"] = _mm(L["x1b"], du2, "tn", f"l{i}_dw_up2", dst=into("f_w_up", i, c0=D_FF))
        GB["ple_w_proj"] = _mm(L["pv"], dpp, "tn", f"l{i}_dw_proj", dst=into("ple_w_proj", i))
        GB["ple_w_gate"] = _mm(L["x1b"], dgl, "tn", f"l{i}_dw_gate", dst=into("ple_w_gate", i))
        tok = comm.part_grads_done(i, FFN, GB) if comm is not None else None
        dx1 = _mm(du1, V(W["f_w_up"], (i,), cols=D_FF), "nt", f"l{i}_dx1_a", add=dh2, add_scale=ALPHA, after=tok)
        dx1 = _mm(du2, V(W["f_w_up"], (i,), c0=D_FF), "nt", f"l{i}_dx1_b", add=dx1)
        dx1 = _mm(dgl, V(W["ple_w_gate"], (i,)), "nt", f"l{i}_dx1_c", add=dx1)
        dh1, dh1b, dg1, db1 = _res_ln_bwd(dx1, L["h1"], W["ln_g"][i, 0][None], None, f"l{i}_ln1_bwd")
        G["ln_g"][i] = jnp.concatenate([dg1, dg2], axis=0)
        G["ln_b"][i] = jnp.concatenate([db1, db2], axis=0)
        if i % 2 == 0:
            GB["e_w_out"] = _mm(L["ya"], dh1b, "tn", f"l{i}_dw_out_a", dst=into("e_w_out", j))
            GB["e_w_out"] = _mm(L["yb"], dh1b, "tn", f"l{i}_dw_out_b", dst=into("e_w_out", j, r0=D))
            dya = _mm(dh1b, V(W["e_w_out"], (j,), rows=D), "nt", f"l{i}_dya")
            dyb = _mm(dh1b, V(W["e_w_out"], (j,), r0=D), "nt", f"l{i}_dyb")
            dac, dga, dba = _ln_silu_bwd(L["ac"], dya, W["e_ln_a_g"][j][None], W["e_ln_a_b"][j][None], f"l{i}_ln_a_bwd")
            G["e_ln_a_g"][j], G["e_ln_a_b"][j] = dga[0], dba[0]
            dal, dag, dwa, dbca = _conv_a_bwd(L["ua"], W["e_conv_a_w"][j], dac, f"l{i}_conv_a_bwd")
            G["e_conv_a_w"][j] = _fold_rows(dwa, CONV_A)
            G["e_conv_a_b"][j] = dbca.sum(0)
            dy, dz, dgn = _gate_rms_bwd(L["y"], L["z"], dyb, W["e_norm_b_g"][j][None], f"l{i}_gate_rms_bwd")
            G["e_norm_b_g"][j] = dgn[0]
            dxs, dbs, dcs, sq_col, cms_row, dda_col, dda_row, ddp = _ssd_bwd(L["xc"], *L["sm"], W["e_d_skip"][j], L["states"], dy,
                                                                             f"l{i}_ssd_bwd")
            draw, G["e_dt_bias"][j], G["e_a_log"][j] = _ssd_small_grads(L["udt"][:, :N_HEADS], W["e_dt_bias"][j], W["e_a_log"][j],
                                                                       sq_col, cms_row, dda_col, dda_row)
            G["e_d_skip"][j] = ddp[:, :, 0, :].sum(0).reshape(N_HEADS, HEAD_P).sum(1)
            dxu, dwb, dbcb = _conv_b_bwd(L["xu"], W["e_conv_b_w"][j], W["e_conv_b_b"][j][None], dxs, dbs, dcs, f"l{i}_conv_b_bwd")
            G["e_conv_b_w"][j] = _fold_rows(dwb, CONV_B)
            G["e_conv_b_b"][j] = dbcb.sum(0)
            dudt = _padcols(draw, LANES)
            w_in = L["w_in"]
            xb_l = L["xb"]
            for nm, dseg, c0 in (("al", dal, 0), ("ag", dag, D), ("z", dz, SEG_Z[0]), ("xbc", dxu, SEG_X[0]), ("dt", dudt, SEG_DT[0])):
                GB["e_w_in"] = _mm(xb_l, dseg, "tn", f"l{i}_dw_in_{nm}", dst=into("e_w_in", j, c0=c0))
            dx = _mm(dal, w_in(SEG_A, cols=D), "nt", f"l{i}_dx_al", add=dh1, add_scale=ALPHA)
            dx = _mm(dag, w_in(SEG_A, c0=D, cols=D), "nt", f"l{i}_dx_ag", add=dx)
            dx = _mm(dz, w_in(SEG_Z), "nt", f"l{i}_dx_z", add=dx)
            dx = _mm(dxu, w_in(SEG_X), "nt", f"l{i}_dx_xbc", add=dx)
            dx = _mm(dudt, w_in(SEG_DT), "nt", f"l{i}_dx_dt", add=dx)
        else:
            GB["o_w_out"] = _mm(L["sc"], dh1b, "tn", f"l{i}_dw_out", dst=into("o_w_out", j))
            dsc = _mm(dh1b, V(W["o_w_out"], (j,)), "nt", f"l{i}_dsc")
            dbg, dcg, dv, dwc = _conv_c_bwd(L["uo"], W["o_conv_w"][j], dsc, f"l{i}_conv_c_bwd")
            G["o_conv_w"][j] = _fold_rows(dwc, CONV_C)
            xb_l = L["xb"]
            dx = dh1
            for nm, dseg, c0, scale in (("bg", dbg, 0, ALPHA), ("cg", dcg, D, 1.0), ("v", dv, 2 * D, 1.0)):
                GB["o_w_in"] = _mm(xb_l, dseg, "tn", f"l{i}_dw_in_{nm}", dst=into("o_w_in", j, c0=c0))
                dx = _mm(dseg, V(W["o_w_in"], (j,), c0=c0, cols=D), "nt", f"l{i}_dx_{nm}", add=dx, add_scale=scale)
        tok = comm.part_grads_done(i, MIX, GB) if comm is not None else None
    grads = {n: jnp.stack(v) for n, v in G.items()}
    return sq, dx, GB, grads


def _ssd_small_inputs(raw, dt_bias, a_log):
    T = raw.shape[0]
    a = -jnp.exp(a_log)
    rg = raw.reshape(T, N_GROUPS, 4)
    raw_col = jnp.pad(jnp.transpose(rg, (1, 0, 2)), ((0, 0), (0, 0), (0, LANES - 4)))
    raw_row = jnp.pad(jnp.transpose(rg, (1, 2, 0)), ((0, 0), (0, SUBLANES - 4), (0, 0)))

    def colv(v):
        return jnp.pad(v.reshape(N_GROUPS, 1, 4), ((0, 0), (0, 0), (0, LANES - 4)))

    def rowv(v):
        return jnp.pad(v.reshape(N_GROUPS, 4, 1), ((0, 0), (0, SUBLANES - 4), (0, 0)))

    return raw_col, raw_row, colv(dt_bias), rowv(dt_bias), colv(a), rowv(a)


def _ssd_small_grads(raw, dt_bias, a_log, sq_col, cms_row, dda_col, dda_row):
    T = raw.shape[0]

    def join(col, row):
        c = jnp.transpose(col[:, :, :4], (1, 0, 2)).reshape(T, N_HEADS)
        r = jnp.transpose(row[:, :4, :], (2, 0, 1)).reshape(T, N_HEADS)
        return c + r

    a = -jnp.exp(a_log)
    pre = raw + dt_bias
    dt = jax.nn.softplus(pre)
    dda = join(dda_col, dda_row)
    ddt = join(sq_col, cms_row) + a * dda
    draw = ddt * jax.nn.sigmoid(pre)
    da = jnp.sum(dt * dda, axis=0)
    return draw, jnp.sum(draw, axis=0), da * a


def kernel(x, p, e_w_in, e_conv_a_w, e_conv_a_b, e_ln_a_g, e_ln_a_b, e_conv_b_w, e_conv_b_b, e_dt_bias, e_a_log, e_d_skip, e_norm_b_g, e_w_out, o_w_in, o_conv_w, o_w_out, f_w_up, f_conv_w, f_conv_b, f_w_down, ple_w_proj, ple_w_gate, ln_g, ln_b, loss_target, m_e_w_in, m_e_conv_a_w, m_e_conv_a_b, m_e_ln_a_g, m_e_ln_a_b, m_e_conv_b_w, m_e_conv_b_b, m_e_dt_bias, m_e_a_log, m_e_d_skip, m_e_norm_b_g, m_e_w_out, m_o_w_in, m_o_conv_w, m_o_w_out, m_f_w_up, m_f_conv_w, m_f_conv_b, m_f_w_down, m_ple_w_proj, m_ple_w_gate, m_ln_g, m_ln_b, v_e_w_in, v_e_conv_a_w, v_e_conv_a_b, v_e_ln_a_g, v_e_ln_a_b, v_e_conv_b_w, v_e_conv_b_b, v_e_dt_bias, v_e_a_log, v_e_d_skip, v_e_norm_b_g, v_e_w_out, v_o_w_in, v_o_conv_w, v_o_w_out, v_f_w_up, v_f_conv_w, v_f_conv_b, v_f_w_down, v_ple_w_proj, v_ple_w_gate, v_ln_g, v_ln_b):
    args = dict(locals())
    w_shard = {n: args[n] for n in WEIGHT_ORDER}
    m_shard = {n: args["m_" + n] for n in WEIGHT_ORDER}
    v_shard = {n: args["v_" + n] for n in WEIGHT_ORDER}
    xi, yi, ci = lax.axis_index("x"), lax.axis_index("y"), lax.axis_index("c")
    chip = 2 * xi + yi

    mevec = jnp.stack([chip, ci]).astype(jnp.int32)
    small_shapes = [w_shard[n].shape for n, _ in SMALL_SHARDED]
    sr = _small_rows(small_shapes)
    ws = _pack_rows([w_shard[n] for n, _ in SMALL_SHARDED], LANES, sr, F32)
    W = {n: w_shard[n] for n in SMALL_REPL}
    W.update({s[0]: Layers(s[2]) for s in TENSORS})
    w_items = [_group_items(layers) for layers in W_GROUPS]
    g_items = [_group_items(layers) for layers in G_GROUPS]

    def install(items, fulls):
        for it, f in zip(items, fulls):
            if it[0] == "e_w_in":
                f = jnp.transpose(f.reshape(it[2], 4, D, E_IN // 4), (0, 2, 1, 3)).reshape(it[2], D, E_IN)
                f = jnp.pad(f, ((0, 0), (0, 0), (0, E_PAD - E_IN)))
            W[it[0]].put(f, it[6])

    casts = [[_cast_into(w_shard[it[0]], it, mevec[:1]) for it in items] for items in w_items]
    fulls, ssem, rsem, _ = _gather_start(casts[0], w_items[0], "gather_start_0")
    fulls = _gather_wait(fulls, ssem, rsem, casts[-1][-1], w_items[0], "gather_wait_0")
    fulls, WS = _gather_fwd(fulls, w_items[0], "gather_fwd_0", ws)
    install(w_items[0], fulls)
    parts_s = [_unpack_rows(WS[k], small_shapes) for k in range(4)]
    for idx, (n, ax) in enumerate(SMALL_SHARDED):
        W[n] = jnp.concatenate([parts_s[k][idx] for k in range(4)], axis=ax)

    class Comm:
        sent = {}
        started = {}
        tail = fulls[0]

        def start_next(self, gi):
            if gi >= len(w_items):
                return None
            self.started[gi] = _gather_start(casts[gi], w_items[gi], f"gather_start_{gi}", self.tail)
            return self.started[gi][3]

        def part_starts(self, layer, part, after):
            if (layer, part) == W_GROUPS[0][0]:
                return self.start_next(1)
            for gi in range(1, len(W_GROUPS)):
                if W_GROUPS[gi][0] == (layer, part):
                    fulls, ssem, rsem, _ = self.started[gi]
                    fulls = _gather_wait(fulls, ssem, rsem, after, w_items[gi], f"gather_wait_{gi}")
                    fulls, _ = _gather_fwd(fulls, w_items[gi], f"gather_fwd_{gi}")
                    install(w_items[gi], fulls)
                    self.tail = fulls[0]
                    return self.start_next(gi + 1)
            return None

        def part_grads_done(self, layer, part, GB):
            tok = None
            for gi, members in enumerate(G_GROUPS):
                if members[-1] == (layer, part):
                    items = g_items[gi]
                    gs = []
                    for it in items:
                        g = GB[(it[0], gi)]
                        if it[0] == "e_w_in":
                            g = jnp.transpose(g[:, :, :E_IN].reshape(it[2], D, 4, E_IN // 4), (0, 2, 1, 3)).reshape(it[2], 4 * D, E_IN // 4)
                        gs.append(g)
                    ras = _swap_grads(gs, items, f"swap_grads_{gi}")
                    ps = [_add_half(g, ra, it, mevec[1:]) for g, ra, it in zip(gs, ras, items)]
                    thru, ssem, rsem, tok = _scatter_start(ps, items, f"scatter_start_{gi}")
                    self.sent[gi] = (thru, ssem, rsem, tok)
            return tok

    comm = Comm()

    sq, dx, GB, G = _local_step(x[0], p, loss_target[0], W, comm)
    loss = lax.psum(0.5 * sq[0, 0] / D, ("x", "y", "c"))
    grad_x = dx[None]

    def shard_of(g, ax, k):
        n = g.shape[ax] // 4
        return lax.slice_in_dim(g, k * n, (k + 1) * n, axis=ax)

    reduced, updated = {}, {}
    after = comm.sent[len(g_items) - 1][3]
    for gi, items in enumerate(g_items):
        thru, ssem, rsem, _ = comm.sent[gi]
        thru = _scatter_wait(thru, ssem, rsem, after, items, f"scatter_wait_{gi}")
        ps, rcs = thru[:len(items)], thru[len(items):]
        rs = [_sum_own(pt, rc, it, mevec, reduced.get(it[0])) for pt, rc, it in zip(ps, rcs, items)]
        rs = _join_halves(rs, items, f"join_halves_{gi}")
        reduced.update({it[0]: r for it, r in zip(items, rs)})
        for it in items:
            n = it[0]
            updated[n] = _adamw_layers(w_shard[n], reduced[n], m_shard[n], v_shard[n], it[6], it[2], updated.get(n), f"adamw_{n}_{it[6]}")
        after = updated[items[0][0]][0]

    small_all = ([shard_of(G[n], ax, k) for k in range(4) for n, ax in SMALL_SHARDED] + [G[n] for n in SMALL_REPL])
    small_all_shapes = [t.shape for t in small_all]
    mr = _small_rows(small_all_shapes)
    sg = _sum8(_allgather_small(_pack_rows(small_all, LANES, mr, F32)), mr)
    sparts = _unpack_rows(sg, small_all_shapes)
    ns = len(SMALL_SHARDED)
    gsmall = {}
    for idx, (n, ax) in enumerate(SMALL_SHARDED):
        stacked = jnp.stack([sparts[k * ns + idx] for k in range(4)])
        gsmall[n] = lax.dynamic_index_in_dim(stacked, chip, axis=0, keepdims=False)
    for idx, n in enumerate(SMALL_REPL):
        gsmall[n] = sparts[4 * ns + idx]

    grads, deltas, new_m, new_v = [], [], [], []
    for n in WEIGHT_ORDER:
        if n in reduced:
            g, (d, mn, vn) = reduced[n], updated[n]
        else:
            g = gsmall[n]
            d, mn, vn = _adamw(w_shard[n], g, m_shard[n], v_shard[n], f"adamw_{n}")
        grads.append(g)
        deltas.append(d)
        new_m.append(mn)
        new_v.append(vn)
    return (loss, grad_x, *grads, *deltas, *new_m, *new_v)
```

```python
import functools
import math

import jax
import jax.numpy as jnp
from jax import lax
from jax.experimental import pallas as pl
from jax.experimental.pallas import tpu as pltpu

F32 = jnp.float32
BF16 = jnp.bfloat16
MESH = pl.DeviceIdType.MESH

DEPTH = 4
ALPHA = (2.0 * DEPTH) ** 0.25
LN_EPS = 1e-5
D = 1024
HEAD_P = 64
N_STATE = 128
N_HEADS = 16
N_GROUPS = 4
CONV_A, CONV_B, CONV_C, CONV_F = 31, 4, 3, 3
D_FF = 2816
PLE = 256
E_IN = 5136

ADAM_LR, ADAM_B1, ADAM_B2, ADAM_EPS, ADAM_WD, ADAM_STEP = 0.001, 0.9, 0.999, 1e-08, 0.01, 10

LANES = 128
SUBLANES = 8
VMEM_LIMIT = 56 * 1024 * 1024
SSD_Q = 128
CONV_R = 128
CONV_PAD = 32
ROW_T = 256
HI = lax.Precision.HIGHEST


def _cparams(sem=None):
    return pltpu.CompilerParams(dimension_semantics=sem, vmem_limit_bytes=VMEM_LIMIT)


def _sig(v):
    return jax.nn.sigmoid(v)


_DIMS = {"nn": (((1,), (0,)), ((), ())), "nt": (((1,), (1,)), ((), ())), "tn": (((0,), (0,)), ((), ()))}


class Layers:
    def __init__(self, n_layers):
        self.where = [None] * n_layers

    def put(self, arr, l0):
        for k in range(arr.shape[0]):
            self.where[l0 + k] = (arr, k)


class V:
    def __init__(self, arr, lead=(), r0=0, c0=0, rows=None, cols=None):
        if isinstance(arr, Layers):
            arr, k = arr.where[lead[0]]
            lead = (k,) + tuple(lead[1:])
        self.arr, self.lead, self.r0, self.c0 = arr, tuple(lead), r0, c0
        R, C = arr.shape[-2:]
        self.rows = R - r0 if rows is None else rows
        self.cols = C - c0 if cols is None else cols

    def spec(self, br, bc, fn):
        assert self.r0 % br == 0 and self.c0 % bc == 0, (self.r0, self.c0, br, bc)
        ro, co, lead = self.r0 // br, self.c0 // bc, self.lead

        def index(i, j, k):
            r, c = fn(i, j, k)
            return lead + (r + ro, c + co)

        return pl.BlockSpec((None,) * len(lead) + (br, bc), index)


def _v(t):
    return t if isinstance(t, V) else V(t)


def _tile(n, offs, cands):
    for c in cands:
        if n % c == 0 and all(o % c == 0 for o in offs):
            return c
    raise ValueError((n, offs))


_TILES = (1024, 1408, 512, 256, 128)


def _mm(a, b, mode, name, out_dtype=F32, add=None, add_scale=1.0, dst=None, after=None):
    a, b = _v(a), _v(b)
    add = _v(add) if add is not None else None
    if mode == "nn":
        M, K, K2, N = a.rows, a.cols, b.rows, b.cols
        am, ak, bk, bn = a.r0, a.c0, b.r0, b.c0
    elif mode == "nt":
        M, K, N, K2 = a.rows, a.cols, b.rows, b.cols
        am, ak, bn, bk = a.r0, a.c0, b.r0, b.c0
    else:
        K, M, K2, N = a.rows, a.cols, b.rows, b.cols
        ak, am, bk, bn = a.r0, a.c0, b.r0, b.c0
    assert K == K2, (name, mode, M, K, K2, N)
    if dst is None:
        buf, full_shape, o_lead, o_r0, o_c0 = None, (M, N), (), 0, 0
    else:
        buf, full_shape, o_lead, o_r0, o_c0 = dst
    tm = _tile(M, [am, o_r0] + ([add.r0] if add else []), _TILES)
    tn = _tile(N, [bn, o_c0] + ([add.c0] if add else []), _TILES)
    tk = _tile(K, [ak, bk], _TILES)
    nk = K // tk
    has_add, has_buf, has_after = add is not None, buf is not None, after is not None

    def body(*refs):
        a_ref, b_ref = refs[0], refs[1]
        add_ref = refs[2] if has_add else None
        o_ref = refs[2 + has_add + has_buf + has_after]

        def finish(r):
            if has_add:
                r = r + add_scale * add_ref[...].astype(F32)
            o_ref[...] = r.astype(o_ref.dtype)

        part = lax.dot_general(a_ref[...].astype(BF16), b_ref[...].astype(BF16), _DIMS[mode], preferred_element_type=F32)
        if nk == 1:
            finish(part)
        else:
            acc_ref = refs[-1]
            k = pl.program_id(2)

            @pl.when(k == 0)
            def _():
                acc_ref[...] = part

            @pl.when(jnp.logical_and(k > 0, k < nk - 1))
            def _():
                acc_ref[...] += part

            @pl.when(k == nk - 1)
            def _():
                finish(acc_ref[...] + part)

    if mode == "tn":
        a_spec = a.spec(tk, tm, lambda i, j, k: (k, i))
    else:
        a_spec = a.spec(tm, tk, lambda i, j, k: (i, k))
    if mode == "nt":
        b_spec = b.spec(tn, tk, lambda i, j, k: (j, k))
    else:
        b_spec = b.spec(tk, tn, lambda i, j, k: (k, j))
    in_specs, args = [a_spec, b_spec], [a.arr, b.arr]
    if has_add:
        in_specs.append(add.spec(tm, tn, lambda i, j, k: (i, j)))
        args.append(add.arr)
    aliases = {}
    if has_buf:
        aliases = {len(args): 0}
        in_specs.append(pl.BlockSpec(memory_space=pl.ANY))
        args.append(buf)
        out_dtype = buf.dtype
    if has_after:
        in_specs.append(pl.BlockSpec(memory_space=pl.ANY))
        args.append(after)
    o_view = V(jax.ShapeDtypeStruct(full_shape, out_dtype), o_lead, o_r0, o_c0, M, N)
    return pl.pallas_call(
        body, name=name, grid=(M // tm, N // tn, nk), in_specs=in_specs, out_specs=o_view.spec(tm, tn, lambda i, j, k: (i, j)),
        out_shape=jax.ShapeDtypeStruct(full_shape, out_dtype), input_output_aliases=aliases,
        scratch_shapes=[pltpu.VMEM((tm, tn), F32)] if nk > 1 else [],
        compiler_params=_cparams(("parallel", "parallel", "arbitrary")))(*args)


def _rows(T, width=D):
    return pl.BlockSpec((ROW_T, width), lambda i: (i, 0))


def _vec(width=D):
    return pl.BlockSpec((1, width), lambda i: (0, 0))


def _ln_stats(h):
    mu = jnp.mean(h, axis=-1, keepdims=True)
    hc = h - mu
    var = jnp.mean(hc * hc, axis=-1, keepdims=True)
    rstd = lax.rsqrt(var + LN_EPS)
    return hc * rstd, rstd


def _res_ln_fwd(x, adds, ple, g, b, name):
    T = x.shape[0]
    n_add = len(adds)
    has_ple = ple is not None

    def body(*refs):
        x_ref = refs[0]
        add_refs = refs[1:1 + n_add]
        pos = 1 + n_add
        if has_ple:
            pp_ref, gl_ref = refs[pos], refs[pos + 1]
            pos += 2
        g_ref, b_ref, h_ref, y_ref, yb_ref = refs[pos:pos + 5]
        h = ALPHA * x_ref[...]
        for r in add_refs:
            h = h + r[...]
        if has_ple:
            h = h + pp_ref[...] * _sig(gl_ref[...])
        xhat, _ = _ln_stats(h)
        y = xhat * g_ref[...] + b_ref[...]
        h_ref[...] = h
        y_ref[...] = y
        yb_ref[...] = y.astype(BF16)

    n_in = 1 + n_add + (2 if has_ple else 0)
    args = (x,) + tuple(adds) + (tuple(ple) if has_ple else ()) + (g, b)
    return pl.pallas_call(
        body, name=name, grid=(T // ROW_T,), in_specs=[_rows(T)] * n_in + [_vec(), _vec()],
        out_specs=(_rows(T), _rows(T), _rows(T)),
        out_shape=(jax.ShapeDtypeStruct((T, D), F32), jax.ShapeDtypeStruct((T, D), F32), jax.ShapeDtypeStruct((T, D), BF16)),
        compiler_params=_cparams(("parallel",)))(*args)


def _res_ln_bwd(dy, h, g, ple, name):
    T = dy.shape[0]
    has_ple = ple is not None

    def body(*refs):
        if has_ple:
            dy_ref, h_ref, g_ref, pp_ref, gl_ref, dh_ref, dhb_ref, dg_ref, db_ref, dpp_ref, dgl_ref = refs
        else:
            dy_ref, h_ref, g_ref, dh_ref, dhb_ref, dg_ref, db_ref = refs
        i = pl.program_id(0)

        @pl.when(i == 0)
        def _():
            dg_ref[...] = jnp.zeros_like(dg_ref)
            db_ref[...] = jnp.zeros_like(db_ref)

        dyv = dy_ref[...]
        xhat, rstd = _ln_stats(h_ref[...])
        dg_ref[...] += jnp.sum(dyv * xhat, axis=0, keepdims=True)
        db_ref[...] += jnp.sum(dyv, axis=0, keepdims=True)
        dxh = dyv * g_ref[...]
        dh = rstd * (dxh - jnp.mean(dxh, axis=-1, keepdims=True) - xhat * jnp.mean(dxh * xhat, axis=-1, keepdims=True))
        dh_ref[...] = dh
        dhb_ref[...] = dh.astype(BF16)
        if has_ple:
            s = _sig(gl_ref[...])
            dpp_ref[...] = (dh * s).astype(BF16)
            dgl_ref[...] = (dh * pp_ref[...] * s * (1.0 - s)).astype(BF16)

    args = (dy, h, g) + (tuple(ple) if has_ple else ())
    in_specs = [_rows(T), _rows(T), _vec()] + ([_rows(T), _rows(T)] if has_ple else [])
    out_specs = [_rows(T), _rows(T), _vec(), _vec()] + ([_rows(T), _rows(T)] if has_ple else [])
    out_shape = [jax.ShapeDtypeStruct((T, D), F32), jax.ShapeDtypeStruct((T, D), BF16),
                 jax.ShapeDtypeStruct((1, D), F32), jax.ShapeDtypeStruct((1, D), F32)]
    if has_ple:
        out_shape += [jax.ShapeDtypeStruct((T, D), BF16), jax.ShapeDtypeStruct((T, D), BF16)]
    return pl.pallas_call(
        body, name=name, grid=(T // ROW_T,), in_specs=in_specs, out_specs=tuple(out_specs), out_shape=tuple(out_shape),
        compiler_params=_cparams(("arbitrary",)))(*args)


def _ln_silu_fwd(ac, g, b, name):
    T = ac.shape[0]

    def body(a_ref, g_ref, b_ref, o_ref):
        xhat, _ = _ln_stats(a_ref[...])
        ln = xhat * g_ref[...] + b_ref[...]
        o_ref[...] = (ln * _sig(ln)).astype(BF16)

    return pl.pallas_call(
        body, name=name, grid=(T // ROW_T,), in_specs=[_rows(T), _vec(), _vec()], out_specs=_rows(T),
        out_shape=jax.ShapeDtypeStruct((T, D), BF16), compiler_params=_cparams(("parallel",)))(ac, g, b)


def _ln_silu_bwd(ac, dya, g, b, name):
    T = ac.shape[0]

    def body(a_ref, d_ref, g_ref, b_ref, da_ref, dg_ref, db_ref):
        i = pl.program_id(0)

        @pl.when(i == 0)
        def _():
            dg_ref[...] = jnp.zeros_like(dg_ref)
            db_ref[...] = jnp.zeros_like(db_ref)

        xhat, rstd = _ln_stats(a_ref[...])
        ln = xhat * g_ref[...] + b_ref[...]
        s = _sig(ln)
        dln = d_ref[...] * s * (1.0 + ln * (1.0 - s))
        dg_ref[...] += jnp.sum(dln * xhat, axis=0, keepdims=True)
        db_ref[...] += jnp.sum(dln, axis=0, keepdims=True)
        dxh = dln * g_ref[...]
        da_ref[...] = rstd * (dxh - jnp.mean(dxh, axis=-1, keepdims=True)
                              - xhat * jnp.mean(dxh * xhat, axis=-1, keepdims=True))

    return pl.pallas_call(
        body, name=name, grid=(T // ROW_T,), in_specs=[_rows(T), _rows(T), _vec(), _vec()],
        out_specs=(_rows(T), _vec(), _vec()),
        out_shape=(jax.ShapeDtypeStruct((T, D), F32), jax.ShapeDtypeStruct((1, D), F32), jax.ShapeDtypeStruct((1, D), F32)),
        compiler_params=_cparams(("arbitrary",)))(ac, dya, g, b)


def _gate_rms_fwd(y, z, g, name):
    T = y.shape[0]

    def body(y_ref, z_ref, g_ref, o_ref):
        zv = z_ref[...]
        yg = y_ref[...] * (zv * _sig(zv))
        r = lax.rsqrt(jnp.mean(yg * yg, axis=-1, keepdims=True) + LN_EPS)
        o_ref[...] = (yg * r * g_ref[...]).astype(BF16)

    return pl.pallas_call(
        body, name=name, grid=(T // ROW_T,), in_specs=[_rows(T), _rows(T), _vec()], out_specs=_rows(T),
        out_shape=jax.ShapeDtypeStruct((T, D), BF16), compiler_params=_cparams(("parallel",)))(y, z, g)


def _gate_rms_bwd(y, z, dout, g, name):
    T = y.shape[0]

    def body(y_ref, z_ref, d_ref, g_ref, dy_ref, dz_ref, dg_ref):
        i = pl.program_id(0)

        @pl.when(i == 0)
        def _():
            dg_ref[...] = jnp.zeros_like(dg_ref)

        yv, zv, dv = y_ref[...], z_ref[...], d_ref[...]
        s = _sig(zv)
        sz = zv * s
        yg = yv * sz
        r = lax.rsqrt(jnp.mean(yg * yg, axis=-1, keepdims=True) + LN_EPS)
        dg_ref[...] += jnp.sum(dv * yg * r, axis=0, keepdims=True)
        dn = dv * g_ref[...]
        dyg = r * dn - yg * (r * r * r) * jnp.mean(dn * yg, axis=-1, keepdims=True)
        dy_ref[...] = dyg * sz
        dz_ref[...] = dyg * yv * s * (1.0 + zv * (1.0 - s))

    return pl.pallas_call(
        body, name=name, grid=(T // ROW_T,), in_specs=[_rows(T), _rows(T), _rows(T), _vec()],
        out_specs=(_rows(T), _rows(T), _vec()),
        out_shape=(jax.ShapeDtypeStruct((T, D), F32), jax.ShapeDtypeStruct((T, D), F32), jax.ShapeDtypeStruct((1, D), F32)),
        compiler_params=_cparams(("arbitrary",)))(y, z, dout, g)


def _loss_head(y, target, name):
    T = y.shape[0]

    def body(y_ref, t_ref, s_ref, d_ref):
        i = pl.program_id(0)

        @pl.when(i == 0)
        def _():
            s_ref[...] = jnp.zeros_like(s_ref)

        err = y_ref[...] - t_ref[...]
        s_ref[...] += jnp.sum(jnp.sum(err * err, axis=1, keepdims=True), axis=0, keepdims=True)
        d_ref[...] = err * (1.0 / D)

    return pl.pallas_call(
        body, name=name, grid=(T // ROW_T,), in_specs=[_rows(T), _rows(T)],
        out_specs=(pl.BlockSpec((SUBLANES, LANES), lambda i: (0, 0)), _rows(T)),
        out_shape=(jax.ShapeDtypeStruct((SUBLANES, LANES), F32), jax.ShapeDtypeStruct((T, D), F32)),
        compiler_params=_cparams(("arbitrary",)))(y, target)


def _taps_fwd(pad_ref, w_ref, K, base):
    off = CONV_PAD - (K - 1)
    acc = w_ref[0:1, :] * pad_ref[pl.ds(base + off, CONV_R), :]
    for k in range(1, K):
        acc = acc + w_ref[k:k + 1, :] * pad_ref[pl.ds(base + off + k, CONV_R), :]
    return acc


def _taps_bwd(padd_ref, w_ref, K, base):
    acc = w_ref[0:1, :] * padd_ref[pl.ds(base + (K - 1), CONV_R), :]
    for k in range(1, K):
        acc = acc + w_ref[k:k + 1, :] * padd_ref[pl.ds(base + (K - 1) - k, CONV_R), :]
    return acc


def _f32(ref, rows):
    return ref[rows, :].astype(F32)


def _fold8(v):
    return v.reshape(CONV_R // SUBLANES, SUBLANES, v.shape[-1]).sum(0)


def _wgrad_acc(dw_ref, pad_ref, d, K, base):
    off = CONV_PAD - (K - 1)
    for k in range(K):
        dw_ref[k * SUBLANES:(k + 1) * SUBLANES, :] += _fold8(d * pad_ref[pl.ds(base + off + k, CONV_R), :])


def _loop_rows(T, fn):
    def step(r, carry):
        fn(pl.multiple_of(r * CONV_R, CONV_R))
        return carry
    lax.fori_loop(0, T // CONV_R, step, 0)


def _col(T, off_blocks=0, rows=None):
    return pl.BlockSpec((T if rows is None else rows, LANES), lambda j: (0, j + off_blocks))


def _conv_call(body, name, T, n_tiles, in_specs, out_specs, out_shape, n_pad, n_padd=0):
    scratch = [pltpu.VMEM((T + CONV_PAD, LANES), F32)] * (n_pad + n_padd)
    return pl.pallas_call(body, name=name, grid=(n_tiles,), in_specs=in_specs, out_specs=out_specs, out_shape=out_shape,
                          scratch_shapes=scratch, compiler_params=_cparams(("parallel",)))


def _zero_head(ref):
    ref[0:CONV_PAD, :] = jnp.zeros((CONV_PAD, LANES), F32)


def _zero_tail(ref, T):
    ref[T:T + CONV_PAD, :] = jnp.zeros((CONV_PAD, LANES), F32)


def _sds(shape, dtype=F32):
    return jax.ShapeDtypeStruct(shape, dtype)


def _conv_a_fwd(ua, w, b, name):
    T = ua.shape[0]
    K, nt = CONV_A, D // LANES

    def body(al_ref, ag_ref, w_ref, b_ref, o_ref, pad_ref):
        _zero_head(pad_ref)

        def pre(base):
            rows = pl.ds(base, CONV_R)
            pad_ref[pl.ds(base + CONV_PAD, CONV_R), :] = _f32(al_ref, rows) * _sig(_f32(ag_ref, rows))
        _loop_rows(T, pre)

        def main(base):
            o_ref[pl.ds(base, CONV_R), :] = _taps_fwd(pad_ref, w_ref, K, base) + b_ref[...]
        _loop_rows(T, main)

    return _conv_call(body, name, T, nt, [_col(T), _col(T, nt), _col(T, rows=K), _col(T, rows=1)], _col(T),
                      _sds((T, D)), 1)(ua, ua, w, b)


def _conv_a_bwd(ua, w, dac, name):
    T = ua.shape[0]
    K, nt = CONV_A, D // LANES

    def body(al_ref, ag_ref, w_ref, d_ref, dal_ref, dag_ref, dw_ref, db_ref, pad_ref, padd_ref):
        _zero_head(pad_ref)
        _zero_tail(padd_ref, T)
        dw_ref[...] = jnp.zeros_like(dw_ref)
        db_ref[...] = jnp.zeros_like(db_ref)

        def pre(base):
            rows = pl.ds(base, CONV_R)
            pad_ref[pl.ds(base + CONV_PAD, CONV_R), :] = _f32(al_ref, rows) * _sig(_f32(ag_ref, rows))
            padd_ref[rows, :] = d_ref[rows, :]
        _loop_rows(T, pre)

        def main(base):
            rows = pl.ds(base, CONV_R)
            d = d_ref[rows, :]
            _wgrad_acc(dw_ref, pad_ref, d, K, base)
            db_ref[...] += _fold8(d)
            da = _taps_bwd(padd_ref, w_ref, K, base)
            al, s = _f32(al_ref, rows), _sig(_f32(ag_ref, rows))
            dal_ref[rows, :] = da * s
            dag_ref[rows, :] = da * al * s * (1.0 - s)
        _loop_rows(T, main)

    return _conv_call(body, name, T, nt, [_col(T), _col(T, nt), _col(T, rows=K), _col(T)],
                      (_col(T), _col(T), _col(T, rows=K * SUBLANES), _col(T, rows=SUBLANES)),
                      (_sds((T, D)), _sds((T, D)), _sds((K * SUBLANES, D)), _sds((SUBLANES, D))), 1, 1)(ua, ua, w, dac)


def _conv_b_fwd(xu, w, b, name):
    T, C = xu.shape
    K, nt = CONV_B, C // LANES

    def body(x_ref, w_ref, b_ref, o_ref, pad_ref):
        _zero_head(pad_ref)
        pad_ref[CONV_PAD:CONV_PAD + T, :] = x_ref[...].astype(F32)

        def main(base):
            hc = _taps_fwd(pad_ref, w_ref, K, base) + b_ref[...]
            o_ref[pl.ds(base, CONV_R), :] = hc * _sig(hc)
        _loop_rows(T, main)

    return _conv_call(body, name, T, nt, [_col(T), _col(T, rows=K), _col(T, rows=1)], _col(T), _sds((T, C)), 1)(xu, w, b)


def _conv_b_bwd(xu, w, b, dxs, dbs, dcs, name):
    T, C = xu.shape
    K, nt = CONV_B, C // LANES
    nx, nb = dxs.shape[1] // LANES, dbs.shape[1] // LANES

    def body(x_ref, w_ref, b_ref, d1_ref, d2_ref, d3_ref, dx_ref, dw_ref, db_ref, pad_ref, padd_ref):
        j = pl.program_id(0)
        _zero_head(pad_ref)
        _zero_tail(padd_ref, T)
        dw_ref[...] = jnp.zeros_like(dw_ref)
        db_ref[...] = jnp.zeros_like(db_ref)
        pad_ref[CONV_PAD:CONV_PAD + T, :] = x_ref[...].astype(F32)

        def pre(base):
            rows = pl.ds(base, CONV_R)
            hc = _taps_fwd(pad_ref, w_ref, K, base) + b_ref[...]
            s = _sig(hc)
            d = jnp.where(j < nx, d1_ref[rows, :], jnp.where(j < nx + nb, d2_ref[rows, :], d3_ref[rows, :]))
            padd_ref[rows, :] = d * s * (1.0 + hc * (1.0 - s))
        _loop_rows(T, pre)

        def main(base):
            d = padd_ref[pl.ds(base, CONV_R), :]
            _wgrad_acc(dw_ref, pad_ref, d, K, base)
            db_ref[...] += _fold8(d)
            dx_ref[pl.ds(base, CONV_R), :] = _taps_bwd(padd_ref, w_ref, K, base)
        _loop_rows(T, main)

    def piece(lo, n):
        return pl.BlockSpec((T, LANES), lambda j: (0, jnp.clip(j - lo, 0, n - 1)))

    return _conv_call(body, name, T, nt,
                      [_col(T), _col(T, rows=K), _col(T, rows=1), piece(0, nx), piece(nx, nb), piece(nx + nb, nt - nx - nb)],
                      (_col(T), _col(T, rows=K * SUBLANES), _col(T, rows=SUBLANES)),
                      (_sds((T, C)), _sds((K * SUBLANES, C)), _sds((SUBLANES, C))), 1, 1)(xu, w, b, dxs, dbs, dcs)


def _conv_c_fwd(uo, w, name):
    T = uo.shape[0]
    K, nt = CONV_C, D // LANES

    def body(bg_ref, cg_ref, v_ref, w_ref, o_ref, pad_ref):
        _zero_head(pad_ref)
        pad_ref[CONV_PAD:CONV_PAD + T, :] = cg_ref[...].astype(F32) * v_ref[...].astype(F32)

        def main(base):
            rows = pl.ds(base, CONV_R)
            o_ref[rows, :] = (_f32(bg_ref, rows) * _taps_fwd(pad_ref, w_ref, K, base)).astype(BF16)
        _loop_rows(T, main)

    return _conv_call(body, name, T, nt, [_col(T), _col(T, nt), _col(T, 2 * nt), _col(T, rows=K)], _col(T),
                      _sds((T, D), BF16), 1)(uo, uo, uo, w)


def _conv_c_bwd(uo, w, dsc, name):
    T = uo.shape[0]
    K, nt = CONV_C, D // LANES

    def body(bg_ref, cg_ref, v_ref, w_ref, d_ref, dbg_ref, dcg_ref, dv_ref, dw_ref, pad_ref, padd_ref):
        _zero_head(pad_ref)
        _zero_tail(padd_ref, T)
        dw_ref[...] = jnp.zeros_like(dw_ref)
        pad_ref[CONV_PAD:CONV_PAD + T, :] = cg_ref[...].astype(F32) * v_ref[...].astype(F32)

        def pre(base):
            rows = pl.ds(base, CONV_R)
            d = d_ref[rows, :]
            dbg_ref[rows, :] = (d * _taps_fwd(pad_ref, w_ref, K, base)).astype(BF16)
            padd_ref[rows, :] = d * _f32(bg_ref, rows)
        _loop_rows(T, pre)

        def main(base):
            rows = pl.ds(base, CONV_R)
            _wgrad_acc(dw_ref, pad_ref, padd_ref[rows, :], K, base)
            dq = _taps_bwd(padd_ref, w_ref, K, base)
            dcg_ref[rows, :] = (dq * _f32(v_ref, rows)).astype(BF16)
            dv_ref[rows, :] = (dq * _f32(cg_ref, rows)).astype(BF16)
        _loop_rows(T, main)

    return _conv_call(body, name, T, nt, [_col(T), _col(T, nt), _col(T, 2 * nt), _col(T, rows=K), _col(T)],
                      (_col(T), _col(T), _col(T), _col(T, rows=K * SUBLANES)),
                      (_sds((T, D), BF16), _sds((T, D), BF16), _sds((T, D), BF16), _sds((K * SUBLANES, D))), 1, 1)(uo, uo, uo, w, dsc)


def _conv_f_fwd(up, w, b, name):
    T = up.shape[0]
    K, nt = CONV_F, D_FF // LANES

    def body(u1_ref, u2_ref, w1_ref, w2_ref, b1_ref, b2_ref, o_ref, pad1_ref, pad2_ref):
        _zero_head(pad1_ref)
        _zero_head(pad2_ref)
        pad1_ref[CONV_PAD:CONV_PAD + T, :] = u1_ref[...].astype(F32)
        pad2_ref[CONV_PAD:CONV_PAD + T, :] = u2_ref[...].astype(F32)

        def main(base):
            h1 = _taps_fwd(pad1_ref, w1_ref, K, base) + b1_ref[...]
            h2 = _taps_fwd(pad2_ref, w2_ref, K, base) + b2_ref[...]
            o_ref[pl.ds(base, CONV_R), :] = (h1 * _sig(h1) * h2).astype(BF16)
        _loop_rows(T, main)

    return _conv_call(body, name, T, nt,
                      [_col(T), _col(T, nt), _col(T, rows=K), _col(T, nt, rows=K), _col(T, rows=1), _col(T, nt, rows=1)],
                      _col(T), _sds((T, D_FF), BF16), 2)(up, up, w, w, b, b)


def _conv_f_bwd(up, w, b, dact, name):
    T = up.shape[0]
    K, nt = CONV_F, D_FF // LANES

    def body(u1_ref, u2_ref, w1_ref, w2_ref, b1_ref, b2_ref, d_ref, du1_ref, du2_ref, dw1_ref, dw2_ref, db1_ref, db2_ref,
             pad1_ref, pad2_ref, padd1_ref, padd2_ref):
        _zero_head(pad1_ref)
        _zero_head(pad2_ref)
        _zero_tail(padd1_ref, T)
        _zero_tail(padd2_ref, T)
        for r in (dw1_ref, dw2_ref, db1_ref, db2_ref):
            r[...] = jnp.zeros_like(r)
        pad1_ref[CONV_PAD:CONV_PAD + T, :] = u1_ref[...].astype(F32)
        pad2_ref[CONV_PAD:CONV_PAD + T, :] = u2_ref[...].astype(F32)

        def pre(base):
            rows = pl.ds(base, CONV_R)
            h1 = _taps_fwd(pad1_ref, w1_ref, K, base) + b1_ref[...]
            h2 = _taps_fwd(pad2_ref, w2_ref, K, base) + b2_ref[...]
            s = _sig(h1)
            d = _f32(d_ref, rows)
            padd1_ref[rows, :] = d * h2 * s * (1.0 + h1 * (1.0 - s))
            padd2_ref[rows, :] = d * h1 * s
        _loop_rows(T, pre)

        def main(base):
            rows = pl.ds(base, CONV_R)
            d1, d2 = padd1_ref[rows, :], padd2_ref[rows, :]
            _wgrad_acc(dw1_ref, pad1_ref, d1, K, base)
            _wgrad_acc(dw2_ref, pad2_ref, d2, K, base)
            db1_ref[...] += _fold8(d1)
            db2_ref[...] += _fold8(d2)
            du1_ref[rows, :] = _taps_bwd(padd1_ref, w1_ref, K, base).astype(BF16)
            du2_ref[rows, :] = _taps_bwd(padd2_ref, w2_ref, K, base).astype(BF16)
        _loop_rows(T, main)

    wrow, brow = _col(T, rows=K * SUBLANES), _col(T, rows=SUBLANES)
    return _conv_call(body, name, T, nt,
                      [_col(T), _col(T, nt), _col(T, rows=K), _col(T, nt, rows=K), _col(T, rows=1), _col(T, nt, rows=1), _col(T)],
                      (_col(T), _col(T), wrow, wrow, brow, brow),
                      (_sds((T, D_FF), BF16), _sds((T, D_FF), BF16), _sds((K * SUBLANES, D_FF)), _sds((K * SUBLANES, D_FF)),
                       _sds((SUBLANES, D_FF)), _sds((SUBLANES, D_FF))), 2, 2)(up, up, w, w, b, b, dact)


def _dot(a, b, dims="nn"):
    return lax.dot_general(a.astype(BF16), b.astype(BF16), _DIMS[dims], preferred_element_type=F32)


def _ssd_small(xcr_ref, xrr_ref, bc_ref, br_ref, ac_ref, ar_ref):
    Q = SSD_Q
    li = lax.broadcasted_iota(jnp.int32, (Q, Q), 0)
    si = lax.broadcasted_iota(jnp.int32, (Q, Q), 1)
    tril = li >= si
    dtc = jax.nn.softplus(xcr_ref[...] + bc_ref[...])
    dtr = jax.nn.softplus(xrr_ref[...] + br_ref[...])
    cumc = jnp.dot(tril.astype(F32), dtc * ac_ref[...], precision=HI, preferred_element_type=F32)
    cumr = jnp.dot(dtr * ar_ref[...], (li <= si).astype(F32), precision=HI, preferred_element_type=F32)
    return tril, dtc, dtr, cumc, cumr


def _ssd_specs(nc, rev):
    Q = SSD_Q
    cc = (lambda c: nc - 1 - c) if rev else (lambda c: c)
    x_spec = pl.BlockSpec((Q, 2 * LANES), lambda g, c: (cc(c), g))
    b_spec = pl.BlockSpec((Q, LANES), lambda g, c: (cc(c), 8 + g))
    c_spec = pl.BlockSpec((Q, LANES), lambda g, c: (cc(c), 12 + g))
    colm = pl.BlockSpec((None, Q, LANES), lambda g, c: (g, cc(c), 0))
    rowm = pl.BlockSpec((None, SUBLANES, Q), lambda g, c: (g, 0, cc(c)))
    colv = pl.BlockSpec((None, 1, LANES), lambda g, c: (g, 0, 0))
    rowv = pl.BlockSpec((None, SUBLANES, 1), lambda g, c: (g, 0, 0))
    st_spec = pl.BlockSpec((None, None, 2 * LANES, N_STATE), lambda g, c: (cc(c), g, 0, 0))
    return x_spec, b_spec, c_spec, colm, rowm, colv, rowv, st_spec


def _ssd_fwd(xc, raw_col, raw_row, bias_col, bias_row, a_col, a_row, dskip, name):
    T = xc.shape[0]
    Q = SSD_Q
    nc = T // Q
    x_spec, b_spec, c_spec, colm, rowm, colv, rowv, st_spec = _ssd_specs(nc, False)

    def body(dk_ref, x_ref, b_ref, c_ref, xcr_ref, xrr_ref, bc_ref, br_ref, ac_ref, ar_ref, y_ref, st_ref, h_ref):
        g = pl.program_id(0)

        @pl.when(pl.program_id(1) == 0)
        def _():
            h_ref[...] = jnp.zeros_like(h_ref)

        tril, dtc, dtr, cumc, cumr = _ssd_small(xcr_ref, xrr_ref, bc_ref, br_ref, ac_ref, ar_ref)
        Bm, Cm = b_ref[...], c_ref[...]
        S = _dot(Cm, Bm, "nt")
        lo = lax.broadcasted_iota(jnp.int32, (Q, LANES), 1) < HEAD_P
        rlo = lax.broadcasted_iota(jnp.int32, (LANES, N_STATE), 0) < HEAD_P
        st_ref[...] = h_ref[...]
        clast = cumc[Q - 1:Q, :]
        for pr in range(2):
            cols = slice(pr * LANES, (pr + 1) * LANES)
            xp = x_ref[:, cols]
            yd = jnp.zeros((Q, LANES), F32)
            for q in range(2):
                hh = 2 * pr + q
                seg = cumc[:, hh:hh + 1] - cumr[hh:hh + 1, :]
                lm = jnp.where(tril, jnp.exp(jnp.where(tril, seg, 0.0)), 0.0)
                w = S * lm * dtr[hh:hh + 1, :]
                xm = jnp.where(lo if q == 0 else jnp.logical_not(lo), xp, 0.0)
                yd = yd + _dot(w, xm)
            h0, h1 = 2 * pr, 2 * pr + 1
            c0, c1 = cumc[:, h0:h0 + 1], cumc[:, h1:h1 + 1]
            e_pair = jnp.where(lo, jnp.exp(c0), jnp.exp(c1))
            hp = h_ref[cols, :]
            ch = _dot(Cm, hp, "nt")
            dsk = jnp.where(lo, dk_ref[4 * g + h0], dk_ref[4 * g + h1])
            y_ref[:, cols] = yd + e_pair * ch + dsk * xp
            cl0, cl1 = clast[:, h0:h0 + 1], clast[:, h1:h1 + 1]
            sdec = jnp.where(lo, jnp.exp(cl0 - c0) * dtc[:, h0:h0 + 1], jnp.exp(cl1 - c1) * dtc[:, h1:h1 + 1])
            decrow = jnp.where(rlo, jnp.exp(cl0), jnp.exp(cl1))
            h_ref[cols, :] = hp * decrow + _dot(xp * sdec, Bm, "tn")

    smem = pl.BlockSpec(memory_space=pltpu.SMEM)
    return pl.pallas_call(
        body, name=name, grid=(N_GROUPS, nc),
        in_specs=[smem, x_spec, b_spec, c_spec, colm, rowm, colv, rowv, colv, rowv],
        out_specs=(x_spec, st_spec),
        out_shape=(_sds((T, D)), _sds((nc, N_GROUPS, 2 * LANES, N_STATE))),
        scratch_shapes=[pltpu.VMEM((2 * LANES, N_STATE), F32)],
        compiler_params=_cparams(("parallel", "arbitrary")))(dskip, xc, xc, xc, raw_col, raw_row, bias_col, bias_row, a_col, a_row)


def _ssd_bwd(xc, raw_col, raw_row, bias_col, bias_row, a_col, a_row, dskip, states, dy, name):
    T = xc.shape[0]
    Q = SSD_Q
    nc = T // Q
    x_spec, b_spec, c_spec, colm, rowm, colv, rowv, st_spec = _ssd_specs(nc, True)
    bo_spec = pl.BlockSpec((Q, LANES), lambda g, c: (nc - 1 - c, g))
    dd_spec = pl.BlockSpec((None, None, SUBLANES, 2 * LANES), lambda g, c: (nc - 1 - c, g, 0, 0))

    def body(dk_ref, x_ref, b_ref, c_ref, xcr_ref, xrr_ref, bc_ref, br_ref, ac_ref, ar_ref, st_ref, dy_ref,
             dx_ref, db_ref, dc_ref, sq_ref, cms_ref, ddac_ref, ddar_ref, dd_ref, dh_ref):
        g = pl.program_id(0)

        @pl.when(pl.program_id(1) == 0)
        def _():
            dh_ref[...] = jnp.zeros_like(dh_ref)

        tril, dtc, dtr, cumc, cumr = _ssd_small(xcr_ref, xrr_ref, bc_ref, br_ref, ac_ref, ar_ref)
        Bm, Cm = b_ref[...], c_ref[...]
        S = _dot(Cm, Bm, "nt")
        lane = lax.broadcasted_iota(jnp.int32, (Q, LANES), 1)
        sub = lax.broadcasted_iota(jnp.int32, (SUBLANES, Q), 0)
        rowi = lax.broadcasted_iota(jnp.int32, (Q, LANES), 0)
        lo = lane < HEAD_P
        rlo = lax.broadcasted_iota(jnp.int32, (LANES, N_STATE), 0) < HEAD_P
        clast = cumc[Q - 1:Q, :]
        ds_g = jnp.zeros((Q, Q), F32)
        dcm = jnp.zeros((Q, N_STATE), F32)
        dbm = jnp.zeros((Q, N_STATE), F32)
        dcum_col = jnp.zeros((Q, LANES), F32)
        dcum_row = jnp.zeros((SUBLANES, Q), F32)
        sq_col = jnp.zeros((Q, LANES), F32)
        cms_row = jnp.zeros((SUBLANES, Q), F32)
        for pr in range(2):
            cols = slice(pr * LANES, (pr + 1) * LANES)
            xp, dyp = x_ref[:, cols], dy_ref[:, cols]
            hin, dhp = st_ref[cols, :], dh_ref[cols, :]
            h0, h1 = 2 * pr, 2 * pr + 1
            c0, c1 = cumc[:, h0:h0 + 1], cumc[:, h1:h1 + 1]
            cl0, cl1 = clast[:, h0:h0 + 1], clast[:, h1:h1 + 1]
            e_pair = jnp.where(lo, jnp.exp(c0), jnp.exp(c1))
            edec = jnp.where(lo, jnp.exp(cl0 - c0), jnp.exp(cl1 - c1))
            dt_pair = jnp.where(lo, dtc[:, h0:h0 + 1], dtc[:, h1:h1 + 1])
            sdec = edec * dt_pair
            ch = _dot(Cm, hin, "nt")
            xb = _dot(Bm, dhp, "nt")
            dye = dyp * e_pair
            t1 = dye * ch
            t2 = xp * xb * edec
            hh_prod = dhp * hin
            dsk = jnp.where(lo, dk_ref[4 * g + h0], dk_ref[4 * g + h1])
            dxp = sdec * xb + dsk * dyp
            for q in range(2):
                hh = 2 * pr + q
                mine = lo if q == 0 else jnp.logical_not(lo)
                seg = cumc[:, hh:hh + 1] - cumr[hh:hh + 1, :]
                lm = jnp.where(tril, jnp.exp(jnp.where(tril, seg, 0.0)), 0.0)
                dtrow = dtr[hh:hh + 1, :]
                w = S * lm * dtrow
                dym = jnp.where(mine, dyp, 0.0)
                gl = _dot(dym, xp, "nt") * lm
                ds_g = ds_g + gl * dtrow
                ms = gl * S
                m = ms * dtrow
                dxp = dxp + _dot(w, dym, "tn")
                cms_row = jnp.where(sub == hh, jnp.sum(ms, axis=0, keepdims=True), cms_row)
                dcum_row = jnp.where(sub == hh, -jnp.sum(m, axis=0, keepdims=True), dcum_row)
                t1h = jnp.sum(jnp.where(mine, t1, 0.0), axis=1, keepdims=True)
                sqh = jnp.sum(jnp.where(mine, t2, 0.0), axis=1, keepdims=True)
                sth = sqh * dtc[:, hh:hh + 1]
                rmine = rlo if q == 0 else jnp.logical_not(rlo)
                hsum = jnp.sum(jnp.sum(jnp.where(rmine, hh_prod, 0.0), axis=1, keepdims=True), axis=0, keepdims=True)
                last = jnp.sum(sth, axis=0, keepdims=True) + jnp.exp(clast[:, hh:hh + 1]) * hsum
                dcol = jnp.sum(m, axis=1, keepdims=True) + t1h - sth
                dcum_col = jnp.where(lane == hh, dcol + jnp.where(rowi == Q - 1, last, 0.0), dcum_col)
                sq_col = jnp.where(lane == hh, sqh, sq_col)
            dcm = dcm + _dot(dye, hin)
            dbm = dbm + _dot(xp * sdec, dhp)
            decrow = jnp.where(rlo, jnp.exp(cl0), jnp.exp(cl1))
            dh_ref[cols, :] = dhp * decrow + _dot(dye, Cm, "tn")
            dx_ref[:, cols] = dxp
            dd_ref[:, cols] = jnp.broadcast_to(jnp.sum(dyp * xp, axis=0, keepdims=True), (SUBLANES, LANES))
        dc_ref[...] = dcm + _dot(ds_g, Bm)
        db_ref[...] = dbm + _dot(ds_g, Cm, "tn")
        li = lax.broadcasted_iota(jnp.int32, (Q, Q), 0)
        si = lax.broadcasted_iota(jnp.int32, (Q, Q), 1)
        ddac_ref[...] = jnp.dot((li <= si).astype(F32), dcum_col, precision=HI, preferred_element_type=F32)
        ddar_ref[...] = jnp.dot(dcum_row, tril.astype(F32), precision=HI, preferred_element_type=F32)
        sq_ref[...] = sq_col
        cms_ref[...] = cms_row

    smem = pl.BlockSpec(memory_space=pltpu.SMEM)
    return pl.pallas_call(
        body, name=name, grid=(N_GROUPS, nc),
        in_specs=[smem, x_spec, b_spec, c_spec, colm, rowm, colv, rowv, colv, rowv, st_spec, x_spec],
        out_specs=(x_spec, bo_spec, bo_spec, colm, rowm, colm, rowm, dd_spec),
        out_shape=(_sds((T, D)), _sds((T, D // 2)), _sds((T, D // 2)), _sds((N_GROUPS, T, LANES)), _sds((N_GROUPS, SUBLANES, T)),
                   _sds((N_GROUPS, T, LANES)), _sds((N_GROUPS, SUBLANES, T)), _sds((nc, N_GROUPS, SUBLANES, 2 * LANES))),
        scratch_shapes=[pltpu.VMEM((2 * LANES, N_STATE), F32)],
        compiler_params=_cparams(("parallel", "arbitrary")))(dskip, xc, xc, xc, raw_col, raw_row, bias_col, bias_row, a_col, a_row,
                                                            states, dy)


def _adam_math(wv, gv, mv, vv):
    c1 = 1.0 - ADAM_B1 ** ADAM_STEP
    c2 = 1.0 - ADAM_B2 ** ADAM_STEP
    mn = ADAM_B1 * mv + (1.0 - ADAM_B1) * gv
    vn = ADAM_B2 * vv + (1.0 - ADAM_B2) * (gv * gv)
    return -ADAM_LR * ((mn / c1) / (jnp.sqrt(vn / c2) + ADAM_EPS) + ADAM_WD * wv), mn, vn


def _adamw_layers(w, g, m, v, l0, Lg, bufs, name):
    L, As, Bs = w.shape
    tr = _tile(As, [], (256, 352, 128))
    has_bufs = bufs is not None

    def body(*refs):
        w_ref, g_ref, m_ref, v_ref = refs[:4]
        d_ref, mo_ref, vo_ref = refs[4 + 3 * has_bufs:]
        d_ref[...], mo_ref[...], vo_ref[...] = _adam_math(w_ref[...], g_ref[...], m_ref[...], v_ref[...])

    spec = pl.BlockSpec((None, tr, Bs), lambda l, i: (l + l0, i, 0))
    args = (w, g, m, v) + (tuple(bufs) if has_bufs else ())
    return pl.pallas_call(
        body, name=name, grid=(Lg, As // tr), in_specs=[spec] * 4 + [_ANY] * (3 * has_bufs), out_specs=(spec,) * 3,
        out_shape=(_sds((L, As, Bs)),) * 3, input_output_aliases={4: 0, 5: 1, 6: 2} if has_bufs else {},
        compiler_params=_cparams(("parallel", "parallel")))(*args)


def _adamw(w, g, m, v, name):
    shape = w.shape
    cols = shape[-1]
    w2, g2, m2, v2 = (t.reshape(-1, cols) for t in (w, g, m, v))
    rows = w2.shape[0]
    tr = 256 if (rows % 256 == 0 and rows > 256) else rows
    c1 = 1.0 - ADAM_B1 ** ADAM_STEP
    c2 = 1.0 - ADAM_B2 ** ADAM_STEP

    def body(w_ref, g_ref, m_ref, v_ref, d_ref, mo_ref, vo_ref):
        gv = g_ref[...]
        mn = ADAM_B1 * m_ref[...] + (1.0 - ADAM_B1) * gv
        vn = ADAM_B2 * v_ref[...] + (1.0 - ADAM_B2) * (gv * gv)
        d_ref[...] = -ADAM_LR * ((mn / c1) / (jnp.sqrt(vn / c2) + ADAM_EPS) + ADAM_WD * w_ref[...])
        mo_ref[...] = mn
        vo_ref[...] = vn

    spec = pl.BlockSpec((tr, cols), lambda i: (i, 0))
    out = pl.pallas_call(body, name=name, grid=(rows // tr,), in_specs=[spec] * 4, out_specs=(spec,) * 3,
                         out_shape=(_sds((rows, cols)),) * 3, compiler_params=_cparams(("parallel",)))(w2, g2, m2, v2)
    return tuple(o.reshape(shape) for o in out)


def _place():
    x, y, c = lax.axis_index("x"), lax.axis_index("y"), lax.axis_index("c")
    chips = [(1 - x, y), (x, 1 - y), (1 - x, 1 - y)]
    return x, y, c, chips


_ANY = pl.BlockSpec(memory_space=pl.ANY)


TENSORS = (("e_w_in", "row", 2, 4096, 1284, 1024), ("e_w_out", "row", 2, 2048, 1024, 512), ("o_w_in", "col", 2, 1024, 3072, 768),
           ("o_w_out", "row", 2, 1024, 1024, 256), ("f_w_up", "col", 4, 1024, 5632, 1408), ("f_w_down", "row", 4, 2816, 1024, 704),
           ("ple_w_proj", "col", 4, 256, 1024, 256), ("ple_w_gate", "row", 4, 1024, 1024, 256))
MIX, FFN = "mix", "ffn"
W_GROUPS = (((0, MIX),), ((0, FFN),), ((1, MIX), (1, FFN)), ((2, MIX), (2, FFN), (3, MIX), (3, FFN)))
G_GROUPS = (((3, FFN), (3, MIX), (2, FFN), (2, MIX), (1, FFN), (1, MIX)), ((0, FFN),), ((0, MIX),))


def _tensor_layer(name, layer):
    if name.startswith("e_"):
        return layer // 2 if layer % 2 == 0 else None
    if name.startswith("o_"):
        return layer // 2 if layer % 2 == 1 else None
    return layer


def _part(name):
    return MIX if name.startswith(("e_", "o_")) else FFN


def _group_items(members):
    items = []
    for name, kind, L, A, B, n in TENSORS:
        tls = sorted(t for t in (_tensor_layer(name, l) for l, part in members if part == _part(name)) if t is not None)
        if tls:
            assert tls == list(range(tls[0], tls[0] + len(tls)))
            items.append((name, kind, len(tls), A, B, n, tls[0]))
    return items


def _hwin(ref, it, k, h):
    name, kind, Lg, A, B, n, l0 = it
    if kind == "row":
        return ref.at[:, pl.ds(pl.multiple_of(k * n + h * (n // 2), 16), n // 2), :]
    return ref.at[:, pl.ds(pl.multiple_of(h * (A // 2), 16), A // 2), pl.ds(pl.multiple_of(k * n, LANES), n)]


def _shard_dims(kind, A, B, n):
    return (n, B) if kind == "row" else (A, n)


def _cast_into(w, it, me):
    name, kind, Lg, A, B, n, l0 = it
    As, Bs = _shard_dims(kind, A, B, n)

    def body(me_ref, w_ref, o_ref):
        o_ref[...] = w_ref[...].astype(BF16)

    omap = (lambda l, m: (l, m[0], 0)) if kind == "row" else (lambda l, m: (l, 0, m[0]))
    grid_spec = pltpu.PrefetchScalarGridSpec(
        num_scalar_prefetch=1, grid=(Lg,), in_specs=[pl.BlockSpec((None, As, Bs), lambda l, m: (l + l0, 0, 0))],
        out_specs=pl.BlockSpec((None, As, Bs), omap))
    return pl.pallas_call(body, name=f"cast_{name}_{l0}", grid_spec=grid_spec, out_shape=_sds((Lg, A, B), BF16),
                          compiler_params=_cparams(("parallel",)))(me, w.reshape(-1, As, Bs))


_HBM = pl.BlockSpec(memory_space=pltpu.HBM)
_SEM = pl.BlockSpec(memory_space=pltpu.SEMAPHORE)
_EFFECT = pltpu.SideEffectType.DATAFLOW_SIDE_EFFECTING


def _hbm(a):
    return pltpu.with_memory_space_constraint(a, pltpu.HBM)


def _split_start(thru, n_copies, issue, name, after=None):
    N = len(thru)
    has_after = after is not None

    def body(*refs):
        outs = refs[N + has_after:2 * N + has_after]
        send_sems, recv_sems, token = refs[2 * N + has_after:]
        for cp in issue(outs, send_sems, recv_sems):
            cp.start()
        token[...] = jnp.zeros_like(token)

    out = pl.pallas_call(
        body, name=name, in_specs=[_HBM] * N + ([_ANY] if has_after else []),
        out_specs=(_HBM,) * N + (_SEM, _SEM, pl.BlockSpec(memory_space=pltpu.VMEM)),
        out_shape=tuple(pltpu.HBM(a.shape, a.dtype) for a in thru)
        + (pltpu.SemaphoreType.DMA((n_copies,)), pltpu.SemaphoreType.DMA((n_copies,)), _sds((SUBLANES, LANES))),
        input_output_aliases={t: t for t in range(N)},
        compiler_params=pltpu.CompilerParams(has_side_effects=_EFFECT))(*[_hbm(a) for a in thru], *([after] if has_after else []))
    return list(out[:N]), out[N], out[N + 1], out[N + 2]


def _split_wait(thru, send_sems, recv_sems, after, waits, name):
    N = len(thru)
    after = list(after) if isinstance(after, (list, tuple)) else [after]

    def body(*refs):
        ins = refs[:N]
        for cp, side in waits(ins, refs[N], refs[N + 1]):
            if side == "send":
                cp.wait_send()
            else:
                cp.wait_recv()

    out = pl.pallas_call(
        body, name=name, in_specs=[_HBM] * N + [_SEM, _SEM] + [_ANY] * len(after), out_specs=(_HBM,) * N,
        out_shape=tuple(pltpu.HBM(a.shape, a.dtype) for a in thru), input_output_aliases={t: t for t in range(N)},
        compiler_params=pltpu.CompilerParams(has_side_effects=_EFFECT))(*thru, send_sems, recv_sems, *after)
    return list(out)


def _rcopy(send_sems, recv_sems, k, src, dst, to):
    return pltpu.make_async_remote_copy(src_ref=src, dst_ref=dst, send_sem=send_sems.at[k], recv_sem=recv_sems.at[k],
                                        device_id=to, device_id_type=MESH)


def _gather_copies(items, refs, send_sems, recv_sems, what):
    x, y, c, chips = _place()
    me = 2 * x + y
    out = []
    for t, it in enumerate(items):
        mine = _hwin(refs[t], it, me, c)
        for j, (px, py) in enumerate(chips):
            if what == "start":
                out.append(_rcopy(send_sems, recv_sems, 3 * t + j, mine, mine, (px, py, c)))
            else:
                slot = _hwin(refs[t], it, 2 * px + py, c)
                out.append((_rcopy(send_sems, recv_sems, 3 * t + j, mine, mine, (px, py, c)), "send"))
                out.append((_rcopy(send_sems, recv_sems, 3 * t + j, slot, slot, (px, py, c)), "recv"))
    return out


def _gather_start(fulls, items, name, after=None):
    return _split_start(fulls, 3 * len(items), functools.partial(_gather_copies, items, what="start"), name, after)


def _gather_wait(fulls, send_sems, recv_sems, after, items, name):
    return _split_wait(fulls, send_sems, recv_sems, after, functools.partial(_gather_copies, items, what="wait"), name)


def _gather_fwd(fulls, items, name, ws=None):
    N = len(fulls)
    has_ws = ws is not None

    def body(*refs):
        outs = refs[N + has_ws:2 * N + has_ws]
        rest = refs[2 * N + has_ws:]
        x, y, c, chips = _place()
        me = 2 * x + y
        sib = (x, y, 1 - c)
        if has_ws:
            ws_ref = refs[N]
            WS_ref, send_sems, recv_sems, lsem = rest
            loc = pltpu.make_async_copy(ws_ref, WS_ref.at[me], lsem)
            loc.start()
        else:
            send_sems, recv_sems = rest
        rc = functools.partial(_rcopy, send_sems, recv_sems)
        cps = []
        for t, it in enumerate(items):
            for j, (px, py) in enumerate(chips):
                slot = _hwin(outs[t], it, 2 * px + py, c)
                cps.append(rc(3 * t + j, slot, slot, sib))
        if has_ws:
            cps += [rc(3 * N + j, ws_ref, WS_ref.at[me], (*chip, c)) for j, chip in enumerate(chips)]
        for cp in cps:
            cp.start()
        for t, it in enumerate(items):
            for j, (px, py) in enumerate(chips):
                oslot = _hwin(outs[t], it, 2 * px + py, 1 - c)
                rc(3 * t + j, oslot, oslot, sib).wait_recv()
        if has_ws:
            for j, (px, py) in enumerate(chips):
                sslot = WS_ref.at[2 * px + py]
                rc(3 * N + j, sslot, sslot, sib).wait_recv()
        for cp in cps:
            cp.wait_send()
        if has_ws:
            loc.wait()

    ns = 3 * N + (3 if has_ws else 0)
    out_shape = tuple(_sds(f.shape, f.dtype) for f in fulls)
    scratch = [pltpu.SemaphoreType.DMA((ns,)), pltpu.SemaphoreType.DMA((ns,))]
    args = list(fulls)
    if has_ws:
        out_shape += (_sds((4,) + ws.shape, ws.dtype),)
        scratch.append(pltpu.SemaphoreType.DMA(()))
        args.append(ws)
    out = pl.pallas_call(
        body, name=name, in_specs=[_ANY] * len(args), out_specs=(_ANY,) * len(out_shape), out_shape=out_shape,
        input_output_aliases={t: t for t in range(N)}, scratch_shapes=scratch,
        compiler_params=pltpu.CompilerParams(has_side_effects=True))(*args)
    return (list(out[:N]), out[N]) if has_ws else (list(out), None)


def _half_shape(it):
    name, kind, Lg, A, B, n, l0 = it
    return (Lg, 4, n // 2, B) if kind == "row" else (Lg, A // 2, B)


def _piece_shape(it):
    name, kind, Lg, A, B, n, l0 = it
    return (Lg, n // 2, B) if kind == "row" else (Lg, A // 2, n)


def _swap_grads(gs, items, name):
    N = len(gs)

    def body(*refs):
        g_refs, o_refs = refs[:N], refs[N:2 * N]
        send_sems, recv_sems = refs[2 * N:]
        x, y, c, _ = _place()
        sib = (x, y, 1 - c)
        cps = []
        for t, it in enumerate(items):
            name_, kind, Lg, A, B, n, l0 = it
            if kind == "row":
                for k in range(4):
                    cps.append(_rcopy(send_sems, recv_sems, 4 * t + k, _hwin(g_refs[t], it, k, 1 - c), o_refs[t].at[:, k], sib))
            else:
                src = g_refs[t].at[:, pl.ds(pl.multiple_of((1 - c) * (A // 2), 16), A // 2), :]
                cps.append(_rcopy(send_sems, recv_sems, 4 * t, src, o_refs[t], sib))
        for cp in cps:
            cp.start()
        for cp in cps:
            cp.wait()

    return pl.pallas_call(
        body, name=name, in_specs=[_ANY] * N, out_specs=(_ANY,) * N, out_shape=tuple(_sds(_half_shape(it)) for it in items),
        scratch_shapes=[pltpu.SemaphoreType.DMA((4 * N,)), pltpu.SemaphoreType.DMA((4 * N,))],
        compiler_params=pltpu.CompilerParams(has_side_effects=True))(*gs)


def _add_half(g, ra, it, cvec):
    name, kind, Lg, A, B, n, l0 = it
    if kind == "row":
        blk = (None, n // 2, B)
        grid = (Lg, 4)
        g_spec = pl.BlockSpec(blk, lambda l, k, cr: (l, 2 * k + cr[0], 0))
        h_spec = pl.BlockSpec((None, None, n // 2, B), lambda l, k, cr: (l, k, 0, 0))
    else:
        tr = _tile(A // 2, [], (256, 128))
        nb = (A // 2) // tr
        grid = (Lg, nb)
        g_spec = pl.BlockSpec((None, tr, B), lambda l, i, cr: (l, cr[0] * nb + i, 0))
        h_spec = pl.BlockSpec((None, tr, B), lambda l, i, cr: (l, i, 0))

    def body(c_ref, g_ref, r_ref, o_ref):
        o_ref[...] = (g_ref[...] + r_ref[...]).astype(BF16)

    grid_spec = pltpu.PrefetchScalarGridSpec(num_scalar_prefetch=1, grid=grid, in_specs=[g_spec, h_spec], out_specs=h_spec)
    return pl.pallas_call(body, name=f"addhalf_{name}_{l0}", grid_spec=grid_spec, out_shape=_sds(_half_shape(it), BF16),
                          compiler_params=_cparams(("parallel", "parallel")))(cvec, g, ra)


def _scatter_copies(items, refs, send_sems, recv_sems, what):
    N = len(items)
    x, y, c, chips = _place()
    out = []
    for t, it in enumerate(items):
        name, kind, Lg, A, B, n, l0 = it
        for j, (px, py) in enumerate(chips):
            k = 2 * px + py
            src = refs[t].at[:, k] if kind == "row" else refs[t].at[:, :, pl.ds(pl.multiple_of(k * n, LANES), n)]
            cp = _rcopy(send_sems, recv_sems, 3 * t + j, src, refs[N + t].at[j], (px, py, c))
            if what == "start":
                out.append(cp)
            else:
                out += [(cp, "send"), (cp, "recv")]
    return out


def _scatter_start(ps, items, name):
    lands = [lax.empty((3,) + _piece_shape(it), BF16) for it in items]
    return _split_start(list(ps) + lands, 3 * len(items), functools.partial(_scatter_copies, items, what="start"), name)


def _scatter_wait(thru, send_sems, recv_sems, after, items, name):
    return _split_wait(thru, send_sems, recv_sems, after, functools.partial(_scatter_copies, items, what="wait"), name)


def _sum_own(p, rc, it, mevec, buf):
    name, kind, Lg, A, B, n, l0 = it
    As, Bs = _shard_dims(kind, A, B, n)
    L = [s[2] for s in TENSORS if s[0] == name][0]
    hb = (As // 2, Bs)
    has_buf = buf is not None

    def body(*refs):
        p_ref, r0, r1, r2 = refs[1:5]
        o_ref = refs[5 + has_buf]
        o_ref[...] = ((p_ref[...].astype(F32) + r0[...].astype(F32)) + r1[...].astype(F32)) + r2[...].astype(F32)

    if kind == "row":
        p_spec = pl.BlockSpec((None, None) + hb, lambda l, m: (l, m[0], 0, 0))
    else:
        p_spec = pl.BlockSpec((None,) + hb, lambda l, m: (l, 0, m[0]))
    r_specs = [pl.BlockSpec((None, None) + hb, functools.partial(lambda l, m, j: (j, l, 0, 0), j=j)) for j in range(3)]
    in_specs = [p_spec] + r_specs + ([_ANY] if has_buf else [])
    grid_spec = pltpu.PrefetchScalarGridSpec(num_scalar_prefetch=1, grid=(Lg,), in_specs=in_specs,
                                             out_specs=pl.BlockSpec((None,) + hb, lambda l, m: (l + l0, m[1], 0)))
    args = (mevec, p, rc, rc, rc) + ((buf,) if has_buf else ())
    return pl.pallas_call(body, name=f"sumown_{name}_{l0}", grid_spec=grid_spec, out_shape=_sds((L, As, Bs)),
                          input_output_aliases={5: 0} if has_buf else {}, compiler_params=_cparams(("parallel",)))(*args)


def _join_halves(rs, items, name):
    N = len(rs)

    def body(*refs):
        outs = refs[N:2 * N]
        send_sems, recv_sems = refs[2 * N:]
        x, y, c, _ = _place()
        sib = (x, y, 1 - c)

        def half(t, h):
            name_, kind, Lg, A, B, n, l0 = items[t]
            hr = _shard_dims(kind, A, B, n)[0] // 2
            return outs[t].at[pl.ds(l0, Lg), pl.ds(pl.multiple_of(h * hr, SUBLANES), hr), :]

        cps = [_rcopy(send_sems, recv_sems, t, half(t, c), half(t, c), sib) for t in range(N)]
        for cp in cps:
            cp.start()
        for t in range(N):
            _rcopy(send_sems, recv_sems, t, half(t, 1 - c), half(t, 1 - c), sib).wait_recv()
        for cp in cps:
            cp.wait_send()

    return list(pl.pallas_call(
        body, name=name, in_specs=[_ANY] * N, out_specs=(_ANY,) * N, out_shape=tuple(_sds(r.shape, r.dtype) for r in rs),
        input_output_aliases={t: t for t in range(N)},
        scratch_shapes=[pltpu.SemaphoreType.DMA((N,)), pltpu.SemaphoreType.DMA((N,))],
        compiler_params=pltpu.CompilerParams(has_side_effects=True))(*rs))


def _allgather_small(v):
    m_per, n = v.shape

    def body(x_ref, out_ref, send_sems, recv_sems, local_sem):
        x, y, c, chips = _place()
        me, sibling = (x, y, c), (x, y, 1 - c)

        def rows(px, py, pc):
            return out_ref.at[pl.ds(pl.multiple_of((4 * px + 2 * py + pc) * m_per, SUBLANES), m_per), :]

        def copy(k, block, to, src=None):
            return pltpu.make_async_remote_copy(src_ref=rows(*block) if src is None else src, dst_ref=rows(*block),
                                                send_sem=send_sems.at[k], recv_sem=recv_sems.at[k], device_id=to, device_id_type=MESH)

        mine = pltpu.make_async_copy(x_ref, rows(*me), local_sem)
        mine.start()
        first = [copy(0, me, sibling, src=x_ref)]
        first += [copy(1 + j, me, (*chip, c), src=x_ref) for j, chip in enumerate(chips)]
        for cp in first:
            cp.start()
        passed = [copy(4 + j, (*chip, c), sibling) for j, chip in enumerate(chips)]
        for j, chip in enumerate(chips):
            copy(1 + j, (*chip, c), me).wait_recv()
            passed[j].start()
        copy(0, sibling, me).wait_recv()
        for j, chip in enumerate(chips):
            copy(4 + j, (*chip, 1 - c), me).wait_recv()
        for cp in first + passed:
            cp.wait_send()
        mine.wait()

    vm = pl.BlockSpec(memory_space=pltpu.VMEM)
    return pl.pallas_call(body, name="allgather_small", in_specs=[vm], out_specs=vm, out_shape=_sds((8 * m_per, n)),
                          scratch_shapes=[pltpu.SemaphoreType.DMA((7,)), pltpu.SemaphoreType.DMA((7,)), pltpu.SemaphoreType.DMA(())],
                          compiler_params=pltpu.CompilerParams(has_side_effects=True, vmem_limit_bytes=VMEM_LIMIT))(v)


def _sum8(v, m_per):
    def body(v_ref, o_ref):
        acc = v_ref[0:m_per, :]
        for k in range(1, 8):
            acc = acc + v_ref[k * m_per:(k + 1) * m_per, :]
        o_ref[...] = acc

    return pl.pallas_call(body, name="small_sum_devices", out_shape=_sds((m_per, v.shape[1])),
                          compiler_params=pltpu.CompilerParams(vmem_limit_bytes=VMEM_LIMIT))(v)


SMALL_SHARDED = (("e_conv_a_w", 2), ("e_conv_b_w", 2), ("o_conv_w", 2), ("f_conv_w", 2), ("ln_g", 2), ("ln_b", 2))
SMALL_REPL = ("e_conv_a_b", "e_ln_a_g", "e_ln_a_b", "e_conv_b_b", "e_dt_bias", "e_a_log", "e_d_skip", "e_norm_b_g", "f_conv_b")

WEIGHT_ORDER = ('e_w_in', 'e_conv_a_w', 'e_conv_a_b', 'e_ln_a_g', 'e_ln_a_b', 'e_conv_b_w', 'e_conv_b_b', 'e_dt_bias', 'e_a_log',
                'e_d_skip', 'e_norm_b_g', 'e_w_out', 'o_w_in', 'o_conv_w', 'o_w_out', 'f_w_up', 'f_conv_w', 'f_conv_b', 'f_w_down',
                'ple_w_proj', 'ple_w_gate', 'ln_g', 'ln_b')


def _pack_rows(parts, width, total_rows, dtype):
    flat = jnp.concatenate([p.reshape(-1).astype(dtype) for p in parts])
    flat = jnp.pad(flat, (0, total_rows * width - flat.shape[0]))
    return flat.reshape(total_rows, width)


def _unpack_rows(buf, shapes):
    flat = buf.reshape(-1)
    out, pos = [], 0
    for s in shapes:
        n = math.prod(s)
        out.append(flat[pos:pos + n].reshape(s))
        pos += n
    return out


def _small_rows(shapes):
    n = sum(math.prod(s) for s in shapes)
    return -(-n // (LANES * SUBLANES)) * SUBLANES


E_PAD = 5248
SEG_A, SEG_Z, SEG_X, SEG_DT = (0, 2 * D), (2 * D, D), (3 * D, 2 * D), (5 * D, LANES)
G_SHAPES = {"e_w_in": (2, D, E_PAD), "e_w_out": (2, 2 * D, D), "o_w_in": (2, D, 3 * D), "o_w_out": (2, D, D),
            "f_w_up": (4, D, 2 * D_FF), "f_w_down": (4, D_FF, D), "ple_w_proj": (4, PLE, D), "ple_w_gate": (4, D, D)}


def _padcols(w, width):
    return jnp.pad(w, ((0, 0), (0, width - w.shape[1])))


def _fold_rows(dw, K):
    return dw.reshape(K, SUBLANES, dw.shape[-1]).sum(1)


class GradBuffers(dict):
    def __init__(self):
        super().__init__()
        self.where = {}
        for gi, layers in enumerate(G_GROUPS):
            for name, kind, Lg, A, B, n, l0 in _group_items(layers):
                for k in range(Lg):
                    self.where[(name, l0 + k)] = (gi, k, Lg)
        self.current = {}

    def into(self, name, layer, r0=0, c0=0):
        gi, k, Lg = self.where[(name, layer)]
        self.current[name] = (name, gi)
        return (self.get((name, gi)), (Lg,) + G_SHAPES[name][1:], (k,), r0, c0)

    def __setitem__(self, name, value):
        super().__setitem__(self.current[name], value)


def _local_step(x, p, target, W, comm=None):
    T = x.shape[0]
    xb = x
    saved = []
    xc_f = x
    for i in range(DEPTH):
        j = i // 2
        L = {}
        L["x"], L["xb"] = xc_f, xb
        tok = comm.part_starts(i, MIX, xb) if comm is not None else None
        if i % 2 == 0:
            def w_in(seg, c0=0, cols=None, j=j):
                return V(W["e_w_in"], (j,), c0=seg[0] + c0, cols=seg[1] if cols is None else cols)

            ua = _mm(xb, w_in(SEG_A), "nn", f"l{i}_in_a", BF16, after=tok)
            z = _mm(xb, w_in(SEG_Z), "nn", f"l{i}_in_z")
            xu = _mm(xb, w_in(SEG_X), "nn", f"l{i}_in_xbc", BF16)
            udt = _mm(xb, w_in(SEG_DT), "nn", f"l{i}_in_dt")
            ac = _conv_a_fwd(ua, W["e_conv_a_w"][j], W["e_conv_a_b"][j][None], f"l{i}_conv_a")
            ya = _ln_silu_fwd(ac, W["e_ln_a_g"][j][None], W["e_ln_a_b"][j][None], f"l{i}_ln_a")
            xc = _conv_b_fwd(xu, W["e_conv_b_w"][j], W["e_conv_b_b"][j][None], f"l{i}_conv_b")
            sm = _ssd_small_inputs(udt[:, :N_HEADS], W["e_dt_bias"][j], W["e_a_log"][j])
            y, states = _ssd_fwd(xc, *sm, W["e_d_skip"][j], f"l{i}_ssd")
            yb = _gate_rms_fwd(y, z, W["e_norm_b_g"][j][None], f"l{i}_gate_rms")
            mix = _mm(ya, V(W["e_w_out"], (j,), rows=D), "nn", f"l{i}_out_a")
            mix = _mm(yb, V(W["e_w_out"], (j,), r0=D), "nn", f"l{i}_out_b", add=mix)
            L.update(ua=ua, z=z, xu=xu, udt=udt, ac=ac, ya=ya, xc=xc, sm=sm, y=y, states=states, yb=yb, w_in=w_in)
        else:
            uo = _mm(xb, V(W["o_w_in"], (j,)), "nn", f"l{i}_in", BF16, after=tok)
            sc = _conv_c_fwd(uo, W["o_conv_w"][j], f"l{i}_conv_c")
            mix = _mm(sc, V(W["o_w_out"], (j,)), "nn", f"l{i}_out")
            L.update(uo=uo, sc=sc)
        h1, x1, x1b = _res_ln_fwd(xc_f, [mix], None, W["ln_g"][i, 0][None], W["ln_b"][i, 0][None], f"l{i}_ln1")
        tok = comm.part_starts(i, FFN, x1b) if comm is not None else None
        up = _mm(x1b, V(W["f_w_up"], (i,)), "nn", f"l{i}_ffn_up", BF16, after=tok)
        act = _conv_f_fwd(up, W["f_conv_w"][i], W["f_conv_b"][i][None], f"l{i}_conv_f")
        ffn = _mm(act, V(W["f_w_down"], (i,)), "nn", f"l{i}_ffn_down")
        pv = V(p, (i, 0))
        pp = _mm(pv, V(W["ple_w_proj"], (i,)), "nn", f"l{i}_ple_proj")
        gl = _mm(x1b, V(W["ple_w_gate"], (i,)), "nn", f"l{i}_ple_gate")
        h2, x2, x2b = _res_ln_fwd(x1, [ffn], (pp, gl), W["ln_g"][i, 1][None], W["ln_b"][i, 1][None], f"l{i}_ln2")
        L.update(h1=h1, x1=x1, x1b=x1b, up=up, act=act, pv=pv, pp=pp, gl=gl, h2=h2)
        saved.append(L)
        xc_f, xb = x2, x2b

    sq, dx = _loss_head(xc_f, target, "loss_head")

    GB = GradBuffers()
    into = GB.into
    tok = None

    G = {n: [None] * (DEPTH if n.startswith(("f_", "ln_")) else DEPTH // 2) for n in WEIGHT_ORDER if n not in G_SHAPES}
    for i in reversed(range(DEPTH)):
        j = i // 2
        L = saved[i]
        dh2, dh2b, dg2, db2, dpp, dgl = _res_ln_bwd(dx, L["h2"], W["ln_g"][i, 1][None], (L["pp"], L["gl"]), f"l{i}_ln2_bwd")
        GB["f_w_down"] = _mm(L["act"], dh2b, "tn", f"l{i}_dw_down", dst=into("f_w_down", i))
        dact = _mm(dh2b, V(W["f_w_down"], (i,)), "nt", f"l{i}_dact", BF16, after=tok)
        du1, du2, dw1, dw2, dbf1, dbf2 = _conv_f_bwd(L["up"], W["f_conv_w"][i], W["f_conv_b"][i][None], dact, f"l{i}_conv_f_bwd")
        G["f_conv_w"][i] = jnp.concatenate([_fold_rows(dw1, CONV_F), _fold_rows(dw2, CONV_F)], axis=1)
        G["f_conv_b"][i] = jnp.concatenate([dbf1.sum(0), dbf2.sum(0)])
        GB["f_w_up"] = _mm(L["x1b"], du1, "tn", f"l{i}_dw_up1", dst=into("f_w_up", i))
        GB["f_w_up"] = _mm(L["x1b"], du2, "tn", f"l{i}_dw_up2", dst=into("f_w_up", i, c0=D_FF))
        GB["ple_w_proj"] = _mm(L["pv"], dpp, "tn", f"l{i}_dw_proj", dst=into("ple_w_proj", i))
        GB["ple_w_gate"] = _mm(L["x1b"], dgl, "tn", f"l{i}_dw_gate", dst=into("ple_w_gate", i))
        tok = comm.part_grads_done(i, FFN, GB) if comm is not None else None
        dx1 = _mm(du1, V(W["f_w_up"], (i,), cols=D_FF), "nt", f"l{i}_dx1_a", add=dh2, add_scale=ALPHA, after=tok)
        dx1 = _mm(du2, V(W["f_w_up"], (i,), c0=D_FF), "nt", f"l{i}_dx1_b", add=dx1)
        dx1 = _mm(dgl, V(W["ple_w_gate"], (i,)), "nt", f"l{i}_dx1_c", add=dx1)
        dh1, dh1b, dg1, db1 = _res_ln_bwd(dx1, L["h1"], W["ln_g"][i, 0][None], None, f"l{i}_ln1_bwd")
        G["ln_g"][i] = jnp.concatenate([dg1, dg2], axis=0)
        G["ln_b"][i] = jnp.concatenate([db1, db2], axis=0)
        if i % 2 == 0:
            GB["e_w_out"] = _mm(L["ya"], dh1b, "tn", f"l{i}_dw_out_a", dst=into("e_w_out", j))
            GB["e_w_out"] = _mm(L["yb"], dh1b, "tn", f"l{i}_dw_out_b", dst=into("e_w_out", j, r0=D))
            dya = _mm(dh1b, V(W["e_w_out"], (j,), rows=D), "nt", f"l{i}_dya")
            dyb = _mm(dh1b, V(W["e_w_out"], (j,), r0=D), "nt", f"l{i}_dyb")
            dac, dga, dba = _ln_silu_bwd(L["ac"], dya, W["e_ln_a_g"][j][None], W["e_ln_a_b"][j][None], f"l{i}_ln_a_bwd")
            G["e_ln_a_g"][j], G["e_ln_a_b"][j] = dga[0], dba[0]
            dal, dag, dwa, dbca = _conv_a_bwd(L["ua"], W["e_conv_a_w"][j], dac, f"l{i}_conv_a_bwd")
            G["e_conv_a_w"][j] = _fold_rows(dwa, CONV_A)
            G["e_conv_a_b"][j] = dbca.sum(0)
            dy, dz, dgn = _gate_rms_bwd(L["y"], L["z"], dyb, W["e_norm_b_g"][j][None], f"l{i}_gate_rms_bwd")
            G["e_norm_b_g"][j] = dgn[0]
            dxs, dbs, dcs, sq_col, cms_row, dda_col, dda_row, ddp = _ssd_bwd(L["xc"], *L["sm"], W["e_d_skip"][j], L["states"], dy,
                                                                             f"l{i}_ssd_bwd")
            draw, G["e_dt_bias"][j], G["e_a_log"][j] = _ssd_small_grads(L["udt"][:, :N_HEADS], W["e_dt_bias"][j], W["e_a_log"][j],
                                                                       sq_col, cms_row, dda_col, dda_row)
            G["e_d_skip"][j] = ddp[:, :, 0, :].sum(0).reshape(N_HEADS, HEAD_P).sum(1)
            dxu, dwb, dbcb = _conv_b_bwd(L["xu"], W["e_conv_b_w"][j], W["e_conv_b_b"][j][None], dxs, dbs, dcs, f"l{i}_conv_b_bwd")
            G["e_conv_b_w"][j] = _fold_rows(dwb, CONV_B)
            G["e_conv_b_b"][j] = dbcb.sum(0)
            dudt = _padcols(draw, LANES)
            w_in = L["w_in"]
            xb_l = L["xb"]
            for nm, dseg, c0 in (("al", dal, 0), ("ag", dag, D), ("z", dz, SEG_Z[0]), ("xbc", dxu, SEG_X[0]), ("dt", dudt, SEG_DT[0])):
                GB["e_w_in"] = _mm(xb_l, dseg, "tn", f"l{i}_dw_in_{nm}", dst=into("e_w_in", j, c0=c0))
            dx = _mm(dal, w_in(SEG_A, cols=D), "nt", f"l{i}_dx_al", add=dh1, add_scale=ALPHA)
            dx = _mm(dag, w_in(SEG_A, c0=D, cols=D), "nt", f"l{i}_dx_ag", add=dx)
            dx = _mm(dz, w_in(SEG_Z), "nt", f"l{i}_dx_z", add=dx)
            dx = _mm(dxu, w_in(SEG_X), "nt", f"l{i}_dx_xbc", add=dx)
            dx = _mm(dudt, w_in(SEG_DT), "nt", f"l{i}_dx_dt", add=dx)
        else:
            GB["o_w_out"] = _mm(L["sc"], dh1b, "tn", f"l{i}_dw_out", dst=into("o_w_out", j))
            dsc = _mm(dh1b, V(W["o_w_out"], (j,)), "nt", f"l{i}_dsc")
            dbg, dcg, dv, dwc = _conv_c_bwd(L["uo"], W["o_conv_w"][j], dsc, f"l{i}_conv_c_bwd")
            G["o_conv_w"][j] = _fold_rows(dwc, CONV_C)
            xb_l = L["xb"]
            dx = dh1
            for nm, dseg, c0, scale in (("bg", dbg, 0, ALPHA), ("cg", dcg, D, 1.0), ("v", dv, 2 * D, 1.0)):
                GB["o_w_in"] = _mm(xb_l, dseg, "tn", f"l{i}_dw_in_{nm}", dst=into("o_w_in", j, c0=c0))
                dx = _mm(dseg, V(W["o_w_in"], (j,), c0=c0, cols=D), "nt", f"l{i}_dx_{nm}", add=dx, add_scale=scale)
        tok = comm.part_grads_done(i, MIX, GB) if comm is not None else None
    grads = {n: jnp.stack(v) for n, v in G.items()}
    return sq, dx, GB, grads


def _ssd_small_inputs(raw, dt_bias, a_log):
    T = raw.shape[0]
    a = -jnp.exp(a_log)
    rg = raw.reshape(T, N_GROUPS, 4)
    raw_col = jnp.pad(jnp.transpose(rg, (1, 0, 2)), ((0, 0), (0, 0), (0, LANES - 4)))
    raw_row = jnp.pad(jnp.transpose(rg, (1, 2, 0)), ((0, 0), (0, SUBLANES - 4), (0, 0)))

    def colv(v):
        return jnp.pad(v.reshape(N_GROUPS, 1, 4), ((0, 0), (0, 0), (0, LANES - 4)))

    def rowv(v):
        return jnp.pad(v.reshape(N_GROUPS, 4, 1), ((0, 0), (0, SUBLANES - 4), (0, 0)))

    return raw_col, raw_row, colv(dt_bias), rowv(dt_bias), colv(a), rowv(a)


def _ssd_small_grads(raw, dt_bias, a_log, sq_col, cms_row, dda_col, dda_row):
    T = raw.shape[0]

    def join(col, row):
        c = jnp.transpose(col[:, :, :4], (1, 0, 2)).reshape(T, N_HEADS)
        r = jnp.transpose(row[:, :4, :], (2, 0, 1)).reshape(T, N_HEADS)
        return c + r

    a = -jnp.exp(a_log)
    pre = raw + dt_bias
    dt = jax.nn.softplus(pre)
    dda = join(dda_col, dda_row)
    ddt = join(sq_col, cms_row) + a * dda
    draw = ddt * jax.nn.sigmoid(pre)
    da = jnp.sum(dt * dda, axis=0)
    return draw, jnp.sum(draw, axis=0), da * a


def kernel(x, p, e_w_in, e_conv_a_w, e_conv_a_b, e_ln_a_g, e_ln_a_b, e_conv_b_w, e_conv_b_b, e_dt_bias, e_a_log, e_d_skip, e_norm_b_g, e_w_out, o_w_in, o_conv_w, o_w_out, f_w_up, f_conv_w, f_conv_b, f_w_down, ple_w_proj, ple_w_gate, ln_g, ln_b, loss_target, m_e_w_in, m_e_conv_a_w, m_e_conv_a_b, m_e_ln_a_g, m_e_ln_a_b, m_e_conv_b_w, m_e_conv_b_b, m_e_dt_bias, m_e_a_log, m_e_d_skip, m_e_norm_b_g, m_e_w_out, m_o_w_in, m_o_conv_w, m_o_w_out, m_f_w_up, m_f_conv_w, m_f_conv_b, m_f_w_down, m_ple_w_proj, m_ple_w_gate, m_ln_g, m_ln_b, v_e_w_in, v_e_conv_a_w, v_e_conv_a_b, v_e_ln_a_g, v_e_ln_a_b, v_e_conv_b_w, v_e_conv_b_b, v_e_dt_bias, v_e_a_log, v_e_d_skip, v_e_norm_b_g, v_e_w_out, v_o_w_in, v_o_conv_w, v_o_w_out, v_f_w_up, v_f_conv_w, v_f_conv_b, v_f_w_down, v_ple_w_proj, v_ple_w_gate, v_ln_g, v_ln_b):
    args = dict(locals())
    w_shard = {n: args[n] for n in WEIGHT_ORDER}
    m_shard = {n: args["m_" + n] for n in WEIGHT_ORDER}
    v_shard = {n: args["v_" + n] for n in WEIGHT_ORDER}
    xi, yi, ci = lax.axis_index("x"), lax.axis_index("y"), lax.axis_index("c")
    chip = 2 * xi + yi

    mevec = jnp.stack([chip, ci]).astype(jnp.int32)
    small_shapes = [w_shard[n].shape for n, _ in SMALL_SHARDED]
    sr = _small_rows(small_shapes)
    ws = _pack_rows([w_shard[n] for n, _ in SMALL_SHARDED], LANES, sr, F32)
    W = {n: w_shard[n] for n in SMALL_REPL}
    W.update({s[0]: Layers(s[2]) for s in TENSORS})
    w_items = [_group_items(layers) for layers in W_GROUPS]
    g_items = [_group_items(layers) for layers in G_GROUPS]

    def install(items, fulls):
        for it, f in zip(items, fulls):
            if it[0] == "e_w_in":
                f = jnp.transpose(f.reshape(it[2], 4, D, E_IN // 4), (0, 2, 1, 3)).reshape(it[2], D, E_IN)
                f = jnp.pad(f, ((0, 0), (0, 0), (0, E_PAD - E_IN)))
            W[it[0]].put(f, it[6])

    casts = [[_cast_into(w_shard[it[0]], it, mevec[:1]) for it in items] for items in w_items]
    fulls, ssem, rsem, _ = _gather_start(casts[0], w_items[0], "gather_start_0")
    fulls = _gather_wait(fulls, ssem, rsem, [c for grp in casts[1:] for c in grp], w_items[0], "gather_wait_0")
    fulls, WS = _gather_fwd(fulls, w_items[0], "gather_fwd_0", ws)
    install(w_items[0], fulls)
    parts_s = [_unpack_rows(WS[k], small_shapes) for k in range(4)]
    for idx, (n, ax) in enumerate(SMALL_SHARDED):
        W[n] = jnp.concatenate([parts_s[k][idx] for k in range(4)], axis=ax)

    class Comm:
        sent = {}
        started = {}
        tail = fulls[0]

        def start_next(self, gi):
            if gi >= len(w_items):
                return None
            self.started[gi] = _gather_start(casts[gi], w_items[gi], f"gather_start_{gi}", self.tail)
            return self.started[gi][3]

        def part_starts(self, layer, part, after):
            if (layer, part) == W_GROUPS[0][0]:
                return self.start_next(1)
            for gi in range(1, len(W_GROUPS)):
                if W_GROUPS[gi][0] == (layer, part):
                    fulls, ssem, rsem, _ = self.started[gi]
                    fulls = _gather_wait(fulls, ssem, rsem, after, w_items[gi], f"gather_wait_{gi}")
                    fulls, _ = _gather_fwd(fulls, w_items[gi], f"gather_fwd_{gi}")
                    install(w_items[gi], fulls)
                    self.tail = fulls[0]
                    return self.start_next(gi + 1)
            return None

        def part_grads_done(self, layer, part, GB):
            tok = None
            for gi, members in enumerate(G_GROUPS):
                if members[-1] == (layer, part):
                    items = g_items[gi]
                    gs = []
                    for it in items:
                        g = GB[(it[0], gi)]
                        if it[0] == "e_w_in":
                            g = jnp.transpose(g[:, :, :E_IN].reshape(it[2], D, 4, E_IN // 4), (0, 2, 1, 3)).reshape(it[2], 4 * D, E_IN // 4)
                        gs.append(g)
                    ras = _swap_grads(gs, items, f"swap_grads_{gi}")
                    ps = [_add_half(g, ra, it, mevec[1:]) for g, ra, it in zip(gs, ras, items)]
                    thru, ssem, rsem, tok = _scatter_start(ps, items, f"scatter_start_{gi}")
                    self.sent[gi] = (thru, ssem, rsem, tok)
            return tok

    comm = Comm()

    sq, dx, GB, G = _local_step(x[0], p, loss_target[0], W, comm)
    loss = lax.psum(0.5 * sq[0, 0] / D, ("x", "y", "c"))
    grad_x = dx[None]

    def shard_of(g, ax, k):
        n = g.shape[ax] // 4
        return lax.slice_in_dim(g, k * n, (k + 1) * n, axis=ax)

    reduced, updated = {}, {}
    after = comm.sent[len(g_items) - 1][3]
    for gi, items in enumerate(g_items):
        thru, ssem, rsem, _ = comm.sent[gi]
        thru = _scatter_wait(thru, ssem, rsem, after, items, f"scatter_wait_{gi}")
        ps, rcs = thru[:len(items)], thru[len(items):]
        rs = [_sum_own(pt, rc, it, mevec, reduced.get(it[0])) for pt, rc, it in zip(ps, rcs, items)]
        rs = _join_halves(rs, items, f"join_halves_{gi}")
        reduced.update({it[0]: r for it, r in zip(items, rs)})
        for it in items:
            n = it[0]
            updated[n] = _adamw_layers(w_shard[n], reduced[n], m_shard[n], v_shard[n], it[6], it[2], updated.get(n), f"adamw_{n}_{it[6]}")
        after = updated[items[0][0]][0]

    small_all = ([shard_of(G[n], ax, k) for k in range(4) for n, ax in SMALL_SHARDED] + [G[n] for n in SMALL_REPL])
    small_all_shapes = [t.shape for t in small_all]
    mr = _small_rows(small_all_shapes)
    sg = _sum8(_allgather_small(_pack_rows(small_all, LANES, mr, F32)), mr)
    sparts = _unpack_rows(sg, small_all_shapes)
    ns = len(SMALL_SHARDED)
    gsmall = {}
    for idx, (n, ax) in enumerate(SMALL_SHARDED):
        stacked = jnp.stack([sparts[k * ns + idx] for k in range(4)])
        gsmall[n] = lax.dynamic_index_in_dim(stacked, chip, axis=0, keepdims=False)
    for idx, n in enumerate(SMALL_REPL):
        gsmall[n] = sparts[4 * ns + idx]

    grads, deltas, new_m, new_v = [], [], [], []
    for n in WEIGHT_ORDER:
        if n in reduced:
            g, (d, mn, vn) = reduced[n], updated[n]
        else:
            g = gsmall[n]
            d, mn, vn = _adamw(w_shard[n], g, m_shard[n], v_shard[n], f"adamw_{n}")
        grads.append(g)
        deltas.append(d)
        new_m.append(mn)
        new_v.append(vn)
    return (loss, grad_x, *grads, *deltas, *new_m, *new_v)
```

```python
import functools
import math

import jax
import jax.numpy as jnp
from jax import lax
from jax.experimental import pallas as pl
from jax.experimental.pallas import tpu as pltpu

F32 = jnp.float32
BF16 = jnp.bfloat16
MESH = pl.DeviceIdType.MESH

DEPTH = 4
ALPHA = (2.0 * DEPTH) ** 0.25
LN_EPS = 1e-5
D = 1024
HEAD_P = 64
N_STATE = 128
N_HEADS = 16
N_GROUPS = 4
CONV_A, CONV_B, CONV_C, CONV_F = 31, 4, 3, 3
D_FF = 2816
PLE = 256
E_IN = 5136

ADAM_LR, ADAM_B1, ADAM_B2, ADAM_EPS, ADAM_WD, ADAM_STEP = 0.001, 0.9, 0.999, 1e-08, 0.01, 10

LANES = 128
SUBLANES = 8
VMEM_LIMIT = 56 * 1024 * 1024
SSD_Q = 128
CONV_R = 128
CONV_PAD = 32
ROW_T = 256


def _cparams(sem=None):
    return pltpu.CompilerParams(dimension_semantics=sem, vmem_limit_bytes=VMEM_LIMIT)


def _sig(v):
    return jax.nn.sigmoid(v)


_DIMS = {"nn": (((1,), (0,)), ((), ())), "nt": (((1,), (1,)), ((), ())), "tn": (((0,), (0,)), ((), ()))}


class Layers:
    def __init__(self, n_layers):
        self.where = [None] * n_layers

    def put(self, arr, l0):
        for k in range(arr.shape[0]):
            self.where[l0 + k] = (arr, k)


class V:
    def __init__(self, arr, lead=(), r0=0, c0=0, rows=None, cols=None):
        if isinstance(arr, Layers):
            arr, k = arr.where[lead[0]]
            lead = (k,) + tuple(lead[1:])
        self.arr, self.lead, self.r0, self.c0 = arr, tuple(lead), r0, c0
        R, C = arr.shape[-2:]
        self.rows = R - r0 if rows is None else rows
        self.cols = C - c0 if cols is None else cols

    def spec(self, br, bc, fn):
        assert self.r0 % br == 0 and self.c0 % bc == 0, (self.r0, self.c0, br, bc)
        ro, co, lead = self.r0 // br, self.c0 // bc, self.lead

        def index(i, j, k):
            r, c = fn(i, j, k)
            return lead + (r + ro, c + co)

        return pl.BlockSpec((None,) * len(lead) + (br, bc), index)


def _v(t):
    return t if isinstance(t, V) else V(t)


def _tile(n, offs, cands):
    for c in cands:
        if n % c == 0 and all(o % c == 0 for o in offs):
            return c
    raise ValueError((n, offs))


_TILES = (1024, 1408, 512, 256, 128)


def _mm(a, b, mode, name, out_dtype=F32, add=None, add_scale=1.0, dst=None, after=None):
    a, b = _v(a), _v(b)
    add = _v(add) if add is not None else None
    if mode == "nn":
        M, K, K2, N = a.rows, a.cols, b.rows, b.cols
        am, ak, bk, bn = a.r0, a.c0, b.r0, b.c0
    elif mode == "nt":
        M, K, N, K2 = a.rows, a.cols, b.rows, b.cols
        am, ak, bn, bk = a.r0, a.c0, b.r0, b.c0
    else:
        K, M, K2, N = a.rows, a.cols, b.rows, b.cols
        ak, am, bk, bn = a.r0, a.c0, b.r0, b.c0
    assert K == K2, (name, mode, M, K, K2, N)
    if dst is None:
        buf, full_shape, o_lead, o_r0, o_c0 = None, (M, N), (), 0, 0
    else:
        buf, full_shape, o_lead, o_r0, o_c0 = dst
    tm = _tile(M, [am, o_r0] + ([add.r0] if add else []), _TILES)
    tn = _tile(N, [bn, o_c0] + ([add.c0] if add else []), _TILES)
    tk = _tile(K, [ak, bk], _TILES)
    nk = K // tk
    has_add, has_buf, has_after = add is not None, buf is not None, after is not None

    def body(*refs):
        a_ref, b_ref = refs[0], refs[1]
        add_ref = refs[2] if has_add else None
        o_ref = refs[2 + has_add + has_buf + has_after]

        def finish(r):
            if has_add:
                r = r + add_scale * add_ref[...].astype(F32)
            o_ref[...] = r.astype(o_ref.dtype)

        part = lax.dot_general(a_ref[...].astype(BF16), b_ref[...].astype(BF16), _DIMS[mode], preferred_element_type=F32)
        if nk == 1:
            finish(part)
        else:
            acc_ref = refs[-1]
            k = pl.program_id(2)

            @pl.when(k == 0)
            def _():
                acc_ref[...] = part

            @pl.when(jnp.logical_and(k > 0, k < nk - 1))
            def _():
                acc_ref[...] += part

            @pl.when(k == nk - 1)
            def _():
                finish(acc_ref[...] + part)

    if mode == "tn":
        a_spec = a.spec(tk, tm, lambda i, j, k: (k, i))
    else:
        a_spec = a.spec(tm, tk, lambda i, j, k: (i, k))
    if mode == "nt":
        b_spec = b.spec(tn, tk, lambda i, j, k: (j, k))
    else:
        b_spec = b.spec(tk, tn, lambda i, j, k: (k, j))
    in_specs, args = [a_spec, b_spec], [a.arr, b.arr]
    if has_add:
        in_specs.append(add.spec(tm, tn, lambda i, j, k: (i, j)))
        args.append(add.arr)
    aliases = {}
    if has_buf:
        aliases = {len(args): 0}
        in_specs.append(pl.BlockSpec(memory_space=pl.ANY))
        args.append(buf)
        out_dtype = buf.dtype
    if has_after:
        in_specs.append(pl.BlockSpec(memory_space=pl.ANY))
        args.append(after)
    o_view = V(jax.ShapeDtypeStruct(full_shape, out_dtype), o_lead, o_r0, o_c0, M, N)
    return pl.pallas_call(
        body, name=name, grid=(M // tm, N // tn, nk), in_specs=in_specs, out_specs=o_view.spec(tm, tn, lambda i, j, k: (i, j)),
        out_shape=jax.ShapeDtypeStruct(full_shape, out_dtype), input_output_aliases=aliases,
        scratch_shapes=[pltpu.VMEM((tm, tn), F32)] if nk > 1 else [],
        compiler_params=_cparams(("parallel", "parallel", "arbitrary")))(*args)


def _mm_sum(pairs, mode, name, out_dtype=F32, add=None, add_scale=1.0, after=None):
    pairs = [(_v(a), _v(b)) for a, b in pairs]
    add = _v(add) if add is not None else None
    M = pairs[0][0].rows
    N = pairs[0][1].cols if mode == "nn" else pairs[0][1].rows
    b_offs = [(b.c0 if mode == "nn" else b.r0) for _, b in pairs]
    tm = _tile(M, [a.r0 for a, _ in pairs] + ([add.r0] if add else []), (512, 256, 128))
    tn = _tile(N, b_offs + ([add.c0] if add else []), (512, 256, 128))
    n_p, has_add, has_after = len(pairs), add is not None, after is not None

    def body(*refs):
        acc = None
        for i in range(n_p):
            part = lax.dot_general(refs[2 * i][...].astype(BF16), refs[2 * i + 1][...].astype(BF16), _DIMS[mode],
                                   preferred_element_type=F32)
            acc = part if acc is None else acc + part
        if has_add:
            acc = acc + add_scale * refs[2 * n_p][...].astype(F32)
        o_ref = refs[2 * n_p + has_add + has_after]
        o_ref[...] = acc.astype(o_ref.dtype)

    in_specs, args = [], []
    for a, b in pairs:
        K = a.cols
        assert K == (b.rows if mode == "nn" else b.cols), (name, K)
        in_specs.append(a.spec(tm, K, lambda i, j, k: (i, 0)))
        in_specs.append(b.spec(K, tn, lambda i, j, k: (0, j)) if mode == "nn" else b.spec(tn, K, lambda i, j, k: (j, 0)))
        args += [a.arr, b.arr]
    if has_add:
        in_specs.append(add.spec(tm, tn, lambda i, j, k: (i, j)))
        args.append(add.arr)
    if has_after:
        in_specs.append(pl.BlockSpec(memory_space=pl.ANY))
        args.append(after)
    return pl.pallas_call(
        body, name=name, grid=(M // tm, N // tn, 1), in_specs=in_specs,
        out_specs=pl.BlockSpec((tm, tn), lambda i, j, k: (i, j)), out_shape=jax.ShapeDtypeStruct((M, N), out_dtype),
        compiler_params=_cparams(("parallel", "parallel", "arbitrary")))(*args)


def _rows(T, width=D):
    return pl.BlockSpec((ROW_T, width), lambda i: (i, 0))


def _vec(width=D):
    return pl.BlockSpec((1, width), lambda i: (0, 0))


def _ln_stats(h):
    mu = jnp.mean(h, axis=-1, keepdims=True)
    hc = h - mu
    var = jnp.mean(hc * hc, axis=-1, keepdims=True)
    rstd = lax.rsqrt(var + LN_EPS)
    return hc * rstd, rstd


def _res_ln_fwd(x, adds, ple, g, b, name):
    T = x.shape[0]
    n_add = len(adds)
    has_ple = ple is not None

    def body(*refs):
        x_ref = refs[0]
        add_refs = refs[1:1 + n_add]
        pos = 1 + n_add
        if has_ple:
            pp_ref, gl_ref = refs[pos], refs[pos + 1]
            pos += 2
        g_ref, b_ref, h_ref, y_ref, yb_ref = refs[pos:pos + 5]
        h = ALPHA * x_ref[...]
        for r in add_refs:
            h = h + r[...]
        if has_ple:
            h = h + pp_ref[...] * _sig(gl_ref[...])
        xhat, _ = _ln_stats(h)
        y = xhat * g_ref[...] + b_ref[...]
        h_ref[...] = h
        y_ref[...] = y
        yb_ref[...] = y.astype(BF16)

    n_in = 1 + n_add + (2 if has_ple else 0)
    args = (x,) + tuple(adds) + (tuple(ple) if has_ple else ()) + (g, b)
    return pl.pallas_call(
        body, name=name, grid=(T // ROW_T,), in_specs=[_rows(T)] * n_in + [_vec(), _vec()],
        out_specs=(_rows(T), _rows(T), _rows(T)),
        out_shape=(jax.ShapeDtypeStruct((T, D), F32), jax.ShapeDtypeStruct((T, D), F32), jax.ShapeDtypeStruct((T, D), BF16)),
        compiler_params=_cparams(("parallel",)))(*args)


def _res_ln_bwd(dy, h, g, ple, name):
    T = dy.shape[0]
    has_ple = ple is not None

    def body(*refs):
        if has_ple:
            dy_ref, h_ref, g_ref, pp_ref, gl_ref, dh_ref, dhb_ref, dg_ref, db_ref, dpp_ref, dgl_ref = refs
        else:
            dy_ref, h_ref, g_ref, dh_ref, dhb_ref, dg_ref, db_ref = refs
        i = pl.program_id(0)

        @pl.when(i == 0)
        def _():
            dg_ref[...] = jnp.zeros_like(dg_ref)
            db_ref[...] = jnp.zeros_like(db_ref)

        dyv = dy_ref[...]
        xhat, rstd = _ln_stats(h_ref[...])
        dg_ref[...] += jnp.sum(dyv * xhat, axis=0, keepdims=True)
        db_ref[...] += jnp.sum(dyv, axis=0, keepdims=True)
        dxh = dyv * g_ref[...]
        dh = rstd * (dxh - jnp.mean(dxh, axis=-1, keepdims=True) - xhat * jnp.mean(dxh * xhat, axis=-1, keepdims=True))
        dh_ref[...] = dh
        dhb_ref[...] = dh.astype(BF16)
        if has_ple:
            s = _sig(gl_ref[...])
            dpp_ref[...] = (dh * s).astype(BF16)
            dgl_ref[...] = (dh * pp_ref[...] * s * (1.0 - s)).astype(BF16)

    args = (dy, h, g) + (tuple(ple) if has_ple else ())
    in_specs = [_rows(T), _rows(T), _vec()] + ([_rows(T), _rows(T)] if has_ple else [])
    out_specs = [_rows(T), _rows(T), _vec(), _vec()] + ([_rows(T), _rows(T)] if has_ple else [])
    out_shape = [jax.ShapeDtypeStruct((T, D), F32), jax.ShapeDtypeStruct((T, D), BF16),
                 jax.ShapeDtypeStruct((1, D), F32), jax.ShapeDtypeStruct((1, D), F32)]
    if has_ple:
        out_shape += [jax.ShapeDtypeStruct((T, D), BF16), jax.ShapeDtypeStruct((T, D), BF16)]
    return pl.pallas_call(
        body, name=name, grid=(T // ROW_T,), in_specs=in_specs, out_specs=tuple(out_specs), out_shape=tuple(out_shape),
        compiler_params=_cparams(("arbitrary",)))(*args)


def _ln_silu_fwd(ac, g, b, name):
    T = ac.shape[0]

    def body(a_ref, g_ref, b_ref, o_ref):
        xhat, _ = _ln_stats(a_ref[...])
        ln = xhat * g_ref[...] + b_ref[...]
        o_ref[...] = (ln * _sig(ln)).astype(BF16)

    return pl.pallas_call(
        body, name=name, grid=(T // ROW_T,), in_specs=[_rows(T), _vec(), _vec()], out_specs=_rows(T),
        out_shape=jax.ShapeDtypeStruct((T, D), BF16), compiler_params=_cparams(("parallel",)))(ac, g, b)


def _ln_silu_bwd(ac, dya, g, b, name):
    T = ac.shape[0]

    def body(a_ref, d_ref, g_ref, b_ref, da_ref, dg_ref, db_ref):
        i = pl.program_id(0)

        @pl.when(i == 0)
        def _():
            dg_ref[...] = jnp.zeros_like(dg_ref)
            db_ref[...] = jnp.zeros_like(db_ref)

        xhat, rstd = _ln_stats(a_ref[...])
        ln = xhat * g_ref[...] + b_ref[...]
        s = _sig(ln)
        dln = d_ref[...] * s * (1.0 + ln * (1.0 - s))
        dg_ref[...] += jnp.sum(dln * xhat, axis=0, keepdims=True)
        db_ref[...] += jnp.sum(dln, axis=0, keepdims=True)
        dxh = dln * g_ref[...]
        da_ref[...] = rstd * (dxh - jnp.mean(dxh, axis=-1, keepdims=True)
                              - xhat * jnp.mean(dxh * xhat, axis=-1, keepdims=True))

    return pl.pallas_call(
        body, name=name, grid=(T // ROW_T,), in_specs=[_rows(T), _rows(T), _vec(), _vec()],
        out_specs=(_rows(T), _vec(), _vec()),
        out_shape=(jax.ShapeDtypeStruct((T, D), F32), jax.ShapeDtypeStruct((1, D), F32), jax.ShapeDtypeStruct((1, D), F32)),
        compiler_params=_cparams(("arbitrary",)))(ac, dya, g, b)


def _gate_rms_fwd(y, z, g, name):
    T = y.shape[0]

    def body(y_ref, z_ref, g_ref, o_ref):
        zv = z_ref[...]
        yg = y_ref[...] * (zv * _sig(zv))
        r = lax.rsqrt(jnp.mean(yg * yg, axis=-1, keepdims=True) + LN_EPS)
        o_ref[...] = (yg * r * g_ref[...]).astype(BF16)

    return pl.pallas_call(
        body, name=name, grid=(T // ROW_T,), in_specs=[_rows(T), _rows(T), _vec()], out_specs=_rows(T),
        out_shape=jax.ShapeDtypeStruct((T, D), BF16), compiler_params=_cparams(("parallel",)))(y, z, g)


def _gate_rms_bwd(y, z, dout, g, name):
    T = y.shape[0]

    def body(y_ref, z_ref, d_ref, g_ref, dy_ref, dz_ref, dg_ref):
        i = pl.program_id(0)

        @pl.when(i == 0)
        def _():
            dg_ref[...] = jnp.zeros_like(dg_ref)

        yv, zv, dv = y_ref[...], z_ref[...], d_ref[...]
        s = _sig(zv)
        sz = zv * s
        yg = yv * sz
        r = lax.rsqrt(jnp.mean(yg * yg, axis=-1, keepdims=True) + LN_EPS)
        dg_ref[...] += jnp.sum(dv * yg * r, axis=0, keepdims=True)
        dn = dv * g_ref[...]
        dyg = r * dn - yg * (r * r * r) * jnp.mean(dn * yg, axis=-1, keepdims=True)
        dy_ref[...] = dyg * sz
        dz_ref[...] = dyg * yv * s * (1.0 + zv * (1.0 - s))

    return pl.pallas_call(
        body, name=name, grid=(T // ROW_T,), in_specs=[_rows(T), _rows(T), _rows(T), _vec()],
        out_specs=(_rows(T), _rows(T), _vec()),
        out_shape=(jax.ShapeDtypeStruct((T, D), F32), jax.ShapeDtypeStruct((T, D), F32), jax.ShapeDtypeStruct((1, D), F32)),
        compiler_params=_cparams(("arbitrary",)))(y, z, dout, g)


def _loss_head(y, target, name):
    T = y.shape[0]

    def body(y_ref, t_ref, s_ref, d_ref):
        i = pl.program_id(0)

        @pl.when(i == 0)
        def _():
            s_ref[...] = jnp.zeros_like(s_ref)

        err = y_ref[...] - t_ref[...]
        s_ref[...] += jnp.sum(jnp.sum(err * err, axis=1, keepdims=True), axis=0, keepdims=True)
        d_ref[...] = err * (1.0 / D)

    return pl.pallas_call(
        body, name=name, grid=(T // ROW_T,), in_specs=[_rows(T), _rows(T)],
        out_specs=(pl.BlockSpec((SUBLANES, LANES), lambda i: (0, 0)), _rows(T)),
        out_shape=(jax.ShapeDtypeStruct((SUBLANES, LANES), F32), jax.ShapeDtypeStruct((T, D), F32)),
        compiler_params=_cparams(("arbitrary",)))(y, target)


def _taps_fwd(pad_ref, w_ref, K, base):
    off = CONV_PAD - (K - 1)
    acc = w_ref[0:1, :] * pad_ref[pl.ds(base + off, CONV_R), :]
    for k in range(1, K):
        acc = acc + w_ref[k:k + 1, :] * pad_ref[pl.ds(base + off + k, CONV_R), :]
    return acc


def _taps_bwd(padd_ref, w_ref, K, base):
    acc = w_ref[0:1, :] * padd_ref[pl.ds(base + (K - 1), CONV_R), :]
    for k in range(1, K):
        acc = acc + w_ref[k:k + 1, :] * padd_ref[pl.ds(base + (K - 1) - k, CONV_R), :]
    return acc


def _f32(ref, rows):
    return ref[rows, :].astype(F32)


def _fold8(v):
    return v.reshape(CONV_R // SUBLANES, SUBLANES, v.shape[-1]).sum(0)


def _wgrad_acc(dw_ref, pad_ref, d, K, base):
    off = CONV_PAD - (K - 1)
    for k in range(K):
        dw_ref[k * SUBLANES:(k + 1) * SUBLANES, :] += _fold8(d * pad_ref[pl.ds(base + off + k, CONV_R), :])


def _loop_rows(T, fn):
    def step(r, carry):
        fn(pl.multiple_of(r * CONV_R, CONV_R))
        return carry
    lax.fori_loop(0, T // CONV_R, step, 0)


def _col(T, off_blocks=0, rows=None):
    return pl.BlockSpec((T if rows is None else rows, LANES), lambda j: (0, j + off_blocks))


def _conv_call(body, name, T, n_tiles, in_specs, out_specs, out_shape, n_pad, n_padd=0):
    scratch = [pltpu.VMEM((T + CONV_PAD, LANES), F32)] * (n_pad + n_padd)
    return pl.pallas_call(body, name=name, grid=(n_tiles,), in_specs=in_specs, out_specs=out_specs, out_shape=out_shape,
                          scratch_shapes=scratch, compiler_params=_cparams(("parallel",)))


def _zero_head(ref):
    ref[0:CONV_PAD, :] = jnp.zeros((CONV_PAD, LANES), F32)


def _zero_tail(ref, T):
    ref[T:T + CONV_PAD, :] = jnp.zeros((CONV_PAD, LANES), F32)


def _sds(shape, dtype=F32):
    return jax.ShapeDtypeStruct(shape, dtype)


def _conv_a_fwd(ua, w, b, name):
    T = ua.shape[0]
    K, nt = CONV_A, D // LANES

    def body(al_ref, ag_ref, w_ref, b_ref, o_ref, pad_ref):
        _zero_head(pad_ref)

        def pre(base):
            rows = pl.ds(base, CONV_R)
            pad_ref[pl.ds(base + CONV_PAD, CONV_R), :] = _f32(al_ref, rows) * _sig(_f32(ag_ref, rows))
        _loop_rows(T, pre)

        def main(base):
            o_ref[pl.ds(base, CONV_R), :] = _taps_fwd(pad_ref, w_ref, K, base) + b_ref[...]
        _loop_rows(T, main)

    return _conv_call(body, name, T, nt, [_col(T), _col(T, nt), _col(T, rows=K), _col(T, rows=1)], _col(T),
                      _sds((T, D)), 1)(ua, ua, w, b)


def _conv_a_bwd(ua, w, dac, name):
    T = ua.shape[0]
    K, nt = CONV_A, D // LANES

    def body(al_ref, ag_ref, w_ref, d_ref, dal_ref, dag_ref, dw_ref, db_ref, pad_ref, padd_ref):
        _zero_head(pad_ref)
        _zero_tail(padd_ref, T)
        dw_ref[...] = jnp.zeros_like(dw_ref)
        db_ref[...] = jnp.zeros_like(db_ref)

        def pre(base):
            rows = pl.ds(base, CONV_R)
            pad_ref[pl.ds(base + CONV_PAD, CONV_R), :] = _f32(al_ref, rows) * _sig(_f32(ag_ref, rows))
            padd_ref[rows, :] = d_ref[rows, :]
        _loop_rows(T, pre)

        def main(base):
            rows = pl.ds(base, CONV_R)
            d = d_ref[rows, :]
            _wgrad_acc(dw_ref, pad_ref, d, K, base)
            db_ref[...] += _fold8(d)
            da = _taps_bwd(padd_ref, w_ref, K, base)
            al, s = _f32(al_ref, rows), _sig(_f32(ag_ref, rows))
            dal_ref[rows, :] = da * s
            dag_ref[rows, :] = da * al * s * (1.0 - s)
        _loop_rows(T, main)

    return _conv_call(body, name, T, nt, [_col(T), _col(T, nt), _col(T, rows=K), _col(T)],
                      (_col(T), _col(T), _col(T, rows=K * SUBLANES), _col(T, rows=SUBLANES)),
                      (_sds((T, D)), _sds((T, D)), _sds((K * SUBLANES, D)), _sds((SUBLANES, D))), 1, 1)(ua, ua, w, dac)


def _conv_b_fwd(xu, w, b, name):
    T, C = xu.shape
    K, nt = CONV_B, C // LANES

    def body(x_ref, w_ref, b_ref, o_ref, pad_ref):
        _zero_head(pad_ref)
        pad_ref[CONV_PAD:CONV_PAD + T, :] = x_ref[...].astype(F32)

        def main(base):
            hc = _taps_fwd(pad_ref, w_ref, K, base) + b_ref[...]
            o_ref[pl.ds(base, CONV_R), :] = hc * _sig(hc)
        _loop_rows(T, main)

    return _conv_call(body, name, T, nt, [_col(T), _col(T, rows=K), _col(T, rows=1)], _col(T), _sds((T, C)), 1)(xu, w, b)


def _conv_b_bwd(xu, w, b, dxs, dbs, dcs, name):
    T, C = xu.shape
    K, nt = CONV_B, C // LANES
    nx, nb = dxs.shape[1] // LANES, dbs.shape[1] // LANES

    def body(x_ref, w_ref, b_ref, d1_ref, d2_ref, d3_ref, dx_ref, dw_ref, db_ref, pad_ref, padd_ref):
        j = pl.program_id(0)
        _zero_head(pad_ref)
        _zero_tail(padd_ref, T)
        dw_ref[...] = jnp.zeros_like(dw_ref)
        db_ref[...] = jnp.zeros_like(db_ref)
        pad_ref[CONV_PAD:CONV_PAD + T, :] = x_ref[...].astype(F32)

        def pre(base):
            rows = pl.ds(base, CONV_R)
            hc = _taps_fwd(pad_ref, w_ref, K, base) + b_ref[...]
            s = _sig(hc)
            d = jnp.where(j < nx, d1_ref[rows, :], jnp.where(j < nx + nb, d2_ref[rows, :], d3_ref[rows, :]))
            padd_ref[rows, :] = d * s * (1.0 + hc * (1.0 - s))
        _loop_rows(T, pre)

        def main(base):
            d = padd_ref[pl.ds(base, CONV_R), :]
            _wgrad_acc(dw_ref, pad_ref, d, K, base)
            db_ref[...] += _fold8(d)
            dx_ref[pl.ds(base, CONV_R), :] = _taps_bwd(padd_ref, w_ref, K, base)
        _loop_rows(T, main)

    def piece(lo, n):
        return pl.BlockSpec((T, LANES), lambda j: (0, jnp.clip(j - lo, 0, n - 1)))

    return _conv_call(body, name, T, nt,
                      [_col(T), _col(T, rows=K), _col(T, rows=1), piece(0, nx), piece(nx, nb), piece(nx + nb, nt - nx - nb)],
                      (_col(T), _col(T, rows=K * SUBLANES), _col(T, rows=SUBLANES)),
                      (_sds((T, C)), _sds((K * SUBLANES, C)), _sds((SUBLANES, C))), 1, 1)(xu, w, b, dxs, dbs, dcs)


def _conv_c_fwd(uo, w, name):
    T = uo.shape[0]
    K, nt = CONV_C, D // LANES

    def body(bg_ref, cg_ref, v_ref, w_ref, o_ref, pad_ref):
        _zero_head(pad_ref)
        pad_ref[CONV_PAD:CONV_PAD + T, :] = cg_ref[...].astype(F32) * v_ref[...].astype(F32)

        def main(base):
            rows = pl.ds(base, CONV_R)
            o_ref[rows, :] = (_f32(bg_ref, rows) * _taps_fwd(pad_ref, w_ref, K, base)).astype(BF16)
        _loop_rows(T, main)

    return _conv_call(body, name, T, nt, [_col(T), _col(T, nt), _col(T, 2 * nt), _col(T, rows=K)], _col(T),
                      _sds((T, D), BF16), 1)(uo, uo, uo, w)


def _conv_c_bwd(uo, w, dsc, name):
    T = uo.shape[0]
    K, nt = CONV_C, D // LANES

    def body(bg_ref, cg_ref, v_ref, w_ref, d_ref, dbg_ref, dcg_ref, dv_ref, dw_ref, pad_ref, padd_ref):
        _zero_head(pad_ref)
        _zero_tail(padd_ref, T)
        dw_ref[...] = jnp.zeros_like(dw_ref)
        pad_ref[CONV_PAD:CONV_PAD + T, :] = cg_ref[...].astype(F32) * v_ref[...].astype(F32)

        def pre(base):
            rows = pl.ds(base, CONV_R)
            d = d_ref[rows, :]
            dbg_ref[rows, :] = (d * _taps_fwd(pad_ref, w_ref, K, base)).astype(BF16)
            padd_ref[rows, :] = d * _f32(bg_ref, rows)
        _loop_rows(T, pre)

        def main(base):
            rows = pl.ds(base, CONV_R)
            _wgrad_acc(dw_ref, pad_ref, padd_ref[rows, :], K, base)
            dq = _taps_bwd(padd_ref, w_ref, K, base)
            dcg_ref[rows, :] = (dq * _f32(v_ref, rows)).astype(BF16)
            dv_ref[rows, :] = (dq * _f32(cg_ref, rows)).astype(BF16)
        _loop_rows(T, main)

    return _conv_call(body, name, T, nt, [_col(T), _col(T, nt), _col(T, 2 * nt), _col(T, rows=K), _col(T)],
                      (_col(T), _col(T), _col(T), _col(T, rows=K * SUBLANES)),
                      (_sds((T, D), BF16), _sds((T, D), BF16), _sds((T, D), BF16), _sds((K * SUBLANES, D))), 1, 1)(uo, uo, uo, w, dsc)


def _conv_f_fwd(up, w, b, name):
    T = up.shape[0]
    K, nt = CONV_F, D_FF // LANES

    def body(u1_ref, u2_ref, w1_ref, w2_ref, b1_ref, b2_ref, o_ref, pad1_ref, pad2_ref):
        _zero_head(pad1_ref)
        _zero_head(pad2_ref)
        pad1_ref[CONV_PAD:CONV_PAD + T, :] = u1_ref[...].astype(F32)
        pad2_ref[CONV_PAD:CONV_PAD + T, :] = u2_ref[...].astype(F32)

        def main(base):
            h1 = _taps_fwd(pad1_ref, w1_ref, K, base) + b1_ref[...]
            h2 = _taps_fwd(pad2_ref, w2_ref, K, base) + b2_ref[...]
            o_ref[pl.ds(base, CONV_R), :] = (h1 * _sig(h1) * h2).astype(BF16)
        _loop_rows(T, main)

    return _conv_call(body, name, T, nt,
                      [_col(T), _col(T, nt), _col(T, rows=K), _col(T, nt, rows=K), _col(T, rows=1), _col(T, nt, rows=1)],
                      _col(T), _sds((T, D_FF), BF16), 2)(up, up, w, w, b, b)


def _conv_f_bwd(up, w, b, dact, name):
    T = up.shape[0]
    K, nt = CONV_F, D_FF // LANES

    def body(u1_ref, u2_ref, w1_ref, w2_ref, b1_ref, b2_ref, d_ref, du1_ref, du2_ref, dw1_ref, dw2_ref, db1_ref, db2_ref,
             pad1_ref, pad2_ref, padd1_ref, padd2_ref):
        _zero_head(pad1_ref)
        _zero_head(pad2_ref)
        _zero_tail(padd1_ref, T)
        _zero_tail(padd2_ref, T)
        for r in (dw1_ref, dw2_ref, db1_ref, db2_ref):
            r[...] = jnp.zeros_like(r)
        pad1_ref[CONV_PAD:CONV_PAD + T, :] = u1_ref[...].astype(F32)
        pad2_ref[CONV_PAD:CONV_PAD + T, :] = u2_ref[...].astype(F32)

        def pre(base):
            rows = pl.ds(base, CONV_R)
            h1 = _taps_fwd(pad1_ref, w1_ref, K, base) + b1_ref[...]
            h2 = _taps_fwd(pad2_ref, w2_ref, K, base) + b2_ref[...]
            s = _sig(h1)
            d = _f32(d_ref, rows)
            padd1_ref[rows, :] = d * h2 * s * (1.0 + h1 * (1.0 - s))
            padd2_ref[rows, :] = d * h1 * s
        _loop_rows(T, pre)

        def main(base):
            rows = pl.ds(base, CONV_R)
            d1, d2 = padd1_ref[rows, :], padd2_ref[rows, :]
            _wgrad_acc(dw1_ref, pad1_ref, d1, K, base)
            _wgrad_acc(dw2_ref, pad2_ref, d2, K, base)
            db1_ref[...] += _fold8(d1)
            db2_ref[...] += _fold8(d2)
            du1_ref[rows, :] = _taps_bwd(padd1_ref, w1_ref, K, base).astype(BF16)
            du2_ref[rows, :] = _taps_bwd(padd2_ref, w2_ref, K, base).astype(BF16)
        _loop_rows(T, main)

    wrow, brow = _col(T, rows=K * SUBLANES), _col(T, rows=SUBLANES)
    return _conv_call(body, name, T, nt,
                      [_col(T), _col(T, nt), _col(T, rows=K), _col(T, nt, rows=K), _col(T, rows=1), _col(T, nt, rows=1), _col(T)],
                      (_col(T), _col(T), wrow, wrow, brow, brow),
                      (_sds((T, D_FF), BF16), _sds((T, D_FF), BF16), _sds((K * SUBLANES, D_FF)), _sds((K * SUBLANES, D_FF)),
                       _sds((SUBLANES, D_FF)), _sds((SUBLANES, D_FF))), 2, 2)(up, up, w, w, b, b, dact)


def _dot(a, b, dims="nn"):
    return lax.dot_general(a.astype(BF16), b.astype(BF16), _DIMS[dims], preferred_element_type=F32)


def _dot_mask(mask, v, mask_left):
    mb = mask.astype(BF16)
    hi = v.astype(BF16)
    r1 = v - hi.astype(F32)
    mid = r1.astype(BF16)
    lo = (r1 - mid.astype(F32)).astype(BF16)
    d = [jnp.dot(mb, t, preferred_element_type=F32) if mask_left else jnp.dot(t, mb, preferred_element_type=F32) for t in (hi, mid, lo)]
    return (d[0] + d[1]) + d[2]


def _ssd_small(xcr_ref, xrr_ref, bc_ref, br_ref, ac_ref, ar_ref):
    Q = SSD_Q
    li = lax.broadcasted_iota(jnp.int32, (Q, Q), 0)
    si = lax.broadcasted_iota(jnp.int32, (Q, Q), 1)
    tril = li >= si
    dtc = jax.nn.softplus(xcr_ref[...] + bc_ref[...])
    dtr = jax.nn.softplus(xrr_ref[...] + br_ref[...])
    cumc = _dot_mask(tril, dtc * ac_ref[...], True)
    cumr = _dot_mask(li <= si, dtr * ar_ref[...], False)
    return tril, dtc, dtr, cumc, cumr


def _ssd_specs(nc, rev):
    Q = SSD_Q
    cc = (lambda c: nc - 1 - c) if rev else (lambda c: c)
    x_spec = pl.BlockSpec((Q, 2 * LANES), lambda g, c: (cc(c), g))
    b_spec = pl.BlockSpec((Q, LANES), lambda g, c: (cc(c), 8 + g))
    c_spec = pl.BlockSpec((Q, LANES), lambda g, c: (cc(c), 12 + g))
    colm = pl.BlockSpec((None, Q, LANES), lambda g, c: (g, cc(c), 0))
    rowm = pl.BlockSpec((None, SUBLANES, Q), lambda g, c: (g, 0, cc(c)))
    colv = pl.BlockSpec((None, 1, LANES), lambda g, c: (g, 0, 0))
    rowv = pl.BlockSpec((None, SUBLANES, 1), lambda g, c: (g, 0, 0))
    st_spec = pl.BlockSpec((None, None, 2 * LANES, N_STATE), lambda g, c: (cc(c), g, 0, 0))
    return x_spec, b_spec, c_spec, colm, rowm, colv, rowv, st_spec


def _ssd_fwd(xc, raw_col, raw_row, bias_col, bias_row, a_col, a_row, dskip, name):
    T = xc.shape[0]
    Q = SSD_Q
    nc = T // Q
    x_spec, b_spec, c_spec, colm, rowm, colv, rowv, st_spec = _ssd_specs(nc, False)

    def body(dk_ref, x_ref, b_ref, c_ref, xcr_ref, xrr_ref, bc_ref, br_ref, ac_ref, ar_ref, y_ref, st_ref, h_ref):
        g = pl.program_id(0)

        @pl.when(pl.program_id(1) == 0)
        def _():
            h_ref[...] = jnp.zeros_like(h_ref)

        tril, dtc, dtr, cumc, cumr = _ssd_small(xcr_ref, xrr_ref, bc_ref, br_ref, ac_ref, ar_ref)
        Bm, Cm = b_ref[...], c_ref[...]
        S = _dot(Cm, Bm, "nt")
        lo = lax.broadcasted_iota(jnp.int32, (Q, LANES), 1) < HEAD_P
        rlo = lax.broadcasted_iota(jnp.int32, (LANES, N_STATE), 0) < HEAD_P
        st_ref[...] = h_ref[...]
        clast = cumc[Q - 1:Q, :]
        for pr in range(2):
            cols = slice(pr * LANES, (pr + 1) * LANES)
            xp = x_ref[:, cols]
            yd = jnp.zeros((Q, LANES), F32)
            for q in range(2):
                hh = 2 * pr + q
                seg = cumc[:, hh:hh + 1] - cumr[hh:hh + 1, :]
                lm = jnp.where(tril, jnp.exp(jnp.where(tril, seg, 0.0)), 0.0)
                w = S * lm * dtr[hh:hh + 1, :]
                xm = jnp.where(lo if q == 0 else jnp.logical_not(lo), xp, 0.0)
                yd = yd + _dot(w, xm)
            h0, h1 = 2 * pr, 2 * pr + 1
            c0, c1 = cumc[:, h0:h0 + 1], cumc[:, h1:h1 + 1]
            e_pair = jnp.where(lo, jnp.exp(c0), jnp.exp(c1))
            hp = h_ref[cols, :]
            ch = _dot(Cm, hp, "nt")
            dsk = jnp.where(lo, dk_ref[4 * g + h0], dk_ref[4 * g + h1])
            y_ref[:, cols] = yd + e_pair * ch + dsk * xp
            cl0, cl1 = clast[:, h0:h0 + 1], clast[:, h1:h1 + 1]
            sdec = jnp.where(lo, jnp.exp(cl0 - c0) * dtc[:, h0:h0 + 1], jnp.exp(cl1 - c1) * dtc[:, h1:h1 + 1])
            decrow = jnp.where(rlo, jnp.exp(cl0), jnp.exp(cl1))
            h_ref[cols, :] = hp * decrow + _dot(xp * sdec, Bm, "tn")

    smem = pl.BlockSpec(memory_space=pltpu.SMEM)
    return pl.pallas_call(
        body, name=name, grid=(N_GROUPS, nc),
        in_specs=[smem, x_spec, b_spec, c_spec, colm, rowm, colv, rowv, colv, rowv],
        out_specs=(x_spec, st_spec),
        out_shape=(_sds((T, D)), _sds((nc, N_GROUPS, 2 * LANES, N_STATE))),
        scratch_shapes=[pltpu.VMEM((2 * LANES, N_STATE), F32)],
        compiler_params=_cparams(("parallel", "arbitrary")))(dskip, xc, xc, xc, raw_col, raw_row, bias_col, bias_row, a_col, a_row)


def _ssd_bwd(xc, raw_col, raw_row, bias_col, bias_row, a_col, a_row, dskip, states, dy, name):
    T = xc.shape[0]
    Q = SSD_Q
    nc = T // Q
    x_spec, b_spec, c_spec, colm, rowm, colv, rowv, st_spec = _ssd_specs(nc, True)
    bo_spec = pl.BlockSpec((Q, LANES), lambda g, c: (nc - 1 - c, g))
    dd_spec = pl.BlockSpec((None, None, SUBLANES, 2 * LANES), lambda g, c: (nc - 1 - c, g, 0, 0))

    def body(dk_ref, x_ref, b_ref, c_ref, xcr_ref, xrr_ref, bc_ref, br_ref, ac_ref, ar_ref, st_ref, dy_ref,
             dx_ref, db_ref, dc_ref, sq_ref, cms_ref, ddac_ref, ddar_ref, dd_ref, dh_ref):
        g = pl.program_id(0)

        @pl.when(pl.program_id(1) == 0)
        def _():
            dh_ref[...] = jnp.zeros_like(dh_ref)

        tril, dtc, dtr, cumc, cumr = _ssd_small(xcr_ref, xrr_ref, bc_ref, br_ref, ac_ref, ar_ref)
        Bm, Cm = b_ref[...], c_ref[...]
        S = _dot(Cm, Bm, "nt")
        lane = lax.broadcasted_iota(jnp.int32, (Q, LANES), 1)
        sub = lax.broadcasted_iota(jnp.int32, (SUBLANES, Q), 0)
        rowi = lax.broadcasted_iota(jnp.int32, (Q, LANES), 0)
        lo = lane < HEAD_P
        rlo = lax.broadcasted_iota(jnp.int32, (LANES, N_STATE), 0) < HEAD_P
        clast = cumc[Q - 1:Q, :]
        ds_g = jnp.zeros((Q, Q), F32)
        dcm = jnp.zeros((Q, N_STATE), F32)
        dbm = jnp.zeros((Q, N_STATE), F32)
        dcum_col = jnp.zeros((Q, LANES), F32)
        dcum_row = jnp.zeros((SUBLANES, Q), F32)
        sq_col = jnp.zeros((Q, LANES), F32)
        cms_row = jnp.zeros((SUBLANES, Q), F32)
        for pr in range(2):
            cols = slice(pr * LANES, (pr + 1) * LANES)
            xp, dyp = x_ref[:, cols], dy_ref[:, cols]
            hin, dhp = st_ref[cols, :], dh_ref[cols, :]
            h0, h1 = 2 * pr, 2 * pr + 1
            c0, c1 = cumc[:, h0:h0 + 1], cumc[:, h1:h1 + 1]
            cl0, cl1 = clast[:, h0:h0 + 1], clast[:, h1:h1 + 1]
            e_pair = jnp.where(lo, jnp.exp(c0), jnp.exp(c1))
            edec = jnp.where(lo, jnp.exp(cl0 - c0), jnp.exp(cl1 - c1))
            dt_pair = jnp.where(lo, dtc[:, h0:h0 + 1], dtc[:, h1:h1 + 1])
            sdec = edec * dt_pair
            ch = _dot(Cm, hin, "nt")
            xb = _dot(Bm, dhp, "nt")
            dye = dyp * e_pair
            t1 = dye * ch
            t2 = xp * xb * edec
            hh_prod = dhp * hin
            dsk = jnp.where(lo, dk_ref[4 * g + h0], dk_ref[4 * g + h1])
            dxp = sdec * xb + dsk * dyp
            for q in range(2):
                hh = 2 * pr + q
                mine = lo if q == 0 else jnp.logical_not(lo)
                seg = cumc[:, hh:hh + 1] - cumr[hh:hh + 1, :]
                lm = jnp.where(tril, jnp.exp(jnp.where(tril, seg, 0.0)), 0.0)
                dtrow = dtr[hh:hh + 1, :]
                w = S * lm * dtrow
                dym = jnp.where(mine, dyp, 0.0)
                gl = _dot(dym, xp, "nt") * lm
                ds_g = ds_g + gl * dtrow
                ms = gl * S
                m = ms * dtrow
                dxp = dxp + _dot(w, dym, "tn")
                cms_row = jnp.where(sub == hh, jnp.sum(ms, axis=0, keepdims=True), cms_row)
                dcum_row = jnp.where(sub == hh, -jnp.sum(m, axis=0, keepdims=True), dcum_row)
                t1h = jnp.sum(jnp.where(mine, t1, 0.0), axis=1, keepdims=True)
                sqh = jnp.sum(jnp.where(mine, t2, 0.0), axis=1, keepdims=True)
                sth = sqh * dtc[:, hh:hh + 1]
                rmine = rlo if q == 0 else jnp.logical_not(rlo)
                hsum = jnp.sum(jnp.sum(jnp.where(rmine, hh_prod, 0.0), axis=1, keepdims=True), axis=0, keepdims=True)
                last = jnp.sum(sth, axis=0, keepdims=True) + jnp.exp(clast[:, hh:hh + 1]) * hsum
                dcol = jnp.sum(m, axis=1, keepdims=True) + t1h - sth
                dcum_col = jnp.where(lane == hh, dcol + jnp.where(rowi == Q - 1, last, 0.0), dcum_col)
                sq_col = jnp.where(lane == hh, sqh, sq_col)
            dcm = dcm + _dot(dye, hin)
            dbm = dbm + _dot(xp * sdec, dhp)
            decrow = jnp.where(rlo, jnp.exp(cl0), jnp.exp(cl1))
            dh_ref[cols, :] = dhp * decrow + _dot(dye, Cm, "tn")
            dx_ref[:, cols] = dxp
            dd_ref[:, cols] = jnp.broadcast_to(jnp.sum(dyp * xp, axis=0, keepdims=True), (SUBLANES, LANES))
        dc_ref[...] = dcm + _dot(ds_g, Bm)
        db_ref[...] = dbm + _dot(ds_g, Cm, "tn")
        li = lax.broadcasted_iota(jnp.int32, (Q, Q), 0)
        si = lax.broadcasted_iota(jnp.int32, (Q, Q), 1)
        ddac_ref[...] = _dot_mask(li <= si, dcum_col, True)
        ddar_ref[...] = _dot_mask(tril, dcum_row, False)
        sq_ref[...] = sq_col
        cms_ref[...] = cms_row

    smem = pl.BlockSpec(memory_space=pltpu.SMEM)
    return pl.pallas_call(
        body, name=name, grid=(N_GROUPS, nc),
        in_specs=[smem, x_spec, b_spec, c_spec, colm, rowm, colv, rowv, colv, rowv, st_spec, x_spec],
        out_specs=(x_spec, bo_spec, bo_spec, colm, rowm, colm, rowm, dd_spec),
        out_shape=(_sds((T, D)), _sds((T, D // 2)), _sds((T, D // 2)), _sds((N_GROUPS, T, LANES)), _sds((N_GROUPS, SUBLANES, T)),
                   _sds((N_GROUPS, T, LANES)), _sds((N_GROUPS, SUBLANES, T)), _sds((nc, N_GROUPS, SUBLANES, 2 * LANES))),
        scratch_shapes=[pltpu.VMEM((2 * LANES, N_STATE), F32)],
        compiler_params=_cparams(("parallel", "arbitrary")))(dskip, xc, xc, xc, raw_col, raw_row, bias_col, bias_row, a_col, a_row,
                                                            states, dy)


def _adam_math(wv, gv, mv, vv):
    c1 = 1.0 - ADAM_B1 ** ADAM_STEP
    c2 = 1.0 - ADAM_B2 ** ADAM_STEP
    mn = ADAM_B1 * mv + (1.0 - ADAM_B1) * gv
    vn = ADAM_B2 * vv + (1.0 - ADAM_B2) * (gv * gv)
    return -ADAM_LR * ((mn / c1) / (jnp.sqrt(vn / c2) + ADAM_EPS) + ADAM_WD * wv), mn, vn


def _adamw_layers(w, g, m, v, l0, Lg, bufs, name):
    L, As, Bs = w.shape
    tr = _tile(As, [], (256, 352, 128))
    has_bufs = bufs is not None

    def body(*refs):
        w_ref, g_ref, m_ref, v_ref = refs[:4]
        d_ref, mo_ref, vo_ref = refs[4 + 3 * has_bufs:]
        d_ref[...], mo_ref[...], vo_ref[...] = _adam_math(w_ref[...], g_ref[...], m_ref[...], v_ref[...])

    spec = pl.BlockSpec((None, tr, Bs), lambda l, i: (l + l0, i, 0))
    args = (w, g, m, v) + (tuple(bufs) if has_bufs else ())
    return pl.pallas_call(
        body, name=name, grid=(Lg, As // tr), in_specs=[spec] * 4 + [_ANY] * (3 * has_bufs), out_specs=(spec,) * 3,
        out_shape=(_sds((L, As, Bs)),) * 3, input_output_aliases={4: 0, 5: 1, 6: 2} if has_bufs else {},
        compiler_params=_cparams(("parallel", "parallel")))(*args)


def _adamw(w, g, m, v, name):
    shape = w.shape
    cols = shape[-1]
    w2, g2, m2, v2 = (t.reshape(-1, cols) for t in (w, g, m, v))
    rows = w2.shape[0]
    tr = 256 if (rows % 256 == 0 and rows > 256) else rows
    c1 = 1.0 - ADAM_B1 ** ADAM_STEP
    c2 = 1.0 - ADAM_B2 ** ADAM_STEP

    def body(w_ref, g_ref, m_ref, v_ref, d_ref, mo_ref, vo_ref):
        gv = g_ref[...]
        mn = ADAM_B1 * m_ref[...] + (1.0 - ADAM_B1) * gv
        vn = ADAM_B2 * v_ref[...] + (1.0 - ADAM_B2) * (gv * gv)
        d_ref[...] = -ADAM_LR * ((mn / c1) / (jnp.sqrt(vn / c2) + ADAM_EPS) + ADAM_WD * w_ref[...])
        mo_ref[...] = mn
        vo_ref[...] = vn

    spec = pl.BlockSpec((tr, cols), lambda i: (i, 0))
    out = pl.pallas_call(body, name=name, grid=(rows // tr,), in_specs=[spec] * 4, out_specs=(spec,) * 3,
                         out_shape=(_sds((rows, cols)),) * 3, compiler_params=_cparams(("parallel",)))(w2, g2, m2, v2)
    return tuple(o.reshape(shape) for o in out)


def _place():
    x, y, c = lax.axis_index("x"), lax.axis_index("y"), lax.axis_index("c")
    chips = [(1 - x, y), (x, 1 - y), (1 - x, 1 - y)]
    return x, y, c, chips


_ANY = pl.BlockSpec(memory_space=pl.ANY)


TENSORS = (("e_w_in", "row", 2, 4096, 1284, 1024), ("e_w_out", "row", 2, 2048, 1024, 512), ("o_w_in", "col", 2, 1024, 3072, 768),
           ("o_w_out", "row", 2, 1024, 1024, 256), ("f_w_up", "col", 4, 1024, 5632, 1408), ("f_w_down", "row", 4, 2816, 1024, 704),
           ("ple_w_proj", "col", 4, 256, 1024, 256), ("ple_w_gate", "row", 4, 1024, 1024, 256))
MIX, FFN = "mix", "ffn"
W_GROUPS = (((0, MIX),), ((0, FFN),), ((1, MIX), (1, FFN)), ((2, MIX), (2, FFN), (3, MIX), (3, FFN)))
G_GROUPS = (((3, FFN), (3, MIX), (2, FFN), (2, MIX), (1, FFN), (1, MIX)), ((0, FFN),), ((0, MIX),))


def _tensor_layer(name, layer):
    if name.startswith("e_"):
        return layer // 2 if layer % 2 == 0 else None
    if name.startswith("o_"):
        return layer // 2 if layer % 2 == 1 else None
    return layer


def _part(name):
    return MIX if name.startswith(("e_", "o_")) else FFN


def _group_items(members):
    items = []
    for name, kind, L, A, B, n in TENSORS:
        tls = sorted(t for t in (_tensor_layer(name, l) for l, part in members if part == _part(name)) if t is not None)
        if tls:
            assert tls == list(range(tls[0], tls[0] + len(tls)))
            items.append((name, kind, len(tls), A, B, n, tls[0]))
    return items


def _hwin(ref, it, k, h):
    name, kind, Lg, A, B, n, l0 = it
    if kind == "row":
        return ref.at[:, pl.ds(pl.multiple_of(k * n + h * (n // 2), 16), n // 2), :]
    return ref.at[:, pl.ds(pl.multiple_of(h * (A // 2), 16), A // 2), pl.ds(pl.multiple_of(k * n, LANES), n)]


def _shard_dims(kind, A, B, n):
    return (n, B) if kind == "row" else (A, n)


def _cast_into(w, it, me):
    name, kind, Lg, A, B, n, l0 = it
    As, Bs = _shard_dims(kind, A, B, n)

    def body(me_ref, w_ref, o_ref):
        o_ref[...] = w_ref[...].astype(BF16)

    omap = (lambda l, m: (l, m[0], 0)) if kind == "row" else (lambda l, m: (l, 0, m[0]))
    grid_spec = pltpu.PrefetchScalarGridSpec(
        num_scalar_prefetch=1, grid=(Lg,), in_specs=[pl.BlockSpec((None, As, Bs), lambda l, m: (l + l0, 0, 0))],
        out_specs=pl.BlockSpec((None, As, Bs), omap))
    return pl.pallas_call(body, name=f"cast_{name}_{l0}", grid_spec=grid_spec, out_shape=_sds((Lg, A, B), BF16),
                          compiler_params=_cparams(("parallel",)))(me, w.reshape(-1, As, Bs))


_HBM = pl.BlockSpec(memory_space=pltpu.HBM)
_SEM = pl.BlockSpec(memory_space=pltpu.SEMAPHORE)
_EFFECT = pltpu.SideEffectType.DATAFLOW_SIDE_EFFECTING


def _hbm(a):
    return pltpu.with_memory_space_constraint(a, pltpu.HBM)


def _split_start(thru, n_copies, issue, name, after=None):
    N = len(thru)
    has_after = after is not None

    def body(*refs):
        outs = refs[N + has_after:2 * N + has_after]
        send_sems, recv_sems, token = refs[2 * N + has_after:]
        for cp in issue(outs, send_sems, recv_sems):
            cp.start()
        token[...] = jnp.zeros_like(token)

    out = pl.pallas_call(
        body, name=name, in_specs=[_HBM] * N + ([_ANY] if has_after else []),
        out_specs=(_HBM,) * N + (_SEM, _SEM, pl.BlockSpec(memory_space=pltpu.VMEM)),
        out_shape=tuple(pltpu.HBM(a.shape, a.dtype) for a in thru)
        + (pltpu.SemaphoreType.DMA((n_copies,)), pltpu.SemaphoreType.DMA((n_copies,)), _sds((SUBLANES, LANES))),
        input_output_aliases={t: t for t in range(N)},
        compiler_params=pltpu.CompilerParams(has_side_effects=_EFFECT))(*[_hbm(a) for a in thru], *([after] if has_after else []))
    return list(out[:N]), out[N], out[N + 1], out[N + 2]


def _split_wait(thru, send_sems, recv_sems, after, waits, name):
    N = len(thru)
    after = list(after) if isinstance(after, (list, tuple)) else [after]

    def body(*refs):
        ins = refs[:N]
        for cp, side in waits(ins, refs[N], refs[N + 1]):
            if side == "send":
                cp.wait_send()
            else:
                cp.wait_recv()

    out = pl.pallas_call(
        body, name=name, in_specs=[_HBM] * N + [_SEM, _SEM] + [_ANY] * len(after), out_specs=(_HBM,) * N,
        out_shape=tuple(pltpu.HBM(a.shape, a.dtype) for a in thru), input_output_aliases={t: t for t in range(N)},
        compiler_params=pltpu.CompilerParams(has_side_effects=_EFFECT))(*thru, send_sems, recv_sems, *after)
    return list(out)


def _rcopy(send_sems, recv_sems, k, src, dst, to):
    return pltpu.make_async_remote_copy(src_ref=src, dst_ref=dst, send_sem=send_sems.at[k], recv_sem=recv_sems.at[k],
                                        device_id=to, device_id_type=MESH)


def _gather_copies(items, refs, send_sems, recv_sems, what):
    x, y, c, chips = _place()
    me = 2 * x + y
    out = []
    for t, it in enumerate(items):
        mine = _hwin(refs[t], it, me, c)
        for j, (px, py) in enumerate(chips):
            if what == "start":
                out.append(_rcopy(send_sems, recv_sems, 3 * t + j, mine, mine, (px, py, c)))
            else:
                slot = _hwin(refs[t], it, 2 * px + py, c)
                out.append((_rcopy(send_sems, recv_sems, 3 * t + j, mine, mine, (px, py, c)), "send"))
                out.append((_rcopy(send_sems, recv_sems, 3 * t + j, slot, slot, (px, py, c)), "recv"))
    return out


def _gather_start(fulls, items, name, after=None):
    return _split_start(fulls, 3 * len(items), functools.partial(_gather_copies, items, what="start"), name, after)


def _gather_wait(fulls, send_sems, recv_sems, after, items, name):
    return _split_wait(fulls, send_sems, recv_sems, after, functools.partial(_gather_copies, items, what="wait"), name)


def _gather_fwd(fulls, items, name, ws=None):
    N = len(fulls)
    has_ws = ws is not None

    def body(*refs):
        outs = refs[N + has_ws:2 * N + has_ws]
        rest = refs[2 * N + has_ws:]
        x, y, c, chips = _place()
        me = 2 * x + y
        sib = (x, y, 1 - c)
        if has_ws:
            ws_ref = refs[N]
            WS_ref, send_sems, recv_sems, lsem = rest
            loc = pltpu.make_async_copy(ws_ref, WS_ref.at[me], lsem)
            loc.start()
        else:
            send_sems, recv_sems = rest
        rc = functools.partial(_rcopy, send_sems, recv_sems)
        cps = []
        for t, it in enumerate(items):
            for j, (px, py) in enumerate(chips):
                slot = _hwin(outs[t], it, 2 * px + py, c)
                cps.append(rc(3 * t + j, slot, slot, sib))
        if has_ws:
            cps += [rc(3 * N + j, ws_ref, WS_ref.at[me], (*chip, c)) for j, chip in enumerate(chips)]
        for cp in cps:
            cp.start()
        for t, it in enumerate(items):
            for j, (px, py) in enumerate(chips):
                oslot = _hwin(outs[t], it, 2 * px + py, 1 - c)
                rc(3 * t + j, oslot, oslot, sib).wait_recv()
        if has_ws:
            for j, (px, py) in enumerate(chips):
                sslot = WS_ref.at[2 * px + py]
                rc(3 * N + j, sslot, sslot, sib).wait_recv()
        for cp in cps:
            cp.wait_send()
        if has_ws:
            loc.wait()

    ns = 3 * N + (3 if has_ws else 0)
    out_shape = tuple(_sds(f.shape, f.dtype) for f in fulls)
    scratch = [pltpu.SemaphoreType.DMA((ns,)), pltpu.SemaphoreType.DMA((ns,))]
    args = list(fulls)
    if has_ws:
        out_shape += (_sds((4,) + ws.shape, ws.dtype),)
        scratch.append(pltpu.SemaphoreType.DMA(()))
        args.append(ws)
    out = pl.pallas_call(
        body, name=name, in_specs=[_ANY] * len(args), out_specs=(_ANY,) * len(out_shape), out_shape=out_shape,
        input_output_aliases={t: t for t in range(N)}, scratch_shapes=scratch,
        compiler_params=pltpu.CompilerParams(has_side_effects=True))(*args)
    return (list(out[:N]), out[N]) if has_ws else (list(out), None)


def _half_shape(it):
    name, kind, Lg, A, B, n, l0 = it
    return (Lg, 4, n // 2, B) if kind == "row" else (Lg, A // 2, B)


def _piece_shape(it):
    name, kind, Lg, A, B, n, l0 = it
    return (Lg, n // 2, B) if kind == "row" else (Lg, A // 2, n)


def _swap_grads(gs, items, name):
    N = len(gs)

    def body(*refs):
        g_refs, o_refs = refs[:N], refs[N:2 * N]
        send_sems, recv_sems = refs[2 * N:]
        x, y, c, _ = _place()
        sib = (x, y, 1 - c)
        cps = []
        for t, it in enumerate(items):
            name_, kind, Lg, A, B, n, l0 = it
            if kind == "row":
                for k in range(4):
                    cps.append(_rcopy(send_sems, recv_sems, 4 * t + k, _hwin(g_refs[t], it, k, 1 - c), o_refs[t].at[:, k], sib))
            else:
                src = g_refs[t].at[:, pl.ds(pl.multiple_of((1 - c) * (A // 2), 16), A // 2), :]
                cps.append(_rcopy(send_sems, recv_sems, 4 * t, src, o_refs[t], sib))
        for cp in cps:
            cp.start()
        for cp in cps:
            cp.wait()

    return pl.pallas_call(
        body, name=name, in_specs=[_ANY] * N, out_specs=(_ANY,) * N, out_shape=tuple(_sds(_half_shape(it)) for it in items),
        scratch_shapes=[pltpu.SemaphoreType.DMA((4 * N,)), pltpu.SemaphoreType.DMA((4 * N,))],
        compiler_params=pltpu.CompilerParams(has_side_effects=True))(*gs)


def _add_half(g, ra, it, cvec):
    name, kind, Lg, A, B, n, l0 = it
    if kind == "row":
        blk = (None, n // 2, B)
        grid = (Lg, 4)
        g_spec = pl.BlockSpec(blk, lambda l, k, cr: (l, 2 * k + cr[0], 0))
        h_spec = pl.BlockSpec((None, None, n // 2, B), lambda l, k, cr: (l, k, 0, 0))
    else:
        tr = _tile(A // 2, [], (256, 128))
        nb = (A // 2) // tr
        grid = (Lg, nb)
        g_spec = pl.BlockSpec((None, tr, B), lambda l, i, cr: (l, cr[0] * nb + i, 0))
        h_spec = pl.BlockSpec((None, tr, B), lambda l, i, cr: (l, i, 0))

    def body(c_ref, g_ref, r_ref, o_ref):
        o_ref[...] = (g_ref[...] + r_ref[...]).astype(BF16)

    grid_spec = pltpu.PrefetchScalarGridSpec(num_scalar_prefetch=1, grid=grid, in_specs=[g_spec, h_spec], out_specs=h_spec)
    return pl.pallas_call(body, name=f"addhalf_{name}_{l0}", grid_spec=grid_spec, out_shape=_sds(_half_shape(it), BF16),
                          compiler_params=_cparams(("parallel", "parallel")))(cvec, g, ra)


def _scatter_copies(items, refs, send_sems, recv_sems, what):
    N = len(items)
    x, y, c, chips = _place()
    out = []
    for t, it in enumerate(items):
        name, kind, Lg, A, B, n, l0 = it
        for j, (px, py) in enumerate(chips):
            k = 2 * px + py
            src = refs[t].at[:, k] if kind == "row" else refs[t].at[:, :, pl.ds(pl.multiple_of(k * n, LANES), n)]
            cp = _rcopy(send_sems, recv_sems, 3 * t + j, src, refs[N + t].at[j], (px, py, c))
            if what == "start":
                out.append(cp)
            else:
                out += [(cp, "send"), (cp, "recv")]
    return out


def _scatter_start(ps, items, name):
    lands = [lax.empty((3,) + _piece_shape(it), BF16) for it in items]
    return _split_start(list(ps) + lands, 3 * len(items), functools.partial(_scatter_copies, items, what="start"), name)


def _scatter_wait(thru, send_sems, recv_sems, after, items, name):
    return _split_wait(thru, send_sems, recv_sems, after, functools.partial(_scatter_copies, items, what="wait"), name)


def _sum_own(p, rc, it, mevec, buf):
    name, kind, Lg, A, B, n, l0 = it
    As, Bs = _shard_dims(kind, A, B, n)
    L = [s[2] for s in TENSORS if s[0] == name][0]
    hb = (As // 2, Bs)
    has_buf = buf is not None

    def body(*refs):
        p_ref, r0, r1, r2 = refs[1:5]
        o_ref = refs[5 + has_buf]
        o_ref[...] = ((p_ref[...].astype(F32) + r0[...].astype(F32)) + r1[...].astype(F32)) + r2[...].astype(F32)

    if kind == "row":
        p_spec = pl.BlockSpec((None, None) + hb, lambda l, m: (l, m[0], 0, 0))
    else:
        p_spec = pl.BlockSpec((None,) + hb, lambda l, m: (l, 0, m[0]))
    r_specs = [pl.BlockSpec((None, None) + hb, functools.partial(lambda l, m, j: (j, l, 0, 0), j=j)) for j in range(3)]
    in_specs = [p_spec] + r_specs + ([_ANY] if has_buf else [])
    grid_spec = pltpu.PrefetchScalarGridSpec(num_scalar_prefetch=1, grid=(Lg,), in_specs=in_specs,
                                             out_specs=pl.BlockSpec((None,) + hb, lambda l, m: (l + l0, m[1], 0)))
    args = (mevec, p, rc, rc, rc) + ((buf,) if has_buf else ())
    return pl.pallas_call(body, name=f"sumown_{name}_{l0}", grid_spec=grid_spec, out_shape=_sds((L, As, Bs)),
                          input_output_aliases={5: 0} if has_buf else {}, compiler_params=_cparams(("parallel",)))(*args)


def _join_halves(rs, items, name):
    N = len(rs)

    def body(*refs):
        outs = refs[N:2 * N]
        send_sems, recv_sems = refs[2 * N:]
        x, y, c, _ = _place()
        sib = (x, y, 1 - c)

        def half(t, h):
            name_, kind, Lg, A, B, n, l0 = items[t]
            hr = _shard_dims(kind, A, B, n)[0] // 2
            return outs[t].at[pl.ds(l0, Lg), pl.ds(pl.multiple_of(h * hr, SUBLANES), hr), :]

        cps = [_rcopy(send_sems, recv_sems, t, half(t, c), half(t, c), sib) for t in range(N)]
        for cp in cps:
            cp.start()
        for t in range(N):
            _rcopy(send_sems, recv_sems, t, half(t, 1 - c), half(t, 1 - c), sib).wait_recv()
        for cp in cps:
            cp.wait_send()

    return list(pl.pallas_call(
        body, name=name, in_specs=[_ANY] * N, out_specs=(_ANY,) * N, out_shape=tuple(_sds(r.shape, r.dtype) for r in rs),
        input_output_aliases={t: t for t in range(N)},
        scratch_shapes=[pltpu.SemaphoreType.DMA((N,)), pltpu.SemaphoreType.DMA((N,))],
        compiler_params=pltpu.CompilerParams(has_side_effects=True))(*rs))


def _allgather_small(v):
    m_per, n = v.shape

    def body(x_ref, out_ref, send_sems, recv_sems, local_sem):
        x, y, c, chips = _place()
        me, sibling = (x, y, c), (x, y, 1 - c)

        def rows(px, py, pc):
            return out_ref.at[pl.ds(pl.multiple_of((4 * px + 2 * py + pc) * m_per, SUBLANES), m_per), :]

        def copy(k, block, to, src=None):
            return pltpu.make_async_remote_copy(src_ref=rows(*block) if src is None else src, dst_ref=rows(*block),
                                                send_sem=send_sems.at[k], recv_sem=recv_sems.at[k], device_id=to, device_id_type=MESH)

        mine = pltpu.make_async_copy(x_ref, rows(*me), local_sem)
        mine.start()
        first = [copy(0, me, sibling, src=x_ref)]
        first += [copy(1 + j, me, (*chip, c), src=x_ref) for j, chip in enumerate(chips)]
        for cp in first:
            cp.start()
        passed = [copy(4 + j, (*chip, c), sibling) for j, chip in enumerate(chips)]
        for j, chip in enumerate(chips):
            copy(1 + j, (*chip, c), me).wait_recv()
            passed[j].start()
        copy(0, sibling, me).wait_recv()
        for j, chip in enumerate(chips):
            copy(4 + j, (*chip, 1 - c), me).wait_recv()
        for cp in first + passed:
            cp.wait_send()
        mine.wait()

    vm = pl.BlockSpec(memory_space=pltpu.VMEM)
    return pl.pallas_call(body, name="allgather_small", in_specs=[vm], out_specs=vm, out_shape=_sds((8 * m_per, n)),
                          scratch_shapes=[pltpu.SemaphoreType.DMA((7,)), pltpu.SemaphoreType.DMA((7,)), pltpu.SemaphoreType.DMA(())],
                          compiler_params=pltpu.CompilerParams(has_side_effects=True, vmem_limit_bytes=VMEM_LIMIT))(v)


def _sum8(v, m_per):
    def body(v_ref, o_ref):
        acc = v_ref[0:m_per, :]
        for k in range(1, 8):
            acc = acc + v_ref[k * m_per:(k + 1) * m_per, :]
        o_ref[...] = acc

    return pl.pallas_call(body, name="small_sum_devices", out_shape=_sds((m_per, v.shape[1])),
                          compiler_params=pltpu.CompilerParams(vmem_limit_bytes=VMEM_LIMIT))(v)


SMALL_SHARDED = (("e_conv_a_w", 2), ("e_conv_b_w", 2), ("o_conv_w", 2), ("f_conv_w", 2), ("ln_g", 2), ("ln_b", 2))
SMALL_REPL = ("e_conv_a_b", "e_ln_a_g", "e_ln_a_b", "e_conv_b_b", "e_dt_bias", "e_a_log", "e_d_skip", "e_norm_b_g", "f_conv_b")

WEIGHT_ORDER = ('e_w_in', 'e_conv_a_w', 'e_conv_a_b', 'e_ln_a_g', 'e_ln_a_b', 'e_conv_b_w', 'e_conv_b_b', 'e_dt_bias', 'e_a_log',
                'e_d_skip', 'e_norm_b_g', 'e_w_out', 'o_w_in', 'o_conv_w', 'o_w_out', 'f_w_up', 'f_conv_w', 'f_conv_b', 'f_w_down',
                'ple_w_proj', 'ple_w_gate', 'ln_g', 'ln_b')


def _pack_rows(parts, width, total_rows, dtype):
    flat = jnp.concatenate([p.reshape(-1).astype(dtype) for p in parts])
    flat = jnp.pad(flat, (0, total_rows * width - flat.shape[0]))
    return flat.reshape(total_rows, width)


def _unpack_rows(buf, shapes):
    flat = buf.reshape(-1)
    out, pos = [], 0
    for s in shapes:
        n = math.prod(s)
        out.append(flat[pos:pos + n].reshape(s))
        pos += n
    return out


def _small_rows(shapes):
    n = sum(math.prod(s) for s in shapes)
    return -(-n // (LANES * SUBLANES)) * SUBLANES


E_PAD = 5248
SEG_A, SEG_Z, SEG_X, SEG_DT = (0, 2 * D), (2 * D, D), (3 * D, 2 * D), (5 * D, LANES)
G_SHAPES = {"e_w_in": (2, D, E_PAD), "e_w_out": (2, 2 * D, D), "o_w_in": (2, D, 3 * D), "o_w_out": (2, D, D),
            "f_w_up": (4, D, 2 * D_FF), "f_w_down": (4, D_FF, D), "ple_w_proj": (4, PLE, D), "ple_w_gate": (4, D, D)}


def _padcols(w, width):
    return jnp.pad(w, ((0, 0), (0, width - w.shape[1])))


def _fold_rows(dw, K):
    return dw.reshape(K, SUBLANES, dw.shape[-1]).sum(1)


class GradBuffers(dict):
    def __init__(self):
        super().__init__()
        self.where = {}
        for gi, layers in enumerate(G_GROUPS):
            for name, kind, Lg, A, B, n, l0 in _group_items(layers):
                for k in range(Lg):
                    self.where[(name, l0 + k)] = (gi, k, Lg)
        self.current = {}

    def into(self, name, layer, r0=0, c0=0):
        gi, k, Lg = self.where[(name, layer)]
        self.current[name] = (name, gi)
        return (self.get((name, gi)), (Lg,) + G_SHAPES[name][1:], (k,), r0, c0)

    def __setitem__(self, name, value):
        super().__setitem__(self.current[name], value)


def _local_step(x, p, target, W, comm=None):
    T = x.shape[0]
    xb = x
    saved = []
    xc_f = x
    for i in range(DEPTH):
        j = i // 2
        L = {}
        L["x"], L["xb"] = xc_f, xb
        tok = comm.part_starts(i, MIX, xb) if comm is not None else None
        if i % 2 == 0:
            def w_in(seg, c0=0, cols=None, j=j):
                return V(W["e_w_in"], (j,), c0=seg[0] + c0, cols=seg[1] if cols is None else cols)

            ua = _mm(xb, w_in(SEG_A), "nn", f"l{i}_in_a", BF16, after=tok)
            z = _mm(xb, w_in(SEG_Z), "nn", f"l{i}_in_z")
            xu = _mm(xb, w_in(SEG_X), "nn", f"l{i}_in_xbc", BF16)
            udt = _mm(xb, w_in(SEG_DT), "nn", f"l{i}_in_dt")
            ac = _conv_a_fwd(ua, W["e_conv_a_w"][j], W["e_conv_a_b"][j][None], f"l{i}_conv_a")
            ya = _ln_silu_fwd(ac, W["e_ln_a_g"][j][None], W["e_ln_a_b"][j][None], f"l{i}_ln_a")
            xc = _conv_b_fwd(xu, W["e_conv_b_w"][j], W["e_conv_b_b"][j][None], f"l{i}_conv_b")
            sm = _ssd_small_inputs(udt[:, :N_HEADS], W["e_dt_bias"][j], W["e_a_log"][j])
            y, states = _ssd_fwd(xc, *sm, W["e_d_skip"][j], f"l{i}_ssd")
            yb = _gate_rms_fwd(y, z, W["e_norm_b_g"][j][None], f"l{i}_gate_rms")
            mix = _mm_sum([(ya, V(W["e_w_out"], (j,), rows=D)), (yb, V(W["e_w_out"], (j,), r0=D))], "nn", f"l{i}_out")
            L.update(ua=ua, z=z, xu=xu, udt=udt, ac=ac, ya=ya, xc=xc, sm=sm, y=y, states=states, yb=yb, w_in=w_in)
        else:
            uo = _mm(xb, V(W["o_w_in"], (j,)), "nn", f"l{i}_in", BF16, after=tok)
            sc = _conv_c_fwd(uo, W["o_conv_w"][j], f"l{i}_conv_c")
            mix = _mm(sc, V(W["o_w_out"], (j,)), "nn", f"l{i}_out")
            L.update(uo=uo, sc=sc)
        h1, x1, x1b = _res_ln_fwd(xc_f, [mix], None, W["ln_g"][i, 0][None], W["ln_b"][i, 0][None], f"l{i}_ln1")
        tok = comm.part_starts(i, FFN, x1b) if comm is not None else None
        up = _mm(x1b, V(W["f_w_up"], (i,)), "nn", f"l{i}_ffn_up", BF16, after=tok)
        act = _conv_f_fwd(up, W["f_conv_w"][i], W["f_conv_b"][i][None], f"l{i}_conv_f")
        ffn = _mm(act, V(W["f_w_down"], (i,)), "nn", f"l{i}_ffn_down")
        pv = V(p, (i, 0))
        pp = _mm(pv, V(W["ple_w_proj"], (i,)), "nn", f"l{i}_ple_proj")
        gl = _mm(x1b, V(W["ple_w_gate"], (i,)), "nn", f"l{i}_ple_gate")
        h2, x2, x2b = _res_ln_fwd(x1, [ffn], (pp, gl), W["ln_g"][i, 1][None], W["ln_b"][i, 1][None], f"l{i}_ln2")
        L.update(h1=h1, x1=x1, x1b=x1b, up=up, act=act, pv=pv, pp=pp, gl=gl, h2=h2)
        saved.append(L)
        xc_f, xb = x2, x2b

    sq, dx = _loss_head(xc_f, target, "loss_head")

    GB = GradBuffers()
    into = GB.into
    tok = None

    G = {n: [None] * (DEPTH if n.startswith(("f_", "ln_")) else DEPTH // 2) for n in WEIGHT_ORDER if n not in G_SHAPES}
    for i in reversed(range(DEPTH)):
        j = i // 2
        L = saved[i]
        dh2, dh2b, dg2, db2, dpp, dgl = _res_ln_bwd(dx, L["h2"], W["ln_g"][i, 1][None], (L["pp"], L["gl"]), f"l{i}_ln2_bwd")
        GB["f_w_down"] = _mm(L["act"], dh2b, "tn", f"l{i}_dw_down", dst=into("f_w_down", i))
        dact = _mm(dh2b, V(W["f_w_down"], (i,)), "nt", f"l{i}_dact", BF16, after=tok)
        du1, du2, dw1, dw2, dbf1, dbf2 = _conv_f_bwd(L["up"], W["f_conv_w"][i], W["f_conv_b"][i][None], dact, f"l{i}_conv_f_bwd")
        G["f_conv_w"][i] = jnp.concatenate([_fold_rows(dw1, CONV_F), _fold_rows(dw2, CONV_F)], axis=1)
        G["f_conv_b"][i] = jnp.concatenate([dbf1.sum(0), dbf2.sum(0)])
        GB["f_w_up"] = _mm(L["x1b"], du1, "tn", f"l{i}_dw_up1", dst=into("f_w_up", i))
        GB["f_w_up"] = _mm(L["x1b"], du2, "tn", f"l{i}_dw_up2", dst=into("f_w_up", i, c0=D_FF))
        GB["ple_w_proj"] = _mm(L["pv"], dpp, "tn", f"l{i}_dw_proj", dst=into("ple_w_proj", i))
        GB["ple_w_gate"] = _mm(L["x1b"], dgl, "tn", f"l{i}_dw_gate", dst=into("ple_w_gate", i))
        tok = comm.part_grads_done(i, FFN, GB) if comm is not None else None
        dx1 = _mm_sum([(du1, V(W["f_w_up"], (i,), cols=D_FF)), (du2, V(W["f_w_up"], (i,), c0=D_FF)), (dgl, V(W["ple_w_gate"], (i,)))],
                      "nt", f"l{i}_dx1", add=dh2, add_scale=ALPHA, after=tok)
        dh1, dh1b, dg1, db1 = _res_ln_bwd(dx1, L["h1"], W["ln_g"][i, 0][None], None, f"l{i}_ln1_bwd")
        G["ln_g"][i] = jnp.concatenate([dg1, dg2], axis=0)
        G["ln_b"][i] = jnp.concatenate([db1, db2], axis=0)
        if i % 2 == 0:
            GB["e_w_out"] = _mm(L["ya"], dh1b, "tn", f"l{i}_dw_out_a", dst=into("e_w_out", j))
            GB["e_w_out"] = _mm(L["yb"], dh1b, "tn", f"l{i}_dw_out_b", dst=into("e_w_out", j, r0=D))
            dya = _mm(dh1b, V(W["e_w_out"], (j,), rows=D), "nt", f"l{i}_dya")
            dyb = _mm(dh1b, V(W["e_w_out"], (j,), r0=D), "nt", f"l{i}_dyb")
            dac, dga, dba = _ln_silu_bwd(L["ac"], dya, W["e_ln_a_g"][j][None], W["e_ln_a_b"][j][None], f"l{i}_ln_a_bwd")
            G["e_ln_a_g"][j], G["e_ln_a_b"][j] = dga[0], dba[0]
            dal, dag, dwa, dbca = _conv_a_bwd(L["ua"], W["e_conv_a_w"][j], dac, f"l{i}_conv_a_bwd")
            G["e_conv_a_w"][j] = _fold_rows(dwa, CONV_A)
            G["e_conv_a_b"][j] = dbca.sum(0)
            dy, dz, dgn = _gate_rms_bwd(L["y"], L["z"], dyb, W["e_norm_b_g"][j][None], f"l{i}_gate_rms_bwd")
            G["e_norm_b_g"][j] = dgn[0]
            dxs, dbs, dcs, sq_col, cms_row, dda_col, dda_row, ddp = _ssd_bwd(L["xc"], *L["sm"], W["e_d_skip"][j], L["states"], dy,
                                                                             f"l{i}_ssd_bwd")
            draw, G["e_dt_bias"][j], G["e_a_log"][j] = _ssd_small_grads(L["udt"][:, :N_HEADS], W["e_dt_bias"][j], W["e_a_log"][j],
                                                                       sq_col, cms_row, dda_col, dda_row)
            G["e_d_skip"][j] = ddp[:, :, 0, :].sum(0).reshape(N_HEADS, HEAD_P).sum(1)
            dxu, dwb, dbcb = _conv_b_bwd(L["xu"], W["e_conv_b_w"][j], W["e_conv_b_b"][j][None], dxs, dbs, dcs, f"l{i}_conv_b_bwd")
            G["e_conv_b_w"][j] = _fold_rows(dwb, CONV_B)
            G["e_conv_b_b"][j] = dbcb.sum(0)
            dudt = _padcols(draw, LANES)
            w_in = L["w_in"]
            xb_l = L["xb"]
            for nm, dseg, c0 in (("al", dal, 0), ("ag", dag, D), ("z", dz, SEG_Z[0]), ("xbc", dxu, SEG_X[0]), ("dt", dudt, SEG_DT[0])):
                GB["e_w_in"] = _mm(xb_l, dseg, "tn", f"l{i}_dw_in_{nm}", dst=into("e_w_in", j, c0=c0))
            dx = _mm_sum([(dal, w_in(SEG_A, cols=D)), (dag, w_in(SEG_A, c0=D, cols=D)), (dz, w_in(SEG_Z)),
                          (V(dxu, cols=D), w_in(SEG_X, cols=D)), (V(dxu, c0=D), w_in(SEG_X, c0=D, cols=D)), (dudt, w_in(SEG_DT))],
                         "nt", f"l{i}_dx", add=dh1, add_scale=ALPHA)
        else:
            GB["o_w_out"] = _mm(L["sc"], dh1b, "tn", f"l{i}_dw_out", dst=into("o_w_out", j))
            dsc = _mm(dh1b, V(W["o_w_out"], (j,)), "nt", f"l{i}_dsc")
            dbg, dcg, dv, dwc = _conv_c_bwd(L["uo"], W["o_conv_w"][j], dsc, f"l{i}_conv_c_bwd")
            G["o_conv_w"][j] = _fold_rows(dwc, CONV_C)
            xb_l = L["xb"]
            for nm, dseg, c0 in (("bg", dbg, 0), ("cg", dcg, D), ("v", dv, 2 * D)):
                GB["o_w_in"] = _mm(xb_l, dseg, "tn", f"l{i}_dw_in_{nm}", dst=into("o_w_in", j, c0=c0))
            dx = _mm_sum([(dseg, V(W["o_w_in"], (j,), c0=c0, cols=D)) for dseg, c0 in ((dbg, 0), (dcg, D), (dv, 2 * D))],
                         "nt", f"l{i}_dx", add=dh1, add_scale=ALPHA)
        tok = comm.part_grads_done(i, MIX, GB) if comm is not None else None
    grads = {n: jnp.stack(v) for n, v in G.items()}
    return sq, dx, GB, grads


def _ssd_small_inputs(raw, dt_bias, a_log):
    T = raw.shape[0]
    a = -jnp.exp(a_log)
    rg = raw.reshape(T, N_GROUPS, 4)
    raw_col = jnp.pad(jnp.transpose(rg, (1, 0, 2)), ((0, 0), (0, 0), (0, LANES - 4)))
    raw_row = jnp.pad(jnp.transpose(rg, (1, 2, 0)), ((0, 0), (0, SUBLANES - 4), (0, 0)))

    def colv(v):
        return jnp.pad(v.reshape(N_GROUPS, 1, 4), ((0, 0), (0, 0), (0, LANES - 4)))

    def rowv(v):
        return jnp.pad(v.reshape(N_GROUPS, 4, 1), ((0, 0), (0, SUBLANES - 4), (0, 0)))

    return raw_col, raw_row, colv(dt_bias), rowv(dt_bias), colv(a), rowv(a)


def _ssd_small_grads(raw, dt_bias, a_log, sq_col, cms_row, dda_col, dda_row):
    T = raw.shape[0]

    def join(col, row):
        c = jnp.transpose(col[:, :, :4], (1, 0, 2)).reshape(T, N_HEADS)
        r = jnp.transpose(row[:, :4, :], (2, 0, 1)).reshape(T, N_HEADS)
        return c + r

    a = -jnp.exp(a_log)
    pre = raw + dt_bias
    dt = jax.nn.softplus(pre)
    dda = join(dda_col, dda_row)
    ddt = join(sq_col, cms_row) + a * dda
    draw = ddt * jax.nn.sigmoid(pre)
    da = jnp.sum(dt * dda, axis=0)
    return draw, jnp.sum(draw, axis=0), da * a


def kernel(x, p, e_w_in, e_conv_a_w, e_conv_a_b, e_ln_a_g, e_ln_a_b, e_conv_b_w, e_conv_b_b, e_dt_bias, e_a_log, e_d_skip, e_norm_b_g, e_w_out, o_w_in, o_conv_w, o_w_out, f_w_up, f_conv_w, f_conv_b, f_w_down, ple_w_proj, ple_w_gate, ln_g, ln_b, loss_target, m_e_w_in, m_e_conv_a_w, m_e_conv_a_b, m_e_ln_a_g, m_e_ln_a_b, m_e_conv_b_w, m_e_conv_b_b, m_e_dt_bias, m_e_a_log, m_e_d_skip, m_e_norm_b_g, m_e_w_out, m_o_w_in, m_o_conv_w, m_o_w_out, m_f_w_up, m_f_conv_w, m_f_conv_b, m_f_w_down, m_ple_w_proj, m_ple_w_gate, m_ln_g, m_ln_b, v_e_w_in, v_e_conv_a_w, v_e_conv_a_b, v_e_ln_a_g, v_e_ln_a_b, v_e_conv_b_w, v_e_conv_b_b, v_e_dt_bias, v_e_a_log, v_e_d_skip, v_e_norm_b_g, v_e_w_out, v_o_w_in, v_o_conv_w, v_o_w_out, v_f_w_up, v_f_conv_w, v_f_conv_b, v_f_w_down, v_ple_w_proj, v_ple_w_gate, v_ln_g, v_ln_b):
    args = dict(locals())
    w_shard = {n: args[n] for n in WEIGHT_ORDER}
    m_shard = {n: args["m_" + n] for n in WEIGHT_ORDER}
    v_shard = {n: args["v_" + n] for n in WEIGHT_ORDER}
    xi, yi, ci = lax.axis_index("x"), lax.axis_index("y"), lax.axis_index("c")
    chip = 2 * xi + yi

    mevec = jnp.stack([chip, ci]).astype(jnp.int32)
    small_shapes = [w_shard[n].shape for n, _ in SMALL_SHARDED]
    sr = _small_rows(small_shapes)
    ws = _pack_rows([w_shard[n] for n, _ in SMALL_SHARDED], LANES, sr, F32)
    W = {n: w_shard[n] for n in SMALL_REPL}
    W.update({s[0]: Layers(s[2]) for s in TENSORS})
    w_items = [_group_items(layers) for layers in W_GROUPS]
    g_items = [_group_items(layers) for layers in G_GROUPS]

    def install(items, fulls):
        for it, f in zip(items, fulls):
            if it[0] == "e_w_in":
                f = jnp.transpose(f.reshape(it[2], 4, D, E_IN // 4), (0, 2, 1, 3)).reshape(it[2], D, E_IN)
                f = jnp.pad(f, ((0, 0), (0, 0), (0, E_PAD - E_IN)))
            W[it[0]].put(f, it[6])

    casts = [[_cast_into(w_shard[it[0]], it, mevec[:1]) for it in items] for items in w_items]
    fulls, ssem, rsem, _ = _gather_start(casts[0], w_items[0], "gather_start_0")
    fulls = _gather_wait(fulls, ssem, rsem, [c for grp in casts[1:] for c in grp], w_items[0], "gather_wait_0")
    fulls, WS = _gather_fwd(fulls, w_items[0], "gather_fwd_0", ws)
    install(w_items[0], fulls)
    parts_s = [_unpack_rows(WS[k], small_shapes) for k in range(4)]
    for idx, (n, ax) in enumerate(SMALL_SHARDED):
        W[n] = jnp.concatenate([parts_s[k][idx] for k in range(4)], axis=ax)

    class Comm:
        sent = {}
        started = {}
        tail = fulls[0]

        def start_next(self, gi):
            if gi >= len(w_items):
                return None
            self.started[gi] = _gather_start(casts[gi], w_items[gi], f"gather_start_{gi}", self.tail)
            return self.started[gi][3]

        def part_starts(self, layer, part, after):
            if (layer, part) == W_GROUPS[0][0]:
                return self.start_next(1)
            for gi in range(1, len(W_GROUPS)):
                if W_GROUPS[gi][0] == (layer, part):
                    fulls, ssem, rsem, _ = self.started[gi]
                    fulls = _gather_wait(fulls, ssem, rsem, after, w_items[gi], f"gather_wait_{gi}")
                    fulls, _ = _gather_fwd(fulls, w_items[gi], f"gather_fwd_{gi}")
                    install(w_items[gi], fulls)
                    self.tail = fulls[0]
                    return self.start_next(gi + 1)
            return None

        def part_grads_done(self, layer, part, GB):
            tok = None
            for gi, members in enumerate(G_GROUPS):
                if members[-1] == (layer, part):
                    items = g_items[gi]
                    gs = []
                    for it in items:
                        g = GB[(it[0], gi)]
                        if it[0] == "e_w_in":
                            g = jnp.transpose(g[:, :, :E_IN].reshape(it[2], D, 4, E_IN // 4), (0, 2, 1, 3)).reshape(it[2], 4 * D, E_IN // 4)
                        gs.append(g)
                    ras = _swap_grads(gs, items, f"swap_grads_{gi}")
                    ps = [_add_half(g, ra, it, mevec[1:]) for g, ra, it in zip(gs, ras, items)]
                    thru, ssem, rsem, tok = _scatter_start(ps, items, f"scatter_start_{gi}")
                    self.sent[gi] = (thru, ssem, rsem, tok)
            return tok

    comm = Comm()

    sq, dx, GB, G = _local_step(x[0], p, loss_target[0], W, comm)
    loss = lax.psum(0.5 * sq[0, 0] / D, ("x", "y", "c"))
    grad_x = dx[None]

    def shard_of(g, ax, k):
        n = g.shape[ax] // 4
        return lax.slice_in_dim(g, k * n, (k + 1) * n, axis=ax)

    reduced, updated = {}, {}
    after = comm.sent[len(g_items) - 1][3]
    for gi, items in enumerate(g_items):
        thru, ssem, rsem, _ = comm.sent[gi]
        thru = _scatter_wait(thru, ssem, rsem, after, items, f"scatter_wait_{gi}")
        ps, rcs = thru[:len(items)], thru[len(items):]
        rs = [_sum_own(pt, rc, it, mevec, reduced.get(it[0])) for pt, rc, it in zip(ps, rcs, items)]
        rs = _join_halves(rs, items, f"join_halves_{gi}")
        reduced.update({it[0]: r for it, r in zip(items, rs)})
        for it in items:
            n = it[0]
            updated[n] = _adamw_layers(w_shard[n], reduced[n], m_shard[n], v_shard[n], it[6], it[2], updated.get(n), f"adamw_{n}_{it[6]}")
        after = updated[items[0][0]][0]

    small_all = ([shard_of(G[n], ax, k) for k in range(4) for n, ax in SMALL_SHARDED] + [G[n] for n in SMALL_REPL])
    small_all_shapes = [t.shape for t in small_all]
    mr = _small_rows(small_all_shapes)
    sg = _sum8(_allgather_small(_pack_rows(small_all, LANES, mr, F32)), mr)
    sparts = _unpack_rows(sg, small_all_shapes)
    ns = len(SMALL_SHARDED)
    gsmall = {}
    for idx, (n, ax) in enumerate(SMALL_SHARDED):
        stacked = jnp.stack([sparts[k * ns + idx] for k in range(4)])
        gsmall[n] = lax.dynamic_index_in_dim(stacked, chip, axis=0, keepdims=False)
    for idx, n in enumerate(SMALL_REPL):
        gsmall[n] = sparts[4 * ns + idx]

    grads, deltas, new_m, new_v = [], [], [], []
    for n in WEIGHT_ORDER:
        if n in reduced:
            g, (d, mn, vn) = reduced[n], updated[n]
        else:
            g = gsmall[n]
            d, mn, vn = _adamw(w_shard[n], g, m_shard[n], v_shard[n], f"adamw_{n}")
        grads.append(g)
        deltas.append(d)
        new_m.append(mn)
        new_v.append(vn)
    return (loss, grad_x, *grads, *deltas, *new_m, *new_v)
```

```python
import functools
import math

import jax
import jax.numpy as jnp
from jax import lax
from jax.experimental import pallas as pl
from jax.experimental.pallas import tpu as pltpu

F32 = jnp.float32
BF16 = jnp.bfloat16
MESH = pl.DeviceIdType.MESH

DEPTH = 4
ALPHA = (2.0 * DEPTH) ** 0.25
LN_EPS = 1e-5
D = 1024
HEAD_P = 64
N_STATE = 128
N_HEADS = 16
N_GROUPS = 4
CONV_A, CONV_B, CONV_C, CONV_F = 31, 4, 3, 3
D_FF = 2816
PLE = 256
E_IN = 5136

ADAM_LR, ADAM_B1, ADAM_B2, ADAM_EPS, ADAM_WD, ADAM_STEP = 0.001, 0.9, 0.999, 1e-08, 0.01, 10

LANES = 128
SUBLANES = 8
VMEM_LIMIT = 56 * 1024 * 1024
SSD_Q = 128
CONV_R = 128
CONV_PAD = 32
ROW_T = 256


def _cparams(sem=None):
    return pltpu.CompilerParams(dimension_semantics=sem, vmem_limit_bytes=VMEM_LIMIT)


def _sig(v):
    return jax.nn.sigmoid(v)


_DIMS = {"nn": (((1,), (0,)), ((), ())), "nt": (((1,), (1,)), ((), ())), "tn": (((0,), (0,)), ((), ()))}


class Layers:
    def __init__(self, n_layers):
        self.where = [None] * n_layers

    def put(self, arr, l0):
        for k in range(arr.shape[0]):
            self.where[l0 + k] = (arr, k)


class V:
    def __init__(self, arr, lead=(), r0=0, c0=0, rows=None, cols=None):
        if isinstance(arr, Layers):
            arr, k = arr.where[lead[0]]
            lead = (k,) + tuple(lead[1:])
        self.arr, self.lead, self.r0, self.c0 = arr, tuple(lead), r0, c0
        R, C = arr.shape[-2:]
        self.rows = R - r0 if rows is None else rows
        self.cols = C - c0 if cols is None else cols

    def spec(self, br, bc, fn):
        assert self.r0 % br == 0 and self.c0 % bc == 0, (self.r0, self.c0, br, bc)
        ro, co, lead = self.r0 // br, self.c0 // bc, self.lead

        def index(i, j, k):
            r, c = fn(i, j, k)
            return lead + (r + ro, c + co)

        return pl.BlockSpec((None,) * len(lead) + (br, bc), index)


def _v(t):
    return t if isinstance(t, V) else V(t)


def _tile(n, offs, cands):
    for c in cands:
        if n % c == 0 and all(o % c == 0 for o in offs):
            return c
    raise ValueError((n, offs))


_TILES = (1024, 1408, 512, 256, 128)


def _mm(a, b, mode, name, out_dtype=F32, add=None, add_scale=1.0, dst=None, after=None):
    a, b = _v(a), _v(b)
    add = _v(add) if add is not None else None
    if mode == "nn":
        M, K, K2, N = a.rows, a.cols, b.rows, b.cols
        am, ak, bk, bn = a.r0, a.c0, b.r0, b.c0
    elif mode == "nt":
        M, K, N, K2 = a.rows, a.cols, b.rows, b.cols
        am, ak, bn, bk = a.r0, a.c0, b.r0, b.c0
    else:
        K, M, K2, N = a.rows, a.cols, b.rows, b.cols
        ak, am, bk, bn = a.r0, a.c0, b.r0, b.c0
    assert K == K2, (name, mode, M, K, K2, N)
    if dst is None:
        buf, full_shape, o_lead, o_r0, o_c0 = None, (M, N), (), 0, 0
    else:
        buf, full_shape, o_lead, o_r0, o_c0 = dst
    tm = _tile(M, [am, o_r0] + ([add.r0] if add else []), _TILES)
    tn = _tile(N, [bn, o_c0] + ([add.c0] if add else []), _TILES)
    narrow = a.arr.dtype.itemsize == 2 and b.arr.dtype.itemsize == 2
    tk = _tile(K, [ak, bk], ((2048,) if narrow else ()) + _TILES)
    nk = K // tk
    has_add, has_buf, has_after = add is not None, buf is not None, after is not None

    def body(*refs):
        a_ref, b_ref = refs[0], refs[1]
        add_ref = refs[2] if has_add else None
        o_ref = refs[2 + has_add + has_buf + has_after]

        def finish(r):
            if has_add:
                r = r + add_scale * add_ref[...].astype(F32)
            o_ref[...] = r.astype(o_ref.dtype)

        part = lax.dot_general(a_ref[...].astype(BF16), b_ref[...].astype(BF16), _DIMS[mode], preferred_element_type=F32)
        if nk == 1:
            finish(part)
        else:
            acc_ref = refs[-1]
            k = pl.program_id(2)

            @pl.when(k == 0)
            def _():
                acc_ref[...] = part

            @pl.when(jnp.logical_and(k > 0, k < nk - 1))
            def _():
                acc_ref[...] += part

            @pl.when(k == nk - 1)
            def _():
                finish(acc_ref[...] + part)

    if mode == "tn":
        a_spec = a.spec(tk, tm, lambda i, j, k: (k, i))
    else:
        a_spec = a.spec(tm, tk, lambda i, j, k: (i, k))
    if mode == "nt":
        b_spec = b.spec(tn, tk, lambda i, j, k: (j, k))
    else:
        b_spec = b.spec(tk, tn, lambda i, j, k: (k, j))
    in_specs, args = [a_spec, b_spec], [a.arr, b.arr]
    if has_add:
        in_specs.append(add.spec(tm, tn, lambda i, j, k: (i, j)))
        args.append(add.arr)
    aliases = {}
    if has_buf:
        aliases = {len(args): 0}
        in_specs.append(pl.BlockSpec(memory_space=pl.ANY))
        args.append(buf)
        out_dtype = buf.dtype
    if has_after:
        in_specs.append(pl.BlockSpec(memory_space=pl.ANY))
        args.append(after)
    o_view = V(jax.ShapeDtypeStruct(full_shape, out_dtype), o_lead, o_r0, o_c0, M, N)
    return pl.pallas_call(
        body, name=name, grid=(M // tm, N // tn, nk), in_specs=in_specs, out_specs=o_view.spec(tm, tn, lambda i, j, k: (i, j)),
        out_shape=jax.ShapeDtypeStruct(full_shape, out_dtype), input_output_aliases=aliases,
        scratch_shapes=[pltpu.VMEM((tm, tn), F32)] if nk > 1 else [],
        compiler_params=_cparams(("parallel", "parallel", "arbitrary")))(*args)


def _ln_stats(h):
    mu = jnp.mean(h, axis=-1, keepdims=True)
    hc = h - mu
    var = jnp.mean(hc * hc, axis=-1, keepdims=True)
    rstd = lax.rsqrt(var + LN_EPS)
    return hc * rstd, rstd


def _ln_bwd_math(dyv, h, g):
    xhat, rstd = _ln_stats(h)
    dxh = dyv * g
    dh = rstd * (dxh - jnp.mean(dxh, axis=-1, keepdims=True) - xhat * jnp.mean(dxh * xhat, axis=-1, keepdims=True))
    return dh, jnp.sum(dyv * xhat, axis=0, keepdims=True), jnp.sum(dyv, axis=0, keepdims=True)


def _mm_sum(pairs, mode, name, out_dtype=F32, add=None, add_scale=1.0, after=None, ln_fwd=None, ln_bwd=None):
    pairs = [(_v(a), _v(b)) for a, b in pairs]
    add = _v(add) if add is not None else None
    M = pairs[0][0].rows
    N = pairs[0][1].cols if mode == "nn" else pairs[0][1].rows
    b_offs = [(b.c0 if mode == "nn" else b.r0) for _, b in pairs]
    fused = ln_fwd is not None or ln_bwd is not None
    tm = _tile(M, [a.r0 for a, _ in pairs] + ([add.r0] if add else []), (256, 128) if fused else (512, 256, 128))
    tn = N if fused else _tile(N, b_offs + ([add.c0] if add else []), (512, 256, 128))
    assert not fused or (N == D and all(o == 0 for o in b_offs))
    n_p, has_add, has_after = len(pairs), add is not None, after is not None
    ple = (ln_fwd[1] if ln_fwd is not None else ln_bwd[2]) if fused else None
    has_ple = ple is not None

    def body(*refs):
        acc = None
        for i in range(n_p):
            part = lax.dot_general(refs[2 * i][...].astype(BF16), refs[2 * i + 1][...].astype(BF16), _DIMS[mode],
                                   preferred_element_type=F32)
            acc = part if acc is None else acc + part
        pos = 2 * n_p
        if has_add:
            acc = acc + add_scale * refs[pos][...].astype(F32)
            pos += 1
        if ln_fwd is not None:
            x_ref = refs[pos]
            pp_ref, gl_ref = (refs[pos + 1], refs[pos + 2]) if has_ple else (None, None)
            pos += 1 + 2 * has_ple
            g_ref, b_ref = refs[pos], refs[pos + 1]
            h_ref, y_ref, yb_ref = refs[pos + 2 + has_after:]
            h = ALPHA * x_ref[...] + acc
            if has_ple:
                h = h + pp_ref[...] * _sig(gl_ref[...])
            xhat, _ = _ln_stats(h)
            y = xhat * g_ref[...] + b_ref[...]
            h_ref[...] = h
            y_ref[...] = y
            yb_ref[...] = y.astype(BF16)
        elif ln_bwd is not None:
            h_ref, g_ref = refs[pos], refs[pos + 1]
            pp_ref, gl_ref = (refs[pos + 2], refs[pos + 3]) if has_ple else (None, None)
            outs = refs[pos + 2 + 2 * has_ple + has_after:]
            dh_ref, dhb_ref, dg_ref, db_ref = outs[:4]

            @pl.when(pl.program_id(0) == 0)
            def _():
                dg_ref[...] = jnp.zeros_like(dg_ref)
                db_ref[...] = jnp.zeros_like(db_ref)

            dh, dg, db = _ln_bwd_math(acc, h_ref[...], g_ref[...])
            dg_ref[...] += dg
            db_ref[...] += db
            dh_ref[...] = dh
            dhb_ref[...] = dh.astype(BF16)
            if has_ple:
                s = _sig(gl_ref[...])
                outs[4][...] = (dh * s).astype(BF16)
                outs[5][...] = (dh * pp_ref[...] * s * (1.0 - s)).astype(BF16)
        else:
            o_ref = refs[pos + has_after]
            o_ref[...] = acc.astype(o_ref.dtype)

    in_specs, args = [], []
    for a, b in pairs:
        K = a.cols
        assert K == (b.rows if mode == "nn" else b.cols), (name, K)
        in_specs.append(a.spec(tm, K, lambda i, j, k: (i, 0)))
        in_specs.append(b.spec(K, tn, lambda i, j, k: (0, j)) if mode == "nn" else b.spec(tn, K, lambda i, j, k: (j, 0)))
        args += [a.arr, b.arr]
    if has_add:
        in_specs.append(add.spec(tm, tn, lambda i, j, k: (i, j)))
        args.append(add.arr)
    row = pl.BlockSpec((tm, tn), lambda i, j, k: (i, j))
    vec = pl.BlockSpec((1, tn), lambda i, j, k: (0, 0))
    if ln_fwd is not None:
        x, _, g, b = ln_fwd
        extra = [x] + (list(ple) if has_ple else []) + [g, b]
        in_specs += [row] * (1 + 2 * has_ple) + [vec, vec]
        args += extra
        out_specs = (row, row, row)
        out_shape = (_sds((M, N)), _sds((M, N)), _sds((M, N), BF16))
    elif ln_bwd is not None:
        h, g, _ = ln_bwd
        in_specs += [row, vec] + [row] * (2 * has_ple)
        args += [h, g] + (list(ple) if has_ple else [])
        out_specs = (row, row, vec, vec) + ((row, row) if has_ple else ())
        out_shape = (_sds((M, N)), _sds((M, N), BF16), _sds((1, N)), _sds((1, N))) + ((_sds((M, N), BF16),) * 2 if has_ple else ())
    else:
        out_specs, out_shape = row, jax.ShapeDtypeStruct((M, N), out_dtype)
    if has_after:
        in_specs.append(pl.BlockSpec(memory_space=pl.ANY))
        args.append(after)
    return pl.pallas_call(
        body, name=name, grid=(M // tm, N // tn, 1), in_specs=in_specs, out_specs=out_specs, out_shape=out_shape,
        compiler_params=_cparams(("arbitrary",) * 3 if ln_bwd is not None else ("parallel", "parallel", "arbitrary")))(*args)


def _rows(T, width=D):
    return pl.BlockSpec((ROW_T, width), lambda i: (i, 0))


def _vec(width=D):
    return pl.BlockSpec((1, width), lambda i: (0, 0))


def _res_ln_fwd(x, adds, ple, g, b, name):
    T = x.shape[0]
    n_add = len(adds)
    has_ple = ple is not None

    def body(*refs):
        x_ref = refs[0]
        add_refs = refs[1:1 + n_add]
        pos = 1 + n_add
        if has_ple:
            pp_ref, gl_ref = refs[pos], refs[pos + 1]
            pos += 2
        g_ref, b_ref, h_ref, y_ref, yb_ref = refs[pos:pos + 5]
        h = ALPHA * x_ref[...]
        for r in add_refs:
            h = h + r[...]
        if has_ple:
            h = h + pp_ref[...] * _sig(gl_ref[...])
        xhat, _ = _ln_stats(h)
        y = xhat * g_ref[...] + b_ref[...]
        h_ref[...] = h
        y_ref[...] = y
        yb_ref[...] = y.astype(BF16)

    n_in = 1 + n_add + (2 if has_ple else 0)
    args = (x,) + tuple(adds) + (tuple(ple) if has_ple else ()) + (g, b)
    return pl.pallas_call(
        body, name=name, grid=(T // ROW_T,), in_specs=[_rows(T)] * n_in + [_vec(), _vec()],
        out_specs=(_rows(T), _rows(T), _rows(T)),
        out_shape=(jax.ShapeDtypeStruct((T, D), F32), jax.ShapeDtypeStruct((T, D), F32), jax.ShapeDtypeStruct((T, D), BF16)),
        compiler_params=_cparams(("parallel",)))(*args)


def _res_ln_bwd(dy, h, g, ple, name):
    T = dy.shape[0]
    has_ple = ple is not None

    def body(*refs):
        if has_ple:
            dy_ref, h_ref, g_ref, pp_ref, gl_ref, dh_ref, dhb_ref, dg_ref, db_ref, dpp_ref, dgl_ref = refs
        else:
            dy_ref, h_ref, g_ref, dh_ref, dhb_ref, dg_ref, db_ref = refs
        i = pl.program_id(0)

        @pl.when(i == 0)
        def _():
            dg_ref[...] = jnp.zeros_like(dg_ref)
            db_ref[...] = jnp.zeros_like(db_ref)

        dyv = dy_ref[...]
        xhat, rstd = _ln_stats(h_ref[...])
        dg_ref[...] += jnp.sum(dyv * xhat, axis=0, keepdims=True)
        db_ref[...] += jnp.sum(dyv, axis=0, keepdims=True)
        dxh = dyv * g_ref[...]
        dh = rstd * (dxh - jnp.mean(dxh, axis=-1, keepdims=True) - xhat * jnp.mean(dxh * xhat, axis=-1, keepdims=True))
        dh_ref[...] = dh
        dhb_ref[...] = dh.astype(BF16)
        if has_ple:
            s = _sig(gl_ref[...])
            dpp_ref[...] = (dh * s).astype(BF16)
            dgl_ref[...] = (dh * pp_ref[...] * s * (1.0 - s)).astype(BF16)

    args = (dy, h, g) + (tuple(ple) if has_ple else ())
    in_specs = [_rows(T), _rows(T), _vec()] + ([_rows(T), _rows(T)] if has_ple else [])
    out_specs = [_rows(T), _rows(T), _vec(), _vec()] + ([_rows(T), _rows(T)] if has_ple else [])
    out_shape = [jax.ShapeDtypeStruct((T, D), F32), jax.ShapeDtypeStruct((T, D), BF16),
                 jax.ShapeDtypeStruct((1, D), F32), jax.ShapeDtypeStruct((1, D), F32)]
    if has_ple:
        out_shape += [jax.ShapeDtypeStruct((T, D), BF16), jax.ShapeDtypeStruct((T, D), BF16)]
    return pl.pallas_call(
        body, name=name, grid=(T // ROW_T,), in_specs=in_specs, out_specs=tuple(out_specs), out_shape=tuple(out_shape),
        compiler_params=_cparams(("arbitrary",)))(*args)


def _ln_silu_fwd(ac, g, b, name):
    T = ac.shape[0]

    def body(a_ref, g_ref, b_ref, o_ref):
        xhat, _ = _ln_stats(a_ref[...])
        ln = xhat * g_ref[...] + b_ref[...]
        o_ref[...] = (ln * _sig(ln)).astype(BF16)

    return pl.pallas_call(
        body, name=name, grid=(T // ROW_T,), in_specs=[_rows(T), _vec(), _vec()], out_specs=_rows(T),
        out_shape=jax.ShapeDtypeStruct((T, D), BF16), compiler_params=_cparams(("parallel",)))(ac, g, b)


def _ln_silu_bwd(ac, dya, g, b, name):
    T = ac.shape[0]

    def body(a_ref, d_ref, g_ref, b_ref, da_ref, dg_ref, db_ref):
        i = pl.program_id(0)

        @pl.when(i == 0)
        def _():
            dg_ref[...] = jnp.zeros_like(dg_ref)
            db_ref[...] = jnp.zeros_like(db_ref)

        xhat, rstd = _ln_stats(a_ref[...])
        ln = xhat * g_ref[...] + b_ref[...]
        s = _sig(ln)
        dln = d_ref[...] * s * (1.0 + ln * (1.0 - s))
        dg_ref[...] += jnp.sum(dln * xhat, axis=0, keepdims=True)
        db_ref[...] += jnp.sum(dln, axis=0, keepdims=True)
        dxh = dln * g_ref[...]
        da_ref[...] = rstd * (dxh - jnp.mean(dxh, axis=-1, keepdims=True)
                              - xhat * jnp.mean(dxh * xhat, axis=-1, keepdims=True))

    return pl.pallas_call(
        body, name=name, grid=(T // ROW_T,), in_specs=[_rows(T), _rows(T), _vec(), _vec()],
        out_specs=(_rows(T), _vec(), _vec()),
        out_shape=(jax.ShapeDtypeStruct((T, D), F32), jax.ShapeDtypeStruct((1, D), F32), jax.ShapeDtypeStruct((1, D), F32)),
        compiler_params=_cparams(("arbitrary",)))(ac, dya, g, b)


def _gate_rms_fwd(y, z, g, name):
    T = y.shape[0]

    def body(y_ref, z_ref, g_ref, o_ref):
        zv = z_ref[...]
        yg = y_ref[...] * (zv * _sig(zv))
        r = lax.rsqrt(jnp.mean(yg * yg, axis=-1, keepdims=True) + LN_EPS)
        o_ref[...] = (yg * r * g_ref[...]).astype(BF16)

    return pl.pallas_call(
        body, name=name, grid=(T // ROW_T,), in_specs=[_rows(T), _rows(T), _vec()], out_specs=_rows(T),
        out_shape=jax.ShapeDtypeStruct((T, D), BF16), compiler_params=_cparams(("parallel",)))(y, z, g)


def _gate_rms_bwd(y, z, dout, g, name):
    T = y.shape[0]

    def body(y_ref, z_ref, d_ref, g_ref, dy_ref, dz_ref, dg_ref):
        i = pl.program_id(0)

        @pl.when(i == 0)
        def _():
            dg_ref[...] = jnp.zeros_like(dg_ref)

        yv, zv, dv = y_ref[...], z_ref[...], d_ref[...]
        s = _sig(zv)
        sz = zv * s
        yg = yv * sz
        r = lax.rsqrt(jnp.mean(yg * yg, axis=-1, keepdims=True) + LN_EPS)
        dg_ref[...] += jnp.sum(dv * yg * r, axis=0, keepdims=True)
        dn = dv * g_ref[...]
        dyg = r * dn - yg * (r * r * r) * jnp.mean(dn * yg, axis=-1, keepdims=True)
        dy_ref[...] = dyg * sz
        dz_ref[...] = dyg * yv * s * (1.0 + zv * (1.0 - s))

    return pl.pallas_call(
        body, name=name, grid=(T // ROW_T,), in_specs=[_rows(T), _rows(T), _rows(T), _vec()],
        out_specs=(_rows(T), _rows(T), _vec()),
        out_shape=(jax.ShapeDtypeStruct((T, D), F32), jax.ShapeDtypeStruct((T, D), F32), jax.ShapeDtypeStruct((1, D), F32)),
        compiler_params=_cparams(("arbitrary",)))(y, z, dout, g)


def _loss_head(y, target, name):
    T = y.shape[0]

    def body(y_ref, t_ref, s_ref, d_ref):
        i = pl.program_id(0)

        @pl.when(i == 0)
        def _():
            s_ref[...] = jnp.zeros_like(s_ref)

        err = y_ref[...] - t_ref[...]
        s_ref[...] += jnp.sum(jnp.sum(err * err, axis=1, keepdims=True), axis=0, keepdims=True)
        d_ref[...] = err * (1.0 / D)

    return pl.pallas_call(
        body, name=name, grid=(T // ROW_T,), in_specs=[_rows(T), _rows(T)],
        out_specs=(pl.BlockSpec((SUBLANES, LANES), lambda i: (0, 0)), _rows(T)),
        out_shape=(jax.ShapeDtypeStruct((SUBLANES, LANES), F32), jax.ShapeDtypeStruct((T, D), F32)),
        compiler_params=_cparams(("arbitrary",)))(y, target)


def _taps_fwd(pad_ref, w_ref, K, base):
    off = CONV_PAD - (K - 1)
    acc = w_ref[0:1, :] * pad_ref[pl.ds(base + off, CONV_R), :]
    for k in range(1, K):
        acc = acc + w_ref[k:k + 1, :] * pad_ref[pl.ds(base + off + k, CONV_R), :]
    return acc


def _taps_bwd(padd_ref, w_ref, K, base):
    acc = w_ref[0:1, :] * padd_ref[pl.ds(base + (K - 1), CONV_R), :]
    for k in range(1, K):
        acc = acc + w_ref[k:k + 1, :] * padd_ref[pl.ds(base + (K - 1) - k, CONV_R), :]
    return acc


def _f32(ref, rows):
    return ref[rows, :].astype(F32)


def _fold8(v):
    return v.reshape(CONV_R // SUBLANES, SUBLANES, v.shape[-1]).sum(0)


def _wgrad_acc(dw_ref, pad_ref, d, K, base):
    off = CONV_PAD - (K - 1)
    for k in range(K):
        dw_ref[k * SUBLANES:(k + 1) * SUBLANES, :] += _fold8(d * pad_ref[pl.ds(base + off + k, CONV_R), :])


def _loop_rows(T, fn):
    def step(r, carry):
        fn(pl.multiple_of(r * CONV_R, CONV_R))
        return carry
    lax.fori_loop(0, T // CONV_R, step, 0)


def _col(T, off_blocks=0, rows=None):
    return pl.BlockSpec((T if rows is None else rows, LANES), lambda j: (0, j + off_blocks))


def _conv_call(body, name, T, n_tiles, in_specs, out_specs, out_shape, n_pad, n_padd=0):
    scratch = [pltpu.VMEM((T + CONV_PAD, LANES), F32)] * (n_pad + n_padd)
    return pl.pallas_call(body, name=name, grid=(n_tiles,), in_specs=in_specs, out_specs=out_specs, out_shape=out_shape,
                          scratch_shapes=scratch, compiler_params=_cparams(("parallel",)))


def _zero_head(ref):
    ref[0:CONV_PAD, :] = jnp.zeros((CONV_PAD, LANES), F32)


def _zero_tail(ref, T):
    ref[T:T + CONV_PAD, :] = jnp.zeros((CONV_PAD, LANES), F32)


def _sds(shape, dtype=F32):
    return jax.ShapeDtypeStruct(shape, dtype)


def _conv_a_fwd(ua, w, b, name):
    T = ua.shape[0]
    K, nt = CONV_A, D // LANES

    def body(al_ref, ag_ref, w_ref, b_ref, o_ref, pad_ref):
        _zero_head(pad_ref)

        def pre(base):
            rows = pl.ds(base, CONV_R)
            pad_ref[pl.ds(base + CONV_PAD, CONV_R), :] = _f32(al_ref, rows) * _sig(_f32(ag_ref, rows))
        _loop_rows(T, pre)

        def main(base):
            o_ref[pl.ds(base, CONV_R), :] = _taps_fwd(pad_ref, w_ref, K, base) + b_ref[...]
        _loop_rows(T, main)

    return _conv_call(body, name, T, nt, [_col(T), _col(T, nt), _col(T, rows=K), _col(T, rows=1)], _col(T),
                      _sds((T, D)), 1)(ua, ua, w, b)


def _conv_a_bwd(ua, w, dac, name):
    T = ua.shape[0]
    K, nt = CONV_A, D // LANES

    def body(al_ref, ag_ref, w_ref, d_ref, dal_ref, dag_ref, dw_ref, db_ref, pad_ref, padd_ref):
        _zero_head(pad_ref)
        _zero_tail(padd_ref, T)
        dw_ref[...] = jnp.zeros_like(dw_ref)
        db_ref[...] = jnp.zeros_like(db_ref)

        def pre(base):
            rows = pl.ds(base, CONV_R)
            pad_ref[pl.ds(base + CONV_PAD, CONV_R), :] = _f32(al_ref, rows) * _sig(_f32(ag_ref, rows))
            padd_ref[rows, :] = d_ref[rows, :]
        _loop_rows(T, pre)

        def main(base):
            rows = pl.ds(base, CONV_R)
            d = d_ref[rows, :]
            _wgrad_acc(dw_ref, pad_ref, d, K, base)
            db_ref[...] += _fold8(d)
            da = _taps_bwd(padd_ref, w_ref, K, base)
            al, s = _f32(al_ref, rows), _sig(_f32(ag_ref, rows))
            dal_ref[rows, :] = da * s
            dag_ref[rows, :] = da * al * s * (1.0 - s)
        _loop_rows(T, main)

    return _conv_call(body, name, T, nt, [_col(T), _col(T, nt), _col(T, rows=K), _col(T)],
                      (_col(T), _col(T), _col(T, rows=K * SUBLANES), _col(T, rows=SUBLANES)),
                      (_sds((T, D)), _sds((T, D)), _sds((K * SUBLANES, D)), _sds((SUBLANES, D))), 1, 1)(ua, ua, w, dac)


def _conv_b_fwd(xu, w, b, name):
    T, C = xu.shape
    K, nt = CONV_B, C // LANES

    def body(x_ref, w_ref, b_ref, o_ref, pad_ref):
        _zero_head(pad_ref)
        pad_ref[CONV_PAD:CONV_PAD + T, :] = x_ref[...].astype(F32)

        def main(base):
            hc = _taps_fwd(pad_ref, w_ref, K, base) + b_ref[...]
            o_ref[pl.ds(base, CONV_R), :] = hc * _sig(hc)
        _loop_rows(T, main)

    return _conv_call(body, name, T, nt, [_col(T), _col(T, rows=K), _col(T, rows=1)], _col(T), _sds((T, C)), 1)(xu, w, b)


def _conv_b_bwd(xu, w, b, dxs, dbs, dcs, name):
    T, C = xu.shape
    K, nt = CONV_B, C // LANES
    nx, nb = dxs.shape[1] // LANES, dbs.shape[1] // LANES

    def body(x_ref, w_ref, b_ref, d1_ref, d2_ref, d3_ref, dx_ref, dw_ref, db_ref, pad_ref, padd_ref):
        j = pl.program_id(0)
        _zero_head(pad_ref)
        _zero_tail(padd_ref, T)
        dw_ref[...] = jnp.zeros_like(dw_ref)
        db_ref[...] = jnp.zeros_like(db_ref)
        pad_ref[CONV_PAD:CONV_PAD + T, :] = x_ref[...].astype(F32)

        def pre(base):
            rows = pl.ds(base, CONV_R)
            hc = _taps_fwd(pad_ref, w_ref, K, base) + b_ref[...]
            s = _sig(hc)
            d = jnp.where(j < nx, d1_ref[rows, :], jnp.where(j < nx + nb, d2_ref[rows, :], d3_ref[rows, :]))
            padd_ref[rows, :] = d * s * (1.0 + hc * (1.0 - s))
        _loop_rows(T, pre)

        def main(base):
            d = padd_ref[pl.ds(base, CONV_R), :]
            _wgrad_acc(dw_ref, pad_ref, d, K, base)
            db_ref[...] += _fold8(d)
            dx_ref[pl.ds(base, CONV_R), :] = _taps_bwd(padd_ref, w_ref, K, base)
        _loop_rows(T, main)

    def piece(lo, n):
        return pl.BlockSpec((T, LANES), lambda j: (0, jnp.clip(j - lo, 0, n - 1)))

    return _conv_call(body, name, T, nt,
                      [_col(T), _col(T, rows=K), _col(T, rows=1), piece(0, nx), piece(nx, nb), piece(nx + nb, nt - nx - nb)],
                      (_col(T), _col(T, rows=K * SUBLANES), _col(T, rows=SUBLANES)),
                      (_sds((T, C)), _sds((K * SUBLANES, C)), _sds((SUBLANES, C))), 1, 1)(xu, w, b, dxs, dbs, dcs)


def _conv_c_fwd(uo, w, name):
    T = uo.shape[0]
    K, nt = CONV_C, D // LANES

    def body(bg_ref, cg_ref, v_ref, w_ref, o_ref, pad_ref):
        _zero_head(pad_ref)
        pad_ref[CONV_PAD:CONV_PAD + T, :] = cg_ref[...].astype(F32) * v_ref[...].astype(F32)

        def main(base):
            rows = pl.ds(base, CONV_R)
            o_ref[rows, :] = (_f32(bg_ref, rows) * _taps_fwd(pad_ref, w_ref, K, base)).astype(BF16)
        _loop_rows(T, main)

    return _conv_call(body, name, T, nt, [_col(T), _col(T, nt), _col(T, 2 * nt), _col(T, rows=K)], _col(T),
                      _sds((T, D), BF16), 1)(uo, uo, uo, w)


def _conv_c_bwd(uo, w, dsc, name):
    T = uo.shape[0]
    K, nt = CONV_C, D // LANES

    def body(bg_ref, cg_ref, v_ref, w_ref, d_ref, dbg_ref, dcg_ref, dv_ref, dw_ref, pad_ref, padd_ref):
        _zero_head(pad_ref)
        _zero_tail(padd_ref, T)
        dw_ref[...] = jnp.zeros_like(dw_ref)
        pad_ref[CONV_PAD:CONV_PAD + T, :] = cg_ref[...].astype(F32) * v_ref[...].astype(F32)

        def pre(base):
            rows = pl.ds(base, CONV_R)
            d = d_ref[rows, :]
            dbg_ref[rows, :] = (d * _taps_fwd(pad_ref, w_ref, K, base)).astype(BF16)
            padd_ref[rows, :] = d * _f32(bg_ref, rows)
        _loop_rows(T, pre)

        def main(base):
            rows = pl.ds(base, CONV_R)
            _wgrad_acc(dw_ref, pad_ref, padd_ref[rows, :], K, base)
            dq = _taps_bwd(padd_ref, w_ref, K, base)
            dcg_ref[rows, :] = (dq * _f32(v_ref, rows)).astype(BF16)
            dv_ref[rows, :] = (dq * _f32(cg_ref, rows)).astype(BF16)
        _loop_rows(T, main)

    return _conv_call(body, name, T, nt, [_col(T), _col(T, nt), _col(T, 2 * nt), _col(T, rows=K), _col(T)],
                      (_col(T), _col(T), _col(T), _col(T, rows=K * SUBLANES)),
                      (_sds((T, D), BF16), _sds((T, D), BF16), _sds((T, D), BF16), _sds((K * SUBLANES, D))), 1, 1)(uo, uo, uo, w, dsc)


def _conv_f_fwd(up, w, b, name):
    T = up.shape[0]
    K, nt = CONV_F, D_FF // LANES

    def body(u1_ref, u2_ref, w1_ref, w2_ref, b1_ref, b2_ref, o_ref, pad1_ref, pad2_ref):
        _zero_head(pad1_ref)
        _zero_head(pad2_ref)
        pad1_ref[CONV_PAD:CONV_PAD + T, :] = u1_ref[...].astype(F32)
        pad2_ref[CONV_PAD:CONV_PAD + T, :] = u2_ref[...].astype(F32)

        def main(base):
            h1 = _taps_fwd(pad1_ref, w1_ref, K, base) + b1_ref[...]
            h2 = _taps_fwd(pad2_ref, w2_ref, K, base) + b2_ref[...]
            o_ref[pl.ds(base, CONV_R), :] = (h1 * _sig(h1) * h2).astype(BF16)
        _loop_rows(T, main)

    return _conv_call(body, name, T, nt,
                      [_col(T), _col(T, nt), _col(T, rows=K), _col(T, nt, rows=K), _col(T, rows=1), _col(T, nt, rows=1)],
                      _col(T), _sds((T, D_FF), BF16), 2)(up, up, w, w, b, b)


def _conv_f_bwd(up, w, b, dact, name):
    T = up.shape[0]
    K, nt = CONV_F, D_FF // LANES

    def body(u1_ref, u2_ref, w1_ref, w2_ref, b1_ref, b2_ref, d_ref, du1_ref, du2_ref, dw1_ref, dw2_ref, db1_ref, db2_ref,
             pad1_ref, pad2_ref, padd1_ref, padd2_ref):
        _zero_head(pad1_ref)
        _zero_head(pad2_ref)
        _zero_tail(padd1_ref, T)
        _zero_tail(padd2_ref, T)
        for r in (dw1_ref, dw2_ref, db1_ref, db2_ref):
            r[...] = jnp.zeros_like(r)
        pad1_ref[CONV_PAD:CONV_PAD + T, :] = u1_ref[...].astype(F32)
        pad2_ref[CONV_PAD:CONV_PAD + T, :] = u2_ref[...].astype(F32)

        def pre(base):
            rows = pl.ds(base, CONV_R)
            h1 = _taps_fwd(pad1_ref, w1_ref, K, base) + b1_ref[...]
            h2 = _taps_fwd(pad2_ref, w2_ref, K, base) + b2_ref[...]
            s = _sig(h1)
            d = _f32(d_ref, rows)
            padd1_ref[rows, :] = d * h2 * s * (1.0 + h1 * (1.0 - s))
            padd2_ref[rows, :] = d * h1 * s
        _loop_rows(T, pre)

        def main(base):
            rows = pl.ds(base, CONV_R)
            d1, d2 = padd1_ref[rows, :], padd2_ref[rows, :]
            _wgrad_acc(dw1_ref, pad1_ref, d1, K, base)
            _wgrad_acc(dw2_ref, pad2_ref, d2, K, base)
            db1_ref[...] += _fold8(d1)
            db2_ref[...] += _fold8(d2)
            du1_ref[rows, :] = _taps_bwd(padd1_ref, w1_ref, K, base).astype(BF16)
            du2_ref[rows, :] = _taps_bwd(padd2_ref, w2_ref, K, base).astype(BF16)
        _loop_rows(T, main)

    wrow, brow = _col(T, rows=K * SUBLANES), _col(T, rows=SUBLANES)
    return _conv_call(body, name, T, nt,
                      [_col(T), _col(T, nt), _col(T, rows=K), _col(T, nt, rows=K), _col(T, rows=1), _col(T, nt, rows=1), _col(T)],
                      (_col(T), _col(T), wrow, wrow, brow, brow),
                      (_sds((T, D_FF), BF16), _sds((T, D_FF), BF16), _sds((K * SUBLANES, D_FF)), _sds((K * SUBLANES, D_FF)),
                       _sds((SUBLANES, D_FF)), _sds((SUBLANES, D_FF))), 2, 2)(up, up, w, w, b, b, dact)


def _dot(a, b, dims="nn"):
    return lax.dot_general(a.astype(BF16), b.astype(BF16), _DIMS[dims], preferred_element_type=F32)


def _dot_mask(mask, v, mask_left):
    mb = mask.astype(BF16)
    hi = v.astype(BF16)
    r1 = v - hi.astype(F32)
    mid = r1.astype(BF16)
    lo = (r1 - mid.astype(F32)).astype(BF16)
    d = [jnp.dot(mb, t, preferred_element_type=F32) if mask_left else jnp.dot(t, mb, preferred_element_type=F32) for t in (hi, mid, lo)]
    return (d[0] + d[1]) + d[2]


def _ssd_small(xcr_ref, xrr_ref, bc_ref, br_ref, ac_ref, ar_ref):
    Q = SSD_Q
    li = lax.broadcasted_iota(jnp.int32, (Q, Q), 0)
    si = lax.broadcasted_iota(jnp.int32, (Q, Q), 1)
    tril = li >= si
    dtc = jax.nn.softplus(xcr_ref[...] + bc_ref[...])
    dtr = jax.nn.softplus(xrr_ref[...] + br_ref[...])
    cumc = _dot_mask(tril, dtc * ac_ref[...], True)
    cumr = _dot_mask(li <= si, dtr * ar_ref[...], False)
    return tril, dtc, dtr, cumc, cumr


def _ssd_specs(nc, rev):
    Q = SSD_Q
    cc = (lambda c: nc - 1 - c) if rev else (lambda c: c)
    x_spec = pl.BlockSpec((Q, 2 * LANES), lambda g, c: (cc(c), g))
    b_spec = pl.BlockSpec((Q, LANES), lambda g, c: (cc(c), 8 + g))
    c_spec = pl.BlockSpec((Q, LANES), lambda g, c: (cc(c), 12 + g))
    colm = pl.BlockSpec((None, Q, LANES), lambda g, c: (g, cc(c), 0))
    rowm = pl.BlockSpec((None, SUBLANES, Q), lambda g, c: (g, 0, cc(c)))
    colv = pl.BlockSpec((None, 1, LANES), lambda g, c: (g, 0, 0))
    rowv = pl.BlockSpec((None, SUBLANES, 1), lambda g, c: (g, 0, 0))
    st_spec = pl.BlockSpec((None, None, 2 * LANES, N_STATE), lambda g, c: (cc(c), g, 0, 0))
    return x_spec, b_spec, c_spec, colm, rowm, colv, rowv, st_spec


def _ssd_fwd(xc, raw_col, raw_row, bias_col, bias_row, a_col, a_row, dskip, name):
    T = xc.shape[0]
    Q = SSD_Q
    nc = T // Q
    x_spec, b_spec, c_spec, colm, rowm, colv, rowv, st_spec = _ssd_specs(nc, False)

    def body(dk_ref, x_ref, b_ref, c_ref, xcr_ref, xrr_ref, bc_ref, br_ref, ac_ref, ar_ref, y_ref, st_ref, h_ref):
        g = pl.program_id(0)

        @pl.when(pl.program_id(1) == 0)
        def _():
            h_ref[...] = jnp.zeros_like(h_ref)

        tril, dtc, dtr, cumc, cumr = _ssd_small(xcr_ref, xrr_ref, bc_ref, br_ref, ac_ref, ar_ref)
        Bm, Cm = b_ref[...], c_ref[...]
        S = _dot(Cm, Bm, "nt")
        lo = lax.broadcasted_iota(jnp.int32, (Q, LANES), 1) < HEAD_P
        rlo = lax.broadcasted_iota(jnp.int32, (LANES, N_STATE), 0) < HEAD_P
        st_ref[...] = h_ref[...]
        clast = cumc[Q - 1:Q, :]
        for pr in range(2):
            cols = slice(pr * LANES, (pr + 1) * LANES)
            xp = x_ref[:, cols]
            yd = jnp.zeros((Q, LANES), F32)
            for q in range(2):
                hh = 2 * pr + q
                seg = cumc[:, hh:hh + 1] - cumr[hh:hh + 1, :]
                lm = jnp.where(tril, jnp.exp(jnp.where(tril, seg, 0.0)), 0.0)
                w = S * lm * dtr[hh:hh + 1, :]
                xm = jnp.where(lo if q == 0 else jnp.logical_not(lo), xp, 0.0)
                yd = yd + _dot(w, xm)
            h0, h1 = 2 * pr, 2 * pr + 1
            c0, c1 = cumc[:, h0:h0 + 1], cumc[:, h1:h1 + 1]
            e_pair = jnp.where(lo, jnp.exp(c0), jnp.exp(c1))
            hp = h_ref[cols, :]
            ch = _dot(Cm, hp, "nt")
            dsk = jnp.where(lo, dk_ref[4 * g + h0], dk_ref[4 * g + h1])
            y_ref[:, cols] = yd + e_pair * ch + dsk * xp
            cl0, cl1 = clast[:, h0:h0 + 1], clast[:, h1:h1 + 1]
            sdec = jnp.where(lo, jnp.exp(cl0 - c0) * dtc[:, h0:h0 + 1], jnp.exp(cl1 - c1) * dtc[:, h1:h1 + 1])
            decrow = jnp.where(rlo, jnp.exp(cl0), jnp.exp(cl1))
            h_ref[cols, :] = hp * decrow + _dot(xp * sdec, Bm, "tn")

    smem = pl.BlockSpec(memory_space=pltpu.SMEM)
    return pl.pallas_call(
        body, name=name, grid=(N_GROUPS, nc),
        in_specs=[smem, x_spec, b_spec, c_spec, colm, rowm, colv, rowv, colv, rowv],
        out_specs=(x_spec, st_spec),
        out_shape=(_sds((T, D)), _sds((nc, N_GROUPS, 2 * LANES, N_STATE))),
        scratch_shapes=[pltpu.VMEM((2 * LANES, N_STATE), F32)],
        compiler_params=_cparams(("parallel", "arbitrary")))(dskip, xc, xc, xc, raw_col, raw_row, bias_col, bias_row, a_col, a_row)


def _ssd_bwd(xc, raw_col, raw_row, bias_col, bias_row, a_col, a_row, dskip, states, dy, name):
    T = xc.shape[0]
    Q = SSD_Q
    nc = T // Q
    x_spec, b_spec, c_spec, colm, rowm, colv, rowv, st_spec = _ssd_specs(nc, True)
    bo_spec = pl.BlockSpec((Q, LANES), lambda g, c: (nc - 1 - c, g))
    dd_spec = pl.BlockSpec((None, None, SUBLANES, 2 * LANES), lambda g, c: (nc - 1 - c, g, 0, 0))

    def body(dk_ref, x_ref, b_ref, c_ref, xcr_ref, xrr_ref, bc_ref, br_ref, ac_ref, ar_ref, st_ref, dy_ref,
             dx_ref, db_ref, dc_ref, sq_ref, cms_ref, ddac_ref, ddar_ref, dd_ref, dh_ref):
        g = pl.program_id(0)

        @pl.when(pl.program_id(1) == 0)
        def _():
            dh_ref[...] = jnp.zeros_like(dh_ref)

        tril, dtc, dtr, cumc, cumr = _ssd_small(xcr_ref, xrr_ref, bc_ref, br_ref, ac_ref, ar_ref)
        Bm, Cm = b_ref[...], c_ref[...]
        S = _dot(Cm, Bm, "nt")
        lane = lax.broadcasted_iota(jnp.int32, (Q, LANES), 1)
        sub = lax.broadcasted_iota(jnp.int32, (SUBLANES, Q), 0)
        rowi = lax.broadcasted_iota(jnp.int32, (Q, LANES), 0)
        lo = lane < HEAD_P
        rlo = lax.broadcasted_iota(jnp.int32, (LANES, N_STATE), 0) < HEAD_P
        clast = cumc[Q - 1:Q, :]
        ds_g = jnp.zeros((Q, Q), F32)
        dcm = jnp.zeros((Q, N_STATE), F32)
        dbm = jnp.zeros((Q, N_STATE), F32)
        dcum_col = jnp.zeros((Q, LANES), F32)
        dcum_row = jnp.zeros((SUBLANES, Q), F32)
        sq_col = jnp.zeros((Q, LANES), F32)
        cms_row = jnp.zeros((SUBLANES, Q), F32)
        for pr in range(2):
            cols = slice(pr * LANES, (pr + 1) * LANES)
            xp, dyp = x_ref[:, cols], dy_ref[:, cols]
            hin, dhp = st_ref[cols, :], dh_ref[cols, :]
            h0, h1 = 2 * pr, 2 * pr + 1
            c0, c1 = cumc[:, h0:h0 + 1], cumc[:, h1:h1 + 1]
            cl0, cl1 = clast[:, h0:h0 + 1], clast[:, h1:h1 + 1]
            e_pair = jnp.where(lo, jnp.exp(c0), jnp.exp(c1))
            edec = jnp.where(lo, jnp.exp(cl0 - c0), jnp.exp(cl1 - c1))
            dt_pair = jnp.where(lo, dtc[:, h0:h0 + 1], dtc[:, h1:h1 + 1])
            sdec = edec * dt_pair
            ch = _dot(Cm, hin, "nt")
            xb = _dot(Bm, dhp, "nt")
            dye = dyp * e_pair
            t1 = dye * ch
            t2 = xp * xb * edec
            hh_prod = dhp * hin
            dsk = jnp.where(lo, dk_ref[4 * g + h0], dk_ref[4 * g + h1])
            dxp = sdec * xb + dsk * dyp
            for q in range(2):
                hh = 2 * pr + q
                mine = lo if q == 0 else jnp.logical_not(lo)
                seg = cumc[:, hh:hh + 1] - cumr[hh:hh + 1, :]
                lm = jnp.where(tril, jnp.exp(jnp.where(tril, seg, 0.0)), 0.0)
                dtrow = dtr[hh:hh + 1, :]
                w = S * lm * dtrow
                dym = jnp.where(mine, dyp, 0.0)
                gl = _dot(dym, xp, "nt") * lm
                ds_g = ds_g + gl * dtrow
                ms = gl * S
                m = ms * dtrow
                dxp = dxp + _dot(w, dym, "tn")
                cms_row = jnp.where(sub == hh, jnp.sum(ms, axis=0, keepdims=True), cms_row)
                dcum_row = jnp.where(sub == hh, -jnp.sum(m, axis=0, keepdims=True), dcum_row)
                t1h = jnp.sum(jnp.where(mine, t1, 0.0), axis=1, keepdims=True)
                sqh = jnp.sum(jnp.where(mine, t2, 0.0), axis=1, keepdims=True)
                sth = sqh * dtc[:, hh:hh + 1]
                rmine = rlo if q == 0 else jnp.logical_not(rlo)
                hsum = jnp.sum(jnp.sum(jnp.where(rmine, hh_prod, 0.0), axis=1, keepdims=True), axis=0, keepdims=True)
                last = jnp.sum(sth, axis=0, keepdims=True) + jnp.exp(clast[:, hh:hh + 1]) * hsum
                dcol = jnp.sum(m, axis=1, keepdims=True) + t1h - sth
                dcum_col = jnp.where(lane == hh, dcol + jnp.where(rowi == Q - 1, last, 0.0), dcum_col)
                sq_col = jnp.where(lane == hh, sqh, sq_col)
            dcm = dcm + _dot(dye, hin)
            dbm = dbm + _dot(xp * sdec, dhp)
            decrow = jnp.where(rlo, jnp.exp(cl0), jnp.exp(cl1))
            dh_ref[cols, :] = dhp * decrow + _dot(dye, Cm, "tn")
            dx_ref[:, cols] = dxp
            dd_ref[:, cols] = jnp.broadcast_to(jnp.sum(dyp * xp, axis=0, keepdims=True), (SUBLANES, LANES))
        dc_ref[...] = dcm + _dot(ds_g, Bm)
        db_ref[...] = dbm + _dot(ds_g, Cm, "tn")
        li = lax.broadcasted_iota(jnp.int32, (Q, Q), 0)
        si = lax.broadcasted_iota(jnp.int32, (Q, Q), 1)
        ddac_ref[...] = _dot_mask(li <= si, dcum_col, True)
        ddar_ref[...] = _dot_mask(tril, dcum_row, False)
        sq_ref[...] = sq_col
        cms_ref[...] = cms_row

    smem = pl.BlockSpec(memory_space=pltpu.SMEM)
    return pl.pallas_call(
        body, name=name, grid=(N_GROUPS, nc),
        in_specs=[smem, x_spec, b_spec, c_spec, colm, rowm, colv, rowv, colv, rowv, st_spec, x_spec],
        out_specs=(x_spec, bo_spec, bo_spec, colm, rowm, colm, rowm, dd_spec),
        out_shape=(_sds((T, D)), _sds((T, D // 2)), _sds((T, D // 2)), _sds((N_GROUPS, T, LANES)), _sds((N_GROUPS, SUBLANES, T)),
                   _sds((N_GROUPS, T, LANES)), _sds((N_GROUPS, SUBLANES, T)), _sds((nc, N_GROUPS, SUBLANES, 2 * LANES))),
        scratch_shapes=[pltpu.VMEM((2 * LANES, N_STATE), F32)],
        compiler_params=_cparams(("parallel", "arbitrary")))(dskip, xc, xc, xc, raw_col, raw_row, bias_col, bias_row, a_col, a_row,
                                                            states, dy)


def _adam_math(wv, gv, mv, vv):
    c1 = 1.0 - ADAM_B1 ** ADAM_STEP
    c2 = 1.0 - ADAM_B2 ** ADAM_STEP
    mn = ADAM_B1 * mv + (1.0 - ADAM_B1) * gv
    vn = ADAM_B2 * vv + (1.0 - ADAM_B2) * (gv * gv)
    return -ADAM_LR * ((mn / c1) / (jnp.sqrt(vn / c2) + ADAM_EPS) + ADAM_WD * wv), mn, vn


def _adamw_layers(w, g, m, v, l0, Lg, bufs, name):
    L, As, Bs = w.shape
    tr = _tile(As, [], (256, 352, 128))
    has_bufs = bufs is not None

    def body(*refs):
        w_ref, g_ref, m_ref, v_ref = refs[:4]
        d_ref, mo_ref, vo_ref = refs[4 + 3 * has_bufs:]
        d_ref[...], mo_ref[...], vo_ref[...] = _adam_math(w_ref[...], g_ref[...], m_ref[...], v_ref[...])

    spec = pl.BlockSpec((None, tr, Bs), lambda l, i: (l + l0, i, 0))
    args = (w, g, m, v) + (tuple(bufs) if has_bufs else ())
    return pl.pallas_call(
        body, name=name, grid=(Lg, As // tr), in_specs=[spec] * 4 + [_ANY] * (3 * has_bufs), out_specs=(spec,) * 3,
        out_shape=(_sds((L, As, Bs)),) * 3, input_output_aliases={4: 0, 5: 1, 6: 2} if has_bufs else {},
        compiler_params=_cparams(("parallel", "parallel")))(*args)


def _adamw(w, g, m, v, name):
    shape = w.shape
    cols = shape[-1]
    w2, g2, m2, v2 = (t.reshape(-1, cols) for t in (w, g, m, v))
    rows = w2.shape[0]
    tr = 256 if (rows % 256 == 0 and rows > 256) else rows
    c1 = 1.0 - ADAM_B1 ** ADAM_STEP
    c2 = 1.0 - ADAM_B2 ** ADAM_STEP

    def body(w_ref, g_ref, m_ref, v_ref, d_ref, mo_ref, vo_ref):
        gv = g_ref[...]
        mn = ADAM_B1 * m_ref[...] + (1.0 - ADAM_B1) * gv
        vn = ADAM_B2 * v_ref[...] + (1.0 - ADAM_B2) * (gv * gv)
        d_ref[...] = -ADAM_LR * ((mn / c1) / (jnp.sqrt(vn / c2) + ADAM_EPS) + ADAM_WD * w_ref[...])
        mo_ref[...] = mn
        vo_ref[...] = vn

    spec = pl.BlockSpec((tr, cols), lambda i: (i, 0))
    out = pl.pallas_call(body, name=name, grid=(rows // tr,), in_specs=[spec] * 4, out_specs=(spec,) * 3,
                         out_shape=(_sds((rows, cols)),) * 3, compiler_params=_cparams(("parallel",)))(w2, g2, m2, v2)
    return tuple(o.reshape(shape) for o in out)


def _place():
    x, y, c = lax.axis_index("x"), lax.axis_index("y"), lax.axis_index("c")
    chips = [(1 - x, y), (x, 1 - y), (1 - x, 1 - y)]
    return x, y, c, chips


_ANY = pl.BlockSpec(memory_space=pl.ANY)


TENSORS = (("e_w_in", "row", 2, 4096, 1284, 1024), ("e_w_out", "row", 2, 2048, 1024, 512), ("o_w_in", "col", 2, 1024, 3072, 768),
           ("o_w_out", "row", 2, 1024, 1024, 256), ("f_w_up", "col", 4, 1024, 5632, 1408), ("f_w_down", "row", 4, 2816, 1024, 704),
           ("ple_w_proj", "col", 4, 256, 1024, 256), ("ple_w_gate", "row", 4, 1024, 1024, 256))
MIX, FFN = "mix", "ffn"
W_GROUPS = (((0, MIX),), ((0, FFN),), ((1, MIX), (1, FFN)), ((2, MIX), (2, FFN), (3, MIX), (3, FFN)))
G_GROUPS = (((3, FFN), (3, MIX), (2, FFN), (2, MIX), (1, FFN), (1, MIX)), ((0, FFN),), ((0, MIX),))


def _tensor_layer(name, layer):
    if name.startswith("e_"):
        return layer // 2 if layer % 2 == 0 else None
    if name.startswith("o_"):
        return layer // 2 if layer % 2 == 1 else None
    return layer


def _part(name):
    return MIX if name.startswith(("e_", "o_")) else FFN


def _group_items(members):
    items = []
    for name, kind, L, A, B, n in TENSORS:
        tls = sorted(t for t in (_tensor_layer(name, l) for l, part in members if part == _part(name)) if t is not None)
        if tls:
            assert tls == list(range(tls[0], tls[0] + len(tls)))
            items.append((name, kind, len(tls), A, B, n, tls[0]))
    return items


def _hwin(ref, it, k, h):
    name, kind, Lg, A, B, n, l0 = it
    if kind == "row":
        return ref.at[:, pl.ds(pl.multiple_of(k * n + h * (n // 2), 16), n // 2), :]
    return ref.at[:, pl.ds(pl.multiple_of(h * (A // 2), 16), A // 2), pl.ds(pl.multiple_of(k * n, LANES), n)]


def _shard_dims(kind, A, B, n):
    return (n, B) if kind == "row" else (A, n)


def _cast_into(w, it, me):
    name, kind, Lg, A, B, n, l0 = it
    As, Bs = _shard_dims(kind, A, B, n)

    def body(me_ref, w_ref, o_ref):
        o_ref[...] = w_ref[...].astype(BF16)

    omap = (lambda l, m: (l, m[0], 0)) if kind == "row" else (lambda l, m: (l, 0, m[0]))
    grid_spec = pltpu.PrefetchScalarGridSpec(
        num_scalar_prefetch=1, grid=(Lg,), in_specs=[pl.BlockSpec((None, As, Bs), lambda l, m: (l + l0, 0, 0))],
        out_specs=pl.BlockSpec((None, As, Bs), omap))
    return pl.pallas_call(body, name=f"cast_{name}_{l0}", grid_spec=grid_spec, out_shape=_sds((Lg, A, B), BF16),
                          compiler_params=_cparams(("parallel",)))(me, w.reshape(-1, As, Bs))


_HBM = pl.BlockSpec(memory_space=pltpu.HBM)
_SEM = pl.BlockSpec(memory_space=pltpu.SEMAPHORE)
_EFFECT = pltpu.SideEffectType.DATAFLOW_SIDE_EFFECTING


def _hbm(a):
    return pltpu.with_memory_space_constraint(a, pltpu.HBM)


def _split_start(thru, n_copies, issue, name, after=None):
    N = len(thru)
    has_after = after is not None

    def body(*refs):
        outs = refs[N + has_after:2 * N + has_after]
        send_sems, recv_sems, token = refs[2 * N + has_after:]
        for cp in issue(outs, send_sems, recv_sems):
            cp.start()
        token[...] = jnp.zeros_like(token)

    out = pl.pallas_call(
        body, name=name, in_specs=[_HBM] * N + ([_ANY] if has_after else []),
        out_specs=(_HBM,) * N + (_SEM, _SEM, pl.BlockSpec(memory_space=pltpu.VMEM)),
        out_shape=tuple(pltpu.HBM(a.shape, a.dtype) for a in thru)
        + (pltpu.SemaphoreType.DMA((n_copies,)), pltpu.SemaphoreType.DMA((n_copies,)), _sds((SUBLANES, LANES))),
        input_output_aliases={t: t for t in range(N)},
        compiler_params=pltpu.CompilerParams(has_side_effects=_EFFECT))(*[_hbm(a) for a in thru], *([after] if has_after else []))
    return list(out[:N]), out[N], out[N + 1], out[N + 2]


def _split_wait(thru, send_sems, recv_sems, after, waits, name):
    N = len(thru)
    after = list(after) if isinstance(after, (list, tuple)) else [after]

    def body(*refs):
        ins = refs[:N]
        for cp, side in waits(ins, refs[N], refs[N + 1]):
            if side == "send":
                cp.wait_send()
            else:
                cp.wait_recv()

    out = pl.pallas_call(
        body, name=name, in_specs=[_HBM] * N + [_SEM, _SEM] + [_ANY] * len(after), out_specs=(_HBM,) * N,
        out_shape=tuple(pltpu.HBM(a.shape, a.dtype) for a in thru), input_output_aliases={t: t for t in range(N)},
        compiler_params=pltpu.CompilerParams(has_side_effects=_EFFECT))(*thru, send_sems, recv_sems, *after)
    return list(out)


def _rcopy(send_sems, recv_sems, k, src, dst, to):
    return pltpu.make_async_remote_copy(src_ref=src, dst_ref=dst, send_sem=send_sems.at[k], recv_sem=recv_sems.at[k],
                                        device_id=to, device_id_type=MESH)


def _gather_copies(items, refs, send_sems, recv_sems, what):
    x, y, c, chips = _place()
    me = 2 * x + y
    out = []
    for t, it in enumerate(items):
        mine = _hwin(refs[t], it, me, c)
        for j, (px, py) in enumerate(chips):
            if what == "start":
                out.append(_rcopy(send_sems, recv_sems, 3 * t + j, mine, mine, (px, py, c)))
            else:
                slot = _hwin(refs[t], it, 2 * px + py, c)
                out.append((_rcopy(send_sems, recv_sems, 3 * t + j, mine, mine, (px, py, c)), "send"))
                out.append((_rcopy(send_sems, recv_sems, 3 * t + j, slot, slot, (px, py, c)), "recv"))
    return out


def _gather_start(fulls, items, name, after=None):
    return _split_start(fulls, 3 * len(items), functools.partial(_gather_copies, items, what="start"), name, after)


def _gather_wait(fulls, send_sems, recv_sems, after, items, name):
    return _split_wait(fulls, send_sems, recv_sems, after, functools.partial(_gather_copies, items, what="wait"), name)


def _gather_fwd(fulls, items, name, ws=None):
    N = len(fulls)
    has_ws = ws is not None

    def body(*refs):
        outs = refs[N + has_ws:2 * N + has_ws]
        rest = refs[2 * N + has_ws:]
        x, y, c, chips = _place()
        me = 2 * x + y
        sib = (x, y, 1 - c)
        if has_ws:
            ws_ref = refs[N]
            WS_ref, send_sems, recv_sems, lsem = rest
            loc = pltpu.make_async_copy(ws_ref, WS_ref.at[me], lsem)
            loc.start()
        else:
            send_sems, recv_sems = rest
        rc = functools.partial(_rcopy, send_sems, recv_sems)
        cps = []
        for t, it in enumerate(items):
            for j, (px, py) in enumerate(chips):
                slot = _hwin(outs[t], it, 2 * px + py, c)
                cps.append(rc(3 * t + j, slot, slot, sib))
        if has_ws:
            cps += [rc(3 * N + j, ws_ref, WS_ref.at[me], (*chip, c)) for j, chip in enumerate(chips)]
        for cp in cps:
            cp.start()
        for t, it in enumerate(items):
            for j, (px, py) in enumerate(chips):
                oslot = _hwin(outs[t], it, 2 * px + py, 1 - c)
                rc(3 * t + j, oslot, oslot, sib).wait_recv()
        if has_ws:
            for j, (px, py) in enumerate(chips):
                sslot = WS_ref.at[2 * px + py]
                rc(3 * N + j, sslot, sslot, sib).wait_recv()
        for cp in cps:
            cp.wait_send()
        if has_ws:
            loc.wait()

    ns = 3 * N + (3 if has_ws else 0)
    out_shape = tuple(_sds(f.shape, f.dtype) for f in fulls)
    scratch = [pltpu.SemaphoreType.DMA((ns,)), pltpu.SemaphoreType.DMA((ns,))]
    args = list(fulls)
    if has_ws:
        out_shape += (_sds((4,) + ws.shape, ws.dtype),)
        scratch.append(pltpu.SemaphoreType.DMA(()))
        args.append(ws)
    out = pl.pallas_call(
        body, name=name, in_specs=[_ANY] * len(args), out_specs=(_ANY,) * len(out_shape), out_shape=out_shape,
        input_output_aliases={t: t for t in range(N)}, scratch_shapes=scratch,
        compiler_params=pltpu.CompilerParams(has_side_effects=True))(*args)
    return (list(out[:N]), out[N]) if has_ws else (list(out), None)


def _half_shape(it):
    name, kind, Lg, A, B, n, l0 = it
    return (Lg, 4, n // 2, B) if kind == "row" else (Lg, A // 2, B)


def _piece_shape(it):
    name, kind, Lg, A, B, n, l0 = it
    return (Lg, n // 2, B) if kind == "row" else (Lg, A // 2, n)


def _swap_grads(gs, items, name):
    N = len(gs)

    def body(*refs):
        g_refs, o_refs = refs[:N], refs[N:2 * N]
        send_sems, recv_sems = refs[2 * N:]
        x, y, c, _ = _place()
        sib = (x, y, 1 - c)
        cps = []
        for t, it in enumerate(items):
            name_, kind, Lg, A, B, n, l0 = it
            if kind == "row":
                for k in range(4):
                    cps.append(_rcopy(send_sems, recv_sems, 4 * t + k, _hwin(g_refs[t], it, k, 1 - c), o_refs[t].at[:, k], sib))
            else:
                src = g_refs[t].at[:, pl.ds(pl.multiple_of((1 - c) * (A // 2), 16), A // 2), :]
                cps.append(_rcopy(send_sems, recv_sems, 4 * t, src, o_refs[t], sib))
        for cp in cps:
            cp.start()
        for cp in cps:
            cp.wait()

    return pl.pallas_call(
        body, name=name, in_specs=[_ANY] * N, out_specs=(_ANY,) * N, out_shape=tuple(_sds(_half_shape(it)) for it in items),
        scratch_shapes=[pltpu.SemaphoreType.DMA((4 * N,)), pltpu.SemaphoreType.DMA((4 * N,))],
        compiler_params=pltpu.CompilerParams(has_side_effects=True))(*gs)


def _add_half(g, ra, it, cvec):
    name, kind, Lg, A, B, n, l0 = it
    if kind == "row":
        blk = (None, n // 2, B)
        grid = (Lg, 4)
        g_spec = pl.BlockSpec(blk, lambda l, k, cr: (l, 2 * k + cr[0], 0))
        h_spec = pl.BlockSpec((None, None, n // 2, B), lambda l, k, cr: (l, k, 0, 0))
    else:
        tr = _tile(A // 2, [], (256, 128))
        nb = (A // 2) // tr
        grid = (Lg, nb)
        g_spec = pl.BlockSpec((None, tr, B), lambda l, i, cr: (l, cr[0] * nb + i, 0))
        h_spec = pl.BlockSpec((None, tr, B), lambda l, i, cr: (l, i, 0))

    def body(c_ref, g_ref, r_ref, o_ref):
        o_ref[...] = (g_ref[...] + r_ref[...]).astype(BF16)

    grid_spec = pltpu.PrefetchScalarGridSpec(num_scalar_prefetch=1, grid=grid, in_specs=[g_spec, h_spec], out_specs=h_spec)
    return pl.pallas_call(body, name=f"addhalf_{name}_{l0}", grid_spec=grid_spec, out_shape=_sds(_half_shape(it), BF16),
                          compiler_params=_cparams(("parallel", "parallel")))(cvec, g, ra)


def _scatter_copies(items, refs, send_sems, recv_sems, what):
    N = len(items)
    x, y, c, chips = _place()
    out = []
    for t, it in enumerate(items):
        name, kind, Lg, A, B, n, l0 = it
        for j, (px, py) in enumerate(chips):
            k = 2 * px + py
            src = refs[t].at[:, k] if kind == "row" else refs[t].at[:, :, pl.ds(pl.multiple_of(k * n, LANES), n)]
            cp = _rcopy(send_sems, recv_sems, 3 * t + j, src, refs[N + t].at[j], (px, py, c))
            if what == "start":
                out.append(cp)
            else:
                out += [(cp, "send"), (cp, "recv")]
    return out


def _scatter_start(ps, items, name):
    lands = [lax.empty((3,) + _piece_shape(it), BF16) for it in items]
    return _split_start(list(ps) + lands, 3 * len(items), functools.partial(_scatter_copies, items, what="start"), name)


def _scatter_wait(thru, send_sems, recv_sems, after, items, name):
    return _split_wait(thru, send_sems, recv_sems, after, functools.partial(_scatter_copies, items, what="wait"), name)


def _sum_own(p, rc, it, mevec, buf):
    name, kind, Lg, A, B, n, l0 = it
    As, Bs = _shard_dims(kind, A, B, n)
    L = [s[2] for s in TENSORS if s[0] == name][0]
    hb = (As // 2, Bs)
    has_buf = buf is not None

    def body(*refs):
        p_ref, r0, r1, r2 = refs[1:5]
        o_ref = refs[5 + has_buf]
        o_ref[...] = ((p_ref[...].astype(F32) + r0[...].astype(F32)) + r1[...].astype(F32)) + r2[...].astype(F32)

    if kind == "row":
        p_spec = pl.BlockSpec((None, None) + hb, lambda l, m: (l, m[0], 0, 0))
    else:
        p_spec = pl.BlockSpec((None,) + hb, lambda l, m: (l, 0, m[0]))
    r_specs = [pl.BlockSpec((None, None) + hb, functools.partial(lambda l, m, j: (j, l, 0, 0), j=j)) for j in range(3)]
    in_specs = [p_spec] + r_specs + ([_ANY] if has_buf else [])
    grid_spec = pltpu.PrefetchScalarGridSpec(num_scalar_prefetch=1, grid=(Lg,), in_specs=in_specs,
                                             out_specs=pl.BlockSpec((None,) + hb, lambda l, m: (l + l0, m[1], 0)))
    args = (mevec, p, rc, rc, rc) + ((buf,) if has_buf else ())
    return pl.pallas_call(body, name=f"sumown_{name}_{l0}", grid_spec=grid_spec, out_shape=_sds((L, As, Bs)),
                          input_output_aliases={5: 0} if has_buf else {}, compiler_params=_cparams(("parallel",)))(*args)


def _join_halves(rs, items, name):
    N = len(rs)

    def body(*refs):
        outs = refs[N:2 * N]
        send_sems, recv_sems = refs[2 * N:]
        x, y, c, _ = _place()
        sib = (x, y, 1 - c)

        def half(t, h):
            name_, kind, Lg, A, B, n, l0 = items[t]
            hr = _shard_dims(kind, A, B, n)[0] // 2
            return outs[t].at[pl.ds(l0, Lg), pl.ds(pl.multiple_of(h * hr, SUBLANES), hr), :]

        cps = [_rcopy(send_sems, recv_sems, t, half(t, c), half(t, c), sib) for t in range(N)]
        for cp in cps:
            cp.start()
        for t in range(N):
            _rcopy(send_sems, recv_sems, t, half(t, 1 - c), half(t, 1 - c), sib).wait_recv()
        for cp in cps:
            cp.wait_send()

    return list(pl.pallas_call(
        body, name=name, in_specs=[_ANY] * N, out_specs=(_ANY,) * N, out_shape=tuple(_sds(r.shape, r.dtype) for r in rs),
        input_output_aliases={t: t for t in range(N)},
        scratch_shapes=[pltpu.SemaphoreType.DMA((N,)), pltpu.SemaphoreType.DMA((N,))],
        compiler_params=pltpu.CompilerParams(has_side_effects=True))(*rs))


def _allgather_small(v):
    m_per, n = v.shape

    def body(x_ref, out_ref, send_sems, recv_sems, local_sem):
        x, y, c, chips = _place()
        me, sibling = (x, y, c), (x, y, 1 - c)

        def rows(px, py, pc):
            return out_ref.at[pl.ds(pl.multiple_of((4 * px + 2 * py + pc) * m_per, SUBLANES), m_per), :]

        def copy(k, block, to, src=None):
            return pltpu.make_async_remote_copy(src_ref=rows(*block) if src is None else src, dst_ref=rows(*block),
                                                send_sem=send_sems.at[k], recv_sem=recv_sems.at[k], device_id=to, device_id_type=MESH)

        mine = pltpu.make_async_copy(x_ref, rows(*me), local_sem)
        mine.start()
        first = [copy(0, me, sibling, src=x_ref)]
        first += [copy(1 + j, me, (*chip, c), src=x_ref) for j, chip in enumerate(chips)]
        for cp in first:
            cp.start()
        passed = [copy(4 + j, (*chip, c), sibling) for j, chip in enumerate(chips)]
        for j, chip in enumerate(chips):
            copy(1 + j, (*chip, c), me).wait_recv()
            passed[j].start()
        copy(0, sibling, me).wait_recv()
        for j, chip in enumerate(chips):
            copy(4 + j, (*chip, 1 - c), me).wait_recv()
        for cp in first + passed:
            cp.wait_send()
        mine.wait()

    vm = pl.BlockSpec(memory_space=pltpu.VMEM)
    return pl.pallas_call(body, name="allgather_small", in_specs=[vm], out_specs=vm, out_shape=_sds((8 * m_per, n)),
                          scratch_shapes=[pltpu.SemaphoreType.DMA((7,)), pltpu.SemaphoreType.DMA((7,)), pltpu.SemaphoreType.DMA(())],
                          compiler_params=pltpu.CompilerParams(has_side_effects=True, vmem_limit_bytes=VMEM_LIMIT))(v)


def _sum8(v, m_per):
    def body(v_ref, o_ref):
        acc = v_ref[0:m_per, :]
        for k in range(1, 8):
            acc = acc + v_ref[k * m_per:(k + 1) * m_per, :]
        o_ref[...] = acc

    return pl.pallas_call(body, name="small_sum_devices", out_shape=_sds((m_per, v.shape[1])),
                          compiler_params=pltpu.CompilerParams(vmem_limit_bytes=VMEM_LIMIT))(v)


SMALL_SHARDED = (("e_conv_a_w", 2), ("e_conv_b_w", 2), ("o_conv_w", 2), ("f_conv_w", 2), ("ln_g", 2), ("ln_b", 2))
SMALL_REPL = ("e_conv_a_b", "e_ln_a_g", "e_ln_a_b", "e_conv_b_b", "e_dt_bias", "e_a_log", "e_d_skip", "e_norm_b_g", "f_conv_b")

WEIGHT_ORDER = ('e_w_in', 'e_conv_a_w', 'e_conv_a_b', 'e_ln_a_g', 'e_ln_a_b', 'e_conv_b_w', 'e_conv_b_b', 'e_dt_bias', 'e_a_log',
                'e_d_skip', 'e_norm_b_g', 'e_w_out', 'o_w_in', 'o_conv_w', 'o_w_out', 'f_w_up', 'f_conv_w', 'f_conv_b', 'f_w_down',
                'ple_w_proj', 'ple_w_gate', 'ln_g', 'ln_b')


def _pack_rows(parts, width, total_rows, dtype):
    flat = jnp.concatenate([p.reshape(-1).astype(dtype) for p in parts])
    flat = jnp.pad(flat, (0, total_rows * width - flat.shape[0]))
    return flat.reshape(total_rows, width)


def _unpack_rows(buf, shapes):
    flat = buf.reshape(-1)
    out, pos = [], 0
    for s in shapes:
        n = math.prod(s)
        out.append(flat[pos:pos + n].reshape(s))
        pos += n
    return out


def _small_rows(shapes):
    n = sum(math.prod(s) for s in shapes)
    return -(-n // (LANES * SUBLANES)) * SUBLANES


E_PAD = 5248
SEG_A, SEG_Z, SEG_X, SEG_DT = (0, 2 * D), (2 * D, D), (3 * D, 2 * D), (5 * D, LANES)
G_SHAPES = {"e_w_in": (2, D, E_PAD), "e_w_out": (2, 2 * D, D), "o_w_in": (2, D, 3 * D), "o_w_out": (2, D, D),
            "f_w_up": (4, D, 2 * D_FF), "f_w_down": (4, D_FF, D), "ple_w_proj": (4, PLE, D), "ple_w_gate": (4, D, D)}


def _padcols(w, width):
    return jnp.pad(w, ((0, 0), (0, width - w.shape[1])))


def _fold_rows(dw, K):
    return dw.reshape(K, SUBLANES, dw.shape[-1]).sum(1)


class GradBuffers(dict):
    def __init__(self):
        super().__init__()
        self.where = {}
        for gi, layers in enumerate(G_GROUPS):
            for name, kind, Lg, A, B, n, l0 in _group_items(layers):
                for k in range(Lg):
                    self.where[(name, l0 + k)] = (gi, k, Lg)
        self.current = {}

    def into(self, name, layer, r0=0, c0=0):
        gi, k, Lg = self.where[(name, layer)]
        self.current[name] = (name, gi)
        return (self.get((name, gi)), (Lg,) + G_SHAPES[name][1:], (k,), r0, c0)

    def __setitem__(self, name, value):
        super().__setitem__(self.current[name], value)


def _local_step(x, p, target, W, comm=None):
    T = x.shape[0]
    xb = x
    saved = []
    xc_f = x
    for i in range(DEPTH):
        j = i // 2
        L = {}
        L["x"], L["xb"] = xc_f, xb
        tok = comm.part_starts(i, MIX, xb) if comm is not None else None
        if i % 2 == 0:
            def w_in(seg, c0=0, cols=None, j=j):
                return V(W["e_w_in"], (j,), c0=seg[0] + c0, cols=seg[1] if cols is None else cols)

            ua = _mm(xb, w_in(SEG_A), "nn", f"l{i}_in_a", BF16, after=tok)
            z = _mm(xb, w_in(SEG_Z), "nn", f"l{i}_in_z")
            xu = _mm(xb, w_in(SEG_X), "nn", f"l{i}_in_xbc", BF16)
            udt = _mm(xb, w_in(SEG_DT), "nn", f"l{i}_in_dt")
            ac = _conv_a_fwd(ua, W["e_conv_a_w"][j], W["e_conv_a_b"][j][None], f"l{i}_conv_a")
            ya = _ln_silu_fwd(ac, W["e_ln_a_g"][j][None], W["e_ln_a_b"][j][None], f"l{i}_ln_a")
            xc = _conv_b_fwd(xu, W["e_conv_b_w"][j], W["e_conv_b_b"][j][None], f"l{i}_conv_b")
            sm = _ssd_small_inputs(udt[:, :N_HEADS], W["e_dt_bias"][j], W["e_a_log"][j])
            y, states = _ssd_fwd(xc, *sm, W["e_d_skip"][j], f"l{i}_ssd")
            yb = _gate_rms_fwd(y, z, W["e_norm_b_g"][j][None], f"l{i}_gate_rms")
            out_pairs = [(ya, V(W["e_w_out"], (j,), rows=D)), (yb, V(W["e_w_out"], (j,), r0=D))]
            L.update(ua=ua, z=z, xu=xu, udt=udt, ac=ac, ya=ya, xc=xc, sm=sm, y=y, states=states, yb=yb, w_in=w_in)
        else:
            uo = _mm(xb, V(W["o_w_in"], (j,)), "nn", f"l{i}_in", BF16, after=tok)
            sc = _conv_c_fwd(uo, W["o_conv_w"][j], f"l{i}_conv_c")
            out_pairs = [(sc, V(W["o_w_out"], (j,)))]
            L.update(uo=uo, sc=sc)
        h1, x1, x1b = _mm_sum(out_pairs, "nn", f"l{i}_out", ln_fwd=(xc_f, None, W["ln_g"][i, 0][None], W["ln_b"][i, 0][None]))
        tok = comm.part_starts(i, FFN, x1b) if comm is not None else None
        up = _mm(x1b, V(W["f_w_up"], (i,)), "nn", f"l{i}_ffn_up", BF16, after=tok)
        act = _conv_f_fwd(up, W["f_conv_w"][i], W["f_conv_b"][i][None], f"l{i}_conv_f")
        pv = V(p, (i, 0))
        pp = _mm(pv, V(W["ple_w_proj"], (i,)), "nn", f"l{i}_ple_proj")
        gl = _mm(x1b, V(W["ple_w_gate"], (i,)), "nn", f"l{i}_ple_gate")
        h2, x2, x2b = _mm_sum([(act, V(W["f_w_down"], (i,)))], "nn", f"l{i}_ffn_down",
                              ln_fwd=(x1, (pp, gl), W["ln_g"][i, 1][None], W["ln_b"][i, 1][None]))
        L.update(h1=h1, x1=x1, x1b=x1b, up=up, act=act, pv=pv, pp=pp, gl=gl, h2=h2)
        saved.append(L)
        xc_f, xb = x2, x2b

    sq, dx = _loss_head(xc_f, target, "loss_head")

    GB = GradBuffers()
    into = GB.into
    tok = None
    ln2_done = None

    G = {n: [None] * (DEPTH if n.startswith(("f_", "ln_")) else DEPTH // 2) for n in WEIGHT_ORDER if n not in G_SHAPES}
    for i in reversed(range(DEPTH)):
        j = i // 2
        L = saved[i]
        if ln2_done is None:
            ln2_done = _res_ln_bwd(dx, L["h2"], W["ln_g"][i, 1][None], (L["pp"], L["gl"]), f"l{i}_ln2_bwd")
        dh2, dh2b, dg2, db2, dpp, dgl = ln2_done
        ln2_done = None
        GB["f_w_down"] = _mm(L["act"], dh2b, "tn", f"l{i}_dw_down", dst=into("f_w_down", i))
        dact = _mm(dh2b, V(W["f_w_down"], (i,)), "nt", f"l{i}_dact", BF16, after=tok)
        du1, du2, dw1, dw2, dbf1, dbf2 = _conv_f_bwd(L["up"], W["f_conv_w"][i], W["f_conv_b"][i][None], dact, f"l{i}_conv_f_bwd")
        G["f_conv_w"][i] = jnp.concatenate([_fold_rows(dw1, CONV_F), _fold_rows(dw2, CONV_F)], axis=1)
        G["f_conv_b"][i] = jnp.concatenate([dbf1.sum(0), dbf2.sum(0)])
        GB["f_w_up"] = _mm(L["x1b"], du1, "tn", f"l{i}_dw_up1", dst=into("f_w_up", i))
        GB["f_w_up"] = _mm(L["x1b"], du2, "tn", f"l{i}_dw_up2", dst=into("f_w_up", i, c0=D_FF))
        GB["ple_w_proj"] = _mm(L["pv"], dpp, "tn", f"l{i}_dw_proj", dst=into("ple_w_proj", i))
        GB["ple_w_gate"] = _mm(L["x1b"], dgl, "tn", f"l{i}_dw_gate", dst=into("ple_w_gate", i))
        tok = comm.part_grads_done(i, FFN, GB) if comm is not None else None
        dh1, dh1b, dg1, db1 = _mm_sum(
            [(du1, V(W["f_w_up"], (i,), cols=D_FF)), (du2, V(W["f_w_up"], (i,), c0=D_FF)), (dgl, V(W["ple_w_gate"], (i,)))],
            "nt", f"l{i}_dx1", add=dh2, add_scale=ALPHA, after=tok, ln_bwd=(L["h1"], W["ln_g"][i, 0][None], None))
        G["ln_g"][i] = jnp.concatenate([dg1, dg2], axis=0)
        G["ln_b"][i] = jnp.concatenate([db1, db2], axis=0)
        if i % 2 == 0:
            GB["e_w_out"] = _mm(L["ya"], dh1b, "tn", f"l{i}_dw_out_a", dst=into("e_w_out", j))
            GB["e_w_out"] = _mm(L["yb"], dh1b, "tn", f"l{i}_dw_out_b", dst=into("e_w_out", j, r0=D))
            dya = _mm(dh1b, V(W["e_w_out"], (j,), rows=D), "nt", f"l{i}_dya")
            dyb = _mm(dh1b, V(W["e_w_out"], (j,), r0=D), "nt", f"l{i}_dyb")
            dac, dga, dba = _ln_silu_bwd(L["ac"], dya, W["e_ln_a_g"][j][None], W["e_ln_a_b"][j][None], f"l{i}_ln_a_bwd")
            G["e_ln_a_g"][j], G["e_ln_a_b"][j] = dga[0], dba[0]
            dal, dag, dwa, dbca = _conv_a_bwd(L["ua"], W["e_conv_a_w"][j], dac, f"l{i}_conv_a_bwd")
            G["e_conv_a_w"][j] = _fold_rows(dwa, CONV_A)
            G["e_conv_a_b"][j] = dbca.sum(0)
            dy, dz, dgn = _gate_rms_bwd(L["y"], L["z"], dyb, W["e_norm_b_g"][j][None], f"l{i}_gate_rms_bwd")
            G["e_norm_b_g"][j] = dgn[0]
            dxs, dbs, dcs, sq_col, cms_row, dda_col, dda_row, ddp = _ssd_bwd(L["xc"], *L["sm"], W["e_d_skip"][j], L["states"], dy,
                                                                             f"l{i}_ssd_bwd")
            draw, G["e_dt_bias"][j], G["e_a_log"][j] = _ssd_small_grads(L["udt"][:, :N_HEADS], W["e_dt_bias"][j], W["e_a_log"][j],
                                                                       sq_col, cms_row, dda_col, dda_row)
            G["e_d_skip"][j] = ddp[:, :, 0, :].sum(0).reshape(N_HEADS, HEAD_P).sum(1)
            dxu, dwb, dbcb = _conv_b_bwd(L["xu"], W["e_conv_b_w"][j], W["e_conv_b_b"][j][None], dxs, dbs, dcs, f"l{i}_conv_b_bwd")
            G["e_conv_b_w"][j] = _fold_rows(dwb, CONV_B)
            G["e_conv_b_b"][j] = dbcb.sum(0)
            dudt = _padcols(draw, LANES)
            w_in = L["w_in"]
            xb_l = L["xb"]
            for nm, dseg, c0 in (("al", dal, 0), ("ag", dag, D), ("z", dz, SEG_Z[0]), ("xbc", dxu, SEG_X[0]), ("dt", dudt, SEG_DT[0])):
                GB["e_w_in"] = _mm(xb_l, dseg, "tn", f"l{i}_dw_in_{nm}", dst=into("e_w_in", j, c0=c0))
            dx = _mm_sum([(dal, w_in(SEG_A, cols=D)), (dag, w_in(SEG_A, c0=D, cols=D)), (dz, w_in(SEG_Z)),
                          (V(dxu, cols=D), w_in(SEG_X, cols=D)), (V(dxu, c0=D), w_in(SEG_X, c0=D, cols=D)), (dudt, w_in(SEG_DT))],
                         "nt", f"l{i}_dx", add=dh1, add_scale=ALPHA)
        else:
            GB["o_w_out"] = _mm(L["sc"], dh1b, "tn", f"l{i}_dw_out", dst=into("o_w_out", j))
            dsc = _mm(dh1b, V(W["o_w_out"], (j,)), "nt", f"l{i}_dsc")
            dbg, dcg, dv, dwc = _conv_c_bwd(L["uo"], W["o_conv_w"][j], dsc, f"l{i}_conv_c_bwd")
            G["o_conv_w"][j] = _fold_rows(dwc, CONV_C)
            xb_l = L["xb"]
            for nm, dseg, c0 in (("bg", dbg, 0), ("cg", dcg, D), ("v", dv, 2 * D)):
                GB["o_w_in"] = _mm(xb_l, dseg, "tn", f"l{i}_dw_in_{nm}", dst=into("o_w_in", j, c0=c0))
            below = saved[i - 1]
            ln2_done = _mm_sum([(dseg, V(W["o_w_in"], (j,), c0=c0, cols=D)) for dseg, c0 in ((dbg, 0), (dcg, D), (dv, 2 * D))],
                               "nt", f"l{i}_dx", add=dh1, add_scale=ALPHA,
                               ln_bwd=(below["h2"], W["ln_g"][i - 1, 1][None], (below["pp"], below["gl"])))
        tok = comm.part_grads_done(i, MIX, GB) if comm is not None else None
    grads = {n: jnp.stack(v) for n, v in G.items()}
    return sq, dx, GB, grads


def _ssd_small_inputs(raw, dt_bias, a_log):
    T = raw.shape[0]
    a = -jnp.exp(a_log)
    rg = raw.reshape(T, N_GROUPS, 4)
    raw_col = jnp.pad(jnp.transpose(rg, (1, 0, 2)), ((0, 0), (0, 0), (0, LANES - 4)))
    raw_row = jnp.pad(jnp.transpose(rg, (1, 2, 0)), ((0, 0), (0, SUBLANES - 4), (0, 0)))

    def colv(v):
        return jnp.pad(v.reshape(N_GROUPS, 1, 4), ((0, 0), (0, 0), (0, LANES - 4)))

    def rowv(v):
        return jnp.pad(v.reshape(N_GROUPS, 4, 1), ((0, 0), (0, SUBLANES - 4), (0, 0)))

    return raw_col, raw_row, colv(dt_bias), rowv(dt_bias), colv(a), rowv(a)


def _ssd_small_grads(raw, dt_bias, a_log, sq_col, cms_row, dda_col, dda_row):
    T = raw.shape[0]

    def join(col, row):
        c = jnp.transpose(col[:, :, :4], (1, 0, 2)).reshape(T, N_HEADS)
        r = jnp.transpose(row[:, :4, :], (2, 0, 1)).reshape(T, N_HEADS)
        return c + r

    a = -jnp.exp(a_log)
    pre = raw + dt_bias
    dt = jax.nn.softplus(pre)
    dda = join(dda_col, dda_row)
    ddt = join(sq_col, cms_row) + a * dda
    draw = ddt * jax.nn.sigmoid(pre)
    da = jnp.sum(dt * dda, axis=0)
    return draw, jnp.sum(draw, axis=0), da * a


def kernel(x, p, e_w_in, e_conv_a_w, e_conv_a_b, e_ln_a_g, e_ln_a_b, e_conv_b_w, e_conv_b_b, e_dt_bias, e_a_log, e_d_skip, e_norm_b_g, e_w_out, o_w_in, o_conv_w, o_w_out, f_w_up, f_conv_w, f_conv_b, f_w_down, ple_w_proj, ple_w_gate, ln_g, ln_b, loss_target, m_e_w_in, m_e_conv_a_w, m_e_conv_a_b, m_e_ln_a_g, m_e_ln_a_b, m_e_conv_b_w, m_e_conv_b_b, m_e_dt_bias, m_e_a_log, m_e_d_skip, m_e_norm_b_g, m_e_w_out, m_o_w_in, m_o_conv_w, m_o_w_out, m_f_w_up, m_f_conv_w, m_f_conv_b, m_f_w_down, m_ple_w_proj, m_ple_w_gate, m_ln_g, m_ln_b, v_e_w_in, v_e_conv_a_w, v_e_conv_a_b, v_e_ln_a_g, v_e_ln_a_b, v_e_conv_b_w, v_e_conv_b_b, v_e_dt_bias, v_e_a_log, v_e_d_skip, v_e_norm_b_g, v_e_w_out, v_o_w_in, v_o_conv_w, v_o_w_out, v_f_w_up, v_f_conv_w, v_f_conv_b, v_f_w_down, v_ple_w_proj, v_ple_w_gate, v_ln_g, v_ln_b):
    args = dict(locals())
    w_shard = {n: args[n] for n in WEIGHT_ORDER}
    m_shard = {n: args["m_" + n] for n in WEIGHT_ORDER}
    v_shard = {n: args["v_" + n] for n in WEIGHT_ORDER}
    xi, yi, ci = lax.axis_index("x"), lax.axis_index("y"), lax.axis_index("c")
    chip = 2 * xi + yi

    mevec = jnp.stack([chip, ci]).astype(jnp.int32)
    small_shapes = [w_shard[n].shape for n, _ in SMALL_SHARDED]
    sr = _small_rows(small_shapes)
    ws = _pack_rows([w_shard[n] for n, _ in SMALL_SHARDED], LANES, sr, F32)
    W = {n: w_shard[n] for n in SMALL_REPL}
    W.update({s[0]: Layers(s[2]) for s in TENSORS})
    w_items = [_group_items(layers) for layers in W_GROUPS]
    g_items = [_group_items(layers) for layers in G_GROUPS]

    def install(items, fulls):
        for it, f in zip(items, fulls):
            if it[0] == "e_w_in":
                f = jnp.transpose(f.reshape(it[2], 4, D, E_IN // 4), (0, 2, 1, 3)).reshape(it[2], D, E_IN)
                f = jnp.pad(f, ((0, 0), (0, 0), (0, E_PAD - E_IN)))
            W[it[0]].put(f, it[6])

    casts = [[_cast_into(w_shard[it[0]], it, mevec[:1]) for it in items] for items in w_items]
    fulls, ssem, rsem, _ = _gather_start(casts[0], w_items[0], "gather_start_0")
    fulls = _gather_wait(fulls, ssem, rsem, [c for grp in casts[1:] for c in grp], w_items[0], "gather_wait_0")
    fulls, WS = _gather_fwd(fulls, w_items[0], "gather_fwd_0", ws)
    install(w_items[0], fulls)
    parts_s = [_unpack_rows(WS[k], small_shapes) for k in range(4)]
    for idx, (n, ax) in enumerate(SMALL_SHARDED):
        W[n] = jnp.concatenate([parts_s[k][idx] for k in range(4)], axis=ax)

    class Comm:
        sent = {}
        started = {}
        tail = fulls[0]

        def start_next(self, gi):
            if gi >= len(w_items):
                return None
            self.started[gi] = _gather_start(casts[gi], w_items[gi], f"gather_start_{gi}", self.tail)
            return self.started[gi][3]

        def part_starts(self, layer, part, after):
            if (layer, part) == W_GROUPS[0][0]:
                return self.start_next(1)
            for gi in range(1, len(W_GROUPS)):
                if W_GROUPS[gi][0] == (layer, part):
                    fulls, ssem, rsem, _ = self.started[gi]
                    fulls = _gather_wait(fulls, ssem, rsem, after, w_items[gi], f"gather_wait_{gi}")
                    fulls, _ = _gather_fwd(fulls, w_items[gi], f"gather_fwd_{gi}")
                    install(w_items[gi], fulls)
                    self.tail = fulls[0]
                    return self.start_next(gi + 1)
            return None

        def part_grads_done(self, layer, part, GB):
            tok = None
            for gi, members in enumerate(G_GROUPS):
                if members[-1] == (layer, part):
                    items = g_items[gi]
                    gs = []
                    for it in items:
                        g = GB[(it[0], gi)]
                        if it[0] == "e_w_in":
                            g = jnp.transpose(g[:, :, :E_IN].reshape(it[2], D, 4, E_IN // 4), (0, 2, 1, 3)).reshape(it[2], 4 * D, E_IN // 4)
                        gs.append(g)
                    ras = _swap_grads(gs, items, f"swap_grads_{gi}")
                    ps = [_add_half(g, ra, it, mevec[1:]) for g, ra, it in zip(gs, ras, items)]
                    thru, ssem, rsem, tok = _scatter_start(ps, items, f"scatter_start_{gi}")
                    self.sent[gi] = (thru, ssem, rsem, tok)
            return tok

    comm = Comm()

    sq, dx, GB, G = _local_step(x[0], p, loss_target[0], W, comm)
    loss = lax.psum(0.5 * sq[0, 0] / D, ("x", "y", "c"))
    grad_x = dx[None]

    def shard_of(g, ax, k):
        n = g.shape[ax] // 4
        return lax.slice_in_dim(g, k * n, (k + 1) * n, axis=ax)

    reduced, updated = {}, {}
    after = comm.sent[len(g_items) - 1][3]
    for gi, items in enumerate(g_items):
        thru, ssem, rsem, _ = comm.sent[gi]
        thru = _scatter_wait(thru, ssem, rsem, after, items, f"scatter_wait_{gi}")
        ps, rcs = thru[:len(items)], thru[len(items):]
        rs = [_sum_own(pt, rc, it, mevec, reduced.get(it[0])) for pt, rc, it in zip(ps, rcs, items)]
        rs = _join_halves(rs, items, f"join_halves_{gi}")
        reduced.update({it[0]: r for it, r in zip(items, rs)})
        for it in items:
            n = it[0]
            updated[n] = _adamw_layers(w_shard[n], reduced[n], m_shard[n], v_shard[n], it[6], it[2], updated.get(n), f"adamw_{n}_{it[6]}")
        after = updated[items[0][0]][0]

    small_all = ([shard_of(G[n], ax, k) for k in range(4) for n, ax in SMALL_SHARDED] + [G[n] for n in SMALL_REPL])
    small_all_shapes = [t.shape for t in small_all]
    mr = _small_rows(small_all_shapes)
    sg = _sum8(_allgather_small(_pack_rows(small_all, LANES, mr, F32)), mr)
    sparts = _unpack_rows(sg, small_all_shapes)
    ns = len(SMALL_SHARDED)
    gsmall = {}
    for idx, (n, ax) in enumerate(SMALL_SHARDED):
        stacked = jnp.stack([sparts[k * ns + idx] for k in range(4)])
        gsmall[n] = lax.dynamic_index_in_dim(stacked, chip, axis=0, keepdims=False)
    for idx, n in enumerate(SMALL_REPL):
        gsmall[n] = sparts[4 * ns + idx]

    grads, deltas, new_m, new_v = [], [], [], []
    for n in WEIGHT_ORDER:
        if n in reduced:
            g, (d, mn, vn) = reduced[n], updated[n]
        else:
            g = gsmall[n]
            d, mn, vn = _adamw(w_shard[n], g, m_shard[n], v_shard[n], f"adamw_{n}")
        grads.append(g)
        deltas.append(d)
        new_m.append(mn)
        new_v.append(vn)
    return (loss, grad_x, *grads, *deltas, *new_m, *new_v)
```

```python
import functools
import math

import jax
import jax.numpy as jnp
from jax import lax
from jax.experimental import pallas as pl
from jax.experimental.pallas import tpu as pltpu

F32 = jnp.float32
BF16 = jnp.bfloat16
MESH = pl.DeviceIdType.MESH

DEPTH = 4
ALPHA = (2.0 * DEPTH) ** 0.25
LN_EPS = 1e-5
D = 1024
HEAD_P = 64
N_STATE = 128
N_HEADS = 16
N_GROUPS = 4
CONV_A, CONV_B, CONV_C, CONV_F = 31, 4, 3, 3
D_FF = 2816
PLE = 256
E_IN = 5136

ADAM_LR, ADAM_B1, ADAM_B2, ADAM_EPS, ADAM_WD, ADAM_STEP = 0.001, 0.9, 0.999, 1e-08, 0.01, 10

LANES = 128
SUBLANES = 8
VMEM_LIMIT = 56 * 1024 * 1024
SSD_Q = 128
CONV_R = 128
CONV_PAD = 32
ROW_T = 256


def _cparams(sem=None):
    return pltpu.CompilerParams(dimension_semantics=sem, vmem_limit_bytes=VMEM_LIMIT)


def _sig(v):
    return jax.nn.sigmoid(v)


_DIMS = {"nn": (((1,), (0,)), ((), ())), "nt": (((1,), (1,)), ((), ())), "tn": (((0,), (0,)), ((), ()))}


class Layers:
    def __init__(self, n_layers):
        self.where = [None] * n_layers

    def put(self, arr, l0):
        for k in range(arr.shape[0]):
            self.where[l0 + k] = (arr, k)


class V:
    def __init__(self, arr, lead=(), r0=0, c0=0, rows=None, cols=None):
        if isinstance(arr, Layers):
            arr, k = arr.where[lead[0]]
            lead = (k,) + tuple(lead[1:])
        self.arr, self.lead, self.r0, self.c0 = arr, tuple(lead), r0, c0
        R, C = arr.shape[-2:]
        self.rows = R - r0 if rows is None else rows
        self.cols = C - c0 if cols is None else cols

    def spec(self, br, bc, fn):
        assert self.r0 % br == 0 and self.c0 % bc == 0, (self.r0, self.c0, br, bc)
        ro, co, lead = self.r0 // br, self.c0 // bc, self.lead

        def index(i, j, k):
            r, c = fn(i, j, k)
            return lead + (r + ro, c + co)

        return pl.BlockSpec((None,) * len(lead) + (br, bc), index)


def _v(t):
    return t if isinstance(t, V) else V(t)


def _tile(n, offs, cands):
    for c in cands:
        if n % c == 0 and all(o % c == 0 for o in offs):
            return c
    raise ValueError((n, offs))


_TILES = (1024, 1408, 512, 256, 128)


def _mm(a, b, mode, name, out_dtype=F32, add=None, add_scale=1.0, dst=None, after=None):
    a, b = _v(a), _v(b)
    add = _v(add) if add is not None else None
    if mode == "nn":
        M, K, K2, N = a.rows, a.cols, b.rows, b.cols
        am, ak, bk, bn = a.r0, a.c0, b.r0, b.c0
    elif mode == "nt":
        M, K, N, K2 = a.rows, a.cols, b.rows, b.cols
        am, ak, bn, bk = a.r0, a.c0, b.r0, b.c0
    else:
        K, M, K2, N = a.rows, a.cols, b.rows, b.cols
        ak, am, bk, bn = a.r0, a.c0, b.r0, b.c0
    assert K == K2, (name, mode, M, K, K2, N)
    if dst is None:
        buf, full_shape, o_lead, o_r0, o_c0 = None, (M, N), (), 0, 0
    else:
        buf, full_shape, o_lead, o_r0, o_c0 = dst
    tm = _tile(M, [am, o_r0] + ([add.r0] if add else []), _TILES)
    tn = _tile(N, [bn, o_c0] + ([add.c0] if add else []), _TILES)
    narrow = a.arr.dtype.itemsize == 2 and b.arr.dtype.itemsize == 2
    tk = _tile(K, [ak, bk], ((2048,) if narrow else ()) + _TILES)
    nk = K // tk
    has_add, has_buf, has_after = add is not None, buf is not None, after is not None

    def body(*refs):
        a_ref, b_ref = refs[0], refs[1]
        add_ref = refs[2] if has_add else None
        o_ref = refs[2 + has_add + has_buf + has_after]

        def finish(r):
            if has_add:
                r = r + add_scale * add_ref[...].astype(F32)
            o_ref[...] = r.astype(o_ref.dtype)

        part = lax.dot_general(a_ref[...].astype(BF16), b_ref[...].astype(BF16), _DIMS[mode], preferred_element_type=F32)
        if nk == 1:
            finish(part)
        else:
            acc_ref = refs[-1]
            k = pl.program_id(2)

            @pl.when(k == 0)
            def _():
                acc_ref[...] = part

            @pl.when(jnp.logical_and(k > 0, k < nk - 1))
            def _():
                acc_ref[...] += part

            @pl.when(k == nk - 1)
            def _():
                finish(acc_ref[...] + part)

    if mode == "tn":
        a_spec = a.spec(tk, tm, lambda i, j, k: (k, i))
    else:
        a_spec = a.spec(tm, tk, lambda i, j, k: (i, k))
    if mode == "nt":
        b_spec = b.spec(tn, tk, lambda i, j, k: (j, k))
    else:
        b_spec = b.spec(tk, tn, lambda i, j, k: (k, j))
    in_specs, args = [a_spec, b_spec], [a.arr, b.arr]
    if has_add:
        in_specs.append(add.spec(tm, tn, lambda i, j, k: (i, j)))
        args.append(add.arr)
    aliases = {}
    if has_buf:
        aliases = {len(args): 0}
        in_specs.append(pl.BlockSpec(memory_space=pl.ANY))
        args.append(buf)
        out_dtype = buf.dtype
    if has_after:
        in_specs.append(pl.BlockSpec(memory_space=pl.ANY))
        args.append(after)
    o_view = V(jax.ShapeDtypeStruct(full_shape, out_dtype), o_lead, o_r0, o_c0, M, N)
    return pl.pallas_call(
        body, name=name, grid=(M // tm, N // tn, nk), in_specs=in_specs, out_specs=o_view.spec(tm, tn, lambda i, j, k: (i, j)),
        out_shape=jax.ShapeDtypeStruct(full_shape, out_dtype), input_output_aliases=aliases,
        scratch_shapes=[pltpu.VMEM((tm, tn), F32)] if nk > 1 else [],
        compiler_params=_cparams(("parallel", "parallel", "arbitrary")))(*args)


def _ln_stats(h):
    mu = jnp.mean(h, axis=-1, keepdims=True)
    hc = h - mu
    var = jnp.mean(hc * hc, axis=-1, keepdims=True)
    rstd = lax.rsqrt(var + LN_EPS)
    return hc * rstd, rstd


def _ln_bwd_math(dyv, h, g):
    xhat, rstd = _ln_stats(h)
    dxh = dyv * g
    dh = rstd * (dxh - jnp.mean(dxh, axis=-1, keepdims=True) - xhat * jnp.mean(dxh * xhat, axis=-1, keepdims=True))
    return dh, jnp.sum(dyv * xhat, axis=0, keepdims=True), jnp.sum(dyv, axis=0, keepdims=True)


def _mm_sum(pairs, mode, name, out_dtype=F32, add=None, add_scale=1.0, after=None, ln_fwd=None, ln_bwd=None):
    pairs = [(_v(a), _v(b)) for a, b in pairs]
    add = _v(add) if add is not None else None
    M = pairs[0][0].rows
    N = pairs[0][1].cols if mode == "nn" else pairs[0][1].rows
    b_offs = [(b.c0 if mode == "nn" else b.r0) for _, b in pairs]
    fused = ln_fwd is not None or ln_bwd is not None
    tm = _tile(M, [a.r0 for a, _ in pairs] + ([add.r0] if add else []), (256, 128) if fused else (512, 256, 128))
    tn = N if fused else _tile(N, b_offs + ([add.c0] if add else []), (512, 256, 128))
    assert not fused or (N == D and all(o == 0 for o in b_offs))
    n_p, has_add, has_after = len(pairs), add is not None, after is not None
    ple = (ln_fwd[1] if ln_fwd is not None else ln_bwd[2]) if fused else None
    has_ple = ple is not None

    def body(*refs):
        acc = None
        for i in range(n_p):
            part = lax.dot_general(refs[2 * i][...].astype(BF16), refs[2 * i + 1][...].astype(BF16), _DIMS[mode],
                                   preferred_element_type=F32)
            acc = part if acc is None else acc + part
        pos = 2 * n_p
        if has_add:
            acc = acc + add_scale * refs[pos][...].astype(F32)
            pos += 1
        if ln_fwd is not None:
            x_ref = refs[pos]
            pp_ref, gl_ref = (refs[pos + 1], refs[pos + 2]) if has_ple else (None, None)
            pos += 1 + 2 * has_ple
            g_ref, b_ref = refs[pos], refs[pos + 1]
            h_ref, y_ref, yb_ref = refs[pos + 2 + has_after:]
            h = ALPHA * x_ref[...] + acc
            if has_ple:
                h = h + pp_ref[...] * _sig(gl_ref[...])
            xhat, _ = _ln_stats(h)
            y = xhat * g_ref[...] + b_ref[...]
            h_ref[...] = h
            y_ref[...] = y
            yb_ref[...] = y.astype(BF16)
        elif ln_bwd is not None:
            h_ref, g_ref = refs[pos], refs[pos + 1]
            pp_ref, gl_ref = (refs[pos + 2], refs[pos + 3]) if has_ple else (None, None)
            outs = refs[pos + 2 + 2 * has_ple + has_after:]
            dh_ref, dhb_ref, dg_ref, db_ref = outs[:4]

            @pl.when(pl.program_id(0) == 0)
            def _():
                dg_ref[...] = jnp.zeros_like(dg_ref)
                db_ref[...] = jnp.zeros_like(db_ref)

            dh, dg, db = _ln_bwd_math(acc, h_ref[...], g_ref[...])
            dg_ref[...] += dg
            db_ref[...] += db
            dh_ref[...] = dh
            dhb_ref[...] = dh.astype(BF16)
            if has_ple:
                s = _sig(gl_ref[...])
                outs[4][...] = (dh * s).astype(BF16)
                outs[5][...] = (dh * pp_ref[...] * s * (1.0 - s)).astype(BF16)
        else:
            o_ref = refs[pos + has_after]
            o_ref[...] = acc.astype(o_ref.dtype)

    in_specs, args = [], []
    for a, b in pairs:
        K = a.cols
        assert K == (b.rows if mode == "nn" else b.cols), (name, K)
        in_specs.append(a.spec(tm, K, lambda i, j, k: (i, 0)))
        in_specs.append(b.spec(K, tn, lambda i, j, k: (0, j)) if mode == "nn" else b.spec(tn, K, lambda i, j, k: (j, 0)))
        args += [a.arr, b.arr]
    if has_add:
        in_specs.append(add.spec(tm, tn, lambda i, j, k: (i, j)))
        args.append(add.arr)
    row = pl.BlockSpec((tm, tn), lambda i, j, k: (i, j))
    vec = pl.BlockSpec((1, tn), lambda i, j, k: (0, 0))
    if ln_fwd is not None:
        x, _, g, b = ln_fwd
        extra = [x] + (list(ple) if has_ple else []) + [g, b]
        in_specs += [row] * (1 + 2 * has_ple) + [vec, vec]
        args += extra
        out_specs = (row, row, row)
        out_shape = (_sds((M, N)), _sds((M, N)), _sds((M, N), BF16))
    elif ln_bwd is not None:
        h, g, _ = ln_bwd
        in_specs += [row, vec] + [row] * (2 * has_ple)
        args += [h, g] + (list(ple) if has_ple else [])
        out_specs = (row, row, vec, vec) + ((row, row) if has_ple else ())
        out_shape = (_sds((M, N)), _sds((M, N), BF16), _sds((1, N)), _sds((1, N))) + ((_sds((M, N), BF16),) * 2 if has_ple else ())
    else:
        out_specs, out_shape = row, jax.ShapeDtypeStruct((M, N), out_dtype)
    if has_after:
        in_specs.append(pl.BlockSpec(memory_space=pl.ANY))
        args.append(after)
    return pl.pallas_call(
        body, name=name, grid=(M // tm, N // tn, 1), in_specs=in_specs, out_specs=out_specs, out_shape=out_shape,
        compiler_params=_cparams(("arbitrary",) * 3 if ln_bwd is not None else ("parallel", "parallel", "arbitrary")))(*args)


def _rows(T, width=D):
    return pl.BlockSpec((ROW_T, width), lambda i: (i, 0))


def _vec(width=D):
    return pl.BlockSpec((1, width), lambda i: (0, 0))


def _res_ln_fwd(x, adds, ple, g, b, name):
    T = x.shape[0]
    n_add = len(adds)
    has_ple = ple is not None

    def body(*refs):
        x_ref = refs[0]
        add_refs = refs[1:1 + n_add]
        pos = 1 + n_add
        if has_ple:
            pp_ref, gl_ref = refs[pos], refs[pos + 1]
            pos += 2
        g_ref, b_ref, h_ref, y_ref, yb_ref = refs[pos:pos + 5]
        h = ALPHA * x_ref[...]
        for r in add_refs:
            h = h + r[...]
        if has_ple:
            h = h + pp_ref[...] * _sig(gl_ref[...])
        xhat, _ = _ln_stats(h)
        y = xhat * g_ref[...] + b_ref[...]
        h_ref[...] = h
        y_ref[...] = y
        yb_ref[...] = y.astype(BF16)

    n_in = 1 + n_add + (2 if has_ple else 0)
    args = (x,) + tuple(adds) + (tuple(ple) if has_ple else ()) + (g, b)
    return pl.pallas_call(
        body, name=name, grid=(T // ROW_T,), in_specs=[_rows(T)] * n_in + [_vec(), _vec()],
        out_specs=(_rows(T), _rows(T), _rows(T)),
        out_shape=(jax.ShapeDtypeStruct((T, D), F32), jax.ShapeDtypeStruct((T, D), F32), jax.ShapeDtypeStruct((T, D), BF16)),
        compiler_params=_cparams(("parallel",)))(*args)


def _res_ln_bwd(dy, h, g, ple, name):
    T = dy.shape[0]
    has_ple = ple is not None

    def body(*refs):
        if has_ple:
            dy_ref, h_ref, g_ref, pp_ref, gl_ref, dh_ref, dhb_ref, dg_ref, db_ref, dpp_ref, dgl_ref = refs
        else:
            dy_ref, h_ref, g_ref, dh_ref, dhb_ref, dg_ref, db_ref = refs
        i = pl.program_id(0)

        @pl.when(i == 0)
        def _():
            dg_ref[...] = jnp.zeros_like(dg_ref)
            db_ref[...] = jnp.zeros_like(db_ref)

        dyv = dy_ref[...]
        xhat, rstd = _ln_stats(h_ref[...])
        dg_ref[...] += jnp.sum(dyv * xhat, axis=0, keepdims=True)
        db_ref[...] += jnp.sum(dyv, axis=0, keepdims=True)
        dxh = dyv * g_ref[...]
        dh = rstd * (dxh - jnp.mean(dxh, axis=-1, keepdims=True) - xhat * jnp.mean(dxh * xhat, axis=-1, keepdims=True))
        dh_ref[...] = dh
        dhb_ref[...] = dh.astype(BF16)
        if has_ple:
            s = _sig(gl_ref[...])
            dpp_ref[...] = (dh * s).astype(BF16)
            dgl_ref[...] = (dh * pp_ref[...] * s * (1.0 - s)).astype(BF16)

    args = (dy, h, g) + (tuple(ple) if has_ple else ())
    in_specs = [_rows(T), _rows(T), _vec()] + ([_rows(T), _rows(T)] if has_ple else [])
    out_specs = [_rows(T), _rows(T), _vec(), _vec()] + ([_rows(T), _rows(T)] if has_ple else [])
    out_shape = [jax.ShapeDtypeStruct((T, D), F32), jax.ShapeDtypeStruct((T, D), BF16),
                 jax.ShapeDtypeStruct((1, D), F32), jax.ShapeDtypeStruct((1, D), F32)]
    if has_ple:
        out_shape += [jax.ShapeDtypeStruct((T, D), BF16), jax.ShapeDtypeStruct((T, D), BF16)]
    return pl.pallas_call(
        body, name=name, grid=(T // ROW_T,), in_specs=in_specs, out_specs=tuple(out_specs), out_shape=tuple(out_shape),
        compiler_params=_cparams(("arbitrary",)))(*args)


def _ln_silu_fwd(ac, g, b, name):
    T = ac.shape[0]

    def body(a_ref, g_ref, b_ref, o_ref):
        xhat, _ = _ln_stats(a_ref[...])
        ln = xhat * g_ref[...] + b_ref[...]
        o_ref[...] = (ln * _sig(ln)).astype(BF16)

    return pl.pallas_call(
        body, name=name, grid=(T // ROW_T,), in_specs=[_rows(T), _vec(), _vec()], out_specs=_rows(T),
        out_shape=jax.ShapeDtypeStruct((T, D), BF16), compiler_params=_cparams(("parallel",)))(ac, g, b)


def _ln_silu_bwd(ac, dya, g, b, name):
    T = ac.shape[0]

    def body(a_ref, d_ref, g_ref, b_ref, da_ref, dg_ref, db_ref):
        i = pl.program_id(0)

        @pl.when(i == 0)
        def _():
            dg_ref[...] = jnp.zeros_like(dg_ref)
            db_ref[...] = jnp.zeros_like(db_ref)

        xhat, rstd = _ln_stats(a_ref[...])
        ln = xhat * g_ref[...] + b_ref[...]
        s = _sig(ln)
        dln = d_ref[...] * s * (1.0 + ln * (1.0 - s))
        dg_ref[...] += jnp.sum(dln * xhat, axis=0, keepdims=True)
        db_ref[...] += jnp.sum(dln, axis=0, keepdims=True)
        dxh = dln * g_ref[...]
        da_ref[...] = rstd * (dxh - jnp.mean(dxh, axis=-1, keepdims=True)
                              - xhat * jnp.mean(dxh * xhat, axis=-1, keepdims=True))

    return pl.pallas_call(
        body, name=name, grid=(T // ROW_T,), in_specs=[_rows(T), _rows(T), _vec(), _vec()],
        out_specs=(_rows(T), _vec(), _vec()),
        out_shape=(jax.ShapeDtypeStruct((T, D), F32), jax.ShapeDtypeStruct((1, D), F32), jax.ShapeDtypeStruct((1, D), F32)),
        compiler_params=_cparams(("arbitrary",)))(ac, dya, g, b)


def _gate_rms_fwd(y, z, g, name):
    T = y.shape[0]

    def body(y_ref, z_ref, g_ref, o_ref):
        zv = z_ref[...]
        yg = y_ref[...] * (zv * _sig(zv))
        r = lax.rsqrt(jnp.mean(yg * yg, axis=-1, keepdims=True) + LN_EPS)
        o_ref[...] = (yg * r * g_ref[...]).astype(BF16)

    return pl.pallas_call(
        body, name=name, grid=(T // ROW_T,), in_specs=[_rows(T), _rows(T), _vec()], out_specs=_rows(T),
        out_shape=jax.ShapeDtypeStruct((T, D), BF16), compiler_params=_cparams(("parallel",)))(y, z, g)


def _gate_rms_bwd(y, z, dout, g, name):
    T = y.shape[0]

    def body(y_ref, z_ref, d_ref, g_ref, dy_ref, dz_ref, dg_ref):
        i = pl.program_id(0)

        @pl.when(i == 0)
        def _():
            dg_ref[...] = jnp.zeros_like(dg_ref)

        yv, zv, dv = y_ref[...], z_ref[...], d_ref[...]
        s = _sig(zv)
        sz = zv * s
        yg = yv * sz
        r = lax.rsqrt(jnp.mean(yg * yg, axis=-1, keepdims=True) + LN_EPS)
        dg_ref[...] += jnp.sum(dv * yg * r, axis=0, keepdims=True)
        dn = dv * g_ref[...]
        dyg = r * dn - yg * (r * r * r) * jnp.mean(dn * yg, axis=-1, keepdims=True)
        dy_ref[...] = dyg * sz
        dz_ref[...] = (dyg * yv * s * (1.0 + zv * (1.0 - s))).astype(BF16)

    return pl.pallas_call(
        body, name=name, grid=(T // ROW_T,), in_specs=[_rows(T), _rows(T), _rows(T), _vec()],
        out_specs=(_rows(T), _rows(T), _vec()),
        out_shape=(jax.ShapeDtypeStruct((T, D), F32), jax.ShapeDtypeStruct((T, D), BF16), jax.ShapeDtypeStruct((1, D), F32)),
        compiler_params=_cparams(("arbitrary",)))(y, z, dout, g)


def _loss_head(y, target, name):
    T = y.shape[0]

    def body(y_ref, t_ref, s_ref, d_ref):
        i = pl.program_id(0)

        @pl.when(i == 0)
        def _():
            s_ref[...] = jnp.zeros_like(s_ref)

        err = y_ref[...] - t_ref[...]
        s_ref[...] += jnp.sum(jnp.sum(err * err, axis=1, keepdims=True), axis=0, keepdims=True)
        d_ref[...] = err * (1.0 / D)

    return pl.pallas_call(
        body, name=name, grid=(T // ROW_T,), in_specs=[_rows(T), _rows(T)],
        out_specs=(pl.BlockSpec((SUBLANES, LANES), lambda i: (0, 0)), _rows(T)),
        out_shape=(jax.ShapeDtypeStruct((SUBLANES, LANES), F32), jax.ShapeDtypeStruct((T, D), F32)),
        compiler_params=_cparams(("arbitrary",)))(y, target)


def _taps_fwd(pad_ref, w_ref, K, base):
    off = CONV_PAD - (K - 1)
    acc = w_ref[0:1, :] * pad_ref[pl.ds(base + off, CONV_R), :]
    for k in range(1, K):
        acc = acc + w_ref[k:k + 1, :] * pad_ref[pl.ds(base + off + k, CONV_R), :]
    return acc


def _taps_bwd(padd_ref, w_ref, K, base):
    acc = w_ref[0:1, :] * padd_ref[pl.ds(base + (K - 1), CONV_R), :]
    for k in range(1, K):
        acc = acc + w_ref[k:k + 1, :] * padd_ref[pl.ds(base + (K - 1) - k, CONV_R), :]
    return acc


def _f32(ref, rows):
    return ref[rows, :].astype(F32)


def _fold8(v):
    return v.reshape(CONV_R // SUBLANES, SUBLANES, v.shape[-1]).sum(0)


def _wgrad_acc(dw_ref, pad_ref, d, K, base):
    off = CONV_PAD - (K - 1)
    for k in range(K):
        dw_ref[k * SUBLANES:(k + 1) * SUBLANES, :] += _fold8(d * pad_ref[pl.ds(base + off + k, CONV_R), :])


def _loop_rows(T, fn):
    def step(r, carry):
        fn(pl.multiple_of(r * CONV_R, CONV_R))
        return carry
    lax.fori_loop(0, T // CONV_R, step, 0)


def _col(T, off_blocks=0, rows=None):
    return pl.BlockSpec((T if rows is None else rows, LANES), lambda j: (0, j + off_blocks))


def _conv_call(body, name, T, n_tiles, in_specs, out_specs, out_shape, n_pad, n_padd=0):
    scratch = [pltpu.VMEM((T + CONV_PAD, LANES), F32)] * (n_pad + n_padd)
    return pl.pallas_call(body, name=name, grid=(n_tiles,), in_specs=in_specs, out_specs=out_specs, out_shape=out_shape,
                          scratch_shapes=scratch, compiler_params=_cparams(("parallel",)))


def _zero_head(ref):
    ref[0:CONV_PAD, :] = jnp.zeros((CONV_PAD, LANES), F32)


def _zero_tail(ref, T):
    ref[T:T + CONV_PAD, :] = jnp.zeros((CONV_PAD, LANES), F32)


def _sds(shape, dtype=F32):
    return jax.ShapeDtypeStruct(shape, dtype)


def _conv_a_fwd(ua, w, b, name):
    T = ua.shape[0]
    K, nt = CONV_A, D // LANES

    def body(al_ref, ag_ref, w_ref, b_ref, o_ref, pad_ref):
        _zero_head(pad_ref)

        def pre(base):
            rows = pl.ds(base, CONV_R)
            pad_ref[pl.ds(base + CONV_PAD, CONV_R), :] = _f32(al_ref, rows) * _sig(_f32(ag_ref, rows))
        _loop_rows(T, pre)

        def main(base):
            o_ref[pl.ds(base, CONV_R), :] = _taps_fwd(pad_ref, w_ref, K, base) + b_ref[...]
        _loop_rows(T, main)

    return _conv_call(body, name, T, nt, [_col(T), _col(T, nt), _col(T, rows=K), _col(T, rows=1)], _col(T),
                      _sds((T, D)), 1)(ua, ua, w, b)


def _conv_a_bwd(ua, w, dac, name):
    T = ua.shape[0]
    K, nt = CONV_A, D // LANES

    def body(al_ref, ag_ref, w_ref, d_ref, dal_ref, dag_ref, dw_ref, db_ref, pad_ref, padd_ref):
        _zero_head(pad_ref)
        _zero_tail(padd_ref, T)
        dw_ref[...] = jnp.zeros_like(dw_ref)
        db_ref[...] = jnp.zeros_like(db_ref)

        def pre(base):
            rows = pl.ds(base, CONV_R)
            pad_ref[pl.ds(base + CONV_PAD, CONV_R), :] = _f32(al_ref, rows) * _sig(_f32(ag_ref, rows))
            padd_ref[rows, :] = d_ref[rows, :]
        _loop_rows(T, pre)

        def main(base):
            rows = pl.ds(base, CONV_R)
            d = d_ref[rows, :]
            _wgrad_acc(dw_ref, pad_ref, d, K, base)
            db_ref[...] += _fold8(d)
            da = _taps_bwd(padd_ref, w_ref, K, base)
            al, s = _f32(al_ref, rows), _sig(_f32(ag_ref, rows))
            dal_ref[rows, :] = (da * s).astype(BF16)
            dag_ref[rows, :] = (da * al * s * (1.0 - s)).astype(BF16)
        _loop_rows(T, main)

    return _conv_call(body, name, T, nt, [_col(T), _col(T, nt), _col(T, rows=K), _col(T)],
                      (_col(T), _col(T), _col(T, rows=K * SUBLANES), _col(T, rows=SUBLANES)),
                      (_sds((T, D), BF16), _sds((T, D), BF16), _sds((K * SUBLANES, D)), _sds((SUBLANES, D))), 1, 1)(ua, ua, w, dac)


def _conv_b_fwd(xu, w, b, name):
    T, C = xu.shape
    K, nt = CONV_B, C // LANES

    def body(x_ref, w_ref, b_ref, o_ref, pad_ref):
        _zero_head(pad_ref)
        pad_ref[CONV_PAD:CONV_PAD + T, :] = x_ref[...].astype(F32)

        def main(base):
            hc = _taps_fwd(pad_ref, w_ref, K, base) + b_ref[...]
            o_ref[pl.ds(base, CONV_R), :] = hc * _sig(hc)
        _loop_rows(T, main)

    return _conv_call(body, name, T, nt, [_col(T), _col(T, rows=K), _col(T, rows=1)], _col(T), _sds((T, C)), 1)(xu, w, b)


def _conv_b_bwd(xu, w, b, dxs, dbs, dcs, name):
    T, C = xu.shape
    K, nt = CONV_B, C // LANES
    nx, nb = dxs.shape[1] // LANES, dbs.shape[1] // LANES

    def body(x_ref, w_ref, b_ref, d1_ref, d2_ref, d3_ref, dx_ref, dw_ref, db_ref, pad_ref, padd_ref):
        j = pl.program_id(0)
        _zero_head(pad_ref)
        _zero_tail(padd_ref, T)
        dw_ref[...] = jnp.zeros_like(dw_ref)
        db_ref[...] = jnp.zeros_like(db_ref)
        pad_ref[CONV_PAD:CONV_PAD + T, :] = x_ref[...].astype(F32)

        def pre(base):
            rows = pl.ds(base, CONV_R)
            hc = _taps_fwd(pad_ref, w_ref, K, base) + b_ref[...]
            s = _sig(hc)
            d = jnp.where(j < nx, d1_ref[rows, :], jnp.where(j < nx + nb, d2_ref[rows, :], d3_ref[rows, :]))
            padd_ref[rows, :] = d * s * (1.0 + hc * (1.0 - s))
        _loop_rows(T, pre)

        def main(base):
            d = padd_ref[pl.ds(base, CONV_R), :]
            _wgrad_acc(dw_ref, pad_ref, d, K, base)
            db_ref[...] += _fold8(d)
            dx_ref[pl.ds(base, CONV_R), :] = _taps_bwd(padd_ref, w_ref, K, base).astype(BF16)
        _loop_rows(T, main)

    def piece(lo, n):
        return pl.BlockSpec((T, LANES), lambda j: (0, jnp.clip(j - lo, 0, n - 1)))

    return _conv_call(body, name, T, nt,
                      [_col(T), _col(T, rows=K), _col(T, rows=1), piece(0, nx), piece(nx, nb), piece(nx + nb, nt - nx - nb)],
                      (_col(T), _col(T, rows=K * SUBLANES), _col(T, rows=SUBLANES)),
                      (_sds((T, C), BF16), _sds((K * SUBLANES, C)), _sds((SUBLANES, C))), 1, 1)(xu, w, b, dxs, dbs, dcs)


def _conv_c_fwd(uo, w, name):
    T = uo.shape[0]
    K, nt = CONV_C, D // LANES

    def body(bg_ref, cg_ref, v_ref, w_ref, o_ref, pad_ref):
        _zero_head(pad_ref)
        pad_ref[CONV_PAD:CONV_PAD + T, :] = cg_ref[...].astype(F32) * v_ref[...].astype(F32)

        def main(base):
            rows = pl.ds(base, CONV_R)
            o_ref[rows, :] = (_f32(bg_ref, rows) * _taps_fwd(pad_ref, w_ref, K, base)).astype(BF16)
        _loop_rows(T, main)

    return _conv_call(body, name, T, nt, [_col(T), _col(T, nt), _col(T, 2 * nt), _col(T, rows=K)], _col(T),
                      _sds((T, D), BF16), 1)(uo, uo, uo, w)


def _conv_c_bwd(uo, w, dsc, name):
    T = uo.shape[0]
    K, nt = CONV_C, D // LANES

    def body(bg_ref, cg_ref, v_ref, w_ref, d_ref, dbg_ref, dcg_ref, dv_ref, dw_ref, pad_ref, padd_ref):
        _zero_head(pad_ref)
        _zero_tail(padd_ref, T)
        dw_ref[...] = jnp.zeros_like(dw_ref)
        pad_ref[CONV_PAD:CONV_PAD + T, :] = cg_ref[...].astype(F32) * v_ref[...].astype(F32)

        def pre(base):
            rows = pl.ds(base, CONV_R)
            d = d_ref[rows, :]
            dbg_ref[rows, :] = (d * _taps_fwd(pad_ref, w_ref, K, base)).astype(BF16)
            padd_ref[rows, :] = d * _f32(bg_ref, rows)
        _loop_rows(T, pre)

        def main(base):
            rows = pl.ds(base, CONV_R)
            _wgrad_acc(dw_ref, pad_ref, padd_ref[rows, :], K, base)
            dq = _taps_bwd(padd_ref, w_ref, K, base)
            dcg_ref[rows, :] = (dq * _f32(v_ref, rows)).astype(BF16)
            dv_ref[rows, :] = (dq * _f32(cg_ref, rows)).astype(BF16)
        _loop_rows(T, main)

    return _conv_call(body, name, T, nt, [_col(T), _col(T, nt), _col(T, 2 * nt), _col(T, rows=K), _col(T)],
                      (_col(T), _col(T), _col(T), _col(T, rows=K * SUBLANES)),
                      (_sds((T, D), BF16), _sds((T, D), BF16), _sds((T, D), BF16), _sds((K * SUBLANES, D))), 1, 1)(uo, uo, uo, w, dsc)


def _conv_f_fwd(up, w, b, name):
    T = up.shape[0]
    K, nt = CONV_F, D_FF // LANES

    def body(u1_ref, u2_ref, w1_ref, w2_ref, b1_ref, b2_ref, o_ref, pad1_ref, pad2_ref):
        _zero_head(pad1_ref)
        _zero_head(pad2_ref)
        pad1_ref[CONV_PAD:CONV_PAD + T, :] = u1_ref[...].astype(F32)
        pad2_ref[CONV_PAD:CONV_PAD + T, :] = u2_ref[...].astype(F32)

        def main(base):
            h1 = _taps_fwd(pad1_ref, w1_ref, K, base) + b1_ref[...]
            h2 = _taps_fwd(pad2_ref, w2_ref, K, base) + b2_ref[...]
            o_ref[pl.ds(base, CONV_R), :] = (h1 * _sig(h1) * h2).astype(BF16)
        _loop_rows(T, main)

    return _conv_call(body, name, T, nt,
                      [_col(T), _col(T, nt), _col(T, rows=K), _col(T, nt, rows=K), _col(T, rows=1), _col(T, nt, rows=1)],
                      _col(T), _sds((T, D_FF), BF16), 2)(up, up, w, w, b, b)


def _conv_f_bwd(up, w, b, dact, name):
    T = up.shape[0]
    K, nt = CONV_F, D_FF // LANES

    def body(u1_ref, u2_ref, w1_ref, w2_ref, b1_ref, b2_ref, d_ref, du1_ref, du2_ref, dw1_ref, dw2_ref, db1_ref, db2_ref,
             pad1_ref, pad2_ref, padd1_ref, padd2_ref):
        _zero_head(pad1_ref)
        _zero_head(pad2_ref)
        _zero_tail(padd1_ref, T)
        _zero_tail(padd2_ref, T)
        for r in (dw1_ref, dw2_ref, db1_ref, db2_ref):
            r[...] = jnp.zeros_like(r)
        pad1_ref[CONV_PAD:CONV_PAD + T, :] = u1_ref[...].astype(F32)
        pad2_ref[CONV_PAD:CONV_PAD + T, :] = u2_ref[...].astype(F32)

        def pre(base):
            rows = pl.ds(base, CONV_R)
            h1 = _taps_fwd(pad1_ref, w1_ref, K, base) + b1_ref[...]
            h2 = _taps_fwd(pad2_ref, w2_ref, K, base) + b2_ref[...]
            s = _sig(h1)
            d = _f32(d_ref, rows)
            padd1_ref[rows, :] = d * h2 * s * (1.0 + h1 * (1.0 - s))
            padd2_ref[rows, :] = d * h1 * s
        _loop_rows(T, pre)

        def main(base):
            rows = pl.ds(base, CONV_R)
            d1, d2 = padd1_ref[rows, :], padd2_ref[rows, :]
            _wgrad_acc(dw1_ref, pad1_ref, d1, K, base)
            _wgrad_acc(dw2_ref, pad2_ref, d2, K, base)
            db1_ref[...] += _fold8(d1)
            db2_ref[...] += _fold8(d2)
            du1_ref[rows, :] = _taps_bwd(padd1_ref, w1_ref, K, base).astype(BF16)
            du2_ref[rows, :] = _taps_bwd(padd2_ref, w2_ref, K, base).astype(BF16)
        _loop_rows(T, main)

    wrow, brow = _col(T, rows=K * SUBLANES), _col(T, rows=SUBLANES)
    return _conv_call(body, name, T, nt,
                      [_col(T), _col(T, nt), _col(T, rows=K), _col(T, nt, rows=K), _col(T, rows=1), _col(T, nt, rows=1), _col(T)],
                      (_col(T), _col(T), wrow, wrow, brow, brow),
                      (_sds((T, D_FF), BF16), _sds((T, D_FF), BF16), _sds((K * SUBLANES, D_FF)), _sds((K * SUBLANES, D_FF)),
                       _sds((SUBLANES, D_FF)), _sds((SUBLANES, D_FF))), 2, 2)(up, up, w, w, b, b, dact)


def _dot(a, b, dims="nn"):
    return lax.dot_general(a.astype(BF16), b.astype(BF16), _DIMS[dims], preferred_element_type=F32)


def _dot_mask(mask, v, mask_left):
    mb = mask.astype(BF16)
    hi = v.astype(BF16)
    r1 = v - hi.astype(F32)
    mid = r1.astype(BF16)
    lo = (r1 - mid.astype(F32)).astype(BF16)
    d = [jnp.dot(mb, t, preferred_element_type=F32) if mask_left else jnp.dot(t, mb, preferred_element_type=F32) for t in (hi, mid, lo)]
    return (d[0] + d[1]) + d[2]


def _ssd_small(xcr_ref, xrr_ref, bc_ref, br_ref, ac_ref, ar_ref):
    Q = SSD_Q
    li = lax.broadcasted_iota(jnp.int32, (Q, Q), 0)
    si = lax.broadcasted_iota(jnp.int32, (Q, Q), 1)
    tril = li >= si
    dtc = jax.nn.softplus(xcr_ref[...] + bc_ref[...])
    dtr = jax.nn.softplus(xrr_ref[...] + br_ref[...])
    cumc = _dot_mask(tril, dtc * ac_ref[...], True)
    cumr = _dot_mask(li <= si, dtr * ar_ref[...], False)
    return tril, dtc, dtr, cumc, cumr


def _ssd_specs(nc, rev):
    Q = SSD_Q
    cc = (lambda c: nc - 1 - c) if rev else (lambda c: c)
    x_spec = pl.BlockSpec((Q, 2 * LANES), lambda g, c: (cc(c), g))
    b_spec = pl.BlockSpec((Q, LANES), lambda g, c: (cc(c), 8 + g))
    c_spec = pl.BlockSpec((Q, LANES), lambda g, c: (cc(c), 12 + g))
    colm = pl.BlockSpec((None, Q, LANES), lambda g, c: (g, cc(c), 0))
    rowm = pl.BlockSpec((None, SUBLANES, Q), lambda g, c: (g, 0, cc(c)))
    colv = pl.BlockSpec((None, 1, LANES), lambda g, c: (g, 0, 0))
    rowv = pl.BlockSpec((None, SUBLANES, 1), lambda g, c: (g, 0, 0))
    st_spec = pl.BlockSpec((None, None, 2 * LANES, N_STATE), lambda g, c: (cc(c), g, 0, 0))
    return x_spec, b_spec, c_spec, colm, rowm, colv, rowv, st_spec


def _ssd_fwd(xc, raw_col, raw_row, bias_col, bias_row, a_col, a_row, dskip, name):
    T = xc.shape[0]
    Q = SSD_Q
    nc = T // Q
    x_spec, b_spec, c_spec, colm, rowm, colv, rowv, st_spec = _ssd_specs(nc, False)

    def body(dk_ref, x_ref, b_ref, c_ref, xcr_ref, xrr_ref, bc_ref, br_ref, ac_ref, ar_ref, y_ref, st_ref, h_ref):
        g = pl.program_id(0)

        @pl.when(pl.program_id(1) == 0)
        def _():
            h_ref[...] = jnp.zeros_like(h_ref)

        tril, dtc, dtr, cumc, cumr = _ssd_small(xcr_ref, xrr_ref, bc_ref, br_ref, ac_ref, ar_ref)
        Bm, Cm = b_ref[...], c_ref[...]
        S = _dot(Cm, Bm, "nt")
        lo = lax.broadcasted_iota(jnp.int32, (Q, LANES), 1) < HEAD_P
        rlo = lax.broadcasted_iota(jnp.int32, (LANES, N_STATE), 0) < HEAD_P
        st_ref[...] = h_ref[...]
        clast = cumc[Q - 1:Q, :]
        for pr in range(2):
            cols = slice(pr * LANES, (pr + 1) * LANES)
            xp = x_ref[:, cols]
            yd = jnp.zeros((Q, LANES), F32)
            for q in range(2):
                hh = 2 * pr + q
                seg = cumc[:, hh:hh + 1] - cumr[hh:hh + 1, :]
                lm = jnp.where(tril, jnp.exp(jnp.where(tril, seg, 0.0)), 0.0)
                w = S * lm * dtr[hh:hh + 1, :]
                xm = jnp.where(lo if q == 0 else jnp.logical_not(lo), xp, 0.0)
                yd = yd + _dot(w, xm)
            h0, h1 = 2 * pr, 2 * pr + 1
            c0, c1 = cumc[:, h0:h0 + 1], cumc[:, h1:h1 + 1]
            e_pair = jnp.where(lo, jnp.exp(c0), jnp.exp(c1))
            hp = h_ref[cols, :]
            ch = _dot(Cm, hp, "nt")
            dsk = jnp.where(lo, dk_ref[4 * g + h0], dk_ref[4 * g + h1])
            y_ref[:, cols] = yd + e_pair * ch + dsk * xp
            cl0, cl1 = clast[:, h0:h0 + 1], clast[:, h1:h1 + 1]
            sdec = jnp.where(lo, jnp.exp(cl0 - c0) * dtc[:, h0:h0 + 1], jnp.exp(cl1 - c1) * dtc[:, h1:h1 + 1])
            decrow = jnp.where(rlo, jnp.exp(cl0), jnp.exp(cl1))
            h_ref[cols, :] = hp * decrow + _dot(xp * sdec, Bm, "tn")

    smem = pl.BlockSpec(memory_space=pltpu.SMEM)
    return pl.pallas_call(
        body, name=name, grid=(N_GROUPS, nc),
        in_specs=[smem, x_spec, b_spec, c_spec, colm, rowm, colv, rowv, colv, rowv],
        out_specs=(x_spec, st_spec),
        out_shape=(_sds((T, D)), _sds((nc, N_GROUPS, 2 * LANES, N_STATE))),
        scratch_shapes=[pltpu.VMEM((2 * LANES, N_STATE), F32)],
        compiler_params=_cparams(("parallel", "arbitrary")))(dskip, xc, xc, xc, raw_col, raw_row, bias_col, bias_row, a_col, a_row)


def _ssd_bwd(xc, raw_col, raw_row, bias_col, bias_row, a_col, a_row, dskip, states, dy, name):
    T = xc.shape[0]
    Q = SSD_Q
    nc = T // Q
    x_spec, b_spec, c_spec, colm, rowm, colv, rowv, st_spec = _ssd_specs(nc, True)
    bo_spec = pl.BlockSpec((Q, LANES), lambda g, c: (nc - 1 - c, g))
    dd_spec = pl.BlockSpec((None, None, SUBLANES, 2 * LANES), lambda g, c: (nc - 1 - c, g, 0, 0))

    def body(dk_ref, x_ref, b_ref, c_ref, xcr_ref, xrr_ref, bc_ref, br_ref, ac_ref, ar_ref, st_ref, dy_ref,
             dx_ref, db_ref, dc_ref, sq_ref, cms_ref, ddac_ref, ddar_ref, dd_ref, dh_ref):
        g = pl.program_id(0)

        @pl.when(pl.program_id(1) == 0)
        def _():
            dh_ref[...] = jnp.zeros_like(dh_ref)

        tril, dtc, dtr, cumc, cumr = _ssd_small(xcr_ref, xrr_ref, bc_ref, br_ref, ac_ref, ar_ref)
        Bm, Cm = b_ref[...], c_ref[...]
        S = _dot(Cm, Bm, "nt")
        lane = lax.broadcasted_iota(jnp.int32, (Q, LANES), 1)
        sub = lax.broadcasted_iota(jnp.int32, (SUBLANES, Q), 0)
        rowi = lax.broadcasted_iota(jnp.int32, (Q, LANES), 0)
        lo = lane < HEAD_P
        rlo = lax.broadcasted_iota(jnp.int32, (LANES, N_STATE), 0) < HEAD_P
        clast = cumc[Q - 1:Q, :]
        ds_g = jnp.zeros((Q, Q), F32)
        dcm = jnp.zeros((Q, N_STATE), F32)
        dbm = jnp.zeros((Q, N_STATE), F32)
        dcum_col = jnp.zeros((Q, LANES), F32)
        dcum_row = jnp.zeros((SUBLANES, Q), F32)
        sq_col = jnp.zeros((Q, LANES), F32)
        cms_row = jnp.zeros((SUBLANES, Q), F32)
        for pr in range(2):
            cols = slice(pr * LANES, (pr + 1) * LANES)
            xp, dyp = x_ref[:, cols], dy_ref[:, cols]
            hin, dhp = st_ref[cols, :], dh_ref[cols, :]
            h0, h1 = 2 * pr, 2 * pr + 1
            c0, c1 = cumc[:, h0:h0 + 1], cumc[:, h1:h1 + 1]
            cl0, cl1 = clast[:, h0:h0 + 1], clast[:, h1:h1 + 1]
            e_pair = jnp.where(lo, jnp.exp(c0), jnp.exp(c1))
            edec = jnp.where(lo, jnp.exp(cl0 - c0), jnp.exp(cl1 - c1))
            dt_pair = jnp.where(lo, dtc[:, h0:h0 + 1], dtc[:, h1:h1 + 1])
            sdec = edec * dt_pair
            ch = _dot(Cm, hin, "nt")
            xb = _dot(Bm, dhp, "nt")
            dye = dyp * e_pair
            t1 = dye * ch
            t2 = xp * xb * edec
            hh_prod = dhp * hin
            dsk = jnp.where(lo, dk_ref[4 * g + h0], dk_ref[4 * g + h1])
            dxp = sdec * xb + dsk * dyp
            for q in range(2):
                hh = 2 * pr + q
                mine = lo if q == 0 else jnp.logical_not(lo)
                seg = cumc[:, hh:hh + 1] - cumr[hh:hh + 1, :]
                lm = jnp.where(tril, jnp.exp(jnp.where(tril, seg, 0.0)), 0.0)
                dtrow = dtr[hh:hh + 1, :]
                w = S * lm * dtrow
                dym = jnp.where(mine, dyp, 0.0)
                gl = _dot(dym, xp, "nt") * lm
                ds_g = ds_g + gl * dtrow
                ms = gl * S
                m = ms * dtrow
                dxp = dxp + _dot(w, dym, "tn")
                cms_row = jnp.where(sub == hh, jnp.sum(ms, axis=0, keepdims=True), cms_row)
                dcum_row = jnp.where(sub == hh, -jnp.sum(m, axis=0, keepdims=True), dcum_row)
                t1h = jnp.sum(jnp.where(mine, t1, 0.0), axis=1, keepdims=True)
                sqh = jnp.sum(jnp.where(mine, t2, 0.0), axis=1, keepdims=True)
                sth = sqh * dtc[:, hh:hh + 1]
                rmine = rlo if q == 0 else jnp.logical_not(rlo)
                hsum = jnp.sum(jnp.sum(jnp.where(rmine, hh_prod, 0.0), axis=1, keepdims=True), axis=0, keepdims=True)
                last = jnp.sum(sth, axis=0, keepdims=True) + jnp.exp(clast[:, hh:hh + 1]) * hsum
                dcol = jnp.sum(m, axis=1, keepdims=True) + t1h - sth
                dcum_col = jnp.where(lane == hh, dcol + jnp.where(rowi == Q - 1, last, 0.0), dcum_col)
                sq_col = jnp.where(lane == hh, sqh, sq_col)
            dcm = dcm + _dot(dye, hin)
            dbm = dbm + _dot(xp * sdec, dhp)
            decrow = jnp.where(rlo, jnp.exp(cl0), jnp.exp(cl1))
            dh_ref[cols, :] = dhp * decrow + _dot(dye, Cm, "tn")
            dx_ref[:, cols] = dxp
            dd_ref[:, cols] = jnp.broadcast_to(jnp.sum(dyp * xp, axis=0, keepdims=True), (SUBLANES, LANES))
        dc_ref[...] = dcm + _dot(ds_g, Bm)
        db_ref[...] = dbm + _dot(ds_g, Cm, "tn")
        li = lax.broadcasted_iota(jnp.int32, (Q, Q), 0)
        si = lax.broadcasted_iota(jnp.int32, (Q, Q), 1)
        ddac_ref[...] = _dot_mask(li <= si, dcum_col, True)
        ddar_ref[...] = _dot_mask(tril, dcum_row, False)
        sq_ref[...] = sq_col
        cms_ref[...] = cms_row

    smem = pl.BlockSpec(memory_space=pltpu.SMEM)
    return pl.pallas_call(
        body, name=name, grid=(N_GROUPS, nc),
        in_specs=[smem, x_spec, b_spec, c_spec, colm, rowm, colv, rowv, colv, rowv, st_spec, x_spec],
        out_specs=(x_spec, bo_spec, bo_spec, colm, rowm, colm, rowm, dd_spec),
        out_shape=(_sds((T, D)), _sds((T, D // 2)), _sds((T, D // 2)), _sds((N_GROUPS, T, LANES)), _sds((N_GROUPS, SUBLANES, T)),
                   _sds((N_GROUPS, T, LANES)), _sds((N_GROUPS, SUBLANES, T)), _sds((nc, N_GROUPS, SUBLANES, 2 * LANES))),
        scratch_shapes=[pltpu.VMEM((2 * LANES, N_STATE), F32)],
        compiler_params=_cparams(("parallel", "arbitrary")))(dskip, xc, xc, xc, raw_col, raw_row, bias_col, bias_row, a_col, a_row,
                                                            states, dy)


def _adam_math(wv, gv, mv, vv):
    c1 = 1.0 - ADAM_B1 ** ADAM_STEP
    c2 = 1.0 - ADAM_B2 ** ADAM_STEP
    mn = ADAM_B1 * mv + (1.0 - ADAM_B1) * gv
    vn = ADAM_B2 * vv + (1.0 - ADAM_B2) * (gv * gv)
    return -ADAM_LR * ((mn / c1) / (jnp.sqrt(vn / c2) + ADAM_EPS) + ADAM_WD * wv), mn, vn


def _adamw_layers(w, g, m, v, l0, Lg, bufs, name):
    L, As, Bs = w.shape
    tr = _tile(As, [], (256, 352, 128))
    has_bufs = bufs is not None

    def body(*refs):
        w_ref, g_ref, m_ref, v_ref = refs[:4]
        d_ref, mo_ref, vo_ref = refs[4 + 3 * has_bufs:]
        d_ref[...], mo_ref[...], vo_ref[...] = _adam_math(w_ref[...], g_ref[...], m_ref[...], v_ref[...])

    spec = pl.BlockSpec((None, tr, Bs), lambda l, i: (l + l0, i, 0))
    args = (w, g, m, v) + (tuple(bufs) if has_bufs else ())
    return pl.pallas_call(
        body, name=name, grid=(Lg, As // tr), in_specs=[spec] * 4 + [_ANY] * (3 * has_bufs), out_specs=(spec,) * 3,
        out_shape=(_sds((L, As, Bs)),) * 3, input_output_aliases={4: 0, 5: 1, 6: 2} if has_bufs else {},
        compiler_params=_cparams(("parallel", "parallel")))(*args)


def _adamw_minor_rows(w, g, m, v, name):
    L, R, C = w.shape
    tr = _tile(C, [], (12, 8, 4, 2, 1))
    wt, gt, mt, vt = (jnp.transpose(t, (2, 0, 1)) for t in (w, g, m, v))

    def body(w_ref, g_ref, m_ref, v_ref, d_ref, mo_ref, vo_ref):
        d_ref[...], mo_ref[...], vo_ref[...] = _adam_math(w_ref[...], g_ref[...], m_ref[...], v_ref[...])

    spec = pl.BlockSpec((tr, L, R), lambda i: (i, 0, 0))
    out = pl.pallas_call(body, name=name, grid=(C // tr,), in_specs=[spec] * 4, out_specs=(spec,) * 3,
                         out_shape=(_sds((C, L, R)),) * 3, compiler_params=_cparams(("parallel",)))(wt, gt, mt, vt)
    return tuple(jnp.transpose(o, (1, 2, 0)) for o in out) + (jnp.transpose(gt, (1, 2, 0)),)


def _adamw(w, g, m, v, name):
    shape = w.shape
    cols = shape[-1]
    w2, g2, m2, v2 = (t.reshape(-1, cols) for t in (w, g, m, v))
    rows = w2.shape[0]
    tr = 256 if (rows % 256 == 0 and rows > 256) else rows
    c1 = 1.0 - ADAM_B1 ** ADAM_STEP
    c2 = 1.0 - ADAM_B2 ** ADAM_STEP

    def body(w_ref, g_ref, m_ref, v_ref, d_ref, mo_ref, vo_ref):
        gv = g_ref[...]
        mn = ADAM_B1 * m_ref[...] + (1.0 - ADAM_B1) * gv
        vn = ADAM_B2 * v_ref[...] + (1.0 - ADAM_B2) * (gv * gv)
        d_ref[...] = -ADAM_LR * ((mn / c1) / (jnp.sqrt(vn / c2) + ADAM_EPS) + ADAM_WD * w_ref[...])
        mo_ref[...] = mn
        vo_ref[...] = vn

    spec = pl.BlockSpec((tr, cols), lambda i: (i, 0))
    out = pl.pallas_call(body, name=name, grid=(rows // tr,), in_specs=[spec] * 4, out_specs=(spec,) * 3,
                         out_shape=(_sds((rows, cols)),) * 3, compiler_params=_cparams(("parallel",)))(w2, g2, m2, v2)
    return tuple(o.reshape(shape) for o in out)


def _place():
    x, y, c = lax.axis_index("x"), lax.axis_index("y"), lax.axis_index("c")
    chips = [(1 - x, y), (x, 1 - y), (1 - x, 1 - y)]
    return x, y, c, chips


_ANY = pl.BlockSpec(memory_space=pl.ANY)


TENSORS = (("e_w_in", "row", 2, 4096, 1284, 1024), ("e_w_out", "row", 2, 2048, 1024, 512), ("o_w_in", "col", 2, 1024, 3072, 768),
           ("o_w_out", "row", 2, 1024, 1024, 256), ("f_w_up", "col", 4, 1024, 5632, 1408), ("f_w_down", "row", 4, 2816, 1024, 704),
           ("ple_w_proj", "col", 4, 256, 1024, 256), ("ple_w_gate", "row", 4, 1024, 1024, 256))
MIX, FFN = "mix", "ffn"
W_GROUPS = (((0, MIX),), ((0, FFN), (1, MIX)), ((1, FFN), (2, MIX)), ((2, FFN), (3, MIX), (3, FFN)))
G_GROUPS = (((3, FFN), (3, MIX), (2, FFN), (2, MIX), (1, FFN), (1, MIX)), ((0, FFN),), ((0, MIX),))


def _tensor_layer(name, layer):
    if name.startswith("e_"):
        return layer // 2 if layer % 2 == 0 else None
    if name.startswith("o_"):
        return layer // 2 if layer % 2 == 1 else None
    return layer


def _part(name):
    return MIX if name.startswith(("e_", "o_")) else FFN


def _group_items(members):
    items = []
    for name, kind, L, A, B, n in TENSORS:
        tls = sorted(t for t in (_tensor_layer(name, l) for l, part in members if part == _part(name)) if t is not None)
        if tls:
            assert tls == list(range(tls[0], tls[0] + len(tls)))
            items.append((name, kind, len(tls), A, B, n, tls[0]))
    return items


def _hwin(ref, it, k, h):
    name, kind, Lg, A, B, n, l0 = it
    if kind == "row":
        return ref.at[:, pl.ds(pl.multiple_of(k * n + h * (n // 2), 16), n // 2), :]
    return ref.at[:, pl.ds(pl.multiple_of(h * (A // 2), 16), A // 2), pl.ds(pl.multiple_of(k * n, LANES), n)]


def _shard_dims(kind, A, B, n):
    return (n, B) if kind == "row" else (A, n)


def _cast_into(w, it, me):
    name, kind, Lg, A, B, n, l0 = it
    As, Bs = _shard_dims(kind, A, B, n)

    def body(me_ref, w_ref, o_ref):
        o_ref[...] = w_ref[...].astype(BF16)

    omap = (lambda l, m: (l, m[0], 0)) if kind == "row" else (lambda l, m: (l, 0, m[0]))
    grid_spec = pltpu.PrefetchScalarGridSpec(
        num_scalar_prefetch=1, grid=(Lg,), in_specs=[pl.BlockSpec((None, As, Bs), lambda l, m: (l + l0, 0, 0))],
        out_specs=pl.BlockSpec((None, As, Bs), omap))
    return pl.pallas_call(body, name=f"cast_{name}_{l0}", grid_spec=grid_spec, out_shape=_sds((Lg, A, B), BF16),
                          compiler_params=_cparams(("parallel",)))(me, w.reshape(-1, As, Bs))


_HBM = pl.BlockSpec(memory_space=pltpu.HBM)
_SEM = pl.BlockSpec(memory_space=pltpu.SEMAPHORE)
_EFFECT = pltpu.SideEffectType.DATAFLOW_SIDE_EFFECTING


def _hbm(a):
    return pltpu.with_memory_space_constraint(a, pltpu.HBM)


def _split_start(thru, n_copies, issue, name, after=None):
    N = len(thru)
    has_after = after is not None

    def body(*refs):
        outs = refs[N + has_after:2 * N + has_after]
        send_sems, recv_sems, token = refs[2 * N + has_after:]
        for cp in issue(outs, send_sems, recv_sems):
            cp.start()
        token[...] = jnp.zeros_like(token)

    out = pl.pallas_call(
        body, name=name, in_specs=[_HBM] * N + ([_ANY] if has_after else []),
        out_specs=(_HBM,) * N + (_SEM, _SEM, pl.BlockSpec(memory_space=pltpu.VMEM)),
        out_shape=tuple(pltpu.HBM(a.shape, a.dtype) for a in thru)
        + (pltpu.SemaphoreType.DMA((n_copies,)), pltpu.SemaphoreType.DMA((n_copies,)), _sds((SUBLANES, LANES))),
        input_output_aliases={t: t for t in range(N)},
        compiler_params=pltpu.CompilerParams(has_side_effects=_EFFECT))(*[_hbm(a) for a in thru], *([after] if has_after else []))
    return list(out[:N]), out[N], out[N + 1], out[N + 2]


def _split_wait(thru, send_sems, recv_sems, after, waits, name):
    N = len(thru)
    after = list(after) if isinstance(after, (list, tuple)) else [after]

    def body(*refs):
        ins = refs[:N]
        for cp, side in waits(ins, refs[N], refs[N + 1]):
            if side == "send":
                cp.wait_send()
            else:
                cp.wait_recv()

    out = pl.pallas_call(
        body, name=name, in_specs=[_HBM] * N + [_SEM, _SEM] + [_ANY] * len(after), out_specs=(_HBM,) * N,
        out_shape=tuple(pltpu.HBM(a.shape, a.dtype) for a in thru), input_output_aliases={t: t for t in range(N)},
        compiler_params=pltpu.CompilerParams(has_side_effects=_EFFECT))(*thru, send_sems, recv_sems, *after)
    return list(out)


def _rcopy(send_sems, recv_sems, k, src, dst, to):
    return pltpu.make_async_remote_copy(src_ref=src, dst_ref=dst, send_sem=send_sems.at[k], recv_sem=recv_sems.at[k],
                                        device_id=to, device_id_type=MESH)


def _gather_copies(items, refs, send_sems, recv_sems, what):
    x, y, c, chips = _place()
    me = 2 * x + y
    out = []
    for t, it in enumerate(items):
        mine = _hwin(refs[t], it, me, c)
        for j, (px, py) in enumerate(chips):
            if what == "start":
                out.append(_rcopy(send_sems, recv_sems, 3 * t + j, mine, mine, (px, py, c)))
            else:
                slot = _hwin(refs[t], it, 2 * px + py, c)
                out.append((_rcopy(send_sems, recv_sems, 3 * t + j, mine, mine, (px, py, c)), "send"))
                out.append((_rcopy(send_sems, recv_sems, 3 * t + j, slot, slot, (px, py, c)), "recv"))
    return out


def _gather_start(fulls, items, name, after=None):
    return _split_start(fulls, 3 * len(items), functools.partial(_gather_copies, items, what="start"), name, after)


def _gather_wait(fulls, send_sems, recv_sems, after, items, name):
    return _split_wait(fulls, send_sems, recv_sems, after, functools.partial(_gather_copies, items, what="wait"), name)


def _gather_fwd(fulls, items, name, ws=None):
    N = len(fulls)
    has_ws = ws is not None

    def body(*refs):
        outs = refs[N + has_ws:2 * N + has_ws]
        rest = refs[2 * N + has_ws:]
        x, y, c, chips = _place()
        me = 2 * x + y
        sib = (x, y, 1 - c)
        if has_ws:
            ws_ref = refs[N]
            WS_ref, send_sems, recv_sems, lsem = rest
            loc = pltpu.make_async_copy(ws_ref, WS_ref.at[me], lsem)
            loc.start()
        else:
            send_sems, recv_sems = rest
        rc = functools.partial(_rcopy, send_sems, recv_sems)
        cps = []
        for t, it in enumerate(items):
            for j, (px, py) in enumerate(chips):
                slot = _hwin(outs[t], it, 2 * px + py, c)
                cps.append(rc(3 * t + j, slot, slot, sib))
        if has_ws:
            cps += [rc(3 * N + j, ws_ref, WS_ref.at[me], (*chip, c)) for j, chip in enumerate(chips)]
        for cp in cps:
            cp.start()
        for t, it in enumerate(items):
            for j, (px, py) in enumerate(chips):
                oslot = _hwin(outs[t], it, 2 * px + py, 1 - c)
                rc(3 * t + j, oslot, oslot, sib).wait_recv()
        if has_ws:
            for j, (px, py) in enumerate(chips):
                sslot = WS_ref.at[2 * px + py]
                rc(3 * N + j, sslot, sslot, sib).wait_recv()
        for cp in cps:
            cp.wait_send()
        if has_ws:
            loc.wait()

    ns = 3 * N + (3 if has_ws else 0)
    out_shape = tuple(_sds(f.shape, f.dtype) for f in fulls)
    scratch = [pltpu.SemaphoreType.DMA((ns,)), pltpu.SemaphoreType.DMA((ns,))]
    args = list(fulls)
    if has_ws:
        out_shape += (_sds((4,) + ws.shape, ws.dtype),)
        scratch.append(pltpu.SemaphoreType.DMA(()))
        args.append(ws)
    out = pl.pallas_call(
        body, name=name, in_specs=[_ANY] * len(args), out_specs=(_ANY,) * len(out_shape), out_shape=out_shape,
        input_output_aliases={t: t for t in range(N)}, scratch_shapes=scratch,
        compiler_params=pltpu.CompilerParams(has_side_effects=True))(*args)
    return (list(out[:N]), out[N]) if has_ws else (list(out), None)


def _half_shape(it):
    name, kind, Lg, A, B, n, l0 = it
    return (Lg, 4, n // 2, B) if kind == "row" else (Lg, A // 2, B)


def _piece_shape(it):
    name, kind, Lg, A, B, n, l0 = it
    return (Lg, n // 2, B) if kind == "row" else (Lg, A // 2, n)


def _swap_grads(gs, items, name):
    N = len(gs)

    def body(*refs):
        g_refs, o_refs = refs[:N], refs[N:2 * N]
        send_sems, recv_sems = refs[2 * N:]
        x, y, c, _ = _place()
        sib = (x, y, 1 - c)
        cps = []
        for t, it in enumerate(items):
            name_, kind, Lg, A, B, n, l0 = it
            if kind == "row":
                for k in range(4):
                    cps.append(_rcopy(send_sems, recv_sems, 4 * t + k, _hwin(g_refs[t], it, k, 1 - c), o_refs[t].at[:, k], sib))
            else:
                src = g_refs[t].at[:, pl.ds(pl.multiple_of((1 - c) * (A // 2), 16), A // 2), :]
                cps.append(_rcopy(send_sems, recv_sems, 4 * t, src, o_refs[t], sib))
        for cp in cps:
            cp.start()
        for cp in cps:
            cp.wait()

    return pl.pallas_call(
        body, name=name, in_specs=[_ANY] * N, out_specs=(_ANY,) * N, out_shape=tuple(_sds(_half_shape(it)) for it in items),
        scratch_shapes=[pltpu.SemaphoreType.DMA((4 * N,)), pltpu.SemaphoreType.DMA((4 * N,))],
        compiler_params=pltpu.CompilerParams(has_side_effects=True))(*gs)


def _add_half(g, ra, it, cvec):
    name, kind, Lg, A, B, n, l0 = it
    if kind == "row":
        blk = (None, n // 2, B)
        grid = (Lg, 4)
        g_spec = pl.BlockSpec(blk, lambda l, k, cr: (l, 2 * k + cr[0], 0))
        h_spec = pl.BlockSpec((None, None, n // 2, B), lambda l, k, cr: (l, k, 0, 0))
    else:
        tr = _tile(A // 2, [], (256, 128))
        nb = (A // 2) // tr
        grid = (Lg, nb)
        g_spec = pl.BlockSpec((None, tr, B), lambda l, i, cr: (l, cr[0] * nb + i, 0))
        h_spec = pl.BlockSpec((None, tr, B), lambda l, i, cr: (l, i, 0))

    def body(c_ref, g_ref, r_ref, o_ref):
        o_ref[...] = (g_ref[...] + r_ref[...]).astype(BF16)

    grid_spec = pltpu.PrefetchScalarGridSpec(num_scalar_prefetch=1, grid=grid, in_specs=[g_spec, h_spec], out_specs=h_spec)
    return pl.pallas_call(body, name=f"addhalf_{name}_{l0}", grid_spec=grid_spec, out_shape=_sds(_half_shape(it), BF16),
                          compiler_params=_cparams(("parallel", "parallel")))(cvec, g, ra)


def _scatter_copies(items, refs, send_sems, recv_sems, what):
    N = len(items)
    x, y, c, chips = _place()
    out = []
    for t, it in enumerate(items):
        name, kind, Lg, A, B, n, l0 = it
        for j, (px, py) in enumerate(chips):
            k = 2 * px + py
            src = refs[t].at[:, k] if kind == "row" else refs[t].at[:, :, pl.ds(pl.multiple_of(k * n, LANES), n)]
            cp = _rcopy(send_sems, recv_sems, 3 * t + j, src, refs[N + t].at[j], (px, py, c))
            if what == "start":
                out.append(cp)
            else:
                out += [(cp, "send"), (cp, "recv")]
    return out


def _scatter_start(ps, items, name):
    lands = [lax.empty((3,) + _piece_shape(it), BF16) for it in items]
    return _split_start(list(ps) + lands, 3 * len(items), functools.partial(_scatter_copies, items, what="start"), name)


def _scatter_wait(thru, send_sems, recv_sems, after, items, name):
    return _split_wait(thru, send_sems, recv_sems, after, functools.partial(_scatter_copies, items, what="wait"), name)


def _sum_own(p, rc, it, mevec, buf):
    name, kind, Lg, A, B, n, l0 = it
    As, Bs = _shard_dims(kind, A, B, n)
    L = [s[2] for s in TENSORS if s[0] == name][0]
    hb = (As // 2, Bs)
    has_buf = buf is not None

    def body(*refs):
        p_ref, r0, r1, r2 = refs[1:5]
        o_ref = refs[5 + has_buf]
        o_ref[...] = ((p_ref[...].astype(F32) + r0[...].astype(F32)) + r1[...].astype(F32)) + r2[...].astype(F32)

    if kind == "row":
        p_spec = pl.BlockSpec((None, None) + hb, lambda l, m: (l, m[0], 0, 0))
    else:
        p_spec = pl.BlockSpec((None,) + hb, lambda l, m: (l, 0, m[0]))
    r_specs = [pl.BlockSpec((None, None) + hb, functools.partial(lambda l, m, j: (j, l, 0, 0), j=j)) for j in range(3)]
    in_specs = [p_spec] + r_specs + ([_ANY] if has_buf else [])
    grid_spec = pltpu.PrefetchScalarGridSpec(num_scalar_prefetch=1, grid=(Lg,), in_specs=in_specs,
                                             out_specs=pl.BlockSpec((None,) + hb, lambda l, m: (l + l0, m[1], 0)))
    args = (mevec, p, rc, rc, rc) + ((buf,) if has_buf else ())
    return pl.pallas_call(body, name=f"sumown_{name}_{l0}", grid_spec=grid_spec, out_shape=_sds((L, As, Bs)),
                          input_output_aliases={5: 0} if has_buf else {}, compiler_params=_cparams(("parallel",)))(*args)


def _join_halves(rs, items, name):
    N = len(rs)

    def body(*refs):
        outs = refs[N:2 * N]
        send_sems, recv_sems = refs[2 * N:]
        x, y, c, _ = _place()
        sib = (x, y, 1 - c)

        def half(t, h):
            name_, kind, Lg, A, B, n, l0 = items[t]
            hr = _shard_dims(kind, A, B, n)[0] // 2
            return outs[t].at[pl.ds(l0, Lg), pl.ds(pl.multiple_of(h * hr, SUBLANES), hr), :]

        cps = [_rcopy(send_sems, recv_sems, t, half(t, c), half(t, c), sib) for t in range(N)]
        for cp in cps:
            cp.start()
        for t in range(N):
            _rcopy(send_sems, recv_sems, t, half(t, 1 - c), half(t, 1 - c), sib).wait_recv()
        for cp in cps:
            cp.wait_send()

    return list(pl.pallas_call(
        body, name=name, in_specs=[_ANY] * N, out_specs=(_ANY,) * N, out_shape=tuple(_sds(r.shape, r.dtype) for r in rs),
        input_output_aliases={t: t for t in range(N)},
        scratch_shapes=[pltpu.SemaphoreType.DMA((N,)), pltpu.SemaphoreType.DMA((N,))],
        compiler_params=pltpu.CompilerParams(has_side_effects=True))(*rs))


def _allgather_small(v):
    m_per, n = v.shape

    def body(x_ref, out_ref, send_sems, recv_sems, local_sem):
        x, y, c, chips = _place()
        me, sibling = (x, y, c), (x, y, 1 - c)

        def rows(px, py, pc):
            return out_ref.at[pl.ds(pl.multiple_of((4 * px + 2 * py + pc) * m_per, SUBLANES), m_per), :]

        def copy(k, block, to, src=None):
            return pltpu.make_async_remote_copy(src_ref=rows(*block) if src is None else src, dst_ref=rows(*block),
                                                send_sem=send_sems.at[k], recv_sem=recv_sems.at[k], device_id=to, device_id_type=MESH)

        mine = pltpu.make_async_copy(x_ref, rows(*me), local_sem)
        mine.start()
        first = [copy(0, me, sibling, src=x_ref)]
        first += [copy(1 + j, me, (*chip, c), src=x_ref) for j, chip in enumerate(chips)]
        for cp in first:
            cp.start()
        passed = [copy(4 + j, (*chip, c), sibling) for j, chip in enumerate(chips)]
        for j, chip in enumerate(chips):
            copy(1 + j, (*chip, c), me).wait_recv()
            passed[j].start()
        copy(0, sibling, me).wait_recv()
        for j, chip in enumerate(chips):
            copy(4 + j, (*chip, 1 - c), me).wait_recv()
        for cp in first + passed:
            cp.wait_send()
        mine.wait()

    vm = pl.BlockSpec(memory_space=pltpu.VMEM)
    return pl.pallas_call(body, name="allgather_small", in_specs=[vm], out_specs=vm, out_shape=_sds((8 * m_per, n)),
                          scratch_shapes=[pltpu.SemaphoreType.DMA((7,)), pltpu.SemaphoreType.DMA((7,)), pltpu.SemaphoreType.DMA(())],
                          compiler_params=pltpu.CompilerParams(has_side_effects=True, vmem_limit_bytes=VMEM_LIMIT))(v)


def _sum8(v, m_per):
    def body(v_ref, o_ref):
        acc = v_ref[0:m_per, :]
        for k in range(1, 8):
            acc = acc + v_ref[k * m_per:(k + 1) * m_per, :]
        o_ref[...] = acc

    return pl.pallas_call(body, name="small_sum_devices", out_shape=_sds((m_per, v.shape[1])),
                          compiler_params=pltpu.CompilerParams(vmem_limit_bytes=VMEM_LIMIT))(v)


SMALL_SHARDED = (("e_conv_a_w", 2), ("e_conv_b_w", 2), ("o_conv_w", 2), ("f_conv_w", 2), ("ln_g", 2), ("ln_b", 2))
SMALL_REPL = ("e_conv_a_b", "e_ln_a_g", "e_ln_a_b", "e_conv_b_b", "e_dt_bias", "e_a_log", "e_d_skip", "e_norm_b_g", "f_conv_b")

WEIGHT_ORDER = ('e_w_in', 'e_conv_a_w', 'e_conv_a_b', 'e_ln_a_g', 'e_ln_a_b', 'e_conv_b_w', 'e_conv_b_b', 'e_dt_bias', 'e_a_log',
                'e_d_skip', 'e_norm_b_g', 'e_w_out', 'o_w_in', 'o_conv_w', 'o_w_out', 'f_w_up', 'f_conv_w', 'f_conv_b', 'f_w_down',
                'ple_w_proj', 'ple_w_gate', 'ln_g', 'ln_b')


def _pack_rows(parts, width, total_rows, dtype):
    flat = jnp.concatenate([p.reshape(-1).astype(dtype) for p in parts])
    flat = jnp.pad(flat, (0, total_rows * width - flat.shape[0]))
    return flat.reshape(total_rows, width)


def _unpack_rows(buf, shapes):
    flat = buf.reshape(-1)
    out, pos = [], 0
    for s in shapes:
        n = math.prod(s)
        out.append(flat[pos:pos + n].reshape(s))
        pos += n
    return out


def _small_rows(shapes):
    n = sum(math.prod(s) for s in shapes)
    return -(-n // (LANES * SUBLANES)) * SUBLANES


E_PAD = 5248
SEG_A, SEG_Z, SEG_X, SEG_DT = (0, 2 * D), (2 * D, D), (3 * D, 2 * D), (5 * D, LANES)
G_SHAPES = {"e_w_in": (2, D, E_PAD), "e_w_out": (2, 2 * D, D), "o_w_in": (2, D, 3 * D), "o_w_out": (2, D, D),
            "f_w_up": (4, D, 2 * D_FF), "f_w_down": (4, D_FF, D), "ple_w_proj": (4, PLE, D), "ple_w_gate": (4, D, D)}


def _padcols(w, width):
    return jnp.pad(w, ((0, 0), (0, width - w.shape[1])))


def _fold_rows(dw, K):
    return dw.reshape(K, SUBLANES, dw.shape[-1]).sum(1)


class GradBuffers(dict):
    def __init__(self):
        super().__init__()
        self.where = {}
        for gi, layers in enumerate(G_GROUPS):
            for name, kind, Lg, A, B, n, l0 in _group_items(layers):
                for k in range(Lg):
                    self.where[(name, l0 + k)] = (gi, k, Lg)
        self.current = {}

    def into(self, name, layer, r0=0, c0=0):
        gi, k, Lg = self.where[(name, layer)]
        self.current[name] = (name, gi)
        return (self.get((name, gi)), (Lg,) + G_SHAPES[name][1:], (k,), r0, c0)

    def __setitem__(self, name, value):
        super().__setitem__(self.current[name], value)


def _local_step(x, p, target, W, comm=None):
    T = x.shape[0]
    xb = x
    saved = []
    xc_f = x
    for i in range(DEPTH):
        j = i // 2
        L = {}
        L["x"], L["xb"] = xc_f, xb
        tok = comm.part_starts(i, MIX, xb) if comm is not None else None
        if i % 2 == 0:
            def w_in(seg, c0=0, cols=None, j=j):
                return V(W["e_w_in"], (j,), c0=seg[0] + c0, cols=seg[1] if cols is None else cols)

            ua = _mm(xb, w_in(SEG_A), "nn", f"l{i}_in_a", BF16, after=tok)
            z = _mm(xb, w_in(SEG_Z), "nn", f"l{i}_in_z")
            xu = _mm(xb, w_in(SEG_X), "nn", f"l{i}_in_xbc", BF16)
            udt = _mm(xb, w_in(SEG_DT), "nn", f"l{i}_in_dt")
            ac = _conv_a_fwd(ua, W["e_conv_a_w"][j], W["e_conv_a_b"][j][None], f"l{i}_conv_a")
            ya = _ln_silu_fwd(ac, W["e_ln_a_g"][j][None], W["e_ln_a_b"][j][None], f"l{i}_ln_a")
            xc = _conv_b_fwd(xu, W["e_conv_b_w"][j], W["e_conv_b_b"][j][None], f"l{i}_conv_b")
            sm = _ssd_small_inputs(udt[:, :N_HEADS], W["e_dt_bias"][j], W["e_a_log"][j])
            y, states = _ssd_fwd(xc, *sm, W["e_d_skip"][j], f"l{i}_ssd")
            yb = _gate_rms_fwd(y, z, W["e_norm_b_g"][j][None], f"l{i}_gate_rms")
            out_pairs = [(ya, V(W["e_w_out"], (j,), rows=D)), (yb, V(W["e_w_out"], (j,), r0=D))]
            L.update(ua=ua, z=z, xu=xu, udt=udt, ac=ac, ya=ya, xc=xc, sm=sm, y=y, states=states, yb=yb, w_in=w_in)
        else:
            uo = _mm(xb, V(W["o_w_in"], (j,)), "nn", f"l{i}_in", BF16, after=tok)
            sc = _conv_c_fwd(uo, W["o_conv_w"][j], f"l{i}_conv_c")
            out_pairs = [(sc, V(W["o_w_out"], (j,)))]
            L.update(uo=uo, sc=sc)
        h1, x1, x1b = _mm_sum(out_pairs, "nn", f"l{i}_out", ln_fwd=(xc_f, None, W["ln_g"][i, 0][None], W["ln_b"][i, 0][None]))
        tok = comm.part_starts(i, FFN, x1b) if comm is not None else None
        up = _mm(x1b, V(W["f_w_up"], (i,)), "nn", f"l{i}_ffn_up", BF16, after=tok)
        act = _conv_f_fwd(up, W["f_conv_w"][i], W["f_conv_b"][i][None], f"l{i}_conv_f")
        pv = V(p, (i, 0))
        pp = _mm(pv, V(W["ple_w_proj"], (i,)), "nn", f"l{i}_ple_proj")
        gl = _mm(x1b, V(W["ple_w_gate"], (i,)), "nn", f"l{i}_ple_gate")
        h2, x2, x2b = _mm_sum([(act, V(W["f_w_down"], (i,)))], "nn", f"l{i}_ffn_down",
                              ln_fwd=(x1, (pp, gl), W["ln_g"][i, 1][None], W["ln_b"][i, 1][None]))
        L.update(h1=h1, x1=x1, x1b=x1b, up=up, act=act, pv=pv, pp=pp, gl=gl, h2=h2)
        saved.append(L)
        xc_f, xb = x2, x2b

    sq, dx = _loss_head(xc_f, target, "loss_head")

    GB = GradBuffers()
    into = GB.into
    tok = None
    ln2_done = None

    G = {n: [None] * (DEPTH if n.startswith(("f_", "ln_")) else DEPTH // 2) for n in WEIGHT_ORDER if n not in G_SHAPES}
    for i in reversed(range(DEPTH)):
        j = i // 2
        L = saved[i]
        if ln2_done is None:
            ln2_done = _res_ln_bwd(dx, L["h2"], W["ln_g"][i, 1][None], (L["pp"], L["gl"]), f"l{i}_ln2_bwd")
        dh2, dh2b, dg2, db2, dpp, dgl = ln2_done
        ln2_done = None
        GB["f_w_down"] = _mm(L["act"], dh2b, "tn", f"l{i}_dw_down", dst=into("f_w_down", i))
        dact = _mm(dh2b, V(W["f_w_down"], (i,)), "nt", f"l{i}_dact", BF16, after=tok)
        du1, du2, dw1, dw2, dbf1, dbf2 = _conv_f_bwd(L["up"], W["f_conv_w"][i], W["f_conv_b"][i][None], dact, f"l{i}_conv_f_bwd")
        G["f_conv_w"][i] = jnp.concatenate([_fold_rows(dw1, CONV_F), _fold_rows(dw2, CONV_F)], axis=1)
        G["f_conv_b"][i] = jnp.concatenate([dbf1.sum(0), dbf2.sum(0)])
        GB["f_w_up"] = _mm(L["x1b"], du1, "tn", f"l{i}_dw_up1", dst=into("f_w_up", i))
        GB["f_w_up"] = _mm(L["x1b"], du2, "tn", f"l{i}_dw_up2", dst=into("f_w_up", i, c0=D_FF))
        GB["ple_w_proj"] = _mm(L["pv"], dpp, "tn", f"l{i}_dw_proj", dst=into("ple_w_proj", i))
        GB["ple_w_gate"] = _mm(L["x1b"], dgl, "tn", f"l{i}_dw_gate", dst=into("ple_w_gate", i))
        tok = comm.part_grads_done(i, FFN, GB) if comm is not None else None
        dh1, dh1b, dg1, db1 = _mm_sum(
            [(du1, V(W["f_w_up"], (i,), cols=D_FF)), (du2, V(W["f_w_up"], (i,), c0=D_FF)), (dgl, V(W["ple_w_gate"], (i,)))],
            "nt", f"l{i}_dx1", add=dh2, add_scale=ALPHA, after=tok, ln_bwd=(L["h1"], W["ln_g"][i, 0][None], None))
        G["ln_g"][i] = jnp.concatenate([dg1, dg2], axis=0)
        G["ln_b"][i] = jnp.concatenate([db1, db2], axis=0)
        if i % 2 == 0:
            GB["e_w_out"] = _mm(L["ya"], dh1b, "tn", f"l{i}_dw_out_a", dst=into("e_w_out", j))
            GB["e_w_out"] = _mm(L["yb"], dh1b, "tn", f"l{i}_dw_out_b", dst=into("e_w_out", j, r0=D))
            dya = _mm(dh1b, V(W["e_w_out"], (j,), rows=D), "nt", f"l{i}_dya")
            dyb = _mm(dh1b, V(W["e_w_out"], (j,), r0=D), "nt", f"l{i}_dyb")
            dac, dga, dba = _ln_silu_bwd(L["ac"], dya, W["e_ln_a_g"][j][None], W["e_ln_a_b"][j][None], f"l{i}_ln_a_bwd")
            G["e_ln_a_g"][j], G["e_ln_a_b"][j] = dga[0], dba[0]
            dal, dag, dwa, dbca = _conv_a_bwd(L["ua"], W["e_conv_a_w"][j], dac, f"l{i}_conv_a_bwd")
            G["e_conv_a_w"][j] = _fold_rows(dwa, CONV_A)
            G["e_conv_a_b"][j] = dbca.sum(0)
            dy, dz, dgn = _gate_rms_bwd(L["y"], L["z"], dyb, W["e_norm_b_g"][j][None], f"l{i}_gate_rms_bwd")
            G["e_norm_b_g"][j] = dgn[0]
            dxs, dbs, dcs, sq_col, cms_row, dda_col, dda_row, ddp = _ssd_bwd(L["xc"], *L["sm"], W["e_d_skip"][j], L["states"], dy,
                                                                             f"l{i}_ssd_bwd")
            draw, G["e_dt_bias"][j], G["e_a_log"][j] = _ssd_small_grads(L["udt"][:, :N_HEADS], W["e_dt_bias"][j], W["e_a_log"][j],
                                                                       sq_col, cms_row, dda_col, dda_row)
            G["e_d_skip"][j] = ddp[:, :, 0, :].sum(0).reshape(N_HEADS, HEAD_P).sum(1)
            dxu, dwb, dbcb = _conv_b_bwd(L["xu"], W["e_conv_b_w"][j], W["e_conv_b_b"][j][None], dxs, dbs, dcs, f"l{i}_conv_b_bwd")
            G["e_conv_b_w"][j] = _fold_rows(dwb, CONV_B)
            G["e_conv_b_b"][j] = dbcb.sum(0)
            dudt = _padcols(draw, LANES)
            w_in = L["w_in"]
            xb_l = L["xb"]
            for nm, dseg, c0 in (("al", dal, 0), ("ag", dag, D), ("z", dz, SEG_Z[0]), ("xbc", dxu, SEG_X[0]), ("dt", dudt, SEG_DT[0])):
                GB["e_w_in"] = _mm(xb_l, dseg, "tn", f"l{i}_dw_in_{nm}", dst=into("e_w_in", j, c0=c0))
            dx = _mm_sum([(dal, w_in(SEG_A, cols=D)), (dag, w_in(SEG_A, c0=D, cols=D)), (dz, w_in(SEG_Z)),
                          (V(dxu, cols=D), w_in(SEG_X, cols=D)), (V(dxu, c0=D), w_in(SEG_X, c0=D, cols=D)), (dudt, w_in(SEG_DT))],
                         "nt", f"l{i}_dx", add=dh1, add_scale=ALPHA)
        else:
            GB["o_w_out"] = _mm(L["sc"], dh1b, "tn", f"l{i}_dw_out", dst=into("o_w_out", j))
            dsc = _mm(dh1b, V(W["o_w_out"], (j,)), "nt", f"l{i}_dsc")
            dbg, dcg, dv, dwc = _conv_c_bwd(L["uo"], W["o_conv_w"][j], dsc, f"l{i}_conv_c_bwd")
            G["o_conv_w"][j] = _fold_rows(dwc, CONV_C)
            xb_l = L["xb"]
            for nm, dseg, c0 in (("bg", dbg, 0), ("cg", dcg, D), ("v", dv, 2 * D)):
                GB["o_w_in"] = _mm(xb_l, dseg, "tn", f"l{i}_dw_in_{nm}", dst=into("o_w_in", j, c0=c0))
            below = saved[i - 1]
            ln2_done = _mm_sum([(dseg, V(W["o_w_in"], (j,), c0=c0, cols=D)) for dseg, c0 in ((dbg, 0), (dcg, D), (dv, 2 * D))],
                               "nt", f"l{i}_dx", add=dh1, add_scale=ALPHA,
                               ln_bwd=(below["h2"], W["ln_g"][i - 1, 1][None], (below["pp"], below["gl"])))
        tok = comm.part_grads_done(i, MIX, GB) if comm is not None else None
    grads = {n: jnp.stack(v) for n, v in G.items()}
    return sq, dx, GB, grads


def _ssd_small_inputs(raw, dt_bias, a_log):
    T = raw.shape[0]
    a = -jnp.exp(a_log)
    rg = raw.reshape(T, N_GROUPS, 4)
    raw_col = jnp.pad(jnp.transpose(rg, (1, 0, 2)), ((0, 0), (0, 0), (0, LANES - 4)))
    raw_row = jnp.pad(jnp.transpose(rg, (1, 2, 0)), ((0, 0), (0, SUBLANES - 4), (0, 0)))

    def colv(v):
        return jnp.pad(v.reshape(N_GROUPS, 1, 4), ((0, 0), (0, 0), (0, LANES - 4)))

    def rowv(v):
        return jnp.pad(v.reshape(N_GROUPS, 4, 1), ((0, 0), (0, SUBLANES - 4), (0, 0)))

    return raw_col, raw_row, colv(dt_bias), rowv(dt_bias), colv(a), rowv(a)


def _ssd_small_grads(raw, dt_bias, a_log, sq_col, cms_row, dda_col, dda_row):
    T = raw.shape[0]

    def join(col, row):
        c = jnp.transpose(col[:, :, :4], (1, 0, 2)).reshape(T, N_HEADS)
        r = jnp.transpose(row[:, :4, :], (2, 0, 1)).reshape(T, N_HEADS)
        return c + r

    a = -jnp.exp(a_log)
    pre = raw + dt_bias
    dt = jax.nn.softplus(pre)
    dda = join(dda_col, dda_row)
    ddt = join(sq_col, cms_row) + a * dda
    draw = ddt * jax.nn.sigmoid(pre)
    da = jnp.sum(dt * dda, axis=0)
    return draw, jnp.sum(draw, axis=0), da * a


def kernel(x, p, e_w_in, e_conv_a_w, e_conv_a_b, e_ln_a_g, e_ln_a_b, e_conv_b_w, e_conv_b_b, e_dt_bias, e_a_log, e_d_skip, e_norm_b_g, e_w_out, o_w_in, o_conv_w, o_w_out, f_w_up, f_conv_w, f_conv_b, f_w_down, ple_w_proj, ple_w_gate, ln_g, ln_b, loss_target, m_e_w_in, m_e_conv_a_w, m_e_conv_a_b, m_e_ln_a_g, m_e_ln_a_b, m_e_conv_b_w, m_e_conv_b_b, m_e_dt_bias, m_e_a_log, m_e_d_skip, m_e_norm_b_g, m_e_w_out, m_o_w_in, m_o_conv_w, m_o_w_out, m_f_w_up, m_f_conv_w, m_f_conv_b, m_f_w_down, m_ple_w_proj, m_ple_w_gate, m_ln_g, m_ln_b, v_e_w_in, v_e_conv_a_w, v_e_conv_a_b, v_e_ln_a_g, v_e_ln_a_b, v_e_conv_b_w, v_e_conv_b_b, v_e_dt_bias, v_e_a_log, v_e_d_skip, v_e_norm_b_g, v_e_w_out, v_o_w_in, v_o_conv_w, v_o_w_out, v_f_w_up, v_f_conv_w, v_f_conv_b, v_f_w_down, v_ple_w_proj, v_ple_w_gate, v_ln_g, v_ln_b):
    args = dict(locals())
    w_shard = {n: args[n] for n in WEIGHT_ORDER}
    m_shard = {n: args["m_" + n] for n in WEIGHT_ORDER}
    v_shard = {n: args["v_" + n] for n in WEIGHT_ORDER}
    xi, yi, ci = lax.axis_index("x"), lax.axis_index("y"), lax.axis_index("c")
    chip = 2 * xi + yi

    mevec = jnp.stack([chip, ci]).astype(jnp.int32)
    small_shapes = [w_shard[n].shape for n, _ in SMALL_SHARDED]
    sr = _small_rows(small_shapes)
    ws = _pack_rows([w_shard[n] for n, _ in SMALL_SHARDED], LANES, sr, F32)
    W = {n: w_shard[n] for n in SMALL_REPL}
    W.update({s[0]: Layers(s[2]) for s in TENSORS})
    w_items = [_group_items(layers) for layers in W_GROUPS]
    g_items = [_group_items(layers) for layers in G_GROUPS]

    def install(items, fulls):
        for it, f in zip(items, fulls):
            if it[0] == "e_w_in":
                f = jnp.transpose(f.reshape(it[2], 4, D, E_IN // 4), (0, 2, 1, 3)).reshape(it[2], D, E_IN)
                f = jnp.pad(f, ((0, 0), (0, 0), (0, E_PAD - E_IN)))
            W[it[0]].put(f, it[6])

    casts = [[_cast_into(w_shard[it[0]], it, mevec[:1]) for it in items] for items in w_items]
    fulls, ssem, rsem, _ = _gather_start(casts[0], w_items[0], "gather_start_0")
    fulls = _gather_wait(fulls, ssem, rsem, [c for grp in casts[1:] for c in grp], w_items[0], "gather_wait_0")
    fulls, WS = _gather_fwd(fulls, w_items[0], "gather_fwd_0", ws)
    install(w_items[0], fulls)
    parts_s = [_unpack_rows(WS[k], small_shapes) for k in range(4)]
    for idx, (n, ax) in enumerate(SMALL_SHARDED):
        W[n] = jnp.concatenate([parts_s[k][idx] for k in range(4)], axis=ax)

    class Comm:
        sent = {}
        started = {}
        tail = fulls[0]

        def start_next(self, gi):
            if gi >= len(w_items):
                return None
            self.started[gi] = _gather_start(casts[gi], w_items[gi], f"gather_start_{gi}", self.tail)
            return self.started[gi][3]

        def part_starts(self, layer, part, after):
            if (layer, part) == W_GROUPS[0][0]:
                return self.start_next(1)
            for gi in range(1, len(W_GROUPS)):
                if W_GROUPS[gi][0] == (layer, part):
                    fulls, ssem, rsem, _ = self.started[gi]
                    fulls = _gather_wait(fulls, ssem, rsem, after, w_items[gi], f"gather_wait_{gi}")
                    fulls, _ = _gather_fwd(fulls, w_items[gi], f"gather_fwd_{gi}")
                    install(w_items[gi], fulls)
                    self.tail = fulls[0]
                    return self.start_next(gi + 1)
            return None

        def part_grads_done(self, layer, part, GB):
            tok = None
            for gi, members in enumerate(G_GROUPS):
                if members[-1] == (layer, part):
                    items = g_items[gi]
                    gs = []
                    for it in items:
                        g = GB[(it[0], gi)]
                        if it[0] == "e_w_in":
                            g = jnp.transpose(g[:, :, :E_IN].reshape(it[2], D, 4, E_IN // 4), (0, 2, 1, 3)).reshape(it[2], 4 * D, E_IN // 4)
                        gs.append(g)
                    ras = _swap_grads(gs, items, f"swap_grads_{gi}")
                    ps = [_add_half(g, ra, it, mevec[1:]) for g, ra, it in zip(gs, ras, items)]
                    thru, ssem, rsem, tok = _scatter_start(ps, items, f"scatter_start_{gi}")
                    self.sent[gi] = (thru, ssem, rsem, tok)
            return tok

    comm = Comm()

    sq, dx, GB, G = _local_step(x[0], p, loss_target[0], W, comm)
    loss = lax.psum(0.5 * sq[0, 0] / D, ("x", "y", "c"))
    grad_x = dx[None]

    def shard_of(g, ax, k):
        n = g.shape[ax] // 4
        return lax.slice_in_dim(g, k * n, (k + 1) * n, axis=ax)

    reduced, updated = {}, {}
    after = comm.sent[len(g_items) - 1][3]
    for gi, items in enumerate(g_items):
        thru, ssem, rsem, _ = comm.sent[gi]
        thru = _scatter_wait(thru, ssem, rsem, after, items, f"scatter_wait_{gi}")
        ps, rcs = thru[:len(items)], thru[len(items):]
        rs = [_sum_own(pt, rc, it, mevec, reduced.get(it[0])) for pt, rc, it in zip(ps, rcs, items)]
        rs = _join_halves(rs, items, f"join_halves_{gi}")
        reduced.update({it[0]: r for it, r in zip(items, rs)})
        for it in items:
            n = it[0]
            if n != "e_w_in":
                updated[n] = _adamw_layers(w_shard[n], reduced[n], m_shard[n], v_shard[n], it[6], it[2], updated.get(n),
                                           f"adamw_{n}_{it[6]}")
        after = updated[items[-1][0]][0]
    *updated["e_w_in"], reduced["e_w_in"] = _adamw_minor_rows(w_shard["e_w_in"], reduced["e_w_in"], m_shard["e_w_in"],
                                                              v_shard["e_w_in"], "adamw_e_w_in")

    small_all = ([shard_of(G[n], ax, k) for k in range(4) for n, ax in SMALL_SHARDED] + [G[n] for n in SMALL_REPL])
    small_all_shapes = [t.shape for t in small_all]
    mr = _small_rows(small_all_shapes)
    sg = _sum8(_allgather_small(_pack_rows(small_all, LANES, mr, F32)), mr)
    sparts = _unpack_rows(sg, small_all_shapes)
    ns = len(SMALL_SHARDED)
    gsmall = {}
    for idx, (n, ax) in enumerate(SMALL_SHARDED):
        stacked = jnp.stack([sparts[k * ns + idx] for k in range(4)])
        gsmall[n] = lax.dynamic_index_in_dim(stacked, chip, axis=0, keepdims=False)
    for idx, n in enumerate(SMALL_REPL):
        gsmall[n] = sparts[4 * ns + idx]

    grads, deltas, new_m, new_v = [], [], [], []
    for n in WEIGHT_ORDER:
        if n in reduced:
            g, (d, mn, vn) = reduced[n], updated[n]
        else:
            g = gsmall[n]
            d, mn, vn = _adamw(w_shard[n], g, m_shard[n], v_shard[n], f"adamw_{n}")
        grads.append(g)
        deltas.append(d)
        new_m.append(mn)
        new_v.append(vn)
    return (loss, grad_x, *grads, *deltas, *new_m, *new_v)
```

```python
import functools
import math

import jax
import jax.numpy as jnp
from jax import lax
from jax.experimental import pallas as pl
from jax.experimental.pallas import tpu as pltpu

F32 = jnp.float32
BF16 = jnp.bfloat16
MESH = pl.DeviceIdType.MESH

DEPTH = 4
ALPHA = (2.0 * DEPTH) ** 0.25
LN_EPS = 1e-5
D = 1024
HEAD_P = 64
N_STATE = 128
N_HEADS = 16
N_GROUPS = 4
CONV_A, CONV_B, CONV_C, CONV_F = 31, 4, 3, 3
D_FF = 2816
PLE = 256
E_IN = 5136

ADAM_LR, ADAM_B1, ADAM_B2, ADAM_EPS, ADAM_WD, ADAM_STEP = 0.001, 0.9, 0.999, 1e-08, 0.01, 10

LANES = 128
SUBLANES = 8
VMEM_LIMIT = 56 * 1024 * 1024
SSD_Q = 128
CONV_R = 128
CONV_PAD = 32
ROW_T = 256


def _cparams(sem=None):
    return pltpu.CompilerParams(dimension_semantics=sem, vmem_limit_bytes=VMEM_LIMIT)


def _sig(v):
    return jax.nn.sigmoid(v)


_DIMS = {"nn": (((1,), (0,)), ((), ())), "nt": (((1,), (1,)), ((), ())), "tn": (((0,), (0,)), ((), ()))}


class Layers:
    def __init__(self, n_layers):
        self.where = [None] * n_layers

    def put(self, arr, l0):
        for k in range(arr.shape[0]):
            self.where[l0 + k] = (arr, k)


class V:
    def __init__(self, arr, lead=(), r0=0, c0=0, rows=None, cols=None):
        if isinstance(arr, Layers):
            arr, k = arr.where[lead[0]]
            lead = (k,) + tuple(lead[1:])
        self.arr, self.lead, self.r0, self.c0 = arr, tuple(lead), r0, c0
        R, C = arr.shape[-2:]
        self.rows = R - r0 if rows is None else rows
        self.cols = C - c0 if cols is None else cols

    def spec(self, br, bc, fn):
        assert self.r0 % br == 0 and self.c0 % bc == 0, (self.r0, self.c0, br, bc)
        ro, co, lead = self.r0 // br, self.c0 // bc, self.lead

        def index(i, j, k):
            r, c = fn(i, j, k)
            return lead + (r + ro, c + co)

        return pl.BlockSpec((None,) * len(lead) + (br, bc), index)


def _v(t):
    return t if isinstance(t, V) else V(t)


def _tile(n, offs, cands):
    for c in cands:
        if n % c == 0 and all(o % c == 0 for o in offs):
            return c
    raise ValueError((n, offs))


_TILES = (1024, 1408, 512, 256, 128)


def _mm(a, b, mode, name, out_dtype=F32, add=None, add_scale=1.0, dst=None, after=None):
    a, b = _v(a), _v(b)
    add = _v(add) if add is not None else None
    if mode == "nn":
        M, K, K2, N = a.rows, a.cols, b.rows, b.cols
        am, ak, bk, bn = a.r0, a.c0, b.r0, b.c0
    elif mode == "nt":
        M, K, N, K2 = a.rows, a.cols, b.rows, b.cols
        am, ak, bn, bk = a.r0, a.c0, b.r0, b.c0
    else:
        K, M, K2, N = a.rows, a.cols, b.rows, b.cols
        ak, am, bk, bn = a.r0, a.c0, b.r0, b.c0
    assert K == K2, (name, mode, M, K, K2, N)
    if dst is None:
        buf, full_shape, o_lead, o_r0, o_c0 = None, (M, N), (), 0, 0
    else:
        buf, full_shape, o_lead, o_r0, o_c0 = dst
    tm = _tile(M, [am, o_r0] + ([add.r0] if add else []), _TILES)
    tn = _tile(N, [bn, o_c0] + ([add.c0] if add else []), _TILES)
    narrow = a.arr.dtype.itemsize == 2 and b.arr.dtype.itemsize == 2
    tk = _tile(K, [ak, bk], ((2048,) if narrow else ()) + _TILES)
    nk = K // tk
    has_add, has_buf, has_after = add is not None, buf is not None, after is not None

    def body(*refs):
        a_ref, b_ref = refs[0], refs[1]
        add_ref = refs[2] if has_add else None
        o_ref = refs[2 + has_add + has_buf + has_after]

        def finish(r):
            if has_add:
                r = r + add_scale * add_ref[...].astype(F32)
            o_ref[...] = r.astype(o_ref.dtype)

        part = lax.dot_general(a_ref[...].astype(BF16), b_ref[...].astype(BF16), _DIMS[mode], preferred_element_type=F32)
        if nk == 1:
            finish(part)
        else:
            acc_ref = refs[-1]
            k = pl.program_id(2)

            @pl.when(k == 0)
            def _():
                acc_ref[...] = part

            @pl.when(jnp.logical_and(k > 0, k < nk - 1))
            def _():
                acc_ref[...] += part

            @pl.when(k == nk - 1)
            def _():
                finish(acc_ref[...] + part)

    if mode == "tn":
        a_spec = a.spec(tk, tm, lambda i, j, k: (k, i))
    else:
        a_spec = a.spec(tm, tk, lambda i, j, k: (i, k))
    if mode == "nt":
        b_spec = b.spec(tn, tk, lambda i, j, k: (j, k))
    else:
        b_spec = b.spec(tk, tn, lambda i, j, k: (k, j))
    in_specs, args = [a_spec, b_spec], [a.arr, b.arr]
    if has_add:
        in_specs.append(add.spec(tm, tn, lambda i, j, k: (i, j)))
        args.append(add.arr)
    aliases = {}
    if has_buf:
        aliases = {len(args): 0}
        in_specs.append(pl.BlockSpec(memory_space=pl.ANY))
        args.append(buf)
        out_dtype = buf.dtype
    if has_after:
        in_specs.append(pl.BlockSpec(memory_space=pl.ANY))
        args.append(after)
    o_view = V(jax.ShapeDtypeStruct(full_shape, out_dtype), o_lead, o_r0, o_c0, M, N)
    return pl.pallas_call(
        body, name=name, grid=(M // tm, N // tn, nk), in_specs=in_specs, out_specs=o_view.spec(tm, tn, lambda i, j, k: (i, j)),
        out_shape=jax.ShapeDtypeStruct(full_shape, out_dtype), input_output_aliases=aliases,
        scratch_shapes=[pltpu.VMEM((tm, tn), F32)] if nk > 1 else [],
        compiler_params=_cparams(("parallel", "parallel", "arbitrary")))(*args)


def _ln_stats(h):
    mu = jnp.mean(h, axis=-1, keepdims=True)
    hc = h - mu
    var = jnp.mean(hc * hc, axis=-1, keepdims=True)
    rstd = lax.rsqrt(var + LN_EPS)
    return hc * rstd, rstd


def _ln_bwd_math(dyv, h, g):
    xhat, rstd = _ln_stats(h)
    dxh = dyv * g
    dh = rstd * (dxh - jnp.mean(dxh, axis=-1, keepdims=True) - xhat * jnp.mean(dxh * xhat, axis=-1, keepdims=True))
    return dh, jnp.sum(dyv * xhat, axis=0, keepdims=True), jnp.sum(dyv, axis=0, keepdims=True)


def _mm_sum(pairs, mode, name, out_dtype=F32, add=None, add_scale=1.0, after=None, ln_fwd=None, ln_bwd=None):
    pairs = [(_v(a), _v(b)) for a, b in pairs]
    add = _v(add) if add is not None else None
    M = pairs[0][0].rows
    N = pairs[0][1].cols if mode == "nn" else pairs[0][1].rows
    b_offs = [(b.c0 if mode == "nn" else b.r0) for _, b in pairs]
    fused = ln_fwd is not None or ln_bwd is not None
    tm = _tile(M, [a.r0 for a, _ in pairs] + ([add.r0] if add else []), (256, 128) if fused else (512, 256, 128))
    tn = N if fused else _tile(N, b_offs + ([add.c0] if add else []), (512, 256, 128))
    assert not fused or (N == D and all(o == 0 for o in b_offs))
    n_p, has_add, has_after = len(pairs), add is not None, after is not None
    ple = (ln_fwd[1] if ln_fwd is not None else ln_bwd[2]) if fused else None
    has_ple = ple is not None

    def body(*refs):
        acc = None
        for i in range(n_p):
            part = lax.dot_general(refs[2 * i][...].astype(BF16), refs[2 * i + 1][...].astype(BF16), _DIMS[mode],
                                   preferred_element_type=F32)
            acc = part if acc is None else acc + part
        pos = 2 * n_p
        if has_add:
            acc = acc + add_scale * refs[pos][...].astype(F32)
            pos += 1
        if ln_fwd is not None:
            x_ref = refs[pos]
            pp_ref, gl_ref = (refs[pos + 1], refs[pos + 2]) if has_ple else (None, None)
            pos += 1 + 2 * has_ple
            g_ref, b_ref = refs[pos], refs[pos + 1]
            h_ref, y_ref, yb_ref = refs[pos + 2 + has_after:]
            h = ALPHA * x_ref[...] + acc
            if has_ple:
                h = h + pp_ref[...] * _sig(gl_ref[...])
            xhat, _ = _ln_stats(h)
            y = xhat * g_ref[...] + b_ref[...]
            h_ref[...] = h
            y_ref[...] = y
            yb_ref[...] = y.astype(BF16)
        elif ln_bwd is not None:
            h_ref, g_ref = refs[pos], refs[pos + 1]
            pp_ref, gl_ref = (refs[pos + 2], refs[pos + 3]) if has_ple else (None, None)
            outs = refs[pos + 2 + 2 * has_ple + has_after:]
            dh_ref, dhb_ref, dg_ref, db_ref = outs[:4]

            @pl.when(pl.program_id(0) == 0)
            def _():
                dg_ref[...] = jnp.zeros_like(dg_ref)
                db_ref[...] = jnp.zeros_like(db_ref)

            dh, dg, db = _ln_bwd_math(acc, h_ref[...], g_ref[...])
            dg_ref[...] += dg
            db_ref[...] += db
            dh_ref[...] = dh
            dhb_ref[...] = dh.astype(BF16)
            if has_ple:
                s = _sig(gl_ref[...])
                outs[4][...] = (dh * s).astype(BF16)
                outs[5][...] = (dh * pp_ref[...] * s * (1.0 - s)).astype(BF16)
        else:
            o_ref = refs[pos + has_after]
            o_ref[...] = acc.astype(o_ref.dtype)

    in_specs, args = [], []
    for a, b in pairs:
        K = a.cols
        assert K == (b.rows if mode == "nn" else b.cols), (name, K)
        in_specs.append(a.spec(tm, K, lambda i, j, k: (i, 0)))
        in_specs.append(b.spec(K, tn, lambda i, j, k: (0, j)) if mode == "nn" else b.spec(tn, K, lambda i, j, k: (j, 0)))
        args += [a.arr, b.arr]
    if has_add:
        in_specs.append(add.spec(tm, tn, lambda i, j, k: (i, j)))
        args.append(add.arr)
    row = pl.BlockSpec((tm, tn), lambda i, j, k: (i, j))
    vec = pl.BlockSpec((1, tn), lambda i, j, k: (0, 0))
    if ln_fwd is not None:
        x, _, g, b = ln_fwd
        extra = [x] + (list(ple) if has_ple else []) + [g, b]
        in_specs += [row] * (1 + 2 * has_ple) + [vec, vec]
        args += extra
        out_specs = (row, row, row)
        out_shape = (_sds((M, N)), _sds((M, N)), _sds((M, N), BF16))
    elif ln_bwd is not None:
        h, g, _ = ln_bwd
        in_specs += [row, vec] + [row] * (2 * has_ple)
        args += [h, g] + (list(ple) if has_ple else [])
        out_specs = (row, row, vec, vec) + ((row, row) if has_ple else ())
        out_shape = (_sds((M, N)), _sds((M, N), BF16), _sds((1, N)), _sds((1, N))) + ((_sds((M, N), BF16),) * 2 if has_ple else ())
    else:
        out_specs, out_shape = row, jax.ShapeDtypeStruct((M, N), out_dtype)
    if has_after:
        in_specs.append(pl.BlockSpec(memory_space=pl.ANY))
        args.append(after)
    return pl.pallas_call(
        body, name=name, grid=(M // tm, N // tn, 1), in_specs=in_specs, out_specs=out_specs, out_shape=out_shape,
        compiler_params=_cparams(("arbitrary",) * 3 if ln_bwd is not None else ("parallel", "parallel", "arbitrary")))(*args)


def _rows(T, width=D):
    return pl.BlockSpec((ROW_T, width), lambda i: (i, 0))


def _vec(width=D):
    return pl.BlockSpec((1, width), lambda i: (0, 0))


def _res_ln_fwd(x, adds, ple, g, b, name):
    T = x.shape[0]
    n_add = len(adds)
    has_ple = ple is not None

    def body(*refs):
        x_ref = refs[0]
        add_refs = refs[1:1 + n_add]
        pos = 1 + n_add
        if has_ple:
            pp_ref, gl_ref = refs[pos], refs[pos + 1]
            pos += 2
        g_ref, b_ref, h_ref, y_ref, yb_ref = refs[pos:pos + 5]
        h = ALPHA * x_ref[...]
        for r in add_refs:
            h = h + r[...]
        if has_ple:
            h = h + pp_ref[...] * _sig(gl_ref[...])
        xhat, _ = _ln_stats(h)
        y = xhat * g_ref[...] + b_ref[...]
        h_ref[...] = h
        y_ref[...] = y
        yb_ref[...] = y.astype(BF16)

    n_in = 1 + n_add + (2 if has_ple else 0)
    args = (x,) + tuple(adds) + (tuple(ple) if has_ple else ()) + (g, b)
    return pl.pallas_call(
        body, name=name, grid=(T // ROW_T,), in_specs=[_rows(T)] * n_in + [_vec(), _vec()],
        out_specs=(_rows(T), _rows(T), _rows(T)),
        out_shape=(jax.ShapeDtypeStruct((T, D), F32), jax.ShapeDtypeStruct((T, D), F32), jax.ShapeDtypeStruct((T, D), BF16)),
        compiler_params=_cparams(("parallel",)))(*args)


def _res_ln_bwd(dy, h, g, ple, name):
    T = dy.shape[0]
    has_ple = ple is not None

    def body(*refs):
        if has_ple:
            dy_ref, h_ref, g_ref, pp_ref, gl_ref, dh_ref, dhb_ref, dg_ref, db_ref, dpp_ref, dgl_ref = refs
        else:
            dy_ref, h_ref, g_ref, dh_ref, dhb_ref, dg_ref, db_ref = refs
        i = pl.program_id(0)

        @pl.when(i == 0)
        def _():
            dg_ref[...] = jnp.zeros_like(dg_ref)
            db_ref[...] = jnp.zeros_like(db_ref)

        dyv = dy_ref[...]
        xhat, rstd = _ln_stats(h_ref[...])
        dg_ref[...] += jnp.sum(dyv * xhat, axis=0, keepdims=True)
        db_ref[...] += jnp.sum(dyv, axis=0, keepdims=True)
        dxh = dyv * g_ref[...]
        dh = rstd * (dxh - jnp.mean(dxh, axis=-1, keepdims=True) - xhat * jnp.mean(dxh * xhat, axis=-1, keepdims=True))
        dh_ref[...] = dh
        dhb_ref[...] = dh.astype(BF16)
        if has_ple:
            s = _sig(gl_ref[...])
            dpp_ref[...] = (dh * s).astype(BF16)
            dgl_ref[...] = (dh * pp_ref[...] * s * (1.0 - s)).astype(BF16)

    args = (dy, h, g) + (tuple(ple) if has_ple else ())
    in_specs = [_rows(T), _rows(T), _vec()] + ([_rows(T), _rows(T)] if has_ple else [])
    out_specs = [_rows(T), _rows(T), _vec(), _vec()] + ([_rows(T), _rows(T)] if has_ple else [])
    out_shape = [jax.ShapeDtypeStruct((T, D), F32), jax.ShapeDtypeStruct((T, D), BF16),
                 jax.ShapeDtypeStruct((1, D), F32), jax.ShapeDtypeStruct((1, D), F32)]
    if has_ple:
        out_shape += [jax.ShapeDtypeStruct((T, D), BF16), jax.ShapeDtypeStruct((T, D), BF16)]
    return pl.pallas_call(
        body, name=name, grid=(T // ROW_T,), in_specs=in_specs, out_specs=tuple(out_specs), out_shape=tuple(out_shape),
        compiler_params=_cparams(("arbitrary",)))(*args)


def _ln_silu_fwd(ac, g, b, name):
    T = ac.shape[0]

    def body(a_ref, g_ref, b_ref, o_ref):
        xhat, _ = _ln_stats(a_ref[...])
        ln = xhat * g_ref[...] + b_ref[...]
        o_ref[...] = (ln * _sig(ln)).astype(BF16)

    return pl.pallas_call(
        body, name=name, grid=(T // ROW_T,), in_specs=[_rows(T), _vec(), _vec()], out_specs=_rows(T),
        out_shape=jax.ShapeDtypeStruct((T, D), BF16), compiler_params=_cparams(("parallel",)))(ac, g, b)


def _ln_silu_bwd(ac, dya, g, b, name):
    T = ac.shape[0]

    def body(a_ref, d_ref, g_ref, b_ref, da_ref, dg_ref, db_ref):
        i = pl.program_id(0)

        @pl.when(i == 0)
        def _():
            dg_ref[...] = jnp.zeros_like(dg_ref)
            db_ref[...] = jnp.zeros_like(db_ref)

        xhat, rstd = _ln_stats(a_ref[...])
        ln = xhat * g_ref[...] + b_ref[...]
        s = _sig(ln)
        dln = d_ref[...] * s * (1.0 + ln * (1.0 - s))
        dg_ref[...] += jnp.sum(dln * xhat, axis=0, keepdims=True)
        db_ref[...] += jnp.sum(dln, axis=0, keepdims=True)
        dxh = dln * g_ref[...]
        da_ref[...] = rstd * (dxh - jnp.mean(dxh, axis=-1, keepdims=True)
                              - xhat * jnp.mean(dxh * xhat, axis=-1, keepdims=True))

    return pl.pallas_call(
        body, name=name, grid=(T // ROW_T,), in_specs=[_rows(T), _rows(T), _vec(), _vec()],
        out_specs=(_rows(T), _vec(), _vec()),
        out_shape=(jax.ShapeDtypeStruct((T, D), F32), jax.ShapeDtypeStruct((1, D), F32), jax.ShapeDtypeStruct((1, D), F32)),
        compiler_params=_cparams(("arbitrary",)))(ac, dya, g, b)


def _gate_rms_fwd(y, z, g, name):
    T = y.shape[0]

    def body(y_ref, z_ref, g_ref, o_ref):
        zv = z_ref[...]
        yg = y_ref[...] * (zv * _sig(zv))
        r = lax.rsqrt(jnp.mean(yg * yg, axis=-1, keepdims=True) + LN_EPS)
        o_ref[...] = (yg * r * g_ref[...]).astype(BF16)

    return pl.pallas_call(
        body, name=name, grid=(T // ROW_T,), in_specs=[_rows(T), _rows(T), _vec()], out_specs=_rows(T),
        out_shape=jax.ShapeDtypeStruct((T, D), BF16), compiler_params=_cparams(("parallel",)))(y, z, g)


def _gate_rms_bwd(y, z, dout, g, name):
    T = y.shape[0]

    def body(y_ref, z_ref, d_ref, g_ref, dy_ref, dz_ref, dg_ref):
        i = pl.program_id(0)

        @pl.when(i == 0)
        def _():
            dg_ref[...] = jnp.zeros_like(dg_ref)

        yv, zv, dv = y_ref[...], z_ref[...], d_ref[...]
        s = _sig(zv)
        sz = zv * s
        yg = yv * sz
        r = lax.rsqrt(jnp.mean(yg * yg, axis=-1, keepdims=True) + LN_EPS)
        dg_ref[...] += jnp.sum(dv * yg * r, axis=0, keepdims=True)
        dn = dv * g_ref[...]
        dyg = r * dn - yg * (r * r * r) * jnp.mean(dn * yg, axis=-1, keepdims=True)
        dy_ref[...] = dyg * sz
        dz_ref[...] = (dyg * yv * s * (1.0 + zv * (1.0 - s))).astype(BF16)

    return pl.pallas_call(
        body, name=name, grid=(T // ROW_T,), in_specs=[_rows(T), _rows(T), _rows(T), _vec()],
        out_specs=(_rows(T), _rows(T), _vec()),
        out_shape=(jax.ShapeDtypeStruct((T, D), F32), jax.ShapeDtypeStruct((T, D), BF16), jax.ShapeDtypeStruct((1, D), F32)),
        compiler_params=_cparams(("arbitrary",)))(y, z, dout, g)


def _loss_head(y, target, name):
    T = y.shape[0]

    def body(y_ref, t_ref, s_ref, d_ref):
        i = pl.program_id(0)

        @pl.when(i == 0)
        def _():
            s_ref[...] = jnp.zeros_like(s_ref)

        err = y_ref[...] - t_ref[...]
        s_ref[...] += jnp.sum(jnp.sum(err * err, axis=1, keepdims=True), axis=0, keepdims=True)
        d_ref[...] = err * (1.0 / D)

    return pl.pallas_call(
        body, name=name, grid=(T // ROW_T,), in_specs=[_rows(T), _rows(T)],
        out_specs=(pl.BlockSpec((SUBLANES, LANES), lambda i: (0, 0)), _rows(T)),
        out_shape=(jax.ShapeDtypeStruct((SUBLANES, LANES), F32), jax.ShapeDtypeStruct((T, D), F32)),
        compiler_params=_cparams(("arbitrary",)))(y, target)


def _taps_fwd(pad_ref, w_ref, K, base):
    off = CONV_PAD - (K - 1)
    acc = w_ref[0:1, :] * pad_ref[pl.ds(base + off, CONV_R), :]
    for k in range(1, K):
        acc = acc + w_ref[k:k + 1, :] * pad_ref[pl.ds(base + off + k, CONV_R), :]
    return acc


def _taps_bwd(padd_ref, w_ref, K, base):
    acc = w_ref[0:1, :] * padd_ref[pl.ds(base + (K - 1), CONV_R), :]
    for k in range(1, K):
        acc = acc + w_ref[k:k + 1, :] * padd_ref[pl.ds(base + (K - 1) - k, CONV_R), :]
    return acc


def _f32(ref, rows):
    return ref[rows, :].astype(F32)


def _fold8(v):
    return v.reshape(CONV_R // SUBLANES, SUBLANES, v.shape[-1]).sum(0)


def _wgrad_acc(dw_ref, pad_ref, d, K, base):
    off = CONV_PAD - (K - 1)
    for k in range(K):
        dw_ref[k * SUBLANES:(k + 1) * SUBLANES, :] += _fold8(d * pad_ref[pl.ds(base + off + k, CONV_R), :])


def _loop_rows(T, fn):
    def step(r, carry):
        fn(pl.multiple_of(r * CONV_R, CONV_R))
        return carry
    lax.fori_loop(0, T // CONV_R, step, 0)


def _col(T, off_blocks=0, rows=None):
    return pl.BlockSpec((T if rows is None else rows, LANES), lambda j: (0, j + off_blocks))


def _conv_call(body, name, T, n_tiles, in_specs, out_specs, out_shape, n_pad, n_padd=0):
    scratch = [pltpu.VMEM((T + CONV_PAD, LANES), F32)] * (n_pad + n_padd)
    return pl.pallas_call(body, name=name, grid=(n_tiles,), in_specs=in_specs, out_specs=out_specs, out_shape=out_shape,
                          scratch_shapes=scratch, compiler_params=_cparams(("parallel",)))


def _zero_head(ref):
    ref[0:CONV_PAD, :] = jnp.zeros((CONV_PAD, LANES), F32)


def _zero_tail(ref, T):
    ref[T:T + CONV_PAD, :] = jnp.zeros((CONV_PAD, LANES), F32)


def _sds(shape, dtype=F32):
    return jax.ShapeDtypeStruct(shape, dtype)


def _conv_a_fwd(ua, w, b, name):
    T = ua.shape[0]
    K, nt = CONV_A, D // LANES

    def body(al_ref, ag_ref, w_ref, b_ref, o_ref, pad_ref):
        _zero_head(pad_ref)

        def pre(base):
            rows = pl.ds(base, CONV_R)
            pad_ref[pl.ds(base + CONV_PAD, CONV_R), :] = _f32(al_ref, rows) * _sig(_f32(ag_ref, rows))
        _loop_rows(T, pre)

        def main(base):
            o_ref[pl.ds(base, CONV_R), :] = _taps_fwd(pad_ref, w_ref, K, base) + b_ref[...]
        _loop_rows(T, main)

    return _conv_call(body, name, T, nt, [_col(T), _col(T, nt), _col(T, rows=K), _col(T, rows=1)], _col(T),
                      _sds((T, D)), 1)(ua, ua, w, b)


def _conv_a_bwd(ua, w, dac, name):
    T = ua.shape[0]
    K, nt = CONV_A, D // LANES

    def body(al_ref, ag_ref, w_ref, d_ref, dal_ref, dag_ref, dw_ref, db_ref, pad_ref, padd_ref):
        _zero_head(pad_ref)
        _zero_tail(padd_ref, T)
        dw_ref[...] = jnp.zeros_like(dw_ref)
        db_ref[...] = jnp.zeros_like(db_ref)

        def pre(base):
            rows = pl.ds(base, CONV_R)
            pad_ref[pl.ds(base + CONV_PAD, CONV_R), :] = _f32(al_ref, rows) * _sig(_f32(ag_ref, rows))
            padd_ref[rows, :] = d_ref[rows, :]
        _loop_rows(T, pre)

        def main(base):
            rows = pl.ds(base, CONV_R)
            d = d_ref[rows, :]
            _wgrad_acc(dw_ref, pad_ref, d, K, base)
            db_ref[...] += _fold8(d)
            da = _taps_bwd(padd_ref, w_ref, K, base)
            al, s = _f32(al_ref, rows), _sig(_f32(ag_ref, rows))
            dal_ref[rows, :] = (da * s).astype(BF16)
            dag_ref[rows, :] = (da * al * s * (1.0 - s)).astype(BF16)
        _loop_rows(T, main)

    return _conv_call(body, name, T, nt, [_col(T), _col(T, nt), _col(T, rows=K), _col(T)],
                      (_col(T), _col(T), _col(T, rows=K * SUBLANES), _col(T, rows=SUBLANES)),
                      (_sds((T, D), BF16), _sds((T, D), BF16), _sds((K * SUBLANES, D)), _sds((SUBLANES, D))), 1, 1)(ua, ua, w, dac)


def _conv_b_fwd(xu, w, b, name):
    T, C = xu.shape
    K, nt = CONV_B, C // LANES

    def body(x_ref, w_ref, b_ref, o_ref, pad_ref):
        _zero_head(pad_ref)
        pad_ref[CONV_PAD:CONV_PAD + T, :] = x_ref[...].astype(F32)

        def main(base):
            hc = _taps_fwd(pad_ref, w_ref, K, base) + b_ref[...]
            o_ref[pl.ds(base, CONV_R), :] = hc * _sig(hc)
        _loop_rows(T, main)

    return _conv_call(body, name, T, nt, [_col(T), _col(T, rows=K), _col(T, rows=1)], _col(T), _sds((T, C)), 1)(xu, w, b)


def _conv_b_bwd(xu, w, b, dxs, dbs, dcs, name):
    T, C = xu.shape
    K, nt = CONV_B, C // LANES
    nx, nb = dxs.shape[1] // LANES, dbs.shape[1] // LANES

    def body(x_ref, w_ref, b_ref, d1_ref, d2_ref, d3_ref, dx_ref, dw_ref, db_ref, pad_ref, padd_ref):
        j = pl.program_id(0)
        _zero_head(pad_ref)
        _zero_tail(padd_ref, T)
        dw_ref[...] = jnp.zeros_like(dw_ref)
        db_ref[...] = jnp.zeros_like(db_ref)
        pad_ref[CONV_PAD:CONV_PAD + T, :] = x_ref[...].astype(F32)

        def pre(base):
            rows = pl.ds(base, CONV_R)
            hc = _taps_fwd(pad_ref, w_ref, K, base) + b_ref[...]
            s = _sig(hc)
            d = jnp.where(j < nx, d1_ref[rows, :], jnp.where(j < nx + nb, d2_ref[rows, :], d3_ref[rows, :]))
            padd_ref[rows, :] = d * s * (1.0 + hc * (1.0 - s))
        _loop_rows(T, pre)

        def main(base):
            d = padd_ref[pl.ds(base, CONV_R), :]
            _wgrad_acc(dw_ref, pad_ref, d, K, base)
            db_ref[...] += _fold8(d)
            dx_ref[pl.ds(base, CONV_R), :] = _taps_bwd(padd_ref, w_ref, K, base).astype(BF16)
        _loop_rows(T, main)

    def piece(lo, n):
        return pl.BlockSpec((T, LANES), lambda j: (0, jnp.clip(j - lo, 0, n - 1)))

    return _conv_call(body, name, T, nt,
                      [_col(T), _col(T, rows=K), _col(T, rows=1), piece(0, nx), piece(nx, nb), piece(nx + nb, nt - nx - nb)],
                      (_col(T), _col(T, rows=K * SUBLANES), _col(T, rows=SUBLANES)),
                      (_sds((T, C), BF16), _sds((K * SUBLANES, C)), _sds((SUBLANES, C))), 1, 1)(xu, w, b, dxs, dbs, dcs)


def _conv_c_fwd(uo, w, name):
    T = uo.shape[0]
    K, nt = CONV_C, D // LANES

    def body(bg_ref, cg_ref, v_ref, w_ref, o_ref, pad_ref):
        _zero_head(pad_ref)
        pad_ref[CONV_PAD:CONV_PAD + T, :] = cg_ref[...].astype(F32) * v_ref[...].astype(F32)

        def main(base):
            rows = pl.ds(base, CONV_R)
            o_ref[rows, :] = (_f32(bg_ref, rows) * _taps_fwd(pad_ref, w_ref, K, base)).astype(BF16)
        _loop_rows(T, main)

    return _conv_call(body, name, T, nt, [_col(T), _col(T, nt), _col(T, 2 * nt), _col(T, rows=K)], _col(T),
                      _sds((T, D), BF16), 1)(uo, uo, uo, w)


def _conv_c_bwd(uo, w, dsc, name):
    T = uo.shape[0]
    K, nt = CONV_C, D // LANES

    def body(bg_ref, cg_ref, v_ref, w_ref, d_ref, dbg_ref, dcg_ref, dv_ref, dw_ref, pad_ref, padd_ref):
        _zero_head(pad_ref)
        _zero_tail(padd_ref, T)
        dw_ref[...] = jnp.zeros_like(dw_ref)
        pad_ref[CONV_PAD:CONV_PAD + T, :] = cg_ref[...].astype(F32) * v_ref[...].astype(F32)

        def pre(base):
            rows = pl.ds(base, CONV_R)
            d = d_ref[rows, :]
            dbg_ref[rows, :] = (d * _taps_fwd(pad_ref, w_ref, K, base)).astype(BF16)
            padd_ref[rows, :] = d * _f32(bg_ref, rows)
        _loop_rows(T, pre)

        def main(base):
            rows = pl.ds(base, CONV_R)
            _wgrad_acc(dw_ref, pad_ref, padd_ref[rows, :], K, base)
            dq = _taps_bwd(padd_ref, w_ref, K, base)
            dcg_ref[rows, :] = (dq * _f32(v_ref, rows)).astype(BF16)
            dv_ref[rows, :] = (dq * _f32(cg_ref, rows)).astype(BF16)
        _loop_rows(T, main)

    return _conv_call(body, name, T, nt, [_col(T), _col(T, nt), _col(T, 2 * nt), _col(T, rows=K), _col(T)],
                      (_col(T), _col(T), _col(T), _col(T, rows=K * SUBLANES)),
                      (_sds((T, D), BF16), _sds((T, D), BF16), _sds((T, D), BF16), _sds((K * SUBLANES, D))), 1, 1)(uo, uo, uo, w, dsc)


def _conv_f_fwd(up, w, b, name):
    T = up.shape[0]
    K, nt = CONV_F, D_FF // LANES

    def body(u1_ref, u2_ref, w1_ref, w2_ref, b1_ref, b2_ref, o_ref, pad1_ref, pad2_ref):
        _zero_head(pad1_ref)
        _zero_head(pad2_ref)
        pad1_ref[CONV_PAD:CONV_PAD + T, :] = u1_ref[...].astype(F32)
        pad2_ref[CONV_PAD:CONV_PAD + T, :] = u2_ref[...].astype(F32)

        def main(base):
            h1 = _taps_fwd(pad1_ref, w1_ref, K, base) + b1_ref[...]
            h2 = _taps_fwd(pad2_ref, w2_ref, K, base) + b2_ref[...]
            o_ref[pl.ds(base, CONV_R), :] = (h1 * _sig(h1) * h2).astype(BF16)
        _loop_rows(T, main)

    return _conv_call(body, name, T, nt,
                      [_col(T), _col(T, nt), _col(T, rows=K), _col(T, nt, rows=K), _col(T, rows=1), _col(T, nt, rows=1)],
                      _col(T), _sds((T, D_FF), BF16), 2)(up, up, w, w, b, b)


def _conv_f_bwd(up, w, b, dact, name):
    T = up.shape[0]
    K, nt = CONV_F, D_FF // LANES

    def body(u1_ref, u2_ref, w1_ref, w2_ref, b1_ref, b2_ref, d_ref, du1_ref, du2_ref, dw1_ref, dw2_ref, db1_ref, db2_ref,
             pad1_ref, pad2_ref, padd1_ref, padd2_ref):
        _zero_head(pad1_ref)
        _zero_head(pad2_ref)
        _zero_tail(padd1_ref, T)
        _zero_tail(padd2_ref, T)
        for r in (dw1_ref, dw2_ref, db1_ref, db2_ref):
            r[...] = jnp.zeros_like(r)
        pad1_ref[CONV_PAD:CONV_PAD + T, :] = u1_ref[...].astype(F32)
        pad2_ref[CONV_PAD:CONV_PAD + T, :] = u2_ref[...].astype(F32)

        def pre(base):
            rows = pl.ds(base, CONV_R)
            h1 = _taps_fwd(pad1_ref, w1_ref, K, base) + b1_ref[...]
            h2 = _taps_fwd(pad2_ref, w2_ref, K, base) + b2_ref[...]
            s = _sig(h1)
            d = _f32(d_ref, rows)
            padd1_ref[rows, :] = d * h2 * s * (1.0 + h1 * (1.0 - s))
            padd2_ref[rows, :] = d * h1 * s
        _loop_rows(T, pre)

        def main(base):
            rows = pl.ds(base, CONV_R)
            d1, d2 = padd1_ref[rows, :], padd2_ref[rows, :]
            _wgrad_acc(dw1_ref, pad1_ref, d1, K, base)
            _wgrad_acc(dw2_ref, pad2_ref, d2, K, base)
            db1_ref[...] += _fold8(d1)
            db2_ref[...] += _fold8(d2)
            du1_ref[rows, :] = _taps_bwd(padd1_ref, w1_ref, K, base).astype(BF16)
            du2_ref[rows, :] = _taps_bwd(padd2_ref, w2_ref, K, base).astype(BF16)
        _loop_rows(T, main)

    wrow, brow = _col(T, rows=K * SUBLANES), _col(T, rows=SUBLANES)
    return _conv_call(body, name, T, nt,
                      [_col(T), _col(T, nt), _col(T, rows=K), _col(T, nt, rows=K), _col(T, rows=1), _col(T, nt, rows=1), _col(T)],
                      (_col(T), _col(T), wrow, wrow, brow, brow),
                      (_sds((T, D_FF), BF16), _sds((T, D_FF), BF16), _sds((K * SUBLANES, D_FF)), _sds((K * SUBLANES, D_FF)),
                       _sds((SUBLANES, D_FF)), _sds((SUBLANES, D_FF))), 2, 2)(up, up, w, w, b, b, dact)


def _dot(a, b, dims="nn"):
    return lax.dot_general(a.astype(BF16), b.astype(BF16), _DIMS[dims], preferred_element_type=F32)


def _dot_mask(mask, v, mask_left):
    mb = mask.astype(BF16)
    hi = v.astype(BF16)
    r1 = v - hi.astype(F32)
    mid = r1.astype(BF16)
    lo = (r1 - mid.astype(F32)).astype(BF16)
    d = [jnp.dot(mb, t, preferred_element_type=F32) if mask_left else jnp.dot(t, mb, preferred_element_type=F32) for t in (hi, mid, lo)]
    return (d[0] + d[1]) + d[2]


def _ssd_small(xcr_ref, xrr_ref, bc_ref, br_ref, ac_ref, ar_ref):
    Q = SSD_Q
    li = lax.broadcasted_iota(jnp.int32, (Q, Q), 0)
    si = lax.broadcasted_iota(jnp.int32, (Q, Q), 1)
    tril = li >= si
    dtc = jax.nn.softplus(xcr_ref[...] + bc_ref[...])
    dtr = jax.nn.softplus(xrr_ref[...] + br_ref[...])
    cumc = _dot_mask(tril, dtc * ac_ref[...], True)
    cumr = _dot_mask(li <= si, dtr * ar_ref[...], False)
    return tril, dtc, dtr, cumc, cumr


def _ssd_specs(nc, rev):
    Q = SSD_Q
    cc = (lambda c: nc - 1 - c) if rev else (lambda c: c)
    x_spec = pl.BlockSpec((Q, 2 * LANES), lambda g, c: (cc(c), g))
    b_spec = pl.BlockSpec((Q, LANES), lambda g, c: (cc(c), 8 + g))
    c_spec = pl.BlockSpec((Q, LANES), lambda g, c: (cc(c), 12 + g))
    colm = pl.BlockSpec((None, Q, LANES), lambda g, c: (g, cc(c), 0))
    rowm = pl.BlockSpec((None, SUBLANES, Q), lambda g, c: (g, 0, cc(c)))
    colv = pl.BlockSpec((None, 1, LANES), lambda g, c: (g, 0, 0))
    rowv = pl.BlockSpec((None, SUBLANES, 1), lambda g, c: (g, 0, 0))
    st_spec = pl.BlockSpec((None, None, 2 * LANES, N_STATE), lambda g, c: (cc(c), g, 0, 0))
    return x_spec, b_spec, c_spec, colm, rowm, colv, rowv, st_spec


def _ssd_fwd(xc, raw_col, raw_row, bias_col, bias_row, a_col, a_row, dskip, name):
    T = xc.shape[0]
    Q = SSD_Q
    nc = T // Q
    x_spec, b_spec, c_spec, colm, rowm, colv, rowv, st_spec = _ssd_specs(nc, False)

    def body(dk_ref, x_ref, b_ref, c_ref, xcr_ref, xrr_ref, bc_ref, br_ref, ac_ref, ar_ref, y_ref, st_ref, h_ref):
        g = pl.program_id(0)

        @pl.when(pl.program_id(1) == 0)
        def _():
            h_ref[...] = jnp.zeros_like(h_ref)

        tril, dtc, dtr, cumc, cumr = _ssd_small(xcr_ref, xrr_ref, bc_ref, br_ref, ac_ref, ar_ref)
        Bm, Cm = b_ref[...], c_ref[...]
        S = _dot(Cm, Bm, "nt")
        lo = lax.broadcasted_iota(jnp.int32, (Q, LANES), 1) < HEAD_P
        rlo = lax.broadcasted_iota(jnp.int32, (LANES, N_STATE), 0) < HEAD_P
        st_ref[...] = h_ref[...]
        clast = cumc[Q - 1:Q, :]
        for pr in range(2):
            cols = slice(pr * LANES, (pr + 1) * LANES)
            xp = x_ref[:, cols]
            yd = jnp.zeros((Q, LANES), F32)
            for q in range(2):
                hh = 2 * pr + q
                seg = cumc[:, hh:hh + 1] - cumr[hh:hh + 1, :]
                lm = jnp.where(tril, jnp.exp(jnp.where(tril, seg, 0.0)), 0.0)
                w = S * lm * dtr[hh:hh + 1, :]
                xm = jnp.where(lo if q == 0 else jnp.logical_not(lo), xp, 0.0)
                yd = yd + _dot(w, xm)
            h0, h1 = 2 * pr, 2 * pr + 1
            c0, c1 = cumc[:, h0:h0 + 1], cumc[:, h1:h1 + 1]
            e_pair = jnp.where(lo, jnp.exp(c0), jnp.exp(c1))
            hp = h_ref[cols, :]
            ch = _dot(Cm, hp, "nt")
            dsk = jnp.where(lo, dk_ref[4 * g + h0], dk_ref[4 * g + h1])
            y_ref[:, cols] = yd + e_pair * ch + dsk * xp
            cl0, cl1 = clast[:, h0:h0 + 1], clast[:, h1:h1 + 1]
            sdec = jnp.where(lo, jnp.exp(cl0 - c0) * dtc[:, h0:h0 + 1], jnp.exp(cl1 - c1) * dtc[:, h1:h1 + 1])
            decrow = jnp.where(rlo, jnp.exp(cl0), jnp.exp(cl1))
            h_ref[cols, :] = hp * decrow + _dot(xp * sdec, Bm, "tn")

    smem = pl.BlockSpec(memory_space=pltpu.SMEM)
    return pl.pallas_call(
        body, name=name, grid=(N_GROUPS, nc),
        in_specs=[smem, x_spec, b_spec, c_spec, colm, rowm, colv, rowv, colv, rowv],
        out_specs=(x_spec, st_spec),
        out_shape=(_sds((T, D)), _sds((nc, N_GROUPS, 2 * LANES, N_STATE))),
        scratch_shapes=[pltpu.VMEM((2 * LANES, N_STATE), F32)],
        compiler_params=_cparams(("parallel", "arbitrary")))(dskip, xc, xc, xc, raw_col, raw_row, bias_col, bias_row, a_col, a_row)


def _ssd_bwd(xc, raw_col, raw_row, bias_col, bias_row, a_col, a_row, dskip, states, dy, name):
    T = xc.shape[0]
    Q = SSD_Q
    nc = T // Q
    x_spec, b_spec, c_spec, colm, rowm, colv, rowv, st_spec = _ssd_specs(nc, True)
    bo_spec = pl.BlockSpec((Q, LANES), lambda g, c: (nc - 1 - c, g))
    dd_spec = pl.BlockSpec((None, None, SUBLANES, 2 * LANES), lambda g, c: (nc - 1 - c, g, 0, 0))

    def body(dk_ref, x_ref, b_ref, c_ref, xcr_ref, xrr_ref, bc_ref, br_ref, ac_ref, ar_ref, st_ref, dy_ref,
             dx_ref, db_ref, dc_ref, sq_ref, cms_ref, ddac_ref, ddar_ref, dd_ref, dh_ref):
        g = pl.program_id(0)

        @pl.when(pl.program_id(1) == 0)
        def _():
            dh_ref[...] = jnp.zeros_like(dh_ref)

        tril, dtc, dtr, cumc, cumr = _ssd_small(xcr_ref, xrr_ref, bc_ref, br_ref, ac_ref, ar_ref)
        Bm, Cm = b_ref[...], c_ref[...]
        S = _dot(Cm, Bm, "nt")
        lane = lax.broadcasted_iota(jnp.int32, (Q, LANES), 1)
        sub = lax.broadcasted_iota(jnp.int32, (SUBLANES, Q), 0)
        rowi = lax.broadcasted_iota(jnp.int32, (Q, LANES), 0)
        lo = lane < HEAD_P
        rlo = lax.broadcasted_iota(jnp.int32, (LANES, N_STATE), 0) < HEAD_P
        clast = cumc[Q - 1:Q, :]
        ds_g = jnp.zeros((Q, Q), F32)
        dcm = jnp.zeros((Q, N_STATE), F32)
        dbm = jnp.zeros((Q, N_STATE), F32)
        dcum_col = jnp.zeros((Q, LANES), F32)
        dcum_row = jnp.zeros((SUBLANES, Q), F32)
        sq_col = jnp.zeros((Q, LANES), F32)
        cms_row = jnp.zeros((SUBLANES, Q), F32)
        for pr in range(2):
            cols = slice(pr * LANES, (pr + 1) * LANES)
            xp, dyp = x_ref[:, cols], dy_ref[:, cols]
            hin, dhp = st_ref[cols, :], dh_ref[cols, :]
            h0, h1 = 2 * pr, 2 * pr + 1
            c0, c1 = cumc[:, h0:h0 + 1], cumc[:, h1:h1 + 1]
            cl0, cl1 = clast[:, h0:h0 + 1], clast[:, h1:h1 + 1]
            e_pair = jnp.where(lo, jnp.exp(c0), jnp.exp(c1))
            edec = jnp.where(lo, jnp.exp(cl0 - c0), jnp.exp(cl1 - c1))
            dt_pair = jnp.where(lo, dtc[:, h0:h0 + 1], dtc[:, h1:h1 + 1])
            sdec = edec * dt_pair
            ch = _dot(Cm, hin, "nt")
            xb = _dot(Bm, dhp, "nt")
            dye = dyp * e_pair
            t1 = dye * ch
            t2 = xp * xb * edec
            hh_prod = dhp * hin
            dsk = jnp.where(lo, dk_ref[4 * g + h0], dk_ref[4 * g + h1])
            dxp = sdec * xb + dsk * dyp
            for q in range(2):
                hh = 2 * pr + q
                mine = lo if q == 0 else jnp.logical_not(lo)
                seg = cumc[:, hh:hh + 1] - cumr[hh:hh + 1, :]
                lm = jnp.where(tril, jnp.exp(jnp.where(tril, seg, 0.0)), 0.0)
                dtrow = dtr[hh:hh + 1, :]
                w = S * lm * dtrow
                dym = jnp.where(mine, dyp, 0.0)
                gl = _dot(dym, xp, "nt") * lm
                ds_g = ds_g + gl * dtrow
                ms = gl * S
                m = ms * dtrow
                dxp = dxp + _dot(w, dym, "tn")
                cms_row = jnp.where(sub == hh, jnp.sum(ms, axis=0, keepdims=True), cms_row)
                dcum_row = jnp.where(sub == hh, -jnp.sum(m, axis=0, keepdims=True), dcum_row)
                t1h = jnp.sum(jnp.where(mine, t1, 0.0), axis=1, keepdims=True)
                sqh = jnp.sum(jnp.where(mine, t2, 0.0), axis=1, keepdims=True)
                sth = sqh * dtc[:, hh:hh + 1]
                rmine = rlo if q == 0 else jnp.logical_not(rlo)
                hsum = jnp.sum(jnp.sum(jnp.where(rmine, hh_prod, 0.0), axis=1, keepdims=True), axis=0, keepdims=True)
                last = jnp.sum(sth, axis=0, keepdims=True) + jnp.exp(clast[:, hh:hh + 1]) * hsum
                dcol = jnp.sum(m, axis=1, keepdims=True) + t1h - sth
                dcum_col = jnp.where(lane == hh, dcol + jnp.where(rowi == Q - 1, last, 0.0), dcum_col)
                sq_col = jnp.where(lane == hh, sqh, sq_col)
            dcm = dcm + _dot(dye, hin)
            dbm = dbm + _dot(xp * sdec, dhp)
            decrow = jnp.where(rlo, jnp.exp(cl0), jnp.exp(cl1))
            dh_ref[cols, :] = dhp * decrow + _dot(dye, Cm, "tn")
            dx_ref[:, cols] = dxp
            dd_ref[:, cols] = jnp.broadcast_to(jnp.sum(dyp * xp, axis=0, keepdims=True), (SUBLANES, LANES))
        dc_ref[...] = dcm + _dot(ds_g, Bm)
        db_ref[...] = dbm + _dot(ds_g, Cm, "tn")
        li = lax.broadcasted_iota(jnp.int32, (Q, Q), 0)
        si = lax.broadcasted_iota(jnp.int32, (Q, Q), 1)
        ddac_ref[...] = _dot_mask(li <= si, dcum_col, True)
        ddar_ref[...] = _dot_mask(tril, dcum_row, False)
        sq_ref[...] = sq_col
        cms_ref[...] = cms_row

    smem = pl.BlockSpec(memory_space=pltpu.SMEM)
    return pl.pallas_call(
        body, name=name, grid=(N_GROUPS, nc),
        in_specs=[smem, x_spec, b_spec, c_spec, colm, rowm, colv, rowv, colv, rowv, st_spec, x_spec],
        out_specs=(x_spec, bo_spec, bo_spec, colm, rowm, colm, rowm, dd_spec),
        out_shape=(_sds((T, D)), _sds((T, D // 2)), _sds((T, D // 2)), _sds((N_GROUPS, T, LANES)), _sds((N_GROUPS, SUBLANES, T)),
                   _sds((N_GROUPS, T, LANES)), _sds((N_GROUPS, SUBLANES, T)), _sds((nc, N_GROUPS, SUBLANES, 2 * LANES))),
        scratch_shapes=[pltpu.VMEM((2 * LANES, N_STATE), F32)],
        compiler_params=_cparams(("parallel", "arbitrary")))(dskip, xc, xc, xc, raw_col, raw_row, bias_col, bias_row, a_col, a_row,
                                                            states, dy)


def _adam_math(wv, gv, mv, vv):
    c1 = 1.0 - ADAM_B1 ** ADAM_STEP
    c2 = 1.0 - ADAM_B2 ** ADAM_STEP
    mn = ADAM_B1 * mv + (1.0 - ADAM_B1) * gv
    vn = ADAM_B2 * vv + (1.0 - ADAM_B2) * (gv * gv)
    return -ADAM_LR * ((mn / c1) / (jnp.sqrt(vn / c2) + ADAM_EPS) + ADAM_WD * wv), mn, vn


def _adamw_layers(w, g, m, v, l0, Lg, bufs, name):
    L, As, Bs = w.shape
    tr = _tile(As, [], (256, 352, 128))
    has_bufs = bufs is not None

    def body(*refs):
        w_ref, g_ref, m_ref, v_ref = refs[:4]
        d_ref, mo_ref, vo_ref = refs[4 + 3 * has_bufs:]
        d_ref[...], mo_ref[...], vo_ref[...] = _adam_math(w_ref[...], g_ref[...], m_ref[...], v_ref[...])

    spec = pl.BlockSpec((None, tr, Bs), lambda l, i: (l + l0, i, 0))
    args = (w, g, m, v) + (tuple(bufs) if has_bufs else ())
    return pl.pallas_call(
        body, name=name, grid=(Lg, As // tr), in_specs=[spec] * 4 + [_ANY] * (3 * has_bufs), out_specs=(spec,) * 3,
        out_shape=(_sds((L, As, Bs)),) * 3, input_output_aliases={4: 0, 5: 1, 6: 2} if has_bufs else {},
        compiler_params=_cparams(("parallel", "parallel")))(*args)


def _adamw_minor_rows(w, g, m, v, name):
    L, R, C = w.shape
    tr = _tile(C, [], (12, 8, 4, 2, 1))
    wt, gt, mt, vt = (jnp.transpose(t, (2, 0, 1)) for t in (w, g, m, v))

    def body(w_ref, g_ref, m_ref, v_ref, d_ref, mo_ref, vo_ref):
        d_ref[...], mo_ref[...], vo_ref[...] = _adam_math(w_ref[...], g_ref[...], m_ref[...], v_ref[...])

    spec = pl.BlockSpec((tr, L, R), lambda i: (i, 0, 0))
    out = pl.pallas_call(body, name=name, grid=(C // tr,), in_specs=[spec] * 4, out_specs=(spec,) * 3,
                         out_shape=(_sds((C, L, R)),) * 3, compiler_params=_cparams(("parallel",)))(wt, gt, mt, vt)
    return tuple(jnp.transpose(o, (1, 2, 0)) for o in out) + (jnp.transpose(gt, (1, 2, 0)),)


def _adamw(w, g, m, v, name):
    shape = w.shape
    cols = shape[-1]
    w2, g2, m2, v2 = (t.reshape(-1, cols) for t in (w, g, m, v))
    rows = w2.shape[0]
    tr = 256 if (rows % 256 == 0 and rows > 256) else rows
    c1 = 1.0 - ADAM_B1 ** ADAM_STEP
    c2 = 1.0 - ADAM_B2 ** ADAM_STEP

    def body(w_ref, g_ref, m_ref, v_ref, d_ref, mo_ref, vo_ref):
        gv = g_ref[...]
        mn = ADAM_B1 * m_ref[...] + (1.0 - ADAM_B1) * gv
        vn = ADAM_B2 * v_ref[...] + (1.0 - ADAM_B2) * (gv * gv)
        d_ref[...] = -ADAM_LR * ((mn / c1) / (jnp.sqrt(vn / c2) + ADAM_EPS) + ADAM_WD * w_ref[...])
        mo_ref[...] = mn
        vo_ref[...] = vn

    spec = pl.BlockSpec((tr, cols), lambda i: (i, 0))
    out = pl.pallas_call(body, name=name, grid=(rows // tr,), in_specs=[spec] * 4, out_specs=(spec,) * 3,
                         out_shape=(_sds((rows, cols)),) * 3, compiler_params=_cparams(("parallel",)))(w2, g2, m2, v2)
    return tuple(o.reshape(shape) for o in out)


def _place():
    x, y, c = lax.axis_index("x"), lax.axis_index("y"), lax.axis_index("c")
    chips = [(1 - x, y), (x, 1 - y), (1 - x, 1 - y)]
    return x, y, c, chips


_ANY = pl.BlockSpec(memory_space=pl.ANY)


TENSORS = (("e_w_in", "row", 2, 4096, 1284, 1024), ("e_w_out", "row", 2, 2048, 1024, 512), ("o_w_in", "col", 2, 1024, 3072, 768),
           ("o_w_out", "row", 2, 1024, 1024, 256), ("f_w_up", "col", 4, 1024, 5632, 1408), ("f_w_down", "row", 4, 2816, 1024, 704),
           ("ple_w_proj", "col", 4, 256, 1024, 256), ("ple_w_gate", "row", 4, 1024, 1024, 256))
MIX, FFN = "mix", "ffn"
W_GROUPS = (((0, MIX),), ((0, FFN), (1, MIX)), ((1, FFN), (2, MIX)), ((2, FFN), (3, MIX), (3, FFN)))
G_GROUPS = (((3, FFN), (3, MIX), (2, FFN), (2, MIX), (1, FFN), (1, MIX)), ((0, FFN),), ((0, MIX),))


def _tensor_layer(name, layer):
    if name.startswith("e_"):
        return layer // 2 if layer % 2 == 0 else None
    if name.startswith("o_"):
        return layer // 2 if layer % 2 == 1 else None
    return layer


def _part(name):
    return MIX if name.startswith(("e_", "o_")) else FFN


def _group_items(members):
    items = []
    for name, kind, L, A, B, n in TENSORS:
        tls = sorted(t for t in (_tensor_layer(name, l) for l, part in members if part == _part(name)) if t is not None)
        if tls:
            assert tls == list(range(tls[0], tls[0] + len(tls)))
            items.append((name, kind, len(tls), A, B, n, tls[0]))
    return items


def _hwin(ref, it, k, h):
    name, kind, Lg, A, B, n, l0 = it
    if kind == "row":
        return ref.at[:, pl.ds(pl.multiple_of(k * n + h * (n // 2), 16), n // 2), :]
    return ref.at[:, pl.ds(pl.multiple_of(h * (A // 2), 16), A // 2), pl.ds(pl.multiple_of(k * n, LANES), n)]


def _shard_dims(kind, A, B, n):
    return (n, B) if kind == "row" else (A, n)


def _cast_into(w, it, me):
    name, kind, Lg, A, B, n, l0 = it
    As, Bs = _shard_dims(kind, A, B, n)

    def body(me_ref, w_ref, o_ref):
        o_ref[...] = w_ref[...].astype(BF16)

    omap = (lambda l, m: (l, m[0], 0)) if kind == "row" else (lambda l, m: (l, 0, m[0]))
    grid_spec = pltpu.PrefetchScalarGridSpec(
        num_scalar_prefetch=1, grid=(Lg,), in_specs=[pl.BlockSpec((None, As, Bs), lambda l, m: (l + l0, 0, 0))],
        out_specs=pl.BlockSpec((None, As, Bs), omap))
    return pl.pallas_call(body, name=f"cast_{name}_{l0}", grid_spec=grid_spec, out_shape=_sds((Lg, A, B), BF16),
                          compiler_params=_cparams(("parallel",)))(me, w.reshape(-1, As, Bs))


_HBM = pl.BlockSpec(memory_space=pltpu.HBM)
_SEM = pl.BlockSpec(memory_space=pltpu.SEMAPHORE)
_EFFECT = pltpu.SideEffectType.DATAFLOW_SIDE_EFFECTING


def _hbm(a):
    return pltpu.with_memory_space_constraint(a, pltpu.HBM)


def _split_start(thru, n_copies, issue, name, after=None):
    N = len(thru)
    has_after = after is not None

    def body(*refs):
        outs = refs[N + has_after:2 * N + has_after]
        send_sems, recv_sems, token = refs[2 * N + has_after:]
        for cp in issue(outs, send_sems, recv_sems):
            cp.start()
        token[...] = jnp.zeros_like(token)

    out = pl.pallas_call(
        body, name=name, in_specs=[_HBM] * N + ([_ANY] if has_after else []),
        out_specs=(_HBM,) * N + (_SEM, _SEM, pl.BlockSpec(memory_space=pltpu.VMEM)),
        out_shape=tuple(pltpu.HBM(a.shape, a.dtype) for a in thru)
        + (pltpu.SemaphoreType.DMA((n_copies,)), pltpu.SemaphoreType.DMA((n_copies,)), _sds((SUBLANES, LANES))),
        input_output_aliases={t: t for t in range(N)},
        compiler_params=pltpu.CompilerParams(has_side_effects=_EFFECT))(*[_hbm(a) for a in thru], *([after] if has_after else []))
    return list(out[:N]), out[N], out[N + 1], out[N + 2]


def _split_wait(thru, send_sems, recv_sems, after, waits, name):
    N = len(thru)
    after = list(after) if isinstance(after, (list, tuple)) else [after]

    def body(*refs):
        ins = refs[:N]
        for cp, side in waits(ins, refs[N], refs[N + 1]):
            if side == "send":
                cp.wait_send()
            else:
                cp.wait_recv()

    out = pl.pallas_call(
        body, name=name, in_specs=[_HBM] * N + [_SEM, _SEM] + [_ANY] * len(after), out_specs=(_HBM,) * N,
        out_shape=tuple(pltpu.HBM(a.shape, a.dtype) for a in thru), input_output_aliases={t: t for t in range(N)},
        compiler_params=pltpu.CompilerParams(has_side_effects=_EFFECT))(*thru, send_sems, recv_sems, *after)
    return list(out)


def _rcopy(send_sems, recv_sems, k, src, dst, to):
    return pltpu.make_async_remote_copy(src_ref=src, dst_ref=dst, send_sem=send_sems.at[k], recv_sem=recv_sems.at[k],
                                        device_id=to, device_id_type=MESH)


def _gather_copies(items, refs, send_sems, recv_sems, what):
    x, y, c, chips = _place()
    me = 2 * x + y
    out = []
    for t, it in enumerate(items):
        mine = _hwin(refs[t], it, me, c)
        for j, (px, py) in enumerate(chips):
            if what == "start":
                out.append(_rcopy(send_sems, recv_sems, 3 * t + j, mine, mine, (px, py, c)))
            else:
                slot = _hwin(refs[t], it, 2 * px + py, c)
                out.append((_rcopy(send_sems, recv_sems, 3 * t + j, mine, mine, (px, py, c)), "send"))
                out.append((_rcopy(send_sems, recv_sems, 3 * t + j, slot, slot, (px, py, c)), "recv"))
    return out


def _gather_start(fulls, items, name, after=None):
    return _split_start(fulls, 3 * len(items), functools.partial(_gather_copies, items, what="start"), name, after)


def _gather_wait(fulls, send_sems, recv_sems, after, items, name):
    return _split_wait(fulls, send_sems, recv_sems, after, functools.partial(_gather_copies, items, what="wait"), name)


def _gather_fwd(fulls, items, name, ws=None):
    N = len(fulls)
    has_ws = ws is not None

    def body(*refs):
        outs = refs[N + has_ws:2 * N + has_ws]
        rest = refs[2 * N + has_ws:]
        x, y, c, chips = _place()
        me = 2 * x + y
        sib = (x, y, 1 - c)
        if has_ws:
            ws_ref = refs[N]
            WS_ref, send_sems, recv_sems, lsem = rest
            loc = pltpu.make_async_copy(ws_ref, WS_ref.at[me], lsem)
            loc.start()
        else:
            send_sems, recv_sems = rest
        rc = functools.partial(_rcopy, send_sems, recv_sems)
        cps = []
        for t, it in enumerate(items):
            for j, (px, py) in enumerate(chips):
                slot = _hwin(outs[t], it, 2 * px + py, c)
                cps.append(rc(3 * t + j, slot, slot, sib))
        if has_ws:
            cps += [rc(3 * N + j, ws_ref, WS_ref.at[me], (*chip, c)) for j, chip in enumerate(chips)]
        for cp in cps:
            cp.start()
        for t, it in enumerate(items):
            for j, (px, py) in enumerate(chips):
                oslot = _hwin(outs[t], it, 2 * px + py, 1 - c)
                rc(3 * t + j, oslot, oslot, sib).wait_recv()
        if has_ws:
            for j, (px, py) in enumerate(chips):
                sslot = WS_ref.at[2 * px + py]
                rc(3 * N + j, sslot, sslot, sib).wait_recv()
        for cp in cps:
            cp.wait_send()
        if has_ws:
            loc.wait()

    ns = 3 * N + (3 if has_ws else 0)
    out_shape = tuple(_sds(f.shape, f.dtype) for f in fulls)
    scratch = [pltpu.SemaphoreType.DMA((ns,)), pltpu.SemaphoreType.DMA((ns,))]
    args = list(fulls)
    if has_ws:
        out_shape += (_sds((4,) + ws.shape, ws.dtype),)
        scratch.append(pltpu.SemaphoreType.DMA(()))
        args.append(ws)
    out = pl.pallas_call(
        body, name=name, in_specs=[_ANY] * len(args), out_specs=(_ANY,) * len(out_shape), out_shape=out_shape,
        input_output_aliases={t: t for t in range(N)}, scratch_shapes=scratch,
        compiler_params=pltpu.CompilerParams(has_side_effects=True))(*args)
    return (list(out[:N]), out[N]) if has_ws else (list(out), None)


def _half_shape(it):
    name, kind, Lg, A, B, n, l0 = it
    return (Lg, 4, n // 2, B) if kind == "row" else (Lg, A // 2, B)


def _piece_shape(it):
    name, kind, Lg, A, B, n, l0 = it
    return (Lg, n // 2, B) if kind == "row" else (Lg, A // 2, n)


def _swap_copies(items, refs, send_sems, recv_sems, what):
    N = len(items)
    x, y, c, _ = _place()
    sib = (x, y, 1 - c)
    out = []
    for t, it in enumerate(items):
        name_, kind, Lg, A, B, n, l0 = it
        if kind == "row":
            cps = [_rcopy(send_sems, recv_sems, 4 * t + k, _hwin(refs[t], it, k, 1 - c), refs[N + t].at[:, k], sib) for k in range(4)]
        else:
            src = refs[t].at[:, pl.ds(pl.multiple_of((1 - c) * (A // 2), 16), A // 2), :]
            cps = [_rcopy(send_sems, recv_sems, 4 * t, src, refs[N + t], sib)]
        for cp in cps:
            if what == "start":
                out.append(cp)
            else:
                out += [(cp, "send"), (cp, "recv")]
    return out


def _swap_start(gs, items, name, after=None):
    lands = [lax.empty(_half_shape(it), F32) for it in items]
    return _split_start(list(gs) + lands, 4 * len(items), functools.partial(_swap_copies, items, what="start"), name, after)


def _swap_wait(thru, send_sems, recv_sems, after, items, name):
    return _split_wait(thru, send_sems, recv_sems, after, functools.partial(_swap_copies, items, what="wait"), name)


def _add_half(g, ra, it, cvec):
    name, kind, Lg, A, B, n, l0 = it
    if kind == "row":
        blk = (None, n // 2, B)
        grid = (Lg, 4)
        g_spec = pl.BlockSpec(blk, lambda l, k, cr: (l, 2 * k + cr[0], 0))
        h_spec = pl.BlockSpec((None, None, n // 2, B), lambda l, k, cr: (l, k, 0, 0))
    else:
        tr = _tile(A // 2, [], (256, 128))
        nb = (A // 2) // tr
        grid = (Lg, nb)
        g_spec = pl.BlockSpec((None, tr, B), lambda l, i, cr: (l, cr[0] * nb + i, 0))
        h_spec = pl.BlockSpec((None, tr, B), lambda l, i, cr: (l, i, 0))

    def body(c_ref, g_ref, r_ref, o_ref):
        o_ref[...] = (g_ref[...] + r_ref[...]).astype(BF16)

    grid_spec = pltpu.PrefetchScalarGridSpec(num_scalar_prefetch=1, grid=grid, in_specs=[g_spec, h_spec], out_specs=h_spec)
    return pl.pallas_call(body, name=f"addhalf_{name}_{l0}", grid_spec=grid_spec, out_shape=_sds(_half_shape(it), BF16),
                          compiler_params=_cparams(("parallel", "parallel")))(cvec, g, ra)


def _scatter_copies(items, refs, send_sems, recv_sems, what):
    N = len(items)
    x, y, c, chips = _place()
    out = []
    for t, it in enumerate(items):
        name, kind, Lg, A, B, n, l0 = it
        for j, (px, py) in enumerate(chips):
            k = 2 * px + py
            src = refs[t].at[:, k] if kind == "row" else refs[t].at[:, :, pl.ds(pl.multiple_of(k * n, LANES), n)]
            cp = _rcopy(send_sems, recv_sems, 3 * t + j, src, refs[N + t].at[j], (px, py, c))
            if what == "start":
                out.append(cp)
            else:
                out += [(cp, "send"), (cp, "recv")]
    return out


def _scatter_start(ps, items, name):
    lands = [lax.empty((3,) + _piece_shape(it), BF16) for it in items]
    return _split_start(list(ps) + lands, 3 * len(items), functools.partial(_scatter_copies, items, what="start"), name)


def _scatter_wait(thru, send_sems, recv_sems, after, items, name):
    return _split_wait(thru, send_sems, recv_sems, after, functools.partial(_scatter_copies, items, what="wait"), name)


def _sum_own(p, rc, it, mevec, buf):
    name, kind, Lg, A, B, n, l0 = it
    As, Bs = _shard_dims(kind, A, B, n)
    L = [s[2] for s in TENSORS if s[0] == name][0]
    hb = (As // 2, Bs)
    has_buf = buf is not None

    def body(*refs):
        p_ref, r0, r1, r2 = refs[1:5]
        o_ref = refs[5 + has_buf]
        o_ref[...] = ((p_ref[...].astype(F32) + r0[...].astype(F32)) + r1[...].astype(F32)) + r2[...].astype(F32)

    if kind == "row":
        p_spec = pl.BlockSpec((None, None) + hb, lambda l, m: (l, m[0], 0, 0))
    else:
        p_spec = pl.BlockSpec((None,) + hb, lambda l, m: (l, 0, m[0]))
    r_specs = [pl.BlockSpec((None, None) + hb, functools.partial(lambda l, m, j: (j, l, 0, 0), j=j)) for j in range(3)]
    in_specs = [p_spec] + r_specs + ([_ANY] if has_buf else [])
    grid_spec = pltpu.PrefetchScalarGridSpec(num_scalar_prefetch=1, grid=(Lg,), in_specs=in_specs,
                                             out_specs=pl.BlockSpec((None,) + hb, lambda l, m: (l + l0, m[1], 0)))
    args = (mevec, p, rc, rc, rc) + ((buf,) if has_buf else ())
    return pl.pallas_call(body, name=f"sumown_{name}_{l0}", grid_spec=grid_spec, out_shape=_sds((L, As, Bs)),
                          input_output_aliases={5: 0} if has_buf else {}, compiler_params=_cparams(("parallel",)))(*args)


def _join_halves(rs, items, name):
    N = len(rs)

    def body(*refs):
        outs = refs[N:2 * N]
        send_sems, recv_sems = refs[2 * N:]
        x, y, c, _ = _place()
        sib = (x, y, 1 - c)

        def half(t, h):
            name_, kind, Lg, A, B, n, l0 = items[t]
            hr = _shard_dims(kind, A, B, n)[0] // 2
            return outs[t].at[pl.ds(l0, Lg), pl.ds(pl.multiple_of(h * hr, SUBLANES), hr), :]

        cps = [_rcopy(send_sems, recv_sems, t, half(t, c), half(t, c), sib) for t in range(N)]
        for cp in cps:
            cp.start()
        for t in range(N):
            _rcopy(send_sems, recv_sems, t, half(t, 1 - c), half(t, 1 - c), sib).wait_recv()
        for cp in cps:
            cp.wait_send()

    return list(pl.pallas_call(
        body, name=name, in_specs=[_ANY] * N, out_specs=(_ANY,) * N, out_shape=tuple(_sds(r.shape, r.dtype) for r in rs),
        input_output_aliases={t: t for t in range(N)},
        scratch_shapes=[pltpu.SemaphoreType.DMA((N,)), pltpu.SemaphoreType.DMA((N,))],
        compiler_params=pltpu.CompilerParams(has_side_effects=True))(*rs))


def _allgather_small(v):
    m_per, n = v.shape

    def body(x_ref, out_ref, send_sems, recv_sems, local_sem):
        x, y, c, chips = _place()
        me, sibling = (x, y, c), (x, y, 1 - c)

        def rows(px, py, pc):
            return out_ref.at[pl.ds(pl.multiple_of((4 * px + 2 * py + pc) * m_per, SUBLANES), m_per), :]

        def copy(k, block, to, src=None):
            return pltpu.make_async_remote_copy(src_ref=rows(*block) if src is None else src, dst_ref=rows(*block),
                                                send_sem=send_sems.at[k], recv_sem=recv_sems.at[k], device_id=to, device_id_type=MESH)

        mine = pltpu.make_async_copy(x_ref, rows(*me), local_sem)
        mine.start()
        first = [copy(0, me, sibling, src=x_ref)]
        first += [copy(1 + j, me, (*chip, c), src=x_ref) for j, chip in enumerate(chips)]
        for cp in first:
            cp.start()
        passed = [copy(4 + j, (*chip, c), sibling) for j, chip in enumerate(chips)]
        for j, chip in enumerate(chips):
            copy(1 + j, (*chip, c), me).wait_recv()
            passed[j].start()
        copy(0, sibling, me).wait_recv()
        for j, chip in enumerate(chips):
            copy(4 + j, (*chip, 1 - c), me).wait_recv()
        for cp in first + passed:
            cp.wait_send()
        mine.wait()

    vm = pl.BlockSpec(memory_space=pltpu.VMEM)
    return pl.pallas_call(body, name="allgather_small", in_specs=[vm], out_specs=vm, out_shape=_sds((8 * m_per, n)),
                          scratch_shapes=[pltpu.SemaphoreType.DMA((7,)), pltpu.SemaphoreType.DMA((7,)), pltpu.SemaphoreType.DMA(())],
                          compiler_params=pltpu.CompilerParams(has_side_effects=True, vmem_limit_bytes=VMEM_LIMIT))(v)


def _sum8(v, m_per):
    def body(v_ref, o_ref):
        acc = v_ref[0:m_per, :]
        for k in range(1, 8):
            acc = acc + v_ref[k * m_per:(k + 1) * m_per, :]
        o_ref[...] = acc

    return pl.pallas_call(body, name="small_sum_devices", out_shape=_sds((m_per, v.shape[1])),
                          compiler_params=pltpu.CompilerParams(vmem_limit_bytes=VMEM_LIMIT))(v)


SMALL_SHARDED = (("e_conv_a_w", 2), ("e_conv_b_w", 2), ("o_conv_w", 2), ("f_conv_w", 2), ("ln_g", 2), ("ln_b", 2))
SMALL_REPL = ("e_conv_a_b", "e_ln_a_g", "e_ln_a_b", "e_conv_b_b", "e_dt_bias", "e_a_log", "e_d_skip", "e_norm_b_g", "f_conv_b")

WEIGHT_ORDER = ('e_w_in', 'e_conv_a_w', 'e_conv_a_b', 'e_ln_a_g', 'e_ln_a_b', 'e_conv_b_w', 'e_conv_b_b', 'e_dt_bias', 'e_a_log',
                'e_d_skip', 'e_norm_b_g', 'e_w_out', 'o_w_in', 'o_conv_w', 'o_w_out', 'f_w_up', 'f_conv_w', 'f_conv_b', 'f_w_down',
                'ple_w_proj', 'ple_w_gate', 'ln_g', 'ln_b')


def _pack_rows(parts, width, total_rows, dtype):
    flat = jnp.concatenate([p.reshape(-1).astype(dtype) for p in parts])
    flat = jnp.pad(flat, (0, total_rows * width - flat.shape[0]))
    return flat.reshape(total_rows, width)


def _unpack_rows(buf, shapes):
    flat = buf.reshape(-1)
    out, pos = [], 0
    for s in shapes:
        n = math.prod(s)
        out.append(flat[pos:pos + n].reshape(s))
        pos += n
    return out


def _small_rows(shapes):
    n = sum(math.prod(s) for s in shapes)
    return -(-n // (LANES * SUBLANES)) * SUBLANES


E_PAD = 5248
SEG_A, SEG_Z, SEG_X, SEG_DT = (0, 2 * D), (2 * D, D), (3 * D, 2 * D), (5 * D, LANES)
G_SHAPES = {"e_w_in": (2, D, E_PAD), "e_w_out": (2, 2 * D, D), "o_w_in": (2, D, 3 * D), "o_w_out": (2, D, D),
            "f_w_up": (4, D, 2 * D_FF), "f_w_down": (4, D_FF, D), "ple_w_proj": (4, PLE, D), "ple_w_gate": (4, D, D)}


def _padcols(w, width):
    return jnp.pad(w, ((0, 0), (0, width - w.shape[1])))


def _fold_rows(dw, K):
    return dw.reshape(K, SUBLANES, dw.shape[-1]).sum(1)


class GradBuffers(dict):
    def __init__(self):
        super().__init__()
        self.where = {}
        for gi, layers in enumerate(G_GROUPS):
            for name, kind, Lg, A, B, n, l0 in _group_items(layers):
                for k in range(Lg):
                    self.where[(name, l0 + k)] = (gi, k, Lg)
        self.current = {}

    def into(self, name, layer, r0=0, c0=0):
        gi, k, Lg = self.where[(name, layer)]
        self.current[name] = (name, gi)
        return (self.get((name, gi)), (Lg,) + G_SHAPES[name][1:], (k,), r0, c0)

    def __setitem__(self, name, value):
        super().__setitem__(self.current[name], value)
        self.last = value


def _local_step(x, p, target, W, comm=None):
    T = x.shape[0]
    xb = x
    saved = []
    xc_f = x
    for i in range(DEPTH):
        j = i // 2
        L = {}
        L["x"], L["xb"] = xc_f, xb
        tok = comm.part_starts(i, MIX, xb) if comm is not None else None
        if i % 2 == 0:
            def w_in(seg, c0=0, cols=None, j=j):
                return V(W["e_w_in"], (j,), c0=seg[0] + c0, cols=seg[1] if cols is None else cols)

            ua = _mm(xb, w_in(SEG_A), "nn", f"l{i}_in_a", BF16, after=tok)
            z = _mm(xb, w_in(SEG_Z), "nn", f"l{i}_in_z")
            xu = _mm(xb, w_in(SEG_X), "nn", f"l{i}_in_xbc", BF16)
            udt = _mm(xb, w_in(SEG_DT), "nn", f"l{i}_in_dt")
            ac = _conv_a_fwd(ua, W["e_conv_a_w"][j], W["e_conv_a_b"][j][None], f"l{i}_conv_a")
            ya = _ln_silu_fwd(ac, W["e_ln_a_g"][j][None], W["e_ln_a_b"][j][None], f"l{i}_ln_a")
            xc = _conv_b_fwd(xu, W["e_conv_b_w"][j], W["e_conv_b_b"][j][None], f"l{i}_conv_b")
            sm = _ssd_small_inputs(udt[:, :N_HEADS], W["e_dt_bias"][j], W["e_a_log"][j])
            y, states = _ssd_fwd(xc, *sm, W["e_d_skip"][j], f"l{i}_ssd")
            yb = _gate_rms_fwd(y, z, W["e_norm_b_g"][j][None], f"l{i}_gate_rms")
            out_pairs = [(ya, V(W["e_w_out"], (j,), rows=D)), (yb, V(W["e_w_out"], (j,), r0=D))]
            L.update(ua=ua, z=z, xu=xu, udt=udt, ac=ac, ya=ya, xc=xc, sm=sm, y=y, states=states, yb=yb, w_in=w_in)
        else:
            uo = _mm(xb, V(W["o_w_in"], (j,)), "nn", f"l{i}_in", BF16, after=tok)
            sc = _conv_c_fwd(uo, W["o_conv_w"][j], f"l{i}_conv_c")
            out_pairs = [(sc, V(W["o_w_out"], (j,)))]
            L.update(uo=uo, sc=sc)
        h1, x1, x1b = _mm_sum(out_pairs, "nn", f"l{i}_out", ln_fwd=(xc_f, None, W["ln_g"][i, 0][None], W["ln_b"][i, 0][None]))
        tok = comm.part_starts(i, FFN, x1b) if comm is not None else None
        up = _mm(x1b, V(W["f_w_up"], (i,)), "nn", f"l{i}_ffn_up", BF16, after=tok)
        act = _conv_f_fwd(up, W["f_conv_w"][i], W["f_conv_b"][i][None], f"l{i}_conv_f")
        pv = V(p, (i, 0))
        pp = _mm(pv, V(W["ple_w_proj"], (i,)), "nn", f"l{i}_ple_proj")
        gl = _mm(x1b, V(W["ple_w_gate"], (i,)), "nn", f"l{i}_ple_gate")
        h2, x2, x2b = _mm_sum([(act, V(W["f_w_down"], (i,)))], "nn", f"l{i}_ffn_down",
                              ln_fwd=(x1, (pp, gl), W["ln_g"][i, 1][None], W["ln_b"][i, 1][None]))
        L.update(h1=h1, x1=x1, x1b=x1b, up=up, act=act, pv=pv, pp=pp, gl=gl, h2=h2)
        saved.append(L)
        xc_f, xb = x2, x2b

    sq, dx = _loss_head(xc_f, target, "loss_head")

    GB = GradBuffers()
    into = GB.into
    tok = None
    ln2_done = None

    G = {n: [None] * (DEPTH if n.startswith(("f_", "ln_")) else DEPTH // 2) for n in WEIGHT_ORDER if n not in G_SHAPES}
    for i in reversed(range(DEPTH)):
        j = i // 2
        L = saved[i]
        if ln2_done is None:
            ln2_done = _res_ln_bwd(dx, L["h2"], W["ln_g"][i, 1][None], (L["pp"], L["gl"]), f"l{i}_ln2_bwd")
        dh2, dh2b, dg2, db2, dpp, dgl = ln2_done
        ln2_done = None
        GB["f_w_down"] = _mm(L["act"], dh2b, "tn", f"l{i}_dw_down", dst=into("f_w_down", i))
        dact = _mm(dh2b, V(W["f_w_down"], (i,)), "nt", f"l{i}_dact", BF16, after=tok)
        du1, du2, dw1, dw2, dbf1, dbf2 = _conv_f_bwd(L["up"], W["f_conv_w"][i], W["f_conv_b"][i][None], dact, f"l{i}_conv_f_bwd")
        G["f_conv_w"][i] = jnp.concatenate([_fold_rows(dw1, CONV_F), _fold_rows(dw2, CONV_F)], axis=1)
        G["f_conv_b"][i] = jnp.concatenate([dbf1.sum(0), dbf2.sum(0)])
        GB["f_w_up"] = _mm(L["x1b"], du1, "tn", f"l{i}_dw_up1", dst=into("f_w_up", i))
        GB["f_w_up"] = _mm(L["x1b"], du2, "tn", f"l{i}_dw_up2", dst=into("f_w_up", i, c0=D_FF))
        GB["ple_w_proj"] = _mm(L["pv"], dpp, "tn", f"l{i}_dw_proj", dst=into("ple_w_proj", i))
        GB["ple_w_gate"] = _mm(L["x1b"], dgl, "tn", f"l{i}_dw_gate", dst=into("ple_w_gate", i))
        tok = comm.part_grads_done(i, FFN, GB) if comm is not None else None
        dh1, dh1b, dg1, db1 = _mm_sum(
            [(du1, V(W["f_w_up"], (i,), cols=D_FF)), (du2, V(W["f_w_up"], (i,), c0=D_FF)), (dgl, V(W["ple_w_gate"], (i,)))],
            "nt", f"l{i}_dx1", add=dh2, add_scale=ALPHA, after=tok, ln_bwd=(L["h1"], W["ln_g"][i, 0][None], None))
        G["ln_g"][i] = jnp.concatenate([dg1, dg2], axis=0)
        G["ln_b"][i] = jnp.concatenate([db1, db2], axis=0)
        if i % 2 == 0:
            GB["e_w_out"] = _mm(L["ya"], dh1b, "tn", f"l{i}_dw_out_a", dst=into("e_w_out", j))
            GB["e_w_out"] = _mm(L["yb"], dh1b, "tn", f"l{i}_dw_out_b", dst=into("e_w_out", j, r0=D))
            dya = _mm(dh1b, V(W["e_w_out"], (j,), rows=D), "nt", f"l{i}_dya")
            dyb = _mm(dh1b, V(W["e_w_out"], (j,), r0=D), "nt", f"l{i}_dyb")
            dac, dga, dba = _ln_silu_bwd(L["ac"], dya, W["e_ln_a_g"][j][None], W["e_ln_a_b"][j][None], f"l{i}_ln_a_bwd")
            G["e_ln_a_g"][j], G["e_ln_a_b"][j] = dga[0], dba[0]
            dal, dag, dwa, dbca = _conv_a_bwd(L["ua"], W["e_conv_a_w"][j], dac, f"l{i}_conv_a_bwd")
            G["e_conv_a_w"][j] = _fold_rows(dwa, CONV_A)
            G["e_conv_a_b"][j] = dbca.sum(0)
            dy, dz, dgn = _gate_rms_bwd(L["y"], L["z"], dyb, W["e_norm_b_g"][j][None], f"l{i}_gate_rms_bwd")
            G["e_norm_b_g"][j] = dgn[0]
            dxs, dbs, dcs, sq_col, cms_row, dda_col, dda_row, ddp = _ssd_bwd(L["xc"], *L["sm"], W["e_d_skip"][j], L["states"], dy,
                                                                             f"l{i}_ssd_bwd")
            draw, G["e_dt_bias"][j], G["e_a_log"][j] = _ssd_small_grads(L["udt"][:, :N_HEADS], W["e_dt_bias"][j], W["e_a_log"][j],
                                                                       sq_col, cms_row, dda_col, dda_row)
            G["e_d_skip"][j] = ddp[:, :, 0, :].sum(0).reshape(N_HEADS, HEAD_P).sum(1)
            dxu, dwb, dbcb = _conv_b_bwd(L["xu"], W["e_conv_b_w"][j], W["e_conv_b_b"][j][None], dxs, dbs, dcs, f"l{i}_conv_b_bwd")
            G["e_conv_b_w"][j] = _fold_rows(dwb, CONV_B)
            G["e_conv_b_b"][j] = dbcb.sum(0)
            dudt = _padcols(draw, LANES)
            w_in = L["w_in"]
            xb_l = L["xb"]
            for nm, dseg, c0 in (("al", dal, 0), ("ag", dag, D), ("z", dz, SEG_Z[0]), ("xbc", dxu, SEG_X[0]), ("dt", dudt, SEG_DT[0])):
                GB["e_w_in"] = _mm(xb_l, dseg, "tn", f"l{i}_dw_in_{nm}", dst=into("e_w_in", j, c0=c0))
            dx = _mm_sum([(dal, w_in(SEG_A, cols=D)), (dag, w_in(SEG_A, c0=D, cols=D)), (dz, w_in(SEG_Z)),
                          (V(dxu, cols=D), w_in(SEG_X, cols=D)), (V(dxu, c0=D), w_in(SEG_X, c0=D, cols=D)), (dudt, w_in(SEG_DT))],
                         "nt", f"l{i}_dx", add=dh1, add_scale=ALPHA)
        else:
            GB["o_w_out"] = _mm(L["sc"], dh1b, "tn", f"l{i}_dw_out", dst=into("o_w_out", j))
            dsc = _mm(dh1b, V(W["o_w_out"], (j,)), "nt", f"l{i}_dsc")
            dbg, dcg, dv, dwc = _conv_c_bwd(L["uo"], W["o_conv_w"][j], dsc, f"l{i}_conv_c_bwd")
            G["o_conv_w"][j] = _fold_rows(dwc, CONV_C)
            xb_l = L["xb"]
            for nm, dseg, c0 in (("bg", dbg, 0), ("cg", dcg, D), ("v", dv, 2 * D)):
                GB["o_w_in"] = _mm(xb_l, dseg, "tn", f"l{i}_dw_in_{nm}", dst=into("o_w_in", j, c0=c0))
            below = saved[i - 1]
            ln2_done = _mm_sum([(dseg, V(W["o_w_in"], (j,), c0=c0, cols=D)) for dseg, c0 in ((dbg, 0), (dcg, D), (dv, 2 * D))],
                               "nt", f"l{i}_dx", add=dh1, add_scale=ALPHA,
                               ln_bwd=(below["h2"], W["ln_g"][i - 1, 1][None], (below["pp"], below["gl"])))
        tok = comm.part_grads_done(i, MIX, GB) if comm is not None else None
    grads = {n: jnp.stack(v) for n, v in G.items()}
    return sq, dx, GB, grads


def _ssd_small_inputs(raw, dt_bias, a_log):
    T = raw.shape[0]
    a = -jnp.exp(a_log)
    rg = raw.reshape(T, N_GROUPS, 4)
    raw_col = jnp.pad(jnp.transpose(rg, (1, 0, 2)), ((0, 0), (0, 0), (0, LANES - 4)))
    raw_row = jnp.pad(jnp.transpose(rg, (1, 2, 0)), ((0, 0), (0, SUBLANES - 4), (0, 0)))

    def colv(v):
        return jnp.pad(v.reshape(N_GROUPS, 1, 4), ((0, 0), (0, 0), (0, LANES - 4)))

    def rowv(v):
        return jnp.pad(v.reshape(N_GROUPS, 4, 1), ((0, 0), (0, SUBLANES - 4), (0, 0)))

    return raw_col, raw_row, colv(dt_bias), rowv(dt_bias), colv(a), rowv(a)


def _ssd_small_grads(raw, dt_bias, a_log, sq_col, cms_row, dda_col, dda_row):
    T = raw.shape[0]

    def join(col, row):
        c = jnp.transpose(col[:, :, :4], (1, 0, 2)).reshape(T, N_HEADS)
        r = jnp.transpose(row[:, :4, :], (2, 0, 1)).reshape(T, N_HEADS)
        return c + r

    a = -jnp.exp(a_log)
    pre = raw + dt_bias
    dt = jax.nn.softplus(pre)
    dda = join(dda_col, dda_row)
    ddt = join(sq_col, cms_row) + a * dda
    draw = ddt * jax.nn.sigmoid(pre)
    da = jnp.sum(dt * dda, axis=0)
    return draw, jnp.sum(draw, axis=0), da * a


def kernel(x, p, e_w_in, e_conv_a_w, e_conv_a_b, e_ln_a_g, e_ln_a_b, e_conv_b_w, e_conv_b_b, e_dt_bias, e_a_log, e_d_skip, e_norm_b_g, e_w_out, o_w_in, o_conv_w, o_w_out, f_w_up, f_conv_w, f_conv_b, f_w_down, ple_w_proj, ple_w_gate, ln_g, ln_b, loss_target, m_e_w_in, m_e_conv_a_w, m_e_conv_a_b, m_e_ln_a_g, m_e_ln_a_b, m_e_conv_b_w, m_e_conv_b_b, m_e_dt_bias, m_e_a_log, m_e_d_skip, m_e_norm_b_g, m_e_w_out, m_o_w_in, m_o_conv_w, m_o_w_out, m_f_w_up, m_f_conv_w, m_f_conv_b, m_f_w_down, m_ple_w_proj, m_ple_w_gate, m_ln_g, m_ln_b, v_e_w_in, v_e_conv_a_w, v_e_conv_a_b, v_e_ln_a_g, v_e_ln_a_b, v_e_conv_b_w, v_e_conv_b_b, v_e_dt_bias, v_e_a_log, v_e_d_skip, v_e_norm_b_g, v_e_w_out, v_o_w_in, v_o_conv_w, v_o_w_out, v_f_w_up, v_f_conv_w, v_f_conv_b, v_f_w_down, v_ple_w_proj, v_ple_w_gate, v_ln_g, v_ln_b):
    args = dict(locals())
    w_shard = {n: args[n] for n in WEIGHT_ORDER}
    m_shard = {n: args["m_" + n] for n in WEIGHT_ORDER}
    v_shard = {n: args["v_" + n] for n in WEIGHT_ORDER}
    xi, yi, ci = lax.axis_index("x"), lax.axis_index("y"), lax.axis_index("c")
    chip = 2 * xi + yi

    mevec = jnp.stack([chip, ci]).astype(jnp.int32)
    small_shapes = [w_shard[n].shape for n, _ in SMALL_SHARDED]
    sr = _small_rows(small_shapes)
    ws = _pack_rows([w_shard[n] for n, _ in SMALL_SHARDED], LANES, sr, F32)
    W = {n: w_shard[n] for n in SMALL_REPL}
    W.update({s[0]: Layers(s[2]) for s in TENSORS})
    w_items = [_group_items(layers) for layers in W_GROUPS]
    g_items = [_group_items(layers) for layers in G_GROUPS]

    def install(items, fulls):
        for it, f in zip(items, fulls):
            if it[0] == "e_w_in":
                f = jnp.transpose(f.reshape(it[2], 4, D, E_IN // 4), (0, 2, 1, 3)).reshape(it[2], D, E_IN)
                f = jnp.pad(f, ((0, 0), (0, 0), (0, E_PAD - E_IN)))
            W[it[0]].put(f, it[6])

    casts = [[_cast_into(w_shard[it[0]], it, mevec[:1]) for it in items] for items in w_items]
    fulls, ssem, rsem, _ = _gather_start(casts[0], w_items[0], "gather_start_0")
    fulls = _gather_wait(fulls, ssem, rsem, [c for grp in casts[1:] for c in grp], w_items[0], "gather_wait_0")
    fulls, WS = _gather_fwd(fulls, w_items[0], "gather_fwd_0", ws)
    install(w_items[0], fulls)
    parts_s = [_unpack_rows(WS[k], small_shapes) for k in range(4)]
    for idx, (n, ax) in enumerate(SMALL_SHARDED):
        W[n] = jnp.concatenate([parts_s[k][idx] for k in range(4)], axis=ax)

    class Comm:
        sent = {}
        started = {}
        tail = fulls[0]

        def start_next(self, gi):
            if gi >= len(w_items):
                return None
            self.started[gi] = _gather_start(casts[gi], w_items[gi], f"gather_start_{gi}", self.tail)
            return self.started[gi][3]

        def part_starts(self, layer, part, after):
            if (layer, part) == W_GROUPS[0][0]:
                return self.start_next(1)
            for gi in range(1, len(W_GROUPS)):
                if W_GROUPS[gi][0] == (layer, part):
                    fulls, ssem, rsem, _ = self.started[gi]
                    fulls = _gather_wait(fulls, ssem, rsem, after, w_items[gi], f"gather_wait_{gi}")
                    fulls, _ = _gather_fwd(fulls, w_items[gi], f"gather_fwd_{gi}")
                    install(w_items[gi], fulls)
                    self.tail = fulls[0]
                    return self.start_next(gi + 1)
            return None

        swapping = None

        def swap_landed(self, after):
            if self.swapping is None:
                return None
            gi, thru, ssem, rsem = self.swapping
            items = g_items[gi]
            thru = _swap_wait(thru, ssem, rsem, after, items, f"swap_wait_{gi}")
            gs, ras = thru[:len(items)], thru[len(items):]
            ps = [_add_half(g, ra, it, mevec[1:]) for g, ra, it in zip(gs, ras, items)]
            thru, ssem, rsem, tok = _scatter_start(ps, items, f"scatter_start_{gi}")
            self.sent[gi] = (thru, ssem, rsem, tok)
            self.swapping = None
            return tok

        def part_grads_done(self, layer, part, GB):
            tok = self.swap_landed(GB.last)
            for gi, members in enumerate(G_GROUPS):
                if members[-1] == (layer, part):
                    items = g_items[gi]
                    gs = []
                    for it in items:
                        g = GB[(it[0], gi)]
                        if it[0] == "e_w_in":
                            g = jnp.transpose(g[:, :, :E_IN].reshape(it[2], D, 4, E_IN // 4), (0, 2, 1, 3)).reshape(it[2], 4 * D, E_IN // 4)
                        gs.append(g)
                    thru, ssem, rsem, tok = _swap_start(gs, items, f"swap_start_{gi}", tok)
                    self.swapping = (gi, thru, ssem, rsem)
            return tok

    comm = Comm()

    sq, dx, GB, G = _local_step(x[0], p, loss_target[0], W, comm)
    loss = lax.psum(0.5 * sq[0, 0] / D, ("x", "y", "c"))
    grad_x = dx[None]

    def shard_of(g, ax, k):
        n = g.shape[ax] // 4
        return lax.slice_in_dim(g, k * n, (k + 1) * n, axis=ax)

    reduced, updated = {}, {}
    comm.swap_landed(dx)
    after = comm.sent[len(g_items) - 1][3]
    for gi, items in enumerate(g_items):
        thru, ssem, rsem, _ = comm.sent[gi]
        thru = _scatter_wait(thru, ssem, rsem, after, items, f"scatter_wait_{gi}")
        ps, rcs = thru[:len(items)], thru[len(items):]
        rs = [_sum_own(pt, rc, it, mevec, reduced.get(it[0])) for pt, rc, it in zip(ps, rcs, items)]
        rs = _join_halves(rs, items, f"join_halves_{gi}")
        reduced.update({it[0]: r for it, r in zip(items, rs)})
        for it in items:
            n = it[0]
            if n != "e_w_in":
                updated[n] = _adamw_layers(w_shard[n], reduced[n], m_shard[n], v_shard[n], it[6], it[2], updated.get(n),
                                           f"adamw_{n}_{it[6]}")
        after = updated[items[-1][0]][0]
    *updated["e_w_in"], reduced["e_w_in"] = _adamw_minor_rows(w_shard["e_w_in"], reduced["e_w_in"], m_shard["e_w_in"],
                                                              v_shard["e_w_in"], "adamw_e_w_in")

    small_all = ([shard_of(G[n], ax, k) for k in range(4) for n, ax in SMALL_SHARDED] + [G[n] for n in SMALL_REPL])
    small_all_shapes = [t.shape for t in small_all]
    mr = _small_rows(small_all_shapes)
    sg = _sum8(_allgather_small(_pack_rows(small_all, LANES, mr, F32)), mr)
    sparts = _unpack_rows(sg, small_all_shapes)
    ns = len(SMALL_SHARDED)
    gsmall = {}
    for idx, (n, ax) in enumerate(SMALL_SHARDED):
        stacked = jnp.stack([sparts[k * ns + idx] for k in range(4)])
        gsmall[n] = lax.dynamic_index_in_dim(stacked, chip, axis=0, keepdims=False)
    for idx, n in enumerate(SMALL_REPL):
        gsmall[n] = sparts[4 * ns + idx]

    grads, deltas, new_m, new_v = [], [], [], []
    for n in WEIGHT_ORDER:
        if n in reduced:
            g, (d, mn, vn) = reduced[n], updated[n]
        else:
            g = gsmall[n]
            d, mn, vn = _adamw(w_shard[n], g, m_shard[n], v_shard[n], f"adamw_{n}")
        grads.append(g)
        deltas.append(d)
        new_m.append(mn)
        new_v.append(vn)
    return (loss, grad_x, *grads, *deltas, *new_m, *new_v)
```

```python
import functools
import math

import jax
import jax.numpy as jnp
from jax import lax
from jax.experimental import pallas as pl
from jax.experimental.pallas import tpu as pltpu

F32 = jnp.float32
BF16 = jnp.bfloat16
MESH = pl.DeviceIdType.MESH

DEPTH = 4
ALPHA = (2.0 * DEPTH) ** 0.25
LN_EPS = 1e-5
D = 1024
HEAD_P = 64
N_STATE = 128
N_HEADS = 16
N_GROUPS = 4
CONV_A, CONV_B, CONV_C, CONV_F = 31, 4, 3, 3
D_FF = 2816
PLE = 256
E_IN = 5136

ADAM_LR, ADAM_B1, ADAM_B2, ADAM_EPS, ADAM_WD, ADAM_STEP = 0.001, 0.9, 0.999, 1e-08, 0.01, 10

LANES = 128
SUBLANES = 8
VMEM_LIMIT = 56 * 1024 * 1024
SSD_Q = 128
CONV_R = 128
CONV_PAD = 32
ROW_T = 256


def _cparams(sem=None):
    return pltpu.CompilerParams(dimension_semantics=sem, vmem_limit_bytes=VMEM_LIMIT)


def _sig(v):
    return jax.nn.sigmoid(v)


_DIMS = {"nn": (((1,), (0,)), ((), ())), "nt": (((1,), (1,)), ((), ())), "tn": (((0,), (0,)), ((), ()))}


class Layers:
    def __init__(self, n_layers):
        self.where = [None] * n_layers

    def put(self, arr, l0):
        for k in range(arr.shape[0]):
            self.where[l0 + k] = (arr, k)


class V:
    def __init__(self, arr, lead=(), r0=0, c0=0, rows=None, cols=None):
        if isinstance(arr, Layers):
            arr, k = arr.where[lead[0]]
            lead = (k,) + tuple(lead[1:])
        self.arr, self.lead, self.r0, self.c0 = arr, tuple(lead), r0, c0
        R, C = arr.shape[-2:]
        self.rows = R - r0 if rows is None else rows
        self.cols = C - c0 if cols is None else cols

    def spec(self, br, bc, fn):
        assert self.r0 % br == 0 and self.c0 % bc == 0, (self.r0, self.c0, br, bc)
        ro, co, lead = self.r0 // br, self.c0 // bc, self.lead

        def index(i, j, k):
            r, c = fn(i, j, k)
            return lead + (r + ro, c + co)

        return pl.BlockSpec((None,) * len(lead) + (br, bc), index)


def _v(t):
    return t if isinstance(t, V) else V(t)


def _tile(n, offs, cands):
    for c in cands:
        if n % c == 0 and all(o % c == 0 for o in offs):
            return c
    raise ValueError((n, offs))


_TILES = (1024, 1408, 512, 256, 128)


def _mm(a, b, mode, name, out_dtype=F32, add=None, add_scale=1.0, dst=None, after=None):
    a, b = _v(a), _v(b)
    add = _v(add) if add is not None else None
    if mode == "nn":
        M, K, K2, N = a.rows, a.cols, b.rows, b.cols
        am, ak, bk, bn = a.r0, a.c0, b.r0, b.c0
    elif mode == "nt":
        M, K, N, K2 = a.rows, a.cols, b.rows, b.cols
        am, ak, bn, bk = a.r0, a.c0, b.r0, b.c0
    else:
        K, M, K2, N = a.rows, a.cols, b.rows, b.cols
        ak, am, bk, bn = a.r0, a.c0, b.r0, b.c0
    assert K == K2, (name, mode, M, K, K2, N)
    if dst is None:
        buf, full_shape, o_lead, o_r0, o_c0 = None, (M, N), (), 0, 0
    else:
        buf, full_shape, o_lead, o_r0, o_c0 = dst
    tm = _tile(M, [am, o_r0] + ([add.r0] if add else []), _TILES)
    tn = _tile(N, [bn, o_c0] + ([add.c0] if add else []), _TILES)
    narrow = a.arr.dtype.itemsize == 2 and b.arr.dtype.itemsize == 2
    tk = _tile(K, [ak, bk], ((2048,) if narrow else ()) + _TILES)
    nk = K // tk
    has_add, has_buf, has_after = add is not None, buf is not None, after is not None

    def body(*refs):
        a_ref, b_ref = refs[0], refs[1]
        add_ref = refs[2] if has_add else None
        o_ref = refs[2 + has_add + has_buf + has_after]

        def finish(r):
            if has_add:
                r = r + add_scale * add_ref[...].astype(F32)
            o_ref[...] = r.astype(o_ref.dtype)

        part = lax.dot_general(a_ref[...].astype(BF16), b_ref[...].astype(BF16), _DIMS[mode], preferred_element_type=F32)
        if nk == 1:
            finish(part)
        else:
            acc_ref = refs[-1]
            k = pl.program_id(2)

            @pl.when(k == 0)
            def _():
                acc_ref[...] = part

            @pl.when(jnp.logical_and(k > 0, k < nk - 1))
            def _():
                acc_ref[...] += part

            @pl.when(k == nk - 1)
            def _():
                finish(acc_ref[...] + part)

    if mode == "tn":
        a_spec = a.spec(tk, tm, lambda i, j, k: (k, i))
    else:
        a_spec = a.spec(tm, tk, lambda i, j, k: (i, k))
    if mode == "nt":
        b_spec = b.spec(tn, tk, lambda i, j, k: (j, k))
    else:
        b_spec = b.spec(tk, tn, lambda i, j, k: (k, j))
    in_specs, args = [a_spec, b_spec], [a.arr, b.arr]
    if has_add:
        in_specs.append(add.spec(tm, tn, lambda i, j, k: (i, j)))
        args.append(add.arr)
    aliases = {}
    if has_buf:
        aliases = {len(args): 0}
        in_specs.append(pl.BlockSpec(memory_space=pl.ANY))
        args.append(buf)
        out_dtype = buf.dtype
    if has_after:
        in_specs.append(pl.BlockSpec(memory_space=pl.ANY))
        args.append(after)
    o_view = V(jax.ShapeDtypeStruct(full_shape, out_dtype), o_lead, o_r0, o_c0, M, N)
    return pl.pallas_call(
        body, name=name, grid=(M // tm, N // tn, nk), in_specs=in_specs, out_specs=o_view.spec(tm, tn, lambda i, j, k: (i, j)),
        out_shape=jax.ShapeDtypeStruct(full_shape, out_dtype), input_output_aliases=aliases,
        scratch_shapes=[pltpu.VMEM((tm, tn), F32)] if nk > 1 else [],
        compiler_params=_cparams(("parallel", "parallel", "arbitrary")))(*args)


def _ln_stats(h):
    mu = jnp.mean(h, axis=-1, keepdims=True)
    hc = h - mu
    var = jnp.mean(hc * hc, axis=-1, keepdims=True)
    rstd = lax.rsqrt(var + LN_EPS)
    return hc * rstd, rstd


def _ln_bwd_math(dyv, h, g):
    xhat, rstd = _ln_stats(h)
    dxh = dyv * g
    dh = rstd * (dxh - jnp.mean(dxh, axis=-1, keepdims=True) - xhat * jnp.mean(dxh * xhat, axis=-1, keepdims=True))
    return dh, jnp.sum(dyv * xhat, axis=0, keepdims=True), jnp.sum(dyv, axis=0, keepdims=True)


def _mm_sum(pairs, mode, name, out_dtype=F32, add=None, add_scale=1.0, after=None, ln_fwd=None, ln_bwd=None):
    pairs = [(_v(a), _v(b)) for a, b in pairs]
    add = _v(add) if add is not None else None
    M = pairs[0][0].rows
    N = pairs[0][1].cols if mode == "nn" else pairs[0][1].rows
    b_offs = [(b.c0 if mode == "nn" else b.r0) for _, b in pairs]
    fused = ln_fwd is not None or ln_bwd is not None
    tm = _tile(M, [a.r0 for a, _ in pairs] + ([add.r0] if add else []), (256, 128) if fused else (512, 256, 128))
    tn = N if fused else _tile(N, b_offs + ([add.c0] if add else []), (512, 256, 128))
    assert not fused or (N == D and all(o == 0 for o in b_offs))
    n_p, has_add, has_after = len(pairs), add is not None, after is not None
    ple = (ln_fwd[1] if ln_fwd is not None else ln_bwd[2]) if fused else None
    has_ple = ple is not None

    def body(*refs):
        acc = None
        for i in range(n_p):
            part = lax.dot_general(refs[2 * i][...].astype(BF16), refs[2 * i + 1][...].astype(BF16), _DIMS[mode],
                                   preferred_element_type=F32)
            acc = part if acc is None else acc + part
        pos = 2 * n_p
        if has_add:
            acc = acc + add_scale * refs[pos][...].astype(F32)
            pos += 1
        if ln_fwd is not None:
            x_ref = refs[pos]
            pp_ref, gl_ref = (refs[pos + 1], refs[pos + 2]) if has_ple else (None, None)
            pos += 1 + 2 * has_ple
            g_ref, b_ref = refs[pos], refs[pos + 1]
            h_ref, y_ref, yb_ref = refs[pos + 2 + has_after:]
            h = ALPHA * x_ref[...] + acc
            if has_ple:
                h = h + pp_ref[...] * _sig(gl_ref[...])
            xhat, _ = _ln_stats(h)
            y = xhat * g_ref[...] + b_ref[...]
            h_ref[...] = h
            y_ref[...] = y
            yb_ref[...] = y.astype(BF16)
        elif ln_bwd is not None:
            h_ref, g_ref = refs[pos], refs[pos + 1]
            pp_ref, gl_ref = (refs[pos + 2], refs[pos + 3]) if has_ple else (None, None)
            outs = refs[pos + 2 + 2 * has_ple + has_after:]
            dh_ref, dhb_ref, dg_ref, db_ref = outs[:4]

            @pl.when(pl.program_id(0) == 0)
            def _():
                dg_ref[...] = jnp.zeros_like(dg_ref)
                db_ref[...] = jnp.zeros_like(db_ref)

            dh, dg, db = _ln_bwd_math(acc, h_ref[...], g_ref[...])
            dg_ref[...] += dg
            db_ref[...] += db
            dh_ref[...] = dh
            dhb_ref[...] = dh.astype(BF16)
            if has_ple:
                s = _sig(gl_ref[...])
                outs[4][...] = (dh * s).astype(BF16)
                outs[5][...] = (dh * pp_ref[...] * s * (1.0 - s)).astype(BF16)
        else:
            o_ref = refs[pos + has_after]
            o_ref[...] = acc.astype(o_ref.dtype)

    in_specs, args = [], []
    for a, b in pairs:
        K = a.cols
        assert K == (b.rows if mode == "nn" else b.cols), (name, K)
        in_specs.append(a.spec(tm, K, lambda i, j, k: (i, 0)))
        in_specs.append(b.spec(K, tn, lambda i, j, k: (0, j)) if mode == "nn" else b.spec(tn, K, lambda i, j, k: (j, 0)))
        args += [a.arr, b.arr]
    if has_add:
        in_specs.append(add.spec(tm, tn, lambda i, j, k: (i, j)))
        args.append(add.arr)
    row = pl.BlockSpec((tm, tn), lambda i, j, k: (i, j))
    vec = pl.BlockSpec((1, tn), lambda i, j, k: (0, 0))
    if ln_fwd is not None:
        x, _, g, b = ln_fwd
        extra = [x] + (list(ple) if has_ple else []) + [g, b]
        in_specs += [row] * (1 + 2 * has_ple) + [vec, vec]
        args += extra
        out_specs = (row, row, row)
        out_shape = (_sds((M, N)), _sds((M, N)), _sds((M, N), BF16))
    elif ln_bwd is not None:
        h, g, _ = ln_bwd
        in_specs += [row, vec] + [row] * (2 * has_ple)
        args += [h, g] + (list(ple) if has_ple else [])
        out_specs = (row, row, vec, vec) + ((row, row) if has_ple else ())
        out_shape = (_sds((M, N)), _sds((M, N), BF16), _sds((1, N)), _sds((1, N))) + ((_sds((M, N), BF16),) * 2 if has_ple else ())
    else:
        out_specs, out_shape = row, jax.ShapeDtypeStruct((M, N), out_dtype)
    if has_after:
        in_specs.append(pl.BlockSpec(memory_space=pl.ANY))
        args.append(after)
    return pl.pallas_call(
        body, name=name, grid=(M // tm, N // tn, 1), in_specs=in_specs, out_specs=out_specs, out_shape=out_shape,
        compiler_params=_cparams(("arbitrary",) * 3 if ln_bwd is not None else ("parallel", "parallel", "arbitrary")))(*args)


def _rows(T, width=D):
    return pl.BlockSpec((ROW_T, width), lambda i: (i, 0))


def _vec(width=D):
    return pl.BlockSpec((1, width), lambda i: (0, 0))


def _res_ln_fwd(x, adds, ple, g, b, name):
    T = x.shape[0]
    n_add = len(adds)
    has_ple = ple is not None

    def body(*refs):
        x_ref = refs[0]
        add_refs = refs[1:1 + n_add]
        pos = 1 + n_add
        if has_ple:
            pp_ref, gl_ref = refs[pos], refs[pos + 1]
            pos += 2
        g_ref, b_ref, h_ref, y_ref, yb_ref = refs[pos:pos + 5]
        h = ALPHA * x_ref[...]
        for r in add_refs:
            h = h + r[...]
        if has_ple:
            h = h + pp_ref[...] * _sig(gl_ref[...])
        xhat, _ = _ln_stats(h)
        y = xhat * g_ref[...] + b_ref[...]
        h_ref[...] = h
        y_ref[...] = y
        yb_ref[...] = y.astype(BF16)

    n_in = 1 + n_add + (2 if has_ple else 0)
    args = (x,) + tuple(adds) + (tuple(ple) if has_ple else ()) + (g, b)
    return pl.pallas_call(
        body, name=name, grid=(T // ROW_T,), in_specs=[_rows(T)] * n_in + [_vec(), _vec()],
        out_specs=(_rows(T), _rows(T), _rows(T)),
        out_shape=(jax.ShapeDtypeStruct((T, D), F32), jax.ShapeDtypeStruct((T, D), F32), jax.ShapeDtypeStruct((T, D), BF16)),
        compiler_params=_cparams(("parallel",)))(*args)


def _res_ln_bwd(dy, h, g, ple, name):
    T = dy.shape[0]
    has_ple = ple is not None

    def body(*refs):
        if has_ple:
            dy_ref, h_ref, g_ref, pp_ref, gl_ref, dh_ref, dhb_ref, dg_ref, db_ref, dpp_ref, dgl_ref = refs
        else:
            dy_ref, h_ref, g_ref, dh_ref, dhb_ref, dg_ref, db_ref = refs
        i = pl.program_id(0)

        @pl.when(i == 0)
        def _():
            dg_ref[...] = jnp.zeros_like(dg_ref)
            db_ref[...] = jnp.zeros_like(db_ref)

        dyv = dy_ref[...]
        xhat, rstd = _ln_stats(h_ref[...])
        dg_ref[...] += jnp.sum(dyv * xhat, axis=0, keepdims=True)
        db_ref[...] += jnp.sum(dyv, axis=0, keepdims=True)
        dxh = dyv * g_ref[...]
        dh = rstd * (dxh - jnp.mean(dxh, axis=-1, keepdims=True) - xhat * jnp.mean(dxh * xhat, axis=-1, keepdims=True))
        dh_ref[...] = dh
        dhb_ref[...] = dh.astype(BF16)
        if has_ple:
            s = _sig(gl_ref[...])
            dpp_ref[...] = (dh * s).astype(BF16)
            dgl_ref[...] = (dh * pp_ref[...] * s * (1.0 - s)).astype(BF16)

    args = (dy, h, g) + (tuple(ple) if has_ple else ())
    in_specs = [_rows(T), _rows(T), _vec()] + ([_rows(T), _rows(T)] if has_ple else [])
    out_specs = [_rows(T), _rows(T), _vec(), _vec()] + ([_rows(T), _rows(T)] if has_ple else [])
    out_shape = [jax.ShapeDtypeStruct((T, D), F32), jax.ShapeDtypeStruct((T, D), BF16),
                 jax.ShapeDtypeStruct((1, D), F32), jax.ShapeDtypeStruct((1, D), F32)]
    if has_ple:
        out_shape += [jax.ShapeDtypeStruct((T, D), BF16), jax.ShapeDtypeStruct((T, D), BF16)]
    return pl.pallas_call(
        body, name=name, grid=(T // ROW_T,), in_specs=in_specs, out_specs=tuple(out_specs), out_shape=tuple(out_shape),
        compiler_params=_cparams(("arbitrary",)))(*args)


def _ln_silu_fwd(ac, g, b, name):
    T = ac.shape[0]

    def body(a_ref, g_ref, b_ref, o_ref):
        xhat, _ = _ln_stats(a_ref[...])
        ln = xhat * g_ref[...] + b_ref[...]
        o_ref[...] = (ln * _sig(ln)).astype(BF16)

    return pl.pallas_call(
        body, name=name, grid=(T // ROW_T,), in_specs=[_rows(T), _vec(), _vec()], out_specs=_rows(T),
        out_shape=jax.ShapeDtypeStruct((T, D), BF16), compiler_params=_cparams(("parallel",)))(ac, g, b)


def _ln_silu_bwd(ac, dya, g, b, name):
    T = ac.shape[0]

    def body(a_ref, d_ref, g_ref, b_ref, da_ref, dg_ref, db_ref):
        i = pl.program_id(0)

        @pl.when(i == 0)
        def _():
            dg_ref[...] = jnp.zeros_like(dg_ref)
            db_ref[...] = jnp.zeros_like(db_ref)

        xhat, rstd = _ln_stats(a_ref[...])
        ln = xhat * g_ref[...] + b_ref[...]
        s = _sig(ln)
        dln = d_ref[...] * s * (1.0 + ln * (1.0 - s))
        dg_ref[...] += jnp.sum(dln * xhat, axis=0, keepdims=True)
        db_ref[...] += jnp.sum(dln, axis=0, keepdims=True)
        dxh = dln * g_ref[...]
        da_ref[...] = rstd * (dxh - jnp.mean(dxh, axis=-1, keepdims=True)
                              - xhat * jnp.mean(dxh * xhat, axis=-1, keepdims=True))

    return pl.pallas_call(
        body, name=name, grid=(T // ROW_T,), in_specs=[_rows(T), _rows(T), _vec(), _vec()],
        out_specs=(_rows(T), _vec(), _vec()),
        out_shape=(jax.ShapeDtypeStruct((T, D), F32), jax.ShapeDtypeStruct((1, D), F32), jax.ShapeDtypeStruct((1, D), F32)),
        compiler_params=_cparams(("arbitrary",)))(ac, dya, g, b)


def _gate_rms_fwd(y, z, g, name):
    T = y.shape[0]

    def body(y_ref, z_ref, g_ref, o_ref):
        zv = z_ref[...]
        yg = y_ref[...] * (zv * _sig(zv))
        r = lax.rsqrt(jnp.mean(yg * yg, axis=-1, keepdims=True) + LN_EPS)
        o_ref[...] = (yg * r * g_ref[...]).astype(BF16)

    return pl.pallas_call(
        body, name=name, grid=(T // ROW_T,), in_specs=[_rows(T), _rows(T), _vec()], out_specs=_rows(T),
        out_shape=jax.ShapeDtypeStruct((T, D), BF16), compiler_params=_cparams(("parallel",)))(y, z, g)


def _gate_rms_bwd(y, z, dout, g, name):
    T = y.shape[0]

    def body(y_ref, z_ref, d_ref, g_ref, dy_ref, dz_ref, dg_ref):
        i = pl.program_id(0)

        @pl.when(i == 0)
        def _():
            dg_ref[...] = jnp.zeros_like(dg_ref)

        yv, zv, dv = y_ref[...], z_ref[...], d_ref[...]
        s = _sig(zv)
        sz = zv * s
        yg = yv * sz
        r = lax.rsqrt(jnp.mean(yg * yg, axis=-1, keepdims=True) + LN_EPS)
        dg_ref[...] += jnp.sum(dv * yg * r, axis=0, keepdims=True)
        dn = dv * g_ref[...]
        dyg = r * dn - yg * (r * r * r) * jnp.mean(dn * yg, axis=-1, keepdims=True)
        dy_ref[...] = dyg * sz
        dz_ref[...] = (dyg * yv * s * (1.0 + zv * (1.0 - s))).astype(BF16)

    return pl.pallas_call(
        body, name=name, grid=(T // ROW_T,), in_specs=[_rows(T), _rows(T), _rows(T), _vec()],
        out_specs=(_rows(T), _rows(T), _vec()),
        out_shape=(jax.ShapeDtypeStruct((T, D), F32), jax.ShapeDtypeStruct((T, D), BF16), jax.ShapeDtypeStruct((1, D), F32)),
        compiler_params=_cparams(("arbitrary",)))(y, z, dout, g)


def _to_bf16(x, name):
    T = x.shape[0]

    def body(x_ref, o_ref):
        o_ref[...] = x_ref[...].astype(BF16)

    return pl.pallas_call(body, name=name, grid=(T // ROW_T,), in_specs=[_rows(T)], out_specs=_rows(T),
                          out_shape=jax.ShapeDtypeStruct((T, D), BF16), compiler_params=_cparams(("parallel",)))(x)


def _loss_head(y, target, name):
    T = y.shape[0]

    def body(y_ref, t_ref, s_ref, d_ref):
        i = pl.program_id(0)

        @pl.when(i == 0)
        def _():
            s_ref[...] = jnp.zeros_like(s_ref)

        err = y_ref[...] - t_ref[...]
        s_ref[...] += jnp.sum(jnp.sum(err * err, axis=1, keepdims=True), axis=0, keepdims=True)
        d_ref[...] = err * (1.0 / D)

    return pl.pallas_call(
        body, name=name, grid=(T // ROW_T,), in_specs=[_rows(T), _rows(T)],
        out_specs=(pl.BlockSpec((SUBLANES, LANES), lambda i: (0, 0)), _rows(T)),
        out_shape=(jax.ShapeDtypeStruct((SUBLANES, LANES), F32), jax.ShapeDtypeStruct((T, D), F32)),
        compiler_params=_cparams(("arbitrary",)))(y, target)


def _taps_fwd(pad_ref, w_ref, K, base):
    off = CONV_PAD - (K - 1)
    acc = w_ref[0:1, :] * pad_ref[pl.ds(base + off, CONV_R), :]
    for k in range(1, K):
        acc = acc + w_ref[k:k + 1, :] * pad_ref[pl.ds(base + off + k, CONV_R), :]
    return acc


def _taps_bwd(padd_ref, w_ref, K, base):
    acc = w_ref[0:1, :] * padd_ref[pl.ds(base + (K - 1), CONV_R), :]
    for k in range(1, K):
        acc = acc + w_ref[k:k + 1, :] * padd_ref[pl.ds(base + (K - 1) - k, CONV_R), :]
    return acc


def _f32(ref, rows):
    return ref[rows, :].astype(F32)


def _fold8(v):
    return v.reshape(CONV_R // SUBLANES, SUBLANES, v.shape[-1]).sum(0)


def _wgrad_acc(dw_ref, pad_ref, d, K, base):
    off = CONV_PAD - (K - 1)
    for k in range(K):
        dw_ref[k * SUBLANES:(k + 1) * SUBLANES, :] += _fold8(d * pad_ref[pl.ds(base + off + k, CONV_R), :])


def _loop_rows(T, fn):
    def step(r, carry):
        fn(pl.multiple_of(r * CONV_R, CONV_R))
        return carry
    lax.fori_loop(0, T // CONV_R, step, 0)


def _col(T, off_blocks=0, rows=None):
    return pl.BlockSpec((T if rows is None else rows, LANES), lambda j: (0, j + off_blocks))


def _conv_call(body, name, T, n_tiles, in_specs, out_specs, out_shape, n_pad, n_padd=0):
    scratch = [pltpu.VMEM((T + CONV_PAD, LANES), F32)] * (n_pad + n_padd)
    return pl.pallas_call(body, name=name, grid=(n_tiles,), in_specs=in_specs, out_specs=out_specs, out_shape=out_shape,
                          scratch_shapes=scratch, compiler_params=_cparams(("parallel",)))


def _zero_head(ref):
    ref[0:CONV_PAD, :] = jnp.zeros((CONV_PAD, LANES), F32)


def _zero_tail(ref, T):
    ref[T:T + CONV_PAD, :] = jnp.zeros((CONV_PAD, LANES), F32)


def _sds(shape, dtype=F32):
    return jax.ShapeDtypeStruct(shape, dtype)


def _conv_a_fwd(ua, w, b, name):
    T = ua.shape[0]
    K, nt = CONV_A, D // LANES

    def body(al_ref, ag_ref, w_ref, b_ref, o_ref, pad_ref):
        _zero_head(pad_ref)

        def pre(base):
            rows = pl.ds(base, CONV_R)
            pad_ref[pl.ds(base + CONV_PAD, CONV_R), :] = _f32(al_ref, rows) * _sig(_f32(ag_ref, rows))
        _loop_rows(T, pre)

        def main(base):
            o_ref[pl.ds(base, CONV_R), :] = _taps_fwd(pad_ref, w_ref, K, base) + b_ref[...]
        _loop_rows(T, main)

    return _conv_call(body, name, T, nt, [_col(T), _col(T, nt), _col(T, rows=K), _col(T, rows=1)], _col(T),
                      _sds((T, D)), 1)(ua, ua, w, b)


def _conv_a_bwd(ua, w, dac, name):
    T = ua.shape[0]
    K, nt = CONV_A, D // LANES

    def body(al_ref, ag_ref, w_ref, d_ref, dal_ref, dag_ref, dw_ref, db_ref, pad_ref, padd_ref):
        _zero_head(pad_ref)
        _zero_tail(padd_ref, T)
        dw_ref[...] = jnp.zeros_like(dw_ref)
        db_ref[...] = jnp.zeros_like(db_ref)

        def pre(base):
            rows = pl.ds(base, CONV_R)
            pad_ref[pl.ds(base + CONV_PAD, CONV_R), :] = _f32(al_ref, rows) * _sig(_f32(ag_ref, rows))
            padd_ref[rows, :] = d_ref[rows, :]
        _loop_rows(T, pre)

        def main(base):
            rows = pl.ds(base, CONV_R)
            d = d_ref[rows, :]
            _wgrad_acc(dw_ref, pad_ref, d, K, base)
            db_ref[...] += _fold8(d)
            da = _taps_bwd(padd_ref, w_ref, K, base)
            al, s = _f32(al_ref, rows), _sig(_f32(ag_ref, rows))
            dal_ref[rows, :] = (da * s).astype(BF16)
            dag_ref[rows, :] = (da * al * s * (1.0 - s)).astype(BF16)
        _loop_rows(T, main)

    return _conv_call(body, name, T, nt, [_col(T), _col(T, nt), _col(T, rows=K), _col(T)],
                      (_col(T), _col(T), _col(T, rows=K * SUBLANES), _col(T, rows=SUBLANES)),
                      (_sds((T, D), BF16), _sds((T, D), BF16), _sds((K * SUBLANES, D)), _sds((SUBLANES, D))), 1, 1)(ua, ua, w, dac)


def _conv_b_fwd(xu, w, b, name):
    T, C = xu.shape
    K, nt = CONV_B, C // LANES

    def body(x_ref, w_ref, b_ref, o_ref, pad_ref):
        _zero_head(pad_ref)
        pad_ref[CONV_PAD:CONV_PAD + T, :] = x_ref[...].astype(F32)

        def main(base):
            hc = _taps_fwd(pad_ref, w_ref, K, base) + b_ref[...]
            o_ref[pl.ds(base, CONV_R), :] = hc * _sig(hc)
        _loop_rows(T, main)

    return _conv_call(body, name, T, nt, [_col(T), _col(T, rows=K), _col(T, rows=1)], _col(T), _sds((T, C)), 1)(xu, w, b)


def _conv_b_bwd(xu, w, b, dxs, dbs, dcs, name):
    T, C = xu.shape
    K, nt = CONV_B, C // LANES
    nx, nb = dxs.shape[1] // LANES, dbs.shape[1] // LANES

    def body(x_ref, w_ref, b_ref, d1_ref, d2_ref, d3_ref, dx_ref, dw_ref, db_ref, pad_ref, padd_ref):
        j = pl.program_id(0)
        _zero_head(pad_ref)
        _zero_tail(padd_ref, T)
        dw_ref[...] = jnp.zeros_like(dw_ref)
        db_ref[...] = jnp.zeros_like(db_ref)
        pad_ref[CONV_PAD:CONV_PAD + T, :] = x_ref[...].astype(F32)

        def pre(base):
            rows = pl.ds(base, CONV_R)
            hc = _taps_fwd(pad_ref, w_ref, K, base) + b_ref[...]
            s = _sig(hc)
            d = jnp.where(j < nx, d1_ref[rows, :], jnp.where(j < nx + nb, d2_ref[rows, :], d3_ref[rows, :]))
            padd_ref[rows, :] = d * s * (1.0 + hc * (1.0 - s))
        _loop_rows(T, pre)

        def main(base):
            d = padd_ref[pl.ds(base, CONV_R), :]
            _wgrad_acc(dw_ref, pad_ref, d, K, base)
            db_ref[...] += _fold8(d)
            dx_ref[pl.ds(base, CONV_R), :] = _taps_bwd(padd_ref, w_ref, K, base).astype(BF16)
        _loop_rows(T, main)

    def piece(lo, n):
        return pl.BlockSpec((T, LANES), lambda j: (0, jnp.clip(j - lo, 0, n - 1)))

    return _conv_call(body, name, T, nt,
                      [_col(T), _col(T, rows=K), _col(T, rows=1), piece(0, nx), piece(nx, nb), piece(nx + nb, nt - nx - nb)],
                      (_col(T), _col(T, rows=K * SUBLANES), _col(T, rows=SUBLANES)),
                      (_sds((T, C), BF16), _sds((K * SUBLANES, C)), _sds((SUBLANES, C))), 1, 1)(xu, w, b, dxs, dbs, dcs)


def _conv_c_fwd(uo, w, name):
    T = uo.shape[0]
    K, nt = CONV_C, D // LANES

    def body(bg_ref, cg_ref, v_ref, w_ref, o_ref, pad_ref):
        _zero_head(pad_ref)
        pad_ref[CONV_PAD:CONV_PAD + T, :] = cg_ref[...].astype(F32) * v_ref[...].astype(F32)

        def main(base):
            rows = pl.ds(base, CONV_R)
            o_ref[rows, :] = (_f32(bg_ref, rows) * _taps_fwd(pad_ref, w_ref, K, base)).astype(BF16)
        _loop_rows(T, main)

    return _conv_call(body, name, T, nt, [_col(T), _col(T, nt), _col(T, 2 * nt), _col(T, rows=K)], _col(T),
                      _sds((T, D), BF16), 1)(uo, uo, uo, w)


def _conv_c_bwd(uo, w, dsc, name):
    T = uo.shape[0]
    K, nt = CONV_C, D // LANES

    def body(bg_ref, cg_ref, v_ref, w_ref, d_ref, dbg_ref, dcg_ref, dv_ref, dw_ref, pad_ref, padd_ref):
        _zero_head(pad_ref)
        _zero_tail(padd_ref, T)
        dw_ref[...] = jnp.zeros_like(dw_ref)
        pad_ref[CONV_PAD:CONV_PAD + T, :] = cg_ref[...].astype(F32) * v_ref[...].astype(F32)

        def pre(base):
            rows = pl.ds(base, CONV_R)
            d = d_ref[rows, :]
            dbg_ref[rows, :] = (d * _taps_fwd(pad_ref, w_ref, K, base)).astype(BF16)
            padd_ref[rows, :] = d * _f32(bg_ref, rows)
        _loop_rows(T, pre)

        def main(base):
            rows = pl.ds(base, CONV_R)
            _wgrad_acc(dw_ref, pad_ref, padd_ref[rows, :], K, base)
            dq = _taps_bwd(padd_ref, w_ref, K, base)
            dcg_ref[rows, :] = (dq * _f32(v_ref, rows)).astype(BF16)
            dv_ref[rows, :] = (dq * _f32(cg_ref, rows)).astype(BF16)
        _loop_rows(T, main)

    return _conv_call(body, name, T, nt, [_col(T), _col(T, nt), _col(T, 2 * nt), _col(T, rows=K), _col(T)],
                      (_col(T), _col(T), _col(T), _col(T, rows=K * SUBLANES)),
                      (_sds((T, D), BF16), _sds((T, D), BF16), _sds((T, D), BF16), _sds((K * SUBLANES, D))), 1, 1)(uo, uo, uo, w, dsc)


def _conv_f_fwd(up, w, b, name):
    T = up.shape[0]
    K, nt = CONV_F, D_FF // LANES

    def body(u1_ref, u2_ref, w1_ref, w2_ref, b1_ref, b2_ref, o_ref, pad1_ref, pad2_ref):
        _zero_head(pad1_ref)
        _zero_head(pad2_ref)
        pad1_ref[CONV_PAD:CONV_PAD + T, :] = u1_ref[...].astype(F32)
        pad2_ref[CONV_PAD:CONV_PAD + T, :] = u2_ref[...].astype(F32)

        def main(base):
            h1 = _taps_fwd(pad1_ref, w1_ref, K, base) + b1_ref[...]
            h2 = _taps_fwd(pad2_ref, w2_ref, K, base) + b2_ref[...]
            o_ref[pl.ds(base, CONV_R), :] = (h1 * _sig(h1) * h2).astype(BF16)
        _loop_rows(T, main)

    return _conv_call(body, name, T, nt,
                      [_col(T), _col(T, nt), _col(T, rows=K), _col(T, nt, rows=K), _col(T, rows=1), _col(T, nt, rows=1)],
                      _col(T), _sds((T, D_FF), BF16), 2)(up, up, w, w, b, b)


def _conv_f_bwd(up, w, b, dact, name):
    T = up.shape[0]
    K, nt = CONV_F, D_FF // LANES

    def body(u1_ref, u2_ref, w1_ref, w2_ref, b1_ref, b2_ref, d_ref, du1_ref, du2_ref, dw1_ref, dw2_ref, db1_ref, db2_ref,
             pad1_ref, pad2_ref, padd1_ref, padd2_ref):
        _zero_head(pad1_ref)
        _zero_head(pad2_ref)
        _zero_tail(padd1_ref, T)
        _zero_tail(padd2_ref, T)
        for r in (dw1_ref, dw2_ref, db1_ref, db2_ref):
            r[...] = jnp.zeros_like(r)
        pad1_ref[CONV_PAD:CONV_PAD + T, :] = u1_ref[...].astype(F32)
        pad2_ref[CONV_PAD:CONV_PAD + T, :] = u2_ref[...].astype(F32)

        def pre(base):
            rows = pl.ds(base, CONV_R)
            h1 = _taps_fwd(pad1_ref, w1_ref, K, base) + b1_ref[...]
            h2 = _taps_fwd(pad2_ref, w2_ref, K, base) + b2_ref[...]
            s = _sig(h1)
            d = _f32(d_ref, rows)
            padd1_ref[rows, :] = d * h2 * s * (1.0 + h1 * (1.0 - s))
            padd2_ref[rows, :] = d * h1 * s
        _loop_rows(T, pre)

        def main(base):
            rows = pl.ds(base, CONV_R)
            d1, d2 = padd1_ref[rows, :], padd2_ref[rows, :]
            _wgrad_acc(dw1_ref, pad1_ref, d1, K, base)
            _wgrad_acc(dw2_ref, pad2_ref, d2, K, base)
            db1_ref[...] += _fold8(d1)
            db2_ref[...] += _fold8(d2)
            du1_ref[rows, :] = _taps_bwd(padd1_ref, w1_ref, K, base).astype(BF16)
            du2_ref[rows, :] = _taps_bwd(padd2_ref, w2_ref, K, base).astype(BF16)
        _loop_rows(T, main)

    wrow, brow = _col(T, rows=K * SUBLANES), _col(T, rows=SUBLANES)
    return _conv_call(body, name, T, nt,
                      [_col(T), _col(T, nt), _col(T, rows=K), _col(T, nt, rows=K), _col(T, rows=1), _col(T, nt, rows=1), _col(T)],
                      (_col(T), _col(T), wrow, wrow, brow, brow),
                      (_sds((T, D_FF), BF16), _sds((T, D_FF), BF16), _sds((K * SUBLANES, D_FF)), _sds((K * SUBLANES, D_FF)),
                       _sds((SUBLANES, D_FF)), _sds((SUBLANES, D_FF))), 2, 2)(up, up, w, w, b, b, dact)


def _dot(a, b, dims="nn"):
    return lax.dot_general(a.astype(BF16), b.astype(BF16), _DIMS[dims], preferred_element_type=F32)


def _dot_mask(mask, v, mask_left):
    mb = mask.astype(BF16)
    hi = v.astype(BF16)
    r1 = v - hi.astype(F32)
    mid = r1.astype(BF16)
    lo = (r1 - mid.astype(F32)).astype(BF16)
    d = [jnp.dot(mb, t, preferred_element_type=F32) if mask_left else jnp.dot(t, mb, preferred_element_type=F32) for t in (hi, mid, lo)]
    return (d[0] + d[1]) + d[2]


def _ssd_small(xcr_ref, xrr_ref, bc_ref, br_ref, ac_ref, ar_ref):
    Q = SSD_Q
    li = lax.broadcasted_iota(jnp.int32, (Q, Q), 0)
    si = lax.broadcasted_iota(jnp.int32, (Q, Q), 1)
    tril = li >= si
    dtc = jax.nn.softplus(xcr_ref[...] + bc_ref[...])
    dtr = jax.nn.softplus(xrr_ref[...] + br_ref[...])
    cumc = _dot_mask(tril, dtc * ac_ref[...], True)
    cumr = _dot_mask(li <= si, dtr * ar_ref[...], False)
    return tril, dtc, dtr, cumc, cumr


def _ssd_specs(nc, rev):
    Q = SSD_Q
    cc = (lambda c: nc - 1 - c) if rev else (lambda c: c)
    x_spec = pl.BlockSpec((Q, 2 * LANES), lambda g, c: (cc(c), g))
    b_spec = pl.BlockSpec((Q, LANES), lambda g, c: (cc(c), 8 + g))
    c_spec = pl.BlockSpec((Q, LANES), lambda g, c: (cc(c), 12 + g))
    colm = pl.BlockSpec((None, Q, LANES), lambda g, c: (g, cc(c), 0))
    rowm = pl.BlockSpec((None, SUBLANES, Q), lambda g, c: (g, 0, cc(c)))
    colv = pl.BlockSpec((None, 1, LANES), lambda g, c: (g, 0, 0))
    rowv = pl.BlockSpec((None, SUBLANES, 1), lambda g, c: (g, 0, 0))
    st_spec = pl.BlockSpec((None, None, 2 * LANES, N_STATE), lambda g, c: (cc(c), g, 0, 0))
    return x_spec, b_spec, c_spec, colm, rowm, colv, rowv, st_spec


def _ssd_fwd(xc, raw_col, raw_row, bias_col, bias_row, a_col, a_row, dskip, name):
    T = xc.shape[0]
    Q = SSD_Q
    nc = T // Q
    x_spec, b_spec, c_spec, colm, rowm, colv, rowv, st_spec = _ssd_specs(nc, False)

    def body(dk_ref, x_ref, b_ref, c_ref, xcr_ref, xrr_ref, bc_ref, br_ref, ac_ref, ar_ref, y_ref, st_ref, h_ref):
        g = pl.program_id(0)

        @pl.when(pl.program_id(1) == 0)
        def _():
            h_ref[...] = jnp.zeros_like(h_ref)

        tril, dtc, dtr, cumc, cumr = _ssd_small(xcr_ref, xrr_ref, bc_ref, br_ref, ac_ref, ar_ref)
        Bm, Cm = b_ref[...], c_ref[...]
        S = _dot(Cm, Bm, "nt")
        lo = lax.broadcasted_iota(jnp.int32, (Q, LANES), 1) < HEAD_P
        rlo = lax.broadcasted_iota(jnp.int32, (LANES, N_STATE), 0) < HEAD_P
        st_ref[...] = h_ref[...]
        clast = cumc[Q - 1:Q, :]
        for pr in range(2):
            cols = slice(pr * LANES, (pr + 1) * LANES)
            xp = x_ref[:, cols]
            yd = jnp.zeros((Q, LANES), F32)
            for q in range(2):
                hh = 2 * pr + q
                seg = cumc[:, hh:hh + 1] - cumr[hh:hh + 1, :]
                lm = jnp.where(tril, jnp.exp(jnp.where(tril, seg, 0.0)), 0.0)
                w = S * lm * dtr[hh:hh + 1, :]
                xm = jnp.where(lo if q == 0 else jnp.logical_not(lo), xp, 0.0)
                yd = yd + _dot(w, xm)
            h0, h1 = 2 * pr, 2 * pr + 1
            c0, c1 = cumc[:, h0:h0 + 1], cumc[:, h1:h1 + 1]
            e_pair = jnp.where(lo, jnp.exp(c0), jnp.exp(c1))
            hp = h_ref[cols, :]
            ch = _dot(Cm, hp, "nt")
            dsk = jnp.where(lo, dk_ref[4 * g + h0], dk_ref[4 * g + h1])
            y_ref[:, cols] = yd + e_pair * ch + dsk * xp
            cl0, cl1 = clast[:, h0:h0 + 1], clast[:, h1:h1 + 1]
            sdec = jnp.where(lo, jnp.exp(cl0 - c0) * dtc[:, h0:h0 + 1], jnp.exp(cl1 - c1) * dtc[:, h1:h1 + 1])
            decrow = jnp.where(rlo, jnp.exp(cl0), jnp.exp(cl1))
            h_ref[cols, :] = hp * decrow + _dot(xp * sdec, Bm, "tn")

    smem = pl.BlockSpec(memory_space=pltpu.SMEM)
    return pl.pallas_call(
        body, name=name, grid=(N_GROUPS, nc),
        in_specs=[smem, x_spec, b_spec, c_spec, colm, rowm, colv, rowv, colv, rowv],
        out_specs=(x_spec, st_spec),
        out_shape=(_sds((T, D)), _sds((nc, N_GROUPS, 2 * LANES, N_STATE))),
        scratch_shapes=[pltpu.VMEM((2 * LANES, N_STATE), F32)],
        compiler_params=_cparams(("parallel", "arbitrary")))(dskip, xc, xc, xc, raw_col, raw_row, bias_col, bias_row, a_col, a_row)


def _ssd_bwd(xc, raw_col, raw_row, bias_col, bias_row, a_col, a_row, dskip, states, dy, name):
    T = xc.shape[0]
    Q = SSD_Q
    nc = T // Q
    x_spec, b_spec, c_spec, colm, rowm, colv, rowv, st_spec = _ssd_specs(nc, True)
    bo_spec = pl.BlockSpec((Q, LANES), lambda g, c: (nc - 1 - c, g))
    dd_spec = pl.BlockSpec((None, None, SUBLANES, 2 * LANES), lambda g, c: (nc - 1 - c, g, 0, 0))

    def body(dk_ref, x_ref, b_ref, c_ref, xcr_ref, xrr_ref, bc_ref, br_ref, ac_ref, ar_ref, st_ref, dy_ref,
             dx_ref, db_ref, dc_ref, sq_ref, cms_ref, ddac_ref, ddar_ref, dd_ref, dh_ref):
        g = pl.program_id(0)

        @pl.when(pl.program_id(1) == 0)
        def _():
            dh_ref[...] = jnp.zeros_like(dh_ref)

        tril, dtc, dtr, cumc, cumr = _ssd_small(xcr_ref, xrr_ref, bc_ref, br_ref, ac_ref, ar_ref)
        Bm, Cm = b_ref[...], c_ref[...]
        S = _dot(Cm, Bm, "nt")
        lane = lax.broadcasted_iota(jnp.int32, (Q, LANES), 1)
        sub = lax.broadcasted_iota(jnp.int32, (SUBLANES, Q), 0)
        rowi = lax.broadcasted_iota(jnp.int32, (Q, LANES), 0)
        lo = lane < HEAD_P
        rlo = lax.broadcasted_iota(jnp.int32, (LANES, N_STATE), 0) < HEAD_P
        clast = cumc[Q - 1:Q, :]
        ds_g = jnp.zeros((Q, Q), F32)
        dcm = jnp.zeros((Q, N_STATE), F32)
        dbm = jnp.zeros((Q, N_STATE), F32)
        dcum_col = jnp.zeros((Q, LANES), F32)
        dcum_row = jnp.zeros((SUBLANES, Q), F32)
        sq_col = jnp.zeros((Q, LANES), F32)
        cms_row = jnp.zeros((SUBLANES, Q), F32)
        for pr in range(2):
            cols = slice(pr * LANES, (pr + 1) * LANES)
            xp, dyp = x_ref[:, cols], dy_ref[:, cols]
            hin, dhp = st_ref[cols, :], dh_ref[cols, :]
            h0, h1 = 2 * pr, 2 * pr + 1
            c0, c1 = cumc[:, h0:h0 + 1], cumc[:, h1:h1 + 1]
            cl0, cl1 = clast[:, h0:h0 + 1], clast[:, h1:h1 + 1]
            e_pair = jnp.where(lo, jnp.exp(c0), jnp.exp(c1))
            edec = jnp.where(lo, jnp.exp(cl0 - c0), jnp.exp(cl1 - c1))
            dt_pair = jnp.where(lo, dtc[:, h0:h0 + 1], dtc[:, h1:h1 + 1])
            sdec = edec * dt_pair
            ch = _dot(Cm, hin, "nt")
            xb = _dot(Bm, dhp, "nt")
            dye = dyp * e_pair
            t1 = dye * ch
            t2 = xp * xb * edec
            hh_prod = dhp * hin
            dsk = jnp.where(lo, dk_ref[4 * g + h0], dk_ref[4 * g + h1])
            dxp = sdec * xb + dsk * dyp
            for q in range(2):
                hh = 2 * pr + q
                mine = lo if q == 0 else jnp.logical_not(lo)
                seg = cumc[:, hh:hh + 1] - cumr[hh:hh + 1, :]
                lm = jnp.where(tril, jnp.exp(jnp.where(tril, seg, 0.0)), 0.0)
                dtrow = dtr[hh:hh + 1, :]
                w = S * lm * dtrow
                dym = jnp.where(mine, dyp, 0.0)
                gl = _dot(dym, xp, "nt") * lm
                ds_g = ds_g + gl * dtrow
                ms = gl * S
                m = ms * dtrow
                dxp = dxp + _dot(w, dym, "tn")
                cms_row = jnp.where(sub == hh, jnp.sum(ms, axis=0, keepdims=True), cms_row)
                dcum_row = jnp.where(sub == hh, -jnp.sum(m, axis=0, keepdims=True), dcum_row)
                t1h = jnp.sum(jnp.where(mine, t1, 0.0), axis=1, keepdims=True)
                sqh = jnp.sum(jnp.where(mine, t2, 0.0), axis=1, keepdims=True)
                sth = sqh * dtc[:, hh:hh + 1]
                rmine = rlo if q == 0 else jnp.logical_not(rlo)
                hsum = jnp.sum(jnp.sum(jnp.where(rmine, hh_prod, 0.0), axis=1, keepdims=True), axis=0, keepdims=True)
                last = jnp.sum(sth, axis=0, keepdims=True) + jnp.exp(clast[:, hh:hh + 1]) * hsum
                dcol = jnp.sum(m, axis=1, keepdims=True) + t1h - sth
                dcum_col = jnp.where(lane == hh, dcol + jnp.where(rowi == Q - 1, last, 0.0), dcum_col)
                sq_col = jnp.where(lane == hh, sqh, sq_col)
            dcm = dcm + _dot(dye, hin)
            dbm = dbm + _dot(xp * sdec, dhp)
            decrow = jnp.where(rlo, jnp.exp(cl0), jnp.exp(cl1))
            dh_ref[cols, :] = dhp * decrow + _dot(dye, Cm, "tn")
            dx_ref[:, cols] = dxp
            dd_ref[:, cols] = jnp.broadcast_to(jnp.sum(dyp * xp, axis=0, keepdims=True), (SUBLANES, LANES))
        dc_ref[...] = dcm + _dot(ds_g, Bm)
        db_ref[...] = dbm + _dot(ds_g, Cm, "tn")
        li = lax.broadcasted_iota(jnp.int32, (Q, Q), 0)
        si = lax.broadcasted_iota(jnp.int32, (Q, Q), 1)
        ddac_ref[...] = _dot_mask(li <= si, dcum_col, True)
        ddar_ref[...] = _dot_mask(tril, dcum_row, False)
        sq_ref[...] = sq_col
        cms_ref[...] = cms_row

    smem = pl.BlockSpec(memory_space=pltpu.SMEM)
    return pl.pallas_call(
        body, name=name, grid=(N_GROUPS, nc),
        in_specs=[smem, x_spec, b_spec, c_spec, colm, rowm, colv, rowv, colv, rowv, st_spec, x_spec],
        out_specs=(x_spec, bo_spec, bo_spec, colm, rowm, colm, rowm, dd_spec),
        out_shape=(_sds((T, D)), _sds((T, D // 2)), _sds((T, D // 2)), _sds((N_GROUPS, T, LANES)), _sds((N_GROUPS, SUBLANES, T)),
                   _sds((N_GROUPS, T, LANES)), _sds((N_GROUPS, SUBLANES, T)), _sds((nc, N_GROUPS, SUBLANES, 2 * LANES))),
        scratch_shapes=[pltpu.VMEM((2 * LANES, N_STATE), F32)],
        compiler_params=_cparams(("parallel", "arbitrary")))(dskip, xc, xc, xc, raw_col, raw_row, bias_col, bias_row, a_col, a_row,
                                                            states, dy)


def _adam_math(wv, gv, mv, vv):
    c1 = 1.0 - ADAM_B1 ** ADAM_STEP
    c2 = 1.0 - ADAM_B2 ** ADAM_STEP
    mn = ADAM_B1 * mv + (1.0 - ADAM_B1) * gv
    vn = ADAM_B2 * vv + (1.0 - ADAM_B2) * (gv * gv)
    return -ADAM_LR * ((mn / c1) / (jnp.sqrt(vn / c2) + ADAM_EPS) + ADAM_WD * wv), mn, vn


def _adamw_layers(w, g, m, v, l0, Lg, bufs, name):
    L, As, Bs = w.shape
    tr = _tile(As, [], (256, 352, 128))
    has_bufs = bufs is not None

    def body(*refs):
        w_ref, g_ref, m_ref, v_ref = refs[:4]
        d_ref, mo_ref, vo_ref = refs[4 + 3 * has_bufs:]
        d_ref[...], mo_ref[...], vo_ref[...] = _adam_math(w_ref[...], g_ref[...], m_ref[...], v_ref[...])

    spec = pl.BlockSpec((None, tr, Bs), lambda l, i: (l + l0, i, 0))
    args = (w, g, m, v) + (tuple(bufs) if has_bufs else ())
    return pl.pallas_call(
        body, name=name, grid=(Lg, As // tr), in_specs=[spec] * 4 + [_ANY] * (3 * has_bufs), out_specs=(spec,) * 3,
        out_shape=(_sds((L, As, Bs)),) * 3, input_output_aliases={4: 0, 5: 1, 6: 2} if has_bufs else {},
        compiler_params=_cparams(("parallel", "parallel")))(*args)


def _adamw_minor_rows(w, g, m, v, name):
    L, R, C = w.shape
    tr = max(t for t in range(1, C + 1) if C % t == 0 and t * L * R * 4 <= (1 << 20))
    wt, gt, mt, vt = (jnp.transpose(t, (2, 0, 1)) for t in (w, g, m, v))

    def body(w_ref, g_ref, m_ref, v_ref, d_ref, mo_ref, vo_ref):
        d_ref[...], mo_ref[...], vo_ref[...] = _adam_math(w_ref[...], g_ref[...], m_ref[...], v_ref[...])

    spec = pl.BlockSpec((tr, L, R), lambda i: (i, 0, 0))
    out = pl.pallas_call(body, name=name, grid=(C // tr,), in_specs=[spec] * 4, out_specs=(spec,) * 3,
                         out_shape=(_sds((C, L, R)),) * 3, compiler_params=_cparams(("parallel",)))(wt, gt, mt, vt)
    return tuple(jnp.transpose(o, (1, 2, 0)) for o in out) + (jnp.transpose(gt, (1, 2, 0)),)


def _adamw(w, g, m, v, name):
    shape = w.shape
    cols = shape[-1]
    w2, g2, m2, v2 = (t.reshape(-1, cols) for t in (w, g, m, v))
    rows = w2.shape[0]
    tr = 256 if (rows % 256 == 0 and rows > 256) else rows
    c1 = 1.0 - ADAM_B1 ** ADAM_STEP
    c2 = 1.0 - ADAM_B2 ** ADAM_STEP

    def body(w_ref, g_ref, m_ref, v_ref, d_ref, mo_ref, vo_ref):
        gv = g_ref[...]
        mn = ADAM_B1 * m_ref[...] + (1.0 - ADAM_B1) * gv
        vn = ADAM_B2 * v_ref[...] + (1.0 - ADAM_B2) * (gv * gv)
        d_ref[...] = -ADAM_LR * ((mn / c1) / (jnp.sqrt(vn / c2) + ADAM_EPS) + ADAM_WD * w_ref[...])
        mo_ref[...] = mn
        vo_ref[...] = vn

    spec = pl.BlockSpec((tr, cols), lambda i: (i, 0))
    out = pl.pallas_call(body, name=name, grid=(rows // tr,), in_specs=[spec] * 4, out_specs=(spec,) * 3,
                         out_shape=(_sds((rows, cols)),) * 3, compiler_params=_cparams(("parallel",)))(w2, g2, m2, v2)
    return tuple(o.reshape(shape) for o in out)


def _place():
    x, y, c = lax.axis_index("x"), lax.axis_index("y"), lax.axis_index("c")
    chips = [(1 - x, y), (x, 1 - y), (1 - x, 1 - y)]
    return x, y, c, chips


_ANY = pl.BlockSpec(memory_space=pl.ANY)


TENSORS = (("e_w_in", "row", 2, 4096, 1284, 1024), ("e_w_out", "row", 2, 2048, 1024, 512), ("o_w_in", "col", 2, 1024, 3072, 768),
           ("o_w_out", "row", 2, 1024, 1024, 256), ("f_w_up", "col", 4, 1024, 5632, 1408), ("f_w_down", "row", 4, 2816, 1024, 704),
           ("ple_w_proj", "col", 4, 256, 1024, 256), ("ple_w_gate", "row", 4, 1024, 1024, 256))
MIX, FFN = "mix", "ffn"
W_GROUPS = (((0, MIX),), ((0, FFN), (1, MIX)), ((1, FFN), (2, MIX)), ((2, FFN), (3, MIX), (3, FFN)))
G_GROUPS = (((3, FFN), (3, MIX), (2, FFN), (2, MIX), (1, FFN), (1, MIX)), ((0, FFN),), ((0, MIX),))


def _tensor_layer(name, layer):
    if name.startswith("e_"):
        return layer // 2 if layer % 2 == 0 else None
    if name.startswith("o_"):
        return layer // 2 if layer % 2 == 1 else None
    return layer


def _part(name):
    return MIX if name.startswith(("e_", "o_")) else FFN


def _group_items(members):
    items = []
    for name, kind, L, A, B, n in TENSORS:
        tls = sorted(t for t in (_tensor_layer(name, l) for l, part in members if part == _part(name)) if t is not None)
        if tls:
            assert tls == list(range(tls[0], tls[0] + len(tls)))
            items.append((name, kind, len(tls), A, B, n, tls[0]))
    return items


def _hwin(ref, it, k, h):
    name, kind, Lg, A, B, n, l0 = it
    if kind == "row":
        return ref.at[:, pl.ds(pl.multiple_of(k * n + h * (n // 2), 16), n // 2), :]
    return ref.at[:, pl.ds(pl.multiple_of(h * (A // 2), 16), A // 2), pl.ds(pl.multiple_of(k * n, LANES), n)]


def _shard_dims(kind, A, B, n):
    return (n, B) if kind == "row" else (A, n)


def _cast_into(w, it, me):
    name, kind, Lg, A, B, n, l0 = it
    As, Bs = _shard_dims(kind, A, B, n)

    def body(me_ref, w_ref, o_ref):
        o_ref[...] = w_ref[...].astype(BF16)

    omap = (lambda l, m: (l, m[0], 0)) if kind == "row" else (lambda l, m: (l, 0, m[0]))
    grid_spec = pltpu.PrefetchScalarGridSpec(
        num_scalar_prefetch=1, grid=(Lg,), in_specs=[pl.BlockSpec((None, As, Bs), lambda l, m: (l + l0, 0, 0))],
        out_specs=pl.BlockSpec((None, As, Bs), omap))
    return pl.pallas_call(body, name=f"cast_{name}_{l0}", grid_spec=grid_spec, out_shape=_sds((Lg, A, B), BF16),
                          compiler_params=_cparams(("parallel",)))(me, w.reshape(-1, As, Bs))


_HBM = pl.BlockSpec(memory_space=pltpu.HBM)
_SEM = pl.BlockSpec(memory_space=pltpu.SEMAPHORE)
_EFFECT = pltpu.SideEffectType.DATAFLOW_SIDE_EFFECTING


def _hbm(a):
    return pltpu.with_memory_space_constraint(a, pltpu.HBM)


def _split_start(thru, n_copies, issue, name, after=None):
    N = len(thru)
    has_after = after is not None

    def body(*refs):
        outs = refs[N + has_after:2 * N + has_after]
        send_sems, recv_sems, token = refs[2 * N + has_after:]
        for cp in issue(outs, send_sems, recv_sems):
            cp.start()
        token[...] = jnp.zeros_like(token)

    out = pl.pallas_call(
        body, name=name, in_specs=[_HBM] * N + ([_ANY] if has_after else []),
        out_specs=(_HBM,) * N + (_SEM, _SEM, pl.BlockSpec(memory_space=pltpu.VMEM)),
        out_shape=tuple(pltpu.HBM(a.shape, a.dtype) for a in thru)
        + (pltpu.SemaphoreType.DMA((n_copies,)), pltpu.SemaphoreType.DMA((n_copies,)), _sds((SUBLANES, LANES))),
        input_output_aliases={t: t for t in range(N)},
        compiler_params=pltpu.CompilerParams(has_side_effects=_EFFECT))(*[_hbm(a) for a in thru], *([after] if has_after else []))
    return list(out[:N]), out[N], out[N + 1], out[N + 2]


def _split_wait(thru, send_sems, recv_sems, after, waits, name):
    N = len(thru)
    after = list(after) if isinstance(after, (list, tuple)) else [after]

    def body(*refs):
        ins = refs[:N]
        for cp, side in waits(ins, refs[N], refs[N + 1]):
            if side == "send":
                cp.wait_send()
            else:
                cp.wait_recv()

    out = pl.pallas_call(
        body, name=name, in_specs=[_HBM] * N + [_SEM, _SEM] + [_ANY] * len(after), out_specs=(_HBM,) * N,
        out_shape=tuple(pltpu.HBM(a.shape, a.dtype) for a in thru), input_output_aliases={t: t for t in range(N)},
        compiler_params=pltpu.CompilerParams(has_side_effects=_EFFECT))(*thru, send_sems, recv_sems, *after)
    return list(out)


def _rcopy(send_sems, recv_sems, k, src, dst, to):
    return pltpu.make_async_remote_copy(src_ref=src, dst_ref=dst, send_sem=send_sems.at[k], recv_sem=recv_sems.at[k],
                                        device_id=to, device_id_type=MESH)


def _gather_copies(items, refs, send_sems, recv_sems, what):
    x, y, c, chips = _place()
    me = 2 * x + y
    out = []
    for t, it in enumerate(items):
        mine = _hwin(refs[t], it, me, c)
        for j, (px, py) in enumerate(chips):
            if what == "start":
                out.append(_rcopy(send_sems, recv_sems, 3 * t + j, mine, mine, (px, py, c)))
            else:
                slot = _hwin(refs[t], it, 2 * px + py, c)
                out.append((_rcopy(send_sems, recv_sems, 3 * t + j, mine, mine, (px, py, c)), "send"))
                out.append((_rcopy(send_sems, recv_sems, 3 * t + j, slot, slot, (px, py, c)), "recv"))
    return out


def _gather_start(fulls, items, name, after=None):
    return _split_start(fulls, 3 * len(items), functools.partial(_gather_copies, items, what="start"), name, after)


def _gather_wait(fulls, send_sems, recv_sems, after, items, name):
    return _split_wait(fulls, send_sems, recv_sems, after, functools.partial(_gather_copies, items, what="wait"), name)


def _gather_fwd(fulls, items, name, ws=None):
    N = len(fulls)
    has_ws = ws is not None

    def body(*refs):
        outs = refs[N + has_ws:2 * N + has_ws]
        rest = refs[2 * N + has_ws:]
        x, y, c, chips = _place()
        me = 2 * x + y
        sib = (x, y, 1 - c)
        if has_ws:
            ws_ref = refs[N]
            WS_ref, send_sems, recv_sems, lsem = rest
            loc = pltpu.make_async_copy(ws_ref, WS_ref.at[me], lsem)
            loc.start()
        else:
            send_sems, recv_sems = rest
        rc = functools.partial(_rcopy, send_sems, recv_sems)
        cps = []
        for t, it in enumerate(items):
            for j, (px, py) in enumerate(chips):
                slot = _hwin(outs[t], it, 2 * px + py, c)
                cps.append(rc(3 * t + j, slot, slot, sib))
        if has_ws:
            cps += [rc(3 * N + j, ws_ref, WS_ref.at[me], (*chip, c)) for j, chip in enumerate(chips)]
        for cp in cps:
            cp.start()
        for t, it in enumerate(items):
            for j, (px, py) in enumerate(chips):
                oslot = _hwin(outs[t], it, 2 * px + py, 1 - c)
                rc(3 * t + j, oslot, oslot, sib).wait_recv()
        if has_ws:
            for j, (px, py) in enumerate(chips):
                sslot = WS_ref.at[2 * px + py]
                rc(3 * N + j, sslot, sslot, sib).wait_recv()
        for cp in cps:
            cp.wait_send()
        if has_ws:
            loc.wait()

    ns = 3 * N + (3 if has_ws else 0)
    out_shape = tuple(_sds(f.shape, f.dtype) for f in fulls)
    scratch = [pltpu.SemaphoreType.DMA((ns,)), pltpu.SemaphoreType.DMA((ns,))]
    args = list(fulls)
    if has_ws:
        out_shape += (_sds((4,) + ws.shape, ws.dtype),)
        scratch.append(pltpu.SemaphoreType.DMA(()))
        args.append(ws)
    out = pl.pallas_call(
        body, name=name, in_specs=[_ANY] * len(args), out_specs=(_ANY,) * len(out_shape), out_shape=out_shape,
        input_output_aliases={t: t for t in range(N)}, scratch_shapes=scratch,
        compiler_params=pltpu.CompilerParams(has_side_effects=True))(*args)
    return (list(out[:N]), out[N]) if has_ws else (list(out), None)


def _half_shape(it):
    name, kind, Lg, A, B, n, l0 = it
    return (Lg, 4, n // 2, B) if kind == "row" else (Lg, A // 2, B)


def _piece_shape(it):
    name, kind, Lg, A, B, n, l0 = it
    return (Lg, n // 2, B) if kind == "row" else (Lg, A // 2, n)


def _swap_copies(items, refs, send_sems, recv_sems, what):
    N = len(items)
    x, y, c, _ = _place()
    sib = (x, y, 1 - c)
    out = []
    for t, it in enumerate(items):
        name_, kind, Lg, A, B, n, l0 = it
        if kind == "row":
            cps = [_rcopy(send_sems, recv_sems, 4 * t + k, _hwin(refs[t], it, k, 1 - c), refs[N + t].at[:, k], sib) for k in range(4)]
        else:
            src = refs[t].at[:, pl.ds(pl.multiple_of((1 - c) * (A // 2), 16), A // 2), :]
            cps = [_rcopy(send_sems, recv_sems, 4 * t, src, refs[N + t], sib)]
        for cp in cps:
            if what == "start":
                out.append(cp)
            else:
                out += [(cp, "send"), (cp, "recv")]
    return out


def _swap_start(gs, items, name, after=None):
    lands = [lax.empty(_half_shape(it), F32) for it in items]
    return _split_start(list(gs) + lands, 4 * len(items), functools.partial(_swap_copies, items, what="start"), name, after)


def _swap_wait(thru, send_sems, recv_sems, after, items, name):
    return _split_wait(thru, send_sems, recv_sems, after, functools.partial(_swap_copies, items, what="wait"), name)


def _add_half(g, ra, it, cvec):
    name, kind, Lg, A, B, n, l0 = it
    if kind == "row":
        blk = (None, n // 2, B)
        grid = (Lg, 4)
        g_spec = pl.BlockSpec(blk, lambda l, k, cr: (l, 2 * k + cr[0], 0))
        h_spec = pl.BlockSpec((None, None, n // 2, B), lambda l, k, cr: (l, k, 0, 0))
    else:
        tr = _tile(A // 2, [], (256, 128))
        nb = (A // 2) // tr
        grid = (Lg, nb)
        g_spec = pl.BlockSpec((None, tr, B), lambda l, i, cr: (l, cr[0] * nb + i, 0))
        h_spec = pl.BlockSpec((None, tr, B), lambda l, i, cr: (l, i, 0))

    def body(c_ref, g_ref, r_ref, o_ref):
        o_ref[...] = (g_ref[...] + r_ref[...]).astype(BF16)

    grid_spec = pltpu.PrefetchScalarGridSpec(num_scalar_prefetch=1, grid=grid, in_specs=[g_spec, h_spec], out_specs=h_spec)
    return pl.pallas_call(body, name=f"addhalf_{name}_{l0}", grid_spec=grid_spec, out_shape=_sds(_half_shape(it), BF16),
                          compiler_params=_cparams(("parallel", "parallel")))(cvec, g, ra)


def _scatter_copies(items, refs, send_sems, recv_sems, what):
    N = len(items)
    x, y, c, chips = _place()
    out = []
    for t, it in enumerate(items):
        name, kind, Lg, A, B, n, l0 = it
        for j, (px, py) in enumerate(chips):
            k = 2 * px + py
            src = refs[t].at[:, k] if kind == "row" else refs[t].at[:, :, pl.ds(pl.multiple_of(k * n, LANES), n)]
            cp = _rcopy(send_sems, recv_sems, 3 * t + j, src, refs[N + t].at[j], (px, py, c))
            if what == "start":
                out.append(cp)
            else:
                out += [(cp, "send"), (cp, "recv")]
    return out


def _scatter_start(ps, items, name):
    lands = [lax.empty((3,) + _piece_shape(it), BF16) for it in items]
    return _split_start(list(ps) + lands, 3 * len(items), functools.partial(_scatter_copies, items, what="start"), name)


def _scatter_wait(thru, send_sems, recv_sems, after, items, name):
    return _split_wait(thru, send_sems, recv_sems, after, functools.partial(_scatter_copies, items, what="wait"), name)


def _sum_own(p, rc, it, mevec, buf):
    name, kind, Lg, A, B, n, l0 = it
    As, Bs = _shard_dims(kind, A, B, n)
    L = [s[2] for s in TENSORS if s[0] == name][0]
    hb = (As // 2, Bs)
    has_buf = buf is not None

    def body(*refs):
        p_ref, r0, r1, r2 = refs[1:5]
        o_ref = refs[5 + has_buf]
        o_ref[...] = ((p_ref[...].astype(F32) + r0[...].astype(F32)) + r1[...].astype(F32)) + r2[...].astype(F32)

    if kind == "row":
        p_spec = pl.BlockSpec((None, None) + hb, lambda l, m: (l, m[0], 0, 0))
    else:
        p_spec = pl.BlockSpec((None,) + hb, lambda l, m: (l, 0, m[0]))
    r_specs = [pl.BlockSpec((None, None) + hb, functools.partial(lambda l, m, j: (j, l, 0, 0), j=j)) for j in range(3)]
    in_specs = [p_spec] + r_specs + ([_ANY] if has_buf else [])
    grid_spec = pltpu.PrefetchScalarGridSpec(num_scalar_prefetch=1, grid=(Lg,), in_specs=in_specs,
                                             out_specs=pl.BlockSpec((None,) + hb, lambda l, m: (l + l0, m[1], 0)))
    args = (mevec, p, rc, rc, rc) + ((buf,) if has_buf else ())
    return pl.pallas_call(body, name=f"sumown_{name}_{l0}", grid_spec=grid_spec, out_shape=_sds((L, As, Bs)),
                          input_output_aliases={5: 0} if has_buf else {}, compiler_params=_cparams(("parallel",)))(*args)


def _join_halves(rs, items, name):
    N = len(rs)

    def body(*refs):
        outs = refs[N:2 * N]
        send_sems, recv_sems = refs[2 * N:]
        x, y, c, _ = _place()
        sib = (x, y, 1 - c)

        def half(t, h):
            name_, kind, Lg, A, B, n, l0 = items[t]
            hr = _shard_dims(kind, A, B, n)[0] // 2
            return outs[t].at[pl.ds(l0, Lg), pl.ds(pl.multiple_of(h * hr, SUBLANES), hr), :]

        cps = [_rcopy(send_sems, recv_sems, t, half(t, c), half(t, c), sib) for t in range(N)]
        for cp in cps:
            cp.start()
        for t in range(N):
            _rcopy(send_sems, recv_sems, t, half(t, 1 - c), half(t, 1 - c), sib).wait_recv()
        for cp in cps:
            cp.wait_send()

    return list(pl.pallas_call(
        body, name=name, in_specs=[_ANY] * N, out_specs=(_ANY,) * N, out_shape=tuple(_sds(r.shape, r.dtype) for r in rs),
        input_output_aliases={t: t for t in range(N)},
        scratch_shapes=[pltpu.SemaphoreType.DMA((N,)), pltpu.SemaphoreType.DMA((N,))],
        compiler_params=pltpu.CompilerParams(has_side_effects=True))(*rs))


def _allgather_small(v):
    m_per, n = v.shape

    def body(x_ref, out_ref, send_sems, recv_sems, local_sem):
        x, y, c, chips = _place()
        me, sibling = (x, y, c), (x, y, 1 - c)

        def rows(px, py, pc):
            return out_ref.at[pl.ds(pl.multiple_of((4 * px + 2 * py + pc) * m_per, SUBLANES), m_per), :]

        def copy(k, block, to, src=None):
            return pltpu.make_async_remote_copy(src_ref=rows(*block) if src is None else src, dst_ref=rows(*block),
                                                send_sem=send_sems.at[k], recv_sem=recv_sems.at[k], device_id=to, device_id_type=MESH)

        mine = pltpu.make_async_copy(x_ref, rows(*me), local_sem)
        mine.start()
        first = [copy(0, me, sibling, src=x_ref)]
        first += [copy(1 + j, me, (*chip, c), src=x_ref) for j, chip in enumerate(chips)]
        for cp in first:
            cp.start()
        passed = [copy(4 + j, (*chip, c), sibling) for j, chip in enumerate(chips)]
        for j, chip in enumerate(chips):
            copy(1 + j, (*chip, c), me).wait_recv()
            passed[j].start()
        copy(0, sibling, me).wait_recv()
        for j, chip in enumerate(chips):
            copy(4 + j, (*chip, 1 - c), me).wait_recv()
        for cp in first + passed:
            cp.wait_send()
        mine.wait()

    vm = pl.BlockSpec(memory_space=pltpu.VMEM)
    return pl.pallas_call(body, name="allgather_small", in_specs=[vm], out_specs=vm, out_shape=_sds((8 * m_per, n)),
                          scratch_shapes=[pltpu.SemaphoreType.DMA((7,)), pltpu.SemaphoreType.DMA((7,)), pltpu.SemaphoreType.DMA(())],
                          compiler_params=pltpu.CompilerParams(has_side_effects=True, vmem_limit_bytes=VMEM_LIMIT))(v)


def _sum8(v, m_per):
    def body(v_ref, o_ref):
        acc = v_ref[0:m_per, :]
        for k in range(1, 8):
            acc = acc + v_ref[k * m_per:(k + 1) * m_per, :]
        o_ref[...] = acc

    return pl.pallas_call(body, name="small_sum_devices", out_shape=_sds((m_per, v.shape[1])),
                          compiler_params=pltpu.CompilerParams(vmem_limit_bytes=VMEM_LIMIT))(v)


SMALL_SHARDED = (("e_conv_a_w", 2), ("e_conv_b_w", 2), ("o_conv_w", 2), ("f_conv_w", 2), ("ln_g", 2), ("ln_b", 2))
SMALL_REPL = ("e_conv_a_b", "e_ln_a_g", "e_ln_a_b", "e_conv_b_b", "e_dt_bias", "e_a_log", "e_d_skip", "e_norm_b_g", "f_conv_b")

WEIGHT_ORDER = ('e_w_in', 'e_conv_a_w', 'e_conv_a_b', 'e_ln_a_g', 'e_ln_a_b', 'e_conv_b_w', 'e_conv_b_b', 'e_dt_bias', 'e_a_log',
                'e_d_skip', 'e_norm_b_g', 'e_w_out', 'o_w_in', 'o_conv_w', 'o_w_out', 'f_w_up', 'f_conv_w', 'f_conv_b', 'f_w_down',
                'ple_w_proj', 'ple_w_gate', 'ln_g', 'ln_b')


def _pack_rows(parts, width, total_rows, dtype):
    flat = jnp.concatenate([p.reshape(-1).astype(dtype) for p in parts])
    flat = jnp.pad(flat, (0, total_rows * width - flat.shape[0]))
    return flat.reshape(total_rows, width)


def _unpack_rows(buf, shapes):
    flat = buf.reshape(-1)
    out, pos = [], 0
    for s in shapes:
        n = math.prod(s)
        out.append(flat[pos:pos + n].reshape(s))
        pos += n
    return out


def _small_rows(shapes):
    n = sum(math.prod(s) for s in shapes)
    return -(-n // (LANES * SUBLANES)) * SUBLANES


E_PAD = 5248
SEG_A, SEG_Z, SEG_X, SEG_DT = (0, 2 * D), (2 * D, D), (3 * D, 2 * D), (5 * D, LANES)
G_SHAPES = {"e_w_in": (2, D, E_PAD), "e_w_out": (2, 2 * D, D), "o_w_in": (2, D, 3 * D), "o_w_out": (2, D, D),
            "f_w_up": (4, D, 2 * D_FF), "f_w_down": (4, D_FF, D), "ple_w_proj": (4, PLE, D), "ple_w_gate": (4, D, D)}


def _padcols(w, width):
    return jnp.pad(w, ((0, 0), (0, width - w.shape[1])))


def _fold_rows(dw, K):
    return dw.reshape(K, SUBLANES, dw.shape[-1]).sum(1)


class GradBuffers(dict):
    def __init__(self):
        super().__init__()
        self.where = {}
        for gi, layers in enumerate(G_GROUPS):
            for name, kind, Lg, A, B, n, l0 in _group_items(layers):
                for k in range(Lg):
                    self.where[(name, l0 + k)] = (gi, k, Lg)
        self.current = {}

    def into(self, name, layer, r0=0, c0=0):
        gi, k, Lg = self.where[(name, layer)]
        self.current[name] = (name, gi)
        return (self.get((name, gi)), (Lg,) + G_SHAPES[name][1:], (k,), r0, c0)

    def __setitem__(self, name, value):
        super().__setitem__(self.current[name], value)
        self.last = value


def _local_step(x, p, target, W, comm=None, xb=None):
    T = x.shape[0]
    if xb is None:
        xb = _to_bf16(x, "x_bf16")
    saved = []
    xc_f = x
    for i in range(DEPTH):
        j = i // 2
        L = {}
        L["x"], L["xb"] = xc_f, xb
        tok = comm.part_starts(i, MIX, xb) if comm is not None else None
        if i % 2 == 0:
            def w_in(seg, c0=0, cols=None, j=j):
                return V(W["e_w_in"], (j,), c0=seg[0] + c0, cols=seg[1] if cols is None else cols)

            ua = _mm(xb, w_in(SEG_A), "nn", f"l{i}_in_a", BF16, after=tok)
            z = _mm(xb, w_in(SEG_Z), "nn", f"l{i}_in_z")
            xu = _mm(xb, w_in(SEG_X), "nn", f"l{i}_in_xbc", BF16)
            udt = _mm(xb, w_in(SEG_DT), "nn", f"l{i}_in_dt")
            ac = _conv_a_fwd(ua, W["e_conv_a_w"][j], W["e_conv_a_b"][j][None], f"l{i}_conv_a")
            ya = _ln_silu_fwd(ac, W["e_ln_a_g"][j][None], W["e_ln_a_b"][j][None], f"l{i}_ln_a")
            xc = _conv_b_fwd(xu, W["e_conv_b_w"][j], W["e_conv_b_b"][j][None], f"l{i}_conv_b")
            sm = _ssd_small_inputs(udt[:, :N_HEADS], W["e_dt_bias"][j], W["e_a_log"][j])
            y, states = _ssd_fwd(xc, *sm, W["e_d_skip"][j], f"l{i}_ssd")
            yb = _gate_rms_fwd(y, z, W["e_norm_b_g"][j][None], f"l{i}_gate_rms")
            out_pairs = [(ya, V(W["e_w_out"], (j,), rows=D)), (yb, V(W["e_w_out"], (j,), r0=D))]
            L.update(ua=ua, z=z, xu=xu, udt=udt, ac=ac, ya=ya, xc=xc, sm=sm, y=y, states=states, yb=yb, w_in=w_in)
        else:
            uo = _mm(xb, V(W["o_w_in"], (j,)), "nn", f"l{i}_in", BF16, after=tok)
            sc = _conv_c_fwd(uo, W["o_conv_w"][j], f"l{i}_conv_c")
            out_pairs = [(sc, V(W["o_w_out"], (j,)))]
            L.update(uo=uo, sc=sc)
        h1, x1, x1b = _mm_sum(out_pairs, "nn", f"l{i}_out", ln_fwd=(xc_f, None, W["ln_g"][i, 0][None], W["ln_b"][i, 0][None]))
        tok = comm.part_starts(i, FFN, x1b) if comm is not None else None
        up = _mm(x1b, V(W["f_w_up"], (i,)), "nn", f"l{i}_ffn_up", BF16, after=tok)
        act = _conv_f_fwd(up, W["f_conv_w"][i], W["f_conv_b"][i][None], f"l{i}_conv_f")
        pv = V(p, (i, 0))
        pp = _mm(pv, V(W["ple_w_proj"], (i,)), "nn", f"l{i}_ple_proj")
        gl = _mm(x1b, V(W["ple_w_gate"], (i,)), "nn", f"l{i}_ple_gate")
        h2, x2, x2b = _mm_sum([(act, V(W["f_w_down"], (i,)))], "nn", f"l{i}_ffn_down",
                              ln_fwd=(x1, (pp, gl), W["ln_g"][i, 1][None], W["ln_b"][i, 1][None]))
        L.update(h1=h1, x1=x1, x1b=x1b, up=up, act=act, pv=pv, pp=pp, gl=gl, h2=h2)
        saved.append(L)
        xc_f, xb = x2, x2b

    sq, dx = _loss_head(xc_f, target, "loss_head")

    GB = GradBuffers()
    into = GB.into
    tok = None
    ln2_done = None

    G = {n: [None] * (DEPTH if n.startswith(("f_", "ln_")) else DEPTH // 2) for n in WEIGHT_ORDER if n not in G_SHAPES}
    for i in reversed(range(DEPTH)):
        j = i // 2
        L = saved[i]
        if ln2_done is None:
            ln2_done = _res_ln_bwd(dx, L["h2"], W["ln_g"][i, 1][None], (L["pp"], L["gl"]), f"l{i}_ln2_bwd")
        dh2, dh2b, dg2, db2, dpp, dgl = ln2_done
        ln2_done = None
        GB["f_w_down"] = _mm(L["act"], dh2b, "tn", f"l{i}_dw_down", dst=into("f_w_down", i))
        dact = _mm(dh2b, V(W["f_w_down"], (i,)), "nt", f"l{i}_dact", BF16, after=tok)
        du1, du2, dw1, dw2, dbf1, dbf2 = _conv_f_bwd(L["up"], W["f_conv_w"][i], W["f_conv_b"][i][None], dact, f"l{i}_conv_f_bwd")
        G["f_conv_w"][i] = jnp.concatenate([_fold_rows(dw1, CONV_F), _fold_rows(dw2, CONV_F)], axis=1)
        G["f_conv_b"][i] = jnp.concatenate([dbf1.sum(0), dbf2.sum(0)])
        GB["f_w_up"] = _mm(L["x1b"], du1, "tn", f"l{i}_dw_up1", dst=into("f_w_up", i))
        GB["f_w_up"] = _mm(L["x1b"], du2, "tn", f"l{i}_dw_up2", dst=into("f_w_up", i, c0=D_FF))
        GB["ple_w_proj"] = _mm(L["pv"], dpp, "tn", f"l{i}_dw_proj", dst=into("ple_w_proj", i))
        GB["ple_w_gate"] = _mm(L["x1b"], dgl, "tn", f"l{i}_dw_gate", dst=into("ple_w_gate", i))
        tok = comm.part_grads_done(i, FFN, GB) if comm is not None else None
        dh1, dh1b, dg1, db1 = _mm_sum(
            [(du1, V(W["f_w_up"], (i,), cols=D_FF)), (du2, V(W["f_w_up"], (i,), c0=D_FF)), (dgl, V(W["ple_w_gate"], (i,)))],
            "nt", f"l{i}_dx1", add=dh2, add_scale=ALPHA, after=tok, ln_bwd=(L["h1"], W["ln_g"][i, 0][None], None))
        G["ln_g"][i] = jnp.concatenate([dg1, dg2], axis=0)
        G["ln_b"][i] = jnp.concatenate([db1, db2], axis=0)
        if i % 2 == 0:
            GB["e_w_out"] = _mm(L["ya"], dh1b, "tn", f"l{i}_dw_out_a", dst=into("e_w_out", j))
            GB["e_w_out"] = _mm(L["yb"], dh1b, "tn", f"l{i}_dw_out_b", dst=into("e_w_out", j, r0=D))
            dya = _mm(dh1b, V(W["e_w_out"], (j,), rows=D), "nt", f"l{i}_dya")
            dyb = _mm(dh1b, V(W["e_w_out"], (j,), r0=D), "nt", f"l{i}_dyb")
            dac, dga, dba = _ln_silu_bwd(L["ac"], dya, W["e_ln_a_g"][j][None], W["e_ln_a_b"][j][None], f"l{i}_ln_a_bwd")
            G["e_ln_a_g"][j], G["e_ln_a_b"][j] = dga[0], dba[0]
            dal, dag, dwa, dbca = _conv_a_bwd(L["ua"], W["e_conv_a_w"][j], dac, f"l{i}_conv_a_bwd")
            G["e_conv_a_w"][j] = _fold_rows(dwa, CONV_A)
            G["e_conv_a_b"][j] = dbca.sum(0)
            dy, dz, dgn = _gate_rms_bwd(L["y"], L["z"], dyb, W["e_norm_b_g"][j][None], f"l{i}_gate_rms_bwd")
            G["e_norm_b_g"][j] = dgn[0]
            dxs, dbs, dcs, sq_col, cms_row, dda_col, dda_row, ddp = _ssd_bwd(L["xc"], *L["sm"], W["e_d_skip"][j], L["states"], dy,
                                                                             f"l{i}_ssd_bwd")
            draw, G["e_dt_bias"][j], G["e_a_log"][j] = _ssd_small_grads(L["udt"][:, :N_HEADS], W["e_dt_bias"][j], W["e_a_log"][j],
                                                                       sq_col, cms_row, dda_col, dda_row)
            G["e_d_skip"][j] = ddp[:, :, 0, :].sum(0).reshape(N_HEADS, HEAD_P).sum(1)
            dxu, dwb, dbcb = _conv_b_bwd(L["xu"], W["e_conv_b_w"][j], W["e_conv_b_b"][j][None], dxs, dbs, dcs, f"l{i}_conv_b_bwd")
            G["e_conv_b_w"][j] = _fold_rows(dwb, CONV_B)
            G["e_conv_b_b"][j] = dbcb.sum(0)
            dudt = _padcols(draw, LANES)
            w_in = L["w_in"]
            xb_l = L["xb"]
            for nm, dseg, c0 in (("al", dal, 0), ("ag", dag, D), ("z", dz, SEG_Z[0]), ("xbc", dxu, SEG_X[0]), ("dt", dudt, SEG_DT[0])):
                GB["e_w_in"] = _mm(xb_l, dseg, "tn", f"l{i}_dw_in_{nm}", dst=into("e_w_in", j, c0=c0))
            dx = _mm_sum([(dal, w_in(SEG_A, cols=D)), (dag, w_in(SEG_A, c0=D, cols=D)), (dz, w_in(SEG_Z)),
                          (V(dxu, cols=D), w_in(SEG_X, cols=D)), (V(dxu, c0=D), w_in(SEG_X, c0=D, cols=D)), (dudt, w_in(SEG_DT))],
                         "nt", f"l{i}_dx", add=dh1, add_scale=ALPHA)
        else:
            GB["o_w_out"] = _mm(L["sc"], dh1b, "tn", f"l{i}_dw_out", dst=into("o_w_out", j))
            dsc = _mm(dh1b, V(W["o_w_out"], (j,)), "nt", f"l{i}_dsc")
            dbg, dcg, dv, dwc = _conv_c_bwd(L["uo"], W["o_conv_w"][j], dsc, f"l{i}_conv_c_bwd")
            G["o_conv_w"][j] = _fold_rows(dwc, CONV_C)
            xb_l = L["xb"]
            for nm, dseg, c0 in (("bg", dbg, 0), ("cg", dcg, D), ("v", dv, 2 * D)):
                GB["o_w_in"] = _mm(xb_l, dseg, "tn", f"l{i}_dw_in_{nm}", dst=into("o_w_in", j, c0=c0))
            below = saved[i - 1]
            ln2_done = _mm_sum([(dseg, V(W["o_w_in"], (j,), c0=c0, cols=D)) for dseg, c0 in ((dbg, 0), (dcg, D), (dv, 2 * D))],
                               "nt", f"l{i}_dx", add=dh1, add_scale=ALPHA,
                               ln_bwd=(below["h2"], W["ln_g"][i - 1, 1][None], (below["pp"], below["gl"])))
        tok = comm.part_grads_done(i, MIX, GB) if comm is not None else None
    grads = {n: jnp.stack(v) for n, v in G.items()}
    return sq, dx, GB, grads


def _ssd_small_inputs(raw, dt_bias, a_log):
    T = raw.shape[0]
    a = -jnp.exp(a_log)
    rg = raw.reshape(T, N_GROUPS, 4)
    raw_col = jnp.pad(jnp.transpose(rg, (1, 0, 2)), ((0, 0), (0, 0), (0, LANES - 4)))
    raw_row = jnp.pad(jnp.transpose(rg, (1, 2, 0)), ((0, 0), (0, SUBLANES - 4), (0, 0)))

    def colv(v):
        return jnp.pad(v.reshape(N_GROUPS, 1, 4), ((0, 0), (0, 0), (0, LANES - 4)))

    def rowv(v):
        return jnp.pad(v.reshape(N_GROUPS, 4, 1), ((0, 0), (0, SUBLANES - 4), (0, 0)))

    return raw_col, raw_row, colv(dt_bias), rowv(dt_bias), colv(a), rowv(a)


def _ssd_small_grads(raw, dt_bias, a_log, sq_col, cms_row, dda_col, dda_row):
    T = raw.shape[0]

    def join(col, row):
        c = jnp.transpose(col[:, :, :4], (1, 0, 2)).reshape(T, N_HEADS)
        r = jnp.transpose(row[:, :4, :], (2, 0, 1)).reshape(T, N_HEADS)
        return c + r

    a = -jnp.exp(a_log)
    pre = raw + dt_bias
    dt = jax.nn.softplus(pre)
    dda = join(dda_col, dda_row)
    ddt = join(sq_col, cms_row) + a * dda
    draw = ddt * jax.nn.sigmoid(pre)
    da = jnp.sum(dt * dda, axis=0)
    return draw, jnp.sum(draw, axis=0), da * a


def kernel(x, p, e_w_in, e_conv_a_w, e_conv_a_b, e_ln_a_g, e_ln_a_b, e_conv_b_w, e_conv_b_b, e_dt_bias, e_a_log, e_d_skip, e_norm_b_g, e_w_out, o_w_in, o_conv_w, o_w_out, f_w_up, f_conv_w, f_conv_b, f_w_down, ple_w_proj, ple_w_gate, ln_g, ln_b, loss_target, m_e_w_in, m_e_conv_a_w, m_e_conv_a_b, m_e_ln_a_g, m_e_ln_a_b, m_e_conv_b_w, m_e_conv_b_b, m_e_dt_bias, m_e_a_log, m_e_d_skip, m_e_norm_b_g, m_e_w_out, m_o_w_in, m_o_conv_w, m_o_w_out, m_f_w_up, m_f_conv_w, m_f_conv_b, m_f_w_down, m_ple_w_proj, m_ple_w_gate, m_ln_g, m_ln_b, v_e_w_in, v_e_conv_a_w, v_e_conv_a_b, v_e_ln_a_g, v_e_ln_a_b, v_e_conv_b_w, v_e_conv_b_b, v_e_dt_bias, v_e_a_log, v_e_d_skip, v_e_norm_b_g, v_e_w_out, v_o_w_in, v_o_conv_w, v_o_w_out, v_f_w_up, v_f_conv_w, v_f_conv_b, v_f_w_down, v_ple_w_proj, v_ple_w_gate, v_ln_g, v_ln_b):
    args = dict(locals())
    w_shard = {n: args[n] for n in WEIGHT_ORDER}
    m_shard = {n: args["m_" + n] for n in WEIGHT_ORDER}
    v_shard = {n: args["v_" + n] for n in WEIGHT_ORDER}
    xi, yi, ci = lax.axis_index("x"), lax.axis_index("y"), lax.axis_index("c")
    chip = 2 * xi + yi

    mevec = jnp.stack([chip, ci]).astype(jnp.int32)
    small_shapes = [w_shard[n].shape for n, _ in SMALL_SHARDED]
    sr = _small_rows(small_shapes)
    ws = _pack_rows([w_shard[n] for n, _ in SMALL_SHARDED], LANES, sr, F32)
    W = {n: w_shard[n] for n in SMALL_REPL}
    W.update({s[0]: Layers(s[2]) for s in TENSORS})
    w_items = [_group_items(layers) for layers in W_GROUPS]
    g_items = [_group_items(layers) for layers in G_GROUPS]

    def install(items, fulls):
        for it, f in zip(items, fulls):
            if it[0] == "e_w_in":
                f = jnp.transpose(f.reshape(it[2], 4, D, E_IN // 4), (0, 2, 1, 3)).reshape(it[2], D, E_IN)
                f = jnp.pad(f, ((0, 0), (0, 0), (0, E_PAD - E_IN)))
            W[it[0]].put(f, it[6])

    casts = [[_cast_into(w_shard[it[0]], it, mevec[:1]) for it in items] for items in w_items]
    fulls, ssem, rsem, _ = _gather_start(casts[0], w_items[0], "gather_start_0")
    xb0 = _to_bf16(x[0], "x_bf16")
    fulls = _gather_wait(fulls, ssem, rsem, [c for grp in casts[1:] for c in grp] + [xb0], w_items[0], "gather_wait_0")
    fulls, WS = _gather_fwd(fulls, w_items[0], "gather_fwd_0", ws)
    install(w_items[0], fulls)
    parts_s = [_unpack_rows(WS[k], small_shapes) for k in range(4)]
    for idx, (n, ax) in enumerate(SMALL_SHARDED):
        W[n] = jnp.concatenate([parts_s[k][idx] for k in range(4)], axis=ax)

    class Comm:
        sent = {}
        started = {}
        tail = fulls[0]

        def start_next(self, gi):
            if gi >= len(w_items):
                return None
            self.started[gi] = _gather_start(casts[gi], w_items[gi], f"gather_start_{gi}", self.tail)
            return self.started[gi][3]

        def part_starts(self, layer, part, after):
            if (layer, part) == W_GROUPS[0][0]:
                return self.start_next(1)
            for gi in range(1, len(W_GROUPS)):
                if W_GROUPS[gi][0] == (layer, part):
                    fulls, ssem, rsem, _ = self.started[gi]
                    fulls = _gather_wait(fulls, ssem, rsem, after, w_items[gi], f"gather_wait_{gi}")
                    fulls, _ = _gather_fwd(fulls, w_items[gi], f"gather_fwd_{gi}")
                    install(w_items[gi], fulls)
                    self.tail = fulls[0]
                    return self.start_next(gi + 1)
            return None

        swapping = None

        def swap_landed(self, after):
            if self.swapping is None:
                return None
            gi, thru, ssem, rsem = self.swapping
            items = g_items[gi]
            thru = _swap_wait(thru, ssem, rsem, after, items, f"swap_wait_{gi}")
            gs, ras = thru[:len(items)], thru[len(items):]
            ps = [_add_half(g, ra, it, mevec[1:]) for g, ra, it in zip(gs, ras, items)]
            thru, ssem, rsem, tok = _scatter_start(ps, items, f"scatter_start_{gi}")
            self.sent[gi] = (thru, ssem, rsem, tok)
            self.swapping = None
            return tok

        def part_grads_done(self, layer, part, GB):
            tok = self.swap_landed(GB.last)
            for gi, members in enumerate(G_GROUPS):
                if members[-1] == (layer, part):
                    items = g_items[gi]
                    gs = []
                    for it in items:
                        g = GB[(it[0], gi)]
                        if it[0] == "e_w_in":
                            g = jnp.transpose(g[:, :, :E_IN].reshape(it[2], D, 4, E_IN // 4), (0, 2, 1, 3)).reshape(it[2], 4 * D, E_IN // 4)
                        gs.append(g)
                    thru, ssem, rsem, tok = _swap_start(gs, items, f"swap_start_{gi}", tok)
                    self.swapping = (gi, thru, ssem, rsem)
            return tok

    comm = Comm()

    sq, dx, GB, G = _local_step(x[0], p, loss_target[0], W, comm, xb0)
    loss = lax.psum(0.5 * sq[0, 0] / D, ("x", "y", "c"))
    grad_x = dx[None]

    def shard_of(g, ax, k):
        n = g.shape[ax] // 4
        return lax.slice_in_dim(g, k * n, (k + 1) * n, axis=ax)

    reduced, updated = {}, {}
    comm.swap_landed(dx)
    after = comm.sent[len(g_items) - 1][3]
    for gi, items in enumerate(g_items):
        thru, ssem, rsem, _ = comm.sent[gi]
        thru = _scatter_wait(thru, ssem, rsem, after, items, f"scatter_wait_{gi}")
        ps, rcs = thru[:len(items)], thru[len(items):]
        rs = [_sum_own(pt, rc, it, mevec, reduced.get(it[0])) for pt, rc, it in zip(ps, rcs, items)]
        rs = _join_halves(rs, items, f"join_halves_{gi}")
        reduced.update({it[0]: r for it, r in zip(items, rs)})
        for it in items:
            n = it[0]
            if n != "e_w_in":
                updated[n] = _adamw_layers(w_shard[n], reduced[n], m_shard[n], v_shard[n], it[6], it[2], updated.get(n),
                                           f"adamw_{n}_{it[6]}")
        after = updated[items[-1][0]][0]
    *updated["e_w_in"], reduced["e_w_in"] = _adamw_minor_rows(w_shard["e_w_in"], reduced["e_w_in"], m_shard["e_w_in"],
                                                              v_shard["e_w_in"], "adamw_e_w_in")

    small_all = ([shard_of(G[n], ax, k) for k in range(4) for n, ax in SMALL_SHARDED] + [G[n] for n in SMALL_REPL])
    small_all_shapes = [t.shape for t in small_all]
    mr = _small_rows(small_all_shapes)
    sg = _sum8(_allgather_small(_pack_rows(small_all, LANES, mr, F32)), mr)
    sparts = _unpack_rows(sg, small_all_shapes)
    ns = len(SMALL_SHARDED)
    gsmall = {}
    for idx, (n, ax) in enumerate(SMALL_SHARDED):
        stacked = jnp.stack([sparts[k * ns + idx] for k in range(4)])
        gsmall[n] = lax.dynamic_index_in_dim(stacked, chip, axis=0, keepdims=False)
    for idx, n in enumerate(SMALL_REPL):
        gsmall[n] = sparts[4 * ns + idx]

    grads, deltas, new_m, new_v = [], [], [], []
    for n in WEIGHT_ORDER:
        if n in reduced:
            g, (d, mn, vn) = reduced[n], updated[n]
        else:
            g = gsmall[n]
            d, mn, vn = _adamw(w_shard[n], g, m_shard[n], v_shard[n], f"adamw_{n}")
        grads.append(g)
        deltas.append(d)
        new_m.append(mn)
        new_v.append(vn)
    return (loss, grad_x, *grads, *deltas, *new_m, *new_v)
```

```python
import functools
import math

import jax
import jax.numpy as jnp
from jax import lax
from jax.experimental import pallas as pl
from jax.experimental.pallas import tpu as pltpu

F32 = jnp.float32
BF16 = jnp.bfloat16
MESH = pl.DeviceIdType.MESH

DEPTH = 4
ALPHA = (2.0 * DEPTH) ** 0.25
LN_EPS = 1e-5
D = 1024
HEAD_P = 64
N_STATE = 128
N_HEADS = 16
N_GROUPS = 4
CONV_A, CONV_B, CONV_C, CONV_F = 31, 4, 3, 3
D_FF = 2816
PLE = 256
E_IN = 5136

ADAM_LR, ADAM_B1, ADAM_B2, ADAM_EPS, ADAM_WD, ADAM_STEP = 0.001, 0.9, 0.999, 1e-08, 0.01, 10

LANES = 128
SUBLANES = 8
VMEM_LIMIT = 56 * 1024 * 1024
SSD_Q = 128
CONV_R = 128
CONV_PAD = 32
ROW_T = 256


def _cparams(sem=None):
    return pltpu.CompilerParams(dimension_semantics=sem, vmem_limit_bytes=VMEM_LIMIT)


def _sig(v):
    return jax.nn.sigmoid(v)


_DIMS = {"nn": (((1,), (0,)), ((), ())), "nt": (((1,), (1,)), ((), ())), "tn": (((0,), (0,)), ((), ()))}


class Layers:
    def __init__(self, n_layers):
        self.where = [None] * n_layers

    def put(self, arr, l0):
        for k in range(arr.shape[0]):
            self.where[l0 + k] = (arr, k)


class V:
    def __init__(self, arr, lead=(), r0=0, c0=0, rows=None, cols=None):
        if isinstance(arr, Layers):
            arr, k = arr.where[lead[0]]
            lead = (k,) + tuple(lead[1:])
        self.arr, self.lead, self.r0, self.c0 = arr, tuple(lead), r0, c0
        R, C = arr.shape[-2:]
        self.rows = R - r0 if rows is None else rows
        self.cols = C - c0 if cols is None else cols

    def spec(self, br, bc, fn):
        assert self.r0 % br == 0 and self.c0 % bc == 0, (self.r0, self.c0, br, bc)
        ro, co, lead = self.r0 // br, self.c0 // bc, self.lead

        def index(i, j, k):
            r, c = fn(i, j, k)
            return lead + (r + ro, c + co)

        return pl.BlockSpec((None,) * len(lead) + (br, bc), index)


def _v(t):
    return t if isinstance(t, V) else V(t)


def _tile(n, offs, cands):
    for c in cands:
        if n % c == 0 and all(o % c == 0 for o in offs):
            return c
    raise ValueError((n, offs))


_TILES = (1024, 1408, 512, 256, 128)


def _mm(a, b, mode, name, out_dtype=F32, add=None, add_scale=1.0, dst=None, after=None):
    a, b = _v(a), _v(b)
    add = _v(add) if add is not None else None
    if mode == "nn":
        M, K, K2, N = a.rows, a.cols, b.rows, b.cols
        am, ak, bk, bn = a.r0, a.c0, b.r0, b.c0
    elif mode == "nt":
        M, K, N, K2 = a.rows, a.cols, b.rows, b.cols
        am, ak, bn, bk = a.r0, a.c0, b.r0, b.c0
    else:
        K, M, K2, N = a.rows, a.cols, b.rows, b.cols
        ak, am, bk, bn = a.r0, a.c0, b.r0, b.c0
    assert K == K2, (name, mode, M, K, K2, N)
    if dst is None:
        buf, full_shape, o_lead, o_r0, o_c0 = None, (M, N), (), 0, 0
    else:
        buf, full_shape, o_lead, o_r0, o_c0 = dst
    tm = _tile(M, [am, o_r0] + ([add.r0] if add else []), _TILES)
    tn = _tile(N, [bn, o_c0] + ([add.c0] if add else []), _TILES)
    narrow = a.arr.dtype.itemsize == 2 and b.arr.dtype.itemsize == 2
    tk = _tile(K, [ak, bk], ((2048,) if narrow else ()) + _TILES)
    nk = K // tk
    has_add, has_buf, has_after = add is not None, buf is not None, after is not None

    def body(*refs):
        a_ref, b_ref = refs[0], refs[1]
        add_ref = refs[2] if has_add else None
        o_ref = refs[2 + has_add + has_buf + has_after]

        def finish(r):
            if has_add:
                r = r + add_scale * add_ref[...].astype(F32)
            o_ref[...] = r.astype(o_ref.dtype)

        part = lax.dot_general(a_ref[...].astype(BF16), b_ref[...].astype(BF16), _DIMS[mode], preferred_element_type=F32)
        if nk == 1:
            finish(part)
        else:
            acc_ref = refs[-1]
            k = pl.program_id(2)

            @pl.when(k == 0)
            def _():
                acc_ref[...] = part

            @pl.when(jnp.logical_and(k > 0, k < nk - 1))
            def _():
                acc_ref[...] += part

            @pl.when(k == nk - 1)
            def _():
                finish(acc_ref[...] + part)

    if mode == "tn":
        a_spec = a.spec(tk, tm, lambda i, j, k: (k, i))
    else:
        a_spec = a.spec(tm, tk, lambda i, j, k: (i, k))
    if mode == "nt":
        b_spec = b.spec(tn, tk, lambda i, j, k: (j, k))
    else:
        b_spec = b.spec(tk, tn, lambda i, j, k: (k, j))
    in_specs, args = [a_spec, b_spec], [a.arr, b.arr]
    if has_add:
        in_specs.append(add.spec(tm, tn, lambda i, j, k: (i, j)))
        args.append(add.arr)
    aliases = {}
    if has_buf:
        aliases = {len(args): 0}
        in_specs.append(pl.BlockSpec(memory_space=pl.ANY))
        args.append(buf)
        out_dtype = buf.dtype
    if has_after:
        in_specs.append(pl.BlockSpec(memory_space=pl.ANY))
        args.append(after)
    o_view = V(jax.ShapeDtypeStruct(full_shape, out_dtype), o_lead, o_r0, o_c0, M, N)
    return pl.pallas_call(
        body, name=name, grid=(M // tm, N // tn, nk), in_specs=in_specs, out_specs=o_view.spec(tm, tn, lambda i, j, k: (i, j)),
        out_shape=jax.ShapeDtypeStruct(full_shape, out_dtype), input_output_aliases=aliases,
        scratch_shapes=[pltpu.VMEM((tm, tn), F32)] if nk > 1 else [],
        compiler_params=_cparams(("parallel", "parallel", "arbitrary")))(*args)


def _ln_stats(h):
    mu = jnp.mean(h, axis=-1, keepdims=True)
    hc = h - mu
    var = jnp.mean(hc * hc, axis=-1, keepdims=True)
    rstd = lax.rsqrt(var + LN_EPS)
    return hc * rstd, rstd


def _ln_bwd_math(dyv, h, g):
    xhat, rstd = _ln_stats(h)
    dxh = dyv * g
    dh = rstd * (dxh - jnp.mean(dxh, axis=-1, keepdims=True) - xhat * jnp.mean(dxh * xhat, axis=-1, keepdims=True))
    return dh, jnp.sum(dyv * xhat, axis=0, keepdims=True), jnp.sum(dyv, axis=0, keepdims=True)


def _mm_sum(pairs, mode, name, out_dtype=F32, add=None, add_scale=1.0, after=None, ln_fwd=None, ln_bwd=None):
    pairs = [(_v(a), _v(b)) for a, b in pairs]
    add = _v(add) if add is not None else None
    M = pairs[0][0].rows
    N = pairs[0][1].cols if mode == "nn" else pairs[0][1].rows
    b_offs = [(b.c0 if mode == "nn" else b.r0) for _, b in pairs]
    fused = ln_fwd is not None or ln_bwd is not None
    tn = N if fused else _tile(N, b_offs + ([add.c0] if add else []), (512, 256, 128))
    ple_n = 2 if fused and (ln_fwd[1] if ln_fwd is not None else ln_bwd[2]) is not None else 0

    def footprint(tm):
        ab = sum(tm * a.cols * a.arr.dtype.itemsize + a.cols * tn * b.arr.dtype.itemsize for a, b in pairs)
        io = tm * tn * 4 * (add is not None)
        if ln_fwd is not None:
            io += tm * tn * (4 * (1 + ple_n) + 10)
        elif ln_bwd is not None:
            io += tm * tn * (4 * (1 + ple_n) + 6 + 2 * ple_n)
        else:
            io += tm * tn * 4
        return 2 * (ab + io)

    offs = [a.r0 for a, _ in pairs] + ([add.r0] if add else [])
    fits = [t for t in (512, 256, 128) if M % t == 0 and all(o % t == 0 for o in offs) and footprint(t) <= (VMEM_LIMIT * 3) // 4]
    tm = fits[0] if fits else _tile(M, offs, (128,))
    assert not fused or (N == D and all(o == 0 for o in b_offs))
    n_p, has_add, has_after = len(pairs), add is not None, after is not None
    ple = (ln_fwd[1] if ln_fwd is not None else ln_bwd[2]) if fused else None
    has_ple = ple is not None

    def body(*refs):
        acc = None
        for i in range(n_p):
            part = lax.dot_general(refs[2 * i][...].astype(BF16), refs[2 * i + 1][...].astype(BF16), _DIMS[mode],
                                   preferred_element_type=F32)
            acc = part if acc is None else acc + part
        pos = 2 * n_p
        if has_add:
            acc = acc + add_scale * refs[pos][...].astype(F32)
            pos += 1
        if ln_fwd is not None:
            x_ref = refs[pos]
            pp_ref, gl_ref = (refs[pos + 1], refs[pos + 2]) if has_ple else (None, None)
            pos += 1 + 2 * has_ple
            g_ref, b_ref = refs[pos], refs[pos + 1]
            h_ref, y_ref, yb_ref = refs[pos + 2 + has_after:]
            h = ALPHA * x_ref[...] + acc
            if has_ple:
                h = h + pp_ref[...] * _sig(gl_ref[...])
            xhat, _ = _ln_stats(h)
            y = xhat * g_ref[...] + b_ref[...]
            h_ref[...] = h
            y_ref[...] = y
            yb_ref[...] = y.astype(BF16)
        elif ln_bwd is not None:
            h_ref, g_ref = refs[pos], refs[pos + 1]
            pp_ref, gl_ref = (refs[pos + 2], refs[pos + 3]) if has_ple else (None, None)
            outs = refs[pos + 2 + 2 * has_ple + has_after:]
            dh_ref, dhb_ref, dg_ref, db_ref = outs[:4]

            @pl.when(pl.program_id(0) == 0)
            def _():
                dg_ref[...] = jnp.zeros_like(dg_ref)
                db_ref[...] = jnp.zeros_like(db_ref)

            dh, dg, db = _ln_bwd_math(acc, h_ref[...], g_ref[...])
            dg_ref[...] += dg
            db_ref[...] += db
            dh_ref[...] = dh
            dhb_ref[...] = dh.astype(BF16)
            if has_ple:
                s = _sig(gl_ref[...])
                outs[4][...] = (dh * s).astype(BF16)
                outs[5][...] = (dh * pp_ref[...] * s * (1.0 - s)).astype(BF16)
        else:
            o_ref = refs[pos + has_after]
            o_ref[...] = acc.astype(o_ref.dtype)

    in_specs, args = [], []
    for a, b in pairs:
        K = a.cols
        assert K == (b.rows if mode == "nn" else b.cols), (name, K)
        in_specs.append(a.spec(tm, K, lambda i, j, k: (i, 0)))
        in_specs.append(b.spec(K, tn, lambda i, j, k: (0, j)) if mode == "nn" else b.spec(tn, K, lambda i, j, k: (j, 0)))
        args += [a.arr, b.arr]
    if has_add:
        in_specs.append(add.spec(tm, tn, lambda i, j, k: (i, j)))
        args.append(add.arr)
    row = pl.BlockSpec((tm, tn), lambda i, j, k: (i, j))
    vec = pl.BlockSpec((1, tn), lambda i, j, k: (0, 0))
    if ln_fwd is not None:
        x, _, g, b = ln_fwd
        extra = [x] + (list(ple) if has_ple else []) + [g, b]
        in_specs += [row] * (1 + 2 * has_ple) + [vec, vec]
        args += extra
        out_specs = (row, row, row)
        out_shape = (_sds((M, N)), _sds((M, N)), _sds((M, N), BF16))
    elif ln_bwd is not None:
        h, g, _ = ln_bwd
        in_specs += [row, vec] + [row] * (2 * has_ple)
        args += [h, g] + (list(ple) if has_ple else [])
        out_specs = (row, row, vec, vec) + ((row, row) if has_ple else ())
        out_shape = (_sds((M, N)), _sds((M, N), BF16), _sds((1, N)), _sds((1, N))) + ((_sds((M, N), BF16),) * 2 if has_ple else ())
    else:
        out_specs, out_shape = row, jax.ShapeDtypeStruct((M, N), out_dtype)
    if has_after:
        in_specs.append(pl.BlockSpec(memory_space=pl.ANY))
        args.append(after)
    return pl.pallas_call(
        body, name=name, grid=(M // tm, N // tn, 1), in_specs=in_specs, out_specs=out_specs, out_shape=out_shape,
        compiler_params=_cparams(("arbitrary",) * 3 if ln_bwd is not None else ("parallel", "parallel", "arbitrary")))(*args)


def _rows(T, width=D):
    return pl.BlockSpec((ROW_T, width), lambda i: (i, 0))


def _vec(width=D):
    return pl.BlockSpec((1, width), lambda i: (0, 0))


def _res_ln_fwd(x, adds, ple, g, b, name):
    T = x.shape[0]
    n_add = len(adds)
    has_ple = ple is not None

    def body(*refs):
        x_ref = refs[0]
        add_refs = refs[1:1 + n_add]
        pos = 1 + n_add
        if has_ple:
            pp_ref, gl_ref = refs[pos], refs[pos + 1]
            pos += 2
        g_ref, b_ref, h_ref, y_ref, yb_ref = refs[pos:pos + 5]
        h = ALPHA * x_ref[...]
        for r in add_refs:
            h = h + r[...]
        if has_ple:
            h = h + pp_ref[...] * _sig(gl_ref[...])
        xhat, _ = _ln_stats(h)
        y = xhat * g_ref[...] + b_ref[...]
        h_ref[...] = h
        y_ref[...] = y
        yb_ref[...] = y.astype(BF16)

    n_in = 1 + n_add + (2 if has_ple else 0)
    args = (x,) + tuple(adds) + (tuple(ple) if has_ple else ()) + (g, b)
    return pl.pallas_call(
        body, name=name, grid=(T // ROW_T,), in_specs=[_rows(T)] * n_in + [_vec(), _vec()],
        out_specs=(_rows(T), _rows(T), _rows(T)),
        out_shape=(jax.ShapeDtypeStruct((T, D), F32), jax.ShapeDtypeStruct((T, D), F32), jax.ShapeDtypeStruct((T, D), BF16)),
        compiler_params=_cparams(("parallel",)))(*args)


def _res_ln_bwd(dy, h, g, ple, name):
    T = dy.shape[0]
    has_ple = ple is not None

    def body(*refs):
        if has_ple:
            dy_ref, h_ref, g_ref, pp_ref, gl_ref, dh_ref, dhb_ref, dg_ref, db_ref, dpp_ref, dgl_ref = refs
        else:
            dy_ref, h_ref, g_ref, dh_ref, dhb_ref, dg_ref, db_ref = refs
        i = pl.program_id(0)

        @pl.when(i == 0)
        def _():
            dg_ref[...] = jnp.zeros_like(dg_ref)
            db_ref[...] = jnp.zeros_like(db_ref)

        dyv = dy_ref[...]
        xhat, rstd = _ln_stats(h_ref[...])
        dg_ref[...] += jnp.sum(dyv * xhat, axis=0, keepdims=True)
        db_ref[...] += jnp.sum(dyv, axis=0, keepdims=True)
        dxh = dyv * g_ref[...]
        dh = rstd * (dxh - jnp.mean(dxh, axis=-1, keepdims=True) - xhat * jnp.mean(dxh * xhat, axis=-1, keepdims=True))
        dh_ref[...] = dh
        dhb_ref[...] = dh.astype(BF16)
        if has_ple:
            s = _sig(gl_ref[...])
            dpp_ref[...] = (dh * s).astype(BF16)
            dgl_ref[...] = (dh * pp_ref[...] * s * (1.0 - s)).astype(BF16)

    args = (dy, h, g) + (tuple(ple) if has_ple else ())
    in_specs = [_rows(T), _rows(T), _vec()] + ([_rows(T), _rows(T)] if has_ple else [])
    out_specs = [_rows(T), _rows(T), _vec(), _vec()] + ([_rows(T), _rows(T)] if has_ple else [])
    out_shape = [jax.ShapeDtypeStruct((T, D), F32), jax.ShapeDtypeStruct((T, D), BF16),
                 jax.ShapeDtypeStruct((1, D), F32), jax.ShapeDtypeStruct((1, D), F32)]
    if has_ple:
        out_shape += [jax.ShapeDtypeStruct((T, D), BF16), jax.ShapeDtypeStruct((T, D), BF16)]
    return pl.pallas_call(
        body, name=name, grid=(T // ROW_T,), in_specs=in_specs, out_specs=tuple(out_specs), out_shape=tuple(out_shape),
        compiler_params=_cparams(("arbitrary",)))(*args)


def _ln_silu_fwd(ac, g, b, name):
    T = ac.shape[0]

    def body(a_ref, g_ref, b_ref, o_ref):
        xhat, _ = _ln_stats(a_ref[...])
        ln = xhat * g_ref[...] + b_ref[...]
        o_ref[...] = (ln * _sig(ln)).astype(BF16)

    return pl.pallas_call(
        body, name=name, grid=(T // ROW_T,), in_specs=[_rows(T), _vec(), _vec()], out_specs=_rows(T),
        out_shape=jax.ShapeDtypeStruct((T, D), BF16), compiler_params=_cparams(("parallel",)))(ac, g, b)


def _ln_silu_bwd(ac, dya, g, b, name):
    T = ac.shape[0]

    def body(a_ref, d_ref, g_ref, b_ref, da_ref, dg_ref, db_ref):
        i = pl.program_id(0)

        @pl.when(i == 0)
        def _():
            dg_ref[...] = jnp.zeros_like(dg_ref)
            db_ref[...] = jnp.zeros_like(db_ref)

        xhat, rstd = _ln_stats(a_ref[...])
        ln = xhat * g_ref[...] + b_ref[...]
        s = _sig(ln)
        dln = d_ref[...] * s * (1.0 + ln * (1.0 - s))
        dg_ref[...] += jnp.sum(dln * xhat, axis=0, keepdims=True)
        db_ref[...] += jnp.sum(dln, axis=0, keepdims=True)
        dxh = dln * g_ref[...]
        da_ref[...] = rstd * (dxh - jnp.mean(dxh, axis=-1, keepdims=True)
                              - xhat * jnp.mean(dxh * xhat, axis=-1, keepdims=True))

    return pl.pallas_call(
        body, name=name, grid=(T // ROW_T,), in_specs=[_rows(T), _rows(T), _vec(), _vec()],
        out_specs=(_rows(T), _vec(), _vec()),
        out_shape=(jax.ShapeDtypeStruct((T, D), F32), jax.ShapeDtypeStruct((1, D), F32), jax.ShapeDtypeStruct((1, D), F32)),
        compiler_params=_cparams(("arbitrary",)))(ac, dya, g, b)


def _gate_rms_fwd(y, z, g, name):
    T = y.shape[0]

    def body(y_ref, z_ref, g_ref, o_ref):
        zv = z_ref[...]
        yg = y_ref[...] * (zv * _sig(zv))
        r = lax.rsqrt(jnp.mean(yg * yg, axis=-1, keepdims=True) + LN_EPS)
        o_ref[...] = (yg * r * g_ref[...]).astype(BF16)

    return pl.pallas_call(
        body, name=name, grid=(T // ROW_T,), in_specs=[_rows(T), _rows(T), _vec()], out_specs=_rows(T),
        out_shape=jax.ShapeDtypeStruct((T, D), BF16), compiler_params=_cparams(("parallel",)))(y, z, g)


def _gate_rms_bwd(y, z, dout, g, name):
    T = y.shape[0]

    def body(y_ref, z_ref, d_ref, g_ref, dy_ref, dz_ref, dg_ref):
        i = pl.program_id(0)

        @pl.when(i == 0)
        def _():
            dg_ref[...] = jnp.zeros_like(dg_ref)

        yv, zv, dv = y_ref[...], z_ref[...], d_ref[...]
        s = _sig(zv)
        sz = zv * s
        yg = yv * sz
        r = lax.rsqrt(jnp.mean(yg * yg, axis=-1, keepdims=True) + LN_EPS)
        dg_ref[...] += jnp.sum(dv * yg * r, axis=0, keepdims=True)
        dn = dv * g_ref[...]
        dyg = r * dn - yg * (r * r * r) * jnp.mean(dn * yg, axis=-1, keepdims=True)
        dy_ref[...] = dyg * sz
        dz_ref[...] = (dyg * yv * s * (1.0 + zv * (1.0 - s))).astype(BF16)

    return pl.pallas_call(
        body, name=name, grid=(T // ROW_T,), in_specs=[_rows(T), _rows(T), _rows(T), _vec()],
        out_specs=(_rows(T), _rows(T), _vec()),
        out_shape=(jax.ShapeDtypeStruct((T, D), F32), jax.ShapeDtypeStruct((T, D), BF16), jax.ShapeDtypeStruct((1, D), F32)),
        compiler_params=_cparams(("arbitrary",)))(y, z, dout, g)


def _to_bf16(x, name):
    T = x.shape[0]

    def body(x_ref, o_ref):
        o_ref[...] = x_ref[...].astype(BF16)

    return pl.pallas_call(body, name=name, grid=(T // ROW_T,), in_specs=[_rows(T)], out_specs=_rows(T),
                          out_shape=jax.ShapeDtypeStruct((T, D), BF16), compiler_params=_cparams(("parallel",)))(x)


def _loss_head(y, target, name):
    T = y.shape[0]

    def body(y_ref, t_ref, s_ref, d_ref):
        i = pl.program_id(0)

        @pl.when(i == 0)
        def _():
            s_ref[...] = jnp.zeros_like(s_ref)

        err = y_ref[...] - t_ref[...]
        s_ref[...] += jnp.sum(jnp.sum(err * err, axis=1, keepdims=True), axis=0, keepdims=True)
        d_ref[...] = err * (1.0 / D)

    return pl.pallas_call(
        body, name=name, grid=(T // ROW_T,), in_specs=[_rows(T), _rows(T)],
        out_specs=(pl.BlockSpec((SUBLANES, LANES), lambda i: (0, 0)), _rows(T)),
        out_shape=(jax.ShapeDtypeStruct((SUBLANES, LANES), F32), jax.ShapeDtypeStruct((T, D), F32)),
        compiler_params=_cparams(("arbitrary",)))(y, target)


def _taps_fwd(pad_ref, w_ref, K, base):
    off = CONV_PAD - (K - 1)
    acc = w_ref[0:1, :] * pad_ref[pl.ds(base + off, CONV_R), :]
    for k in range(1, K):
        acc = acc + w_ref[k:k + 1, :] * pad_ref[pl.ds(base + off + k, CONV_R), :]
    return acc


def _taps_bwd(padd_ref, w_ref, K, base):
    acc = w_ref[0:1, :] * padd_ref[pl.ds(base + (K - 1), CONV_R), :]
    for k in range(1, K):
        acc = acc + w_ref[k:k + 1, :] * padd_ref[pl.ds(base + (K - 1) - k, CONV_R), :]
    return acc


def _f32(ref, rows):
    return ref[rows, :].astype(F32)


def _fold8(v):
    return v.reshape(CONV_R // SUBLANES, SUBLANES, v.shape[-1]).sum(0)


def _wgrad_acc(dw_ref, pad_ref, d, K, base):
    off = CONV_PAD - (K - 1)
    for k in range(K):
        dw_ref[k * SUBLANES:(k + 1) * SUBLANES, :] += _fold8(d * pad_ref[pl.ds(base + off + k, CONV_R), :])


def _loop_rows(T, fn):
    def step(r, carry):
        fn(pl.multiple_of(r * CONV_R, CONV_R))
        return carry
    lax.fori_loop(0, T // CONV_R, step, 0)


def _col(T, off_blocks=0, rows=None):
    return pl.BlockSpec((T if rows is None else rows, LANES), lambda j: (0, j + off_blocks))


def _conv_call(body, name, T, n_tiles, in_specs, out_specs, out_shape, n_pad, n_padd=0):
    scratch = [pltpu.VMEM((T + CONV_PAD, LANES), F32)] * (n_pad + n_padd)
    return pl.pallas_call(body, name=name, grid=(n_tiles,), in_specs=in_specs, out_specs=out_specs, out_shape=out_shape,
                          scratch_shapes=scratch, compiler_params=_cparams(("parallel",)))


def _zero_head(ref):
    ref[0:CONV_PAD, :] = jnp.zeros((CONV_PAD, LANES), F32)


def _zero_tail(ref, T):
    ref[T:T + CONV_PAD, :] = jnp.zeros((CONV_PAD, LANES), F32)


def _sds(shape, dtype=F32):
    return jax.ShapeDtypeStruct(shape, dtype)


def _conv_a_fwd(ua, w, b, name):
    T = ua.shape[0]
    K, nt = CONV_A, D // LANES

    def body(al_ref, ag_ref, w_ref, b_ref, o_ref, pad_ref):
        _zero_head(pad_ref)

        def pre(base):
            rows = pl.ds(base, CONV_R)
            pad_ref[pl.ds(base + CONV_PAD, CONV_R), :] = _f32(al_ref, rows) * _sig(_f32(ag_ref, rows))
        _loop_rows(T, pre)

        def main(base):
            o_ref[pl.ds(base, CONV_R), :] = _taps_fwd(pad_ref, w_ref, K, base) + b_ref[...]
        _loop_rows(T, main)

    return _conv_call(body, name, T, nt, [_col(T), _col(T, nt), _col(T, rows=K), _col(T, rows=1)], _col(T),
                      _sds((T, D)), 1)(ua, ua, w, b)


def _conv_a_bwd(ua, w, dac, name):
    T = ua.shape[0]
    K, nt = CONV_A, D // LANES

    def body(al_ref, ag_ref, w_ref, d_ref, dal_ref, dag_ref, dw_ref, db_ref, pad_ref, padd_ref):
        _zero_head(pad_ref)
        _zero_tail(padd_ref, T)
        dw_ref[...] = jnp.zeros_like(dw_ref)
        db_ref[...] = jnp.zeros_like(db_ref)

        def pre(base):
            rows = pl.ds(base, CONV_R)
            pad_ref[pl.ds(base + CONV_PAD, CONV_R), :] = _f32(al_ref, rows) * _sig(_f32(ag_ref, rows))
            padd_ref[rows, :] = d_ref[rows, :]
        _loop_rows(T, pre)

        def main(base):
            rows = pl.ds(base, CONV_R)
            d = d_ref[rows, :]
            _wgrad_acc(dw_ref, pad_ref, d, K, base)
            db_ref[...] += _fold8(d)
            da = _taps_bwd(padd_ref, w_ref, K, base)
            al, s = _f32(al_ref, rows), _sig(_f32(ag_ref, rows))
            dal_ref[rows, :] = (da * s).astype(BF16)
            dag_ref[rows, :] = (da * al * s * (1.0 - s)).astype(BF16)
        _loop_rows(T, main)

    return _conv_call(body, name, T, nt, [_col(T), _col(T, nt), _col(T, rows=K), _col(T)],
                      (_col(T), _col(T), _col(T, rows=K * SUBLANES), _col(T, rows=SUBLANES)),
                      (_sds((T, D), BF16), _sds((T, D), BF16), _sds((K * SUBLANES, D)), _sds((SUBLANES, D))), 1, 1)(ua, ua, w, dac)


def _conv_b_fwd(xu, w, b, name):
    T, C = xu.shape
    K, nt = CONV_B, C // LANES

    def body(x_ref, w_ref, b_ref, o_ref, pad_ref):
        _zero_head(pad_ref)
        pad_ref[CONV_PAD:CONV_PAD + T, :] = x_ref[...].astype(F32)

        def main(base):
            hc = _taps_fwd(pad_ref, w_ref, K, base) + b_ref[...]
            o_ref[pl.ds(base, CONV_R), :] = hc * _sig(hc)
        _loop_rows(T, main)

    return _conv_call(body, name, T, nt, [_col(T), _col(T, rows=K), _col(T, rows=1)], _col(T), _sds((T, C)), 1)(xu, w, b)


def _conv_b_bwd(xu, w, b, dxs, dbs, dcs, name):
    T, C = xu.shape
    K, nt = CONV_B, C // LANES
    nx, nb = dxs.shape[1] // LANES, dbs.shape[1] // LANES

    def body(x_ref, w_ref, b_ref, d1_ref, d2_ref, d3_ref, dx_ref, dw_ref, db_ref, pad_ref, padd_ref):
        j = pl.program_id(0)
        _zero_head(pad_ref)
        _zero_tail(padd_ref, T)
        dw_ref[...] = jnp.zeros_like(dw_ref)
        db_ref[...] = jnp.zeros_like(db_ref)
        pad_ref[CONV_PAD:CONV_PAD + T, :] = x_ref[...].astype(F32)

        def pre(base):
            rows = pl.ds(base, CONV_R)
            hc = _taps_fwd(pad_ref, w_ref, K, base) + b_ref[...]
            s = _sig(hc)
            d = jnp.where(j < nx, d1_ref[rows, :], jnp.where(j < nx + nb, d2_ref[rows, :], d3_ref[rows, :]))
            padd_ref[rows, :] = d * s * (1.0 + hc * (1.0 - s))
        _loop_rows(T, pre)

        def main(base):
            d = padd_ref[pl.ds(base, CONV_R), :]
            _wgrad_acc(dw_ref, pad_ref, d, K, base)
            db_ref[...] += _fold8(d)
            dx_ref[pl.ds(base, CONV_R), :] = _taps_bwd(padd_ref, w_ref, K, base).astype(BF16)
        _loop_rows(T, main)

    def piece(lo, n):
        return pl.BlockSpec((T, LANES), lambda j: (0, jnp.clip(j - lo, 0, n - 1)))

    return _conv_call(body, name, T, nt,
                      [_col(T), _col(T, rows=K), _col(T, rows=1), piece(0, nx), piece(nx, nb), piece(nx + nb, nt - nx - nb)],
                      (_col(T), _col(T, rows=K * SUBLANES), _col(T, rows=SUBLANES)),
                      (_sds((T, C), BF16), _sds((K * SUBLANES, C)), _sds((SUBLANES, C))), 1, 1)(xu, w, b, dxs, dbs, dcs)


def _conv_c_fwd(uo, w, name):
    T = uo.shape[0]
    K, nt = CONV_C, D // LANES

    def body(bg_ref, cg_ref, v_ref, w_ref, o_ref, pad_ref):
        _zero_head(pad_ref)
        pad_ref[CONV_PAD:CONV_PAD + T, :] = cg_ref[...].astype(F32) * v_ref[...].astype(F32)

        def main(base):
            rows = pl.ds(base, CONV_R)
            o_ref[rows, :] = (_f32(bg_ref, rows) * _taps_fwd(pad_ref, w_ref, K, base)).astype(BF16)
        _loop_rows(T, main)

    return _conv_call(body, name, T, nt, [_col(T), _col(T, nt), _col(T, 2 * nt), _col(T, rows=K)], _col(T),
                      _sds((T, D), BF16), 1)(uo, uo, uo, w)


def _conv_c_bwd(uo, w, dsc, name):
    T = uo.shape[0]
    K, nt = CONV_C, D // LANES

    def body(bg_ref, cg_ref, v_ref, w_ref, d_ref, dbg_ref, dcg_ref, dv_ref, dw_ref, pad_ref, padd_ref):
        _zero_head(pad_ref)
        _zero_tail(padd_ref, T)
        dw_ref[...] = jnp.zeros_like(dw_ref)
        pad_ref[CONV_PAD:CONV_PAD + T, :] = cg_ref[...].astype(F32) * v_ref[...].astype(F32)

        def pre(base):
            rows = pl.ds(base, CONV_R)
            d = d_ref[rows, :]
            dbg_ref[rows, :] = (d * _taps_fwd(pad_ref, w_ref, K, base)).astype(BF16)
            padd_ref[rows, :] = d * _f32(bg_ref, rows)
        _loop_rows(T, pre)

        def main(base):
            rows = pl.ds(base, CONV_R)
            _wgrad_acc(dw_ref, pad_ref, padd_ref[rows, :], K, base)
            dq = _taps_bwd(padd_ref, w_ref, K, base)
            dcg_ref[rows, :] = (dq * _f32(v_ref, rows)).astype(BF16)
            dv_ref[rows, :] = (dq * _f32(cg_ref, rows)).astype(BF16)
        _loop_rows(T, main)

    return _conv_call(body, name, T, nt, [_col(T), _col(T, nt), _col(T, 2 * nt), _col(T, rows=K), _col(T)],
                      (_col(T), _col(T), _col(T), _col(T, rows=K * SUBLANES)),
                      (_sds((T, D), BF16), _sds((T, D), BF16), _sds((T, D), BF16), _sds((K * SUBLANES, D))), 1, 1)(uo, uo, uo, w, dsc)


def _conv_f_fwd(up, w, b, name):
    T = up.shape[0]
    K, nt = CONV_F, D_FF // LANES

    def body(u1_ref, u2_ref, w1_ref, w2_ref, b1_ref, b2_ref, o_ref, pad1_ref, pad2_ref):
        _zero_head(pad1_ref)
        _zero_head(pad2_ref)
        pad1_ref[CONV_PAD:CONV_PAD + T, :] = u1_ref[...].astype(F32)
        pad2_ref[CONV_PAD:CONV_PAD + T, :] = u2_ref[...].astype(F32)

        def main(base):
            h1 = _taps_fwd(pad1_ref, w1_ref, K, base) + b1_ref[...]
            h2 = _taps_fwd(pad2_ref, w2_ref, K, base) + b2_ref[...]
            o_ref[pl.ds(base, CONV_R), :] = (h1 * _sig(h1) * h2).astype(BF16)
        _loop_rows(T, main)

    return _conv_call(body, name, T, nt,
                      [_col(T), _col(T, nt), _col(T, rows=K), _col(T, nt, rows=K), _col(T, rows=1), _col(T, nt, rows=1)],
                      _col(T), _sds((T, D_FF), BF16), 2)(up, up, w, w, b, b)


def _conv_f_bwd(up, w, b, dact, name):
    T = up.shape[0]
    K, nt = CONV_F, D_FF // LANES

    def body(u1_ref, u2_ref, w1_ref, w2_ref, b1_ref, b2_ref, d_ref, du1_ref, du2_ref, dw1_ref, dw2_ref, db1_ref, db2_ref,
             pad1_ref, pad2_ref, padd1_ref, padd2_ref):
        _zero_head(pad1_ref)
        _zero_head(pad2_ref)
        _zero_tail(padd1_ref, T)
        _zero_tail(padd2_ref, T)
        for r in (dw1_ref, dw2_ref, db1_ref, db2_ref):
            r[...] = jnp.zeros_like(r)
        pad1_ref[CONV_PAD:CONV_PAD + T, :] = u1_ref[...].astype(F32)
        pad2_ref[CONV_PAD:CONV_PAD + T, :] = u2_ref[...].astype(F32)

        def pre(base):
            rows = pl.ds(base, CONV_R)
            h1 = _taps_fwd(pad1_ref, w1_ref, K, base) + b1_ref[...]
            h2 = _taps_fwd(pad2_ref, w2_ref, K, base) + b2_ref[...]
            s = _sig(h1)
            d = _f32(d_ref, rows)
            padd1_ref[rows, :] = d * h2 * s * (1.0 + h1 * (1.0 - s))
            padd2_ref[rows, :] = d * h1 * s
        _loop_rows(T, pre)

        def main(base):
            rows = pl.ds(base, CONV_R)
            d1, d2 = padd1_ref[rows, :], padd2_ref[rows, :]
            _wgrad_acc(dw1_ref, pad1_ref, d1, K, base)
            _wgrad_acc(dw2_ref, pad2_ref, d2, K, base)
            db1_ref[...] += _fold8(d1)
            db2_ref[...] += _fold8(d2)
            du1_ref[rows, :] = _taps_bwd(padd1_ref, w1_ref, K, base).astype(BF16)
            du2_ref[rows, :] = _taps_bwd(padd2_ref, w2_ref, K, base).astype(BF16)
        _loop_rows(T, main)

    wrow, brow = _col(T, rows=K * SUBLANES), _col(T, rows=SUBLANES)
    return _conv_call(body, name, T, nt,
                      [_col(T), _col(T, nt), _col(T, rows=K), _col(T, nt, rows=K), _col(T, rows=1), _col(T, nt, rows=1), _col(T)],
                      (_col(T), _col(T), wrow, wrow, brow, brow),
                      (_sds((T, D_FF), BF16), _sds((T, D_FF), BF16), _sds((K * SUBLANES, D_FF)), _sds((K * SUBLANES, D_FF)),
                       _sds((SUBLANES, D_FF)), _sds((SUBLANES, D_FF))), 2, 2)(up, up, w, w, b, b, dact)


def _dot(a, b, dims="nn"):
    return lax.dot_general(a.astype(BF16), b.astype(BF16), _DIMS[dims], preferred_element_type=F32)


def _dot_mask(mask, v, mask_left):
    mb = mask.astype(BF16)
    hi = v.astype(BF16)
    r1 = v - hi.astype(F32)
    mid = r1.astype(BF16)
    lo = (r1 - mid.astype(F32)).astype(BF16)
    d = [jnp.dot(mb, t, preferred_element_type=F32) if mask_left else jnp.dot(t, mb, preferred_element_type=F32) for t in (hi, mid, lo)]
    return (d[0] + d[1]) + d[2]


def _ssd_small(xcr_ref, xrr_ref, bc_ref, br_ref, ac_ref, ar_ref):
    Q = SSD_Q
    li = lax.broadcasted_iota(jnp.int32, (Q, Q), 0)
    si = lax.broadcasted_iota(jnp.int32, (Q, Q), 1)
    tril = li >= si
    dtc = jax.nn.softplus(xcr_ref[...] + bc_ref[...])
    dtr = jax.nn.softplus(xrr_ref[...] + br_ref[...])
    cumc = _dot_mask(tril, dtc * ac_ref[...], True)
    cumr = _dot_mask(li <= si, dtr * ar_ref[...], False)
    return tril, dtc, dtr, cumc, cumr


def _ssd_specs(nc, rev):
    Q = SSD_Q
    cc = (lambda c: nc - 1 - c) if rev else (lambda c: c)
    x_spec = pl.BlockSpec((Q, 2 * LANES), lambda g, c: (cc(c), g))
    b_spec = pl.BlockSpec((Q, LANES), lambda g, c: (cc(c), 8 + g))
    c_spec = pl.BlockSpec((Q, LANES), lambda g, c: (cc(c), 12 + g))
    colm = pl.BlockSpec((None, Q, LANES), lambda g, c: (g, cc(c), 0))
    rowm = pl.BlockSpec((None, SUBLANES, Q), lambda g, c: (g, 0, cc(c)))
    colv = pl.BlockSpec((None, 1, LANES), lambda g, c: (g, 0, 0))
    rowv = pl.BlockSpec((None, SUBLANES, 1), lambda g, c: (g, 0, 0))
    st_spec = pl.BlockSpec((None, None, 2 * LANES, N_STATE), lambda g, c: (cc(c), g, 0, 0))
    return x_spec, b_spec, c_spec, colm, rowm, colv, rowv, st_spec


def _ssd_fwd(xc, raw_col, raw_row, bias_col, bias_row, a_col, a_row, dskip, name):
    T = xc.shape[0]
    Q = SSD_Q
    nc = T // Q
    x_spec, b_spec, c_spec, colm, rowm, colv, rowv, st_spec = _ssd_specs(nc, False)

    def body(dk_ref, x_ref, b_ref, c_ref, xcr_ref, xrr_ref, bc_ref, br_ref, ac_ref, ar_ref, y_ref, st_ref, h_ref):
        g = pl.program_id(0)

        @pl.when(pl.program_id(1) == 0)
        def _():
            h_ref[...] = jnp.zeros_like(h_ref)

        tril, dtc, dtr, cumc, cumr = _ssd_small(xcr_ref, xrr_ref, bc_ref, br_ref, ac_ref, ar_ref)
        Bm, Cm = b_ref[...], c_ref[...]
        S = _dot(Cm, Bm, "nt")
        lo = lax.broadcasted_iota(jnp.int32, (Q, LANES), 1) < HEAD_P
        rlo = lax.broadcasted_iota(jnp.int32, (LANES, N_STATE), 0) < HEAD_P
        st_ref[...] = h_ref[...]
        clast = cumc[Q - 1:Q, :]
        for pr in range(2):
            cols = slice(pr * LANES, (pr + 1) * LANES)
            xp = x_ref[:, cols]
            yd = jnp.zeros((Q, LANES), F32)
            for q in range(2):
                hh = 2 * pr + q
                seg = cumc[:, hh:hh + 1] - cumr[hh:hh + 1, :]
                lm = jnp.where(tril, jnp.exp(jnp.where(tril, seg, 0.0)), 0.0)
                w = S * lm * dtr[hh:hh + 1, :]
                xm = jnp.where(lo if q == 0 else jnp.logical_not(lo), xp, 0.0)
                yd = yd + _dot(w, xm)
            h0, h1 = 2 * pr, 2 * pr + 1
            c0, c1 = cumc[:, h0:h0 + 1], cumc[:, h1:h1 + 1]
            e_pair = jnp.where(lo, jnp.exp(c0), jnp.exp(c1))
            hp = h_ref[cols, :]
            ch = _dot(Cm, hp, "nt")
            dsk = jnp.where(lo, dk_ref[4 * g + h0], dk_ref[4 * g + h1])
            y_ref[:, cols] = yd + e_pair * ch + dsk * xp
            cl0, cl1 = clast[:, h0:h0 + 1], clast[:, h1:h1 + 1]
            sdec = jnp.where(lo, jnp.exp(cl0 - c0) * dtc[:, h0:h0 + 1], jnp.exp(cl1 - c1) * dtc[:, h1:h1 + 1])
            decrow = jnp.where(rlo, jnp.exp(cl0), jnp.exp(cl1))
            h_ref[cols, :] = hp * decrow + _dot(xp * sdec, Bm, "tn")

    smem = pl.BlockSpec(memory_space=pltpu.SMEM)
    return pl.pallas_call(
        body, name=name, grid=(N_GROUPS, nc),
        in_specs=[smem, x_spec, b_spec, c_spec, colm, rowm, colv, rowv, colv, rowv],
        out_specs=(x_spec, st_spec),
        out_shape=(_sds((T, D)), _sds((nc, N_GROUPS, 2 * LANES, N_STATE))),
        scratch_shapes=[pltpu.VMEM((2 * LANES, N_STATE), F32)],
        compiler_params=_cparams(("parallel", "arbitrary")))(dskip, xc, xc, xc, raw_col, raw_row, bias_col, bias_row, a_col, a_row)


def _ssd_bwd(xc, raw_col, raw_row, bias_col, bias_row, a_col, a_row, dskip, states, dy, name):
    T = xc.shape[0]
    Q = SSD_Q
    nc = T // Q
    x_spec, b_spec, c_spec, colm, rowm, colv, rowv, st_spec = _ssd_specs(nc, True)
    bo_spec = pl.BlockSpec((Q, LANES), lambda g, c: (nc - 1 - c, g))
    dd_spec = pl.BlockSpec((None, None, SUBLANES, 2 * LANES), lambda g, c: (nc - 1 - c, g, 0, 0))

    def body(dk_ref, x_ref, b_ref, c_ref, xcr_ref, xrr_ref, bc_ref, br_ref, ac_ref, ar_ref, st_ref, dy_ref,
             dx_ref, db_ref, dc_ref, sq_ref, cms_ref, ddac_ref, ddar_ref, dd_ref, dh_ref):
        g = pl.program_id(0)

        @pl.when(pl.program_id(1) == 0)
        def _():
            dh_ref[...] = jnp.zeros_like(dh_ref)

        tril, dtc, dtr, cumc, cumr = _ssd_small(xcr_ref, xrr_ref, bc_ref, br_ref, ac_ref, ar_ref)
        Bm, Cm = b_ref[...], c_ref[...]
        S = _dot(Cm, Bm, "nt")
        lane = lax.broadcasted_iota(jnp.int32, (Q, LANES), 1)
        sub = lax.broadcasted_iota(jnp.int32, (SUBLANES, Q), 0)
        rowi = lax.broadcasted_iota(jnp.int32, (Q, LANES), 0)
        lo = lane < HEAD_P
        rlo = lax.broadcasted_iota(jnp.int32, (LANES, N_STATE), 0) < HEAD_P
        clast = cumc[Q - 1:Q, :]
        ds_g = jnp.zeros((Q, Q), F32)
        dcm = jnp.zeros((Q, N_STATE), F32)
        dbm = jnp.zeros((Q, N_STATE), F32)
        dcum_col = jnp.zeros((Q, LANES), F32)
        dcum_row = jnp.zeros((SUBLANES, Q), F32)
        sq_col = jnp.zeros((Q, LANES), F32)
        cms_row = jnp.zeros((SUBLANES, Q), F32)
        for pr in range(2):
            cols = slice(pr * LANES, (pr + 1) * LANES)
            xp, dyp = x_ref[:, cols], dy_ref[:, cols]
            hin, dhp = st_ref[cols, :], dh_ref[cols, :]
            h0, h1 = 2 * pr, 2 * pr + 1
            c0, c1 = cumc[:, h0:h0 + 1], cumc[:, h1:h1 + 1]
            cl0, cl1 = clast[:, h0:h0 + 1], clast[:, h1:h1 + 1]
            e_pair = jnp.where(lo, jnp.exp(c0), jnp.exp(c1))
            edec = jnp.where(lo, jnp.exp(cl0 - c0), jnp.exp(cl1 - c1))
            dt_pair = jnp.where(lo, dtc[:, h0:h0 + 1], dtc[:, h1:h1 + 1])
            sdec = edec * dt_pair
            ch = _dot(Cm, hin, "nt")
            xb = _dot(Bm, dhp, "nt")
            dye = dyp * e_pair
            t1 = dye * ch
            t2 = xp * xb * edec
            hh_prod = dhp * hin
            dsk = jnp.where(lo, dk_ref[4 * g + h0], dk_ref[4 * g + h1])
            dxp = sdec * xb + dsk * dyp
            for q in range(2):
                hh = 2 * pr + q
                mine = lo if q == 0 else jnp.logical_not(lo)
                seg = cumc[:, hh:hh + 1] - cumr[hh:hh + 1, :]
                lm = jnp.where(tril, jnp.exp(jnp.where(tril, seg, 0.0)), 0.0)
                dtrow = dtr[hh:hh + 1, :]
                w = S * lm * dtrow
                dym = jnp.where(mine, dyp, 0.0)
                gl = _dot(dym, xp, "nt") * lm
                ds_g = ds_g + gl * dtrow
                ms = gl * S
                m = ms * dtrow
                dxp = dxp + _dot(w, dym, "tn")
                cms_row = jnp.where(sub == hh, jnp.sum(ms, axis=0, keepdims=True), cms_row)
                dcum_row = jnp.where(sub == hh, -jnp.sum(m, axis=0, keepdims=True), dcum_row)
                t1h = jnp.sum(jnp.where(mine, t1, 0.0), axis=1, keepdims=True)
                sqh = jnp.sum(jnp.where(mine, t2, 0.0), axis=1, keepdims=True)
                sth = sqh * dtc[:, hh:hh + 1]
                rmine = rlo if q == 0 else jnp.logical_not(rlo)
                hsum = jnp.sum(jnp.sum(jnp.where(rmine, hh_prod, 0.0), axis=1, keepdims=True), axis=0, keepdims=True)
                last = jnp.sum(sth, axis=0, keepdims=True) + jnp.exp(clast[:, hh:hh + 1]) * hsum
                dcol = jnp.sum(m, axis=1, keepdims=True) + t1h - sth
                dcum_col = jnp.where(lane == hh, dcol + jnp.where(rowi == Q - 1, last, 0.0), dcum_col)
                sq_col = jnp.where(lane == hh, sqh, sq_col)
            dcm = dcm + _dot(dye, hin)
            dbm = dbm + _dot(xp * sdec, dhp)
            decrow = jnp.where(rlo, jnp.exp(cl0), jnp.exp(cl1))
            dh_ref[cols, :] = dhp * decrow + _dot(dye, Cm, "tn")
            dx_ref[:, cols] = dxp
            dd_ref[:, cols] = jnp.broadcast_to(jnp.sum(dyp * xp, axis=0, keepdims=True), (SUBLANES, LANES))
        dc_ref[...] = dcm + _dot(ds_g, Bm)
        db_ref[...] = dbm + _dot(ds_g, Cm, "tn")
        li = lax.broadcasted_iota(jnp.int32, (Q, Q), 0)
        si = lax.broadcasted_iota(jnp.int32, (Q, Q), 1)
        ddac_ref[...] = _dot_mask(li <= si, dcum_col, True)
        ddar_ref[...] = _dot_mask(tril, dcum_row, False)
        sq_ref[...] = sq_col
        cms_ref[...] = cms_row

    smem = pl.BlockSpec(memory_space=pltpu.SMEM)
    return pl.pallas_call(
        body, name=name, grid=(N_GROUPS, nc),
        in_specs=[smem, x_spec, b_spec, c_spec, colm, rowm, colv, rowv, colv, rowv, st_spec, x_spec],
        out_specs=(x_spec, bo_spec, bo_spec, colm, rowm, colm, rowm, dd_spec),
        out_shape=(_sds((T, D)), _sds((T, D // 2)), _sds((T, D // 2)), _sds((N_GROUPS, T, LANES)), _sds((N_GROUPS, SUBLANES, T)),
                   _sds((N_GROUPS, T, LANES)), _sds((N_GROUPS, SUBLANES, T)), _sds((nc, N_GROUPS, SUBLANES, 2 * LANES))),
        scratch_shapes=[pltpu.VMEM((2 * LANES, N_STATE), F32)],
        compiler_params=_cparams(("parallel", "arbitrary")))(dskip, xc, xc, xc, raw_col, raw_row, bias_col, bias_row, a_col, a_row,
                                                            states, dy)


def _adam_math(wv, gv, mv, vv):
    c1 = 1.0 - ADAM_B1 ** ADAM_STEP
    c2 = 1.0 - ADAM_B2 ** ADAM_STEP
    mn = ADAM_B1 * mv + (1.0 - ADAM_B1) * gv
    vn = ADAM_B2 * vv + (1.0 - ADAM_B2) * (gv * gv)
    return -ADAM_LR * ((mn / c1) / (jnp.sqrt(vn / c2) + ADAM_EPS) + ADAM_WD * wv), mn, vn


def _adamw_layers(w, g, m, v, l0, Lg, bufs, name):
    L, As, Bs = w.shape
    tr = _tile(As, [], (256, 352, 128))
    has_bufs = bufs is not None

    def body(*refs):
        w_ref, g_ref, m_ref, v_ref = refs[:4]
        d_ref, mo_ref, vo_ref = refs[4 + 3 * has_bufs:]
        d_ref[...], mo_ref[...], vo_ref[...] = _adam_math(w_ref[...], g_ref[...], m_ref[...], v_ref[...])

    spec = pl.BlockSpec((None, tr, Bs), lambda l, i: (l + l0, i, 0))
    args = (w, g, m, v) + (tuple(bufs) if has_bufs else ())
    return pl.pallas_call(
        body, name=name, grid=(Lg, As // tr), in_specs=[spec] * 4 + [_ANY] * (3 * has_bufs), out_specs=(spec,) * 3,
        out_shape=(_sds((L, As, Bs)),) * 3, input_output_aliases={4: 0, 5: 1, 6: 2} if has_bufs else {},
        compiler_params=_cparams(("parallel", "parallel")))(*args)


def _adamw_minor_rows(w, g, m, v, name):
    L, R, C = w.shape
    tr = max(t for t in range(1, C + 1) if C % t == 0 and t * L * R * 4 <= (1 << 20))
    wt, gt, mt, vt = (jnp.transpose(t, (2, 0, 1)) for t in (w, g, m, v))

    def body(w_ref, g_ref, m_ref, v_ref, d_ref, mo_ref, vo_ref):
        d_ref[...], mo_ref[...], vo_ref[...] = _adam_math(w_ref[...], g_ref[...], m_ref[...], v_ref[...])

    spec = pl.BlockSpec((tr, L, R), lambda i: (i, 0, 0))
    out = pl.pallas_call(body, name=name, grid=(C // tr,), in_specs=[spec] * 4, out_specs=(spec,) * 3,
                         out_shape=(_sds((C, L, R)),) * 3, compiler_params=_cparams(("parallel",)))(wt, gt, mt, vt)
    return tuple(jnp.transpose(o, (1, 2, 0)) for o in out) + (jnp.transpose(gt, (1, 2, 0)),)


def _adamw(w, g, m, v, name):
    shape = w.shape
    cols = shape[-1]
    w2, g2, m2, v2 = (t.reshape(-1, cols) for t in (w, g, m, v))
    rows = w2.shape[0]
    tr = 256 if (rows % 256 == 0 and rows > 256) else rows
    c1 = 1.0 - ADAM_B1 ** ADAM_STEP
    c2 = 1.0 - ADAM_B2 ** ADAM_STEP

    def body(w_ref, g_ref, m_ref, v_ref, d_ref, mo_ref, vo_ref):
        gv = g_ref[...]
        mn = ADAM_B1 * m_ref[...] + (1.0 - ADAM_B1) * gv
        vn = ADAM_B2 * v_ref[...] + (1.0 - ADAM_B2) * (gv * gv)
        d_ref[...] = -ADAM_LR * ((mn / c1) / (jnp.sqrt(vn / c2) + ADAM_EPS) + ADAM_WD * w_ref[...])
        mo_ref[...] = mn
        vo_ref[...] = vn

    spec = pl.BlockSpec((tr, cols), lambda i: (i, 0))
    out = pl.pallas_call(body, name=name, grid=(rows // tr,), in_specs=[spec] * 4, out_specs=(spec,) * 3,
                         out_shape=(_sds((rows, cols)),) * 3, compiler_params=_cparams(("parallel",)))(w2, g2, m2, v2)
    return tuple(o.reshape(shape) for o in out)


def _place():
    x, y, c = lax.axis_index("x"), lax.axis_index("y"), lax.axis_index("c")
    chips = [(1 - x, y), (x, 1 - y), (1 - x, 1 - y)]
    return x, y, c, chips


_ANY = pl.BlockSpec(memory_space=pl.ANY)


TENSORS = (("e_w_in", "row", 2, 4096, 1284, 1024), ("e_w_out", "row", 2, 2048, 1024, 512), ("o_w_in", "col", 2, 1024, 3072, 768),
           ("o_w_out", "row", 2, 1024, 1024, 256), ("f_w_up", "col", 4, 1024, 5632, 1408), ("f_w_down", "row", 4, 2816, 1024, 704),
           ("ple_w_proj", "col", 4, 256, 1024, 256), ("ple_w_gate", "row", 4, 1024, 1024, 256))
MIX, FFN = "mix", "ffn"
W_GROUPS = (((0, MIX),), ((0, FFN), (1, MIX)), ((1, FFN), (2, MIX)), ((2, FFN), (3, MIX), (3, FFN)))
G_GROUPS = (((3, FFN), (3, MIX), (2, FFN), (2, MIX), (1, FFN), (1, MIX)), ((0, FFN),), ((0, MIX),))


def _tensor_layer(name, layer):
    if name.startswith("e_"):
        return layer // 2 if layer % 2 == 0 else None
    if name.startswith("o_"):
        return layer // 2 if layer % 2 == 1 else None
    return layer


def _part(name):
    return MIX if name.startswith(("e_", "o_")) else FFN


def _group_items(members):
    items = []
    for name, kind, L, A, B, n in TENSORS:
        tls = sorted(t for t in (_tensor_layer(name, l) for l, part in members if part == _part(name)) if t is not None)
        if tls:
            assert tls == list(range(tls[0], tls[0] + len(tls)))
            items.append((name, kind, len(tls), A, B, n, tls[0]))
    return items


def _hwin(ref, it, k, h):
    name, kind, Lg, A, B, n, l0 = it
    if kind == "row":
        return ref.at[:, pl.ds(pl.multiple_of(k * n + h * (n // 2), 16), n // 2), :]
    return ref.at[:, pl.ds(pl.multiple_of(h * (A // 2), 16), A // 2), pl.ds(pl.multiple_of(k * n, LANES), n)]


def _shard_dims(kind, A, B, n):
    return (n, B) if kind == "row" else (A, n)


def _cast_into(w, it, me):
    name, kind, Lg, A, B, n, l0 = it
    As, Bs = _shard_dims(kind, A, B, n)

    def body(me_ref, w_ref, o_ref):
        o_ref[...] = w_ref[...].astype(BF16)

    omap = (lambda l, m: (l, m[0], 0)) if kind == "row" else (lambda l, m: (l, 0, m[0]))
    grid_spec = pltpu.PrefetchScalarGridSpec(
        num_scalar_prefetch=1, grid=(Lg,), in_specs=[pl.BlockSpec((None, As, Bs), lambda l, m: (l + l0, 0, 0))],
        out_specs=pl.BlockSpec((None, As, Bs), omap))
    return pl.pallas_call(body, name=f"cast_{name}_{l0}", grid_spec=grid_spec, out_shape=_sds((Lg, A, B), BF16),
                          compiler_params=_cparams(("parallel",)))(me, w.reshape(-1, As, Bs))


_HBM = pl.BlockSpec(memory_space=pltpu.HBM)
_SEM = pl.BlockSpec(memory_space=pltpu.SEMAPHORE)
_EFFECT = pltpu.SideEffectType.DATAFLOW_SIDE_EFFECTING


def _hbm(a):
    return pltpu.with_memory_space_constraint(a, pltpu.HBM)


def _split_start(thru, n_copies, issue, name, after=None):
    N = len(thru)
    has_after = after is not None

    def body(*refs):
        outs = refs[N + has_after:2 * N + has_after]
        send_sems, recv_sems, token = refs[2 * N + has_after:]
        for cp in issue(outs, send_sems, recv_sems):
            cp.start()
        token[...] = jnp.zeros_like(token)

    out = pl.pallas_call(
        body, name=name, in_specs=[_HBM] * N + ([_ANY] if has_after else []),
        out_specs=(_HBM,) * N + (_SEM, _SEM, pl.BlockSpec(memory_space=pltpu.VMEM)),
        out_shape=tuple(pltpu.HBM(a.shape, a.dtype) for a in thru)
        + (pltpu.SemaphoreType.DMA((n_copies,)), pltpu.SemaphoreType.DMA((n_copies,)), _sds((SUBLANES, LANES))),
        input_output_aliases={t: t for t in range(N)},
        compiler_params=pltpu.CompilerParams(has_side_effects=_EFFECT))(*[_hbm(a) for a in thru], *([after] if has_after else []))
    return list(out[:N]), out[N], out[N + 1], out[N + 2]


def _split_wait(thru, send_sems, recv_sems, after, waits, name):
    N = len(thru)
    after = list(after) if isinstance(after, (list, tuple)) else [after]

    def body(*refs):
        ins = refs[:N]
        for cp, side in waits(ins, refs[N], refs[N + 1]):
            if side == "send":
                cp.wait_send()
            else:
                cp.wait_recv()

    out = pl.pallas_call(
        body, name=name, in_specs=[_HBM] * N + [_SEM, _SEM] + [_ANY] * len(after), out_specs=(_HBM,) * N,
        out_shape=tuple(pltpu.HBM(a.shape, a.dtype) for a in thru), input_output_aliases={t: t for t in range(N)},
        compiler_params=pltpu.CompilerParams(has_side_effects=_EFFECT))(*thru, send_sems, recv_sems, *after)
    return list(out)


def _rcopy(send_sems, recv_sems, k, src, dst, to):
    return pltpu.make_async_remote_copy(src_ref=src, dst_ref=dst, send_sem=send_sems.at[k], recv_sem=recv_sems.at[k],
                                        device_id=to, device_id_type=MESH)


def _gather_copies(items, refs, send_sems, recv_sems, what):
    x, y, c, chips = _place()
    me = 2 * x + y
    out = []
    for t, it in enumerate(items):
        mine = _hwin(refs[t], it, me, c)
        for j, (px, py) in enumerate(chips):
            if what == "start":
                out.append(_rcopy(send_sems, recv_sems, 3 * t + j, mine, mine, (px, py, c)))
            else:
                slot = _hwin(refs[t], it, 2 * px + py, c)
                out.append((_rcopy(send_sems, recv_sems, 3 * t + j, mine, mine, (px, py, c)), "send"))
                out.append((_rcopy(send_sems, recv_sems, 3 * t + j, slot, slot, (px, py, c)), "recv"))
    return out


def _gather_start(fulls, items, name, after=None):
    return _split_start(fulls, 3 * len(items), functools.partial(_gather_copies, items, what="start"), name, after)


def _gather_wait(fulls, send_sems, recv_sems, after, items, name):
    return _split_wait(fulls, send_sems, recv_sems, after, functools.partial(_gather_copies, items, what="wait"), name)


def _gather_fwd(fulls, items, name, ws=None):
    N = len(fulls)
    has_ws = ws is not None

    def body(*refs):
        outs = refs[N + has_ws:2 * N + has_ws]
        rest = refs[2 * N + has_ws:]
        x, y, c, chips = _place()
        me = 2 * x + y
        sib = (x, y, 1 - c)
        if has_ws:
            ws_ref = refs[N]
            WS_ref, send_sems, recv_sems, lsem = rest
            loc = pltpu.make_async_copy(ws_ref, WS_ref.at[me], lsem)
            loc.start()
        else:
            send_sems, recv_sems = rest
        rc = functools.partial(_rcopy, send_sems, recv_sems)
        cps = []
        for t, it in enumerate(items):
            for j, (px, py) in enumerate(chips):
                slot = _hwin(outs[t], it, 2 * px + py, c)
                cps.append(rc(3 * t + j, slot, slot, sib))
        if has_ws:
            cps += [rc(3 * N + j, ws_ref, WS_ref.at[me], (*chip, c)) for j, chip in enumerate(chips)]
        for cp in cps:
            cp.start()
        for t, it in enumerate(items):
            for j, (px, py) in enumerate(chips):
                oslot = _hwin(outs[t], it, 2 * px + py, 1 - c)
                rc(3 * t + j, oslot, oslot, sib).wait_recv()
        if has_ws:
            for j, (px, py) in enumerate(chips):
                sslot = WS_ref.at[2 * px + py]
                rc(3 * N + j, sslot, sslot, sib).wait_recv()
        for cp in cps:
            cp.wait_send()
        if has_ws:
            loc.wait()

    ns = 3 * N + (3 if has_ws else 0)
    out_shape = tuple(_sds(f.shape, f.dtype) for f in fulls)
    scratch = [pltpu.SemaphoreType.DMA((ns,)), pltpu.SemaphoreType.DMA((ns,))]
    args = list(fulls)
    if has_ws:
        out_shape += (_sds((4,) + ws.shape, ws.dtype),)
        scratch.append(pltpu.SemaphoreType.DMA(()))
        args.append(ws)
    out = pl.pallas_call(
        body, name=name, in_specs=[_ANY] * len(args), out_specs=(_ANY,) * len(out_shape), out_shape=out_shape,
        input_output_aliases={t: t for t in range(N)}, scratch_shapes=scratch,
        compiler_params=pltpu.CompilerParams(has_side_effects=True))(*args)
    return (list(out[:N]), out[N]) if has_ws else (list(out), None)


def _half_shape(it):
    name, kind, Lg, A, B, n, l0 = it
    return (Lg, 4, n // 2, B) if kind == "row" else (Lg, A // 2, B)


def _piece_shape(it):
    name, kind, Lg, A, B, n, l0 = it
    return (Lg, n // 2, B) if kind == "row" else (Lg, A // 2, n)


def _swap_copies(items, refs, send_sems, recv_sems, what):
    N = len(items)
    x, y, c, _ = _place()
    sib = (x, y, 1 - c)
    out = []
    for t, it in enumerate(items):
        name_, kind, Lg, A, B, n, l0 = it
        if kind == "row":
            cps = [_rcopy(send_sems, recv_sems, 4 * t + k, _hwin(refs[t], it, k, 1 - c), refs[N + t].at[:, k], sib) for k in range(4)]
        else:
            src = refs[t].at[:, pl.ds(pl.multiple_of((1 - c) * (A // 2), 16), A // 2), :]
            cps = [_rcopy(send_sems, recv_sems, 4 * t, src, refs[N + t], sib)]
        for cp in cps:
            if what == "start":
                out.append(cp)
            else:
                out += [(cp, "send"), (cp, "recv")]
    return out


def _swap_start(gs, items, name, after=None):
    lands = [lax.empty(_half_shape(it), F32) for it in items]
    return _split_start(list(gs) + lands, 4 * len(items), functools.partial(_swap_copies, items, what="start"), name, after)


def _swap_wait(thru, send_sems, recv_sems, after, items, name):
    return _split_wait(thru, send_sems, recv_sems, after, functools.partial(_swap_copies, items, what="wait"), name)


def _add_half(g, ra, it, cvec):
    name, kind, Lg, A, B, n, l0 = it
    if kind == "row":
        blk = (None, n // 2, B)
        grid = (Lg, 4)
        g_spec = pl.BlockSpec(blk, lambda l, k, cr: (l, 2 * k + cr[0], 0))
        h_spec = pl.BlockSpec((None, None, n // 2, B), lambda l, k, cr: (l, k, 0, 0))
    else:
        tr = _tile(A // 2, [], (256, 128))
        nb = (A // 2) // tr
        grid = (Lg, nb)
        g_spec = pl.BlockSpec((None, tr, B), lambda l, i, cr: (l, cr[0] * nb + i, 0))
        h_spec = pl.BlockSpec((None, tr, B), lambda l, i, cr: (l, i, 0))

    def body(c_ref, g_ref, r_ref, o_ref):
        o_ref[...] = (g_ref[...] + r_ref[...]).astype(BF16)

    grid_spec = pltpu.PrefetchScalarGridSpec(num_scalar_prefetch=1, grid=grid, in_specs=[g_spec, h_spec], out_specs=h_spec)
    return pl.pallas_call(body, name=f"addhalf_{name}_{l0}", grid_spec=grid_spec, out_shape=_sds(_half_shape(it), BF16),
                          compiler_params=_cparams(("parallel", "parallel")))(cvec, g, ra)


def _scatter_copies(items, refs, send_sems, recv_sems, what):
    N = len(items)
    x, y, c, chips = _place()
    out = []
    for t, it in enumerate(items):
        name, kind, Lg, A, B, n, l0 = it
        for j, (px, py) in enumerate(chips):
            k = 2 * px + py
            src = refs[t].at[:, k] if kind == "row" else refs[t].at[:, :, pl.ds(pl.multiple_of(k * n, LANES), n)]
            cp = _rcopy(send_sems, recv_sems, 3 * t + j, src, refs[N + t].at[j], (px, py, c))
            if what == "start":
                out.append(cp)
            else:
                out += [(cp, "send"), (cp, "recv")]
    return out


def _scatter_start(ps, items, name):
    lands = [lax.empty((3,) + _piece_shape(it), BF16) for it in items]
    return _split_start(list(ps) + lands, 3 * len(items), functools.partial(_scatter_copies, items, what="start"), name)


def _scatter_wait(thru, send_sems, recv_sems, after, items, name):
    return _split_wait(thru, send_sems, recv_sems, after, functools.partial(_scatter_copies, items, what="wait"), name)


def _sum_own(p, rc, it, mevec, buf):
    name, kind, Lg, A, B, n, l0 = it
    As, Bs = _shard_dims(kind, A, B, n)
    L = [s[2] for s in TENSORS if s[0] == name][0]
    hb = (As // 2, Bs)
    has_buf = buf is not None

    def body(*refs):
        p_ref, r0, r1, r2 = refs[1:5]
        o_ref = refs[5 + has_buf]
        o_ref[...] = ((p_ref[...].astype(F32) + r0[...].astype(F32)) + r1[...].astype(F32)) + r2[...].astype(F32)

    if kind == "row":
        p_spec = pl.BlockSpec((None, None) + hb, lambda l, m: (l, m[0], 0, 0))
    else:
        p_spec = pl.BlockSpec((None,) + hb, lambda l, m: (l, 0, m[0]))
    r_specs = [pl.BlockSpec((None, None) + hb, functools.partial(lambda l, m, j: (j, l, 0, 0), j=j)) for j in range(3)]
    in_specs = [p_spec] + r_specs + ([_ANY] if has_buf else [])
    grid_spec = pltpu.PrefetchScalarGridSpec(num_scalar_prefetch=1, grid=(Lg,), in_specs=in_specs,
                                             out_specs=pl.BlockSpec((None,) + hb, lambda l, m: (l + l0, m[1], 0)))
    args = (mevec, p, rc, rc, rc) + ((buf,) if has_buf else ())
    return pl.pallas_call(body, name=f"sumown_{name}_{l0}", grid_spec=grid_spec, out_shape=_sds((L, As, Bs)),
                          input_output_aliases={5: 0} if has_buf else {}, compiler_params=_cparams(("parallel",)))(*args)


def _join_halves(rs, items, name):
    N = len(rs)

    def body(*refs):
        outs = refs[N:2 * N]
        send_sems, recv_sems = refs[2 * N:]
        x, y, c, _ = _place()
        sib = (x, y, 1 - c)

        def half(t, h):
            name_, kind, Lg, A, B, n, l0 = items[t]
            hr = _shard_dims(kind, A, B, n)[0] // 2
            return outs[t].at[pl.ds(l0, Lg), pl.ds(pl.multiple_of(h * hr, SUBLANES), hr), :]

        cps = [_rcopy(send_sems, recv_sems, t, half(t, c), half(t, c), sib) for t in range(N)]
        for cp in cps:
            cp.start()
        for t in range(N):
            _rcopy(send_sems, recv_sems, t, half(t, 1 - c), half(t, 1 - c), sib).wait_recv()
        for cp in cps:
            cp.wait_send()

    return list(pl.pallas_call(
        body, name=name, in_specs=[_ANY] * N, out_specs=(_ANY,) * N, out_shape=tuple(_sds(r.shape, r.dtype) for r in rs),
        input_output_aliases={t: t for t in range(N)},
        scratch_shapes=[pltpu.SemaphoreType.DMA((N,)), pltpu.SemaphoreType.DMA((N,))],
        compiler_params=pltpu.CompilerParams(has_side_effects=True))(*rs))


def _allgather_small(v):
    m_per, n = v.shape

    def body(x_ref, out_ref, send_sems, recv_sems, local_sem):
        x, y, c, chips = _place()
        me, sibling = (x, y, c), (x, y, 1 - c)

        def rows(px, py, pc):
            return out_ref.at[pl.ds(pl.multiple_of((4 * px + 2 * py + pc) * m_per, SUBLANES), m_per), :]

        def copy(k, block, to, src=None):
            return pltpu.make_async_remote_copy(src_ref=rows(*block) if src is None else src, dst_ref=rows(*block),
                                                send_sem=send_sems.at[k], recv_sem=recv_sems.at[k], device_id=to, device_id_type=MESH)

        mine = pltpu.make_async_copy(x_ref, rows(*me), local_sem)
        mine.start()
        first = [copy(0, me, sibling, src=x_ref)]
        first += [copy(1 + j, me, (*chip, c), src=x_ref) for j, chip in enumerate(chips)]
        for cp in first:
            cp.start()
        passed = [copy(4 + j, (*chip, c), sibling) for j, chip in enumerate(chips)]
        for j, chip in enumerate(chips):
            copy(1 + j, (*chip, c), me).wait_recv()
            passed[j].start()
        copy(0, sibling, me).wait_recv()
        for j, chip in enumerate(chips):
            copy(4 + j, (*chip, 1 - c), me).wait_recv()
        for cp in first + passed:
            cp.wait_send()
        mine.wait()

    vm = pl.BlockSpec(memory_space=pltpu.VMEM)
    return pl.pallas_call(body, name="allgather_small", in_specs=[vm], out_specs=vm, out_shape=_sds((8 * m_per, n)),
                          scratch_shapes=[pltpu.SemaphoreType.DMA((7,)), pltpu.SemaphoreType.DMA((7,)), pltpu.SemaphoreType.DMA(())],
                          compiler_params=pltpu.CompilerParams(has_side_effects=True, vmem_limit_bytes=VMEM_LIMIT))(v)


def _sum8(v, m_per):
    def body(v_ref, o_ref):
        acc = v_ref[0:m_per, :]
        for k in range(1, 8):
            acc = acc + v_ref[k * m_per:(k + 1) * m_per, :]
        o_ref[...] = acc

    return pl.pallas_call(body, name="small_sum_devices", out_shape=_sds((m_per, v.shape[1])),
                          compiler_params=pltpu.CompilerParams(vmem_limit_bytes=VMEM_LIMIT))(v)


SMALL_SHARDED = (("e_conv_a_w", 2), ("e_conv_b_w", 2), ("o_conv_w", 2), ("f_conv_w", 2), ("ln_g", 2), ("ln_b", 2))
SMALL_REPL = ("e_conv_a_b", "e_ln_a_g", "e_ln_a_b", "e_conv_b_b", "e_dt_bias", "e_a_log", "e_d_skip", "e_norm_b_g", "f_conv_b")

WEIGHT_ORDER = ('e_w_in', 'e_conv_a_w', 'e_conv_a_b', 'e_ln_a_g', 'e_ln_a_b', 'e_conv_b_w', 'e_conv_b_b', 'e_dt_bias', 'e_a_log',
                'e_d_skip', 'e_norm_b_g', 'e_w_out', 'o_w_in', 'o_conv_w', 'o_w_out', 'f_w_up', 'f_conv_w', 'f_conv_b', 'f_w_down',
                'ple_w_proj', 'ple_w_gate', 'ln_g', 'ln_b')


def _pack_rows(parts, width, total_rows, dtype):
    flat = jnp.concatenate([p.reshape(-1).astype(dtype) for p in parts])
    flat = jnp.pad(flat, (0, total_rows * width - flat.shape[0]))
    return flat.reshape(total_rows, width)


def _unpack_rows(buf, shapes):
    flat = buf.reshape(-1)
    out, pos = [], 0
    for s in shapes:
        n = math.prod(s)
        out.append(flat[pos:pos + n].reshape(s))
        pos += n
    return out


def _small_rows(shapes):
    n = sum(math.prod(s) for s in shapes)
    return -(-n // (LANES * SUBLANES)) * SUBLANES


E_PAD = 5248
SEG_A, SEG_Z, SEG_X, SEG_DT = (0, 2 * D), (2 * D, D), (3 * D, 2 * D), (5 * D, LANES)
G_SHAPES = {"e_w_in": (2, D, E_PAD), "e_w_out": (2, 2 * D, D), "o_w_in": (2, D, 3 * D), "o_w_out": (2, D, D),
            "f_w_up": (4, D, 2 * D_FF), "f_w_down": (4, D_FF, D), "ple_w_proj": (4, PLE, D), "ple_w_gate": (4, D, D)}


def _padcols(w, width):
    return jnp.pad(w, ((0, 0), (0, width - w.shape[1])))


def _fold_rows(dw, K):
    return dw.reshape(K, SUBLANES, dw.shape[-1]).sum(1)


class GradBuffers(dict):
    def __init__(self):
        super().__init__()
        self.where = {}
        for gi, layers in enumerate(G_GROUPS):
            for name, kind, Lg, A, B, n, l0 in _group_items(layers):
                for k in range(Lg):
                    self.where[(name, l0 + k)] = (gi, k, Lg)
        self.current = {}

    def into(self, name, layer, r0=0, c0=0):
        gi, k, Lg = self.where[(name, layer)]
        self.current[name] = (name, gi)
        return (self.get((name, gi)), (Lg,) + G_SHAPES[name][1:], (k,), r0, c0)

    def __setitem__(self, name, value):
        super().__setitem__(self.current[name], value)
        self.last = value


def _local_step(x, p, target, W, comm=None, xb=None):
    T = x.shape[0]
    if xb is None:
        xb = _to_bf16(x, "x_bf16")
    saved = []
    xc_f = x
    for i in range(DEPTH):
        j = i // 2
        L = {}
        L["x"], L["xb"] = xc_f, xb
        tok = comm.part_starts(i, MIX, xb) if comm is not None else None
        if i % 2 == 0:
            def w_in(seg, c0=0, cols=None, j=j):
                return V(W["e_w_in"], (j,), c0=seg[0] + c0, cols=seg[1] if cols is None else cols)

            ua = _mm(xb, w_in(SEG_A), "nn", f"l{i}_in_a", BF16, after=tok)
            z = _mm(xb, w_in(SEG_Z), "nn", f"l{i}_in_z")
            xu = _mm(xb, w_in(SEG_X), "nn", f"l{i}_in_xbc", BF16)
            udt = _mm(xb, w_in(SEG_DT), "nn", f"l{i}_in_dt")
            ac = _conv_a_fwd(ua, W["e_conv_a_w"][j], W["e_conv_a_b"][j][None], f"l{i}_conv_a")
            ya = _ln_silu_fwd(ac, W["e_ln_a_g"][j][None], W["e_ln_a_b"][j][None], f"l{i}_ln_a")
            xc = _conv_b_fwd(xu, W["e_conv_b_w"][j], W["e_conv_b_b"][j][None], f"l{i}_conv_b")
            sm = _ssd_small_inputs(udt[:, :N_HEADS], W["e_dt_bias"][j], W["e_a_log"][j])
            y, states = _ssd_fwd(xc, *sm, W["e_d_skip"][j], f"l{i}_ssd")
            yb = _gate_rms_fwd(y, z, W["e_norm_b_g"][j][None], f"l{i}_gate_rms")
            out_pairs = [(ya, V(W["e_w_out"], (j,), rows=D)), (yb, V(W["e_w_out"], (j,), r0=D))]
            L.update(ua=ua, z=z, xu=xu, udt=udt, ac=ac, ya=ya, xc=xc, sm=sm, y=y, states=states, yb=yb, w_in=w_in)
        else:
            uo = _mm(xb, V(W["o_w_in"], (j,)), "nn", f"l{i}_in", BF16, after=tok)
            sc = _conv_c_fwd(uo, W["o_conv_w"][j], f"l{i}_conv_c")
            out_pairs = [(sc, V(W["o_w_out"], (j,)))]
            L.update(uo=uo, sc=sc)
        h1, x1, x1b = _mm_sum(out_pairs, "nn", f"l{i}_out", ln_fwd=(xc_f, None, W["ln_g"][i, 0][None], W["ln_b"][i, 0][None]))
        tok = comm.part_starts(i, FFN, x1b) if comm is not None else None
        up = _mm(x1b, V(W["f_w_up"], (i,)), "nn", f"l{i}_ffn_up", BF16, after=tok)
        act = _conv_f_fwd(up, W["f_conv_w"][i], W["f_conv_b"][i][None], f"l{i}_conv_f")
        pv = V(p, (i, 0))
        pp = _mm(pv, V(W["ple_w_proj"], (i,)), "nn", f"l{i}_ple_proj")
        gl = _mm(x1b, V(W["ple_w_gate"], (i,)), "nn", f"l{i}_ple_gate")
        h2, x2, x2b = _mm_sum([(act, V(W["f_w_down"], (i,)))], "nn", f"l{i}_ffn_down",
                              ln_fwd=(x1, (pp, gl), W["ln_g"][i, 1][None], W["ln_b"][i, 1][None]))
        L.update(h1=h1, x1=x1, x1b=x1b, up=up, act=act, pv=pv, pp=pp, gl=gl, h2=h2)
        saved.append(L)
        xc_f, xb = x2, x2b

    sq, dx = _loss_head(xc_f, target, "loss_head")

    GB = GradBuffers()
    into = GB.into
    tok = None
    ln2_done = None

    G = {n: [None] * (DEPTH if n.startswith(("f_", "ln_")) else DEPTH // 2) for n in WEIGHT_ORDER if n not in G_SHAPES}
    for i in reversed(range(DEPTH)):
        j = i // 2
        L = saved[i]
        if ln2_done is None:
            ln2_done = _res_ln_bwd(dx, L["h2"], W["ln_g"][i, 1][None], (L["pp"], L["gl"]), f"l{i}_ln2_bwd")
        dh2, dh2b, dg2, db2, dpp, dgl = ln2_done
        ln2_done = None
        GB["f_w_down"] = _mm(L["act"], dh2b, "tn", f"l{i}_dw_down", dst=into("f_w_down", i))
        dact = _mm(dh2b, V(W["f_w_down"], (i,)), "nt", f"l{i}_dact", BF16, after=tok)
        du1, du2, dw1, dw2, dbf1, dbf2 = _conv_f_bwd(L["up"], W["f_conv_w"][i], W["f_conv_b"][i][None], dact, f"l{i}_conv_f_bwd")
        G["f_conv_w"][i] = jnp.concatenate([_fold_rows(dw1, CONV_F), _fold_rows(dw2, CONV_F)], axis=1)
        G["f_conv_b"][i] = jnp.concatenate([dbf1.sum(0), dbf2.sum(0)])
        GB["f_w_up"] = _mm(L["x1b"], du1, "tn", f"l{i}_dw_up1", dst=into("f_w_up", i))
        GB["f_w_up"] = _mm(L["x1b"], du2, "tn", f"l{i}_dw_up2", dst=into("f_w_up", i, c0=D_FF))
        GB["ple_w_proj"] = _mm(L["pv"], dpp, "tn", f"l{i}_dw_proj", dst=into("ple_w_proj", i))
        GB["ple_w_gate"] = _mm(L["x1b"], dgl, "tn", f"l{i}_dw_gate", dst=into("ple_w_gate", i))
        tok = comm.part_grads_done(i, FFN, GB) if comm is not None else None
        dh1, dh1b, dg1, db1 = _mm_sum(
            [(du1, V(W["f_w_up"], (i,), cols=D_FF)), (du2, V(W["f_w_up"], (i,), c0=D_FF)), (dgl, V(W["ple_w_gate"], (i,)))],
            "nt", f"l{i}_dx1", add=dh2, add_scale=ALPHA, after=tok, ln_bwd=(L["h1"], W["ln_g"][i, 0][None], None))
        G["ln_g"][i] = jnp.concatenate([dg1, dg2], axis=0)
        G["ln_b"][i] = jnp.concatenate([db1, db2], axis=0)
        if i % 2 == 0:
            GB["e_w_out"] = _mm(L["ya"], dh1b, "tn", f"l{i}_dw_out_a", dst=into("e_w_out", j))
            GB["e_w_out"] = _mm(L["yb"], dh1b, "tn", f"l{i}_dw_out_b", dst=into("e_w_out", j, r0=D))
            dya = _mm(dh1b, V(W["e_w_out"], (j,), rows=D), "nt", f"l{i}_dya")
            dyb = _mm(dh1b, V(W["e_w_out"], (j,), r0=D), "nt", f"l{i}_dyb")
            dac, dga, dba = _ln_silu_bwd(L["ac"], dya, W["e_ln_a_g"][j][None], W["e_ln_a_b"][j][None], f"l{i}_ln_a_bwd")
            G["e_ln_a_g"][j], G["e_ln_a_b"][j] = dga[0], dba[0]
            dal, dag, dwa, dbca = _conv_a_bwd(L["ua"], W["e_conv_a_w"][j], dac, f"l{i}_conv_a_bwd")
            G["e_conv_a_w"][j] = _fold_rows(dwa, CONV_A)
            G["e_conv_a_b"][j] = dbca.sum(0)
            dy, dz, dgn = _gate_rms_bwd(L["y"], L["z"], dyb, W["e_norm_b_g"][j][None], f"l{i}_gate_rms_bwd")
            G["e_norm_b_g"][j] = dgn[0]
            dxs, dbs, dcs, sq_col, cms_row, dda_col, dda_row, ddp = _ssd_bwd(L["xc"], *L["sm"], W["e_d_skip"][j], L["states"], dy,
                                                                             f"l{i}_ssd_bwd")
            draw, G["e_dt_bias"][j], G["e_a_log"][j] = _ssd_small_grads(L["udt"][:, :N_HEADS], W["e_dt_bias"][j], W["e_a_log"][j],
                                                                       sq_col, cms_row, dda_col, dda_row)
            G["e_d_skip"][j] = ddp[:, :, 0, :].sum(0).reshape(N_HEADS, HEAD_P).sum(1)
            dxu, dwb, dbcb = _conv_b_bwd(L["xu"], W["e_conv_b_w"][j], W["e_conv_b_b"][j][None], dxs, dbs, dcs, f"l{i}_conv_b_bwd")
            G["e_conv_b_w"][j] = _fold_rows(dwb, CONV_B)
            G["e_conv_b_b"][j] = dbcb.sum(0)
            dudt = _padcols(draw, LANES)
            w_in = L["w_in"]
            xb_l = L["xb"]
            for nm, dseg, c0 in (("al", dal, 0), ("ag", dag, D), ("z", dz, SEG_Z[0]), ("xbc", dxu, SEG_X[0]), ("dt", dudt, SEG_DT[0])):
                GB["e_w_in"] = _mm(xb_l, dseg, "tn", f"l{i}_dw_in_{nm}", dst=into("e_w_in", j, c0=c0))
            dx = _mm_sum([(dal, w_in(SEG_A, cols=D)), (dag, w_in(SEG_A, c0=D, cols=D)), (dz, w_in(SEG_Z)),
                          (V(dxu, cols=D), w_in(SEG_X, cols=D)), (V(dxu, c0=D), w_in(SEG_X, c0=D, cols=D)), (dudt, w_in(SEG_DT))],
                         "nt", f"l{i}_dx", add=dh1, add_scale=ALPHA)
        else:
            GB["o_w_out"] = _mm(L["sc"], dh1b, "tn", f"l{i}_dw_out", dst=into("o_w_out", j))
            dsc = _mm(dh1b, V(W["o_w_out"], (j,)), "nt", f"l{i}_dsc")
            dbg, dcg, dv, dwc = _conv_c_bwd(L["uo"], W["o_conv_w"][j], dsc, f"l{i}_conv_c_bwd")
            G["o_conv_w"][j] = _fold_rows(dwc, CONV_C)
            xb_l = L["xb"]
            for nm, dseg, c0 in (("bg", dbg, 0), ("cg", dcg, D), ("v", dv, 2 * D)):
                GB["o_w_in"] = _mm(xb_l, dseg, "tn", f"l{i}_dw_in_{nm}", dst=into("o_w_in", j, c0=c0))
            below = saved[i - 1]
            ln2_done = _mm_sum([(dseg, V(W["o_w_in"], (j,), c0=c0, cols=D)) for dseg, c0 in ((dbg, 0), (dcg, D), (dv, 2 * D))],
                               "nt", f"l{i}_dx", add=dh1, add_scale=ALPHA,
                               ln_bwd=(below["h2"], W["ln_g"][i - 1, 1][None], (below["pp"], below["gl"])))
        tok = comm.part_grads_done(i, MIX, GB) if comm is not None else None
    grads = {n: jnp.stack(v) for n, v in G.items()}
    return sq, dx, GB, grads


def _ssd_small_inputs(raw, dt_bias, a_log):
    T = raw.shape[0]
    a = -jnp.exp(a_log)
    rg = raw.reshape(T, N_GROUPS, 4)
    raw_col = jnp.pad(jnp.transpose(rg, (1, 0, 2)), ((0, 0), (0, 0), (0, LANES - 4)))
    raw_row = jnp.pad(jnp.transpose(rg, (1, 2, 0)), ((0, 0), (0, SUBLANES - 4), (0, 0)))

    def colv(v):
        return jnp.pad(v.reshape(N_GROUPS, 1, 4), ((0, 0), (0, 0), (0, LANES - 4)))

    def rowv(v):
        return jnp.pad(v.reshape(N_GROUPS, 4, 1), ((0, 0), (0, SUBLANES - 4), (0, 0)))

    return raw_col, raw_row, colv(dt_bias), rowv(dt_bias), colv(a), rowv(a)


def _ssd_small_grads(raw, dt_bias, a_log, sq_col, cms_row, dda_col, dda_row):
    T = raw.shape[0]

    def join(col, row):
        c = jnp.transpose(col[:, :, :4], (1, 0, 2)).reshape(T, N_HEADS)
        r = jnp.transpose(row[:, :4, :], (2, 0, 1)).reshape(T, N_HEADS)
        return c + r

    a = -jnp.exp(a_log)
    pre = raw + dt_bias
    dt = jax.nn.softplus(pre)
    dda = join(dda_col, dda_row)
    ddt = join(sq_col, cms_row) + a * dda
    draw = ddt * jax.nn.sigmoid(pre)
    da = jnp.sum(dt * dda, axis=0)
    return draw, jnp.sum(draw, axis=0), da * a


def kernel(x, p, e_w_in, e_conv_a_w, e_conv_a_b, e_ln_a_g, e_ln_a_b, e_conv_b_w, e_conv_b_b, e_dt_bias, e_a_log, e_d_skip, e_norm_b_g, e_w_out, o_w_in, o_conv_w, o_w_out, f_w_up, f_conv_w, f_conv_b, f_w_down, ple_w_proj, ple_w_gate, ln_g, ln_b, loss_target, m_e_w_in, m_e_conv_a_w, m_e_conv_a_b, m_e_ln_a_g, m_e_ln_a_b, m_e_conv_b_w, m_e_conv_b_b, m_e_dt_bias, m_e_a_log, m_e_d_skip, m_e_norm_b_g, m_e_w_out, m_o_w_in, m_o_conv_w, m_o_w_out, m_f_w_up, m_f_conv_w, m_f_conv_b, m_f_w_down, m_ple_w_proj, m_ple_w_gate, m_ln_g, m_ln_b, v_e_w_in, v_e_conv_a_w, v_e_conv_a_b, v_e_ln_a_g, v_e_ln_a_b, v_e_conv_b_w, v_e_conv_b_b, v_e_dt_bias, v_e_a_log, v_e_d_skip, v_e_norm_b_g, v_e_w_out, v_o_w_in, v_o_conv_w, v_o_w_out, v_f_w_up, v_f_conv_w, v_f_conv_b, v_f_w_down, v_ple_w_proj, v_ple_w_gate, v_ln_g, v_ln_b):
    args = dict(locals())
    w_shard = {n: args[n] for n in WEIGHT_ORDER}
    m_shard = {n: args["m_" + n] for n in WEIGHT_ORDER}
    v_shard = {n: args["v_" + n] for n in WEIGHT_ORDER}
    xi, yi, ci = lax.axis_index("x"), lax.axis_index("y"), lax.axis_index("c")
    chip = 2 * xi + yi

    mevec = jnp.stack([chip, ci]).astype(jnp.int32)
    small_shapes = [w_shard[n].shape for n, _ in SMALL_SHARDED]
    sr = _small_rows(small_shapes)
    ws = _pack_rows([w_shard[n] for n, _ in SMALL_SHARDED], LANES, sr, F32)
    W = {n: w_shard[n] for n in SMALL_REPL}
    W.update({s[0]: Layers(s[2]) for s in TENSORS})
    w_items = [_group_items(layers) for layers in W_GROUPS]
    g_items = [_group_items(layers) for layers in G_GROUPS]

    def install(items, fulls):
        for it, f in zip(items, fulls):
            if it[0] == "e_w_in":
                f = jnp.transpose(f.reshape(it[2], 4, D, E_IN // 4), (0, 2, 1, 3)).reshape(it[2], D, E_IN)
                f = jnp.pad(f, ((0, 0), (0, 0), (0, E_PAD - E_IN)))
            W[it[0]].put(f, it[6])

    casts = [[_cast_into(w_shard[it[0]], it, mevec[:1]) for it in items] for items in w_items]
    fulls, ssem, rsem, _ = _gather_start(casts[0], w_items[0], "gather_start_0")
    xb0 = _to_bf16(x[0], "x_bf16")
    fulls = _gather_wait(fulls, ssem, rsem, [c for grp in casts[1:] for c in grp] + [xb0], w_items[0], "gather_wait_0")
    fulls, WS = _gather_fwd(fulls, w_items[0], "gather_fwd_0", ws)
    install(w_items[0], fulls)
    parts_s = [_unpack_rows(WS[k], small_shapes) for k in range(4)]
    for idx, (n, ax) in enumerate(SMALL_SHARDED):
        W[n] = jnp.concatenate([parts_s[k][idx] for k in range(4)], axis=ax)

    class Comm:
        sent = {}
        started = {}
        tail = fulls[0]

        def start_next(self, gi):
            if gi >= len(w_items):
                return None
            self.started[gi] = _gather_start(casts[gi], w_items[gi], f"gather_start_{gi}", self.tail)
            return self.started[gi][3]

        def part_starts(self, layer, part, after):
            if (layer, part) == W_GROUPS[0][0]:
                return self.start_next(1)
            for gi in range(1, len(W_GROUPS)):
                if W_GROUPS[gi][0] == (layer, part):
                    fulls, ssem, rsem, _ = self.started[gi]
                    fulls = _gather_wait(fulls, ssem, rsem, after, w_items[gi], f"gather_wait_{gi}")
                    fulls, _ = _gather_fwd(fulls, w_items[gi], f"gather_fwd_{gi}")
                    install(w_items[gi], fulls)
                    self.tail = fulls[0]
                    return self.start_next(gi + 1)
            return None

        swapping = None

        def swap_landed(self, after):
            if self.swapping is None:
                return None
            gi, thru, ssem, rsem = self.swapping
            items = g_items[gi]
            thru = _swap_wait(thru, ssem, rsem, after, items, f"swap_wait_{gi}")
            gs, ras = thru[:len(items)], thru[len(items):]
            ps = [_add_half(g, ra, it, mevec[1:]) for g, ra, it in zip(gs, ras, items)]
            thru, ssem, rsem, tok = _scatter_start(ps, items, f"scatter_start_{gi}")
            self.sent[gi] = (thru, ssem, rsem, tok)
            self.swapping = None
            return tok

        def part_grads_done(self, layer, part, GB):
            tok = self.swap_landed(GB.last)
            for gi, members in enumerate(G_GROUPS):
                if members[-1] == (layer, part):
                    items = g_items[gi]
                    gs = []
                    for it in items:
                        g = GB[(it[0], gi)]
                        if it[0] == "e_w_in":
                            g = jnp.transpose(g[:, :, :E_IN].reshape(it[2], D, 4, E_IN // 4), (0, 2, 1, 3)).reshape(it[2], 4 * D, E_IN // 4)
                        gs.append(g)
                    thru, ssem, rsem, tok = _swap_start(gs, items, f"swap_start_{gi}", tok)
                    self.swapping = (gi, thru, ssem, rsem)
            return tok

    comm = Comm()

    sq, dx, GB, G = _local_step(x[0], p, loss_target[0], W, comm, xb0)
    loss = lax.psum(0.5 * sq[0, 0] / D, ("x", "y", "c"))
    grad_x = dx[None]

    def shard_of(g, ax, k):
        n = g.shape[ax] // 4
        return lax.slice_in_dim(g, k * n, (k + 1) * n, axis=ax)

    reduced, updated = {}, {}
    comm.swap_landed(dx)
    after = comm.sent[len(g_items) - 1][3]
    for gi, items in enumerate(g_items):
        thru, ssem, rsem, _ = comm.sent[gi]
        thru = _scatter_wait(thru, ssem, rsem, after, items, f"scatter_wait_{gi}")
        ps, rcs = thru[:len(items)], thru[len(items):]
        rs = [_sum_own(pt, rc, it, mevec, reduced.get(it[0])) for pt, rc, it in zip(ps, rcs, items)]
        rs = _join_halves(rs, items, f"join_halves_{gi}")
        reduced.update({it[0]: r for it, r in zip(items, rs)})
        for it in items:
            n = it[0]
            if n != "e_w_in":
                updated[n] = _adamw_layers(w_shard[n], reduced[n], m_shard[n], v_shard[n], it[6], it[2], updated.get(n),
                                           f"adamw_{n}_{it[6]}")
        after = updated[items[-1][0]][0]
    *updated["e_w_in"], reduced["e_w_in"] = _adamw_minor_rows(w_shard["e_w_in"], reduced["e_w_in"], m_shard["e_w_in"],
                                                              v_shard["e_w_in"], "adamw_e_w_in")

    small_all = ([shard_of(G[n], ax, k) for k in range(4) for n, ax in SMALL_SHARDED] + [G[n] for n in SMALL_REPL])
    small_all_shapes = [t.shape for t in small_all]
    mr = _small_rows(small_all_shapes)
    sg = _sum8(_allgather_small(_pack_rows(small_all, LANES, mr, F32)), mr)
    sparts = _unpack_rows(sg, small_all_shapes)
    ns = len(SMALL_SHARDED)
    gsmall = {}
    for idx, (n, ax) in enumerate(SMALL_SHARDED):
        stacked = jnp.stack([sparts[k * ns + idx] for k in range(4)])
        gsmall[n] = lax.dynamic_index_in_dim(stacked, chip, axis=0, keepdims=False)
    for idx, n in enumerate(SMALL_REPL):
        gsmall[n] = sparts[4 * ns + idx]

    grads, deltas, new_m, new_v = [], [], [], []
    for n in WEIGHT_ORDER:
        if n in reduced:
            g, (d, mn, vn) = reduced[n], updated[n]
        else:
            g = gsmall[n]
            d, mn, vn = _adamw(w_shard[n], g, m_shard[n], v_shard[n], f"adamw_{n}")
        grads.append(g)
        deltas.append(d)
        new_m.append(mn)
        new_v.append(vn)
    return (loss, grad_x, *grads, *deltas, *new_m, *new_v)
```

```python
import functools
import math

import jax
import jax.numpy as jnp
from jax import lax
from jax.experimental import pallas as pl
from jax.experimental.pallas import tpu as pltpu

F32 = jnp.float32
BF16 = jnp.bfloat16
MESH = pl.DeviceIdType.MESH

DEPTH = 4
ALPHA = (2.0 * DEPTH) ** 0.25
LN_EPS = 1e-5
D = 1024
HEAD_P = 64
N_STATE = 128
N_HEADS = 16
N_GROUPS = 4
CONV_A, CONV_B, CONV_C, CONV_F = 31, 4, 3, 3
D_FF = 2816
PLE = 256
E_IN = 5136

ADAM_LR, ADAM_B1, ADAM_B2, ADAM_EPS, ADAM_WD, ADAM_STEP = 0.001, 0.9, 0.999, 1e-08, 0.01, 10

LANES = 128
SUBLANES = 8
VMEM_LIMIT = 56 * 1024 * 1024
SSD_Q = 128
CONV_R = 128
CONV_PAD = 32
ROW_T = 512


def _cparams(sem=None):
    return pltpu.CompilerParams(dimension_semantics=sem, vmem_limit_bytes=VMEM_LIMIT)


def _sig(v):
    return jax.nn.sigmoid(v)


_DIMS = {"nn": (((1,), (0,)), ((), ())), "nt": (((1,), (1,)), ((), ())), "tn": (((0,), (0,)), ((), ()))}


class Layers:
    def __init__(self, n_layers):
        self.where = [None] * n_layers

    def put(self, arr, l0):
        for k in range(arr.shape[0]):
            self.where[l0 + k] = (arr, k)


class V:
    def __init__(self, arr, lead=(), r0=0, c0=0, rows=None, cols=None):
        if isinstance(arr, Layers):
            arr, k = arr.where[lead[0]]
            lead = (k,) + tuple(lead[1:])
        self.arr, self.lead, self.r0, self.c0 = arr, tuple(lead), r0, c0
        R, C = arr.shape[-2:]
        self.rows = R - r0 if rows is None else rows
        self.cols = C - c0 if cols is None else cols

    def spec(self, br, bc, fn):
        assert self.r0 % br == 0 and self.c0 % bc == 0, (self.r0, self.c0, br, bc)
        ro, co, lead = self.r0 // br, self.c0 // bc, self.lead

        def index(i, j, k):
            r, c = fn(i, j, k)
            return lead + (r + ro, c + co)

        return pl.BlockSpec((None,) * len(lead) + (br, bc), index)


def _v(t):
    return t if isinstance(t, V) else V(t)


def _tile(n, offs, cands):
    for c in cands:
        if n % c == 0 and all(o % c == 0 for o in offs):
            return c
    raise ValueError((n, offs))


_TILES = (1024, 1408, 512, 256, 128)


def _mm(a, b, mode, name, out_dtype=F32, add=None, add_scale=1.0, dst=None, after=None):
    a, b = _v(a), _v(b)
    add = _v(add) if add is not None else None
    if mode == "nn":
        M, K, K2, N = a.rows, a.cols, b.rows, b.cols
        am, ak, bk, bn = a.r0, a.c0, b.r0, b.c0
    elif mode == "nt":
        M, K, N, K2 = a.rows, a.cols, b.rows, b.cols
        am, ak, bn, bk = a.r0, a.c0, b.r0, b.c0
    else:
        K, M, K2, N = a.rows, a.cols, b.rows, b.cols
        ak, am, bk, bn = a.r0, a.c0, b.r0, b.c0
    assert K == K2, (name, mode, M, K, K2, N)
    if dst is None:
        buf, full_shape, o_lead, o_r0, o_c0 = None, (M, N), (), 0, 0
    else:
        buf, full_shape, o_lead, o_r0, o_c0 = dst
    tm = _tile(M, [am, o_r0] + ([add.r0] if add else []), _TILES)
    tn = _tile(N, [bn, o_c0] + ([add.c0] if add else []), _TILES)
    narrow = a.arr.dtype.itemsize == 2 and b.arr.dtype.itemsize == 2
    tk = _tile(K, [ak, bk], ((2048,) if narrow else ()) + _TILES)
    nk = K // tk
    has_add, has_buf, has_after = add is not None, buf is not None, after is not None

    def body(*refs):
        a_ref, b_ref = refs[0], refs[1]
        add_ref = refs[2] if has_add else None
        o_ref = refs[2 + has_add + has_buf + has_after]

        def finish(r):
            if has_add:
                r = r + add_scale * add_ref[...].astype(F32)
            o_ref[...] = r.astype(o_ref.dtype)

        part = lax.dot_general(a_ref[...].astype(BF16), b_ref[...].astype(BF16), _DIMS[mode], preferred_element_type=F32)
        if nk == 1:
            finish(part)
        else:
            acc_ref = refs[-1]
            k = pl.program_id(2)

            @pl.when(k == 0)
            def _():
                acc_ref[...] = part

            @pl.when(jnp.logical_and(k > 0, k < nk - 1))
            def _():
                acc_ref[...] += part

            @pl.when(k == nk - 1)
            def _():
                finish(acc_ref[...] + part)

    if mode == "tn":
        a_spec = a.spec(tk, tm, lambda i, j, k: (k, i))
    else:
        a_spec = a.spec(tm, tk, lambda i, j, k: (i, k))
    if mode == "nt":
        b_spec = b.spec(tn, tk, lambda i, j, k: (j, k))
    else:
        b_spec = b.spec(tk, tn, lambda i, j, k: (k, j))
    in_specs, args = [a_spec, b_spec], [a.arr, b.arr]
    if has_add:
        in_specs.append(add.spec(tm, tn, lambda i, j, k: (i, j)))
        args.append(add.arr)
    aliases = {}
    if has_buf:
        aliases = {len(args): 0}
        in_specs.append(pl.BlockSpec(memory_space=pl.ANY))
        args.append(buf)
        out_dtype = buf.dtype
    if has_after:
        in_specs.append(pl.BlockSpec(memory_space=pl.ANY))
        args.append(after)
    o_view = V(jax.ShapeDtypeStruct(full_shape, out_dtype), o_lead, o_r0, o_c0, M, N)
    return pl.pallas_call(
        body, name=name, grid=(M // tm, N // tn, nk), in_specs=in_specs, out_specs=o_view.spec(tm, tn, lambda i, j, k: (i, j)),
        out_shape=jax.ShapeDtypeStruct(full_shape, out_dtype), input_output_aliases=aliases,
        scratch_shapes=[pltpu.VMEM((tm, tn), F32)] if nk > 1 else [],
        compiler_params=_cparams(("parallel", "parallel", "arbitrary")))(*args)


def _ln_stats(h):
    mu = jnp.mean(h, axis=-1, keepdims=True)
    hc = h - mu
    var = jnp.mean(hc * hc, axis=-1, keepdims=True)
    rstd = lax.rsqrt(var + LN_EPS)
    return hc * rstd, rstd


def _ln_bwd_math(dyv, h, g):
    xhat, rstd = _ln_stats(h)
    dxh = dyv * g
    dh = rstd * (dxh - jnp.mean(dxh, axis=-1, keepdims=True) - xhat * jnp.mean(dxh * xhat, axis=-1, keepdims=True))
    return dh, jnp.sum(dyv * xhat, axis=0, keepdims=True), jnp.sum(dyv, axis=0, keepdims=True)


def _mm_sum(pairs, mode, name, out_dtype=F32, add=None, add_scale=1.0, after=None, ln_fwd=None, ln_bwd=None):
    pairs = [(_v(a), _v(b)) for a, b in pairs]
    add = _v(add) if add is not None else None
    M = pairs[0][0].rows
    N = pairs[0][1].cols if mode == "nn" else pairs[0][1].rows
    b_offs = [(b.c0 if mode == "nn" else b.r0) for _, b in pairs]
    fused = ln_fwd is not None or ln_bwd is not None
    tn = N if fused else _tile(N, b_offs + ([add.c0] if add else []), (512, 256, 128))
    ple_n = 2 if fused and (ln_fwd[1] if ln_fwd is not None else ln_bwd[2]) is not None else 0

    def footprint(tm):
        ab = sum(tm * a.cols * a.arr.dtype.itemsize + a.cols * tn * b.arr.dtype.itemsize for a, b in pairs)
        io = tm * tn * 4 * (add is not None)
        if ln_fwd is not None:
            io += tm * tn * (4 * (1 + ple_n) + 10)
        elif ln_bwd is not None:
            io += tm * tn * (4 * (1 + ple_n) + 6 + 2 * ple_n)
        else:
            io += tm * tn * 4
        return 2 * (ab + io)

    offs = [a.r0 for a, _ in pairs] + ([add.r0] if add else [])
    fits = [t for t in (512, 256, 128) if M % t == 0 and all(o % t == 0 for o in offs) and footprint(t) <= (VMEM_LIMIT * 3) // 4]
    tm = fits[0] if fits else _tile(M, offs, (128,))
    assert not fused or (N == D and all(o == 0 for o in b_offs))
    n_p, has_add, has_after = len(pairs), add is not None, after is not None
    ple = (ln_fwd[1] if ln_fwd is not None else ln_bwd[2]) if fused else None
    has_ple = ple is not None

    def body(*refs):
        acc = None
        for i in range(n_p):
            part = lax.dot_general(refs[2 * i][...].astype(BF16), refs[2 * i + 1][...].astype(BF16), _DIMS[mode],
                                   preferred_element_type=F32)
            acc = part if acc is None else acc + part
        pos = 2 * n_p
        if has_add:
            acc = acc + add_scale * refs[pos][...].astype(F32)
            pos += 1
        if ln_fwd is not None:
            x_ref = refs[pos]
            pp_ref, gl_ref = (refs[pos + 1], refs[pos + 2]) if has_ple else (None, None)
            pos += 1 + 2 * has_ple
            g_ref, b_ref = refs[pos], refs[pos + 1]
            h_ref, y_ref, yb_ref = refs[pos + 2 + has_after:]
            h = ALPHA * x_ref[...] + acc
            if has_ple:
                h = h + pp_ref[...] * _sig(gl_ref[...])
            xhat, _ = _ln_stats(h)
            y = xhat * g_ref[...] + b_ref[...]
            h_ref[...] = h
            y_ref[...] = y
            yb_ref[...] = y.astype(BF16)
        elif ln_bwd is not None:
            h_ref, g_ref = refs[pos], refs[pos + 1]
            pp_ref, gl_ref = (refs[pos + 2], refs[pos + 3]) if has_ple else (None, None)
            outs = refs[pos + 2 + 2 * has_ple + has_after:]
            dh_ref, dhb_ref, dg_ref, db_ref = outs[:4]

            @pl.when(pl.program_id(0) == 0)
            def _():
                dg_ref[...] = jnp.zeros_like(dg_ref)
                db_ref[...] = jnp.zeros_like(db_ref)

            dh, dg, db = _ln_bwd_math(acc, h_ref[...], g_ref[...])
            dg_ref[...] += dg
            db_ref[...] += db
            dh_ref[...] = dh
            dhb_ref[...] = dh.astype(BF16)
            if has_ple:
                s = _sig(gl_ref[...])
                outs[4][...] = (dh * s).astype(BF16)
                outs[5][...] = (dh * pp_ref[...] * s * (1.0 - s)).astype(BF16)
        else:
            o_ref = refs[pos + has_after]
            o_ref[...] = acc.astype(o_ref.dtype)

    in_specs, args = [], []
    for a, b in pairs:
        K = a.cols
        assert K == (b.rows if mode == "nn" else b.cols), (name, K)
        in_specs.append(a.spec(tm, K, lambda i, j, k: (i, 0)))
        in_specs.append(b.spec(K, tn, lambda i, j, k: (0, j)) if mode == "nn" else b.spec(tn, K, lambda i, j, k: (j, 0)))
        args += [a.arr, b.arr]
    if has_add:
        in_specs.append(add.spec(tm, tn, lambda i, j, k: (i, j)))
        args.append(add.arr)
    row = pl.BlockSpec((tm, tn), lambda i, j, k: (i, j))
    vec = pl.BlockSpec((1, tn), lambda i, j, k: (0, 0))
    if ln_fwd is not None:
        x, _, g, b = ln_fwd
        extra = [x] + (list(ple) if has_ple else []) + [g, b]
        in_specs += [row] * (1 + 2 * has_ple) + [vec, vec]
        args += extra
        out_specs = (row, row, row)
        out_shape = (_sds((M, N)), _sds((M, N)), _sds((M, N), BF16))
    elif ln_bwd is not None:
        h, g, _ = ln_bwd
        in_specs += [row, vec] + [row] * (2 * has_ple)
        args += [h, g] + (list(ple) if has_ple else [])
        out_specs = (row, row, vec, vec) + ((row, row) if has_ple else ())
        out_shape = (_sds((M, N)), _sds((M, N), BF16), _sds((1, N)), _sds((1, N))) + ((_sds((M, N), BF16),) * 2 if has_ple else ())
    else:
        out_specs, out_shape = row, jax.ShapeDtypeStruct((M, N), out_dtype)
    if has_after:
        in_specs.append(pl.BlockSpec(memory_space=pl.ANY))
        args.append(after)
    return pl.pallas_call(
        body, name=name, grid=(M // tm, N // tn, 1), in_specs=in_specs, out_specs=out_specs, out_shape=out_shape,
        compiler_params=_cparams(("arbitrary",) * 3 if ln_bwd is not None else ("parallel", "parallel", "arbitrary")))(*args)


def _rows(T, width=D):
    return pl.BlockSpec((ROW_T, width), lambda i: (i, 0))


def _vec(width=D):
    return pl.BlockSpec((1, width), lambda i: (0, 0))


def _res_ln_fwd(x, adds, ple, g, b, name):
    T = x.shape[0]
    n_add = len(adds)
    has_ple = ple is not None

    def body(*refs):
        x_ref = refs[0]
        add_refs = refs[1:1 + n_add]
        pos = 1 + n_add
        if has_ple:
            pp_ref, gl_ref = refs[pos], refs[pos + 1]
            pos += 2
        g_ref, b_ref, h_ref, y_ref, yb_ref = refs[pos:pos + 5]
        h = ALPHA * x_ref[...]
        for r in add_refs:
            h = h + r[...]
        if has_ple:
            h = h + pp_ref[...] * _sig(gl_ref[...])
        xhat, _ = _ln_stats(h)
        y = xhat * g_ref[...] + b_ref[...]
        h_ref[...] = h
        y_ref[...] = y
        yb_ref[...] = y.astype(BF16)

    n_in = 1 + n_add + (2 if has_ple else 0)
    args = (x,) + tuple(adds) + (tuple(ple) if has_ple else ()) + (g, b)
    return pl.pallas_call(
        body, name=name, grid=(T // ROW_T,), in_specs=[_rows(T)] * n_in + [_vec(), _vec()],
        out_specs=(_rows(T), _rows(T), _rows(T)),
        out_shape=(jax.ShapeDtypeStruct((T, D), F32), jax.ShapeDtypeStruct((T, D), F32), jax.ShapeDtypeStruct((T, D), BF16)),
        compiler_params=_cparams(("parallel",)))(*args)


def _res_ln_bwd(dy, h, g, ple, name):
    T = dy.shape[0]
    has_ple = ple is not None

    def body(*refs):
        if has_ple:
            dy_ref, h_ref, g_ref, pp_ref, gl_ref, dh_ref, dhb_ref, dg_ref, db_ref, dpp_ref, dgl_ref = refs
        else:
            dy_ref, h_ref, g_ref, dh_ref, dhb_ref, dg_ref, db_ref = refs
        i = pl.program_id(0)

        @pl.when(i == 0)
        def _():
            dg_ref[...] = jnp.zeros_like(dg_ref)
            db_ref[...] = jnp.zeros_like(db_ref)

        dyv = dy_ref[...]
        xhat, rstd = _ln_stats(h_ref[...])
        dg_ref[...] += jnp.sum(dyv * xhat, axis=0, keepdims=True)
        db_ref[...] += jnp.sum(dyv, axis=0, keepdims=True)
        dxh = dyv * g_ref[...]
        dh = rstd * (dxh - jnp.mean(dxh, axis=-1, keepdims=True) - xhat * jnp.mean(dxh * xhat, axis=-1, keepdims=True))
        dh_ref[...] = dh
        dhb_ref[...] = dh.astype(BF16)
        if has_ple:
            s = _sig(gl_ref[...])
            dpp_ref[...] = (dh * s).astype(BF16)
            dgl_ref[...] = (dh * pp_ref[...] * s * (1.0 - s)).astype(BF16)

    args = (dy, h, g) + (tuple(ple) if has_ple else ())
    in_specs = [_rows(T), _rows(T), _vec()] + ([_rows(T), _rows(T)] if has_ple else [])
    out_specs = [_rows(T), _rows(T), _vec(), _vec()] + ([_rows(T), _rows(T)] if has_ple else [])
    out_shape = [jax.ShapeDtypeStruct((T, D), F32), jax.ShapeDtypeStruct((T, D), BF16),
                 jax.ShapeDtypeStruct((1, D), F32), jax.ShapeDtypeStruct((1, D), F32)]
    if has_ple:
        out_shape += [jax.ShapeDtypeStruct((T, D), BF16), jax.ShapeDtypeStruct((T, D), BF16)]
    return pl.pallas_call(
        body, name=name, grid=(T // ROW_T,), in_specs=in_specs, out_specs=tuple(out_specs), out_shape=tuple(out_shape),
        compiler_params=_cparams(("arbitrary",)))(*args)


def _ln_silu_fwd(ac, g, b, name):
    T = ac.shape[0]

    def body(a_ref, g_ref, b_ref, o_ref):
        xhat, _ = _ln_stats(a_ref[...])
        ln = xhat * g_ref[...] + b_ref[...]
        o_ref[...] = (ln * _sig(ln)).astype(BF16)

    return pl.pallas_call(
        body, name=name, grid=(T // ROW_T,), in_specs=[_rows(T), _vec(), _vec()], out_specs=_rows(T),
        out_shape=jax.ShapeDtypeStruct((T, D), BF16), compiler_params=_cparams(("parallel",)))(ac, g, b)


def _ln_silu_bwd(ac, dya, g, b, name):
    T = ac.shape[0]

    def body(a_ref, d_ref, g_ref, b_ref, da_ref, dg_ref, db_ref):
        i = pl.program_id(0)

        @pl.when(i == 0)
        def _():
            dg_ref[...] = jnp.zeros_like(dg_ref)
            db_ref[...] = jnp.zeros_like(db_ref)

        xhat, rstd = _ln_stats(a_ref[...])
        ln = xhat * g_ref[...] + b_ref[...]
        s = _sig(ln)
        dln = d_ref[...] * s * (1.0 + ln * (1.0 - s))
        dg_ref[...] += jnp.sum(dln * xhat, axis=0, keepdims=True)
        db_ref[...] += jnp.sum(dln, axis=0, keepdims=True)
        dxh = dln * g_ref[...]
        da_ref[...] = rstd * (dxh - jnp.mean(dxh, axis=-1, keepdims=True)
                              - xhat * jnp.mean(dxh * xhat, axis=-1, keepdims=True))

    return pl.pallas_call(
        body, name=name, grid=(T // ROW_T,), in_specs=[_rows(T), _rows(T), _vec(), _vec()],
        out_specs=(_rows(T), _vec(), _vec()),
        out_shape=(jax.ShapeDtypeStruct((T, D), F32), jax.ShapeDtypeStruct((1, D), F32), jax.ShapeDtypeStruct((1, D), F32)),
        compiler_params=_cparams(("arbitrary",)))(ac, dya, g, b)


def _gate_rms_fwd(y, z, g, name):
    T = y.shape[0]

    def body(y_ref, z_ref, g_ref, o_ref):
        zv = z_ref[...]
        yg = y_ref[...] * (zv * _sig(zv))
        r = lax.rsqrt(jnp.mean(yg * yg, axis=-1, keepdims=True) + LN_EPS)
        o_ref[...] = (yg * r * g_ref[...]).astype(BF16)

    return pl.pallas_call(
        body, name=name, grid=(T // ROW_T,), in_specs=[_rows(T), _rows(T), _vec()], out_specs=_rows(T),
        out_shape=jax.ShapeDtypeStruct((T, D), BF16), compiler_params=_cparams(("parallel",)))(y, z, g)


def _gate_rms_bwd(y, z, dout, g, name):
    T = y.shape[0]

    def body(y_ref, z_ref, d_ref, g_ref, dy_ref, dz_ref, dg_ref):
        i = pl.program_id(0)

        @pl.when(i == 0)
        def _():
            dg_ref[...] = jnp.zeros_like(dg_ref)

        yv, zv, dv = y_ref[...], z_ref[...], d_ref[...]
        s = _sig(zv)
        sz = zv * s
        yg = yv * sz
        r = lax.rsqrt(jnp.mean(yg * yg, axis=-1, keepdims=True) + LN_EPS)
        dg_ref[...] += jnp.sum(dv * yg * r, axis=0, keepdims=True)
        dn = dv * g_ref[...]
        dyg = r * dn - yg * (r * r * r) * jnp.mean(dn * yg, axis=-1, keepdims=True)
        dy_ref[...] = dyg * sz
        dz_ref[...] = (dyg * yv * s * (1.0 + zv * (1.0 - s))).astype(BF16)

    return pl.pallas_call(
        body, name=name, grid=(T // ROW_T,), in_specs=[_rows(T), _rows(T), _rows(T), _vec()],
        out_specs=(_rows(T), _rows(T), _vec()),
        out_shape=(jax.ShapeDtypeStruct((T, D), F32), jax.ShapeDtypeStruct((T, D), BF16), jax.ShapeDtypeStruct((1, D), F32)),
        compiler_params=_cparams(("arbitrary",)))(y, z, dout, g)


def _to_bf16(x, name):
    T = x.shape[0]

    def body(x_ref, o_ref):
        o_ref[...] = x_ref[...].astype(BF16)

    return pl.pallas_call(body, name=name, grid=(T // ROW_T,), in_specs=[_rows(T)], out_specs=_rows(T),
                          out_shape=jax.ShapeDtypeStruct((T, D), BF16), compiler_params=_cparams(("parallel",)))(x)


def _loss_head(y, target, name):
    T = y.shape[0]

    def body(y_ref, t_ref, s_ref, d_ref):
        i = pl.program_id(0)

        @pl.when(i == 0)
        def _():
            s_ref[...] = jnp.zeros_like(s_ref)

        err = y_ref[...] - t_ref[...]
        s_ref[...] += jnp.sum(jnp.sum(err * err, axis=1, keepdims=True), axis=0, keepdims=True)
        d_ref[...] = err * (1.0 / D)

    return pl.pallas_call(
        body, name=name, grid=(T // ROW_T,), in_specs=[_rows(T), _rows(T)],
        out_specs=(pl.BlockSpec((SUBLANES, LANES), lambda i: (0, 0)), _rows(T)),
        out_shape=(jax.ShapeDtypeStruct((SUBLANES, LANES), F32), jax.ShapeDtypeStruct((T, D), F32)),
        compiler_params=_cparams(("arbitrary",)))(y, target)


def _taps_fwd(pad_ref, w_ref, K, base):
    off = CONV_PAD - (K - 1)
    acc = w_ref[0:1, :] * pad_ref[pl.ds(base + off, CONV_R), :]
    for k in range(1, K):
        acc = acc + w_ref[k:k + 1, :] * pad_ref[pl.ds(base + off + k, CONV_R), :]
    return acc


def _taps_bwd(padd_ref, w_ref, K, base):
    acc = w_ref[0:1, :] * padd_ref[pl.ds(base + (K - 1), CONV_R), :]
    for k in range(1, K):
        acc = acc + w_ref[k:k + 1, :] * padd_ref[pl.ds(base + (K - 1) - k, CONV_R), :]
    return acc


def _f32(ref, rows):
    return ref[rows, :].astype(F32)


def _fold8(v):
    return v.reshape(CONV_R // SUBLANES, SUBLANES, v.shape[-1]).sum(0)


def _wgrad_acc(dw_ref, pad_ref, d, K, base):
    off = CONV_PAD - (K - 1)
    for k in range(K):
        dw_ref[k * SUBLANES:(k + 1) * SUBLANES, :] += _fold8(d * pad_ref[pl.ds(base + off + k, CONV_R), :])


def _loop_rows(T, fn):
    def step(r, carry):
        fn(pl.multiple_of(r * CONV_R, CONV_R))
        return carry
    lax.fori_loop(0, T // CONV_R, step, 0)


def _col(T, off_blocks=0, rows=None):
    return pl.BlockSpec((T if rows is None else rows, LANES), lambda j: (0, j + off_blocks))


def _conv_call(body, name, T, n_tiles, in_specs, out_specs, out_shape, n_pad, n_padd=0):
    scratch = [pltpu.VMEM((T + CONV_PAD, LANES), F32)] * (n_pad + n_padd)
    return pl.pallas_call(body, name=name, grid=(n_tiles,), in_specs=in_specs, out_specs=out_specs, out_shape=out_shape,
                          scratch_shapes=scratch, compiler_params=_cparams(("parallel",)))


def _zero_head(ref):
    ref[0:CONV_PAD, :] = jnp.zeros((CONV_PAD, LANES), F32)


def _zero_tail(ref, T):
    ref[T:T + CONV_PAD, :] = jnp.zeros((CONV_PAD, LANES), F32)


def _sds(shape, dtype=F32):
    return jax.ShapeDtypeStruct(shape, dtype)


def _conv_a_fwd(ua, w, b, name):
    T = ua.shape[0]
    K, nt = CONV_A, D // LANES

    def body(al_ref, ag_ref, w_ref, b_ref, o_ref, pad_ref):
        _zero_head(pad_ref)

        def pre(base):
            rows = pl.ds(base, CONV_R)
            pad_ref[pl.ds(base + CONV_PAD, CONV_R), :] = _f32(al_ref, rows) * _sig(_f32(ag_ref, rows))
        _loop_rows(T, pre)

        def main(base):
            o_ref[pl.ds(base, CONV_R), :] = _taps_fwd(pad_ref, w_ref, K, base) + b_ref[...]
        _loop_rows(T, main)

    return _conv_call(body, name, T, nt, [_col(T), _col(T, nt), _col(T, rows=K), _col(T, rows=1)], _col(T),
                      _sds((T, D)), 1)(ua, ua, w, b)


def _conv_a_bwd(ua, w, dac, name):
    T = ua.shape[0]
    K, nt = CONV_A, D // LANES

    def body(al_ref, ag_ref, w_ref, d_ref, dal_ref, dag_ref, dw_ref, db_ref, pad_ref, padd_ref):
        _zero_head(pad_ref)
        _zero_tail(padd_ref, T)
        dw_ref[...] = jnp.zeros_like(dw_ref)
        db_ref[...] = jnp.zeros_like(db_ref)

        def pre(base):
            rows = pl.ds(base, CONV_R)
            pad_ref[pl.ds(base + CONV_PAD, CONV_R), :] = _f32(al_ref, rows) * _sig(_f32(ag_ref, rows))
            padd_ref[rows, :] = d_ref[rows, :]
        _loop_rows(T, pre)

        def main(base):
            rows = pl.ds(base, CONV_R)
            d = d_ref[rows, :]
            _wgrad_acc(dw_ref, pad_ref, d, K, base)
            db_ref[...] += _fold8(d)
            da = _taps_bwd(padd_ref, w_ref, K, base)
            al, s = _f32(al_ref, rows), _sig(_f32(ag_ref, rows))
            dal_ref[rows, :] = (da * s).astype(BF16)
            dag_ref[rows, :] = (da * al * s * (1.0 - s)).astype(BF16)
        _loop_rows(T, main)

    return _conv_call(body, name, T, nt, [_col(T), _col(T, nt), _col(T, rows=K), _col(T)],
                      (_col(T), _col(T), _col(T, rows=K * SUBLANES), _col(T, rows=SUBLANES)),
                      (_sds((T, D), BF16), _sds((T, D), BF16), _sds((K * SUBLANES, D)), _sds((SUBLANES, D))), 1, 1)(ua, ua, w, dac)


def _conv_b_fwd(xu, w, b, name):
    T, C = xu.shape
    K, nt = CONV_B, C // LANES

    def body(x_ref, w_ref, b_ref, o_ref, pad_ref):
        _zero_head(pad_ref)
        pad_ref[CONV_PAD:CONV_PAD + T, :] = x_ref[...].astype(F32)

        def main(base):
            hc = _taps_fwd(pad_ref, w_ref, K, base) + b_ref[...]
            o_ref[pl.ds(base, CONV_R), :] = hc * _sig(hc)
        _loop_rows(T, main)

    return _conv_call(body, name, T, nt, [_col(T), _col(T, rows=K), _col(T, rows=1)], _col(T), _sds((T, C)), 1)(xu, w, b)


def _conv_b_bwd(xu, w, b, dxs, dbs, dcs, name):
    T, C = xu.shape
    K, nt = CONV_B, C // LANES
    nx, nb = dxs.shape[1] // LANES, dbs.shape[1] // LANES

    def body(x_ref, w_ref, b_ref, d1_ref, d2_ref, d3_ref, dx_ref, dw_ref, db_ref, pad_ref, padd_ref):
        j = pl.program_id(0)
        _zero_head(pad_ref)
        _zero_tail(padd_ref, T)
        dw_ref[...] = jnp.zeros_like(dw_ref)
        db_ref[...] = jnp.zeros_like(db_ref)
        pad_ref[CONV_PAD:CONV_PAD + T, :] = x_ref[...].astype(F32)

        def pre(base):
            rows = pl.ds(base, CONV_R)
            hc = _taps_fwd(pad_ref, w_ref, K, base) + b_ref[...]
            s = _sig(hc)
            d = jnp.where(j < nx, d1_ref[rows, :], jnp.where(j < nx + nb, d2_ref[rows, :], d3_ref[rows, :]))
            padd_ref[rows, :] = d * s * (1.0 + hc * (1.0 - s))
        _loop_rows(T, pre)

        def main(base):
            d = padd_ref[pl.ds(base, CONV_R), :]
            _wgrad_acc(dw_ref, pad_ref, d, K, base)
            db_ref[...] += _fold8(d)
            dx_ref[pl.ds(base, CONV_R), :] = _taps_bwd(padd_ref, w_ref, K, base).astype(BF16)
        _loop_rows(T, main)

    def piece(lo, n):
        return pl.BlockSpec((T, LANES), lambda j: (0, jnp.clip(j - lo, 0, n - 1)))

    return _conv_call(body, name, T, nt,
                      [_col(T), _col(T, rows=K), _col(T, rows=1), piece(0, nx), piece(nx, nb), piece(nx + nb, nt - nx - nb)],
                      (_col(T), _col(T, rows=K * SUBLANES), _col(T, rows=SUBLANES)),
                      (_sds((T, C), BF16), _sds((K * SUBLANES, C)), _sds((SUBLANES, C))), 1, 1)(xu, w, b, dxs, dbs, dcs)


def _conv_c_fwd(uo, w, name):
    T = uo.shape[0]
    K, nt = CONV_C, D // LANES

    def body(bg_ref, cg_ref, v_ref, w_ref, o_ref, pad_ref):
        _zero_head(pad_ref)
        pad_ref[CONV_PAD:CONV_PAD + T, :] = cg_ref[...].astype(F32) * v_ref[...].astype(F32)

        def main(base):
            rows = pl.ds(base, CONV_R)
            o_ref[rows, :] = (_f32(bg_ref, rows) * _taps_fwd(pad_ref, w_ref, K, base)).astype(BF16)
        _loop_rows(T, main)

    return _conv_call(body, name, T, nt, [_col(T), _col(T, nt), _col(T, 2 * nt), _col(T, rows=K)], _col(T),
                      _sds((T, D), BF16), 1)(uo, uo, uo, w)


def _conv_c_bwd(uo, w, dsc, name):
    T = uo.shape[0]
    K, nt = CONV_C, D // LANES

    def body(bg_ref, cg_ref, v_ref, w_ref, d_ref, dbg_ref, dcg_ref, dv_ref, dw_ref, pad_ref, padd_ref):
        _zero_head(pad_ref)
        _zero_tail(padd_ref, T)
        dw_ref[...] = jnp.zeros_like(dw_ref)
        pad_ref[CONV_PAD:CONV_PAD + T, :] = cg_ref[...].astype(F32) * v_ref[...].astype(F32)

        def pre(base):
            rows = pl.ds(base, CONV_R)
            d = d_ref[rows, :]
            dbg_ref[rows, :] = (d * _taps_fwd(pad_ref, w_ref, K, base)).astype(BF16)
            padd_ref[rows, :] = d * _f32(bg_ref, rows)
        _loop_rows(T, pre)

        def main(base):
            rows = pl.ds(base, CONV_R)
            _wgrad_acc(dw_ref, pad_ref, padd_ref[rows, :], K, base)
            dq = _taps_bwd(padd_ref, w_ref, K, base)
            dcg_ref[rows, :] = (dq * _f32(v_ref, rows)).astype(BF16)
            dv_ref[rows, :] = (dq * _f32(cg_ref, rows)).astype(BF16)
        _loop_rows(T, main)

    return _conv_call(body, name, T, nt, [_col(T), _col(T, nt), _col(T, 2 * nt), _col(T, rows=K), _col(T)],
                      (_col(T), _col(T), _col(T), _col(T, rows=K * SUBLANES)),
                      (_sds((T, D), BF16), _sds((T, D), BF16), _sds((T, D), BF16), _sds((K * SUBLANES, D))), 1, 1)(uo, uo, uo, w, dsc)


def _conv_f_fwd(up, w, b, name):
    T = up.shape[0]
    K, nt = CONV_F, D_FF // LANES

    def body(u1_ref, u2_ref, w1_ref, w2_ref, b1_ref, b2_ref, o_ref, pad1_ref, pad2_ref):
        _zero_head(pad1_ref)
        _zero_head(pad2_ref)
        pad1_ref[CONV_PAD:CONV_PAD + T, :] = u1_ref[...].astype(F32)
        pad2_ref[CONV_PAD:CONV_PAD + T, :] = u2_ref[...].astype(F32)

        def main(base):
            h1 = _taps_fwd(pad1_ref, w1_ref, K, base) + b1_ref[...]
            h2 = _taps_fwd(pad2_ref, w2_ref, K, base) + b2_ref[...]
            o_ref[pl.ds(base, CONV_R), :] = (h1 * _sig(h1) * h2).astype(BF16)
        _loop_rows(T, main)

    return _conv_call(body, name, T, nt,
                      [_col(T), _col(T, nt), _col(T, rows=K), _col(T, nt, rows=K), _col(T, rows=1), _col(T, nt, rows=1)],
                      _col(T), _sds((T, D_FF), BF16), 2)(up, up, w, w, b, b)


def _conv_f_bwd(up, w, b, dact, name):
    T = up.shape[0]
    K, nt = CONV_F, D_FF // LANES

    def body(u1_ref, u2_ref, w1_ref, w2_ref, b1_ref, b2_ref, d_ref, du1_ref, du2_ref, dw1_ref, dw2_ref, db1_ref, db2_ref,
             pad1_ref, pad2_ref, padd1_ref, padd2_ref):
        _zero_head(pad1_ref)
        _zero_head(pad2_ref)
        _zero_tail(padd1_ref, T)
        _zero_tail(padd2_ref, T)
        for r in (dw1_ref, dw2_ref, db1_ref, db2_ref):
            r[...] = jnp.zeros_like(r)
        pad1_ref[CONV_PAD:CONV_PAD + T, :] = u1_ref[...].astype(F32)
        pad2_ref[CONV_PAD:CONV_PAD + T, :] = u2_ref[...].astype(F32)

        def pre(base):
            rows = pl.ds(base, CONV_R)
            h1 = _taps_fwd(pad1_ref, w1_ref, K, base) + b1_ref[...]
            h2 = _taps_fwd(pad2_ref, w2_ref, K, base) + b2_ref[...]
            s = _sig(h1)
            d = _f32(d_ref, rows)
            padd1_ref[rows, :] = d * h2 * s * (1.0 + h1 * (1.0 - s))
            padd2_ref[rows, :] = d * h1 * s
        _loop_rows(T, pre)

        def main(base):
            rows = pl.ds(base, CONV_R)
            d1, d2 = padd1_ref[rows, :], padd2_ref[rows, :]
            _wgrad_acc(dw1_ref, pad1_ref, d1, K, base)
            _wgrad_acc(dw2_ref, pad2_ref, d2, K, base)
            db1_ref[...] += _fold8(d1)
            db2_ref[...] += _fold8(d2)
            du1_ref[rows, :] = _taps_bwd(padd1_ref, w1_ref, K, base).astype(BF16)
            du2_ref[rows, :] = _taps_bwd(padd2_ref, w2_ref, K, base).astype(BF16)
        _loop_rows(T, main)

    wrow, brow = _col(T, rows=K * SUBLANES), _col(T, rows=SUBLANES)
    return _conv_call(body, name, T, nt,
                      [_col(T), _col(T, nt), _col(T, rows=K), _col(T, nt, rows=K), _col(T, rows=1), _col(T, nt, rows=1), _col(T)],
                      (_col(T), _col(T), wrow, wrow, brow, brow),
                      (_sds((T, D_FF), BF16), _sds((T, D_FF), BF16), _sds((K * SUBLANES, D_FF)), _sds((K * SUBLANES, D_FF)),
                       _sds((SUBLANES, D_FF)), _sds((SUBLANES, D_FF))), 2, 2)(up, up, w, w, b, b, dact)


def _dot(a, b, dims="nn"):
    return lax.dot_general(a.astype(BF16), b.astype(BF16), _DIMS[dims], preferred_element_type=F32)


def _dot_mask(mask, v, mask_left):
    mb = mask.astype(BF16)
    hi = v.astype(BF16)
    r1 = v - hi.astype(F32)
    mid = r1.astype(BF16)
    lo = (r1 - mid.astype(F32)).astype(BF16)
    d = [jnp.dot(mb, t, preferred_element_type=F32) if mask_left else jnp.dot(t, mb, preferred_element_type=F32) for t in (hi, mid, lo)]
    return (d[0] + d[1]) + d[2]


def _ssd_small(xcr_ref, xrr_ref, bc_ref, br_ref, ac_ref, ar_ref):
    Q = SSD_Q
    li = lax.broadcasted_iota(jnp.int32, (Q, Q), 0)
    si = lax.broadcasted_iota(jnp.int32, (Q, Q), 1)
    tril = li >= si
    dtc = jax.nn.softplus(xcr_ref[...] + bc_ref[...])
    dtr = jax.nn.softplus(xrr_ref[...] + br_ref[...])
    cumc = _dot_mask(tril, dtc * ac_ref[...], True)
    cumr = _dot_mask(li <= si, dtr * ar_ref[...], False)
    return tril, dtc, dtr, cumc, cumr


def _ssd_specs(nc, rev):
    Q = SSD_Q
    cc = (lambda c: nc - 1 - c) if rev else (lambda c: c)
    x_spec = pl.BlockSpec((Q, 2 * LANES), lambda g, c: (cc(c), g))
    b_spec = pl.BlockSpec((Q, LANES), lambda g, c: (cc(c), 8 + g))
    c_spec = pl.BlockSpec((Q, LANES), lambda g, c: (cc(c), 12 + g))
    colm = pl.BlockSpec((None, Q, LANES), lambda g, c: (g, cc(c), 0))
    rowm = pl.BlockSpec((None, SUBLANES, Q), lambda g, c: (g, 0, cc(c)))
    colv = pl.BlockSpec((None, 1, LANES), lambda g, c: (g, 0, 0))
    rowv = pl.BlockSpec((None, SUBLANES, 1), lambda g, c: (g, 0, 0))
    st_spec = pl.BlockSpec((None, None, 2 * LANES, N_STATE), lambda g, c: (cc(c), g, 0, 0))
    return x_spec, b_spec, c_spec, colm, rowm, colv, rowv, st_spec


def _ssd_fwd(xc, raw_col, raw_row, bias_col, bias_row, a_col, a_row, dskip, name):
    T = xc.shape[0]
    Q = SSD_Q
    nc = T // Q
    x_spec, b_spec, c_spec, colm, rowm, colv, rowv, st_spec = _ssd_specs(nc, False)

    def body(dk_ref, x_ref, b_ref, c_ref, xcr_ref, xrr_ref, bc_ref, br_ref, ac_ref, ar_ref, y_ref, st_ref, h_ref):
        g = pl.program_id(0)

        @pl.when(pl.program_id(1) == 0)
        def _():
            h_ref[...] = jnp.zeros_like(h_ref)

        tril, dtc, dtr, cumc, cumr = _ssd_small(xcr_ref, xrr_ref, bc_ref, br_ref, ac_ref, ar_ref)
        Bm, Cm = b_ref[...], c_ref[...]
        S = _dot(Cm, Bm, "nt")
        lo = lax.broadcasted_iota(jnp.int32, (Q, LANES), 1) < HEAD_P
        rlo = lax.broadcasted_iota(jnp.int32, (LANES, N_STATE), 0) < HEAD_P
        st_ref[...] = h_ref[...]
        clast = cumc[Q - 1:Q, :]
        for pr in range(2):
            cols = slice(pr * LANES, (pr + 1) * LANES)
            xp = x_ref[:, cols]
            yd = jnp.zeros((Q, LANES), F32)
            for q in range(2):
                hh = 2 * pr + q
                seg = cumc[:, hh:hh + 1] - cumr[hh:hh + 1, :]
                lm = jnp.where(tril, jnp.exp(jnp.where(tril, seg, 0.0)), 0.0)
                w = S * lm * dtr[hh:hh + 1, :]
                xm = jnp.where(lo if q == 0 else jnp.logical_not(lo), xp, 0.0)
                yd = yd + _dot(w, xm)
            h0, h1 = 2 * pr, 2 * pr + 1
            c0, c1 = cumc[:, h0:h0 + 1], cumc[:, h1:h1 + 1]
            e_pair = jnp.where(lo, jnp.exp(c0), jnp.exp(c1))
            hp = h_ref[cols, :]
            ch = _dot(Cm, hp, "nt")
            dsk = jnp.where(lo, dk_ref[4 * g + h0], dk_ref[4 * g + h1])
            y_ref[:, cols] = yd + e_pair * ch + dsk * xp
            cl0, cl1 = clast[:, h0:h0 + 1], clast[:, h1:h1 + 1]
            sdec = jnp.where(lo, jnp.exp(cl0 - c0) * dtc[:, h0:h0 + 1], jnp.exp(cl1 - c1) * dtc[:, h1:h1 + 1])
            decrow = jnp.where(rlo, jnp.exp(cl0), jnp.exp(cl1))
            h_ref[cols, :] = hp * decrow + _dot(xp * sdec, Bm, "tn")

    smem = pl.BlockSpec(memory_space=pltpu.SMEM)
    return pl.pallas_call(
        body, name=name, grid=(N_GROUPS, nc),
        in_specs=[smem, x_spec, b_spec, c_spec, colm, rowm, colv, rowv, colv, rowv],
        out_specs=(x_spec, st_spec),
        out_shape=(_sds((T, D)), _sds((nc, N_GROUPS, 2 * LANES, N_STATE))),
        scratch_shapes=[pltpu.VMEM((2 * LANES, N_STATE), F32)],
        compiler_params=_cparams(("parallel", "arbitrary")))(dskip, xc, xc, xc, raw_col, raw_row, bias_col, bias_row, a_col, a_row)


def _ssd_bwd(xc, raw_col, raw_row, bias_col, bias_row, a_col, a_row, dskip, states, dy, name):
    T = xc.shape[0]
    Q = SSD_Q
    nc = T // Q
    x_spec, b_spec, c_spec, colm, rowm, colv, rowv, st_spec = _ssd_specs(nc, True)
    bo_spec = pl.BlockSpec((Q, LANES), lambda g, c: (nc - 1 - c, g))
    dd_spec = pl.BlockSpec((None, None, SUBLANES, 2 * LANES), lambda g, c: (nc - 1 - c, g, 0, 0))

    def body(dk_ref, x_ref, b_ref, c_ref, xcr_ref, xrr_ref, bc_ref, br_ref, ac_ref, ar_ref, st_ref, dy_ref,
             dx_ref, db_ref, dc_ref, sq_ref, cms_ref, ddac_ref, ddar_ref, dd_ref, dh_ref):
        g = pl.program_id(0)

        @pl.when(pl.program_id(1) == 0)
        def _():
            dh_ref[...] = jnp.zeros_like(dh_ref)

        tril, dtc, dtr, cumc, cumr = _ssd_small(xcr_ref, xrr_ref, bc_ref, br_ref, ac_ref, ar_ref)
        Bm, Cm = b_ref[...], c_ref[...]
        S = _dot(Cm, Bm, "nt")
        lane = lax.broadcasted_iota(jnp.int32, (Q, LANES), 1)
        sub = lax.broadcasted_iota(jnp.int32, (SUBLANES, Q), 0)
        rowi = lax.broadcasted_iota(jnp.int32, (Q, LANES), 0)
        lo = lane < HEAD_P
        rlo = lax.broadcasted_iota(jnp.int32, (LANES, N_STATE), 0) < HEAD_P
        clast = cumc[Q - 1:Q, :]
        ds_g = jnp.zeros((Q, Q), F32)
        dcm = jnp.zeros((Q, N_STATE), F32)
        dbm = jnp.zeros((Q, N_STATE), F32)
        dcum_col = jnp.zeros((Q, LANES), F32)
        dcum_row = jnp.zeros((SUBLANES, Q), F32)
        sq_col = jnp.zeros((Q, LANES), F32)
        cms_row = jnp.zeros((SUBLANES, Q), F32)
        for pr in range(2):
            cols = slice(pr * LANES, (pr + 1) * LANES)
            xp, dyp = x_ref[:, cols], dy_ref[:, cols]
            hin, dhp = st_ref[cols, :], dh_ref[cols, :]
            h0, h1 = 2 * pr, 2 * pr + 1
            c0, c1 = cumc[:, h0:h0 + 1], cumc[:, h1:h1 + 1]
            cl0, cl1 = clast[:, h0:h0 + 1], clast[:, h1:h1 + 1]
            e_pair = jnp.where(lo, jnp.exp(c0), jnp.exp(c1))
            edec = jnp.where(lo, jnp.exp(cl0 - c0), jnp.exp(cl1 - c1))
            dt_pair = jnp.where(lo, dtc[:, h0:h0 + 1], dtc[:, h1:h1 + 1])
            sdec = edec * dt_pair
            ch = _dot(Cm, hin, "nt")
            xb = _dot(Bm, dhp, "nt")
            dye = dyp * e_pair
            t1 = dye * ch
            t2 = xp * xb * edec
            hh_prod = dhp * hin
            dsk = jnp.where(lo, dk_ref[4 * g + h0], dk_ref[4 * g + h1])
            dxp = sdec * xb + dsk * dyp
            for q in range(2):
                hh = 2 * pr + q
                mine = lo if q == 0 else jnp.logical_not(lo)
                seg = cumc[:, hh:hh + 1] - cumr[hh:hh + 1, :]
                lm = jnp.where(tril, jnp.exp(jnp.where(tril, seg, 0.0)), 0.0)
                dtrow = dtr[hh:hh + 1, :]
                w = S * lm * dtrow
                dym = jnp.where(mine, dyp, 0.0)
                gl = _dot(dym, xp, "nt") * lm
                ds_g = ds_g + gl * dtrow
                ms = gl * S
                m = ms * dtrow
                dxp = dxp + _dot(w, dym, "tn")
                cms_row = jnp.where(sub == hh, jnp.sum(ms, axis=0, keepdims=True), cms_row)
                dcum_row = jnp.where(sub == hh, -jnp.sum(m, axis=0, keepdims=True), dcum_row)
                t1h = jnp.sum(jnp.where(mine, t1, 0.0), axis=1, keepdims=True)
                sqh = jnp.sum(jnp.where(mine, t2, 0.0), axis=1, keepdims=True)
                sth = sqh * dtc[:, hh:hh + 1]
                rmine = rlo if q == 0 else jnp.logical_not(rlo)
                hsum = jnp.sum(jnp.sum(jnp.where(rmine, hh_prod, 0.0), axis=1, keepdims=True), axis=0, keepdims=True)
                last = jnp.sum(sth, axis=0, keepdims=True) + jnp.exp(clast[:, hh:hh + 1]) * hsum
                dcol = jnp.sum(m, axis=1, keepdims=True) + t1h - sth
                dcum_col = jnp.where(lane == hh, dcol + jnp.where(rowi == Q - 1, last, 0.0), dcum_col)
                sq_col = jnp.where(lane == hh, sqh, sq_col)
            dcm = dcm + _dot(dye, hin)
            dbm = dbm + _dot(xp * sdec, dhp)
            decrow = jnp.where(rlo, jnp.exp(cl0), jnp.exp(cl1))
            dh_ref[cols, :] = dhp * decrow + _dot(dye, Cm, "tn")
            dx_ref[:, cols] = dxp
            dd_ref[:, cols] = jnp.broadcast_to(jnp.sum(dyp * xp, axis=0, keepdims=True), (SUBLANES, LANES))
        dc_ref[...] = dcm + _dot(ds_g, Bm)
        db_ref[...] = dbm + _dot(ds_g, Cm, "tn")
        li = lax.broadcasted_iota(jnp.int32, (Q, Q), 0)
        si = lax.broadcasted_iota(jnp.int32, (Q, Q), 1)
        ddac_ref[...] = _dot_mask(li <= si, dcum_col, True)
        ddar_ref[...] = _dot_mask(tril, dcum_row, False)
        sq_ref[...] = sq_col
        cms_ref[...] = cms_row

    smem = pl.BlockSpec(memory_space=pltpu.SMEM)
    return pl.pallas_call(
        body, name=name, grid=(N_GROUPS, nc),
        in_specs=[smem, x_spec, b_spec, c_spec, colm, rowm, colv, rowv, colv, rowv, st_spec, x_spec],
        out_specs=(x_spec, bo_spec, bo_spec, colm, rowm, colm, rowm, dd_spec),
        out_shape=(_sds((T, D)), _sds((T, D // 2)), _sds((T, D // 2)), _sds((N_GROUPS, T, LANES)), _sds((N_GROUPS, SUBLANES, T)),
                   _sds((N_GROUPS, T, LANES)), _sds((N_GROUPS, SUBLANES, T)), _sds((nc, N_GROUPS, SUBLANES, 2 * LANES))),
        scratch_shapes=[pltpu.VMEM((2 * LANES, N_STATE), F32)],
        compiler_params=_cparams(("parallel", "arbitrary")))(dskip, xc, xc, xc, raw_col, raw_row, bias_col, bias_row, a_col, a_row,
                                                            states, dy)


def _adam_math(wv, gv, mv, vv):
    c1 = 1.0 - ADAM_B1 ** ADAM_STEP
    c2 = 1.0 - ADAM_B2 ** ADAM_STEP
    mn = ADAM_B1 * mv + (1.0 - ADAM_B1) * gv
    vn = ADAM_B2 * vv + (1.0 - ADAM_B2) * (gv * gv)
    return -ADAM_LR * ((mn / c1) / (jnp.sqrt(vn / c2) + ADAM_EPS) + ADAM_WD * wv), mn, vn


def _adamw_layers(w, g, m, v, l0, Lg, bufs, name):
    L, As, Bs = w.shape
    tr = _tile(As, [], (256, 352, 128))
    has_bufs = bufs is not None

    def body(*refs):
        w_ref, g_ref, m_ref, v_ref = refs[:4]
        d_ref, mo_ref, vo_ref = refs[4 + 3 * has_bufs:]
        d_ref[...], mo_ref[...], vo_ref[...] = _adam_math(w_ref[...], g_ref[...], m_ref[...], v_ref[...])

    spec = pl.BlockSpec((None, tr, Bs), lambda l, i: (l + l0, i, 0))
    args = (w, g, m, v) + (tuple(bufs) if has_bufs else ())
    return pl.pallas_call(
        body, name=name, grid=(Lg, As // tr), in_specs=[spec] * 4 + [_ANY] * (3 * has_bufs), out_specs=(spec,) * 3,
        out_shape=(_sds((L, As, Bs)),) * 3, input_output_aliases={4: 0, 5: 1, 6: 2} if has_bufs else {},
        compiler_params=_cparams(("parallel", "parallel")))(*args)


def _adamw_minor_rows(w, g, m, v, name):
    L, R, C = w.shape
    tr = max(t for t in range(1, C + 1) if C % t == 0 and t * L * R * 4 <= (1 << 20))
    wt, gt, mt, vt = (jnp.transpose(t, (2, 0, 1)) for t in (w, g, m, v))

    def body(w_ref, g_ref, m_ref, v_ref, d_ref, mo_ref, vo_ref):
        d_ref[...], mo_ref[...], vo_ref[...] = _adam_math(w_ref[...], g_ref[...], m_ref[...], v_ref[...])

    spec = pl.BlockSpec((tr, L, R), lambda i: (i, 0, 0))
    out = pl.pallas_call(body, name=name, grid=(C // tr,), in_specs=[spec] * 4, out_specs=(spec,) * 3,
                         out_shape=(_sds((C, L, R)),) * 3, compiler_params=_cparams(("parallel",)))(wt, gt, mt, vt)
    return tuple(jnp.transpose(o, (1, 2, 0)) for o in out) + (jnp.transpose(gt, (1, 2, 0)),)


def _adamw(w, g, m, v, name):
    shape = w.shape
    cols = shape[-1]
    w2, g2, m2, v2 = (t.reshape(-1, cols) for t in (w, g, m, v))
    rows = w2.shape[0]
    tr = 256 if (rows % 256 == 0 and rows > 256) else rows
    c1 = 1.0 - ADAM_B1 ** ADAM_STEP
    c2 = 1.0 - ADAM_B2 ** ADAM_STEP

    def body(w_ref, g_ref, m_ref, v_ref, d_ref, mo_ref, vo_ref):
        gv = g_ref[...]
        mn = ADAM_B1 * m_ref[...] + (1.0 - ADAM_B1) * gv
        vn = ADAM_B2 * v_ref[...] + (1.0 - ADAM_B2) * (gv * gv)
        d_ref[...] = -ADAM_LR * ((mn / c1) / (jnp.sqrt(vn / c2) + ADAM_EPS) + ADAM_WD * w_ref[...])
        mo_ref[...] = mn
        vo_ref[...] = vn

    spec = pl.BlockSpec((tr, cols), lambda i: (i, 0))
    out = pl.pallas_call(body, name=name, grid=(rows // tr,), in_specs=[spec] * 4, out_specs=(spec,) * 3,
                         out_shape=(_sds((rows, cols)),) * 3, compiler_params=_cparams(("parallel",)))(w2, g2, m2, v2)
    return tuple(o.reshape(shape) for o in out)


def _place():
    x, y, c = lax.axis_index("x"), lax.axis_index("y"), lax.axis_index("c")
    chips = [(1 - x, y), (x, 1 - y), (1 - x, 1 - y)]
    return x, y, c, chips


_ANY = pl.BlockSpec(memory_space=pl.ANY)


TENSORS = (("e_w_in", "row", 2, 4096, 1284, 1024), ("e_w_out", "row", 2, 2048, 1024, 512), ("o_w_in", "col", 2, 1024, 3072, 768),
           ("o_w_out", "row", 2, 1024, 1024, 256), ("f_w_up", "col", 4, 1024, 5632, 1408), ("f_w_down", "row", 4, 2816, 1024, 704),
           ("ple_w_proj", "col", 4, 256, 1024, 256), ("ple_w_gate", "row", 4, 1024, 1024, 256))
MIX, FFN = "mix", "ffn"
W_GROUPS = (((0, MIX),), ((0, FFN), (1, MIX)), ((1, FFN), (2, MIX)), ((2, FFN), (3, MIX), (3, FFN)))
G_GROUPS = (((3, FFN), (3, MIX), (2, FFN), (2, MIX), (1, FFN), (1, MIX)), ((0, FFN),), ((0, MIX),))


def _tensor_layer(name, layer):
    if name.startswith("e_"):
        return layer // 2 if layer % 2 == 0 else None
    if name.startswith("o_"):
        return layer // 2 if layer % 2 == 1 else None
    return layer


def _part(name):
    return MIX if name.startswith(("e_", "o_")) else FFN


def _group_items(members):
    items = []
    for name, kind, L, A, B, n in TENSORS:
        tls = sorted(t for t in (_tensor_layer(name, l) for l, part in members if part == _part(name)) if t is not None)
        if tls:
            assert tls == list(range(tls[0], tls[0] + len(tls)))
            items.append((name, kind, len(tls), A, B, n, tls[0]))
    return items


def _hwin(ref, it, k, h):
    name, kind, Lg, A, B, n, l0 = it
    if kind == "row":
        return ref.at[:, pl.ds(pl.multiple_of(k * n + h * (n // 2), 16), n // 2), :]
    return ref.at[:, pl.ds(pl.multiple_of(h * (A // 2), 16), A // 2), pl.ds(pl.multiple_of(k * n, LANES), n)]


def _shard_dims(kind, A, B, n):
    return (n, B) if kind == "row" else (A, n)


def _cast_into(w, it, me):
    name, kind, Lg, A, B, n, l0 = it
    As, Bs = _shard_dims(kind, A, B, n)

    def body(me_ref, w_ref, o_ref):
        o_ref[...] = w_ref[...].astype(BF16)

    omap = (lambda l, m: (l, m[0], 0)) if kind == "row" else (lambda l, m: (l, 0, m[0]))
    grid_spec = pltpu.PrefetchScalarGridSpec(
        num_scalar_prefetch=1, grid=(Lg,), in_specs=[pl.BlockSpec((None, As, Bs), lambda l, m: (l + l0, 0, 0))],
        out_specs=pl.BlockSpec((None, As, Bs), omap))
    return pl.pallas_call(body, name=f"cast_{name}_{l0}", grid_spec=grid_spec, out_shape=_sds((Lg, A, B), BF16),
                          compiler_params=_cparams(("parallel",)))(me, w.reshape(-1, As, Bs))


_HBM = pl.BlockSpec(memory_space=pltpu.HBM)
_SEM = pl.BlockSpec(memory_space=pltpu.SEMAPHORE)
_EFFECT = pltpu.SideEffectType.DATAFLOW_SIDE_EFFECTING


def _hbm(a):
    return pltpu.with_memory_space_constraint(a, pltpu.HBM)


def _split_start(thru, n_copies, issue, name, after=None):
    N = len(thru)
    has_after = after is not None

    def body(*refs):
        outs = refs[N + has_after:2 * N + has_after]
        send_sems, recv_sems, token = refs[2 * N + has_after:]
        for cp in issue(outs, send_sems, recv_sems):
            cp.start()
        token[...] = jnp.zeros_like(token)

    out = pl.pallas_call(
        body, name=name, in_specs=[_HBM] * N + ([_ANY] if has_after else []),
        out_specs=(_HBM,) * N + (_SEM, _SEM, pl.BlockSpec(memory_space=pltpu.VMEM)),
        out_shape=tuple(pltpu.HBM(a.shape, a.dtype) for a in thru)
        + (pltpu.SemaphoreType.DMA((n_copies,)), pltpu.SemaphoreType.DMA((n_copies,)), _sds((SUBLANES, LANES))),
        input_output_aliases={t: t for t in range(N)},
        compiler_params=pltpu.CompilerParams(has_side_effects=_EFFECT))(*[_hbm(a) for a in thru], *([after] if has_after else []))
    return list(out[:N]), out[N], out[N + 1], out[N + 2]


def _split_wait(thru, send_sems, recv_sems, after, waits, name):
    N = len(thru)
    after = list(after) if isinstance(after, (list, tuple)) else [after]

    def body(*refs):
        ins = refs[:N]
        for cp, side in waits(ins, refs[N], refs[N + 1]):
            if side == "send":
                cp.wait_send()
            else:
                cp.wait_recv()

    out = pl.pallas_call(
        body, name=name, in_specs=[_HBM] * N + [_SEM, _SEM] + [_ANY] * len(after), out_specs=(_HBM,) * N,
        out_shape=tuple(pltpu.HBM(a.shape, a.dtype) for a in thru), input_output_aliases={t: t for t in range(N)},
        compiler_params=pltpu.CompilerParams(has_side_effects=_EFFECT))(*thru, send_sems, recv_sems, *after)
    return list(out)


def _rcopy(send_sems, recv_sems, k, src, dst, to):
    return pltpu.make_async_remote_copy(src_ref=src, dst_ref=dst, send_sem=send_sems.at[k], recv_sem=recv_sems.at[k],
                                        device_id=to, device_id_type=MESH)


def _gather_copies(items, refs, send_sems, recv_sems, what):
    x, y, c, chips = _place()
    me = 2 * x + y
    out = []
    for t, it in enumerate(items):
        mine = _hwin(refs[t], it, me, c)
        for j, (px, py) in enumerate(chips):
            if what == "start":
                out.append(_rcopy(send_sems, recv_sems, 3 * t + j, mine, mine, (px, py, c)))
            else:
                slot = _hwin(refs[t], it, 2 * px + py, c)
                out.append((_rcopy(send_sems, recv_sems, 3 * t + j, mine, mine, (px, py, c)), "send"))
                out.append((_rcopy(send_sems, recv_sems, 3 * t + j, slot, slot, (px, py, c)), "recv"))
    return out


def _gather_start(fulls, items, name, after=None):
    return _split_start(fulls, 3 * len(items), functools.partial(_gather_copies, items, what="start"), name, after)


def _gather_wait(fulls, send_sems, recv_sems, after, items, name):
    return _split_wait(fulls, send_sems, recv_sems, after, functools.partial(_gather_copies, items, what="wait"), name)


def _gather_fwd(fulls, items, name, ws=None):
    N = len(fulls)
    has_ws = ws is not None

    def body(*refs):
        outs = refs[N + has_ws:2 * N + has_ws]
        rest = refs[2 * N + has_ws:]
        x, y, c, chips = _place()
        me = 2 * x + y
        sib = (x, y, 1 - c)
        if has_ws:
            ws_ref = refs[N]
            WS_ref, send_sems, recv_sems, lsem = rest
            loc = pltpu.make_async_copy(ws_ref, WS_ref.at[me], lsem)
            loc.start()
        else:
            send_sems, recv_sems = rest
        rc = functools.partial(_rcopy, send_sems, recv_sems)
        cps = []
        for t, it in enumerate(items):
            for j, (px, py) in enumerate(chips):
                slot = _hwin(outs[t], it, 2 * px + py, c)
                cps.append(rc(3 * t + j, slot, slot, sib))
        if has_ws:
            cps += [rc(3 * N + j, ws_ref, WS_ref.at[me], (*chip, c)) for j, chip in enumerate(chips)]
        for cp in cps:
            cp.start()
        for t, it in enumerate(items):
            for j, (px, py) in enumerate(chips):
                oslot = _hwin(outs[t], it, 2 * px + py, 1 - c)
                rc(3 * t + j, oslot, oslot, sib).wait_recv()
        if has_ws:
            for j, (px, py) in enumerate(chips):
                sslot = WS_ref.at[2 * px + py]
                rc(3 * N + j, sslot, sslot, sib).wait_recv()
        for cp in cps:
            cp.wait_send()
        if has_ws:
            loc.wait()

    ns = 3 * N + (3 if has_ws else 0)
    out_shape = tuple(_sds(f.shape, f.dtype) for f in fulls)
    scratch = [pltpu.SemaphoreType.DMA((ns,)), pltpu.SemaphoreType.DMA((ns,))]
    args = list(fulls)
    if has_ws:
        out_shape += (_sds((4,) + ws.shape, ws.dtype),)
        scratch.append(pltpu.SemaphoreType.DMA(()))
        args.append(ws)
    out = pl.pallas_call(
        body, name=name, in_specs=[_ANY] * len(args), out_specs=(_ANY,) * len(out_shape), out_shape=out_shape,
        input_output_aliases={t: t for t in range(N)}, scratch_shapes=scratch,
        compiler_params=pltpu.CompilerParams(has_side_effects=True))(*args)
    return (list(out[:N]), out[N]) if has_ws else (list(out), None)


def _half_shape(it):
    name, kind, Lg, A, B, n, l0 = it
    return (Lg, 4, n // 2, B) if kind == "row" else (Lg, A // 2, B)


def _piece_shape(it):
    name, kind, Lg, A, B, n, l0 = it
    return (Lg, n // 2, B) if kind == "row" else (Lg, A // 2, n)


def _swap_copies(items, refs, send_sems, recv_sems, what):
    N = len(items)
    x, y, c, _ = _place()
    sib = (x, y, 1 - c)
    out = []
    for t, it in enumerate(items):
        name_, kind, Lg, A, B, n, l0 = it
        if kind == "row":
            cps = [_rcopy(send_sems, recv_sems, 4 * t + k, _hwin(refs[t], it, k, 1 - c), refs[N + t].at[:, k], sib) for k in range(4)]
        else:
            src = refs[t].at[:, pl.ds(pl.multiple_of((1 - c) * (A // 2), 16), A // 2), :]
            cps = [_rcopy(send_sems, recv_sems, 4 * t, src, refs[N + t], sib)]
        for cp in cps:
            if what == "start":
                out.append(cp)
            else:
                out += [(cp, "send"), (cp, "recv")]
    return out


def _swap_start(gs, items, name, after=None):
    lands = [lax.empty(_half_shape(it), F32) for it in items]
    return _split_start(list(gs) + lands, 4 * len(items), functools.partial(_swap_copies, items, what="start"), name, after)


def _swap_wait(thru, send_sems, recv_sems, after, items, name):
    return _split_wait(thru, send_sems, recv_sems, after, functools.partial(_swap_copies, items, what="wait"), name)


def _add_half(g, ra, it, cvec):
    name, kind, Lg, A, B, n, l0 = it
    if kind == "row":
        blk = (None, n // 2, B)
        grid = (Lg, 4)
        g_spec = pl.BlockSpec(blk, lambda l, k, cr: (l, 2 * k + cr[0], 0))
        h_spec = pl.BlockSpec((None, None, n // 2, B), lambda l, k, cr: (l, k, 0, 0))
    else:
        tr = _tile(A // 2, [], (256, 128))
        nb = (A // 2) // tr
        grid = (Lg, nb)
        g_spec = pl.BlockSpec((None, tr, B), lambda l, i, cr: (l, cr[0] * nb + i, 0))
        h_spec = pl.BlockSpec((None, tr, B), lambda l, i, cr: (l, i, 0))

    def body(c_ref, g_ref, r_ref, o_ref):
        o_ref[...] = (g_ref[...] + r_ref[...]).astype(BF16)

    grid_spec = pltpu.PrefetchScalarGridSpec(num_scalar_prefetch=1, grid=grid, in_specs=[g_spec, h_spec], out_specs=h_spec)
    return pl.pallas_call(body, name=f"addhalf_{name}_{l0}", grid_spec=grid_spec, out_shape=_sds(_half_shape(it), BF16),
                          compiler_params=_cparams(("parallel", "parallel")))(cvec, g, ra)


def _scatter_copies(items, refs, send_sems, recv_sems, what):
    N = len(items)
    x, y, c, chips = _place()
    out = []
    for t, it in enumerate(items):
        name, kind, Lg, A, B, n, l0 = it
        for j, (px, py) in enumerate(chips):
            k = 2 * px + py
            src = refs[t].at[:, k] if kind == "row" else refs[t].at[:, :, pl.ds(pl.multiple_of(k * n, LANES), n)]
            cp = _rcopy(send_sems, recv_sems, 3 * t + j, src, refs[N + t].at[j], (px, py, c))
            if what == "start":
                out.append(cp)
            else:
                out += [(cp, "send"), (cp, "recv")]
    return out


def _scatter_start(ps, items, name):
    lands = [lax.empty((3,) + _piece_shape(it), BF16) for it in items]
    return _split_start(list(ps) + lands, 3 * len(items), functools.partial(_scatter_copies, items, what="start"), name)


def _scatter_wait(thru, send_sems, recv_sems, after, items, name):
    return _split_wait(thru, send_sems, recv_sems, after, functools.partial(_scatter_copies, items, what="wait"), name)


def _sum_own(p, rc, it, mevec, buf):
    name, kind, Lg, A, B, n, l0 = it
    As, Bs = _shard_dims(kind, A, B, n)
    L = [s[2] for s in TENSORS if s[0] == name][0]
    hb = (As // 2, Bs)
    has_buf = buf is not None

    def body(*refs):
        p_ref, r0, r1, r2 = refs[1:5]
        o_ref = refs[5 + has_buf]
        o_ref[...] = ((p_ref[...].astype(F32) + r0[...].astype(F32)) + r1[...].astype(F32)) + r2[...].astype(F32)

    if kind == "row":
        p_spec = pl.BlockSpec((None, None) + hb, lambda l, m: (l, m[0], 0, 0))
    else:
        p_spec = pl.BlockSpec((None,) + hb, lambda l, m: (l, 0, m[0]))
    r_specs = [pl.BlockSpec((None, None) + hb, functools.partial(lambda l, m, j: (j, l, 0, 0), j=j)) for j in range(3)]
    in_specs = [p_spec] + r_specs + ([_ANY] if has_buf else [])
    grid_spec = pltpu.PrefetchScalarGridSpec(num_scalar_prefetch=1, grid=(Lg,), in_specs=in_specs,
                                             out_specs=pl.BlockSpec((None,) + hb, lambda l, m: (l + l0, m[1], 0)))
    args = (mevec, p, rc, rc, rc) + ((buf,) if has_buf else ())
    return pl.pallas_call(body, name=f"sumown_{name}_{l0}", grid_spec=grid_spec, out_shape=_sds((L, As, Bs)),
                          input_output_aliases={5: 0} if has_buf else {}, compiler_params=_cparams(("parallel",)))(*args)


def _join_halves(rs, items, name):
    N = len(rs)

    def body(*refs):
        outs = refs[N:2 * N]
        send_sems, recv_sems = refs[2 * N:]
        x, y, c, _ = _place()
        sib = (x, y, 1 - c)

        def half(t, h):
            name_, kind, Lg, A, B, n, l0 = items[t]
            hr = _shard_dims(kind, A, B, n)[0] // 2
            return outs[t].at[pl.ds(l0, Lg), pl.ds(pl.multiple_of(h * hr, SUBLANES), hr), :]

        cps = [_rcopy(send_sems, recv_sems, t, half(t, c), half(t, c), sib) for t in range(N)]
        for cp in cps:
            cp.start()
        for t in range(N):
            _rcopy(send_sems, recv_sems, t, half(t, 1 - c), half(t, 1 - c), sib).wait_recv()
        for cp in cps:
            cp.wait_send()

    return list(pl.pallas_call(
        body, name=name, in_specs=[_ANY] * N, out_specs=(_ANY,) * N, out_shape=tuple(_sds(r.shape, r.dtype) for r in rs),
        input_output_aliases={t: t for t in range(N)},
        scratch_shapes=[pltpu.SemaphoreType.DMA((N,)), pltpu.SemaphoreType.DMA((N,))],
        compiler_params=pltpu.CompilerParams(has_side_effects=True))(*rs))


def _allgather_small(v):
    m_per, n = v.shape

    def body(x_ref, out_ref, send_sems, recv_sems, local_sem):
        x, y, c, chips = _place()
        me, sibling = (x, y, c), (x, y, 1 - c)

        def rows(px, py, pc):
            return out_ref.at[pl.ds(pl.multiple_of((4 * px + 2 * py + pc) * m_per, SUBLANES), m_per), :]

        def copy(k, block, to, src=None):
            return pltpu.make_async_remote_copy(src_ref=rows(*block) if src is None else src, dst_ref=rows(*block),
                                                send_sem=send_sems.at[k], recv_sem=recv_sems.at[k], device_id=to, device_id_type=MESH)

        mine = pltpu.make_async_copy(x_ref, rows(*me), local_sem)
        mine.start()
        first = [copy(0, me, sibling, src=x_ref)]
        first += [copy(1 + j, me, (*chip, c), src=x_ref) for j, chip in enumerate(chips)]
        for cp in first:
            cp.start()
        passed = [copy(4 + j, (*chip, c), sibling) for j, chip in enumerate(chips)]
        for j, chip in enumerate(chips):
            copy(1 + j, (*chip, c), me).wait_recv()
            passed[j].start()
        copy(0, sibling, me).wait_recv()
        for j, chip in enumerate(chips):
            copy(4 + j, (*chip, 1 - c), me).wait_recv()
        for cp in first + passed:
            cp.wait_send()
        mine.wait()

    vm = pl.BlockSpec(memory_space=pltpu.VMEM)
    return pl.pallas_call(body, name="allgather_small", in_specs=[vm], out_specs=vm, out_shape=_sds((8 * m_per, n)),
                          scratch_shapes=[pltpu.SemaphoreType.DMA((7,)), pltpu.SemaphoreType.DMA((7,)), pltpu.SemaphoreType.DMA(())],
                          compiler_params=pltpu.CompilerParams(has_side_effects=True, vmem_limit_bytes=VMEM_LIMIT))(v)


def _sum8(v, m_per):
    def body(v_ref, o_ref):
        acc = v_ref[0:m_per, :]
        for k in range(1, 8):
            acc = acc + v_ref[k * m_per:(k + 1) * m_per, :]
        o_ref[...] = acc

    return pl.pallas_call(body, name="small_sum_devices", out_shape=_sds((m_per, v.shape[1])),
                          compiler_params=pltpu.CompilerParams(vmem_limit_bytes=VMEM_LIMIT))(v)


SMALL_SHARDED = (("e_conv_a_w", 2), ("e_conv_b_w", 2), ("o_conv_w", 2), ("f_conv_w", 2), ("ln_g", 2), ("ln_b", 2))
SMALL_REPL = ("e_conv_a_b", "e_ln_a_g", "e_ln_a_b", "e_conv_b_b", "e_dt_bias", "e_a_log", "e_d_skip", "e_norm_b_g", "f_conv_b")

WEIGHT_ORDER = ('e_w_in', 'e_conv_a_w', 'e_conv_a_b', 'e_ln_a_g', 'e_ln_a_b', 'e_conv_b_w', 'e_conv_b_b', 'e_dt_bias', 'e_a_log',
                'e_d_skip', 'e_norm_b_g', 'e_w_out', 'o_w_in', 'o_conv_w', 'o_w_out', 'f_w_up', 'f_conv_w', 'f_conv_b', 'f_w_down',
                'ple_w_proj', 'ple_w_gate', 'ln_g', 'ln_b')


def _pack_rows(parts, width, total_rows, dtype):
    flat = jnp.concatenate([p.reshape(-1).astype(dtype) for p in parts])
    flat = jnp.pad(flat, (0, total_rows * width - flat.shape[0]))
    return flat.reshape(total_rows, width)


def _unpack_rows(buf, shapes):
    flat = buf.reshape(-1)
    out, pos = [], 0
    for s in shapes:
        n = math.prod(s)
        out.append(flat[pos:pos + n].reshape(s))
        pos += n
    return out


def _small_rows(shapes):
    n = sum(math.prod(s) for s in shapes)
    return -(-n // (LANES * SUBLANES)) * SUBLANES


E_PAD = 5248
SEG_A, SEG_Z, SEG_X, SEG_DT = (0, 2 * D), (2 * D, D), (3 * D, 2 * D), (5 * D, LANES)
G_SHAPES = {"e_w_in": (2, D, E_PAD), "e_w_out": (2, 2 * D, D), "o_w_in": (2, D, 3 * D), "o_w_out": (2, D, D),
            "f_w_up": (4, D, 2 * D_FF), "f_w_down": (4, D_FF, D), "ple_w_proj": (4, PLE, D), "ple_w_gate": (4, D, D)}


def _padcols(w, width):
    return jnp.pad(w, ((0, 0), (0, width - w.shape[1])))


def _fold_rows(dw, K):
    return dw.reshape(K, SUBLANES, dw.shape[-1]).sum(1)


class GradBuffers(dict):
    def __init__(self):
        super().__init__()
        self.where = {}
        for gi, layers in enumerate(G_GROUPS):
            for name, kind, Lg, A, B, n, l0 in _group_items(layers):
                for k in range(Lg):
                    self.where[(name, l0 + k)] = (gi, k, Lg)
        self.current = {}

    def into(self, name, layer, r0=0, c0=0):
        gi, k, Lg = self.where[(name, layer)]
        self.current[name] = (name, gi)
        return (self.get((name, gi)), (Lg,) + G_SHAPES[name][1:], (k,), r0, c0)

    def __setitem__(self, name, value):
        super().__setitem__(self.current[name], value)
        self.last = value


def _local_step(x, p, target, W, comm=None, xb=None):
    T = x.shape[0]
    if xb is None:
        xb = _to_bf16(x, "x_bf16")
    saved = []
    xc_f = x
    for i in range(DEPTH):
        j = i // 2
        L = {}
        L["x"], L["xb"] = xc_f, xb
        tok = comm.part_starts(i, MIX, xb) if comm is not None else None
        if i % 2 == 0:
            def w_in(seg, c0=0, cols=None, j=j):
                return V(W["e_w_in"], (j,), c0=seg[0] + c0, cols=seg[1] if cols is None else cols)

            ua = _mm(xb, w_in(SEG_A), "nn", f"l{i}_in_a", BF16, after=tok)
            z = _mm(xb, w_in(SEG_Z), "nn", f"l{i}_in_z")
            xu = _mm(xb, w_in(SEG_X), "nn", f"l{i}_in_xbc", BF16)
            udt = _mm(xb, w_in(SEG_DT), "nn", f"l{i}_in_dt")
            ac = _conv_a_fwd(ua, W["e_conv_a_w"][j], W["e_conv_a_b"][j][None], f"l{i}_conv_a")
            ya = _ln_silu_fwd(ac, W["e_ln_a_g"][j][None], W["e_ln_a_b"][j][None], f"l{i}_ln_a")
            xc = _conv_b_fwd(xu, W["e_conv_b_w"][j], W["e_conv_b_b"][j][None], f"l{i}_conv_b")
            sm = _ssd_small_inputs(udt[:, :N_HEADS], W["e_dt_bias"][j], W["e_a_log"][j])
            y, states = _ssd_fwd(xc, *sm, W["e_d_skip"][j], f"l{i}_ssd")
            yb = _gate_rms_fwd(y, z, W["e_norm_b_g"][j][None], f"l{i}_gate_rms")
            out_pairs = [(ya, V(W["e_w_out"], (j,), rows=D)), (yb, V(W["e_w_out"], (j,), r0=D))]
            L.update(ua=ua, z=z, xu=xu, udt=udt, ac=ac, ya=ya, xc=xc, sm=sm, y=y, states=states, yb=yb, w_in=w_in)
        else:
            uo = _mm(xb, V(W["o_w_in"], (j,)), "nn", f"l{i}_in", BF16, after=tok)
            sc = _conv_c_fwd(uo, W["o_conv_w"][j], f"l{i}_conv_c")
            out_pairs = [(sc, V(W["o_w_out"], (j,)))]
            L.update(uo=uo, sc=sc)
        h1, x1, x1b = _mm_sum(out_pairs, "nn", f"l{i}_out", ln_fwd=(xc_f, None, W["ln_g"][i, 0][None], W["ln_b"][i, 0][None]))
        tok = comm.part_starts(i, FFN, x1b) if comm is not None else None
        up = _mm(x1b, V(W["f_w_up"], (i,)), "nn", f"l{i}_ffn_up", BF16, after=tok)
        act = _conv_f_fwd(up, W["f_conv_w"][i], W["f_conv_b"][i][None], f"l{i}_conv_f")
        pv = V(p, (i, 0))
        pp = _mm(pv, V(W["ple_w_proj"], (i,)), "nn", f"l{i}_ple_proj")
        gl = _mm(x1b, V(W["ple_w_gate"], (i,)), "nn", f"l{i}_ple_gate")
        h2, x2, x2b = _mm_sum([(act, V(W["f_w_down"], (i,)))], "nn", f"l{i}_ffn_down",
                              ln_fwd=(x1, (pp, gl), W["ln_g"][i, 1][None], W["ln_b"][i, 1][None]))
        L.update(h1=h1, x1=x1, x1b=x1b, up=up, act=act, pv=pv, pp=pp, gl=gl, h2=h2)
        saved.append(L)
        xc_f, xb = x2, x2b

    sq, dx = _loss_head(xc_f, target, "loss_head")

    GB = GradBuffers()
    into = GB.into
    tok = None
    ln2_done = None

    G = {n: [None] * (DEPTH if n.startswith(("f_", "ln_")) else DEPTH // 2) for n in WEIGHT_ORDER if n not in G_SHAPES}
    for i in reversed(range(DEPTH)):
        j = i // 2
        L = saved[i]
        if ln2_done is None:
            ln2_done = _res_ln_bwd(dx, L["h2"], W["ln_g"][i, 1][None], (L["pp"], L["gl"]), f"l{i}_ln2_bwd")
        dh2, dh2b, dg2, db2, dpp, dgl = ln2_done
        ln2_done = None
        GB["f_w_down"] = _mm(L["act"], dh2b, "tn", f"l{i}_dw_down", dst=into("f_w_down", i))
        dact = _mm(dh2b, V(W["f_w_down"], (i,)), "nt", f"l{i}_dact", BF16, after=tok)
        du1, du2, dw1, dw2, dbf1, dbf2 = _conv_f_bwd(L["up"], W["f_conv_w"][i], W["f_conv_b"][i][None], dact, f"l{i}_conv_f_bwd")
        G["f_conv_w"][i] = jnp.concatenate([_fold_rows(dw1, CONV_F), _fold_rows(dw2, CONV_F)], axis=1)
        G["f_conv_b"][i] = jnp.concatenate([dbf1.sum(0), dbf2.sum(0)])
        GB["f_w_up"] = _mm(L["x1b"], du1, "tn", f"l{i}_dw_up1", dst=into("f_w_up", i))
        GB["f_w_up"] = _mm(L["x1b"], du2, "tn", f"l{i}_dw_up2", dst=into("f_w_up", i, c0=D_FF))
        GB["ple_w_proj"] = _mm(L["pv"], dpp, "tn", f"l{i}_dw_proj", dst=into("ple_w_proj", i))
        GB["ple_w_gate"] = _mm(L["x1b"], dgl, "tn", f"l{i}_dw_gate", dst=into("ple_w_gate", i))
        tok = comm.part_grads_done(i, FFN, GB) if comm is not None else None
        dh1, dh1b, dg1, db1 = _mm_sum(
            [(du1, V(W["f_w_up"], (i,), cols=D_FF)), (du2, V(W["f_w_up"], (i,), c0=D_FF)), (dgl, V(W["ple_w_gate"], (i,)))],
            "nt", f"l{i}_dx1", add=dh2, add_scale=ALPHA, after=tok, ln_bwd=(L["h1"], W["ln_g"][i, 0][None], None))
        G["ln_g"][i] = jnp.concatenate([dg1, dg2], axis=0)
        G["ln_b"][i] = jnp.concatenate([db1, db2], axis=0)
        if i % 2 == 0:
            GB["e_w_out"] = _mm(L["ya"], dh1b, "tn", f"l{i}_dw_out_a", dst=into("e_w_out", j))
            GB["e_w_out"] = _mm(L["yb"], dh1b, "tn", f"l{i}_dw_out_b", dst=into("e_w_out", j, r0=D))
            dya = _mm(dh1b, V(W["e_w_out"], (j,), rows=D), "nt", f"l{i}_dya")
            dyb = _mm(dh1b, V(W["e_w_out"], (j,), r0=D), "nt", f"l{i}_dyb")
            dac, dga, dba = _ln_silu_bwd(L["ac"], dya, W["e_ln_a_g"][j][None], W["e_ln_a_b"][j][None], f"l{i}_ln_a_bwd")
            G["e_ln_a_g"][j], G["e_ln_a_b"][j] = dga[0], dba[0]
            dal, dag, dwa, dbca = _conv_a_bwd(L["ua"], W["e_conv_a_w"][j], dac, f"l{i}_conv_a_bwd")
            G["e_conv_a_w"][j] = _fold_rows(dwa, CONV_A)
            G["e_conv_a_b"][j] = dbca.sum(0)
            dy, dz, dgn = _gate_rms_bwd(L["y"], L["z"], dyb, W["e_norm_b_g"][j][None], f"l{i}_gate_rms_bwd")
            G["e_norm_b_g"][j] = dgn[0]
            dxs, dbs, dcs, sq_col, cms_row, dda_col, dda_row, ddp = _ssd_bwd(L["xc"], *L["sm"], W["e_d_skip"][j], L["states"], dy,
                                                                             f"l{i}_ssd_bwd")
            draw, G["e_dt_bias"][j], G["e_a_log"][j] = _ssd_small_grads(L["udt"][:, :N_HEADS], W["e_dt_bias"][j], W["e_a_log"][j],
                                                                       sq_col, cms_row, dda_col, dda_row)
            G["e_d_skip"][j] = ddp[:, :, 0, :].sum(0).reshape(N_HEADS, HEAD_P).sum(1)
            dxu, dwb, dbcb = _conv_b_bwd(L["xu"], W["e_conv_b_w"][j], W["e_conv_b_b"][j][None], dxs, dbs, dcs, f"l{i}_conv_b_bwd")
            G["e_conv_b_w"][j] = _fold_rows(dwb, CONV_B)
            G["e_conv_b_b"][j] = dbcb.sum(0)
            dudt = _padcols(draw, LANES)
            w_in = L["w_in"]
            xb_l = L["xb"]
            for nm, dseg, c0 in (("al", dal, 0), ("ag", dag, D), ("z", dz, SEG_Z[0]), ("xbc", dxu, SEG_X[0]), ("dt", dudt, SEG_DT[0])):
                GB["e_w_in"] = _mm(xb_l, dseg, "tn", f"l{i}_dw_in_{nm}", dst=into("e_w_in", j, c0=c0))
            dx = _mm_sum([(dal, w_in(SEG_A, cols=D)), (dag, w_in(SEG_A, c0=D, cols=D)), (dz, w_in(SEG_Z)),
                          (V(dxu, cols=D), w_in(SEG_X, cols=D)), (V(dxu, c0=D), w_in(SEG_X, c0=D, cols=D)), (dudt, w_in(SEG_DT))],
                         "nt", f"l{i}_dx", add=dh1, add_scale=ALPHA)
        else:
            GB["o_w_out"] = _mm(L["sc"], dh1b, "tn", f"l{i}_dw_out", dst=into("o_w_out", j))
            dsc = _mm(dh1b, V(W["o_w_out"], (j,)), "nt", f"l{i}_dsc")
            dbg, dcg, dv, dwc = _conv_c_bwd(L["uo"], W["o_conv_w"][j], dsc, f"l{i}_conv_c_bwd")
            G["o_conv_w"][j] = _fold_rows(dwc, CONV_C)
            xb_l = L["xb"]
            for nm, dseg, c0 in (("bg", dbg, 0), ("cg", dcg, D), ("v", dv, 2 * D)):
                GB["o_w_in"] = _mm(xb_l, dseg, "tn", f"l{i}_dw_in_{nm}", dst=into("o_w_in", j, c0=c0))
            below = saved[i - 1]
            ln2_done = _mm_sum([(dseg, V(W["o_w_in"], (j,), c0=c0, cols=D)) for dseg, c0 in ((dbg, 0), (dcg, D), (dv, 2 * D))],
                               "nt", f"l{i}_dx", add=dh1, add_scale=ALPHA,
                               ln_bwd=(below["h2"], W["ln_g"][i - 1, 1][None], (below["pp"], below["gl"])))
        tok = comm.part_grads_done(i, MIX, GB) if comm is not None else None
    grads = {n: jnp.stack(v) for n, v in G.items()}
    return sq, dx, GB, grads


def _ssd_small_inputs(raw, dt_bias, a_log):
    T = raw.shape[0]
    a = -jnp.exp(a_log)
    rg = raw.reshape(T, N_GROUPS, 4)
    raw_col = jnp.pad(jnp.transpose(rg, (1, 0, 2)), ((0, 0), (0, 0), (0, LANES - 4)))
    raw_row = jnp.pad(jnp.transpose(rg, (1, 2, 0)), ((0, 0), (0, SUBLANES - 4), (0, 0)))

    def colv(v):
        return jnp.pad(v.reshape(N_GROUPS, 1, 4), ((0, 0), (0, 0), (0, LANES - 4)))

    def rowv(v):
        return jnp.pad(v.reshape(N_GROUPS, 4, 1), ((0, 0), (0, SUBLANES - 4), (0, 0)))

    return raw_col, raw_row, colv(dt_bias), rowv(dt_bias), colv(a), rowv(a)


def _ssd_small_grads(raw, dt_bias, a_log, sq_col, cms_row, dda_col, dda_row):
    T = raw.shape[0]

    def join(col, row):
        c = jnp.transpose(col[:, :, :4], (1, 0, 2)).reshape(T, N_HEADS)
        r = jnp.transpose(row[:, :4, :], (2, 0, 1)).reshape(T, N_HEADS)
        return c + r

    a = -jnp.exp(a_log)
    pre = raw + dt_bias
    dt = jax.nn.softplus(pre)
    dda = join(dda_col, dda_row)
    ddt = join(sq_col, cms_row) + a * dda
    draw = ddt * jax.nn.sigmoid(pre)
    da = jnp.sum(dt * dda, axis=0)
    return draw, jnp.sum(draw, axis=0), da * a


def kernel(x, p, e_w_in, e_conv_a_w, e_conv_a_b, e_ln_a_g, e_ln_a_b, e_conv_b_w, e_conv_b_b, e_dt_bias, e_a_log, e_d_skip, e_norm_b_g, e_w_out, o_w_in, o_conv_w, o_w_out, f_w_up, f_conv_w, f_conv_b, f_w_down, ple_w_proj, ple_w_gate, ln_g, ln_b, loss_target, m_e_w_in, m_e_conv_a_w, m_e_conv_a_b, m_e_ln_a_g, m_e_ln_a_b, m_e_conv_b_w, m_e_conv_b_b, m_e_dt_bias, m_e_a_log, m_e_d_skip, m_e_norm_b_g, m_e_w_out, m_o_w_in, m_o_conv_w, m_o_w_out, m_f_w_up, m_f_conv_w, m_f_conv_b, m_f_w_down, m_ple_w_proj, m_ple_w_gate, m_ln_g, m_ln_b, v_e_w_in, v_e_conv_a_w, v_e_conv_a_b, v_e_ln_a_g, v_e_ln_a_b, v_e_conv_b_w, v_e_conv_b_b, v_e_dt_bias, v_e_a_log, v_e_d_skip, v_e_norm_b_g, v_e_w_out, v_o_w_in, v_o_conv_w, v_o_w_out, v_f_w_up, v_f_conv_w, v_f_conv_b, v_f_w_down, v_ple_w_proj, v_ple_w_gate, v_ln_g, v_ln_b):
    args = dict(locals())
    w_shard = {n: args[n] for n in WEIGHT_ORDER}
    m_shard = {n: args["m_" + n] for n in WEIGHT_ORDER}
    v_shard = {n: args["v_" + n] for n in WEIGHT_ORDER}
    xi, yi, ci = lax.axis_index("x"), lax.axis_index("y"), lax.axis_index("c")
    chip = 2 * xi + yi

    mevec = jnp.stack([chip, ci]).astype(jnp.int32)
    small_shapes = [w_shard[n].shape for n, _ in SMALL_SHARDED]
    sr = _small_rows(small_shapes)
    ws = _pack_rows([w_shard[n] for n, _ in SMALL_SHARDED], LANES, sr, F32)
    W = {n: w_shard[n] for n in SMALL_REPL}
    W.update({s[0]: Layers(s[2]) for s in TENSORS})
    w_items = [_group_items(layers) for layers in W_GROUPS]
    g_items = [_group_items(layers) for layers in G_GROUPS]

    def install(items, fulls):
        for it, f in zip(items, fulls):
            if it[0] == "e_w_in":
                f = jnp.transpose(f.reshape(it[2], 4, D, E_IN // 4), (0, 2, 1, 3)).reshape(it[2], D, E_IN)
                f = jnp.pad(f, ((0, 0), (0, 0), (0, E_PAD - E_IN)))
            W[it[0]].put(f, it[6])

    casts = [[_cast_into(w_shard[it[0]], it, mevec[:1]) for it in items] for items in w_items]
    fulls, ssem, rsem, _ = _gather_start(casts[0], w_items[0], "gather_start_0")
    xb0 = _to_bf16(x[0], "x_bf16")
    fulls = _gather_wait(fulls, ssem, rsem, [c for grp in casts[1:] for c in grp] + [xb0], w_items[0], "gather_wait_0")
    fulls, WS = _gather_fwd(fulls, w_items[0], "gather_fwd_0", ws)
    install(w_items[0], fulls)
    parts_s = [_unpack_rows(WS[k], small_shapes) for k in range(4)]
    for idx, (n, ax) in enumerate(SMALL_SHARDED):
        W[n] = jnp.concatenate([parts_s[k][idx] for k in range(4)], axis=ax)

    class Comm:
        sent = {}
        started = {}
        tail = fulls[0]

        def start_next(self, gi):
            if gi >= len(w_items):
                return None
            self.started[gi] = _gather_start(casts[gi], w_items[gi], f"gather_start_{gi}", self.tail)
            return self.started[gi][3]

        def part_starts(self, layer, part, after):
            if (layer, part) == W_GROUPS[0][0]:
                return self.start_next(1)
            for gi in range(1, len(W_GROUPS)):
                if W_GROUPS[gi][0] == (layer, part):
                    fulls, ssem, rsem, _ = self.started[gi]
                    fulls = _gather_wait(fulls, ssem, rsem, after, w_items[gi], f"gather_wait_{gi}")
                    fulls, _ = _gather_fwd(fulls, w_items[gi], f"gather_fwd_{gi}")
                    install(w_items[gi], fulls)
                    self.tail = fulls[0]
                    return self.start_next(gi + 1)
            return None

        swapping = None

        def swap_landed(self, after):
            if self.swapping is None:
                return None
            gi, thru, ssem, rsem = self.swapping
            items = g_items[gi]
            thru = _swap_wait(thru, ssem, rsem, after, items, f"swap_wait_{gi}")
            gs, ras = thru[:len(items)], thru[len(items):]
            ps = [_add_half(g, ra, it, mevec[1:]) for g, ra, it in zip(gs, ras, items)]
            thru, ssem, rsem, tok = _scatter_start(ps, items, f"scatter_start_{gi}")
            self.sent[gi] = (thru, ssem, rsem, tok)
            self.swapping = None
            return tok

        def part_grads_done(self, layer, part, GB):
            tok = self.swap_landed(GB.last)
            for gi, members in enumerate(G_GROUPS):
                if members[-1] == (layer, part):
                    items = g_items[gi]
                    gs = []
                    for it in items:
                        g = GB[(it[0], gi)]
                        if it[0] == "e_w_in":
                            g = jnp.transpose(g[:, :, :E_IN].reshape(it[2], D, 4, E_IN // 4), (0, 2, 1, 3)).reshape(it[2], 4 * D, E_IN // 4)
                        gs.append(g)
                    thru, ssem, rsem, tok = _swap_start(gs, items, f"swap_start_{gi}", tok)
                    self.swapping = (gi, thru, ssem, rsem)
            return tok

    comm = Comm()

    sq, dx, GB, G = _local_step(x[0], p, loss_target[0], W, comm, xb0)
    loss = lax.psum(0.5 * sq[0, 0] / D, ("x", "y", "c"))
    grad_x = dx[None]

    def shard_of(g, ax, k):
        n = g.shape[ax] // 4
        return lax.slice_in_dim(g, k * n, (k + 1) * n, axis=ax)

    reduced, updated = {}, {}
    comm.swap_landed(dx)
    after = comm.sent[len(g_items) - 1][3]
    for gi, items in enumerate(g_items):
        thru, ssem, rsem, _ = comm.sent[gi]
        thru = _scatter_wait(thru, ssem, rsem, after, items, f"scatter_wait_{gi}")
        ps, rcs = thru[:len(items)], thru[len(items):]
        rs = [_sum_own(pt, rc, it, mevec, reduced.get(it[0])) for pt, rc, it in zip(ps, rcs, items)]
        rs = _join_halves(rs, items, f"join_halves_{gi}")
        reduced.update({it[0]: r for it, r in zip(items, rs)})
        for it in items:
            n = it[0]
            if n != "e_w_in":
                updated[n] = _adamw_layers(w_shard[n], reduced[n], m_shard[n], v_shard[n], it[6], it[2], updated.get(n),
                                           f"adamw_{n}_{it[6]}")
        after = updated[items[-1][0]][0]
    *updated["e_w_in"], reduced["e_w_in"] = _adamw_minor_rows(w_shard["e_w_in"], reduced["e_w_in"], m_shard["e_w_in"],
                                                              v_shard["e_w_in"], "adamw_e_w_in")

    small_all = ([shard_of(G[n], ax, k) for k in range(4) for n, ax in SMALL_SHARDED] + [G[n] for n in SMALL_REPL])
    small_all_shapes = [t.shape for t in small_all]
    mr = _small_rows(small_all_shapes)
    sg = _sum8(_allgather_small(_pack_rows(small_all, LANES, mr, F32)), mr)
    sparts = _unpack_rows(sg, small_all_shapes)
    ns = len(SMALL_SHARDED)
    gsmall = {}
    for idx, (n, ax) in enumerate(SMALL_SHARDED):
        stacked = jnp.stack([sparts[k * ns + idx] for k in range(4)])
        gsmall[n] = lax.dynamic_index_in_dim(stacked, chip, axis=0, keepdims=False)
    for idx, n in enumerate(SMALL_REPL):
        gsmall[n] = sparts[4 * ns + idx]

    grads, deltas, new_m, new_v = [], [], [], []
    for n in WEIGHT_ORDER:
        if n in reduced:
            g, (d, mn, vn) = reduced[n], updated[n]
        else:
            g = gsmall[n]
            d, mn, vn = _adamw(w_shard[n], g, m_shard[n], v_shard[n], f"adamw_{n}")
        grads.append(g)
        deltas.append(d)
        new_m.append(mn)
        new_v.append(vn)
    return (loss, grad_x, *grads, *deltas, *new_m, *new_v)
```

```python
import functools
import math

import jax
import jax.numpy as jnp
from jax import lax
from jax.experimental import pallas as pl
from jax.experimental.pallas import tpu as pltpu

F32 = jnp.float32
BF16 = jnp.bfloat16
MESH = pl.DeviceIdType.MESH

DEPTH = 4
ALPHA = (2.0 * DEPTH) ** 0.25
LN_EPS = 1e-5
D = 1024
HEAD_P = 64
N_STATE = 128
N_HEADS = 16
N_GROUPS = 4
CONV_A, CONV_B, CONV_C, CONV_F = 31, 4, 3, 3
D_FF = 2816
PLE = 256
E_IN = 5136

ADAM_LR, ADAM_B1, ADAM_B2, ADAM_EPS, ADAM_WD, ADAM_STEP = 0.001, 0.9, 0.999, 1e-08, 0.01, 10

LANES = 128
SUBLANES = 8
VMEM_LIMIT = 56 * 1024 * 1024
SSD_Q = 128
CONV_R = 128
CONV_PAD = 32
ROW_T = 512


def _cparams(sem=None):
    return pltpu.CompilerParams(dimension_semantics=sem, vmem_limit_bytes=VMEM_LIMIT)


def _sig(v):
    return jax.nn.sigmoid(v)


_DIMS = {"nn": (((1,), (0,)), ((), ())), "nt": (((1,), (1,)), ((), ())), "tn": (((0,), (0,)), ((), ()))}


class Layers:
    def __init__(self, n_layers):
        self.where = [None] * n_layers

    def put(self, arr, l0):
        for k in range(arr.shape[0]):
            self.where[l0 + k] = (arr, k)


class V:
    def __init__(self, arr, lead=(), r0=0, c0=0, rows=None, cols=None):
        if isinstance(arr, Layers):
            arr, k = arr.where[lead[0]]
            lead = (k,) + tuple(lead[1:])
        self.arr, self.lead, self.r0, self.c0 = arr, tuple(lead), r0, c0
        R, C = arr.shape[-2:]
        self.rows = R - r0 if rows is None else rows
        self.cols = C - c0 if cols is None else cols

    def spec(self, br, bc, fn):
        assert self.r0 % br == 0 and self.c0 % bc == 0, (self.r0, self.c0, br, bc)
        ro, co, lead = self.r0 // br, self.c0 // bc, self.lead

        def index(i, j, k):
            r, c = fn(i, j, k)
            return lead + (r + ro, c + co)

        return pl.BlockSpec((None,) * len(lead) + (br, bc), index)


def _v(t):
    return t if isinstance(t, V) else V(t)


def _tile(n, offs, cands):
    for c in cands:
        if n % c == 0 and all(o % c == 0 for o in offs):
            return c
    raise ValueError((n, offs))


_TILES = (1024, 1408, 512, 256, 128)


def _mm(a, b, mode, name, out_dtype=F32, add=None, add_scale=1.0, dst=None, after=None):
    a, b = _v(a), _v(b)
    add = _v(add) if add is not None else None
    if mode == "nn":
        M, K, K2, N = a.rows, a.cols, b.rows, b.cols
        am, ak, bk, bn = a.r0, a.c0, b.r0, b.c0
    elif mode == "nt":
        M, K, N, K2 = a.rows, a.cols, b.rows, b.cols
        am, ak, bn, bk = a.r0, a.c0, b.r0, b.c0
    else:
        K, M, K2, N = a.rows, a.cols, b.rows, b.cols
        ak, am, bk, bn = a.r0, a.c0, b.r0, b.c0
    assert K == K2, (name, mode, M, K, K2, N)
    if dst is None:
        buf, full_shape, o_lead, o_r0, o_c0 = None, (M, N), (), 0, 0
    else:
        buf, full_shape, o_lead, o_r0, o_c0 = dst
    tm = _tile(M, [am, o_r0] + ([add.r0] if add else []), ((2048,) if mode != "tn" and K <= 1408 else ()) + _TILES)
    tn = _tile(N, [bn, o_c0] + ([add.c0] if add else []), _TILES)
    narrow = a.arr.dtype.itemsize == 2 and b.arr.dtype.itemsize == 2
    tk = _tile(K, [ak, bk], ((2048,) if narrow else ()) + _TILES)
    nk = K // tk
    has_add, has_buf, has_after = add is not None, buf is not None, after is not None

    def body(*refs):
        a_ref, b_ref = refs[0], refs[1]
        add_ref = refs[2] if has_add else None
        o_ref = refs[2 + has_add + has_buf + has_after]

        def finish(r):
            if has_add:
                r = r + add_scale * add_ref[...].astype(F32)
            o_ref[...] = r.astype(o_ref.dtype)

        part = lax.dot_general(a_ref[...].astype(BF16), b_ref[...].astype(BF16), _DIMS[mode], preferred_element_type=F32)
        if nk == 1:
            finish(part)
        else:
            acc_ref = refs[-1]
            k = pl.program_id(2)

            @pl.when(k == 0)
            def _():
                acc_ref[...] = part

            @pl.when(jnp.logical_and(k > 0, k < nk - 1))
            def _():
                acc_ref[...] += part

            @pl.when(k == nk - 1)
            def _():
                finish(acc_ref[...] + part)

    if mode == "tn":
        a_spec = a.spec(tk, tm, lambda i, j, k: (k, i))
    else:
        a_spec = a.spec(tm, tk, lambda i, j, k: (i, k))
    if mode == "nt":
        b_spec = b.spec(tn, tk, lambda i, j, k: (j, k))
    else:
        b_spec = b.spec(tk, tn, lambda i, j, k: (k, j))
    in_specs, args = [a_spec, b_spec], [a.arr, b.arr]
    if has_add:
        in_specs.append(add.spec(tm, tn, lambda i, j, k: (i, j)))
        args.append(add.arr)
    aliases = {}
    if has_buf:
        aliases = {len(args): 0}
        in_specs.append(pl.BlockSpec(memory_space=pl.ANY))
        args.append(buf)
        out_dtype = buf.dtype
    if has_after:
        in_specs.append(pl.BlockSpec(memory_space=pl.ANY))
        args.append(after)
    o_view = V(jax.ShapeDtypeStruct(full_shape, out_dtype), o_lead, o_r0, o_c0, M, N)
    return pl.pallas_call(
        body, name=name, grid=(M // tm, N // tn, nk), in_specs=in_specs, out_specs=o_view.spec(tm, tn, lambda i, j, k: (i, j)),
        out_shape=jax.ShapeDtypeStruct(full_shape, out_dtype), input_output_aliases=aliases,
        scratch_shapes=[pltpu.VMEM((tm, tn), F32)] if nk > 1 else [],
        compiler_params=_cparams(("parallel", "parallel", "arbitrary")))(*args)


def _ln_stats(h):
    mu = jnp.mean(h, axis=-1, keepdims=True)
    hc = h - mu
    var = jnp.mean(hc * hc, axis=-1, keepdims=True)
    rstd = lax.rsqrt(var + LN_EPS)
    return hc * rstd, rstd


def _ln_bwd_math(dyv, h, g):
    xhat, rstd = _ln_stats(h)
    dxh = dyv * g
    dh = rstd * (dxh - jnp.mean(dxh, axis=-1, keepdims=True) - xhat * jnp.mean(dxh * xhat, axis=-1, keepdims=True))
    return dh, jnp.sum(dyv * xhat, axis=0, keepdims=True), jnp.sum(dyv, axis=0, keepdims=True)


def _mm_sum(pairs, mode, name, out_dtype=F32, add=None, add_scale=1.0, after=None, ln_fwd=None, ln_bwd=None):
    pairs = [(_v(a), _v(b)) for a, b in pairs]
    add = _v(add) if add is not None else None
    M = pairs[0][0].rows
    N = pairs[0][1].cols if mode == "nn" else pairs[0][1].rows
    b_offs = [(b.c0 if mode == "nn" else b.r0) for _, b in pairs]
    fused = ln_fwd is not None or ln_bwd is not None
    tn = N if fused else _tile(N, b_offs + ([add.c0] if add else []), (512, 256, 128))
    ple_n = 2 if fused and (ln_fwd[1] if ln_fwd is not None else ln_bwd[2]) is not None else 0

    def footprint(tm):
        ab = sum(tm * a.cols * a.arr.dtype.itemsize + a.cols * tn * b.arr.dtype.itemsize for a, b in pairs)
        io = tm * tn * 4 * (add is not None)
        if ln_fwd is not None:
            io += tm * tn * (4 * (1 + ple_n) + 10)
        elif ln_bwd is not None:
            io += tm * tn * (4 * (1 + ple_n) + 6 + 2 * ple_n)
        else:
            io += tm * tn * 4
        return 2 * (ab + io)

    offs = [a.r0 for a, _ in pairs] + ([add.r0] if add else [])
    fits = [t for t in (512, 256, 128) if M % t == 0 and all(o % t == 0 for o in offs) and footprint(t) <= (VMEM_LIMIT * 3) // 4]
    tm = fits[0] if fits else _tile(M, offs, (128,))
    assert not fused or (N == D and all(o == 0 for o in b_offs))
    n_p, has_add, has_after = len(pairs), add is not None, after is not None
    ple = (ln_fwd[1] if ln_fwd is not None else ln_bwd[2]) if fused else None
    has_ple = ple is not None

    def body(*refs):
        acc = None
        for i in range(n_p):
            part = lax.dot_general(refs[2 * i][...].astype(BF16), refs[2 * i + 1][...].astype(BF16), _DIMS[mode],
                                   preferred_element_type=F32)
            acc = part if acc is None else acc + part
        pos = 2 * n_p
        if has_add:
            acc = acc + add_scale * refs[pos][...].astype(F32)
            pos += 1
        if ln_fwd is not None:
            x_ref = refs[pos]
            pp_ref, gl_ref = (refs[pos + 1], refs[pos + 2]) if has_ple else (None, None)
            pos += 1 + 2 * has_ple
            g_ref, b_ref = refs[pos], refs[pos + 1]
            h_ref, y_ref, yb_ref = refs[pos + 2 + has_after:]
            h = ALPHA * x_ref[...] + acc
            if has_ple:
                h = h + pp_ref[...] * _sig(gl_ref[...])
            xhat, _ = _ln_stats(h)
            y = xhat * g_ref[...] + b_ref[...]
            h_ref[...] = h
            y_ref[...] = y
            yb_ref[...] = y.astype(BF16)
        elif ln_bwd is not None:
            h_ref, g_ref = refs[pos], refs[pos + 1]
            pp_ref, gl_ref = (refs[pos + 2], refs[pos + 3]) if has_ple else (None, None)
            outs = refs[pos + 2 + 2 * has_ple + has_after:]
            dh_ref, dhb_ref, dg_ref, db_ref = outs[:4]

            @pl.when(pl.program_id(0) == 0)
            def _():
                dg_ref[...] = jnp.zeros_like(dg_ref)
                db_ref[...] = jnp.zeros_like(db_ref)

            dh, dg, db = _ln_bwd_math(acc, h_ref[...], g_ref[...])
            dg_ref[...] += dg
            db_ref[...] += db
            dh_ref[...] = dh
            dhb_ref[...] = dh.astype(BF16)
            if has_ple:
                s = _sig(gl_ref[...])
                outs[4][...] = (dh * s).astype(BF16)
                outs[5][...] = (dh * pp_ref[...] * s * (1.0 - s)).astype(BF16)
        else:
            o_ref = refs[pos + has_after]
            o_ref[...] = acc.astype(o_ref.dtype)

    in_specs, args = [], []
    for a, b in pairs:
        K = a.cols
        assert K == (b.rows if mode == "nn" else b.cols), (name, K)
        in_specs.append(a.spec(tm, K, lambda i, j, k: (i, 0)))
        in_specs.append(b.spec(K, tn, lambda i, j, k: (0, j)) if mode == "nn" else b.spec(tn, K, lambda i, j, k: (j, 0)))
        args += [a.arr, b.arr]
    if has_add:
        in_specs.append(add.spec(tm, tn, lambda i, j, k: (i, j)))
        args.append(add.arr)
    row = pl.BlockSpec((tm, tn), lambda i, j, k: (i, j))
    vec = pl.BlockSpec((1, tn), lambda i, j, k: (0, 0))
    if ln_fwd is not None:
        x, _, g, b = ln_fwd
        extra = [x] + (list(ple) if has_ple else []) + [g, b]
        in_specs += [row] * (1 + 2 * has_ple) + [vec, vec]
        args += extra
        out_specs = (row, row, row)
        out_shape = (_sds((M, N)), _sds((M, N)), _sds((M, N), BF16))
    elif ln_bwd is not None:
        h, g, _ = ln_bwd
        in_specs += [row, vec] + [row] * (2 * has_ple)
        args += [h, g] + (list(ple) if has_ple else [])
        out_specs = (row, row, vec, vec) + ((row, row) if has_ple else ())
        out_shape = (_sds((M, N)), _sds((M, N), BF16), _sds((1, N)), _sds((1, N))) + ((_sds((M, N), BF16),) * 2 if has_ple else ())
    else:
        out_specs, out_shape = row, jax.ShapeDtypeStruct((M, N), out_dtype)
    if has_after:
        in_specs.append(pl.BlockSpec(memory_space=pl.ANY))
        args.append(after)
    return pl.pallas_call(
        body, name=name, grid=(M // tm, N // tn, 1), in_specs=in_specs, out_specs=out_specs, out_shape=out_shape,
        compiler_params=_cparams(("arbitrary",) * 3 if ln_bwd is not None else ("parallel", "parallel", "arbitrary")))(*args)


def _rows(T, width=D):
    return pl.BlockSpec((ROW_T, width), lambda i: (i, 0))


def _vec(width=D):
    return pl.BlockSpec((1, width), lambda i: (0, 0))


def _res_ln_fwd(x, adds, ple, g, b, name):
    T = x.shape[0]
    n_add = len(adds)
    has_ple = ple is not None

    def body(*refs):
        x_ref = refs[0]
        add_refs = refs[1:1 + n_add]
        pos = 1 + n_add
        if has_ple:
            pp_ref, gl_ref = refs[pos], refs[pos + 1]
            pos += 2
        g_ref, b_ref, h_ref, y_ref, yb_ref = refs[pos:pos + 5]
        h = ALPHA * x_ref[...]
        for r in add_refs:
            h = h + r[...]
        if has_ple:
            h = h + pp_ref[...] * _sig(gl_ref[...])
        xhat, _ = _ln_stats(h)
        y = xhat * g_ref[...] + b_ref[...]
        h_ref[...] = h
        y_ref[...] = y
        yb_ref[...] = y.astype(BF16)

    n_in = 1 + n_add + (2 if has_ple else 0)
    args = (x,) + tuple(adds) + (tuple(ple) if has_ple else ()) + (g, b)
    return pl.pallas_call(
        body, name=name, grid=(T // ROW_T,), in_specs=[_rows(T)] * n_in + [_vec(), _vec()],
        out_specs=(_rows(T), _rows(T), _rows(T)),
        out_shape=(jax.ShapeDtypeStruct((T, D), F32), jax.ShapeDtypeStruct((T, D), F32), jax.ShapeDtypeStruct((T, D), BF16)),
        compiler_params=_cparams(("parallel",)))(*args)


def _res_ln_bwd(dy, h, g, ple, name):
    T = dy.shape[0]
    has_ple = ple is not None

    def body(*refs):
        if has_ple:
            dy_ref, h_ref, g_ref, pp_ref, gl_ref, dh_ref, dhb_ref, dg_ref, db_ref, dpp_ref, dgl_ref = refs
        else:
            dy_ref, h_ref, g_ref, dh_ref, dhb_ref, dg_ref, db_ref = refs
        i = pl.program_id(0)

        @pl.when(i == 0)
        def _():
            dg_ref[...] = jnp.zeros_like(dg_ref)
            db_ref[...] = jnp.zeros_like(db_ref)

        dyv = dy_ref[...]
        xhat, rstd = _ln_stats(h_ref[...])
        dg_ref[...] += jnp.sum(dyv * xhat, axis=0, keepdims=True)
        db_ref[...] += jnp.sum(dyv, axis=0, keepdims=True)
        dxh = dyv * g_ref[...]
        dh = rstd * (dxh - jnp.mean(dxh, axis=-1, keepdims=True) - xhat * jnp.mean(dxh * xhat, axis=-1, keepdims=True))
        dh_ref[...] = dh
        dhb_ref[...] = dh.astype(BF16)
        if has_ple:
            s = _sig(gl_ref[...])
            dpp_ref[...] = (dh * s).astype(BF16)
            dgl_ref[...] = (dh * pp_ref[...] * s * (1.0 - s)).astype(BF16)

    args = (dy, h, g) + (tuple(ple) if has_ple else ())
    in_specs = [_rows(T), _rows(T), _vec()] + ([_rows(T), _rows(T)] if has_ple else [])
    out_specs = [_rows(T), _rows(T), _vec(), _vec()] + ([_rows(T), _rows(T)] if has_ple else [])
    out_shape = [jax.ShapeDtypeStruct((T, D), F32), jax.ShapeDtypeStruct((T, D), BF16),
                 jax.ShapeDtypeStruct((1, D), F32), jax.ShapeDtypeStruct((1, D), F32)]
    if has_ple:
        out_shape += [jax.ShapeDtypeStruct((T, D), BF16), jax.ShapeDtypeStruct((T, D), BF16)]
    return pl.pallas_call(
        body, name=name, grid=(T // ROW_T,), in_specs=in_specs, out_specs=tuple(out_specs), out_shape=tuple(out_shape),
        compiler_params=_cparams(("arbitrary",)))(*args)


def _ln_silu_fwd(ac, g, b, name):
    T = ac.shape[0]

    def body(a_ref, g_ref, b_ref, o_ref):
        xhat, _ = _ln_stats(a_ref[...])
        ln = xhat * g_ref[...] + b_ref[...]
        o_ref[...] = (ln * _sig(ln)).astype(BF16)

    return pl.pallas_call(
        body, name=name, grid=(T // ROW_T,), in_specs=[_rows(T), _vec(), _vec()], out_specs=_rows(T),
        out_shape=jax.ShapeDtypeStruct((T, D), BF16), compiler_params=_cparams(("parallel",)))(ac, g, b)


def _ln_silu_bwd(ac, dya, g, b, name):
    T = ac.shape[0]

    def body(a_ref, d_ref, g_ref, b_ref, da_ref, dg_ref, db_ref):
        i = pl.program_id(0)

        @pl.when(i == 0)
        def _():
            dg_ref[...] = jnp.zeros_like(dg_ref)
            db_ref[...] = jnp.zeros_like(db_ref)

        xhat, rstd = _ln_stats(a_ref[...])
        ln = xhat * g_ref[...] + b_ref[...]
        s = _sig(ln)
        dln = d_ref[...] * s * (1.0 + ln * (1.0 - s))
        dg_ref[...] += jnp.sum(dln * xhat, axis=0, keepdims=True)
        db_ref[...] += jnp.sum(dln, axis=0, keepdims=True)
        dxh = dln * g_ref[...]
        da_ref[...] = rstd * (dxh - jnp.mean(dxh, axis=-1, keepdims=True)
                              - xhat * jnp.mean(dxh * xhat, axis=-1, keepdims=True))

    return pl.pallas_call(
        body, name=name, grid=(T // ROW_T,), in_specs=[_rows(T), _rows(T), _vec(), _vec()],
        out_specs=(_rows(T), _vec(), _vec()),
        out_shape=(jax.ShapeDtypeStruct((T, D), F32), jax.ShapeDtypeStruct((1, D), F32), jax.ShapeDtypeStruct((1, D), F32)),
        compiler_params=_cparams(("arbitrary",)))(ac, dya, g, b)


def _gate_rms_fwd(y, z, g, name):
    T = y.shape[0]

    def body(y_ref, z_ref, g_ref, o_ref):
        zv = z_ref[...]
        yg = y_ref[...] * (zv * _sig(zv))
        r = lax.rsqrt(jnp.mean(yg * yg, axis=-1, keepdims=True) + LN_EPS)
        o_ref[...] = (yg * r * g_ref[...]).astype(BF16)

    return pl.pallas_call(
        body, name=name, grid=(T // ROW_T,), in_specs=[_rows(T), _rows(T), _vec()], out_specs=_rows(T),
        out_shape=jax.ShapeDtypeStruct((T, D), BF16), compiler_params=_cparams(("parallel",)))(y, z, g)


def _gate_rms_bwd(y, z, dout, g, name):
    T = y.shape[0]

    def body(y_ref, z_ref, d_ref, g_ref, dy_ref, dz_ref, dg_ref):
        i = pl.program_id(0)

        @pl.when(i == 0)
        def _():
            dg_ref[...] = jnp.zeros_like(dg_ref)

        yv, zv, dv = y_ref[...], z_ref[...], d_ref[...]
        s = _sig(zv)
        sz = zv * s
        yg = yv * sz
        r = lax.rsqrt(jnp.mean(yg * yg, axis=-1, keepdims=True) + LN_EPS)
        dg_ref[...] += jnp.sum(dv * yg * r, axis=0, keepdims=True)
        dn = dv * g_ref[...]
        dyg = r * dn - yg * (r * r * r) * jnp.mean(dn * yg, axis=-1, keepdims=True)
        dy_ref[...] = dyg * sz
        dz_ref[...] = (dyg * yv * s * (1.0 + zv * (1.0 - s))).astype(BF16)

    return pl.pallas_call(
        body, name=name, grid=(T // ROW_T,), in_specs=[_rows(T), _rows(T), _rows(T), _vec()],
        out_specs=(_rows(T), _rows(T), _vec()),
        out_shape=(jax.ShapeDtypeStruct((T, D), F32), jax.ShapeDtypeStruct((T, D), BF16), jax.ShapeDtypeStruct((1, D), F32)),
        compiler_params=_cparams(("arbitrary",)))(y, z, dout, g)


def _to_bf16(x, name):
    T = x.shape[0]

    def body(x_ref, o_ref):
        o_ref[...] = x_ref[...].astype(BF16)

    return pl.pallas_call(body, name=name, grid=(T // ROW_T,), in_specs=[_rows(T)], out_specs=_rows(T),
                          out_shape=jax.ShapeDtypeStruct((T, D), BF16), compiler_params=_cparams(("parallel",)))(x)


def _loss_head(y, target, name):
    T = y.shape[0]

    def body(y_ref, t_ref, s_ref, d_ref):
        i = pl.program_id(0)

        @pl.when(i == 0)
        def _():
            s_ref[...] = jnp.zeros_like(s_ref)

        err = y_ref[...] - t_ref[...]
        s_ref[...] += jnp.sum(jnp.sum(err * err, axis=1, keepdims=True), axis=0, keepdims=True)
        d_ref[...] = err * (1.0 / D)

    return pl.pallas_call(
        body, name=name, grid=(T // ROW_T,), in_specs=[_rows(T), _rows(T)],
        out_specs=(pl.BlockSpec((SUBLANES, LANES), lambda i: (0, 0)), _rows(T)),
        out_shape=(jax.ShapeDtypeStruct((SUBLANES, LANES), F32), jax.ShapeDtypeStruct((T, D), F32)),
        compiler_params=_cparams(("arbitrary",)))(y, target)


def _taps_fwd(pad_ref, w_ref, K, base):
    off = CONV_PAD - (K - 1)
    acc = w_ref[0:1, :] * pad_ref[pl.ds(base + off, CONV_R), :]
    for k in range(1, K):
        acc = acc + w_ref[k:k + 1, :] * pad_ref[pl.ds(base + off + k, CONV_R), :]
    return acc


def _taps_bwd(padd_ref, w_ref, K, base):
    acc = w_ref[0:1, :] * padd_ref[pl.ds(base + (K - 1), CONV_R), :]
    for k in range(1, K):
        acc = acc + w_ref[k:k + 1, :] * padd_ref[pl.ds(base + (K - 1) - k, CONV_R), :]
    return acc


def _f32(ref, rows):
    return ref[rows, :].astype(F32)


def _fold8(v):
    return v.reshape(CONV_R // SUBLANES, SUBLANES, v.shape[-1]).sum(0)


def _wgrad_acc(dw_ref, pad_ref, d, K, base):
    off = CONV_PAD - (K - 1)
    for k in range(K):
        dw_ref[k * SUBLANES:(k + 1) * SUBLANES, :] += _fold8(d * pad_ref[pl.ds(base + off + k, CONV_R), :])


def _loop_rows(T, fn):
    def step(r, carry):
        fn(pl.multiple_of(r * CONV_R, CONV_R))
        return carry
    lax.fori_loop(0, T // CONV_R, step, 0)


def _col(T, off_blocks=0, rows=None):
    return pl.BlockSpec((T if rows is None else rows, LANES), lambda j: (0, j + off_blocks))


def _conv_call(body, name, T, n_tiles, in_specs, out_specs, out_shape, n_pad, n_padd=0):
    scratch = [pltpu.VMEM((T + CONV_PAD, LANES), F32)] * (n_pad + n_padd)
    return pl.pallas_call(body, name=name, grid=(n_tiles,), in_specs=in_specs, out_specs=out_specs, out_shape=out_shape,
                          scratch_shapes=scratch, compiler_params=_cparams(("parallel",)))


def _zero_head(ref):
    ref[0:CONV_PAD, :] = jnp.zeros((CONV_PAD, LANES), F32)


def _zero_tail(ref, T):
    ref[T:T + CONV_PAD, :] = jnp.zeros((CONV_PAD, LANES), F32)


def _sds(shape, dtype=F32):
    return jax.ShapeDtypeStruct(shape, dtype)


def _conv_a_fwd(ua, w, b, name):
    T = ua.shape[0]
    K, nt = CONV_A, D // LANES

    def body(al_ref, ag_ref, w_ref, b_ref, o_ref, pad_ref):
        _zero_head(pad_ref)

        def pre(base):
            rows = pl.ds(base, CONV_R)
            pad_ref[pl.ds(base + CONV_PAD, CONV_R), :] = _f32(al_ref, rows) * _sig(_f32(ag_ref, rows))
        _loop_rows(T, pre)

        def main(base):
            o_ref[pl.ds(base, CONV_R), :] = _taps_fwd(pad_ref, w_ref, K, base) + b_ref[...]
        _loop_rows(T, main)

    return _conv_call(body, name, T, nt, [_col(T), _col(T, nt), _col(T, rows=K), _col(T, rows=1)], _col(T),
                      _sds((T, D)), 1)(ua, ua, w, b)


def _conv_a_bwd(ua, w, dac, name):
    T = ua.shape[0]
    K, nt = CONV_A, D // LANES

    def body(al_ref, ag_ref, w_ref, d_ref, dal_ref, dag_ref, dw_ref, db_ref, pad_ref, padd_ref):
        _zero_head(pad_ref)
        _zero_tail(padd_ref, T)
        dw_ref[...] = jnp.zeros_like(dw_ref)
        db_ref[...] = jnp.zeros_like(db_ref)

        def pre(base):
            rows = pl.ds(base, CONV_R)
            pad_ref[pl.ds(base + CONV_PAD, CONV_R), :] = _f32(al_ref, rows) * _sig(_f32(ag_ref, rows))
            padd_ref[rows, :] = d_ref[rows, :]
        _loop_rows(T, pre)

        def main(base):
            rows = pl.ds(base, CONV_R)
            d = d_ref[rows, :]
            _wgrad_acc(dw_ref, pad_ref, d, K, base)
            db_ref[...] += _fold8(d)
            da = _taps_bwd(padd_ref, w_ref, K, base)
            al, s = _f32(al_ref, rows), _sig(_f32(ag_ref, rows))
            dal_ref[rows, :] = (da * s).astype(BF16)
            dag_ref[rows, :] = (da * al * s * (1.0 - s)).astype(BF16)
        _loop_rows(T, main)

    return _conv_call(body, name, T, nt, [_col(T), _col(T, nt), _col(T, rows=K), _col(T)],
                      (_col(T), _col(T), _col(T, rows=K * SUBLANES), _col(T, rows=SUBLANES)),
                      (_sds((T, D), BF16), _sds((T, D), BF16), _sds((K * SUBLANES, D)), _sds((SUBLANES, D))), 1, 1)(ua, ua, w, dac)


def _conv_b_fwd(xu, w, b, name):
    T, C = xu.shape
    K, nt = CONV_B, C // LANES

    def body(x_ref, w_ref, b_ref, o_ref, pad_ref):
        _zero_head(pad_ref)
        pad_ref[CONV_PAD:CONV_PAD + T, :] = x_ref[...].astype(F32)

        def main(base):
            hc = _taps_fwd(pad_ref, w_ref, K, base) + b_ref[...]
            o_ref[pl.ds(base, CONV_R), :] = hc * _sig(hc)
        _loop_rows(T, main)

    return _conv_call(body, name, T, nt, [_col(T), _col(T, rows=K), _col(T, rows=1)], _col(T), _sds((T, C)), 1)(xu, w, b)


def _conv_b_bwd(xu, w, b, dxs, dbs, dcs, name):
    T, C = xu.shape
    K, nt = CONV_B, C // LANES
    nx, nb = dxs.shape[1] // LANES, dbs.shape[1] // LANES

    def body(x_ref, w_ref, b_ref, d1_ref, d2_ref, d3_ref, dx_ref, dw_ref, db_ref, pad_ref, padd_ref):
        j = pl.program_id(0)
        _zero_head(pad_ref)
        _zero_tail(padd_ref, T)
        dw_ref[...] = jnp.zeros_like(dw_ref)
        db_ref[...] = jnp.zeros_like(db_ref)
        pad_ref[CONV_PAD:CONV_PAD + T, :] = x_ref[...].astype(F32)

        def pre(base):
            rows = pl.ds(base, CONV_R)
            hc = _taps_fwd(pad_ref, w_ref, K, base) + b_ref[...]
            s = _sig(hc)
            d = jnp.where(j < nx, d1_ref[rows, :], jnp.where(j < nx + nb, d2_ref[rows, :], d3_ref[rows, :]))
            padd_ref[rows, :] = d * s * (1.0 + hc * (1.0 - s))
        _loop_rows(T, pre)

        def main(base):
            d = padd_ref[pl.ds(base, CONV_R), :]
            _wgrad_acc(dw_ref, pad_ref, d, K, base)
            db_ref[...] += _fold8(d)
            dx_ref[pl.ds(base, CONV_R), :] = _taps_bwd(padd_ref, w_ref, K, base).astype(BF16)
        _loop_rows(T, main)

    def piece(lo, n):
        return pl.BlockSpec((T, LANES), lambda j: (0, jnp.clip(j - lo, 0, n - 1)))

    return _conv_call(body, name, T, nt,
                      [_col(T), _col(T, rows=K), _col(T, rows=1), piece(0, nx), piece(nx, nb), piece(nx + nb, nt - nx - nb)],
                      (_col(T), _col(T, rows=K * SUBLANES), _col(T, rows=SUBLANES)),
                      (_sds((T, C), BF16), _sds((K * SUBLANES, C)), _sds((SUBLANES, C))), 1, 1)(xu, w, b, dxs, dbs, dcs)


def _conv_c_fwd(uo, w, name):
    T = uo.shape[0]
    K, nt = CONV_C, D // LANES

    def body(bg_ref, cg_ref, v_ref, w_ref, o_ref, pad_ref):
        _zero_head(pad_ref)
        pad_ref[CONV_PAD:CONV_PAD + T, :] = cg_ref[...].astype(F32) * v_ref[...].astype(F32)

        def main(base):
            rows = pl.ds(base, CONV_R)
            o_ref[rows, :] = (_f32(bg_ref, rows) * _taps_fwd(pad_ref, w_ref, K, base)).astype(BF16)
        _loop_rows(T, main)

    return _conv_call(body, name, T, nt, [_col(T), _col(T, nt), _col(T, 2 * nt), _col(T, rows=K)], _col(T),
                      _sds((T, D), BF16), 1)(uo, uo, uo, w)


def _conv_c_bwd(uo, w, dsc, name):
    T = uo.shape[0]
    K, nt = CONV_C, D // LANES

    def body(bg_ref, cg_ref, v_ref, w_ref, d_ref, dbg_ref, dcg_ref, dv_ref, dw_ref, pad_ref, padd_ref):
        _zero_head(pad_ref)
        _zero_tail(padd_ref, T)
        dw_ref[...] = jnp.zeros_like(dw_ref)
        pad_ref[CONV_PAD:CONV_PAD + T, :] = cg_ref[...].astype(F32) * v_ref[...].astype(F32)

        def pre(base):
            rows = pl.ds(base, CONV_R)
            d = d_ref[rows, :]
            dbg_ref[rows, :] = (d * _taps_fwd(pad_ref, w_ref, K, base)).astype(BF16)
            padd_ref[rows, :] = d * _f32(bg_ref, rows)
        _loop_rows(T, pre)

        def main(base):
            rows = pl.ds(base, CONV_R)
            _wgrad_acc(dw_ref, pad_ref, padd_ref[rows, :], K, base)
            dq = _taps_bwd(padd_ref, w_ref, K, base)
            dcg_ref[rows, :] = (dq * _f32(v_ref, rows)).astype(BF16)
            dv_ref[rows, :] = (dq * _f32(cg_ref, rows)).astype(BF16)
        _loop_rows(T, main)

    return _conv_call(body, name, T, nt, [_col(T), _col(T, nt), _col(T, 2 * nt), _col(T, rows=K), _col(T)],
                      (_col(T), _col(T), _col(T), _col(T, rows=K * SUBLANES)),
                      (_sds((T, D), BF16), _sds((T, D), BF16), _sds((T, D), BF16), _sds((K * SUBLANES, D))), 1, 1)(uo, uo, uo, w, dsc)


def _conv_f_fwd(up, w, b, name):
    T = up.shape[0]
    K, nt = CONV_F, D_FF // LANES

    def body(u1_ref, u2_ref, w1_ref, w2_ref, b1_ref, b2_ref, o_ref, pad1_ref, pad2_ref):
        _zero_head(pad1_ref)
        _zero_head(pad2_ref)
        pad1_ref[CONV_PAD:CONV_PAD + T, :] = u1_ref[...].astype(F32)
        pad2_ref[CONV_PAD:CONV_PAD + T, :] = u2_ref[...].astype(F32)

        def main(base):
            h1 = _taps_fwd(pad1_ref, w1_ref, K, base) + b1_ref[...]
            h2 = _taps_fwd(pad2_ref, w2_ref, K, base) + b2_ref[...]
            o_ref[pl.ds(base, CONV_R), :] = (h1 * _sig(h1) * h2).astype(BF16)
        _loop_rows(T, main)

    return _conv_call(body, name, T, nt,
                      [_col(T), _col(T, nt), _col(T, rows=K), _col(T, nt, rows=K), _col(T, rows=1), _col(T, nt, rows=1)],
                      _col(T), _sds((T, D_FF), BF16), 2)(up, up, w, w, b, b)


def _conv_f_bwd(up, w, b, dact, name):
    T = up.shape[0]
    K, nt = CONV_F, D_FF // LANES

    def body(u1_ref, u2_ref, w1_ref, w2_ref, b1_ref, b2_ref, d_ref, du1_ref, du2_ref, dw1_ref, dw2_ref, db1_ref, db2_ref,
             pad1_ref, pad2_ref, padd1_ref, padd2_ref):
        _zero_head(pad1_ref)
        _zero_head(pad2_ref)
        _zero_tail(padd1_ref, T)
        _zero_tail(padd2_ref, T)
        for r in (dw1_ref, dw2_ref, db1_ref, db2_ref):
            r[...] = jnp.zeros_like(r)
        pad1_ref[CONV_PAD:CONV_PAD + T, :] = u1_ref[...].astype(F32)
        pad2_ref[CONV_PAD:CONV_PAD + T, :] = u2_ref[...].astype(F32)

        def pre(base):
            rows = pl.ds(base, CONV_R)
            h1 = _taps_fwd(pad1_ref, w1_ref, K, base) + b1_ref[...]
            h2 = _taps_fwd(pad2_ref, w2_ref, K, base) + b2_ref[...]
            s = _sig(h1)
            d = _f32(d_ref, rows)
            padd1_ref[rows, :] = d * h2 * s * (1.0 + h1 * (1.0 - s))
            padd2_ref[rows, :] = d * h1 * s
        _loop_rows(T, pre)

        def main(base):
            rows = pl.ds(base, CONV_R)
            d1, d2 = padd1_ref[rows, :], padd2_ref[rows, :]
            _wgrad_acc(dw1_ref, pad1_ref, d1, K, base)
            _wgrad_acc(dw2_ref, pad2_ref, d2, K, base)
            db1_ref[...] += _fold8(d1)
            db2_ref[...] += _fold8(d2)
            du1_ref[rows, :] = _taps_bwd(padd1_ref, w1_ref, K, base).astype(BF16)
            du2_ref[rows, :] = _taps_bwd(padd2_ref, w2_ref, K, base).astype(BF16)
        _loop_rows(T, main)

    wrow, brow = _col(T, rows=K * SUBLANES), _col(T, rows=SUBLANES)
    return _conv_call(body, name, T, nt,
                      [_col(T), _col(T, nt), _col(T, rows=K), _col(T, nt, rows=K), _col(T, rows=1), _col(T, nt, rows=1), _col(T)],
                      (_col(T), _col(T), wrow, wrow, brow, brow),
                      (_sds((T, D_FF), BF16), _sds((T, D_FF), BF16), _sds((K * SUBLANES, D_FF)), _sds((K * SUBLANES, D_FF)),
                       _sds((SUBLANES, D_FF)), _sds((SUBLANES, D_FF))), 2, 2)(up, up, w, w, b, b, dact)


def _dot(a, b, dims="nn"):
    return lax.dot_general(a.astype(BF16), b.astype(BF16), _DIMS[dims], preferred_element_type=F32)


def _dot_mask(mask, v, mask_left):
    mb = mask.astype(BF16)
    hi = v.astype(BF16)
    r1 = v - hi.astype(F32)
    mid = r1.astype(BF16)
    lo = (r1 - mid.astype(F32)).astype(BF16)
    d = [jnp.dot(mb, t, preferred_element_type=F32) if mask_left else jnp.dot(t, mb, preferred_element_type=F32) for t in (hi, mid, lo)]
    return (d[0] + d[1]) + d[2]


def _ssd_small(xcr_ref, xrr_ref, bc_ref, br_ref, ac_ref, ar_ref):
    Q = SSD_Q
    li = lax.broadcasted_iota(jnp.int32, (Q, Q), 0)
    si = lax.broadcasted_iota(jnp.int32, (Q, Q), 1)
    tril = li >= si
    dtc = jax.nn.softplus(xcr_ref[...] + bc_ref[...])
    dtr = jax.nn.softplus(xrr_ref[...] + br_ref[...])
    cumc = _dot_mask(tril, dtc * ac_ref[...], True)
    cumr = _dot_mask(li <= si, dtr * ar_ref[...], False)
    return tril, dtc, dtr, cumc, cumr


def _ssd_specs(nc, rev):
    Q = SSD_Q
    cc = (lambda c: nc - 1 - c) if rev else (lambda c: c)
    x_spec = pl.BlockSpec((Q, 2 * LANES), lambda g, c: (cc(c), g))
    b_spec = pl.BlockSpec((Q, LANES), lambda g, c: (cc(c), 8 + g))
    c_spec = pl.BlockSpec((Q, LANES), lambda g, c: (cc(c), 12 + g))
    colm = pl.BlockSpec((None, Q, LANES), lambda g, c: (g, cc(c), 0))
    rowm = pl.BlockSpec((None, SUBLANES, Q), lambda g, c: (g, 0, cc(c)))
    colv = pl.BlockSpec((None, 1, LANES), lambda g, c: (g, 0, 0))
    rowv = pl.BlockSpec((None, SUBLANES, 1), lambda g, c: (g, 0, 0))
    st_spec = pl.BlockSpec((None, None, 2 * LANES, N_STATE), lambda g, c: (cc(c), g, 0, 0))
    return x_spec, b_spec, c_spec, colm, rowm, colv, rowv, st_spec


def _ssd_fwd(xc, raw_col, raw_row, bias_col, bias_row, a_col, a_row, dskip, name):
    T = xc.shape[0]
    Q = SSD_Q
    nc = T // Q
    x_spec, b_spec, c_spec, colm, rowm, colv, rowv, st_spec = _ssd_specs(nc, False)

    def body(dk_ref, x_ref, b_ref, c_ref, xcr_ref, xrr_ref, bc_ref, br_ref, ac_ref, ar_ref, y_ref, st_ref, h_ref):
        g = pl.program_id(0)

        @pl.when(pl.program_id(1) == 0)
        def _():
            h_ref[...] = jnp.zeros_like(h_ref)

        tril, dtc, dtr, cumc, cumr = _ssd_small(xcr_ref, xrr_ref, bc_ref, br_ref, ac_ref, ar_ref)
        Bm, Cm = b_ref[...], c_ref[...]
        S = _dot(Cm, Bm, "nt")
        lo = lax.broadcasted_iota(jnp.int32, (Q, LANES), 1) < HEAD_P
        rlo = lax.broadcasted_iota(jnp.int32, (LANES, N_STATE), 0) < HEAD_P
        st_ref[...] = h_ref[...]
        clast = cumc[Q - 1:Q, :]
        for pr in range(2):
            cols = slice(pr * LANES, (pr + 1) * LANES)
            xp = x_ref[:, cols]
            yd = jnp.zeros((Q, LANES), F32)
            for q in range(2):
                hh = 2 * pr + q
                seg = cumc[:, hh:hh + 1] - cumr[hh:hh + 1, :]
                lm = jnp.where(tril, jnp.exp(jnp.where(tril, seg, 0.0)), 0.0)
                w = S * lm * dtr[hh:hh + 1, :]
                xm = jnp.where(lo if q == 0 else jnp.logical_not(lo), xp, 0.0)
                yd = yd + _dot(w, xm)
            h0, h1 = 2 * pr, 2 * pr + 1
            c0, c1 = cumc[:, h0:h0 + 1], cumc[:, h1:h1 + 1]
            e_pair = jnp.where(lo, jnp.exp(c0), jnp.exp(c1))
            hp = h_ref[cols, :]
            ch = _dot(Cm, hp, "nt")
            dsk = jnp.where(lo, dk_ref[4 * g + h0], dk_ref[4 * g + h1])
            y_ref[:, cols] = yd + e_pair * ch + dsk * xp
            cl0, cl1 = clast[:, h0:h0 + 1], clast[:, h1:h1 + 1]
            sdec = jnp.where(lo, jnp.exp(cl0 - c0) * dtc[:, h0:h0 + 1], jnp.exp(cl1 - c1) * dtc[:, h1:h1 + 1])
            decrow = jnp.where(rlo, jnp.exp(cl0), jnp.exp(cl1))
            h_ref[cols, :] = hp * decrow + _dot(xp * sdec, Bm, "tn")

    smem = pl.BlockSpec(memory_space=pltpu.SMEM)
    return pl.pallas_call(
        body, name=name, grid=(N_GROUPS, nc),
        in_specs=[smem, x_spec, b_spec, c_spec, colm, rowm, colv, rowv, colv, rowv],
        out_specs=(x_spec, st_spec),
        out_shape=(_sds((T, D)), _sds((nc, N_GROUPS, 2 * LANES, N_STATE))),
        scratch_shapes=[pltpu.VMEM((2 * LANES, N_STATE), F32)],
        compiler_params=_cparams(("parallel", "arbitrary")))(dskip, xc, xc, xc, raw_col, raw_row, bias_col, bias_row, a_col, a_row)


def _ssd_bwd(xc, raw_col, raw_row, bias_col, bias_row, a_col, a_row, dskip, states, dy, name):
    T = xc.shape[0]
    Q = SSD_Q
    nc = T // Q
    x_spec, b_spec, c_spec, colm, rowm, colv, rowv, st_spec = _ssd_specs(nc, True)
    bo_spec = pl.BlockSpec((Q, LANES), lambda g, c: (nc - 1 - c, g))
    dd_spec = pl.BlockSpec((None, None, SUBLANES, 2 * LANES), lambda g, c: (nc - 1 - c, g, 0, 0))

    def body(dk_ref, x_ref, b_ref, c_ref, xcr_ref, xrr_ref, bc_ref, br_ref, ac_ref, ar_ref, st_ref, dy_ref,
             dx_ref, db_ref, dc_ref, sq_ref, cms_ref, ddac_ref, ddar_ref, dd_ref, dh_ref):
        g = pl.program_id(0)

        @pl.when(pl.program_id(1) == 0)
        def _():
            dh_ref[...] = jnp.zeros_like(dh_ref)

        tril, dtc, dtr, cumc, cumr = _ssd_small(xcr_ref, xrr_ref, bc_ref, br_ref, ac_ref, ar_ref)
        Bm, Cm = b_ref[...], c_ref[...]
        S = _dot(Cm, Bm, "nt")
        lane = lax.broadcasted_iota(jnp.int32, (Q, LANES), 1)
        sub = lax.broadcasted_iota(jnp.int32, (SUBLANES, Q), 0)
        rowi = lax.broadcasted_iota(jnp.int32, (Q, LANES), 0)
        lo = lane < HEAD_P
        rlo = lax.broadcasted_iota(jnp.int32, (LANES, N_STATE), 0) < HEAD_P
        clast = cumc[Q - 1:Q, :]
        ds_g = jnp.zeros((Q, Q), F32)
        dcm = jnp.zeros((Q, N_STATE), F32)
        dbm = jnp.zeros((Q, N_STATE), F32)
        dcum_col = jnp.zeros((Q, LANES), F32)
        dcum_row = jnp.zeros((SUBLANES, Q), F32)
        sq_col = jnp.zeros((Q, LANES), F32)
        cms_row = jnp.zeros((SUBLANES, Q), F32)
        for pr in range(2):
            cols = slice(pr * LANES, (pr + 1) * LANES)
            xp, dyp = x_ref[:, cols], dy_ref[:, cols]
            hin, dhp = st_ref[cols, :], dh_ref[cols, :]
            h0, h1 = 2 * pr, 2 * pr + 1
            c0, c1 = cumc[:, h0:h0 + 1], cumc[:, h1:h1 + 1]
            cl0, cl1 = clast[:, h0:h0 + 1], clast[:, h1:h1 + 1]
            e_pair = jnp.where(lo, jnp.exp(c0), jnp.exp(c1))
            edec = jnp.where(lo, jnp.exp(cl0 - c0), jnp.exp(cl1 - c1))
            dt_pair = jnp.where(lo, dtc[:, h0:h0 + 1], dtc[:, h1:h1 + 1])
            sdec = edec * dt_pair
            ch = _dot(Cm, hin, "nt")
            xb = _dot(Bm, dhp, "nt")
            dye = dyp * e_pair
            t1 = dye * ch
            t2 = xp * xb * edec
            hh_prod = dhp * hin
            dsk = jnp.where(lo, dk_ref[4 * g + h0], dk_ref[4 * g + h1])
            dxp = sdec * xb + dsk * dyp
            for q in range(2):
                hh = 2 * pr + q
                mine = lo if q == 0 else jnp.logical_not(lo)
                seg = cumc[:, hh:hh + 1] - cumr[hh:hh + 1, :]
                lm = jnp.where(tril, jnp.exp(jnp.where(tril, seg, 0.0)), 0.0)
                dtrow = dtr[hh:hh + 1, :]
                w = S * lm * dtrow
                dym = jnp.where(mine, dyp, 0.0)
                gl = _dot(dym, xp, "nt") * lm
                ds_g = ds_g + gl * dtrow
                ms = gl * S
                m = ms * dtrow
                dxp = dxp + _dot(w, dym, "tn")
                cms_row = jnp.where(sub == hh, jnp.sum(ms, axis=0, keepdims=True), cms_row)
                dcum_row = jnp.where(sub == hh, -jnp.sum(m, axis=0, keepdims=True), dcum_row)
                t1h = jnp.sum(jnp.where(mine, t1, 0.0), axis=1, keepdims=True)
                sqh = jnp.sum(jnp.where(mine, t2, 0.0), axis=1, keepdims=True)
                sth = sqh * dtc[:, hh:hh + 1]
                rmine = rlo if q == 0 else jnp.logical_not(rlo)
                hsum = jnp.sum(jnp.sum(jnp.where(rmine, hh_prod, 0.0), axis=1, keepdims=True), axis=0, keepdims=True)
                last = jnp.sum(sth, axis=0, keepdims=True) + jnp.exp(clast[:, hh:hh + 1]) * hsum
                dcol = jnp.sum(m, axis=1, keepdims=True) + t1h - sth
                dcum_col = jnp.where(lane == hh, dcol + jnp.where(rowi == Q - 1, last, 0.0), dcum_col)
                sq_col = jnp.where(lane == hh, sqh, sq_col)
            dcm = dcm + _dot(dye, hin)
            dbm = dbm + _dot(xp * sdec, dhp)
            decrow = jnp.where(rlo, jnp.exp(cl0), jnp.exp(cl1))
            dh_ref[cols, :] = dhp * decrow + _dot(dye, Cm, "tn")
            dx_ref[:, cols] = dxp
            dd_ref[:, cols] = jnp.broadcast_to(jnp.sum(dyp * xp, axis=0, keepdims=True), (SUBLANES, LANES))
        dc_ref[...] = dcm + _dot(ds_g, Bm)
        db_ref[...] = dbm + _dot(ds_g, Cm, "tn")
        li = lax.broadcasted_iota(jnp.int32, (Q, Q), 0)
        si = lax.broadcasted_iota(jnp.int32, (Q, Q), 1)
        ddac_ref[...] = _dot_mask(li <= si, dcum_col, True)
        ddar_ref[...] = _dot_mask(tril, dcum_row, False)
        sq_ref[...] = sq_col
        cms_ref[...] = cms_row

    smem = pl.BlockSpec(memory_space=pltpu.SMEM)
    return pl.pallas_call(
        body, name=name, grid=(N_GROUPS, nc),
        in_specs=[smem, x_spec, b_spec, c_spec, colm, rowm, colv, rowv, colv, rowv, st_spec, x_spec],
        out_specs=(x_spec, bo_spec, bo_spec, colm, rowm, colm, rowm, dd_spec),
        out_shape=(_sds((T, D)), _sds((T, D // 2)), _sds((T, D // 2)), _sds((N_GROUPS, T, LANES)), _sds((N_GROUPS, SUBLANES, T)),
                   _sds((N_GROUPS, T, LANES)), _sds((N_GROUPS, SUBLANES, T)), _sds((nc, N_GROUPS, SUBLANES, 2 * LANES))),
        scratch_shapes=[pltpu.VMEM((2 * LANES, N_STATE), F32)],
        compiler_params=_cparams(("parallel", "arbitrary")))(dskip, xc, xc, xc, raw_col, raw_row, bias_col, bias_row, a_col, a_row,
                                                            states, dy)


def _adam_math(wv, gv, mv, vv):
    c1 = 1.0 - ADAM_B1 ** ADAM_STEP
    c2 = 1.0 - ADAM_B2 ** ADAM_STEP
    mn = ADAM_B1 * mv + (1.0 - ADAM_B1) * gv
    vn = ADAM_B2 * vv + (1.0 - ADAM_B2) * (gv * gv)
    return -ADAM_LR * ((mn / c1) / (jnp.sqrt(vn / c2) + ADAM_EPS) + ADAM_WD * wv), mn, vn


def _adamw_layers(w, g, m, v, l0, Lg, bufs, name):
    L, As, Bs = w.shape
    tr = _tile(As, [], (256, 352, 128))
    has_bufs = bufs is not None

    def body(*refs):
        w_ref, g_ref, m_ref, v_ref = refs[:4]
        d_ref, mo_ref, vo_ref = refs[4 + 3 * has_bufs:]
        d_ref[...], mo_ref[...], vo_ref[...] = _adam_math(w_ref[...], g_ref[...], m_ref[...], v_ref[...])

    spec = pl.BlockSpec((None, tr, Bs), lambda l, i: (l + l0, i, 0))
    args = (w, g, m, v) + (tuple(bufs) if has_bufs else ())
    return pl.pallas_call(
        body, name=name, grid=(Lg, As // tr), in_specs=[spec] * 4 + [_ANY] * (3 * has_bufs), out_specs=(spec,) * 3,
        out_shape=(_sds((L, As, Bs)),) * 3, input_output_aliases={4: 0, 5: 1, 6: 2} if has_bufs else {},
        compiler_params=_cparams(("parallel", "parallel")))(*args)


def _adamw_minor_rows(w, g, m, v, name):
    L, R, C = w.shape
    tr = max(t for t in range(1, C + 1) if C % t == 0 and t * L * R * 4 <= (1 << 20))
    wt, gt, mt, vt = (jnp.transpose(t, (2, 0, 1)) for t in (w, g, m, v))

    def body(w_ref, g_ref, m_ref, v_ref, d_ref, mo_ref, vo_ref):
        d_ref[...], mo_ref[...], vo_ref[...] = _adam_math(w_ref[...], g_ref[...], m_ref[...], v_ref[...])

    spec = pl.BlockSpec((tr, L, R), lambda i: (i, 0, 0))
    out = pl.pallas_call(body, name=name, grid=(C // tr,), in_specs=[spec] * 4, out_specs=(spec,) * 3,
                         out_shape=(_sds((C, L, R)),) * 3, compiler_params=_cparams(("parallel",)))(wt, gt, mt, vt)
    return tuple(jnp.transpose(o, (1, 2, 0)) for o in out) + (jnp.transpose(gt, (1, 2, 0)),)


def _adamw(w, g, m, v, name):
    shape = w.shape
    cols = shape[-1]
    w2, g2, m2, v2 = (t.reshape(-1, cols) for t in (w, g, m, v))
    rows = w2.shape[0]
    tr = 256 if (rows % 256 == 0 and rows > 256) else rows
    c1 = 1.0 - ADAM_B1 ** ADAM_STEP
    c2 = 1.0 - ADAM_B2 ** ADAM_STEP

    def body(w_ref, g_ref, m_ref, v_ref, d_ref, mo_ref, vo_ref):
        gv = g_ref[...]
        mn = ADAM_B1 * m_ref[...] + (1.0 - ADAM_B1) * gv
        vn = ADAM_B2 * v_ref[...] + (1.0 - ADAM_B2) * (gv * gv)
        d_ref[...] = -ADAM_LR * ((mn / c1) / (jnp.sqrt(vn / c2) + ADAM_EPS) + ADAM_WD * w_ref[...])
        mo_ref[...] = mn
        vo_ref[...] = vn

    spec = pl.BlockSpec((tr, cols), lambda i: (i, 0))
    out = pl.pallas_call(body, name=name, grid=(rows // tr,), in_specs=[spec] * 4, out_specs=(spec,) * 3,
                         out_shape=(_sds((rows, cols)),) * 3, compiler_params=_cparams(("parallel",)))(w2, g2, m2, v2)
    return tuple(o.reshape(shape) for o in out)


def _place():
    x, y, c = lax.axis_index("x"), lax.axis_index("y"), lax.axis_index("c")
    chips = [(1 - x, y), (x, 1 - y), (1 - x, 1 - y)]
    return x, y, c, chips


_ANY = pl.BlockSpec(memory_space=pl.ANY)


TENSORS = (("e_w_in", "row", 2, 4096, 1284, 1024), ("e_w_out", "row", 2, 2048, 1024, 512), ("o_w_in", "col", 2, 1024, 3072, 768),
           ("o_w_out", "row", 2, 1024, 1024, 256), ("f_w_up", "col", 4, 1024, 5632, 1408), ("f_w_down", "row", 4, 2816, 1024, 704),
           ("ple_w_proj", "col", 4, 256, 1024, 256), ("ple_w_gate", "row", 4, 1024, 1024, 256))
MIX, FFN = "mix", "ffn"
W_GROUPS = (((0, MIX),), ((0, FFN), (1, MIX)), ((1, FFN), (2, MIX)), ((2, FFN), (3, MIX), (3, FFN)))
G_GROUPS = (((3, FFN), (3, MIX), (2, FFN), (2, MIX), (1, FFN), (1, MIX)), ((0, FFN),), ((0, MIX),))


def _tensor_layer(name, layer):
    if name.startswith("e_"):
        return layer // 2 if layer % 2 == 0 else None
    if name.startswith("o_"):
        return layer // 2 if layer % 2 == 1 else None
    return layer


def _part(name):
    return MIX if name.startswith(("e_", "o_")) else FFN


def _group_items(members):
    items = []
    for name, kind, L, A, B, n in TENSORS:
        tls = sorted(t for t in (_tensor_layer(name, l) for l, part in members if part == _part(name)) if t is not None)
        if tls:
            assert tls == list(range(tls[0], tls[0] + len(tls)))
            items.append((name, kind, len(tls), A, B, n, tls[0]))
    return items


def _hwin(ref, it, k, h):
    name, kind, Lg, A, B, n, l0 = it
    if kind == "row":
        return ref.at[:, pl.ds(pl.multiple_of(k * n + h * (n // 2), 16), n // 2), :]
    return ref.at[:, pl.ds(pl.multiple_of(h * (A // 2), 16), A // 2), pl.ds(pl.multiple_of(k * n, LANES), n)]


def _shard_dims(kind, A, B, n):
    return (n, B) if kind == "row" else (A, n)


def _cast_into(w, it, me):
    name, kind, Lg, A, B, n, l0 = it
    As, Bs = _shard_dims(kind, A, B, n)

    def body(me_ref, w_ref, o_ref):
        o_ref[...] = w_ref[...].astype(BF16)

    omap = (lambda l, m: (l, m[0], 0)) if kind == "row" else (lambda l, m: (l, 0, m[0]))
    grid_spec = pltpu.PrefetchScalarGridSpec(
        num_scalar_prefetch=1, grid=(Lg,), in_specs=[pl.BlockSpec((None, As, Bs), lambda l, m: (l + l0, 0, 0))],
        out_specs=pl.BlockSpec((None, As, Bs), omap))
    return pl.pallas_call(body, name=f"cast_{name}_{l0}", grid_spec=grid_spec, out_shape=_sds((Lg, A, B), BF16),
                          compiler_params=_cparams(("parallel",)))(me, w.reshape(-1, As, Bs))


_HBM = pl.BlockSpec(memory_space=pltpu.HBM)
_SEM = pl.BlockSpec(memory_space=pltpu.SEMAPHORE)
_EFFECT = pltpu.SideEffectType.DATAFLOW_SIDE_EFFECTING


def _hbm(a):
    return pltpu.with_memory_space_constraint(a, pltpu.HBM)


def _split_start(thru, n_copies, issue, name, after=None):
    N = len(thru)
    has_after = after is not None

    def body(*refs):
        outs = refs[N + has_after:2 * N + has_after]
        send_sems, recv_sems, token = refs[2 * N + has_after:]
        for cp in issue(outs, send_sems, recv_sems):
            cp.start()
        token[...] = jnp.zeros_like(token)

    out = pl.pallas_call(
        body, name=name, in_specs=[_HBM] * N + ([_ANY] if has_after else []),
        out_specs=(_HBM,) * N + (_SEM, _SEM, pl.BlockSpec(memory_space=pltpu.VMEM)),
        out_shape=tuple(pltpu.HBM(a.shape, a.dtype) for a in thru)
        + (pltpu.SemaphoreType.DMA((n_copies,)), pltpu.SemaphoreType.DMA((n_copies,)), _sds((SUBLANES, LANES))),
        input_output_aliases={t: t for t in range(N)},
        compiler_params=pltpu.CompilerParams(has_side_effects=_EFFECT))(*[_hbm(a) for a in thru], *([after] if has_after else []))
    return list(out[:N]), out[N], out[N + 1], out[N + 2]


def _split_wait(thru, send_sems, recv_sems, after, waits, name):
    N = len(thru)
    after = list(after) if isinstance(after, (list, tuple)) else [after]

    def body(*refs):
        ins = refs[:N]
        for cp, side in waits(ins, refs[N], refs[N + 1]):
            if side == "send":
                cp.wait_send()
            else:
                cp.wait_recv()

    out = pl.pallas_call(
        body, name=name, in_specs=[_HBM] * N + [_SEM, _SEM] + [_ANY] * len(after), out_specs=(_HBM,) * N,
        out_shape=tuple(pltpu.HBM(a.shape, a.dtype) for a in thru), input_output_aliases={t: t for t in range(N)},
        compiler_params=pltpu.CompilerParams(has_side_effects=_EFFECT))(*thru, send_sems, recv_sems, *after)
    return list(out)


def _rcopy(send_sems, recv_sems, k, src, dst, to):
    return pltpu.make_async_remote_copy(src_ref=src, dst_ref=dst, send_sem=send_sems.at[k], recv_sem=recv_sems.at[k],
                                        device_id=to, device_id_type=MESH)


def _gather_copies(items, refs, send_sems, recv_sems, what):
    x, y, c, chips = _place()
    me = 2 * x + y
    out = []
    for t, it in enumerate(items):
        mine = _hwin(refs[t], it, me, c)
        for j, (px, py) in enumerate(chips):
            if what == "start":
                out.append(_rcopy(send_sems, recv_sems, 3 * t + j, mine, mine, (px, py, c)))
            else:
                slot = _hwin(refs[t], it, 2 * px + py, c)
                out.append((_rcopy(send_sems, recv_sems, 3 * t + j, mine, mine, (px, py, c)), "send"))
                out.append((_rcopy(send_sems, recv_sems, 3 * t + j, slot, slot, (px, py, c)), "recv"))
    return out


def _gather_start(fulls, items, name, after=None):
    return _split_start(fulls, 3 * len(items), functools.partial(_gather_copies, items, what="start"), name, after)


def _gather_wait(fulls, send_sems, recv_sems, after, items, name):
    return _split_wait(fulls, send_sems, recv_sems, after, functools.partial(_gather_copies, items, what="wait"), name)


def _gather_fwd(fulls, items, name, ws=None):
    N = len(fulls)
    has_ws = ws is not None

    def body(*refs):
        outs = refs[N + has_ws:2 * N + has_ws]
        rest = refs[2 * N + has_ws:]
        x, y, c, chips = _place()
        me = 2 * x + y
        sib = (x, y, 1 - c)
        if has_ws:
            ws_ref = refs[N]
            WS_ref, send_sems, recv_sems, lsem = rest
            loc = pltpu.make_async_copy(ws_ref, WS_ref.at[me], lsem)
            loc.start()
        else:
            send_sems, recv_sems = rest
        rc = functools.partial(_rcopy, send_sems, recv_sems)
        cps = []
        for t, it in enumerate(items):
            for j, (px, py) in enumerate(chips):
                slot = _hwin(outs[t], it, 2 * px + py, c)
                cps.append(rc(3 * t + j, slot, slot, sib))
        if has_ws:
            cps += [rc(3 * N + j, ws_ref, WS_ref.at[me], (*chip, c)) for j, chip in enumerate(chips)]
        for cp in cps:
            cp.start()
        for t, it in enumerate(items):
            for j, (px, py) in enumerate(chips):
                oslot = _hwin(outs[t], it, 2 * px + py, 1 - c)
                rc(3 * t + j, oslot, oslot, sib).wait_recv()
        if has_ws:
            for j, (px, py) in enumerate(chips):
                sslot = WS_ref.at[2 * px + py]
                rc(3 * N + j, sslot, sslot, sib).wait_recv()
        for cp in cps:
            cp.wait_send()
        if has_ws:
            loc.wait()

    ns = 3 * N + (3 if has_ws else 0)
    out_shape = tuple(_sds(f.shape, f.dtype) for f in fulls)
    scratch = [pltpu.SemaphoreType.DMA((ns,)), pltpu.SemaphoreType.DMA((ns,))]
    args = list(fulls)
    if has_ws:
        out_shape += (_sds((4,) + ws.shape, ws.dtype),)
        scratch.append(pltpu.SemaphoreType.DMA(()))
        args.append(ws)
    out = pl.pallas_call(
        body, name=name, in_specs=[_ANY] * len(args), out_specs=(_ANY,) * len(out_shape), out_shape=out_shape,
        input_output_aliases={t: t for t in range(N)}, scratch_shapes=scratch,
        compiler_params=pltpu.CompilerParams(has_side_effects=True))(*args)
    return (list(out[:N]), out[N]) if has_ws else (list(out), None)


def _half_shape(it):
    name, kind, Lg, A, B, n, l0 = it
    return (Lg, 4, n // 2, B) if kind == "row" else (Lg, A // 2, B)


def _piece_shape(it):
    name, kind, Lg, A, B, n, l0 = it
    return (Lg, n // 2, B) if kind == "row" else (Lg, A // 2, n)


def _swap_copies(items, refs, send_sems, recv_sems, what):
    N = len(items)
    x, y, c, _ = _place()
    sib = (x, y, 1 - c)
    out = []
    for t, it in enumerate(items):
        name_, kind, Lg, A, B, n, l0 = it
        if kind == "row":
            cps = [_rcopy(send_sems, recv_sems, 4 * t + k, _hwin(refs[t], it, k, 1 - c), refs[N + t].at[:, k], sib) for k in range(4)]
        else:
            src = refs[t].at[:, pl.ds(pl.multiple_of((1 - c) * (A // 2), 16), A // 2), :]
            cps = [_rcopy(send_sems, recv_sems, 4 * t, src, refs[N + t], sib)]
        for cp in cps:
            if what == "start":
                out.append(cp)
            else:
                out += [(cp, "send"), (cp, "recv")]
    return out


def _swap_start(gs, items, name, after=None):
    lands = [lax.empty(_half_shape(it), F32) for it in items]
    return _split_start(list(gs) + lands, 4 * len(items), functools.partial(_swap_copies, items, what="start"), name, after)


def _swap_wait(thru, send_sems, recv_sems, after, items, name):
    return _split_wait(thru, send_sems, recv_sems, after, functools.partial(_swap_copies, items, what="wait"), name)


def _add_half(g, ra, it, cvec):
    name, kind, Lg, A, B, n, l0 = it
    if kind == "row":
        blk = (None, n // 2, B)
        grid = (Lg, 4)
        g_spec = pl.BlockSpec(blk, lambda l, k, cr: (l, 2 * k + cr[0], 0))
        h_spec = pl.BlockSpec((None, None, n // 2, B), lambda l, k, cr: (l, k, 0, 0))
    else:
        tr = _tile(A // 2, [], (256, 128))
        nb = (A // 2) // tr
        grid = (Lg, nb)
        g_spec = pl.BlockSpec((None, tr, B), lambda l, i, cr: (l, cr[0] * nb + i, 0))
        h_spec = pl.BlockSpec((None, tr, B), lambda l, i, cr: (l, i, 0))

    def body(c_ref, g_ref, r_ref, o_ref):
        o_ref[...] = (g_ref[...] + r_ref[...]).astype(BF16)

    grid_spec = pltpu.PrefetchScalarGridSpec(num_scalar_prefetch=1, grid=grid, in_specs=[g_spec, h_spec], out_specs=h_spec)
    return pl.pallas_call(body, name=f"addhalf_{name}_{l0}", grid_spec=grid_spec, out_shape=_sds(_half_shape(it), BF16),
                          compiler_params=_cparams(("parallel", "parallel")))(cvec, g, ra)


def _scatter_copies(items, refs, send_sems, recv_sems, what):
    N = len(items)
    x, y, c, chips = _place()
    out = []
    for t, it in enumerate(items):
        name, kind, Lg, A, B, n, l0 = it
        for j, (px, py) in enumerate(chips):
            k = 2 * px + py
            src = refs[t].at[:, k] if kind == "row" else refs[t].at[:, :, pl.ds(pl.multiple_of(k * n, LANES), n)]
            cp = _rcopy(send_sems, recv_sems, 3 * t + j, src, refs[N + t].at[j], (px, py, c))
            if what == "start":
                out.append(cp)
            else:
                out += [(cp, "send"), (cp, "recv")]
    return out


def _scatter_start(ps, items, name):
    lands = [lax.empty((3,) + _piece_shape(it), BF16) for it in items]
    return _split_start(list(ps) + lands, 3 * len(items), functools.partial(_scatter_copies, items, what="start"), name)


def _scatter_wait(thru, send_sems, recv_sems, after, items, name):
    return _split_wait(thru, send_sems, recv_sems, after, functools.partial(_scatter_copies, items, what="wait"), name)


def _sum_own(p, rc, it, mevec, buf):
    name, kind, Lg, A, B, n, l0 = it
    As, Bs = _shard_dims(kind, A, B, n)
    L = [s[2] for s in TENSORS if s[0] == name][0]
    hb = (As // 2, Bs)
    has_buf = buf is not None

    def body(*refs):
        p_ref, r0, r1, r2 = refs[1:5]
        o_ref = refs[5 + has_buf]
        o_ref[...] = ((p_ref[...].astype(F32) + r0[...].astype(F32)) + r1[...].astype(F32)) + r2[...].astype(F32)

    if kind == "row":
        p_spec = pl.BlockSpec((None, None) + hb, lambda l, m: (l, m[0], 0, 0))
    else:
        p_spec = pl.BlockSpec((None,) + hb, lambda l, m: (l, 0, m[0]))
    r_specs = [pl.BlockSpec((None, None) + hb, functools.partial(lambda l, m, j: (j, l, 0, 0), j=j)) for j in range(3)]
    in_specs = [p_spec] + r_specs + ([_ANY] if has_buf else [])
    grid_spec = pltpu.PrefetchScalarGridSpec(num_scalar_prefetch=1, grid=(Lg,), in_specs=in_specs,
                                             out_specs=pl.BlockSpec((None,) + hb, lambda l, m: (l + l0, m[1], 0)))
    args = (mevec, p, rc, rc, rc) + ((buf,) if has_buf else ())
    return pl.pallas_call(body, name=f"sumown_{name}_{l0}", grid_spec=grid_spec, out_shape=_sds((L, As, Bs)),
                          input_output_aliases={5: 0} if has_buf else {}, compiler_params=_cparams(("parallel",)))(*args)


def _join_halves(rs, items, name):
    N = len(rs)

    def body(*refs):
        outs = refs[N:2 * N]
        send_sems, recv_sems = refs[2 * N:]
        x, y, c, _ = _place()
        sib = (x, y, 1 - c)

        def half(t, h):
            name_, kind, Lg, A, B, n, l0 = items[t]
            hr = _shard_dims(kind, A, B, n)[0] // 2
            return outs[t].at[pl.ds(l0, Lg), pl.ds(pl.multiple_of(h * hr, SUBLANES), hr), :]

        cps = [_rcopy(send_sems, recv_sems, t, half(t, c), half(t, c), sib) for t in range(N)]
        for cp in cps:
            cp.start()
        for t in range(N):
            _rcopy(send_sems, recv_sems, t, half(t, 1 - c), half(t, 1 - c), sib).wait_recv()
        for cp in cps:
            cp.wait_send()

    return list(pl.pallas_call(
        body, name=name, in_specs=[_ANY] * N, out_specs=(_ANY,) * N, out_shape=tuple(_sds(r.shape, r.dtype) for r in rs),
        input_output_aliases={t: t for t in range(N)},
        scratch_shapes=[pltpu.SemaphoreType.DMA((N,)), pltpu.SemaphoreType.DMA((N,))],
        compiler_params=pltpu.CompilerParams(has_side_effects=True))(*rs))


def _allgather_small(v):
    m_per, n = v.shape

    def body(x_ref, out_ref, send_sems, recv_sems, local_sem):
        x, y, c, chips = _place()
        me, sibling = (x, y, c), (x, y, 1 - c)

        def rows(px, py, pc):
            return out_ref.at[pl.ds(pl.multiple_of((4 * px + 2 * py + pc) * m_per, SUBLANES), m_per), :]

        def copy(k, block, to, src=None):
            return pltpu.make_async_remote_copy(src_ref=rows(*block) if src is None else src, dst_ref=rows(*block),
                                                send_sem=send_sems.at[k], recv_sem=recv_sems.at[k], device_id=to, device_id_type=MESH)

        mine = pltpu.make_async_copy(x_ref, rows(*me), local_sem)
        mine.start()
        first = [copy(0, me, sibling, src=x_ref)]
        first += [copy(1 + j, me, (*chip, c), src=x_ref) for j, chip in enumerate(chips)]
        for cp in first:
            cp.start()
        passed = [copy(4 + j, (*chip, c), sibling) for j, chip in enumerate(chips)]
        for j, chip in enumerate(chips):
            copy(1 + j, (*chip, c), me).wait_recv()
            passed[j].start()
        copy(0, sibling, me).wait_recv()
        for j, chip in enumerate(chips):
            copy(4 + j, (*chip, 1 - c), me).wait_recv()
        for cp in first + passed:
            cp.wait_send()
        mine.wait()

    vm = pl.BlockSpec(memory_space=pltpu.VMEM)
    return pl.pallas_call(body, name="allgather_small", in_specs=[vm], out_specs=vm, out_shape=_sds((8 * m_per, n)),
                          scratch_shapes=[pltpu.SemaphoreType.DMA((7,)), pltpu.SemaphoreType.DMA((7,)), pltpu.SemaphoreType.DMA(())],
                          compiler_params=pltpu.CompilerParams(has_side_effects=True, vmem_limit_bytes=VMEM_LIMIT))(v)


def _sum8(v, m_per):
    def body(v_ref, o_ref):
        acc = v_ref[0:m_per, :]
        for k in range(1, 8):
            acc = acc + v_ref[k * m_per:(k + 1) * m_per, :]
        o_ref[...] = acc

    return pl.pallas_call(body, name="small_sum_devices", out_shape=_sds((m_per, v.shape[1])),
                          compiler_params=pltpu.CompilerParams(vmem_limit_bytes=VMEM_LIMIT))(v)


SMALL_SHARDED = (("e_conv_a_w", 2), ("e_conv_b_w", 2), ("o_conv_w", 2), ("f_conv_w", 2), ("ln_g", 2), ("ln_b", 2))
SMALL_REPL = ("e_conv_a_b", "e_ln_a_g", "e_ln_a_b", "e_conv_b_b", "e_dt_bias", "e_a_log", "e_d_skip", "e_norm_b_g", "f_conv_b")

WEIGHT_ORDER = ('e_w_in', 'e_conv_a_w', 'e_conv_a_b', 'e_ln_a_g', 'e_ln_a_b', 'e_conv_b_w', 'e_conv_b_b', 'e_dt_bias', 'e_a_log',
                'e_d_skip', 'e_norm_b_g', 'e_w_out', 'o_w_in', 'o_conv_w', 'o_w_out', 'f_w_up', 'f_conv_w', 'f_conv_b', 'f_w_down',
                'ple_w_proj', 'ple_w_gate', 'ln_g', 'ln_b')


def _pack_rows(parts, width, total_rows, dtype):
    flat = jnp.concatenate([p.reshape(-1).astype(dtype) for p in parts])
    flat = jnp.pad(flat, (0, total_rows * width - flat.shape[0]))
    return flat.reshape(total_rows, width)


def _unpack_rows(buf, shapes):
    flat = buf.reshape(-1)
    out, pos = [], 0
    for s in shapes:
        n = math.prod(s)
        out.append(flat[pos:pos + n].reshape(s))
        pos += n
    return out


def _small_rows(shapes):
    n = sum(math.prod(s) for s in shapes)
    return -(-n // (LANES * SUBLANES)) * SUBLANES


E_PAD = 5248
SEG_A, SEG_Z, SEG_X, SEG_DT = (0, 2 * D), (2 * D, D), (3 * D, 2 * D), (5 * D, LANES)
G_SHAPES = {"e_w_in": (2, D, E_PAD), "e_w_out": (2, 2 * D, D), "o_w_in": (2, D, 3 * D), "o_w_out": (2, D, D),
            "f_w_up": (4, D, 2 * D_FF), "f_w_down": (4, D_FF, D), "ple_w_proj": (4, PLE, D), "ple_w_gate": (4, D, D)}


def _padcols(w, width):
    return jnp.pad(w, ((0, 0), (0, width - w.shape[1])))


def _fold_rows(dw, K):
    return dw.reshape(K, SUBLANES, dw.shape[-1]).sum(1)


class GradBuffers(dict):
    def __init__(self):
        super().__init__()
        self.where = {}
        for gi, layers in enumerate(G_GROUPS):
            for name, kind, Lg, A, B, n, l0 in _group_items(layers):
                for k in range(Lg):
                    self.where[(name, l0 + k)] = (gi, k, Lg)
        self.current = {}

    def into(self, name, layer, r0=0, c0=0):
        gi, k, Lg = self.where[(name, layer)]
        self.current[name] = (name, gi)
        return (self.get((name, gi)), (Lg,) + G_SHAPES[name][1:], (k,), r0, c0)

    def __setitem__(self, name, value):
        super().__setitem__(self.current[name], value)
        self.last = value


def _local_step(x, p, target, W, comm=None, xb=None):
    T = x.shape[0]
    if xb is None:
        xb = _to_bf16(x, "x_bf16")
    saved = []
    xc_f = x
    for i in range(DEPTH):
        j = i // 2
        L = {}
        L["x"], L["xb"] = xc_f, xb
        tok = comm.part_starts(i, MIX, xb) if comm is not None else None
        if i % 2 == 0:
            def w_in(seg, c0=0, cols=None, j=j):
                return V(W["e_w_in"], (j,), c0=seg[0] + c0, cols=seg[1] if cols is None else cols)

            ua = _mm(xb, w_in(SEG_A), "nn", f"l{i}_in_a", BF16, after=tok)
            z = _mm(xb, w_in(SEG_Z), "nn", f"l{i}_in_z")
            xu = _mm(xb, w_in(SEG_X), "nn", f"l{i}_in_xbc", BF16)
            udt = _mm(xb, w_in(SEG_DT), "nn", f"l{i}_in_dt")
            ac = _conv_a_fwd(ua, W["e_conv_a_w"][j], W["e_conv_a_b"][j][None], f"l{i}_conv_a")
            ya = _ln_silu_fwd(ac, W["e_ln_a_g"][j][None], W["e_ln_a_b"][j][None], f"l{i}_ln_a")
            xc = _conv_b_fwd(xu, W["e_conv_b_w"][j], W["e_conv_b_b"][j][None], f"l{i}_conv_b")
            sm = _ssd_small_inputs(udt[:, :N_HEADS], W["e_dt_bias"][j], W["e_a_log"][j])
            y, states = _ssd_fwd(xc, *sm, W["e_d_skip"][j], f"l{i}_ssd")
            yb = _gate_rms_fwd(y, z, W["e_norm_b_g"][j][None], f"l{i}_gate_rms")
            out_pairs = [(ya, V(W["e_w_out"], (j,), rows=D)), (yb, V(W["e_w_out"], (j,), r0=D))]
            L.update(ua=ua, z=z, xu=xu, udt=udt, ac=ac, ya=ya, xc=xc, sm=sm, y=y, states=states, yb=yb, w_in=w_in)
        else:
            uo = _mm(xb, V(W["o_w_in"], (j,)), "nn", f"l{i}_in", BF16, after=tok)
            sc = _conv_c_fwd(uo, W["o_conv_w"][j], f"l{i}_conv_c")
            out_pairs = [(sc, V(W["o_w_out"], (j,)))]
            L.update(uo=uo, sc=sc)
        h1, x1, x1b = _mm_sum(out_pairs, "nn", f"l{i}_out", ln_fwd=(xc_f, None, W["ln_g"][i, 0][None], W["ln_b"][i, 0][None]))
        tok = comm.part_starts(i, FFN, x1b) if comm is not None else None
        up = _mm(x1b, V(W["f_w_up"], (i,)), "nn", f"l{i}_ffn_up", BF16, after=tok)
        act = _conv_f_fwd(up, W["f_conv_w"][i], W["f_conv_b"][i][None], f"l{i}_conv_f")
        pv = V(p, (i, 0))
        pp = _mm(pv, V(W["ple_w_proj"], (i,)), "nn", f"l{i}_ple_proj")
        gl = _mm(x1b, V(W["ple_w_gate"], (i,)), "nn", f"l{i}_ple_gate")
        h2, x2, x2b = _mm_sum([(act, V(W["f_w_down"], (i,)))], "nn", f"l{i}_ffn_down",
                              ln_fwd=(x1, (pp, gl), W["ln_g"][i, 1][None], W["ln_b"][i, 1][None]))
        L.update(h1=h1, x1=x1, x1b=x1b, up=up, act=act, pv=pv, pp=pp, gl=gl, h2=h2)
        saved.append(L)
        xc_f, xb = x2, x2b

    sq, dx = _loss_head(xc_f, target, "loss_head")

    GB = GradBuffers()
    into = GB.into
    tok = None
    ln2_done = None

    G = {n: [None] * (DEPTH if n.startswith(("f_", "ln_")) else DEPTH // 2) for n in WEIGHT_ORDER if n not in G_SHAPES}
    for i in reversed(range(DEPTH)):
        j = i // 2
        L = saved[i]
        if ln2_done is None:
            ln2_done = _res_ln_bwd(dx, L["h2"], W["ln_g"][i, 1][None], (L["pp"], L["gl"]), f"l{i}_ln2_bwd")
        dh2, dh2b, dg2, db2, dpp, dgl = ln2_done
        ln2_done = None
        GB["f_w_down"] = _mm(L["act"], dh2b, "tn", f"l{i}_dw_down", dst=into("f_w_down", i))
        dact = _mm(dh2b, V(W["f_w_down"], (i,)), "nt", f"l{i}_dact", BF16, after=tok)
        du1, du2, dw1, dw2, dbf1, dbf2 = _conv_f_bwd(L["up"], W["f_conv_w"][i], W["f_conv_b"][i][None], dact, f"l{i}_conv_f_bwd")
        G["f_conv_w"][i] = jnp.concatenate([_fold_rows(dw1, CONV_F), _fold_rows(dw2, CONV_F)], axis=1)
        G["f_conv_b"][i] = jnp.concatenate([dbf1.sum(0), dbf2.sum(0)])
        GB["f_w_up"] = _mm(L["x1b"], du1, "tn", f"l{i}_dw_up1", dst=into("f_w_up", i))
        GB["f_w_up"] = _mm(L["x1b"], du2, "tn", f"l{i}_dw_up2", dst=into("f_w_up", i, c0=D_FF))
        GB["ple_w_proj"] = _mm(L["pv"], dpp, "tn", f"l{i}_dw_proj", dst=into("ple_w_proj", i))
        GB["ple_w_gate"] = _mm(L["x1b"], dgl, "tn", f"l{i}_dw_gate", dst=into("ple_w_gate", i))
        tok = comm.part_grads_done(i, FFN, GB) if comm is not None else None
        dh1, dh1b, dg1, db1 = _mm_sum(
            [(du1, V(W["f_w_up"], (i,), cols=D_FF)), (du2, V(W["f_w_up"], (i,), c0=D_FF)), (dgl, V(W["ple_w_gate"], (i,)))],
            "nt", f"l{i}_dx1", add=dh2, add_scale=ALPHA, after=tok, ln_bwd=(L["h1"], W["ln_g"][i, 0][None], None))
        G["ln_g"][i] = jnp.concatenate([dg1, dg2], axis=0)
        G["ln_b"][i] = jnp.concatenate([db1, db2], axis=0)
        if i % 2 == 0:
            GB["e_w_out"] = _mm(L["ya"], dh1b, "tn", f"l{i}_dw_out_a", dst=into("e_w_out", j))
            GB["e_w_out"] = _mm(L["yb"], dh1b, "tn", f"l{i}_dw_out_b", dst=into("e_w_out", j, r0=D))
            dya = _mm(dh1b, V(W["e_w_out"], (j,), rows=D), "nt", f"l{i}_dya")
            dyb = _mm(dh1b, V(W["e_w_out"], (j,), r0=D), "nt", f"l{i}_dyb")
            dac, dga, dba = _ln_silu_bwd(L["ac"], dya, W["e_ln_a_g"][j][None], W["e_ln_a_b"][j][None], f"l{i}_ln_a_bwd")
            G["e_ln_a_g"][j], G["e_ln_a_b"][j] = dga[0], dba[0]
            dal, dag, dwa, dbca = _conv_a_bwd(L["ua"], W["e_conv_a_w"][j], dac, f"l{i}_conv_a_bwd")
            G["e_conv_a_w"][j] = _fold_rows(dwa, CONV_A)
            G["e_conv_a_b"][j] = dbca.sum(0)
            dy, dz, dgn = _gate_rms_bwd(L["y"], L["z"], dyb, W["e_norm_b_g"][j][None], f"l{i}_gate_rms_bwd")
            G["e_norm_b_g"][j] = dgn[0]
            dxs, dbs, dcs, sq_col, cms_row, dda_col, dda_row, ddp = _ssd_bwd(L["xc"], *L["sm"], W["e_d_skip"][j], L["states"], dy,
                                                                             f"l{i}_ssd_bwd")
            draw, G["e_dt_bias"][j], G["e_a_log"][j] = _ssd_small_grads(L["udt"][:, :N_HEADS], W["e_dt_bias"][j], W["e_a_log"][j],
                                                                       sq_col, cms_row, dda_col, dda_row)
            G["e_d_skip"][j] = ddp[:, :, 0, :].sum(0).reshape(N_HEADS, HEAD_P).sum(1)
            dxu, dwb, dbcb = _conv_b_bwd(L["xu"], W["e_conv_b_w"][j], W["e_conv_b_b"][j][None], dxs, dbs, dcs, f"l{i}_conv_b_bwd")
            G["e_conv_b_w"][j] = _fold_rows(dwb, CONV_B)
            G["e_conv_b_b"][j] = dbcb.sum(0)
            dudt = _padcols(draw, LANES)
            w_in = L["w_in"]
            xb_l = L["xb"]
            for nm, dseg, c0 in (("al", dal, 0), ("ag", dag, D), ("z", dz, SEG_Z[0]), ("xbc", dxu, SEG_X[0]), ("dt", dudt, SEG_DT[0])):
                GB["e_w_in"] = _mm(xb_l, dseg, "tn", f"l{i}_dw_in_{nm}", dst=into("e_w_in", j, c0=c0))
            dx = _mm_sum([(dal, w_in(SEG_A, cols=D)), (dag, w_in(SEG_A, c0=D, cols=D)), (dz, w_in(SEG_Z)),
                          (V(dxu, cols=D), w_in(SEG_X, cols=D)), (V(dxu, c0=D), w_in(SEG_X, c0=D, cols=D)), (dudt, w_in(SEG_DT))],
                         "nt", f"l{i}_dx", add=dh1, add_scale=ALPHA)
        else:
            GB["o_w_out"] = _mm(L["sc"], dh1b, "tn", f"l{i}_dw_out", dst=into("o_w_out", j))
            dsc = _mm(dh1b, V(W["o_w_out"], (j,)), "nt", f"l{i}_dsc")
            dbg, dcg, dv, dwc = _conv_c_bwd(L["uo"], W["o_conv_w"][j], dsc, f"l{i}_conv_c_bwd")
            G["o_conv_w"][j] = _fold_rows(dwc, CONV_C)
            xb_l = L["xb"]
            for nm, dseg, c0 in (("bg", dbg, 0), ("cg", dcg, D), ("v", dv, 2 * D)):
                GB["o_w_in"] = _mm(xb_l, dseg, "tn", f"l{i}_dw_in_{nm}", dst=into("o_w_in", j, c0=c0))
            below = saved[i - 1]
            ln2_done = _mm_sum([(dseg, V(W["o_w_in"], (j,), c0=c0, cols=D)) for dseg, c0 in ((dbg, 0), (dcg, D), (dv, 2 * D))],
                               "nt", f"l{i}_dx", add=dh1, add_scale=ALPHA,
                               ln_bwd=(below["h2"], W["ln_g"][i - 1, 1][None], (below["pp"], below["gl"])))
        tok = comm.part_grads_done(i, MIX, GB) if comm is not None else None
    grads = {n: jnp.stack(v) for n, v in G.items()}
    return sq, dx, GB, grads


def _ssd_small_inputs(raw, dt_bias, a_log):
    T = raw.shape[0]
    a = -jnp.exp(a_log)
    rg = raw.reshape(T, N_GROUPS, 4)
    raw_col = jnp.pad(jnp.transpose(rg, (1, 0, 2)), ((0, 0), (0, 0), (0, LANES - 4)))
    raw_row = jnp.pad(jnp.transpose(rg, (1, 2, 0)), ((0, 0), (0, SUBLANES - 4), (0, 0)))

    def colv(v):
        return jnp.pad(v.reshape(N_GROUPS, 1, 4), ((0, 0), (0, 0), (0, LANES - 4)))

    def rowv(v):
        return jnp.pad(v.reshape(N_GROUPS, 4, 1), ((0, 0), (0, SUBLANES - 4), (0, 0)))

    return raw_col, raw_row, colv(dt_bias), rowv(dt_bias), colv(a), rowv(a)


def _ssd_small_grads(raw, dt_bias, a_log, sq_col, cms_row, dda_col, dda_row):
    T = raw.shape[0]

    def join(col, row):
        c = jnp.transpose(col[:, :, :4], (1, 0, 2)).reshape(T, N_HEADS)
        r = jnp.transpose(row[:, :4, :], (2, 0, 1)).reshape(T, N_HEADS)
        return c + r

    a = -jnp.exp(a_log)
    pre = raw + dt_bias
    dt = jax.nn.softplus(pre)
    dda = join(dda_col, dda_row)
    ddt = join(sq_col, cms_row) + a * dda
    draw = ddt * jax.nn.sigmoid(pre)
    da = jnp.sum(dt * dda, axis=0)
    return draw, jnp.sum(draw, axis=0), da * a


def kernel(x, p, e_w_in, e_conv_a_w, e_conv_a_b, e_ln_a_g, e_ln_a_b, e_conv_b_w, e_conv_b_b, e_dt_bias, e_a_log, e_d_skip, e_norm_b_g, e_w_out, o_w_in, o_conv_w, o_w_out, f_w_up, f_conv_w, f_conv_b, f_w_down, ple_w_proj, ple_w_gate, ln_g, ln_b, loss_target, m_e_w_in, m_e_conv_a_w, m_e_conv_a_b, m_e_ln_a_g, m_e_ln_a_b, m_e_conv_b_w, m_e_conv_b_b, m_e_dt_bias, m_e_a_log, m_e_d_skip, m_e_norm_b_g, m_e_w_out, m_o_w_in, m_o_conv_w, m_o_w_out, m_f_w_up, m_f_conv_w, m_f_conv_b, m_f_w_down, m_ple_w_proj, m_ple_w_gate, m_ln_g, m_ln_b, v_e_w_in, v_e_conv_a_w, v_e_conv_a_b, v_e_ln_a_g, v_e_ln_a_b, v_e_conv_b_w, v_e_conv_b_b, v_e_dt_bias, v_e_a_log, v_e_d_skip, v_e_norm_b_g, v_e_w_out, v_o_w_in, v_o_conv_w, v_o_w_out, v_f_w_up, v_f_conv_w, v_f_conv_b, v_f_w_down, v_ple_w_proj, v_ple_w_gate, v_ln_g, v_ln_b):
    args = dict(locals())
    w_shard = {n: args[n] for n in WEIGHT_ORDER}
    m_shard = {n: args["m_" + n] for n in WEIGHT_ORDER}
    v_shard = {n: args["v_" + n] for n in WEIGHT_ORDER}
    xi, yi, ci = lax.axis_index("x"), lax.axis_index("y"), lax.axis_index("c")
    chip = 2 * xi + yi

    mevec = jnp.stack([chip, ci]).astype(jnp.int32)
    small_shapes = [w_shard[n].shape for n, _ in SMALL_SHARDED]
    sr = _small_rows(small_shapes)
    ws = _pack_rows([w_shard[n] for n, _ in SMALL_SHARDED], LANES, sr, F32)
    W = {n: w_shard[n] for n in SMALL_REPL}
    W.update({s[0]: Layers(s[2]) for s in TENSORS})
    w_items = [_group_items(layers) for layers in W_GROUPS]
    g_items = [_group_items(layers) for layers in G_GROUPS]

    def install(items, fulls):
        for it, f in zip(items, fulls):
            if it[0] == "e_w_in":
                f = jnp.transpose(f.reshape(it[2], 4, D, E_IN // 4), (0, 2, 1, 3)).reshape(it[2], D, E_IN)
                f = jnp.pad(f, ((0, 0), (0, 0), (0, E_PAD - E_IN)))
            W[it[0]].put(f, it[6])

    casts = [[_cast_into(w_shard[it[0]], it, mevec[:1]) for it in items] for items in w_items]
    fulls, ssem, rsem, _ = _gather_start(casts[0], w_items[0], "gather_start_0")
    xb0 = _to_bf16(x[0], "x_bf16")
    fulls = _gather_wait(fulls, ssem, rsem, [c for grp in casts[1:] for c in grp] + [xb0], w_items[0], "gather_wait_0")
    fulls, WS = _gather_fwd(fulls, w_items[0], "gather_fwd_0", ws)
    install(w_items[0], fulls)
    parts_s = [_unpack_rows(WS[k], small_shapes) for k in range(4)]
    for idx, (n, ax) in enumerate(SMALL_SHARDED):
        W[n] = jnp.concatenate([parts_s[k][idx] for k in range(4)], axis=ax)

    class Comm:
        sent = {}
        started = {}
        tail = fulls[0]

        def start_next(self, gi):
            if gi >= len(w_items):
                return None
            self.started[gi] = _gather_start(casts[gi], w_items[gi], f"gather_start_{gi}", self.tail)
            return self.started[gi][3]

        def part_starts(self, layer, part, after):
            if (layer, part) == W_GROUPS[0][0]:
                return self.start_next(1)
            for gi in range(1, len(W_GROUPS)):
                if W_GROUPS[gi][0] == (layer, part):
                    fulls, ssem, rsem, _ = self.started[gi]
                    fulls = _gather_wait(fulls, ssem, rsem, after, w_items[gi], f"gather_wait_{gi}")
                    fulls, _ = _gather_fwd(fulls, w_items[gi], f"gather_fwd_{gi}")
                    install(w_items[gi], fulls)
                    self.tail = fulls[0]
                    return self.start_next(gi + 1)
            return None

        swapping = None

        def swap_landed(self, after):
            if self.swapping is None:
                return None
            gi, thru, ssem, rsem = self.swapping
            items = g_items[gi]
            thru = _swap_wait(thru, ssem, rsem, after, items, f"swap_wait_{gi}")
            gs, ras = thru[:len(items)], thru[len(items):]
            ps = [_add_half(g, ra, it, mevec[1:]) for g, ra, it in zip(gs, ras, items)]
            thru, ssem, rsem, tok = _scatter_start(ps, items, f"scatter_start_{gi}")
            self.sent[gi] = (thru, ssem, rsem, tok)
            self.swapping = None
            return tok

        def part_grads_done(self, layer, part, GB):
            tok = self.swap_landed(GB.last)
            for gi, members in enumerate(G_GROUPS):
                if members[-1] == (layer, part):
                    items = g_items[gi]
                    gs = []
                    for it in items:
                        g = GB[(it[0], gi)]
                        if it[0] == "e_w_in":
                            g = jnp.transpose(g[:, :, :E_IN].reshape(it[2], D, 4, E_IN // 4), (0, 2, 1, 3)).reshape(it[2], 4 * D, E_IN // 4)
                        gs.append(g)
                    thru, ssem, rsem, tok = _swap_start(gs, items, f"swap_start_{gi}", tok)
                    self.swapping = (gi, thru, ssem, rsem)
            return tok

    comm = Comm()

    sq, dx, GB, G = _local_step(x[0], p, loss_target[0], W, comm, xb0)
    loss = lax.psum(0.5 * sq[0, 0] / D, ("x", "y", "c"))
    grad_x = dx[None]

    def shard_of(g, ax, k):
        n = g.shape[ax] // 4
        return lax.slice_in_dim(g, k * n, (k + 1) * n, axis=ax)

    reduced, updated = {}, {}
    comm.swap_landed(dx)
    after = comm.sent[len(g_items) - 1][3]
    for gi, items in enumerate(g_items):
        thru, ssem, rsem, _ = comm.sent[gi]
        thru = _scatter_wait(thru, ssem, rsem, after, items, f"scatter_wait_{gi}")
        ps, rcs = thru[:len(items)], thru[len(items):]
        rs = [_sum_own(pt, rc, it, mevec, reduced.get(it[0])) for pt, rc, it in zip(ps, rcs, items)]
        rs = _join_halves(rs, items, f"join_halves_{gi}")
        reduced.update({it[0]: r for it, r in zip(items, rs)})
        for it in items:
            n = it[0]
            if n != "e_w_in":
                updated[n] = _adamw_layers(w_shard[n], reduced[n], m_shard[n], v_shard[n], it[6], it[2], updated.get(n),
                                           f"adamw_{n}_{it[6]}")
        after = updated[items[-1][0]][0]
    *updated["e_w_in"], reduced["e_w_in"] = _adamw_minor_rows(w_shard["e_w_in"], reduced["e_w_in"], m_shard["e_w_in"],
                                                              v_shard["e_w_in"], "adamw_e_w_in")

    small_all = ([shard_of(G[n], ax, k) for k in range(4) for n, ax in SMALL_SHARDED] + [G[n] for n in SMALL_REPL])
    small_all_shapes = [t.shape for t in small_all]
    mr = _small_rows(small_all_shapes)
    sg = _sum8(_allgather_small(_pack_rows(small_all, LANES, mr, F32)), mr)
    sparts = _unpack_rows(sg, small_all_shapes)
    ns = len(SMALL_SHARDED)
    gsmall = {}
    for idx, (n, ax) in enumerate(SMALL_SHARDED):
        stacked = jnp.stack([sparts[k * ns + idx] for k in range(4)])
        gsmall[n] = lax.dynamic_index_in_dim(stacked, chip, axis=0, keepdims=False)
    for idx, n in enumerate(SMALL_REPL):
        gsmall[n] = sparts[4 * ns + idx]

    grads, deltas, new_m, new_v = [], [], [], []
    for n in WEIGHT_ORDER:
        if n in reduced:
            g, (d, mn, vn) = reduced[n], updated[n]
        else:
            g = gsmall[n]
            d, mn, vn = _adamw(w_shard[n], g, m_shard[n], v_shard[n], f"adamw_{n}")
        grads.append(g)
        deltas.append(d)
        new_m.append(mn)
        new_v.append(vn)
    return (loss, grad_x, *grads, *deltas, *new_m, *new_v)
```
